```python
import math
import jax, jax.numpy as jnp
from jax import lax
import numpy as np

D_MODEL = 1024
BATCH = 2
SEQ = 8192
DEPTH = 2

CHUNK = 64
N_MIXERS = 2
N_LAYERS_A = (DEPTH + 1) // 2
N_LAYERS_B = DEPTH // 2

A_HEADS = 8
A_HEAD_DIM = 64
A_V_DIM = 2 * A_HEAD_DIM
Q_BLOCK = 128
T5_BUCKETS = 32
T5_MAX_DIST = 1024

B_HEADS = 16
B_HEAD_DIM = D_MODEL // B_HEADS
LEFT_CHUNKS = 8
BAND = (LEFT_CHUNKS + 1) * CHUNK
MAX_REL = 256

N_EXPERTS = 16
N_GROUPS = 4
E_PER_GROUP = N_EXPERTS // N_GROUPS
TOPK_GROUPS = 1
TOP_K = 2
D_FF_EXPERT = 512

NORM_EPS = 1e-6
NEG_INF = -1e30

kernel_name = "hybrid_diffattn_chunkattn_grouped_moe"


def rms_norm(x, g):
    xf = x.astype(jnp.float32)
    y = xf * lax.rsqrt(jnp.mean(xf * xf, axis=-1, keepdims=True) + NORM_EPS)
    return (y * g.astype(jnp.float32)).astype(x.dtype)


def t5_bucket(rel):
    nb = T5_BUCKETS // 2
    ret = jnp.where(rel > 0, nb, 0)
    n = jnp.abs(rel)
    max_exact = nb // 2
    nf = jnp.maximum(n, 1).astype(jnp.float32)
    large = max_exact + (jnp.log(nf / max_exact) / math.log(T5_MAX_DIST / max_exact)
                         * (nb - max_exact)).astype(jnp.int32)
    large = jnp.minimum(large, nb - 1)
    return ret + jnp.where(n < max_exact, n, large)


def diff_attention(h, w_qkv, q_gain, k_gain, lam, subln_g, w_o, t5_bias, lambda_init):
    B, S, D = h.shape
    qkv = h @ w_qkv
    q, k, v = jnp.split(qkv, 3, axis=-1)
    q = q.reshape(B, S, A_HEADS, 2, A_HEAD_DIM)
    k = k.reshape(B, S, A_HEADS, 2, A_HEAD_DIM)
    v = v.reshape(B, S, A_HEADS, A_V_DIM)
    q = rms_norm(q, q_gain) * (A_HEAD_DIM ** -0.5)
    k = rms_norm(k, k_gain)
    lamf = lam.astype(jnp.float32)
    lam_full = (jnp.exp(jnp.sum(lamf[0] * lamf[1])) - jnp.exp(jnp.sum(lamf[2] * lamf[3]))
                + lambda_init)
    n_blocks = S // Q_BLOCK
    q_blocks = q.reshape(B, n_blocks, Q_BLOCK, A_HEADS, 2, A_HEAD_DIM).transpose(1, 0, 2, 3, 4, 5)
    k_pos = jnp.arange(S, dtype=jnp.int32)

    def one_block(args):
        qb, blk = args
        q_pos = blk * Q_BLOCK + jnp.arange(Q_BLOCK, dtype=jnp.int32)
        rel = k_pos[None, :] - q_pos[:, None]
        bias = t5_bias.astype(jnp.float32)[t5_bucket(rel)].transpose(2, 0, 1)
        allowed = (k_pos[None, :] // CHUNK) <= (q_pos[:, None] // CHUNK)
        s = jnp.einsum('bqhmd,bkhmd->bmhqk', qb, k).astype(jnp.float32) + bias[None, None]
        s = jnp.where(allowed[None, None, None], s, NEG_INF)
        p = jax.nn.softmax(s, axis=-1)
        a = p[:, 0] - lam_full * p[:, 1]
        return jnp.einsum('bhqk,bkhe->bqhe', a.astype(v.dtype), v)

    o = lax.map(one_block, (q_blocks, jnp.arange(n_blocks, dtype=jnp.int32)))
    o = o.transpose(1, 0, 2, 3, 4).reshape(B, S, A_HEADS, A_V_DIM)
    o = rms_norm(o, subln_g) * (1.0 - lambda_init)
    return o.reshape(B, S, D) @ w_o


def chunked_attention(h, w_qkv, q_gain, k_gain, rel_bias, w_o):
    B, S, D = h.shape
    NC = S // CHUNK
    qkv = h @ w_qkv
    q, k, v = jnp.split(qkv, 3, axis=-1)
    q = rms_norm(q.reshape(B, S, B_HEADS, B_HEAD_DIM), q_gain) * (B_HEAD_DIM ** -0.5)
    k = rms_norm(k.reshape(B, S, B_HEADS, B_HEAD_DIM), k_gain)
    v = v.reshape(B, S, B_HEADS, B_HEAD_DIM)
    pad = ((0, 0), (LEFT_CHUNKS * CHUNK, 0), (0, 0), (0, 0))
    kp = jnp.pad(k, pad)
    vp = jnp.pad(v, pad)
    qi = jnp.arange(CHUNK, dtype=jnp.int32)
    kb = jnp.arange(BAND, dtype=jnp.int32)
    rel = kb[None, :] - LEFT_CHUNKS * CHUNK - qi[:, None]
    idx = jnp.clip(rel, -MAX_REL, MAX_REL) + MAX_REL
    bias = rel_bias.astype(jnp.float32)[:, idx]
    band_chunk = kb // CHUNK
    q_chunks = q.reshape(B, NC, CHUNK, B_HEADS, B_HEAD_DIM).transpose(1, 0, 2, 3, 4)

    def one_chunk(args):
        qc, ci = args
        kband = lax.dynamic_slice_in_dim(kp, ci * CHUNK, BAND, axis=1)
        vband = lax.dynamic_slice_in_dim(vp, ci * CHUNK, BAND, axis=1)
        valid = (ci - LEFT_CHUNKS + band_chunk) >= 0
        s = jnp.einsum('bqhd,bkhd->bhqk', qc, kband).astype(jnp.float32) + bias[None]
        s = jnp.where(valid[None, None, None, :], s, NEG_INF)
        p = jax.nn.softmax(s, axis=-1)
        return jnp.einsum('bhqk,bkhd->bqhd', p.astype(vband.dtype), vband)

    o = lax.map(one_chunk, (q_chunks, jnp.arange(NC, dtype=jnp.int32)))
    o = o.transpose(1, 0, 2, 3, 4).reshape(B, S, D)
    return o @ w_o


def grouped_moe(h, router_w, router_bias, w_gate, w_up, w_down):
    B, S, D = h.shape
    t = h.reshape(B * S, D)
    scores = jax.nn.sigmoid((t @ router_w).astype(jnp.float32))
    sel = scores + router_bias.astype(jnp.float32)
    grp = sel.reshape(-1, N_GROUPS, E_PER_GROUP)
    group_score = lax.top_k(grp, TOP_K)[0].sum(-1)
    _, g_idx = lax.top_k(group_score, TOPK_GROUPS)
    group_mask = jax.nn.one_hot(g_idx, N_GROUPS, dtype=jnp.float32).sum(1) > 0
    expert_mask = jnp.repeat(group_mask, E_PER_GROUP, axis=1)
    masked = jnp.where(expert_mask, sel, NEG_INF)
    _, e_idx = lax.top_k(masked, TOP_K)
    w = jnp.take_along_axis(scores, e_idx, axis=1)
    w = w / jnp.sum(w, axis=-1, keepdims=True)
    gates = jnp.sum(jax.nn.one_hot(e_idx, N_EXPERTS, dtype=jnp.float32) * w[..., None], axis=1)
    gates = gates.astype(t.dtype)
    out = jnp.zeros_like(t)
    for e in range(N_EXPERTS):
        he = jax.nn.silu(t @ w_gate[e]) * (t @ w_up[e])
        out = out + gates[:, e:e + 1] * (he @ w_down[e])
    return out.reshape(B, S, D)


def setup_inputs(seed: int = 0) -> dict:
    key = jax.random.key(seed)
    ks = jax.random.split(key, 24)
    D, F, E = D_MODEL, D_FF_EXPERT, N_EXPERTS
    nrm = lambda k, shape, s: jax.random.normal(k, shape, jnp.float32) * s
    gain = lambda k, shape: 1.0 + 0.02 * jax.random.normal(k, shape, jnp.float32)
    return {
        "x": nrm(ks[0], (BATCH, SEQ, D), 1.0),
        "c": nrm(ks[1], (BATCH, D), 1.0),
        "ada_w": nrm(ks[2], (DEPTH, D, 6 * D), 0.5 * D ** -0.5),
        "ada_b": nrm(ks[3], (DEPTH, 6 * D), 0.01),
        "norm_mix_g": gain(ks[4], (DEPTH, D)),
        "norm_ffn_g": gain(ks[5], (DEPTH, D)),
        "t5_bias": nrm(ks[6], (T5_BUCKETS, A_HEADS), 0.2),
        "a_w_qkv": nrm(ks[7], (N_LAYERS_A, D, 3 * D), D ** -0.5),
        "a_q_gain": gain(ks[8], (N_LAYERS_A, A_HEAD_DIM)),
        "a_k_gain": gain(ks[9], (N_LAYERS_A, A_HEAD_DIM)),
        "a_lambda": nrm(ks[10], (N_LAYERS_A, 4, A_HEAD_DIM), 0.1),
        "a_subln_g": gain(ks[11], (N_LAYERS_A, A_V_DIM)),
        "a_w_o": nrm(ks[12], (N_LAYERS_A, D, D), D ** -0.5),
        "b_w_qkv": nrm(ks[13], (N_LAYERS_B, D, 3 * D), D ** -0.5),
        "b_q_gain": gain(ks[14], (N_LAYERS_B, B_HEAD_DIM)),
        "b_k_gain": gain(ks[15], (N_LAYERS_B, B_HEAD_DIM)),
        "b_rel_bias": nrm(ks[16], (N_LAYERS_B, B_HEADS, 2 * MAX_REL + 1), 0.2),
        "b_w_o": nrm(ks[17], (N_LAYERS_B, D, D), D ** -0.5),
        "router_w": nrm(ks[18], (D, E), D ** -0.5),
        "router_bias": nrm(ks[19], (E,), 0.01),
        "moe_w_gate": nrm(ks[20], (DEPTH, E, D, F), D ** -0.5),
        "moe_w_up": nrm(ks[21], (DEPTH, E, D, F), D ** -0.5),
        "moe_w_down": nrm(ks[22], (DEPTH, E, F, D), F ** -0.5),
    }


def reference(x, c, ada_w, ada_b, norm_mix_g, norm_ffn_g, t5_bias,
              a_w_qkv, a_q_gain, a_k_gain, a_lambda, a_subln_g, a_w_o,
              b_w_qkv, b_q_gain, b_k_gain, b_rel_bias, b_w_o,
              router_w, router_bias, moe_w_gate, moe_w_up, moe_w_down):
    silu_c = jax.nn.silu(c)
    for i in range(DEPTH):
        mod = silu_c @ ada_w[i] + ada_b[i]
        sh1, sc1, g1, sh2, sc2, g2 = [m[:, None, :] for m in jnp.split(mod, 6, axis=-1)]
        h = rms_norm(x, norm_mix_g[i]) * (1.0 + sc1) + sh1
        if i % N_MIXERS == 0:
            j = i // N_MIXERS
            lambda_init = 0.8 - 0.6 * math.exp(-0.3 * i)
            y = diff_attention(h, a_w_qkv[j], a_q_gain[j], a_k_gain[j], a_lambda[j],
                               a_subln_g[j], a_w_o[j], t5_bias, lambda_init)
        else:
            j = i // N_MIXERS
            y = chunked_attention(h, b_w_qkv[j], b_q_gain[j], b_k_gain[j], b_rel_bias[j], b_w_o[j])
        x = x + g1 * y
        h = rms_norm(x, norm_ffn_g[i]) * (1.0 + sc2) + sh2
        x = x + g2 * grouped_moe(h, router_w, router_bias, moe_w_gate[i], moe_w_up[i], moe_w_down[i])
    return x
```

```python
import functools
import math

import numpy as np
import jax
import jax.numpy as jnp
from jax import lax
from jax.experimental import pallas as pl
from jax.experimental.pallas import tpu as pltpu

F32 = jnp.float32
BF16 = jnp.bfloat16

CHUNK = 64
A_HEADS = 8
A_HEAD_DIM = 64
T5_BUCKETS = 32
T5_MAX_DIST = 1024
B_HEADS = 16
B_HEAD_DIM = 64
LEFT_CHUNKS = 8
MAX_REL = 256
N_EXPERTS = 16
N_GROUPS = 4
E_PER_GROUP = N_EXPERTS // N_GROUPS
NORM_EPS = 1e-6
NEG_INF = -1e30

V7X_LANES = 128
V7X_MXU_DIM = 256
V7X_VMEM_BYTES = 64 * 1024 * 1024

ATTN_TILE = LEFT_CHUNKS * CHUNK
ROW_TILE = 512
MOE_ROW_TILE = 1024
VMEM_LIMIT = 56 * 1024 * 1024


def _cparams(sem):
    return pltpu.CompilerParams(dimension_semantics=sem, vmem_limit_bytes=VMEM_LIMIT)


def _mod_kernel(c_ref, w_ref, b_ref, o_ref):
    c = c_ref[...]
    s = c * (1.0 / (1.0 + jnp.exp(-c)))
    o_ref[...] = jnp.dot(s, w_ref[...], preferred_element_type=F32,
                         precision=lax.Precision.HIGHEST) + b_ref[...]


def _modulation(c_pad, ada_w, ada_b):
    depth, d, n = ada_w.shape
    rows = c_pad.shape[0]
    tn = 1536
    return pl.pallas_call(
        _mod_kernel,
        out_shape=jax.ShapeDtypeStruct((depth, rows, n), F32),
        grid=(depth, n // tn),
        in_specs=[
            pl.BlockSpec((rows, d), lambda i, j: (0, 0)),
            pl.BlockSpec((None, d, tn), lambda i, j: (i, 0, j)),
            pl.BlockSpec((None, 1, tn), lambda i, j: (i, 0, j)),
        ],
        out_specs=pl.BlockSpec((None, rows, tn), lambda i, j: (i, 0, j)),
        compiler_params=_cparams(("parallel", "parallel")),
        name="adaln_mod",
    )(c_pad, ada_w, ada_b.reshape(depth, 1, n))


def _norm_mod(x, g, sc, sh):
    ms = jnp.mean(x * x, axis=-1, keepdims=True)
    return x * lax.rsqrt(ms + NORM_EPS) * g * (1.0 + sc) + sh


def _qkv_kernel(x_ref, g_ref, sc_ref, sh_ref, w_ref, gain_ref, o_ref, *, d_model, head_dim):
    h = _norm_mod(x_ref[...], g_ref[...], sc_ref[...], sh_ref[...]).astype(BF16)
    cw = V7X_MXU_DIM
    r = lax.broadcasted_iota(jnp.int32, (cw, cw), 0) // head_dim
    c = lax.broadcasted_iota(jnp.int32, (cw, cw), 1) // head_dim
    gmat = jnp.where(r == c, 1.0 / head_dim, 0.0).astype(BF16)
    for ci in range(2 * d_model // cw):
        cols = slice(ci * cw, (ci + 1) * cw)
        y = jnp.dot(h, w_ref[:, cols], preferred_element_type=F32)
        ss = jnp.dot((y * y).astype(BF16), gmat, preferred_element_type=F32)
        o_ref[:, cols] = (y * lax.rsqrt(ss + NORM_EPS) * gain_ref[:, cols]).astype(BF16)
    o_ref[:, 2 * d_model:] = jnp.dot(h, w_ref[:, 2 * d_model:],
                                     preferred_element_type=F32).astype(BF16)


def _qkv_proj(x2, g, sc, sh, w_bf16, qk_gain, seq, head_dim):
    t, d = x2.shape
    tm = ROW_TILE
    per_b = seq // tm
    vec = lambda i: (i // per_b, 0, 0)
    return pl.pallas_call(
        functools.partial(_qkv_kernel, d_model=d, head_dim=head_dim),
        out_shape=jax.ShapeDtypeStruct((t, 3 * d), BF16),
        grid=(t // tm,),
        in_specs=[
            pl.BlockSpec((tm, d), lambda i: (i, 0)),
            pl.BlockSpec((1, d), lambda i: (0, 0)),
            pl.BlockSpec((None, 1, d), vec),
            pl.BlockSpec((None, 1, d), vec),
            pl.BlockSpec((d, 3 * d), lambda i: (0, 0)),
            pl.BlockSpec((1, 2 * d), lambda i: (0, 0)),
        ],
        out_specs=pl.BlockSpec((tm, 3 * d), lambda i: (i, 0)),
        compiler_params=_cparams(("parallel",)),
        name="qkv_proj",
    )(x2, g, sc, sh, w_bf16, qk_gain)


def _flash_update(q, k, v, bias, m_ref, l_ref, acc_ref):
    s = lax.dot_general(q, k, (((1,), (1,)), ((), ())), preferred_element_type=F32)
    if bias is not None:
        s = s + bias
    m_prev = m_ref[...]
    m_new = jnp.maximum(m_prev, jnp.max(s, axis=-1, keepdims=True))
    alpha = jnp.exp(m_prev - m_new)
    p = jnp.exp(s - m_new)
    l_ref[...] = alpha * l_ref[...] + jnp.sum(p, axis=-1, keepdims=True)
    acc_ref[...] = alpha * acc_ref[...] + jnp.dot(p.astype(BF16), v, preferred_element_type=F32)
    m_ref[...] = m_new


def _split_maps(q, head_dim):
    lane = lax.broadcasted_iota(jnp.int32, q.shape, 1)
    zero = jnp.zeros_like(q)
    return jnp.where(lane < head_dim, q, zero), jnp.where(lane >= head_dim, q, zero)


def _init_state(refs):
    for m_ref, l_ref, acc_ref in refs:
        m_ref[...] = jnp.full(m_ref.shape, NEG_INF, F32)
        l_ref[...] = jnp.zeros(l_ref.shape, F32)
        acc_ref[...] = jnp.zeros(acc_ref.shape, F32)


def _attn_a_kernel(q_ref, k_ref, v_ref, bias_ref, lam_ref, sub_ref, o_ref,
                   ma, la, acca, mb, lb, accb, *, n_off, lambda_init):
    tq = q_ref.shape[0]
    qi = pl.program_id(2)
    qa, qb = _split_maps(q_ref[...], A_HEAD_DIM)
    _init_state(((ma, la, acca), (mb, lb, accb)))

    def tile(j, bias):
        ks = pl.multiple_of(j * tq, tq)
        k = k_ref[pl.ds(ks, tq), :]
        v = v_ref[pl.ds(ks, tq), :]
        _flash_update(qa, k, v, bias, ma, la, acca)
        _flash_update(qb, k, v, bias, mb, lb, accb)

    def far_body(j, carry):
        tile(j, None)
        return carry
    lax.fori_loop(0, jnp.maximum(qi - (n_off - 1), 0), far_body, 0)

    for o in range(n_off - 1, -1, -1):
        @pl.when(qi >= o)
        def _():
            tile(qi - o, bias_ref[o])

    lam = lam_ref[...]
    lam_full = (jnp.exp(jnp.sum(lam[0:1] * lam[1:2], axis=-1, keepdims=True))
                - jnp.exp(jnp.sum(lam[2:3] * lam[3:4], axis=-1, keepdims=True)) + lambda_init)
    a = acca[...] / la[...] - lam_full * (accb[...] / lb[...])
    ms = jnp.mean(a * a, axis=-1, keepdims=True)
    o_ref[...] = (a * lax.rsqrt(ms + NORM_EPS) * sub_ref[...] * (1.0 - lambda_init)).astype(BF16)


def _attn_b_kernel(q_ref, k_ref, v_ref, bias_ref, o_ref, ma, la, acca, mb, lb, accb):
    tq = q_ref.shape[0]
    qi = pl.program_id(2)
    qa, qb = _split_maps(q_ref[...], B_HEAD_DIM)
    _init_state(((ma, la, acca), (mb, lb, accb)))

    def tile(j, o):
        ks = pl.multiple_of(j * tq, tq)
        k = k_ref[pl.ds(ks, tq), :]
        v = v_ref[pl.ds(ks, tq), :]
        _flash_update(qa, k, v, bias_ref[0, o], ma, la, acca)
        _flash_update(qb, k, v, bias_ref[1, o], mb, lb, accb)

    @pl.when(qi >= 1)
    def _():
        tile(qi - 1, 1)
    tile(qi, 0)

    lane = lax.broadcasted_iota(jnp.int32, o_ref.shape, 1)
    o_ref[...] = jnp.where(lane < B_HEAD_DIM, acca[...] / la[...], accb[...] / lb[...]).astype(BF16)


def _attn_scratch(tq):
    one = [pltpu.VMEM((tq, 1), F32), pltpu.VMEM((tq, 1), F32), pltpu.VMEM((tq, V7X_LANES), F32)]
    return one + one


def _attention_a(qkv, bias_tab, lam, sub_g, batch, seq, lambda_init):
    t, d3 = qkv.shape
    d = d3 // 3
    nblk = d // V7X_LANES
    tq = ATTN_TILE
    nq = seq // tq
    n_off = bias_tab.shape[1]
    return pl.pallas_call(
        functools.partial(_attn_a_kernel, n_off=n_off, lambda_init=lambda_init),
        out_shape=jax.ShapeDtypeStruct((t, d), BF16),
        grid=(batch, nblk, nq),
        in_specs=[
            pl.BlockSpec((tq, V7X_LANES), lambda b, h, i: (b * nq + i, h)),
            pl.BlockSpec((seq, V7X_LANES), lambda b, h, i: (b, nblk + h)),
            pl.BlockSpec((seq, V7X_LANES), lambda b, h, i: (b, 2 * nblk + h)),
            pl.BlockSpec((None, n_off, tq, tq), lambda b, h, i: (h, 0, 0, 0)),
            pl.BlockSpec(lam.shape, lambda b, h, i: (0, 0)),
            pl.BlockSpec(sub_g.shape, lambda b, h, i: (0, 0)),
        ],
        out_specs=pl.BlockSpec((tq, V7X_LANES), lambda b, h, i: (b * nq + i, h)),
        scratch_shapes=_attn_scratch(tq),
        compiler_params=_cparams(("parallel", "parallel", "arbitrary")),
        name="diff_attention",
    )(qkv, qkv, qkv, bias_tab, lam, sub_g)


def _attention_b(qkv, bias_tab, batch, seq):
    t, d3 = qkv.shape
    d = d3 // 3
    nblk = d // V7X_LANES
    tq = ATTN_TILE
    nq = seq // tq
    return pl.pallas_call(
        _attn_b_kernel,
        out_shape=jax.ShapeDtypeStruct((t, d), BF16),
        grid=(batch, nblk, nq),
        in_specs=[
            pl.BlockSpec((tq, V7X_LANES), lambda b, h, i: (b * nq + i, h)),
            pl.BlockSpec((seq, V7X_LANES), lambda b, h, i: (b, nblk + h)),
            pl.BlockSpec((seq, V7X_LANES), lambda b, h, i: (b, 2 * nblk + h)),
            pl.BlockSpec((None, 2, 2, tq, tq), lambda b, h, i: (h, 0, 0, 0, 0)),
        ],
        out_specs=pl.BlockSpec((tq, V7X_LANES), lambda b, h, i: (b * nq + i, h)),
        scratch_shapes=_attn_scratch(tq),
        compiler_params=_cparams(("parallel", "parallel", "arbitrary")),
        name="chunk_attention",
    )(qkv, qkv, qkv, bias_tab)


def _t5_bucket(rel):
    nb = T5_BUCKETS // 2
    ret = jnp.where(rel > 0, nb, 0)
    n = jnp.abs(rel)
    max_exact = nb // 2
    nf = jnp.maximum(n, 1).astype(F32)
    large = max_exact + (jnp.log(nf / max_exact) / math.log(T5_MAX_DIST / max_exact)
                         * (nb - max_exact)).astype(jnp.int32)
    large = jnp.minimum(large, nb - 1)
    return ret + jnp.where(n < max_exact, n, large)


def _t5_const_distance():
    nb = T5_BUCKETS // 2
    max_exact = nb // 2
    n = np.arange(max_exact, 4 * T5_MAX_DIST, dtype=np.float64)
    large = max_exact + np.floor(np.log(n / max_exact) / math.log(T5_MAX_DIST / max_exact) * (nb - max_exact))
    below = np.nonzero(large < nb - 1)[0]
    return int(n[below[-1]]) + 2


def _t5_tables(t5_bias, tile):
    n_off = 1
    while (n_off - 1) * tile + 1 < _t5_const_distance():
        n_off += 1
    a = jnp.arange(tile, dtype=jnp.int32)[:, None]
    b = jnp.arange(tile, dtype=jnp.int32)[None, :]
    tabs = []
    far = t5_bias.astype(F32)[T5_BUCKETS // 2 - 1]
    for o in range(n_off):
        rel = (b - a) - o * tile
        bias = t5_bias.astype(F32)[_t5_bucket(rel)].transpose(2, 0, 1) - far[:, None, None]
        if o == 0:
            bias = jnp.where((b // CHUNK) <= (a // CHUNK), bias, NEG_INF)
        tabs.append(bias)
    return jnp.stack(tabs, axis=1)


def _band_tables(rel_bias, tile):
    a = jnp.arange(tile, dtype=jnp.int32)[:, None]
    b = jnp.arange(tile, dtype=jnp.int32)[None, :]
    tabs = []
    for o in range(2):
        rel = (b - a) - o * tile
        idx = jnp.clip(rel, -MAX_REL, MAX_REL) + MAX_REL
        bias = rel_bias.astype(F32)[:, idx]
        dchunk = (a // CHUNK) - (b // CHUNK) + o * (tile // CHUNK)
        tabs.append(jnp.where((dchunk >= 0) & (dchunk <= LEFT_CHUNKS), bias, NEG_INF))
    tab = jnp.stack(tabs, axis=1)
    return tab.reshape(rel_bias.shape[0] // 2, 2, 2, tile, tile)


def _top2_sum4(r0, r1, r2, r3):
    a, b = jnp.maximum(r0, r1), jnp.minimum(r0, r1)
    c, d = jnp.maximum(r2, r3), jnp.minimum(r2, r3)
    return jnp.maximum(a, c) + jnp.maximum(jnp.minimum(a, c), jnp.maximum(b, d))


def _route(logits, rbias):
    n_e = logits.shape[0]
    scores = 1.0 / (1.0 + jnp.exp(-logits))
    sel = scores + rbias
    row = lax.broadcasted_iota(jnp.int32, sel.shape, 0)
    best = None
    for g in range(N_GROUPS):
        rows = [sel[g * E_PER_GROUP + i: g * E_PER_GROUP + i + 1, :] for i in range(E_PER_GROUP)]
        gs = _top2_sum4(*rows)
        if best is None:
            best, gidx = gs, jnp.zeros(gs.shape, jnp.int32)
        else:
            gidx = jnp.where(gs > best, g, gidx)
            best = jnp.maximum(best, gs)
    masked = jnp.where(row // E_PER_GROUP == gidx, sel, NEG_INF)
    m1 = jnp.max(masked, axis=0, keepdims=True)
    i1 = jnp.min(jnp.where(masked == m1, row, n_e), axis=0, keepdims=True)
    masked2 = jnp.where(row == i1, -3.0e38, masked)
    m2 = jnp.max(masked2, axis=0, keepdims=True)
    i2 = jnp.min(jnp.where(masked2 == m2, row, n_e), axis=0, keepdims=True)
    w1 = jnp.sum(jnp.where(row == i1, scores, 0.0), axis=0, keepdims=True)
    w2 = jnp.sum(jnp.where(row == i2, scores, 0.0), axis=0, keepdims=True)
    den = w1 + w2
    return jnp.where(row == i1, w1 / den, 0.0) + jnp.where(row == i2, w2 / den, 0.0)


def _oproj_kernel(o_ref, wo_ref, x_ref, g1_ref, gn_ref, sc_ref, sh_ref, rwt_ref, rb_ref,
                  xo_ref, h_ref, gates_ref):
    y = jnp.dot(o_ref[...], wo_ref[...], preferred_element_type=F32)
    xn = x_ref[...] + g1_ref[...] * y
    xo_ref[...] = xn
    h = _norm_mod(xn, gn_ref[...], sc_ref[...], sh_ref[...])
    h_ref[...] = h.astype(BF16)
    logits = lax.dot_general(rwt_ref[...], h, (((1,), (1,)), ((), ())),
                             preferred_element_type=F32, precision=lax.Precision.HIGHEST)
    gates_ref[...] = _route(logits, rb_ref[...])


def _out_proj(o, wo_bf16, x2, g1, gn, sc, sh, rwt, rb, seq):
    t, d = x2.shape
    tm = ROW_TILE
    per_b = seq // tm
    n_e = rwt.shape[0]
    vec = lambda i: (i // per_b, 0, 0)
    return pl.pallas_call(
        _oproj_kernel,
        out_shape=(jax.ShapeDtypeStruct((t, d), F32),
                   jax.ShapeDtypeStruct((t, d), BF16),
                   jax.ShapeDtypeStruct((n_e, t), F32)),
        grid=(t // tm,),
        in_specs=[
            pl.BlockSpec((tm, d), lambda i: (i, 0)),
            pl.BlockSpec((d, d), lambda i: (0, 0)),
            pl.BlockSpec((tm, d), lambda i: (i, 0)),
            pl.BlockSpec((None, 1, d), vec),
            pl.BlockSpec((1, d), lambda i: (0, 0)),
            pl.BlockSpec((None, 1, d), vec),
            pl.BlockSpec((None, 1, d), vec),
            pl.BlockSpec((n_e, d), lambda i: (0, 0)),
            pl.BlockSpec((n_e, 1), lambda i: (0, 0)),
        ],
        out_specs=(pl.BlockSpec((tm, d), lambda i: (i, 0)),
                   pl.BlockSpec((tm, d), lambda i: (i, 0)),
                   pl.BlockSpec((n_e, tm), lambda i: (0, i))),
        compiler_params=_cparams(("parallel",)),
        name="out_proj_route",
    )(o, wo_bf16, x2, g1, gn, sc, sh, rwt, rb)


def _moe_kernel(h_ref, gates_ref, wg_ref, wu_ref, wd_ref, x_ref, g2_ref, o_ref, acc_ref):
    e = pl.program_id(1)

    @pl.when(e == 0)
    def _():
        acc_ref[...] = jnp.zeros(acc_ref.shape, F32)

    gates = gates_ref[...]
    lane = lax.broadcasted_iota(jnp.int32, gates.shape, 1)
    gcol = jnp.sum(jnp.where(lane == e, gates, 0.0), axis=-1, keepdims=True)
    h = h_ref[...]
    hg = jnp.dot(h, wg_ref[...], preferred_element_type=F32)
    hu = jnp.dot(h, wu_ref[...], preferred_element_type=F32)
    he = hg * (1.0 / (1.0 + jnp.exp(-hg))) * hu * gcol
    acc_ref[...] += jnp.dot(he.astype(BF16), wd_ref[...], preferred_element_type=F32)

    @pl.when(e == pl.num_programs(1) - 1)
    def _():
        o_ref[...] = x_ref[...] + g2_ref[...] * acc_ref[...]


def _moe(h, gates, wg, wu, wd, x2, g2, seq):
    t, d = x2.shape
    n_e, _, f = wg.shape
    tm = MOE_ROW_TILE
    per_b = seq // tm
    return pl.pallas_call(
        _moe_kernel,
        out_shape=jax.ShapeDtypeStruct((t, d), F32),
        grid=(t // tm, n_e),
        in_specs=[
            pl.BlockSpec((tm, d), lambda i, e: (i, 0)),
            pl.BlockSpec((tm, n_e), lambda i, e: (i, 0)),
            pl.BlockSpec((None, d, f), lambda i, e: (e, 0, 0)),
            pl.BlockSpec((None, d, f), lambda i, e: (e, 0, 0)),
            pl.BlockSpec((None, f, d), lambda i, e: (e, 0, 0)),
            pl.BlockSpec((tm, d), lambda i, e: (i, 0)),
            pl.BlockSpec((None, 1, d), lambda i, e: (i // per_b, 0, 0)),
        ],
        out_specs=pl.BlockSpec((tm, d), lambda i, e: (i, 0)),
        scratch_shapes=[pltpu.VMEM((tm, d), F32)],
        compiler_params=_cparams(("parallel", "arbitrary")),
        name="moe_experts",
    )(h, gates, wg, wu, wd, x2, g2)


def kernel(x, c, ada_w, ada_b, norm_mix_g, norm_ffn_g, t5_bias, a_w_qkv, a_q_gain, a_k_gain, a_lambda, a_subln_g, a_w_o, b_w_qkv, b_q_gain, b_k_gain, b_rel_bias, b_w_o, router_w, router_bias, moe_w_gate, moe_w_up, moe_w_down):
    batch, seq, d = x.shape
    depth = ada_w.shape[0]
    assert seq % ATTN_TILE == 0 and seq % MOE_ROW_TILE == 0 and d == A_HEADS * 2 * A_HEAD_DIM
    assert d == B_HEADS * B_HEAD_DIM and A_HEAD_DIM == B_HEAD_DIM

    c_pad = jnp.zeros((8, d), F32).at[:batch].set(c.astype(F32))
    mod = _modulation(c_pad, ada_w.astype(F32), ada_b.astype(F32))[:, :batch]
    mod = mod.reshape(depth, batch, 6, 1, d)

    t5_tab = _t5_tables(t5_bias, ATTN_TILE)
    rwt = router_w.astype(F32).T
    rb = router_bias.astype(F32).reshape(-1, 1)

    x2 = x.astype(F32).reshape(batch * seq, d)
    for i in range(depth):
        sh1, sc1, g1, sh2, sc2, g2 = [mod[i, :, k] for k in range(6)]
        j = i // 2
        if i % 2 == 0:
            w_qkv, qg, kg, w_o = a_w_qkv[j], a_q_gain[j], a_k_gain[j], a_w_o[j]
        else:
            w_qkv, qg, kg, w_o = b_w_qkv[j], b_q_gain[j], b_k_gain[j], b_w_o[j]
        n_rep = d // qg.shape[0]
        qk_gain = jnp.concatenate([jnp.tile(qg.astype(F32), n_rep) * (A_HEAD_DIM ** -0.5),
                                   jnp.tile(kg.astype(F32), n_rep)]).reshape(1, 2 * d)
        qkv = _qkv_proj(x2, norm_mix_g[i].reshape(1, d), sc1, sh1, w_qkv.astype(BF16), qk_gain,
                        seq, A_HEAD_DIM)
        if i % 2 == 0:
            lambda_init = 0.8 - 0.6 * math.exp(-0.3 * i)
            o = _attention_a(qkv, t5_tab, a_lambda[j].astype(F32), a_subln_g[j].reshape(1, -1),
                             batch, seq, lambda_init)
        else:
            o = _attention_b(qkv, _band_tables(b_rel_bias[j], ATTN_TILE), batch, seq)
        x2, h, gates_t = _out_proj(o, w_o.astype(BF16), x2, g1, norm_ffn_g[i].reshape(1, d),
                                   sc2, sh2, rwt, rb, seq)
        x2 = _moe(h, gates_t.T, moe_w_gate[i].astype(BF16), moe_w_up[i].astype(BF16),
                  moe_w_down[i].astype(BF16), x2, g2, seq)
    return x2.reshape(batch, seq, d)
```

```python
import functools
import math

import numpy as np
import jax
import jax.numpy as jnp
from jax import lax
from jax.experimental import pallas as pl
from jax.experimental.pallas import tpu as pltpu

F32 = jnp.float32
BF16 = jnp.bfloat16

CHUNK = 64
A_HEADS = 8
A_HEAD_DIM = 64
T5_BUCKETS = 32
T5_MAX_DIST = 1024
B_HEADS = 16
B_HEAD_DIM = 64
LEFT_CHUNKS = 8
MAX_REL = 256
N_EXPERTS = 16
N_GROUPS = 4
E_PER_GROUP = N_EXPERTS // N_GROUPS
NORM_EPS = 1e-6
NEG_INF = -1e30
LOG2E = math.log2(math.e)

V7X_LANES = 128
V7X_MXU_DIM = 256

ATTN_TILE = LEFT_CHUNKS * CHUNK
ROW_TILE = 512
MOE_ROW_TILE = 1024
VMEM_LIMIT = 56 * 1024 * 1024

_NT = (((1,), (1,)), ((), ()))


def _cparams(sem):
    return pltpu.CompilerParams(dimension_semantics=sem, vmem_limit_bytes=VMEM_LIMIT)


def _mod_kernel(c_ref, w_ref, b_ref, o_ref):
    c = c_ref[...]
    s = c * (1.0 / (1.0 + jnp.exp(-c)))
    o_ref[...] = jnp.dot(s, w_ref[...], preferred_element_type=F32,
                         precision=lax.Precision.HIGHEST) + b_ref[...]


def _modulation(c_pad, ada_w, ada_b):
    depth, d, n = ada_w.shape
    rows = c_pad.shape[0]
    tn = 1536
    return pl.pallas_call(
        _mod_kernel,
        out_shape=jax.ShapeDtypeStruct((depth, rows, n), F32),
        grid=(depth, n // tn),
        in_specs=[
            pl.BlockSpec((rows, d), lambda i, j: (0, 0)),
            pl.BlockSpec((None, d, tn), lambda i, j: (i, 0, j)),
            pl.BlockSpec((None, 1, tn), lambda i, j: (i, 0, j)),
        ],
        out_specs=pl.BlockSpec((None, rows, tn), lambda i, j: (i, 0, j)),
        compiler_params=_cparams(("parallel", "parallel")),
        name="adaln_mod",
    )(c_pad, ada_w, ada_b.reshape(depth, 1, n))


def _norm_mod(x, g, sc, sh):
    ms = jnp.mean(x * x, axis=-1, keepdims=True)
    return x * lax.rsqrt(ms + NORM_EPS) * g * (1.0 + sc) + sh


def _qkv_kernel(x_ref, g_ref, sc_ref, sh_ref, wqt_ref, wk_ref, wvt_ref, gq_ref, gk_ref,
                qt_ref, k_ref, vt_ref, *, head_dim):
    h = _norm_mod(x_ref[...], g_ref[...], sc_ref[...], sh_ref[...]).astype(BF16)
    d, tm = qt_ref.shape
    y = lax.dot_general(wqt_ref[...], h, _NT, preferred_element_type=F32)
    y3 = y.reshape(d // head_dim, head_dim, tm)
    ss = jnp.mean(y3 * y3, axis=1, keepdims=True)
    qt_ref[...] = ((y3 * lax.rsqrt(ss + NORM_EPS)).reshape(d, tm) * gq_ref[...]).astype(BF16)
    cw = V7X_MXU_DIM
    r = lax.broadcasted_iota(jnp.int32, (cw, cw), 0) // head_dim
    c = lax.broadcasted_iota(jnp.int32, (cw, cw), 1) // head_dim
    gmat = jnp.where(r == c, 1.0 / head_dim, 0.0).astype(BF16)
    for ci in range(d // cw):
        cols = slice(ci * cw, (ci + 1) * cw)
        y = jnp.dot(h, wk_ref[:, cols], preferred_element_type=F32)
        ss = jnp.dot((y * y).astype(BF16), gmat, preferred_element_type=F32)
        k_ref[:, cols] = (y * lax.rsqrt(ss + NORM_EPS) * gk_ref[:, cols]).astype(BF16)
    vt_ref[...] = lax.dot_general(wvt_ref[...], h, _NT, preferred_element_type=F32).astype(BF16)


def _qkv_proj(x2, g, sc, sh, wqt, wk, wvt, gq, gk, seq, head_dim):
    t, d = x2.shape
    tm = ROW_TILE
    per_b = seq // tm
    vec = lambda i: (i // per_b, 0, 0)
    full = lambda i: (0, 0)
    return pl.pallas_call(
        functools.partial(_qkv_kernel, head_dim=head_dim),
        out_shape=(jax.ShapeDtypeStruct((d, t), BF16),
                   jax.ShapeDtypeStruct((t, d), BF16),
                   jax.ShapeDtypeStruct((d, t), BF16)),
        grid=(t // tm,),
        in_specs=[
            pl.BlockSpec((tm, d), lambda i: (i, 0)),
            pl.BlockSpec((1, d), full),
            pl.BlockSpec((None, 1, d), vec),
            pl.BlockSpec((None, 1, d), vec),
            pl.BlockSpec((d, d), full),
            pl.BlockSpec((d, d), full),
            pl.BlockSpec((d, d), full),
            pl.BlockSpec((d, 1), full),
            pl.BlockSpec((1, d), full),
        ],
        out_specs=(pl.BlockSpec((d, tm), lambda i: (0, i)),
                   pl.BlockSpec((tm, d), lambda i: (i, 0)),
                   pl.BlockSpec((d, tm), lambda i: (0, i))),
        compiler_params=_cparams(("parallel",)),
        name="qkv_proj",
    )(x2, g, sc, sh, wqt, wk, wvt, gq, gk)


def _softmax_pv(s, vt1, m_ref, acc_ref):
    m_prev = m_ref[...]
    m_new = jnp.maximum(m_prev, jnp.max(s, axis=0, keepdims=True))
    alpha = jnp.exp2(m_prev - m_new)
    p = jnp.exp2(s - m_new).astype(BF16)
    acc_ref[...] = alpha * acc_ref[...] + jnp.dot(vt1, p, preferred_element_type=F32)
    m_ref[...] = m_new


class _TilePipe:
    def __init__(self, k_ref, vt_ref, qa, qb, bufs, state, tk):
        self.k_ref, self.vt_ref, self.qa, self.qb = k_ref, vt_ref, qa, qb
        self.bufs, self.state, self.tk = bufs, state, tk

    def scores(self, j, parity):
        ks = pl.multiple_of(j * self.tk, self.tk)
        k = self.k_ref[pl.ds(ks, self.tk), :]
        buf = self.bufs[parity]
        buf[0] = jnp.dot(k, self.qa, preferred_element_type=F32)
        buf[1] = jnp.dot(k, self.qb, preferred_element_type=F32)

    def consume(self, j, parity, bias_a=None, bias_b=None):
        ks = pl.multiple_of(j * self.tk, self.tk)
        vt = self.vt_ref[:, pl.ds(ks, self.tk)]
        vt1 = jnp.concatenate([vt, jnp.ones((ONES_ROWS, self.tk), vt.dtype)], axis=0)
        buf = self.bufs[parity]
        (ma, acca), (mb, accb) = self.state
        sa, sb = buf[0], buf[1]
        _softmax_pv(sa if bias_a is None else sa + bias_a, vt1, ma, acca)
        _softmax_pv(sb if bias_b is None else sb + bias_b, vt1, mb, accb)

    def step(self, j, parity, bias_a=None, bias_b=None, last=False):
        if not last:
            self.scores(j + 1, 1 - parity)
        self.consume(j, parity, bias_a, bias_b)


def _split_maps(qt, head_dim):
    row = lax.broadcasted_iota(jnp.int32, qt.shape, 0)
    zero = jnp.zeros_like(qt)
    return jnp.where(row < head_dim, qt, zero), jnp.where(row >= head_dim, qt, zero)


ONES_ROWS = 16


def _init_state(refs):
    for m_ref, acc_ref in refs:
        m_ref[...] = jnp.full(m_ref.shape, NEG_INF, F32)
        acc_ref[...] = jnp.zeros(acc_ref.shape, F32)


def _normalized(acc_ref):
    acc = acc_ref[...]
    return acc[:V7X_LANES] / acc[V7X_LANES:V7X_LANES + 1]


def _toeplitz(x_row, n):
    x = jnp.broadcast_to(x_row, (n, x_row.shape[1]))
    return pltpu.roll(x, 0, 1, stride=1, stride_axis=0)[:, :n]


def _chunk_delta(n):
    kc = lax.broadcasted_iota(jnp.int32, (n, n), 0) // CHUNK
    qc = lax.broadcasted_iota(jnp.int32, (n, n), 1) // CHUNK
    return qc - kc


def _attn_a_kernel(qt_ref, k_ref, vt_ref, x_ref, lam_ref, sub_ref, o_ref,
                   tab, buf0, buf1, ma, acca, mb, accb, *, n_off, lambda_init):
    tq = qt_ref.shape[1]
    qi = pl.program_id(2)

    @pl.when(qi == 0)
    def _():
        for o in range(n_off):
            t = _toeplitz(x_ref[o], tq) * LOG2E
            if o == 0:
                t = jnp.where(_chunk_delta(tq) >= 0, t, NEG_INF)
            tab[o] = t

    qa, qb = _split_maps(qt_ref[...], A_HEAD_DIM)
    _init_state(((ma, acca), (mb, accb)))

    pipe = _TilePipe(k_ref, vt_ref, qa, qb, (buf0, buf1), ((ma, acca), (mb, accb)), tq)
    def near_steps(first):
        for o in range(first, -1, -1):
            pipe.step(qi - o, o % 2, tab[o], tab[o], last=(o == 0))

    for q0 in range(n_off - 1):
        @pl.when(qi == q0)
        def _():
            pipe.scores(0, q0 % 2)
            near_steps(q0)

    n_far = jnp.maximum(qi - (n_off - 1), 0)
    odd = n_far % 2
    for par in range(2):
        @pl.when((qi >= n_off - 1) & (qi % 2 == par))
        def _():
            pipe.scores(0, par)

    @pl.when(odd == 1)
    def _():
        pipe.step(0, n_off % 2)

    def far_pair(t, carry):
        j = 2 * t + odd
        pipe.step(j, (n_off - 1) % 2)
        pipe.step(j + 1, n_off % 2)
        return carry
    lax.fori_loop(0, n_far // 2, far_pair, 0)

    @pl.when(qi >= n_off - 1)
    def _():
        near_steps(n_off - 1)

    lam = lam_ref[...]
    lam_full = (jnp.exp(jnp.sum(lam[0:1] * lam[1:2], axis=-1, keepdims=True))
                - jnp.exp(jnp.sum(lam[2:3] * lam[3:4], axis=-1, keepdims=True)) + lambda_init)
    a = _normalized(acca) - lam_full * _normalized(accb)
    ms = jnp.mean(a * a, axis=0, keepdims=True)
    an = a * lax.rsqrt(ms + NORM_EPS)
    o_ref[...] = (an.T * (sub_ref[...] * (1.0 - lambda_init))).astype(BF16)


def _attn_b_kernel(qt_ref, k_ref, vt_ref, x_ref, o_ref, tab, buf0, buf1, ma, acca, mb, accb):
    tq = qt_ref.shape[1]
    qi = pl.program_id(2)

    @pl.when(qi == 0)
    def _():
        for m in range(2):
            for o in range(2):
                d = _chunk_delta(tq) + o * (tq // CHUNK)
                t = _toeplitz(x_ref[2 * m + o], tq) * LOG2E
                tab[2 * m + o] = jnp.where((d >= 0) & (d <= LEFT_CHUNKS), t, NEG_INF)

    qa, qb = _split_maps(qt_ref[...], B_HEAD_DIM)
    _init_state(((ma, acca), (mb, accb)))

    pipe = _TilePipe(k_ref, vt_ref, qa, qb, (buf0, buf1), ((ma, acca), (mb, accb)), tq)

    @pl.when(qi == 0)
    def _():
        pipe.scores(0, 0)
        pipe.step(0, 0, tab[0], tab[2], last=True)

    @pl.when(qi >= 1)
    def _():
        pipe.scores(qi - 1, 1)
        pipe.step(qi - 1, 1, tab[1], tab[3])
        pipe.step(qi, 0, tab[0], tab[2], last=True)

    oa, ob = _normalized(acca), _normalized(accb)
    row = lax.broadcasted_iota(jnp.int32, oa.shape, 0)
    o_ref[...] = jnp.where(row < B_HEAD_DIM, oa, ob).T.astype(BF16)


def _attn_scratch(n_tab, tq):
    one = [pltpu.VMEM((1, tq), F32), pltpu.VMEM((V7X_LANES + ONES_ROWS, tq), F32)]
    score_buf = pltpu.VMEM((2, tq, tq), F32)
    return [pltpu.VMEM((n_tab, tq, tq), F32), score_buf, score_buf] + one + one


def _attn_specs(batch, seq, nq, tq):
    return dict(
        q=pl.BlockSpec((V7X_LANES, tq), lambda b, h, i: (h, b * nq + i)),
        k=pl.BlockSpec((seq, V7X_LANES), lambda b, h, i: (b, h)),
        v=pl.BlockSpec((V7X_LANES, seq), lambda b, h, i: (h, b)),
        o=pl.BlockSpec((tq, V7X_LANES), lambda b, h, i: (b * nq + i, h)),
    )


def _attention_a(qt, k, vt, xvec, lam, sub_g, batch, seq, lambda_init):
    d, t = qt.shape
    tq = ATTN_TILE
    nq = seq // tq
    n_off = xvec.shape[1]
    sp = _attn_specs(batch, seq, nq, tq)
    return pl.pallas_call(
        functools.partial(_attn_a_kernel, n_off=n_off, lambda_init=lambda_init),
        out_shape=jax.ShapeDtypeStruct((t, d), BF16),
        grid=(batch, d // V7X_LANES, nq),
        in_specs=[sp["q"], sp["k"], sp["v"],
                  pl.BlockSpec((None,) + xvec.shape[1:], lambda b, h, i: (h, 0, 0, 0)),
                  pl.BlockSpec(lam.shape, lambda b, h, i: (0, 0)),
                  pl.BlockSpec(sub_g.shape, lambda b, h, i: (0, 0))],
        out_specs=sp["o"],
        scratch_shapes=_attn_scratch(n_off, tq),
        compiler_params=_cparams(("parallel", "parallel", "arbitrary")),
        name="diff_attention",
    )(qt, k, vt, xvec, lam, sub_g)


def _attention_b(qt, k, vt, xvec, batch, seq):
    d, t = qt.shape
    tq = ATTN_TILE
    nq = seq // tq
    sp = _attn_specs(batch, seq, nq, tq)
    return pl.pallas_call(
        _attn_b_kernel,
        out_shape=jax.ShapeDtypeStruct((t, d), BF16),
        grid=(batch, d // V7X_LANES, nq),
        in_specs=[sp["q"], sp["k"], sp["v"],
                  pl.BlockSpec((None,) + xvec.shape[1:], lambda b, h, i: (h, 0, 0, 0))],
        out_specs=sp["o"],
        scratch_shapes=_attn_scratch(4, tq),
        compiler_params=_cparams(("parallel", "parallel", "arbitrary")),
        name="chunk_attention",
    )(qt, k, vt, xvec)


def _t5_bucket(rel):
    nb = T5_BUCKETS // 2
    ret = jnp.where(rel > 0, nb, 0)
    n = jnp.abs(rel)
    max_exact = nb // 2
    nf = jnp.maximum(n, 1).astype(F32)
    large = max_exact + (jnp.log(nf / max_exact) / math.log(T5_MAX_DIST / max_exact)
                         * (nb - max_exact)).astype(jnp.int32)
    large = jnp.minimum(large, nb - 1)
    return ret + jnp.where(n < max_exact, n, large)


def _t5_const_distance():
    nb = T5_BUCKETS // 2
    max_exact = nb // 2
    n = np.arange(max_exact, 4 * T5_MAX_DIST, dtype=np.float64)
    large = max_exact + np.floor(np.log(n / max_exact) / math.log(T5_MAX_DIST / max_exact) * (nb - max_exact))
    below = np.nonzero(large < nb - 1)[0]
    return int(n[below[-1]]) + 2


def _tile_rel(tile, n_off):
    i = jnp.arange(2 * tile, dtype=jnp.int32)
    rel = jnp.where(i < tile, -i, 2 * tile - i)
    return rel[None, :] - tile * jnp.arange(n_off, dtype=jnp.int32)[:, None]


def _t5_vectors(t5_bias, tile):
    n_off = 1
    while (n_off - 1) * tile + 1 < _t5_const_distance():
        n_off += 1
    tb = t5_bias.astype(F32)
    vals = tb[_t5_bucket(_tile_rel(tile, n_off))] - tb[T5_BUCKETS // 2 - 1]
    return vals.transpose(2, 0, 1)[:, :, None, :]


def _band_vectors(rel_bias, tile):
    idx = jnp.clip(_tile_rel(tile, 2), -MAX_REL, MAX_REL) + MAX_REL
    vals = rel_bias.astype(F32)[:, idx]
    return vals.reshape(rel_bias.shape[0] // 2, 4, 1, 2 * tile)


def _top2_sum4(r0, r1, r2, r3):
    a, b = jnp.maximum(r0, r1), jnp.minimum(r0, r1)
    c, d = jnp.maximum(r2, r3), jnp.minimum(r2, r3)
    return jnp.maximum(a, c) + jnp.maximum(jnp.minimum(a, c), jnp.maximum(b, d))


def _route(logits, rbias):
    n_e = logits.shape[0]
    scores = 1.0 / (1.0 + jnp.exp(-logits))
    sel = scores + rbias
    row = lax.broadcasted_iota(jnp.int32, sel.shape, 0)
    best = None
    for g in range(N_GROUPS):
        rows = [sel[g * E_PER_GROUP + i: g * E_PER_GROUP + i + 1, :] for i in range(E_PER_GROUP)]
        gs = _top2_sum4(*rows)
        if best is None:
            best, gidx = gs, jnp.zeros(gs.shape, jnp.int32)
        else:
            gidx = jnp.where(gs > best, g, gidx)
            best = jnp.maximum(best, gs)
    masked = jnp.where(row // E_PER_GROUP == gidx, sel, NEG_INF)
    m1 = jnp.max(masked, axis=0, keepdims=True)
    i1 = jnp.min(jnp.where(masked == m1, row, n_e), axis=0, keepdims=True)
    masked2 = jnp.where(row == i1, -3.0e38, masked)
    m2 = jnp.max(masked2, axis=0, keepdims=True)
    i2 = jnp.min(jnp.where(masked2 == m2, row, n_e), axis=0, keepdims=True)
    w1 = jnp.sum(jnp.where(row == i1, scores, 0.0), axis=0, keepdims=True)
    w2 = jnp.sum(jnp.where(row == i2, scores, 0.0), axis=0, keepdims=True)
    den = w1 + w2
    return jnp.where(row == i1, w1 / den, 0.0) + jnp.where(row == i2, w2 / den, 0.0)


def _oproj_kernel(o_ref, wo_ref, x_ref, g1_ref, gn_ref, sc_ref, sh_ref, rwt_ref, rb_ref,
                  xo_ref, h_ref, gates_ref):
    y = jnp.dot(o_ref[...], wo_ref[...], preferred_element_type=F32)
    xn = x_ref[...] + g1_ref[...] * y
    xo_ref[...] = xn
    h = _norm_mod(xn, gn_ref[...], sc_ref[...], sh_ref[...])
    h_ref[...] = h.astype(BF16)
    logits = lax.dot_general(rwt_ref[...], h, _NT,
                             preferred_element_type=F32, precision=lax.Precision.HIGHEST)
    gates_ref[...] = _route(logits, rb_ref[...])


def _out_proj(o, wo_bf16, x2, g1, gn, sc, sh, rwt, rb, seq):
    t, d = x2.shape
    tm = ROW_TILE
    per_b = seq // tm
    n_e = rwt.shape[0]
    vec = lambda i: (i // per_b, 0, 0)
    return pl.pallas_call(
        _oproj_kernel,
        out_shape=(jax.ShapeDtypeStruct((t, d), F32),
                   jax.ShapeDtypeStruct((t, d), BF16),
                   jax.ShapeDtypeStruct((n_e, t), F32)),
        grid=(t // tm,),
        in_specs=[
            pl.BlockSpec((tm, d), lambda i: (i, 0)),
            pl.BlockSpec((d, d), lambda i: (0, 0)),
            pl.BlockSpec((tm, d), lambda i: (i, 0)),
            pl.BlockSpec((None, 1, d), vec),
            pl.BlockSpec((1, d), lambda i: (0, 0)),
            pl.BlockSpec((None, 1, d), vec),
            pl.BlockSpec((None, 1, d), vec),
            pl.BlockSpec((n_e, d), lambda i: (0, 0)),
            pl.BlockSpec((n_e, 1), lambda i: (0, 0)),
        ],
        out_specs=(pl.BlockSpec((tm, d), lambda i: (i, 0)),
                   pl.BlockSpec((tm, d), lambda i: (i, 0)),
                   pl.BlockSpec((n_e, tm), lambda i: (0, i))),
        compiler_params=_cparams(("parallel",)),
        name="out_proj_route",
    )(o, wo_bf16, x2, g1, gn, sc, sh, rwt, rb)


def _moe_kernel(h_ref, gates_ref, wg_ref, wu_ref, wd_ref, x_ref, g2_ref, o_ref, acc_ref):
    e = pl.program_id(1)

    @pl.when(e == 0)
    def _():
        acc_ref[...] = jnp.zeros(acc_ref.shape, F32)

    gates = gates_ref[...]
    lane = lax.broadcasted_iota(jnp.int32, gates.shape, 1)
    gcol = jnp.sum(jnp.where(lane == e, gates, 0.0), axis=-1, keepdims=True)
    h = h_ref[...]
    hg = jnp.dot(h, wg_ref[...], preferred_element_type=F32)
    hu = jnp.dot(h, wu_ref[...], preferred_element_type=F32)
    he = hg * (1.0 / (1.0 + jnp.exp(-hg))) * hu * gcol
    acc_ref[...] += jnp.dot(he.astype(BF16), wd_ref[...], preferred_element_type=F32)

    @pl.when(e == pl.num_programs(1) - 1)
    def _():
        o_ref[...] = x_ref[...] + g2_ref[...] * acc_ref[...]


def _moe(h, gates, wg, wu, wd, x2, g2, seq):
    t, d = x2.shape
    n_e, _, f = wg.shape
    tm = MOE_ROW_TILE
    per_b = seq // tm
    return pl.pallas_call(
        _moe_kernel,
        out_shape=jax.ShapeDtypeStruct((t, d), F32),
        grid=(t // tm, n_e),
        in_specs=[
            pl.BlockSpec((tm, d), lambda i, e: (i, 0)),
            pl.BlockSpec((tm, n_e), lambda i, e: (i, 0)),
            pl.BlockSpec((None, d, f), lambda i, e: (e, 0, 0)),
            pl.BlockSpec((None, d, f), lambda i, e: (e, 0, 0)),
            pl.BlockSpec((None, f, d), lambda i, e: (e, 0, 0)),
            pl.BlockSpec((tm, d), lambda i, e: (i, 0)),
            pl.BlockSpec((None, 1, d), lambda i, e: (i // per_b, 0, 0)),
        ],
        out_specs=pl.BlockSpec((tm, d), lambda i, e: (i, 0)),
        scratch_shapes=[pltpu.VMEM((tm, d), F32)],
        compiler_params=_cparams(("parallel", "arbitrary")),
        name="moe_experts",
    )(h, gates, wg, wu, wd, x2, g2)


def kernel(x, c, ada_w, ada_b, norm_mix_g, norm_ffn_g, t5_bias, a_w_qkv, a_q_gain, a_k_gain, a_lambda, a_subln_g, a_w_o, b_w_qkv, b_q_gain, b_k_gain, b_rel_bias, b_w_o, router_w, router_bias, moe_w_gate, moe_w_up, moe_w_down):
    batch, seq, d = x.shape
    depth = ada_w.shape[0]
    assert seq % ATTN_TILE == 0 and seq % MOE_ROW_TILE == 0 and d == A_HEADS * 2 * A_HEAD_DIM
    assert d == B_HEADS * B_HEAD_DIM and A_HEAD_DIM == B_HEAD_DIM

    c_pad = jnp.zeros((8, d), F32).at[:batch].set(c.astype(F32))
    mod = _modulation(c_pad, ada_w.astype(F32), ada_b.astype(F32))[:, :batch]
    mod = mod.reshape(depth, batch, 6, 1, d)

    rwt = router_w.astype(F32).T
    rb = router_bias.astype(F32).reshape(-1, 1)

    x2 = x.astype(F32).reshape(batch * seq, d)
    for i in range(depth):
        sh1, sc1, g1, sh2, sc2, g2 = [mod[i, :, k] for k in range(6)]
        j = i // 2
        if i % 2 == 0:
            w_qkv, qg, kg, w_o = a_w_qkv[j], a_q_gain[j], a_k_gain[j], a_w_o[j]
        else:
            w_qkv, qg, kg, w_o = b_w_qkv[j], b_q_gain[j], b_k_gain[j], b_w_o[j]
        n_rep = d // qg.shape[0]
        gq = (jnp.tile(qg.astype(F32), n_rep) * (A_HEAD_DIM ** -0.5 * LOG2E)).reshape(d, 1)
        gk = jnp.tile(kg.astype(F32), n_rep).reshape(1, d)
        wb = w_qkv.astype(BF16)
        qt, k, vt = _qkv_proj(x2, norm_mix_g[i].reshape(1, d), sc1, sh1,
                              wb[:, :d].T, wb[:, d:2 * d], wb[:, 2 * d:].T, gq, gk, seq, A_HEAD_DIM)
        if i % 2 == 0:
            lambda_init = 0.8 - 0.6 * math.exp(-0.3 * i)
            o = _attention_a(qt, k, vt, _t5_vectors(t5_bias, ATTN_TILE), a_lambda[j].astype(F32),
                             a_subln_g[j].reshape(1, -1), batch, seq, lambda_init)
        else:
            o = _attention_b(qt, k, vt, _band_vectors(b_rel_bias[j], ATTN_TILE), batch, seq)
        x2, h, gates_t = _out_proj(o, w_o.astype(BF16), x2, g1, norm_ffn_g[i].reshape(1, d),
                                   sc2, sh2, rwt, rb, seq)
        x2 = _moe(h, gates_t.T, moe_w_gate[i].astype(BF16), moe_w_up[i].astype(BF16),
                  moe_w_down[i].astype(BF16), x2, g2, seq)
    return x2.reshape(batch, seq, d)
```

```python
import functools
import math

import numpy as np
import jax
import jax.numpy as jnp
from jax import lax
from jax.experimental import pallas as pl
from jax.experimental.pallas import tpu as pltpu

F32 = jnp.float32
BF16 = jnp.bfloat16

CHUNK = 64
A_HEADS = 8
A_HEAD_DIM = 64
T5_BUCKETS = 32
T5_MAX_DIST = 1024
B_HEADS = 16
B_HEAD_DIM = 64
LEFT_CHUNKS = 8
MAX_REL = 256
N_EXPERTS = 16
N_GROUPS = 4
E_PER_GROUP = N_EXPERTS // N_GROUPS
NORM_EPS = 1e-6
NEG_INF = -1e30
LOG2E = math.log2(math.e)

V7X_LANES = 128
V7X_MXU_DIM = 256

ATTN_TILE = LEFT_CHUNKS * CHUNK
ROW_TILE = 512
MOE_SLOTS = 96
MOE_TILES_PER_STEP = 8
VMEM_LIMIT = 56 * 1024 * 1024

_NT = (((1,), (1,)), ((), ()))


def _cparams(sem):
    return pltpu.CompilerParams(dimension_semantics=sem, vmem_limit_bytes=VMEM_LIMIT)


def _mod_kernel(c_ref, w_ref, b_ref, o_ref):
    c = c_ref[...]
    s = c * (1.0 / (1.0 + jnp.exp(-c)))
    o_ref[...] = jnp.dot(s, w_ref[...], preferred_element_type=F32,
                         precision=lax.Precision.HIGHEST) + b_ref[...]


def _modulation(c_pad, ada_w, ada_b):
    depth, d, n = ada_w.shape
    rows = c_pad.shape[0]
    tn = 1536
    return pl.pallas_call(
        _mod_kernel,
        out_shape=jax.ShapeDtypeStruct((depth, rows, n), F32),
        grid=(depth, n // tn),
        in_specs=[
            pl.BlockSpec((rows, d), lambda i, j: (0, 0)),
            pl.BlockSpec((None, d, tn), lambda i, j: (i, 0, j)),
            pl.BlockSpec((None, 1, tn), lambda i, j: (i, 0, j)),
        ],
        out_specs=pl.BlockSpec((None, rows, tn), lambda i, j: (i, 0, j)),
        compiler_params=_cparams(("parallel", "parallel")),
        name="adaln_mod",
    )(c_pad, ada_w, ada_b.reshape(depth, 1, n))


def _norm_mod(x, g, sc, sh):
    ms = jnp.mean(x * x, axis=-1, keepdims=True)
    return x * lax.rsqrt(ms + NORM_EPS) * g * (1.0 + sc) + sh


def _qkv_kernel(x_ref, g_ref, sc_ref, sh_ref, wqt_ref, wk_ref, wvt_ref, gq_ref, gk_ref,
                qt_ref, k_ref, vt_ref, *, head_dim):
    h = _norm_mod(x_ref[...], g_ref[...], sc_ref[...], sh_ref[...]).astype(BF16)
    d, tm = qt_ref.shape
    y = lax.dot_general(wqt_ref[...], h, _NT, preferred_element_type=F32)
    y3 = y.reshape(d // head_dim, head_dim, tm)
    ss = jnp.mean(y3 * y3, axis=1, keepdims=True)
    qt_ref[...] = ((y3 * lax.rsqrt(ss + NORM_EPS)).reshape(d, tm) * gq_ref[...]).astype(BF16)
    cw = V7X_MXU_DIM
    r = lax.broadcasted_iota(jnp.int32, (cw, cw), 0) // head_dim
    c = lax.broadcasted_iota(jnp.int32, (cw, cw), 1) // head_dim
    gmat = jnp.where(r == c, 1.0 / head_dim, 0.0).astype(BF16)
    for ci in range(d // cw):
        cols = slice(ci * cw, (ci + 1) * cw)
        y = jnp.dot(h, wk_ref[:, cols], preferred_element_type=F32)
        ss = jnp.dot((y * y).astype(BF16), gmat, preferred_element_type=F32)
        k_ref[:, cols] = (y * lax.rsqrt(ss + NORM_EPS) * gk_ref[:, cols]).astype(BF16)
    vt_ref[...] = lax.dot_general(wvt_ref[...], h, _NT, preferred_element_type=F32).astype(BF16)


def _qkv_proj(x2, g, sc, sh, wqt, wk, wvt, gq, gk, seq, head_dim):
    t, d = x2.shape
    tm = ROW_TILE
    per_b = seq // tm
    vec = lambda i: (i // per_b, 0, 0)
    full = lambda i: (0, 0)
    return pl.pallas_call(
        functools.partial(_qkv_kernel, head_dim=head_dim),
        out_shape=(jax.ShapeDtypeStruct((d, t), BF16),
                   jax.ShapeDtypeStruct((t, d), BF16),
                   jax.ShapeDtypeStruct((d, t), BF16)),
        grid=(t // tm,),
        in_specs=[
            pl.BlockSpec((tm, d), lambda i: (i, 0)),
            pl.BlockSpec((1, d), full),
            pl.BlockSpec((None, 1, d), vec),
            pl.BlockSpec((None, 1, d), vec),
            pl.BlockSpec((d, d), full),
            pl.BlockSpec((d, d), full),
            pl.BlockSpec((d, d), full),
            pl.BlockSpec((d, 1), full),
            pl.BlockSpec((1, d), full),
        ],
        out_specs=(pl.BlockSpec((d, tm), lambda i: (0, i)),
                   pl.BlockSpec((tm, d), lambda i: (i, 0)),
                   pl.BlockSpec((d, tm), lambda i: (0, i))),
        compiler_params=_cparams(("parallel",)),
        name="qkv_proj",
    )(x2, g, sc, sh, wqt, wk, wvt, gq, gk)


def _softmax_pv(s, vt1, m_ref, acc_ref):
    m_prev = m_ref[...]
    m_new = jnp.maximum(m_prev, jnp.max(s, axis=0, keepdims=True))
    alpha = jnp.exp2(m_prev - m_new)
    p = jnp.exp2(s - m_new).astype(BF16)
    acc_ref[...] = alpha * acc_ref[...] + jnp.dot(vt1, p, preferred_element_type=F32)
    m_ref[...] = m_new


class _TilePipe:
    def __init__(self, k_ref, vt_ref, qa, qb, bufs, state, tk):
        self.k_ref, self.vt_ref, self.qa, self.qb = k_ref, vt_ref, qa, qb
        self.bufs, self.state, self.tk = bufs, state, tk

    def scores(self, j, parity):
        ks = pl.multiple_of(j * self.tk, self.tk)
        k = self.k_ref[pl.ds(ks, self.tk), :]
        buf = self.bufs[parity]
        buf[0] = jnp.dot(k, self.qa, preferred_element_type=F32)
        buf[1] = jnp.dot(k, self.qb, preferred_element_type=F32)

    def consume(self, j, parity, bias_a=None, bias_b=None):
        ks = pl.multiple_of(j * self.tk, self.tk)
        vt = self.vt_ref[:, pl.ds(ks, self.tk)]
        vt1 = jnp.concatenate([vt, jnp.ones((ONES_ROWS, self.tk), vt.dtype)], axis=0)
        buf = self.bufs[parity]
        (ma, acca), (mb, accb) = self.state
        sa, sb = buf[0], buf[1]
        _softmax_pv(sa if bias_a is None else sa + bias_a, vt1, ma, acca)
        _softmax_pv(sb if bias_b is None else sb + bias_b, vt1, mb, accb)

    def step(self, j, parity, bias_a=None, bias_b=None, last=False):
        if not last:
            self.scores(j + 1, 1 - parity)
        self.consume(j, parity, bias_a, bias_b)


def _split_maps(qt, head_dim):
    row = lax.broadcasted_iota(jnp.int32, qt.shape, 0)
    zero = jnp.zeros_like(qt)
    return jnp.where(row < head_dim, qt, zero), jnp.where(row >= head_dim, qt, zero)


ONES_ROWS = 16


def _init_state(refs):
    for m_ref, acc_ref in refs:
        m_ref[...] = jnp.full(m_ref.shape, NEG_INF, F32)
        acc_ref[...] = jnp.zeros(acc_ref.shape, F32)


def _normalized(acc_ref):
    acc = acc_ref[...]
    return acc[:V7X_LANES] / acc[V7X_LANES:V7X_LANES + 1]


def _toeplitz(x_row, n):
    x = jnp.broadcast_to(x_row, (n, x_row.shape[1]))
    return pltpu.roll(x, 0, 1, stride=1, stride_axis=0)[:, :n]


def _chunk_delta(n):
    kc = lax.broadcasted_iota(jnp.int32, (n, n), 0) // CHUNK
    qc = lax.broadcasted_iota(jnp.int32, (n, n), 1) // CHUNK
    return qc - kc


def _attn_a_kernel(qt_ref, k_ref, vt_ref, x_ref, lam_ref, sub_ref, o_ref,
                   tab, buf0, buf1, ma, acca, mb, accb, *, n_off, lambda_init):
    tq = qt_ref.shape[1]
    qi = pl.program_id(2)

    @pl.when(qi == 0)
    def _():
        for o in range(n_off):
            t = _toeplitz(x_ref[o], tq) * LOG2E
            if o == 0:
                t = jnp.where(_chunk_delta(tq) >= 0, t, NEG_INF)
            tab[o] = t

    qa, qb = _split_maps(qt_ref[...], A_HEAD_DIM)
    _init_state(((ma, acca), (mb, accb)))

    pipe = _TilePipe(k_ref, vt_ref, qa, qb, (buf0, buf1), ((ma, acca), (mb, accb)), tq)
    def near_steps(first):
        for o in range(first, -1, -1):
            pipe.step(qi - o, o % 2, tab[o], tab[o], last=(o == 0))

    for q0 in range(n_off - 1):
        @pl.when(qi == q0)
        def _():
            pipe.scores(0, q0 % 2)
            near_steps(q0)

    n_far = jnp.maximum(qi - (n_off - 1), 0)
    odd = n_far % 2
    for par in range(2):
        @pl.when((qi >= n_off - 1) & (qi % 2 == par))
        def _():
            pipe.scores(0, par)

    @pl.when(odd == 1)
    def _():
        pipe.step(0, n_off % 2)

    def far_pair(t, carry):
        j = 2 * t + odd
        pipe.step(j, (n_off - 1) % 2)
        pipe.step(j + 1, n_off % 2)
        return carry
    lax.fori_loop(0, n_far // 2, far_pair, 0)

    @pl.when(qi >= n_off - 1)
    def _():
        near_steps(n_off - 1)

    lam = lam_ref[...]
    lam_full = (jnp.exp(jnp.sum(lam[0:1] * lam[1:2], axis=-1, keepdims=True))
                - jnp.exp(jnp.sum(lam[2:3] * lam[3:4], axis=-1, keepdims=True)) + lambda_init)
    a = _normalized(acca) - lam_full * _normalized(accb)
    ms = jnp.mean(a * a, axis=0, keepdims=True)
    an = a * lax.rsqrt(ms + NORM_EPS)
    o_ref[...] = (an.T * (sub_ref[...] * (1.0 - lambda_init))).astype(BF16)


def _attn_b_kernel(qt_ref, k_ref, vt_ref, x_ref, o_ref, tab, buf0, buf1, ma, acca, mb, accb):
    tq = qt_ref.shape[1]
    qi = pl.program_id(2)

    @pl.when(qi == 0)
    def _():
        for m in range(2):
            for o in range(2):
                d = _chunk_delta(tq) + o * (tq // CHUNK)
                t = _toeplitz(x_ref[2 * m + o], tq) * LOG2E
                tab[2 * m + o] = jnp.where((d >= 0) & (d <= LEFT_CHUNKS), t, NEG_INF)

    qa, qb = _split_maps(qt_ref[...], B_HEAD_DIM)
    _init_state(((ma, acca), (mb, accb)))

    pipe = _TilePipe(k_ref, vt_ref, qa, qb, (buf0, buf1), ((ma, acca), (mb, accb)), tq)

    @pl.when(qi == 0)
    def _():
        pipe.scores(0, 0)
        pipe.step(0, 0, tab[0], tab[2], last=True)

    @pl.when(qi >= 1)
    def _():
        pipe.scores(qi - 1, 1)
        pipe.step(qi - 1, 1, tab[1], tab[3])
        pipe.step(qi, 0, tab[0], tab[2], last=True)

    oa, ob = _normalized(acca), _normalized(accb)
    row = lax.broadcasted_iota(jnp.int32, oa.shape, 0)
    o_ref[...] = jnp.where(row < B_HEAD_DIM, oa, ob).T.astype(BF16)


def _attn_scratch(n_tab, tq):
    one = [pltpu.VMEM((1, tq), F32), pltpu.VMEM((V7X_LANES + ONES_ROWS, tq), F32)]
    score_buf = pltpu.VMEM((2, tq, tq), F32)
    return [pltpu.VMEM((n_tab, tq, tq), F32), score_buf, score_buf] + one + one


def _attn_specs(batch, seq, nq, tq):
    return dict(
        q=pl.BlockSpec((V7X_LANES, tq), lambda b, h, i: (h, b * nq + i)),
        k=pl.BlockSpec((seq, V7X_LANES), lambda b, h, i: (b, h)),
        v=pl.BlockSpec((V7X_LANES, seq), lambda b, h, i: (h, b)),
        o=pl.BlockSpec((tq, V7X_LANES), lambda b, h, i: (b * nq + i, h)),
    )


def _attention_a(qt, k, vt, xvec, lam, sub_g, batch, seq, lambda_init):
    d, t = qt.shape
    tq = ATTN_TILE
    nq = seq // tq
    n_off = xvec.shape[1]
    sp = _attn_specs(batch, seq, nq, tq)
    return pl.pallas_call(
        functools.partial(_attn_a_kernel, n_off=n_off, lambda_init=lambda_init),
        out_shape=jax.ShapeDtypeStruct((t, d), BF16),
        grid=(batch, d // V7X_LANES, nq),
        in_specs=[sp["q"], sp["k"], sp["v"],
                  pl.BlockSpec((None,) + xvec.shape[1:], lambda b, h, i: (h, 0, 0, 0)),
                  pl.BlockSpec(lam.shape, lambda b, h, i: (0, 0)),
                  pl.BlockSpec(sub_g.shape, lambda b, h, i: (0, 0))],
        out_specs=sp["o"],
        scratch_shapes=_attn_scratch(n_off, tq),
        compiler_params=_cparams(("parallel", "parallel", "arbitrary")),
        name="diff_attention",
    )(qt, k, vt, xvec, lam, sub_g)


def _attention_b(qt, k, vt, xvec, batch, seq):
    d, t = qt.shape
    tq = ATTN_TILE
    nq = seq // tq
    sp = _attn_specs(batch, seq, nq, tq)
    return pl.pallas_call(
        _attn_b_kernel,
        out_shape=jax.ShapeDtypeStruct((t, d), BF16),
        grid=(batch, d // V7X_LANES, nq),
        in_specs=[sp["q"], sp["k"], sp["v"],
                  pl.BlockSpec((None,) + xvec.shape[1:], lambda b, h, i: (h, 0, 0, 0))],
        out_specs=sp["o"],
        scratch_shapes=_attn_scratch(4, tq),
        compiler_params=_cparams(("parallel", "parallel", "arbitrary")),
        name="chunk_attention",
    )(qt, k, vt, xvec)


def _t5_bucket(rel):
    nb = T5_BUCKETS // 2
    ret = jnp.where(rel > 0, nb, 0)
    n = jnp.abs(rel)
    max_exact = nb // 2
    nf = jnp.maximum(n, 1).astype(F32)
    large = max_exact + (jnp.log(nf / max_exact) / math.log(T5_MAX_DIST / max_exact)
                         * (nb - max_exact)).astype(jnp.int32)
    large = jnp.minimum(large, nb - 1)
    return ret + jnp.where(n < max_exact, n, large)


def _t5_const_distance():
    nb = T5_BUCKETS // 2
    max_exact = nb // 2
    n = np.arange(max_exact, 4 * T5_MAX_DIST, dtype=np.float64)
    large = max_exact + np.floor(np.log(n / max_exact) / math.log(T5_MAX_DIST / max_exact) * (nb - max_exact))
    below = np.nonzero(large < nb - 1)[0]
    return int(n[below[-1]]) + 2


def _tile_rel(tile, n_off):
    i = jnp.arange(2 * tile, dtype=jnp.int32)
    rel = jnp.where(i < tile, -i, 2 * tile - i)
    return rel[None, :] - tile * jnp.arange(n_off, dtype=jnp.int32)[:, None]


def _t5_vectors(t5_bias, tile):
    n_off = 1
    while (n_off - 1) * tile + 1 < _t5_const_distance():
        n_off += 1
    tb = t5_bias.astype(F32)
    vals = tb[_t5_bucket(_tile_rel(tile, n_off))] - tb[T5_BUCKETS // 2 - 1]
    return vals.transpose(2, 0, 1)[:, :, None, :]


def _band_vectors(rel_bias, tile):
    idx = jnp.clip(_tile_rel(tile, 2), -MAX_REL, MAX_REL) + MAX_REL
    vals = rel_bias.astype(F32)[:, idx]
    return vals.reshape(rel_bias.shape[0] // 2, 4, 1, 2 * tile)


def _top2_sum4(r0, r1, r2, r3):
    a, b = jnp.maximum(r0, r1), jnp.minimum(r0, r1)
    c, d = jnp.maximum(r2, r3), jnp.minimum(r2, r3)
    return jnp.maximum(a, c) + jnp.maximum(jnp.minimum(a, c), jnp.maximum(b, d))


def _route(logits, rbias):
    n_e, n = logits.shape
    scores = 1.0 / (1.0 + jnp.exp(-logits))
    sel = scores + rbias
    row = lax.broadcasted_iota(jnp.int32, sel.shape, 0)
    best = None
    for g in range(N_GROUPS):
        rows = [sel[g * E_PER_GROUP + i: g * E_PER_GROUP + i + 1, :] for i in range(E_PER_GROUP)]
        gs = _top2_sum4(*rows)
        if best is None:
            best, gidx = gs, jnp.zeros(gs.shape, jnp.int32)
        else:
            gidx = jnp.where(gs > best, g, gidx)
            best = jnp.maximum(best, gs)
    masked = jnp.where(row // E_PER_GROUP == gidx, sel, NEG_INF)
    m1 = jnp.max(masked, axis=0, keepdims=True)
    i1 = jnp.min(jnp.where(masked == m1, row, n_e), axis=0, keepdims=True)
    masked2 = jnp.where(row == i1, -3.0e38, masked)
    m2 = jnp.max(masked2, axis=0, keepdims=True)
    i2 = jnp.min(jnp.where(masked2 == m2, row, n_e), axis=0, keepdims=True)
    w1 = jnp.sum(jnp.where(row == i1, scores, 0.0), axis=0, keepdims=True)
    w2 = jnp.sum(jnp.where(row == i2, scores, 0.0), axis=0, keepdims=True)
    den = w1 + w2
    member = jnp.where((row == i1) | (row == i2), 1.0, 0.0).astype(BF16)
    before = jnp.where(lax.broadcasted_iota(jnp.int32, (n, n), 0) < lax.broadcasted_iota(jnp.int32, (n, n), 1),
                       1.0, 0.0).astype(BF16)
    pos = jnp.dot(member, before, preferred_element_type=F32)
    p1 = jnp.sum(jnp.where(row == i1, pos, 0.0), axis=0, keepdims=True)
    p2 = jnp.sum(jnp.where(row == i2, pos, 0.0), axis=0, keepdims=True)
    out_row = lax.broadcasted_iota(jnp.int32, (8, n), 0)
    out = jnp.zeros((8, n), F32)
    for r, val in enumerate((i1.astype(F32), i2.astype(F32), p1, p2, w1 / den, w2 / den)):
        out = jnp.where(out_row == r, val, out)
    return out


def _oproj_kernel(o_ref, wo_ref, x_ref, g1_ref, gn_ref, sc_ref, sh_ref, rwt_ref, rb_ref,
                  xo_ref, h_ref, route_ref):
    y = jnp.dot(o_ref[...], wo_ref[...], preferred_element_type=F32)
    xn = x_ref[...] + g1_ref[...] * y
    xo_ref[...] = xn
    h = _norm_mod(xn, gn_ref[...], sc_ref[...], sh_ref[...])
    h_ref[...] = h.astype(BF16)
    logits = lax.dot_general(rwt_ref[...], h, _NT,
                             preferred_element_type=F32, precision=lax.Precision.HIGHEST)
    route_ref[...] = _route(logits, rb_ref[...])


def _out_proj(o, wo_bf16, x2, g1, gn, sc, sh, rwt, rb, seq):
    t, d = x2.shape
    tm = ROW_TILE
    per_b = seq // tm
    n_e = rwt.shape[0]
    vec = lambda i: (i // per_b, 0, 0)
    return pl.pallas_call(
        _oproj_kernel,
        out_shape=(jax.ShapeDtypeStruct((t, d), F32),
                   jax.ShapeDtypeStruct((t, d), BF16),
                   jax.ShapeDtypeStruct((8, t), F32)),
        grid=(t // tm,),
        in_specs=[
            pl.BlockSpec((tm, d), lambda i: (i, 0)),
            pl.BlockSpec((d, d), lambda i: (0, 0)),
            pl.BlockSpec((tm, d), lambda i: (i, 0)),
            pl.BlockSpec((None, 1, d), vec),
            pl.BlockSpec((1, d), lambda i: (0, 0)),
            pl.BlockSpec((None, 1, d), vec),
            pl.BlockSpec((None, 1, d), vec),
            pl.BlockSpec((n_e, d), lambda i: (0, 0)),
            pl.BlockSpec((n_e, 1), lambda i: (0, 0)),
        ],
        out_specs=(pl.BlockSpec((tm, d), lambda i: (i, 0)),
                   pl.BlockSpec((tm, d), lambda i: (i, 0)),
                   pl.BlockSpec((8, tm), lambda i: (0, i))),
        compiler_params=_cparams(("parallel",)),
        name="out_proj_route",
    )(o, wo_bf16, x2, g1, gn, sc, sh, rwt, rb)


def _slot_index(expert, slot, rnd, cap):
    local = slot - rnd * cap
    return jnp.where((local >= 0) & (local < cap), expert * cap + local, -1.0)


def _dispatch_kernel(rnd_ref, route_ref, h_ref, hc_ref):
    n_e, cap, d = hc_ref.shape
    tm = h_ref.shape[0]
    rnd = rnd_ref[0].astype(F32)
    r = route_ref[...]
    c1 = _slot_index(r[0:1], r[2:3], rnd, cap)
    c2 = _slot_index(r[1:2], r[3:4], rnd, cap)
    srow = lax.broadcasted_iota(jnp.int32, (n_e * cap, tm), 0).astype(F32)
    sel = jnp.where((srow == c1) | (srow == c2), 1.0, 0.0).astype(BF16)
    hc = jnp.dot(sel, h_ref[...], preferred_element_type=F32)
    hc_ref[...] = hc.reshape(n_e, cap, d).astype(BF16)


def _dispatch(rnd, route, h, cap, n_e):
    t, d = h.shape
    tm = ROW_TILE
    return pl.pallas_call(
        _dispatch_kernel,
        out_shape=jax.ShapeDtypeStruct((t // tm, n_e, cap, d), BF16),
        grid_spec=pltpu.PrefetchScalarGridSpec(
            num_scalar_prefetch=1,
            grid=(t // tm,),
            in_specs=[pl.BlockSpec((8, tm), lambda i, rnd: (0, i)),
                      pl.BlockSpec((tm, d), lambda i, rnd: (i, 0))],
            out_specs=pl.BlockSpec((None, n_e, cap, d), lambda i, rnd: (i, 0, 0, 0)),
        ),
        compiler_params=_cparams(("parallel",)),
        name="moe_dispatch",
    )(rnd, route, h)


def _expert_kernel(hc_ref, wg_ref, wu_ref, wd_ref, yc_ref, wg_b, wu_b, wd_b):
    @pl.when(pl.program_id(1) == 0)
    def _():
        wg_b[...] = wg_ref[...].astype(BF16)
        wu_b[...] = wu_ref[...].astype(BF16)
        wd_b[...] = wd_ref[...].astype(BF16)

    tb, cap, d = hc_ref.shape
    xc = hc_ref[...].reshape(tb * cap, d)
    hg = jnp.dot(xc, wg_b[...], preferred_element_type=F32)
    hu = jnp.dot(xc, wu_b[...], preferred_element_type=F32)
    he = hg * (1.0 / (1.0 + jnp.exp(-hg))) * hu
    y = jnp.dot(he.astype(BF16), wd_b[...], preferred_element_type=F32)
    yc_ref[...] = y.reshape(tb, cap, d).astype(BF16)


def _experts(hc, wg, wu, wd):
    n_tiles, n_e, cap, d = hc.shape
    f = wg.shape[2]
    tb = MOE_TILES_PER_STEP
    blk = pl.BlockSpec((tb, None, cap, d), lambda e, j: (j, e, 0, 0))
    return pl.pallas_call(
        _expert_kernel,
        out_shape=jax.ShapeDtypeStruct(hc.shape, BF16),
        grid=(n_e, n_tiles // tb),
        in_specs=[blk,
                  pl.BlockSpec((None, d, f), lambda e, j: (e, 0, 0)),
                  pl.BlockSpec((None, d, f), lambda e, j: (e, 0, 0)),
                  pl.BlockSpec((None, f, d), lambda e, j: (e, 0, 0))],
        out_specs=blk,
        scratch_shapes=[pltpu.VMEM((d, f), BF16), pltpu.VMEM((d, f), BF16), pltpu.VMEM((f, d), BF16)],
        compiler_params=_cparams(("parallel", "arbitrary")),
        name="moe_experts",
    )(hc, wg, wu, wd)


def _combine_kernel(rnd_ref, route_ref, yc_ref, x_ref, g2_ref, o_ref):
    n_e, cap, d = yc_ref.shape
    tm = x_ref.shape[0]
    rnd = rnd_ref[0].astype(F32)
    r = route_ref[...]
    yc = yc_ref[...].reshape(n_e * cap, d)
    scol = lax.broadcasted_iota(jnp.int32, (tm, n_e * cap), 1).astype(F32)
    acc = jnp.zeros((tm, d), F32)
    for k in range(2):
        ck = _slot_index(r[:, k:k + 1], r[:, 2 + k:3 + k], rnd, cap)
        sel = jnp.where(scol == ck, 1.0, 0.0).astype(BF16)
        acc = acc + r[:, 4 + k:5 + k] * jnp.dot(sel, yc, preferred_element_type=F32)
    o_ref[...] = x_ref[...] + g2_ref[...] * acc


def _combine(rnd, route_t, yc, x2, g2, seq):
    t, d = x2.shape
    n_tiles, n_e, cap, _ = yc.shape
    tm = ROW_TILE
    per_b = seq // tm
    return pl.pallas_call(
        _combine_kernel,
        out_shape=jax.ShapeDtypeStruct((t, d), F32),
        grid_spec=pltpu.PrefetchScalarGridSpec(
            num_scalar_prefetch=1,
            grid=(n_tiles,),
            in_specs=[pl.BlockSpec((tm, 8), lambda i, rnd: (i, 0)),
                      pl.BlockSpec((None, n_e, cap, d), lambda i, rnd: (i, 0, 0, 0)),
                      pl.BlockSpec((tm, d), lambda i, rnd: (i, 0)),
                      pl.BlockSpec((None, 1, d), lambda i, rnd: (i // per_b, 0, 0))],
            out_specs=pl.BlockSpec((tm, d), lambda i, rnd: (i, 0)),
        ),
        compiler_params=_cparams(("parallel",)),
        name="moe_combine",
    )(rnd, route_t, yc, x2, g2)


def _moe(h, route, wg, wu, wd, x2, g2, seq):
    cap = MOE_SLOTS
    n_e = wg.shape[0]
    route_t = route.T
    n_rounds = (jnp.max(route[2:4]) // cap).astype(jnp.int32) + 1

    def one_round(r, x_acc):
        rnd = jnp.full((1,), r, jnp.int32)
        yc = _experts(_dispatch(rnd, route, h, cap, n_e), wg, wu, wd)
        return _combine(rnd, route_t, yc, x_acc, g2, seq)

    return lax.fori_loop(0, n_rounds, one_round, x2)


def kernel(x, c, ada_w, ada_b, norm_mix_g, norm_ffn_g, t5_bias, a_w_qkv, a_q_gain, a_k_gain, a_lambda, a_subln_g, a_w_o, b_w_qkv, b_q_gain, b_k_gain, b_rel_bias, b_w_o, router_w, router_bias, moe_w_gate, moe_w_up, moe_w_down):
    batch, seq, d = x.shape
    depth = ada_w.shape[0]
    assert seq % ATTN_TILE == 0 and seq % ROW_TILE == 0 and d == A_HEADS * 2 * A_HEAD_DIM
    assert (batch * seq // ROW_TILE) % MOE_TILES_PER_STEP == 0
    assert d == B_HEADS * B_HEAD_DIM and A_HEAD_DIM == B_HEAD_DIM

    c_pad = jnp.zeros((8, d), F32).at[:batch].set(c.astype(F32))
    mod = _modulation(c_pad, ada_w.astype(F32), ada_b.astype(F32))[:, :batch]
    mod = mod.reshape(depth, batch, 6, 1, d)

    rwt = router_w.astype(F32).T
    rb = router_bias.astype(F32).reshape(-1, 1)

    x2 = x.astype(F32).reshape(batch * seq, d)
    for i in range(depth):
        sh1, sc1, g1, sh2, sc2, g2 = [mod[i, :, k] for k in range(6)]
        j = i // 2
        if i % 2 == 0:
            w_qkv, qg, kg, w_o = a_w_qkv[j], a_q_gain[j], a_k_gain[j], a_w_o[j]
        else:
            w_qkv, qg, kg, w_o = b_w_qkv[j], b_q_gain[j], b_k_gain[j], b_w_o[j]
        n_rep = d // qg.shape[0]
        gq = (jnp.tile(qg.astype(F32), n_rep) * (A_HEAD_DIM ** -0.5 * LOG2E)).reshape(d, 1)
        gk = jnp.tile(kg.astype(F32), n_rep).reshape(1, d)
        wb = w_qkv.astype(BF16)
        qt, k, vt = _qkv_proj(x2, norm_mix_g[i].reshape(1, d), sc1, sh1,
                              wb[:, :d].T, wb[:, d:2 * d], wb[:, 2 * d:].T, gq, gk, seq, A_HEAD_DIM)
        if i % 2 == 0:
            lambda_init = 0.8 - 0.6 * math.exp(-0.3 * i)
            o = _attention_a(qt, k, vt, _t5_vectors(t5_bias, ATTN_TILE), a_lambda[j].astype(F32),
                             a_subln_g[j].reshape(1, -1), batch, seq, lambda_init)
        else:
            o = _attention_b(qt, k, vt, _band_vectors(b_rel_bias[j], ATTN_TILE), batch, seq)
        x2, h, route = _out_proj(o, w_o.astype(BF16), x2, g1, norm_ffn_g[i].reshape(1, d),
                                 sc2, sh2, rwt, rb, seq)
        x2 = _moe(h, route, moe_w_gate[i], moe_w_up[i], moe_w_down[i], x2, g2, seq)
    return x2.reshape(batch, seq, d)
```

```python
import functools
import math

import numpy as np
import jax
import jax.numpy as jnp
from jax import lax
from jax.experimental import pallas as pl
from jax.experimental.pallas import tpu as pltpu

F32 = jnp.float32
BF16 = jnp.bfloat16

CHUNK = 64
A_HEADS = 8
A_HEAD_DIM = 64
T5_BUCKETS = 32
T5_MAX_DIST = 1024
B_HEADS = 16
B_HEAD_DIM = 64
LEFT_CHUNKS = 8
MAX_REL = 256
N_EXPERTS = 16
N_GROUPS = 4
E_PER_GROUP = N_EXPERTS // N_GROUPS
NORM_EPS = 1e-6
NEG_INF = -1e30
LOG2E = math.log2(math.e)

V7X_LANES = 128
V7X_MXU_DIM = 256

ATTN_TILE = LEFT_CHUNKS * CHUNK
ROW_TILE = 512
MOE_CHUNK = 16
MOE_FFN_TILE = 256
VMEM_LIMIT = 56 * 1024 * 1024

_NT = (((1,), (1,)), ((), ()))


def _cparams(sem):
    return pltpu.CompilerParams(dimension_semantics=sem, vmem_limit_bytes=VMEM_LIMIT)


def _mod_kernel(c_ref, w_ref, b_ref, o_ref):
    c = c_ref[...]
    s = c * (1.0 / (1.0 + jnp.exp(-c)))
    o_ref[...] = jnp.dot(s, w_ref[...], preferred_element_type=F32,
                         precision=lax.Precision.HIGHEST) + b_ref[...]


def _modulation(c_pad, ada_w, ada_b):
    depth, d, n = ada_w.shape
    rows = c_pad.shape[0]
    tn = 1536
    return pl.pallas_call(
        _mod_kernel,
        out_shape=jax.ShapeDtypeStruct((depth, rows, n), F32),
        grid=(depth, n // tn),
        in_specs=[
            pl.BlockSpec((rows, d), lambda i, j: (0, 0)),
            pl.BlockSpec((None, d, tn), lambda i, j: (i, 0, j)),
            pl.BlockSpec((None, 1, tn), lambda i, j: (i, 0, j)),
        ],
        out_specs=pl.BlockSpec((None, rows, tn), lambda i, j: (i, 0, j)),
        compiler_params=_cparams(("parallel", "parallel")),
        name="adaln_mod",
    )(c_pad, ada_w, ada_b.reshape(depth, 1, n))


def _norm_mod(x, g, sc, sh):
    ms = jnp.mean(x * x, axis=-1, keepdims=True)
    return x * lax.rsqrt(ms + NORM_EPS) * g * (1.0 + sc) + sh


def _qkv_kernel(x_ref, g_ref, sc_ref, sh_ref, wqt_ref, wk_ref, wvt_ref, gq_ref, gk_ref,
                qt_ref, k_ref, vt_ref, *, head_dim):
    h = _norm_mod(x_ref[...], g_ref[...], sc_ref[...], sh_ref[...]).astype(BF16)
    d, tm = qt_ref.shape
    y = lax.dot_general(wqt_ref[...], h, _NT, preferred_element_type=F32)
    y3 = y.reshape(d // head_dim, head_dim, tm)
    ss = jnp.mean(y3 * y3, axis=1, keepdims=True)
    qt_ref[...] = ((y3 * lax.rsqrt(ss + NORM_EPS)).reshape(d, tm) * gq_ref[...]).astype(BF16)
    cw = V7X_MXU_DIM
    r = lax.broadcasted_iota(jnp.int32, (cw, cw), 0) // head_dim
    c = lax.broadcasted_iota(jnp.int32, (cw, cw), 1) // head_dim
    gmat = jnp.where(r == c, 1.0 / head_dim, 0.0).astype(BF16)
    for ci in range(d // cw):
        cols = slice(ci * cw, (ci + 1) * cw)
        y = jnp.dot(h, wk_ref[:, cols], preferred_element_type=F32)
        ss = jnp.dot((y * y).astype(BF16), gmat, preferred_element_type=F32)
        k_ref[:, cols] = (y * lax.rsqrt(ss + NORM_EPS) * gk_ref[:, cols]).astype(BF16)
    vt_ref[...] = lax.dot_general(wvt_ref[...], h, _NT, preferred_element_type=F32).astype(BF16)


def _qkv_proj(x2, g, sc, sh, wqt, wk, wvt, gq, gk, seq, head_dim):
    t, d = x2.shape
    tm = ROW_TILE
    per_b = seq // tm
    vec = lambda i: (i // per_b, 0, 0)
    full = lambda i: (0, 0)
    return pl.pallas_call(
        functools.partial(_qkv_kernel, head_dim=head_dim),
        out_shape=(jax.ShapeDtypeStruct((d, t), BF16),
                   jax.ShapeDtypeStruct((t, d), BF16),
                   jax.ShapeDtypeStruct((d, t), BF16)),
        grid=(t // tm,),
        in_specs=[
            pl.BlockSpec((tm, d), lambda i: (i, 0)),
            pl.BlockSpec((1, d), full),
            pl.BlockSpec((None, 1, d), vec),
            pl.BlockSpec((None, 1, d), vec),
            pl.BlockSpec((d, d), full),
            pl.BlockSpec((d, d), full),
            pl.BlockSpec((d, d), full),
            pl.BlockSpec((d, 1), full),
            pl.BlockSpec((1, d), full),
        ],
        out_specs=(pl.BlockSpec((d, tm), lambda i: (0, i)),
                   pl.BlockSpec((tm, d), lambda i: (i, 0)),
                   pl.BlockSpec((d, tm), lambda i: (0, i))),
        compiler_params=_cparams(("parallel",)),
        name="qkv_proj",
    )(x2, g, sc, sh, wqt, wk, wvt, gq, gk)


def _softmax_pv(s, vt1, m_ref, acc_ref):
    m_prev = m_ref[...]
    m_new = jnp.maximum(m_prev, jnp.max(s, axis=0, keepdims=True))
    alpha = jnp.exp2(m_prev - m_new)
    p = jnp.exp2(s - m_new).astype(BF16)
    acc_ref[...] = alpha * acc_ref[...] + jnp.dot(vt1, p, preferred_element_type=F32)
    m_ref[...] = m_new


class _TilePipe:
    def __init__(self, k_ref, vt_ref, qa, qb, bufs, state, tk):
        self.k_ref, self.vt_ref, self.qa, self.qb = k_ref, vt_ref, qa, qb
        self.bufs, self.state, self.tk = bufs, state, tk

    def scores(self, j, parity):
        ks = pl.multiple_of(j * self.tk, self.tk)
        k = self.k_ref[pl.ds(ks, self.tk), :]
        buf = self.bufs[parity]
        buf[0] = jnp.dot(k, self.qa, preferred_element_type=F32)
        buf[1] = jnp.dot(k, self.qb, preferred_element_type=F32)

    def consume(self, j, parity, bias_a=None, bias_b=None):
        ks = pl.multiple_of(j * self.tk, self.tk)
        vt = self.vt_ref[:, pl.ds(ks, self.tk)]
        vt1 = jnp.concatenate([vt, jnp.ones((ONES_ROWS, self.tk), vt.dtype)], axis=0)
        buf = self.bufs[parity]
        (ma, acca), (mb, accb) = self.state
        sa, sb = buf[0], buf[1]
        _softmax_pv(sa if bias_a is None else sa + bias_a, vt1, ma, acca)
        _softmax_pv(sb if bias_b is None else sb + bias_b, vt1, mb, accb)

    def step(self, j, parity, bias_a=None, bias_b=None, last=False):
        if not last:
            self.scores(j + 1, 1 - parity)
        self.consume(j, parity, bias_a, bias_b)


def _split_maps(qt, head_dim):
    row = lax.broadcasted_iota(jnp.int32, qt.shape, 0)
    zero = jnp.zeros_like(qt)
    return jnp.where(row < head_dim, qt, zero), jnp.where(row >= head_dim, qt, zero)


ONES_ROWS = 16


def _init_state(refs):
    for m_ref, acc_ref in refs:
        m_ref[...] = jnp.full(m_ref.shape, NEG_INF, F32)
        acc_ref[...] = jnp.zeros(acc_ref.shape, F32)


def _normalized(acc_ref):
    acc = acc_ref[...]
    return acc[:V7X_LANES] / acc[V7X_LANES:V7X_LANES + 1]


def _toeplitz(x_row, n):
    x = jnp.broadcast_to(x_row, (n, x_row.shape[1]))
    return pltpu.roll(x, 0, 1, stride=1, stride_axis=0)[:, :n]


def _chunk_delta(n):
    kc = lax.broadcasted_iota(jnp.int32, (n, n), 0) // CHUNK
    qc = lax.broadcasted_iota(jnp.int32, (n, n), 1) // CHUNK
    return qc - kc


def _attn_a_kernel(qt_ref, k_ref, vt_ref, x_ref, lam_ref, sub_ref, o_ref,
                   tab, buf0, buf1, ma, acca, mb, accb, *, n_off, lambda_init):
    tq = qt_ref.shape[1]
    qi = pl.program_id(2)

    @pl.when(qi == 0)
    def _():
        for o in range(n_off):
            t = _toeplitz(x_ref[o], tq) * LOG2E
            if o == 0:
                t = jnp.where(_chunk_delta(tq) >= 0, t, NEG_INF)
            tab[o] = t

    qa, qb = _split_maps(qt_ref[...], A_HEAD_DIM)
    _init_state(((ma, acca), (mb, accb)))

    pipe = _TilePipe(k_ref, vt_ref, qa, qb, (buf0, buf1), ((ma, acca), (mb, accb)), tq)
    def near_steps(first):
        for o in range(first, -1, -1):
            pipe.step(qi - o, o % 2, tab[o], tab[o], last=(o == 0))

    for q0 in range(n_off - 1):
        @pl.when(qi == q0)
        def _():
            pipe.scores(0, q0 % 2)
            near_steps(q0)

    n_far = jnp.maximum(qi - (n_off - 1), 0)
    odd = n_far % 2
    for par in range(2):
        @pl.when((qi >= n_off - 1) & (qi % 2 == par))
        def _():
            pipe.scores(0, par)

    @pl.when(odd == 1)
    def _():
        pipe.step(0, n_off % 2)

    def far_pair(t, carry):
        j = 2 * t + odd
        pipe.step(j, (n_off - 1) % 2)
        pipe.step(j + 1, n_off % 2)
        return carry
    lax.fori_loop(0, n_far // 2, far_pair, 0)

    @pl.when(qi >= n_off - 1)
    def _():
        near_steps(n_off - 1)

    lam = lam_ref[...]
    lam_full = (jnp.exp(jnp.sum(lam[0:1] * lam[1:2], axis=-1, keepdims=True))
                - jnp.exp(jnp.sum(lam[2:3] * lam[3:4], axis=-1, keepdims=True)) + lambda_init)
    a = _normalized(acca) - lam_full * _normalized(accb)
    ms = jnp.mean(a * a, axis=0, keepdims=True)
    an = a * lax.rsqrt(ms + NORM_EPS)
    o_ref[...] = (an.T * (sub_ref[...] * (1.0 - lambda_init))).astype(BF16)


def _attn_b_kernel(qt_ref, k_ref, vt_ref, x_ref, o_ref, tab, buf0, buf1, ma, acca, mb, accb):
    tq = qt_ref.shape[1]
    qi = pl.program_id(2)

    @pl.when(qi == 0)
    def _():
        for m in range(2):
            for o in range(2):
                d = _chunk_delta(tq) + o * (tq // CHUNK)
                t = _toeplitz(x_ref[2 * m + o], tq) * LOG2E
                tab[2 * m + o] = jnp.where((d >= 0) & (d <= LEFT_CHUNKS), t, NEG_INF)

    qa, qb = _split_maps(qt_ref[...], B_HEAD_DIM)
    _init_state(((ma, acca), (mb, accb)))

    pipe = _TilePipe(k_ref, vt_ref, qa, qb, (buf0, buf1), ((ma, acca), (mb, accb)), tq)

    @pl.when(qi == 0)
    def _():
        pipe.scores(0, 0)
        pipe.step(0, 0, tab[0], tab[2], last=True)

    @pl.when(qi >= 1)
    def _():
        pipe.scores(qi - 1, 1)
        pipe.step(qi - 1, 1, tab[1], tab[3])
        pipe.step(qi, 0, tab[0], tab[2], last=True)

    oa, ob = _normalized(acca), _normalized(accb)
    row = lax.broadcasted_iota(jnp.int32, oa.shape, 0)
    o_ref[...] = jnp.where(row < B_HEAD_DIM, oa, ob).T.astype(BF16)


def _attn_scratch(n_tab, tq):
    one = [pltpu.VMEM((1, tq), F32), pltpu.VMEM((V7X_LANES + ONES_ROWS, tq), F32)]
    score_buf = pltpu.VMEM((2, tq, tq), F32)
    return [pltpu.VMEM((n_tab, tq, tq), F32), score_buf, score_buf] + one + one


def _attn_specs(batch, seq, nq, tq):
    return dict(
        q=pl.BlockSpec((V7X_LANES, tq), lambda b, h, i: (h, b * nq + i)),
        k=pl.BlockSpec((seq, V7X_LANES), lambda b, h, i: (b, h)),
        v=pl.BlockSpec((V7X_LANES, seq), lambda b, h, i: (h, b)),
        o=pl.BlockSpec((tq, V7X_LANES), lambda b, h, i: (b * nq + i, h)),
    )


def _attention_a(qt, k, vt, xvec, lam, sub_g, batch, seq, lambda_init):
    d, t = qt.shape
    tq = ATTN_TILE
    nq = seq // tq
    n_off = xvec.shape[1]
    sp = _attn_specs(batch, seq, nq, tq)
    return pl.pallas_call(
        functools.partial(_attn_a_kernel, n_off=n_off, lambda_init=lambda_init),
        out_shape=jax.ShapeDtypeStruct((t, d), BF16),
        grid=(batch, d // V7X_LANES, nq),
        in_specs=[sp["q"], sp["k"], sp["v"],
                  pl.BlockSpec((None,) + xvec.shape[1:], lambda b, h, i: (h, 0, 0, 0)),
                  pl.BlockSpec(lam.shape, lambda b, h, i: (0, 0)),
                  pl.BlockSpec(sub_g.shape, lambda b, h, i: (0, 0))],
        out_specs=sp["o"],
        scratch_shapes=_attn_scratch(n_off, tq),
        compiler_params=_cparams(("parallel", "parallel", "arbitrary")),
        name="diff_attention",
    )(qt, k, vt, xvec, lam, sub_g)


def _attention_b(qt, k, vt, xvec, batch, seq):
    d, t = qt.shape
    tq = ATTN_TILE
    nq = seq // tq
    sp = _attn_specs(batch, seq, nq, tq)
    return pl.pallas_call(
        _attn_b_kernel,
        out_shape=jax.ShapeDtypeStruct((t, d), BF16),
        grid=(batch, d // V7X_LANES, nq),
        in_specs=[sp["q"], sp["k"], sp["v"],
                  pl.BlockSpec((None,) + xvec.shape[1:], lambda b, h, i: (h, 0, 0, 0))],
        out_specs=sp["o"],
        scratch_shapes=_attn_scratch(4, tq),
        compiler_params=_cparams(("parallel", "parallel", "arbitrary")),
        name="chunk_attention",
    )(qt, k, vt, xvec)


def _t5_bucket(rel):
    nb = T5_BUCKETS // 2
    ret = jnp.where(rel > 0, nb, 0)
    n = jnp.abs(rel)
    max_exact = nb // 2
    nf = jnp.maximum(n, 1).astype(F32)
    large = max_exact + (jnp.log(nf / max_exact) / math.log(T5_MAX_DIST / max_exact)
                         * (nb - max_exact)).astype(jnp.int32)
    large = jnp.minimum(large, nb - 1)
    return ret + jnp.where(n < max_exact, n, large)


def _t5_const_distance():
    nb = T5_BUCKETS // 2
    max_exact = nb // 2
    n = np.arange(max_exact, 4 * T5_MAX_DIST, dtype=np.float64)
    large = max_exact + np.floor(np.log(n / max_exact) / math.log(T5_MAX_DIST / max_exact) * (nb - max_exact))
    below = np.nonzero(large < nb - 1)[0]
    return int(n[below[-1]]) + 2


def _tile_rel(tile, n_off):
    i = jnp.arange(2 * tile, dtype=jnp.int32)
    rel = jnp.where(i < tile, -i, 2 * tile - i)
    return rel[None, :] - tile * jnp.arange(n_off, dtype=jnp.int32)[:, None]


def _t5_vectors(t5_bias, tile):
    n_off = 1
    while (n_off - 1) * tile + 1 < _t5_const_distance():
        n_off += 1
    tb = t5_bias.astype(F32)
    vals = tb[_t5_bucket(_tile_rel(tile, n_off))] - tb[T5_BUCKETS // 2 - 1]
    return vals.transpose(2, 0, 1)[:, :, None, :]


def _band_vectors(rel_bias, tile):
    idx = jnp.clip(_tile_rel(tile, 2), -MAX_REL, MAX_REL) + MAX_REL
    vals = rel_bias.astype(F32)[:, idx]
    return vals.reshape(rel_bias.shape[0] // 2, 4, 1, 2 * tile)


def _top2_sum4(r0, r1, r2, r3):
    a, b = jnp.maximum(r0, r1), jnp.minimum(r0, r1)
    c, d = jnp.maximum(r2, r3), jnp.minimum(r2, r3)
    return jnp.maximum(a, c) + jnp.maximum(jnp.minimum(a, c), jnp.maximum(b, d))


def _route(logits, rbias):
    n_e, n = logits.shape
    scores = 1.0 / (1.0 + jnp.exp(-logits))
    sel = scores + rbias
    row = lax.broadcasted_iota(jnp.int32, sel.shape, 0)
    best = None
    for g in range(N_GROUPS):
        rows = [sel[g * E_PER_GROUP + i: g * E_PER_GROUP + i + 1, :] for i in range(E_PER_GROUP)]
        gs = _top2_sum4(*rows)
        if best is None:
            best, gidx = gs, jnp.zeros(gs.shape, jnp.int32)
        else:
            gidx = jnp.where(gs > best, g, gidx)
            best = jnp.maximum(best, gs)
    masked = jnp.where(row // E_PER_GROUP == gidx, sel, NEG_INF)
    m1 = jnp.max(masked, axis=0, keepdims=True)
    i1 = jnp.min(jnp.where(masked == m1, row, n_e), axis=0, keepdims=True)
    masked2 = jnp.where(row == i1, -3.0e38, masked)
    m2 = jnp.max(masked2, axis=0, keepdims=True)
    i2 = jnp.min(jnp.where(masked2 == m2, row, n_e), axis=0, keepdims=True)
    w1 = jnp.sum(jnp.where(row == i1, scores, 0.0), axis=0, keepdims=True)
    w2 = jnp.sum(jnp.where(row == i2, scores, 0.0), axis=0, keepdims=True)
    den = w1 + w2
    member = jnp.where((row == i1) | (row == i2), 1.0, 0.0).astype(BF16)
    before = jnp.where(lax.broadcasted_iota(jnp.int32, (n, n), 0) < lax.broadcasted_iota(jnp.int32, (n, n), 1),
                       1.0, 0.0).astype(BF16)
    pos = jnp.dot(member, before, preferred_element_type=F32)
    p1 = jnp.sum(jnp.where(row == i1, pos, 0.0), axis=0, keepdims=True)
    p2 = jnp.sum(jnp.where(row == i2, pos, 0.0), axis=0, keepdims=True)
    out_row = lax.broadcasted_iota(jnp.int32, (8, n), 0)
    out = jnp.zeros((8, n), F32)
    for r, val in enumerate((i1.astype(F32), i2.astype(F32), p1, p2, w1 / den, w2 / den)):
        out = jnp.where(out_row == r, val, out)
    return out


def _oproj_kernel(o_ref, wo_ref, x_ref, g1_ref, gn_ref, sc_ref, sh_ref, rwt_ref, rb_ref,
                  xo_ref, h_ref, route_ref):
    y = jnp.dot(o_ref[...], wo_ref[...], preferred_element_type=F32)
    xn = x_ref[...] + g1_ref[...] * y
    xo_ref[...] = xn
    h = _norm_mod(xn, gn_ref[...], sc_ref[...], sh_ref[...])
    h_ref[...] = h.astype(BF16)
    logits = lax.dot_general(rwt_ref[...], h, _NT,
                             preferred_element_type=F32, precision=lax.Precision.HIGHEST)
    route_ref[...] = _route(logits, rb_ref[...])


def _out_proj(o, wo_bf16, x2, g1, gn, sc, sh, rwt, rb, seq):
    t, d = x2.shape
    tm = ROW_TILE
    per_b = seq // tm
    n_e = rwt.shape[0]
    vec = lambda i: (i // per_b, 0, 0)
    return pl.pallas_call(
        _oproj_kernel,
        out_shape=(jax.ShapeDtypeStruct((t, d), F32),
                   jax.ShapeDtypeStruct((t, d), BF16),
                   jax.ShapeDtypeStruct((8, t), F32)),
        grid=(t // tm,),
        in_specs=[
            pl.BlockSpec((tm, d), lambda i: (i, 0)),
            pl.BlockSpec((d, d), lambda i: (0, 0)),
            pl.BlockSpec((tm, d), lambda i: (i, 0)),
            pl.BlockSpec((None, 1, d), vec),
            pl.BlockSpec((1, d), lambda i: (0, 0)),
            pl.BlockSpec((None, 1, d), vec),
            pl.BlockSpec((None, 1, d), vec),
            pl.BlockSpec((n_e, d), lambda i: (0, 0)),
            pl.BlockSpec((n_e, 1), lambda i: (0, 0)),
        ],
        out_specs=(pl.BlockSpec((tm, d), lambda i: (i, 0)),
                   pl.BlockSpec((tm, d), lambda i: (i, 0)),
                   pl.BlockSpec((8, tm), lambda i: (0, i))),
        compiler_params=_cparams(("parallel",)),
        name="out_proj_route",
    )(o, wo_bf16, x2, g1, gn, sc, sh, rwt, rb)


def _moe_plan(route, t, tm, n_e):
    n_tiles = t // tm
    ch, ft = MOE_CHUNK, MOE_FFN_TILE
    loc_rows, main_rows = _moe_rows(t, tm, n_e)
    ids = jnp.arange(n_e, dtype=jnp.int32)
    e = route[0:2].astype(jnp.int32)
    pos = route[2:4].astype(jnp.int32)
    oh = (e[:, :, None] == ids).astype(jnp.int32)
    cnt = oh.sum(0).reshape(n_tiles, tm, n_e).sum(1)
    seg = (cnt + ch - 1) // ch * ch
    loc = jnp.cumsum(seg, axis=1) - seg
    tot = seg.sum(0)
    totp = (tot + ft - 1) // ft * ft
    base = jnp.cumsum(totp) - totp
    gdest = base[None, :] + jnp.cumsum(seg, axis=0) - seg
    col = (oh * jnp.repeat(loc, tm, axis=0)[None]).sum(-1) + pos

    row0 = jnp.arange(loc_rows // ch, dtype=jnp.int32) * ch
    ej = (row0[None, :, None] >= (loc + seg)[:, None, :]).sum(-1)
    ohj = (jnp.minimum(ej, n_e - 1)[..., None] == ids).astype(jnp.int32)
    dst = (ohj * (gdest - loc)[:, None, :]).sum(-1) + row0[None, :]
    dst = jnp.where(ej < n_e, dst, main_rows + row0[None, :]) // ch

    r0 = jnp.arange((main_rows + loc_rows) // ft, dtype=jnp.int32) * ft
    ends = base + totp
    te = jnp.minimum((r0[:, None] >= ends[None, :]).sum(-1), n_e - 1)
    tvalid = (r0 < ends[-1]).astype(jnp.int32)
    tfirst = tvalid * (r0 == (((te[:, None] == ids) * base[None, :]).sum(-1))).astype(jnp.int32)
    return col, dst.astype(jnp.int32), te.astype(jnp.int32), tvalid, tfirst


def _moe_rows(t, tm, n_e):
    ch, ft = MOE_CHUNK, MOE_FFN_TILE
    loc_rows = 2 * tm + n_e * ch
    main = 2 * t + (t // tm) * n_e * (ch - 1) + n_e * (ft - 1)
    main = (main + ft - 1) // ft * ft
    assert (main + loc_rows) % ft == 0
    return loc_rows, main


def _chunk_copies(dst_ref, tile, local_ref, sorted_ref, sem, to_sorted):
    ch = MOE_CHUNK
    copies = []
    for j in range(local_ref.shape[0] // ch):
        far = sorted_ref.at[pl.ds(pl.multiple_of(dst_ref[tile, j] * ch, ch), ch)]
        near = local_ref.at[pl.ds(j * ch, ch)]
        copies.append(pltpu.make_async_copy(near, far, sem) if to_sorted
                      else pltpu.make_async_copy(far, near, sem))
    return copies


def _dispatch_kernel(dst_ref, col_ref, h_ref, init_ref, xs_ref, comp, sem):
    del init_ref
    tile = pl.program_id(0)
    col = col_ref[...]
    srow = lax.broadcasted_iota(jnp.int32, (comp.shape[0], col.shape[1]), 0)
    sel = jnp.where((srow == col[0:1]) | (srow == col[1:2]), 1.0, 0.0).astype(BF16)
    comp[...] = jnp.dot(sel, h_ref[...], preferred_element_type=F32).astype(BF16)
    copies = _chunk_copies(dst_ref, tile, comp, xs_ref, sem, True)
    for cp in copies:
        cp.start()
    for cp in copies:
        cp.wait()


def _dispatch(dst, col, h, n_e):
    t, d = h.shape
    tm = ROW_TILE
    loc_rows, main_rows = _moe_rows(t, tm, n_e)
    n_sorted = main_rows + loc_rows
    return pl.pallas_call(
        _dispatch_kernel,
        out_shape=jax.ShapeDtypeStruct((n_sorted, d), BF16),
        grid_spec=pltpu.PrefetchScalarGridSpec(
            num_scalar_prefetch=1,
            grid=(t // tm,),
            in_specs=[pl.BlockSpec((2, tm), lambda i, dst: (0, i)),
                      pl.BlockSpec((tm, d), lambda i, dst: (i, 0)),
                      pl.BlockSpec(memory_space=pl.ANY)],
            out_specs=pl.BlockSpec(memory_space=pl.ANY),
            scratch_shapes=[pltpu.VMEM((loc_rows, d), BF16), pltpu.SemaphoreType.DMA(())],
        ),
        input_output_aliases={3: 0},
        compiler_params=_cparams(("arbitrary",)),
        name="moe_dispatch",
    )(dst, col, h, jnp.zeros((n_sorted, d), BF16))


def _expert_kernel(te_ref, tv_ref, tf_ref, xs_ref, wg_ref, wu_ref, wd_ref, ys_ref, wg_b, wu_b, wd_b):
    del te_ref
    s = pl.program_id(0)

    @pl.when(tf_ref[s] == 1)
    def _():
        wg_b[...] = wg_ref[...].astype(BF16)
        wu_b[...] = wu_ref[...].astype(BF16)
        wd_b[...] = wd_ref[...].astype(BF16)

    @pl.when(tv_ref[s] == 1)
    def _():
        xs = xs_ref[...]
        hg = jnp.dot(xs, wg_b[...], preferred_element_type=F32)
        hu = jnp.dot(xs, wu_b[...], preferred_element_type=F32)
        he = hg * (1.0 / (1.0 + jnp.exp(-hg))) * hu
        ys_ref[...] = jnp.dot(he.astype(BF16), wd_b[...], preferred_element_type=F32).astype(BF16)

    @pl.when(tv_ref[s] == 0)
    def _():
        ys_ref[...] = jnp.zeros(ys_ref.shape, ys_ref.dtype)


def _experts(te, tvalid, tfirst, xs, wg, wu, wd):
    n_sorted, d = xs.shape
    f = wg.shape[2]
    ft = MOE_FFN_TILE
    wspec = lambda r, c: pl.BlockSpec((None, r, c), lambda s, te, tv, tf: (te[s], 0, 0))
    return pl.pallas_call(
        _expert_kernel,
        out_shape=jax.ShapeDtypeStruct((n_sorted, d), BF16),
        grid_spec=pltpu.PrefetchScalarGridSpec(
            num_scalar_prefetch=3,
            grid=(n_sorted // ft,),
            in_specs=[pl.BlockSpec((ft, d), lambda s, te, tv, tf: (s, 0)),
                      wspec(d, f), wspec(d, f), wspec(f, d)],
            out_specs=pl.BlockSpec((ft, d), lambda s, te, tv, tf: (s, 0)),
            scratch_shapes=[pltpu.VMEM((d, f), BF16), pltpu.VMEM((d, f), BF16), pltpu.VMEM((f, d), BF16)],
        ),
        compiler_params=_cparams(("arbitrary",)),
        name="moe_experts",
    )(te, tvalid, tfirst, xs, wg, wu, wd)


def _combine_kernel(dst_ref, col_ref, w_ref, ys_ref, x_ref, g2_ref, o_ref, comp, sem):
    tile = pl.program_id(0)
    copies = _chunk_copies(dst_ref, tile, comp, ys_ref, sem, False)
    for cp in copies:
        cp.start()
    col = col_ref[...]
    w = w_ref[...]
    scol = lax.broadcasted_iota(jnp.int32, (col.shape[0], comp.shape[0]), 1)
    sels = [jnp.where(scol == col[:, k:k + 1], 1.0, 0.0).astype(BF16) for k in range(2)]
    for cp in copies:
        cp.wait()
    yc = comp[...]
    acc = w[:, 4:5] * jnp.dot(sels[0], yc, preferred_element_type=F32)
    acc = acc + w[:, 5:6] * jnp.dot(sels[1], yc, preferred_element_type=F32)
    o_ref[...] = x_ref[...] + g2_ref[...] * acc


def _combine(dst, col_t, route_t, ys, x2, g2, seq, n_e):
    t, d = x2.shape
    tm = ROW_TILE
    per_b = seq // tm
    loc_rows, _ = _moe_rows(t, tm, n_e)
    return pl.pallas_call(
        _combine_kernel,
        out_shape=jax.ShapeDtypeStruct((t, d), F32),
        grid_spec=pltpu.PrefetchScalarGridSpec(
            num_scalar_prefetch=1,
            grid=(t // tm,),
            in_specs=[pl.BlockSpec((tm, 2), lambda i, dst: (i, 0)),
                      pl.BlockSpec((tm, 8), lambda i, dst: (i, 0)),
                      pl.BlockSpec(memory_space=pl.ANY),
                      pl.BlockSpec((tm, d), lambda i, dst: (i, 0)),
                      pl.BlockSpec((None, 1, d), lambda i, dst: (i // per_b, 0, 0))],
            out_specs=pl.BlockSpec((tm, d), lambda i, dst: (i, 0)),
            scratch_shapes=[pltpu.VMEM((loc_rows, d), BF16), pltpu.SemaphoreType.DMA(())],
        ),
        compiler_params=_cparams(("arbitrary",)),
        name="moe_combine",
    )(dst, col_t, route_t, ys, x2, g2)


def _moe(h, route, wg, wu, wd, x2, g2, seq):
    t = x2.shape[0]
    n_e = wg.shape[0]
    col, dst, te, tvalid, tfirst = _moe_plan(route, t, ROW_TILE, n_e)
    xs = _dispatch(dst, col, h, n_e)
    ys = _experts(te, tvalid, tfirst, xs, wg, wu, wd)
    return _combine(dst, col.T, route.T, ys, x2, g2, seq, n_e)


def kernel(x, c, ada_w, ada_b, norm_mix_g, norm_ffn_g, t5_bias, a_w_qkv, a_q_gain, a_k_gain, a_lambda, a_subln_g, a_w_o, b_w_qkv, b_q_gain, b_k_gain, b_rel_bias, b_w_o, router_w, router_bias, moe_w_gate, moe_w_up, moe_w_down):
    batch, seq, d = x.shape
    depth = ada_w.shape[0]
    assert seq % ATTN_TILE == 0 and seq % ROW_TILE == 0 and d == A_HEADS * 2 * A_HEAD_DIM
    assert d == B_HEADS * B_HEAD_DIM and A_HEAD_DIM == B_HEAD_DIM

    c_pad = jnp.zeros((8, d), F32).at[:batch].set(c.astype(F32))
    mod = _modulation(c_pad, ada_w.astype(F32), ada_b.astype(F32))[:, :batch]
    mod = mod.reshape(depth, batch, 6, 1, d)

    rwt = router_w.astype(F32).T
    rb = router_bias.astype(F32).reshape(-1, 1)

    x2 = x.astype(F32).reshape(batch * seq, d)
    for i in range(depth):
        sh1, sc1, g1, sh2, sc2, g2 = [mod[i, :, k] for k in range(6)]
        j = i // 2
        if i % 2 == 0:
            w_qkv, qg, kg, w_o = a_w_qkv[j], a_q_gain[j], a_k_gain[j], a_w_o[j]
        else:
            w_qkv, qg, kg, w_o = b_w_qkv[j], b_q_gain[j], b_k_gain[j], b_w_o[j]
        n_rep = d // qg.shape[0]
        gq = (jnp.tile(qg.astype(F32), n_rep) * (A_HEAD_DIM ** -0.5 * LOG2E)).reshape(d, 1)
        gk = jnp.tile(kg.astype(F32), n_rep).reshape(1, d)
        wb = w_qkv.astype(BF16)
        qt, k, vt = _qkv_proj(x2, norm_mix_g[i].reshape(1, d), sc1, sh1,
                              wb[:, :d].T, wb[:, d:2 * d], wb[:, 2 * d:].T, gq, gk, seq, A_HEAD_DIM)
        if i % 2 == 0:
            lambda_init = 0.8 - 0.6 * math.exp(-0.3 * i)
            o = _attention_a(qt, k, vt, _t5_vectors(t5_bias, ATTN_TILE), a_lambda[j].astype(F32),
                             a_subln_g[j].reshape(1, -1), batch, seq, lambda_init)
        else:
            o = _attention_b(qt, k, vt, _band_vectors(b_rel_bias[j], ATTN_TILE), batch, seq)
        x2, h, route = _out_proj(o, w_o.astype(BF16), x2, g1, norm_ffn_g[i].reshape(1, d),
                                 sc2, sh2, rwt, rb, seq)
        x2 = _moe(h, route, moe_w_gate[i], moe_w_up[i], moe_w_down[i], x2, g2, seq)
    return x2.reshape(batch, seq, d)
```

```python
import functools
import math

import numpy as np
import jax
import jax.numpy as jnp
from jax import lax
from jax.experimental import pallas as pl
from jax.experimental.pallas import tpu as pltpu

F32 = jnp.float32
BF16 = jnp.bfloat16

CHUNK = 64
A_HEADS = 8
A_HEAD_DIM = 64
T5_BUCKETS = 32
T5_MAX_DIST = 1024
B_HEADS = 16
B_HEAD_DIM = 64
LEFT_CHUNKS = 8
MAX_REL = 256
N_EXPERTS = 16
N_GROUPS = 4
E_PER_GROUP = N_EXPERTS // N_GROUPS
NORM_EPS = 1e-6
NEG_INF = -1e30
LOG2E = math.log2(math.e)

V7X_LANES = 128
V7X_MXU_DIM = 256

ATTN_TILE = LEFT_CHUNKS * CHUNK
ROW_TILE = 512
MOE_CHUNK = 16
MOE_FFN_TILE = 512
VMEM_LIMIT = 56 * 1024 * 1024

_NT = (((1,), (1,)), ((), ()))


def _cparams(sem):
    return pltpu.CompilerParams(dimension_semantics=sem, vmem_limit_bytes=VMEM_LIMIT)


def _mod_kernel(c_ref, w_ref, b_ref, o_ref):
    c = c_ref[...]
    s = c * (1.0 / (1.0 + jnp.exp(-c)))
    o_ref[...] = jnp.dot(s, w_ref[...], preferred_element_type=F32,
                         precision=lax.Precision.HIGHEST) + b_ref[...]


def _modulation(c_pad, ada_w, ada_b):
    depth, d, n = ada_w.shape
    rows = c_pad.shape[0]
    tn = 1536
    return pl.pallas_call(
        _mod_kernel,
        out_shape=jax.ShapeDtypeStruct((depth, rows, n), F32),
        grid=(depth, n // tn),
        in_specs=[
            pl.BlockSpec((rows, d), lambda i, j: (0, 0)),
            pl.BlockSpec((None, d, tn), lambda i, j: (i, 0, j)),
            pl.BlockSpec((None, 1, tn), lambda i, j: (i, 0, j)),
        ],
        out_specs=pl.BlockSpec((None, rows, tn), lambda i, j: (i, 0, j)),
        compiler_params=_cparams(("parallel", "parallel")),
        name="adaln_mod",
    )(c_pad, ada_w, ada_b.reshape(depth, 1, n))


def _norm_mod(x, g, sc, sh):
    ms = jnp.mean(x * x, axis=-1, keepdims=True)
    return x * lax.rsqrt(ms + NORM_EPS) * g * (1.0 + sc) + sh


def _qkv_kernel(x_ref, g_ref, sc_ref, sh_ref, wqt_ref, wk_ref, wvt_ref, gq_ref, gk_ref,
                qt_ref, k_ref, vt_ref, *, head_dim):
    h = _norm_mod(x_ref[...], g_ref[...], sc_ref[...], sh_ref[...]).astype(BF16)
    d, tm = qt_ref.shape
    y = lax.dot_general(wqt_ref[...], h, _NT, preferred_element_type=F32)
    y3 = y.reshape(d // head_dim, head_dim, tm)
    ss = jnp.mean(y3 * y3, axis=1, keepdims=True)
    qt_ref[...] = ((y3 * lax.rsqrt(ss + NORM_EPS)).reshape(d, tm) * gq_ref[...]).astype(BF16)
    cw = V7X_MXU_DIM
    r = lax.broadcasted_iota(jnp.int32, (cw, cw), 0) // head_dim
    c = lax.broadcasted_iota(jnp.int32, (cw, cw), 1) // head_dim
    gmat = jnp.where(r == c, 1.0 / head_dim, 0.0).astype(BF16)
    for ci in range(d // cw):
        cols = slice(ci * cw, (ci + 1) * cw)
        y = jnp.dot(h, wk_ref[:, cols], preferred_element_type=F32)
        ss = jnp.dot((y * y).astype(BF16), gmat, preferred_element_type=F32)
        k_ref[:, cols] = (y * lax.rsqrt(ss + NORM_EPS) * gk_ref[:, cols]).astype(BF16)
    vt_ref[...] = lax.dot_general(wvt_ref[...], h, _NT, preferred_element_type=F32).astype(BF16)


def _qkv_proj(x2, g, sc, sh, wqt, wk, wvt, gq, gk, seq, head_dim):
    t, d = x2.shape
    tm = ROW_TILE
    per_b = seq // tm
    vec = lambda i: (i // per_b, 0, 0)
    full = lambda i: (0, 0)
    return pl.pallas_call(
        functools.partial(_qkv_kernel, head_dim=head_dim),
        out_shape=(jax.ShapeDtypeStruct((d, t), BF16),
                   jax.ShapeDtypeStruct((t, d), BF16),
                   jax.ShapeDtypeStruct((d, t), BF16)),
        grid=(t // tm,),
        in_specs=[
            pl.BlockSpec((tm, d), lambda i: (i, 0)),
            pl.BlockSpec((1, d), full),
            pl.BlockSpec((None, 1, d), vec),
            pl.BlockSpec((None, 1, d), vec),
            pl.BlockSpec((d, d), full),
            pl.BlockSpec((d, d), full),
            pl.BlockSpec((d, d), full),
            pl.BlockSpec((d, 1), full),
            pl.BlockSpec((1, d), full),
        ],
        out_specs=(pl.BlockSpec((d, tm), lambda i: (0, i)),
                   pl.BlockSpec((tm, d), lambda i: (i, 0)),
                   pl.BlockSpec((d, tm), lambda i: (0, i))),
        compiler_params=_cparams(("parallel",)),
        name="qkv_proj",
    )(x2, g, sc, sh, wqt, wk, wvt, gq, gk)


def _softmax_pv(s, vt1, m_ref, acc_ref):
    m_prev = m_ref[...]
    m_new = jnp.maximum(m_prev, jnp.max(s, axis=0, keepdims=True))
    alpha = jnp.exp2(m_prev - m_new)
    p = jnp.exp2(s - m_new).astype(BF16)
    acc_ref[...] = alpha * acc_ref[...] + jnp.dot(vt1, p, preferred_element_type=F32)
    m_ref[...] = m_new


class _TilePipe:
    def __init__(self, k_ref, vt_ref, qa, qb, bufs, state, tk):
        self.k_ref, self.vt_ref, self.qa, self.qb = k_ref, vt_ref, qa, qb
        self.bufs, self.state, self.tk = bufs, state, tk

    def scores(self, j, parity):
        ks = pl.multiple_of(j * self.tk, self.tk)
        k = self.k_ref[pl.ds(ks, self.tk), :]
        buf = self.bufs[parity]
        buf[0] = jnp.dot(k, self.qa, preferred_element_type=F32)
        buf[1] = jnp.dot(k, self.qb, preferred_element_type=F32)

    def consume(self, j, parity, bias_a=None, bias_b=None):
        ks = pl.multiple_of(j * self.tk, self.tk)
        vt = self.vt_ref[:, pl.ds(ks, self.tk)]
        vt1 = jnp.concatenate([vt, jnp.ones((ONES_ROWS, self.tk), vt.dtype)], axis=0)
        buf = self.bufs[parity]
        (ma, acca), (mb, accb) = self.state
        sa, sb = buf[0], buf[1]
        _softmax_pv(sa if bias_a is None else sa + bias_a, vt1, ma, acca)
        _softmax_pv(sb if bias_b is None else sb + bias_b, vt1, mb, accb)

    def step(self, j, parity, bias_a=None, bias_b=None, last=False):
        if not last:
            self.scores(j + 1, 1 - parity)
        self.consume(j, parity, bias_a, bias_b)


def _split_maps(qt, head_dim):
    row = lax.broadcasted_iota(jnp.int32, qt.shape, 0)
    zero = jnp.zeros_like(qt)
    return jnp.where(row < head_dim, qt, zero), jnp.where(row >= head_dim, qt, zero)


ONES_ROWS = 16


def _init_state(refs):
    for m_ref, acc_ref in refs:
        m_ref[...] = jnp.full(m_ref.shape, NEG_INF, F32)
        acc_ref[...] = jnp.zeros(acc_ref.shape, F32)


def _normalized(acc_ref):
    acc = acc_ref[...]
    return acc[:V7X_LANES] / acc[V7X_LANES:V7X_LANES + 1]


def _toeplitz(x_row, n):
    x = jnp.broadcast_to(x_row, (n, x_row.shape[1]))
    return pltpu.roll(x, 0, 1, stride=1, stride_axis=0)[:, :n]


def _chunk_delta(n):
    kc = lax.broadcasted_iota(jnp.int32, (n, n), 0) // CHUNK
    qc = lax.broadcasted_iota(jnp.int32, (n, n), 1) // CHUNK
    return qc - kc


def _attn_a_kernel(qt_ref, k_ref, vt_ref, x_ref, lam_ref, sub_ref, o_ref,
                   tab, buf0, buf1, ma, acca, mb, accb, *, n_off, lambda_init):
    tq = qt_ref.shape[1]
    qi = pl.program_id(2)

    @pl.when(qi == 0)
    def _():
        for o in range(n_off):
            t = _toeplitz(x_ref[o], tq) * LOG2E
            if o == 0:
                t = jnp.where(_chunk_delta(tq) >= 0, t, NEG_INF)
            tab[o] = t

    qa, qb = _split_maps(qt_ref[...], A_HEAD_DIM)
    _init_state(((ma, acca), (mb, accb)))

    pipe = _TilePipe(k_ref, vt_ref, qa, qb, (buf0, buf1), ((ma, acca), (mb, accb)), tq)
    def near_steps(first):
        for o in range(first, -1, -1):
            pipe.step(qi - o, o % 2, tab[o], tab[o], last=(o == 0))

    for q0 in range(n_off - 1):
        @pl.when(qi == q0)
        def _():
            pipe.scores(0, q0 % 2)
            near_steps(q0)

    n_far = jnp.maximum(qi - (n_off - 1), 0)
    odd = n_far % 2
    for par in range(2):
        @pl.when((qi >= n_off - 1) & (qi % 2 == par))
        def _():
            pipe.scores(0, par)

    @pl.when(odd == 1)
    def _():
        pipe.step(0, n_off % 2)

    def far_pair(t, carry):
        j = 2 * t + odd
        pipe.step(j, (n_off - 1) % 2)
        pipe.step(j + 1, n_off % 2)
        return carry
    lax.fori_loop(0, n_far // 2, far_pair, 0)

    @pl.when(qi >= n_off - 1)
    def _():
        near_steps(n_off - 1)

    lam = lam_ref[...]
    lam_full = (jnp.exp(jnp.sum(lam[0:1] * lam[1:2], axis=-1, keepdims=True))
                - jnp.exp(jnp.sum(lam[2:3] * lam[3:4], axis=-1, keepdims=True)) + lambda_init)
    a = _normalized(acca) - lam_full * _normalized(accb)
    ms = jnp.mean(a * a, axis=0, keepdims=True)
    an = a * lax.rsqrt(ms + NORM_EPS)
    o_ref[...] = (an.T * (sub_ref[...] * (1.0 - lambda_init))).astype(BF16)


def _attn_b_kernel(qt_ref, k_ref, vt_ref, x_ref, o_ref, tab, buf0, buf1, ma, acca, mb, accb):
    tq = qt_ref.shape[1]
    qi = pl.program_id(2)

    @pl.when(qi == 0)
    def _():
        for m in range(2):
            for o in range(2):
                d = _chunk_delta(tq) + o * (tq // CHUNK)
                t = _toeplitz(x_ref[2 * m + o], tq) * LOG2E
                tab[2 * m + o] = jnp.where((d >= 0) & (d <= LEFT_CHUNKS), t, NEG_INF)

    qa, qb = _split_maps(qt_ref[...], B_HEAD_DIM)
    _init_state(((ma, acca), (mb, accb)))

    pipe = _TilePipe(k_ref, vt_ref, qa, qb, (buf0, buf1), ((ma, acca), (mb, accb)), tq)

    @pl.when(qi == 0)
    def _():
        pipe.scores(0, 0)
        pipe.step(0, 0, tab[0], tab[2], last=True)

    @pl.when(qi >= 1)
    def _():
        pipe.scores(qi - 1, 1)
        pipe.step(qi - 1, 1, tab[1], tab[3])
        pipe.step(qi, 0, tab[0], tab[2], last=True)

    oa, ob = _normalized(acca), _normalized(accb)
    row = lax.broadcasted_iota(jnp.int32, oa.shape, 0)
    o_ref[...] = jnp.where(row < B_HEAD_DIM, oa, ob).T.astype(BF16)


def _attn_scratch(n_tab, tq):
    one = [pltpu.VMEM((1, tq), F32), pltpu.VMEM((V7X_LANES + ONES_ROWS, tq), F32)]
    score_buf = pltpu.VMEM((2, tq, tq), F32)
    return [pltpu.VMEM((n_tab, tq, tq), F32), score_buf, score_buf] + one + one


def _attn_specs(batch, seq, nq, tq):
    return dict(
        q=pl.BlockSpec((V7X_LANES, tq), lambda b, h, i: (h, b * nq + i)),
        k=pl.BlockSpec((seq, V7X_LANES), lambda b, h, i: (b, h)),
        v=pl.BlockSpec((V7X_LANES, seq), lambda b, h, i: (h, b)),
        o=pl.BlockSpec((tq, V7X_LANES), lambda b, h, i: (b * nq + i, h)),
    )


def _attention_a(qt, k, vt, xvec, lam, sub_g, batch, seq, lambda_init):
    d, t = qt.shape
    tq = ATTN_TILE
    nq = seq // tq
    n_off = xvec.shape[1]
    sp = _attn_specs(batch, seq, nq, tq)
    return pl.pallas_call(
        functools.partial(_attn_a_kernel, n_off=n_off, lambda_init=lambda_init),
        out_shape=jax.ShapeDtypeStruct((t, d), BF16),
        grid=(batch, d // V7X_LANES, nq),
        in_specs=[sp["q"], sp["k"], sp["v"],
                  pl.BlockSpec((None,) + xvec.shape[1:], lambda b, h, i: (h, 0, 0, 0)),
                  pl.BlockSpec(lam.shape, lambda b, h, i: (0, 0)),
                  pl.BlockSpec(sub_g.shape, lambda b, h, i: (0, 0))],
        out_specs=sp["o"],
        scratch_shapes=_attn_scratch(n_off, tq),
        compiler_params=_cparams(("parallel", "parallel", "arbitrary")),
        name="diff_attention",
    )(qt, k, vt, xvec, lam, sub_g)


def _attention_b(qt, k, vt, xvec, batch, seq):
    d, t = qt.shape
    tq = ATTN_TILE
    nq = seq // tq
    sp = _attn_specs(batch, seq, nq, tq)
    return pl.pallas_call(
        _attn_b_kernel,
        out_shape=jax.ShapeDtypeStruct((t, d), BF16),
        grid=(batch, d // V7X_LANES, nq),
        in_specs=[sp["q"], sp["k"], sp["v"],
                  pl.BlockSpec((None,) + xvec.shape[1:], lambda b, h, i: (h, 0, 0, 0))],
        out_specs=sp["o"],
        scratch_shapes=_attn_scratch(4, tq),
        compiler_params=_cparams(("parallel", "parallel", "arbitrary")),
        name="chunk_attention",
    )(qt, k, vt, xvec)


def _t5_bucket(rel):
    nb = T5_BUCKETS // 2
    ret = jnp.where(rel > 0, nb, 0)
    n = jnp.abs(rel)
    max_exact = nb // 2
    nf = jnp.maximum(n, 1).astype(F32)
    large = max_exact + (jnp.log(nf / max_exact) / math.log(T5_MAX_DIST / max_exact)
                         * (nb - max_exact)).astype(jnp.int32)
    large = jnp.minimum(large, nb - 1)
    return ret + jnp.where(n < max_exact, n, large)


def _t5_const_distance():
    nb = T5_BUCKETS // 2
    max_exact = nb // 2
    n = np.arange(max_exact, 4 * T5_MAX_DIST, dtype=np.float64)
    large = max_exact + np.floor(np.log(n / max_exact) / math.log(T5_MAX_DIST / max_exact) * (nb - max_exact))
    below = np.nonzero(large < nb - 1)[0]
    return int(n[below[-1]]) + 2


def _tile_rel(tile, n_off):
    i = jnp.arange(2 * tile, dtype=jnp.int32)
    rel = jnp.where(i < tile, -i, 2 * tile - i)
    return rel[None, :] - tile * jnp.arange(n_off, dtype=jnp.int32)[:, None]


def _t5_vectors(t5_bias, tile):
    n_off = 1
    while (n_off - 1) * tile + 1 < _t5_const_distance():
        n_off += 1
    tb = t5_bias.astype(F32)
    vals = tb[_t5_bucket(_tile_rel(tile, n_off))] - tb[T5_BUCKETS // 2 - 1]
    return vals.transpose(2, 0, 1)[:, :, None, :]


def _band_vectors(rel_bias, tile):
    idx = jnp.clip(_tile_rel(tile, 2), -MAX_REL, MAX_REL) + MAX_REL
    vals = rel_bias.astype(F32)[:, idx]
    return vals.reshape(rel_bias.shape[0] // 2, 4, 1, 2 * tile)


def _top2_sum4(r0, r1, r2, r3):
    a, b = jnp.maximum(r0, r1), jnp.minimum(r0, r1)
    c, d = jnp.maximum(r2, r3), jnp.minimum(r2, r3)
    return jnp.maximum(a, c) + jnp.maximum(jnp.minimum(a, c), jnp.maximum(b, d))


def _route(logits, rbias):
    n_e, n = logits.shape
    scores = 1.0 / (1.0 + jnp.exp(-logits))
    sel = scores + rbias
    row = lax.broadcasted_iota(jnp.int32, sel.shape, 0)
    best = None
    for g in range(N_GROUPS):
        rows = [sel[g * E_PER_GROUP + i: g * E_PER_GROUP + i + 1, :] for i in range(E_PER_GROUP)]
        gs = _top2_sum4(*rows)
        if best is None:
            best, gidx = gs, jnp.zeros(gs.shape, jnp.int32)
        else:
            gidx = jnp.where(gs > best, g, gidx)
            best = jnp.maximum(best, gs)
    masked = jnp.where(row // E_PER_GROUP == gidx, sel, NEG_INF)
    m1 = jnp.max(masked, axis=0, keepdims=True)
    i1 = jnp.min(jnp.where(masked == m1, row, n_e), axis=0, keepdims=True)
    masked2 = jnp.where(row == i1, -3.0e38, masked)
    m2 = jnp.max(masked2, axis=0, keepdims=True)
    i2 = jnp.min(jnp.where(masked2 == m2, row, n_e), axis=0, keepdims=True)
    w1 = jnp.sum(jnp.where(row == i1, scores, 0.0), axis=0, keepdims=True)
    w2 = jnp.sum(jnp.where(row == i2, scores, 0.0), axis=0, keepdims=True)
    den = w1 + w2
    member = jnp.where((row == i1) | (row == i2), 1.0, 0.0).astype(BF16)
    before = jnp.where(lax.broadcasted_iota(jnp.int32, (n, n), 0) < lax.broadcasted_iota(jnp.int32, (n, n), 1),
                       1.0, 0.0).astype(BF16)
    pos = jnp.dot(member, before, preferred_element_type=F32)
    p1 = jnp.sum(jnp.where(row == i1, pos, 0.0), axis=0, keepdims=True)
    p2 = jnp.sum(jnp.where(row == i2, pos, 0.0), axis=0, keepdims=True)
    out_row = lax.broadcasted_iota(jnp.int32, (8, n), 0)
    out = jnp.zeros((8, n), F32)
    for r, val in enumerate((i1.astype(F32), i2.astype(F32), p1, p2, w1 / den, w2 / den)):
        out = jnp.where(out_row == r, val, out)
    return out


def _oproj_kernel(o_ref, wo_ref, x_ref, g1_ref, gn_ref, sc_ref, sh_ref, rwt_ref, rb_ref,
                  xo_ref, h_ref, route_ref):
    y = jnp.dot(o_ref[...], wo_ref[...], preferred_element_type=F32)
    xn = x_ref[...] + g1_ref[...] * y
    xo_ref[...] = xn
    h = _norm_mod(xn, gn_ref[...], sc_ref[...], sh_ref[...])
    h_ref[...] = h.astype(BF16)
    logits = lax.dot_general(rwt_ref[...], h, _NT,
                             preferred_element_type=F32, precision=lax.Precision.HIGHEST)
    route_ref[...] = _route(logits, rb_ref[...])


def _out_proj(o, wo_bf16, x2, g1, gn, sc, sh, rwt, rb, seq):
    t, d = x2.shape
    tm = ROW_TILE
    per_b = seq // tm
    n_e = rwt.shape[0]
    vec = lambda i: (i // per_b, 0, 0)
    return pl.pallas_call(
        _oproj_kernel,
        out_shape=(jax.ShapeDtypeStruct((t, d), F32),
                   jax.ShapeDtypeStruct((t, d), BF16),
                   jax.ShapeDtypeStruct((8, t), F32)),
        grid=(t // tm,),
        in_specs=[
            pl.BlockSpec((tm, d), lambda i: (i, 0)),
            pl.BlockSpec((d, d), lambda i: (0, 0)),
            pl.BlockSpec((tm, d), lambda i: (i, 0)),
            pl.BlockSpec((None, 1, d), vec),
            pl.BlockSpec((1, d), lambda i: (0, 0)),
            pl.BlockSpec((None, 1, d), vec),
            pl.BlockSpec((None, 1, d), vec),
            pl.BlockSpec((n_e, d), lambda i: (0, 0)),
            pl.BlockSpec((n_e, 1), lambda i: (0, 0)),
        ],
        out_specs=(pl.BlockSpec((tm, d), lambda i: (i, 0)),
                   pl.BlockSpec((tm, d), lambda i: (i, 0)),
                   pl.BlockSpec((8, tm), lambda i: (0, i))),
        compiler_params=_cparams(("parallel",)),
        name="out_proj_route",
    )(o, wo_bf16, x2, g1, gn, sc, sh, rwt, rb)


def _moe_plan(route, t, tm, n_e):
    n_tiles = t // tm
    ch, ft = MOE_CHUNK, MOE_FFN_TILE
    loc_rows, main_rows, n_sorted = _moe_rows(t, tm, n_e)
    ids = jnp.arange(n_e, dtype=jnp.int32)
    e = route[0:2].astype(jnp.int32)
    pos = route[2:4].astype(jnp.int32)
    oh = (e[:, :, None] == ids).astype(jnp.int32)
    cnt = oh.sum(0).reshape(n_tiles, tm, n_e).sum(1)
    seg = (cnt + ch - 1) // ch * ch
    loc = jnp.cumsum(seg, axis=1) - seg
    tot = seg.sum(0)
    totp = (tot + ft - 1) // ft * ft
    base = jnp.cumsum(totp) - totp
    gdest = base[None, :] + jnp.cumsum(seg, axis=0) - seg
    col = (oh * jnp.repeat(loc, tm, axis=0)[None]).sum(-1) + pos

    row0 = jnp.arange(loc_rows // ch, dtype=jnp.int32) * ch
    ej = (row0[None, :, None] >= (loc + seg)[:, None, :]).sum(-1)
    ohj = (jnp.minimum(ej, n_e - 1)[..., None] == ids).astype(jnp.int32)
    dst = (ohj * (gdest - loc)[:, None, :]).sum(-1) + row0[None, :]
    dump = main_rows + (jnp.arange(n_tiles, dtype=jnp.int32) % 2)[:, None] * loc_rows + row0[None, :]
    dst = jnp.where(ej < n_e, dst, dump) // ch

    r0 = jnp.arange(n_sorted // ft, dtype=jnp.int32) * ft
    ends = base + totp
    te = jnp.minimum((r0[:, None] >= ends[None, :]).sum(-1), n_e - 1)
    tvalid = (r0 < ends[-1]).astype(jnp.int32)
    tfirst = tvalid * (r0 == (((te[:, None] == ids) * base[None, :]).sum(-1))).astype(jnp.int32)
    return col, dst.astype(jnp.int32), te.astype(jnp.int32), tvalid, tfirst


def _moe_rows(t, tm, n_e):
    ch, ft = MOE_CHUNK, MOE_FFN_TILE
    loc_rows = 2 * tm + n_e * ch
    main = 2 * t + (t // tm) * n_e * (ch - 1) + n_e * (ft - 1)
    main = (main + ft - 1) // ft * ft
    return loc_rows, main, (main + 2 * loc_rows + ft - 1) // ft * ft


def _chunk_copies(dst_ref, tile, local_ref, sorted_ref, sem, to_sorted):
    ch = MOE_CHUNK
    copies = []
    for j in range(local_ref.shape[0] // ch):
        start = 0 if tile is None else pl.multiple_of(dst_ref[tile, j] * ch, ch)
        far = sorted_ref.at[pl.ds(start, ch)]
        near = local_ref.at[pl.ds(j * ch, ch)]
        copies.append(pltpu.make_async_copy(near, far, sem) if to_sorted
                      else pltpu.make_async_copy(far, near, sem))
    return copies


def _dispatch_kernel(dst_ref, col_ref, h_ref, init_ref, xs_ref, comp, sem):
    del init_ref
    tile = pl.program_id(0)
    last = pl.num_programs(0) - 1
    slot = tile % 2

    def wait_slot(s):
        for cp in _chunk_copies(dst_ref, None, comp.at[s], xs_ref, sem.at[s], True):
            cp.wait()

    @pl.when(tile >= 2)
    def _():
        wait_slot(slot)

    col = col_ref[...]
    srow = lax.broadcasted_iota(jnp.int32, (comp.shape[1], col.shape[1]), 0)
    sel = jnp.where((srow == col[0:1]) | (srow == col[1:2]), 1.0, 0.0).astype(BF16)
    comp[slot] = jnp.dot(sel, h_ref[...], preferred_element_type=F32).astype(BF16)
    for cp in _chunk_copies(dst_ref, tile, comp.at[slot], xs_ref, sem.at[slot], True):
        cp.start()

    @pl.when(tile == last)
    def _():
        wait_slot(slot)

    @pl.when((tile == last) & (tile >= 1))
    def _():
        wait_slot(1 - slot)


def _dispatch(dst, col, h, n_e):
    t, d = h.shape
    tm = ROW_TILE
    loc_rows, _, n_sorted = _moe_rows(t, tm, n_e)
    return pl.pallas_call(
        _dispatch_kernel,
        out_shape=jax.ShapeDtypeStruct((n_sorted, d), BF16),
        grid_spec=pltpu.PrefetchScalarGridSpec(
            num_scalar_prefetch=1,
            grid=(t // tm,),
            in_specs=[pl.BlockSpec((2, tm), lambda i, dst: (0, i)),
                      pl.BlockSpec((tm, d), lambda i, dst: (i, 0)),
                      pl.BlockSpec(memory_space=pl.ANY)],
            out_specs=pl.BlockSpec(memory_space=pl.ANY),
            scratch_shapes=[pltpu.VMEM((2, loc_rows, d), BF16), pltpu.SemaphoreType.DMA((2,))],
        ),
        input_output_aliases={3: 0},
        compiler_params=_cparams(("arbitrary",)),
        name="moe_dispatch",
    )(dst, col, h, jnp.zeros((n_sorted, d), BF16))


def _expert_kernel(te_ref, tv_ref, tf_ref, xs_ref, wg_ref, wu_ref, wd_ref, ys_ref, wg_b, wu_b, wd_b):
    del te_ref
    s = pl.program_id(0)

    @pl.when(tf_ref[s] == 1)
    def _():
        wg_b[...] = wg_ref[...].astype(BF16)
        wu_b[...] = wu_ref[...].astype(BF16)
        wd_b[...] = wd_ref[...].astype(BF16)

    @pl.when(tv_ref[s] == 1)
    def _():
        xs = xs_ref[...]
        hg = jnp.dot(xs, wg_b[...], preferred_element_type=F32)
        hu = jnp.dot(xs, wu_b[...], preferred_element_type=F32)
        he = hg * (1.0 / (1.0 + jnp.exp(-hg))) * hu
        ys_ref[...] = jnp.dot(he.astype(BF16), wd_b[...], preferred_element_type=F32).astype(BF16)

    @pl.when(tv_ref[s] == 0)
    def _():
        ys_ref[...] = jnp.zeros(ys_ref.shape, ys_ref.dtype)


def _experts(te, tvalid, tfirst, xs, wg, wu, wd, layer):
    n_sorted, d = xs.shape
    f = wg.shape[3]
    ft = MOE_FFN_TILE
    wspec = lambda r, c: pl.BlockSpec((None, None, r, c), lambda s, te, tv, tf: (layer, te[s], 0, 0))
    return pl.pallas_call(
        _expert_kernel,
        out_shape=jax.ShapeDtypeStruct((n_sorted, d), BF16),
        grid_spec=pltpu.PrefetchScalarGridSpec(
            num_scalar_prefetch=3,
            grid=(n_sorted // ft,),
            in_specs=[pl.BlockSpec((ft, d), lambda s, te, tv, tf: (s, 0)),
                      wspec(d, f), wspec(d, f), wspec(f, d)],
            out_specs=pl.BlockSpec((ft, d), lambda s, te, tv, tf: (s, 0)),
            scratch_shapes=[pltpu.VMEM((d, f), BF16), pltpu.VMEM((d, f), BF16), pltpu.VMEM((f, d), BF16)],
        ),
        compiler_params=_cparams(("arbitrary",)),
        name="moe_experts",
    )(te, tvalid, tfirst, xs, wg, wu, wd)


def _combine_kernel(dst_ref, col_ref, w_ref, ys_ref, x_ref, g2_ref, o_ref, comp, sem):
    tile = pl.program_id(0)
    slot = tile % 2

    def fetch(t, s):
        for cp in _chunk_copies(dst_ref, t, comp.at[s], ys_ref, sem.at[s], False):
            cp.start()

    @pl.when(tile == 0)
    def _():
        fetch(tile, slot)

    @pl.when(tile + 1 < pl.num_programs(0))
    def _():
        fetch(tile + 1, 1 - slot)

    col = col_ref[...]
    w = w_ref[...]
    scol = lax.broadcasted_iota(jnp.int32, (col.shape[0], comp.shape[1]), 1)
    sels = [jnp.where(scol == col[:, k:k + 1], 1.0, 0.0).astype(BF16) for k in range(2)]
    for cp in _chunk_copies(dst_ref, None, comp.at[slot], ys_ref, sem.at[slot], False):
        cp.wait()
    yc = comp[slot]
    acc = w[:, 4:5] * jnp.dot(sels[0], yc, preferred_element_type=F32)
    acc = acc + w[:, 5:6] * jnp.dot(sels[1], yc, preferred_element_type=F32)
    o_ref[...] = x_ref[...] + g2_ref[...] * acc


def _combine(dst, col_t, route_t, ys, x2, g2, seq, n_e):
    t, d = x2.shape
    tm = ROW_TILE
    per_b = seq // tm
    loc_rows = _moe_rows(t, tm, n_e)[0]
    return pl.pallas_call(
        _combine_kernel,
        out_shape=jax.ShapeDtypeStruct((t, d), F32),
        grid_spec=pltpu.PrefetchScalarGridSpec(
            num_scalar_prefetch=1,
            grid=(t // tm,),
            in_specs=[pl.BlockSpec((tm, 2), lambda i, dst: (i, 0)),
                      pl.BlockSpec((tm, 8), lambda i, dst: (i, 0)),
                      pl.BlockSpec(memory_space=pl.ANY),
                      pl.BlockSpec((tm, d), lambda i, dst: (i, 0)),
                      pl.BlockSpec((None, 1, d), lambda i, dst: (i // per_b, 0, 0))],
            out_specs=pl.BlockSpec((tm, d), lambda i, dst: (i, 0)),
            scratch_shapes=[pltpu.VMEM((2, loc_rows, d), BF16), pltpu.SemaphoreType.DMA((2,))],
        ),
        compiler_params=_cparams(("arbitrary",)),
        name="moe_combine",
    )(dst, col_t, route_t, ys, x2, g2)


def _moe(h, route, wg, wu, wd, layer, x2, g2, seq):
    t = x2.shape[0]
    n_e = wg.shape[1]
    col, dst, te, tvalid, tfirst = _moe_plan(route, t, ROW_TILE, n_e)
    xs = _dispatch(dst, col, h, n_e)
    ys = _experts(te, tvalid, tfirst, xs, wg, wu, wd, layer)
    return _combine(dst, col.T, route.T, ys, x2, g2, seq, n_e)


def kernel(x, c, ada_w, ada_b, norm_mix_g, norm_ffn_g, t5_bias, a_w_qkv, a_q_gain, a_k_gain, a_lambda, a_subln_g, a_w_o, b_w_qkv, b_q_gain, b_k_gain, b_rel_bias, b_w_o, router_w, router_bias, moe_w_gate, moe_w_up, moe_w_down):
    batch, seq, d = x.shape
    depth = ada_w.shape[0]
    assert seq % ATTN_TILE == 0 and seq % ROW_TILE == 0 and d == A_HEADS * 2 * A_HEAD_DIM
    assert d == B_HEADS * B_HEAD_DIM and A_HEAD_DIM == B_HEAD_DIM

    c_pad = jnp.zeros((8, d), F32).at[:batch].set(c.astype(F32))
    mod = _modulation(c_pad, ada_w.astype(F32), ada_b.astype(F32))[:, :batch]
    mod = mod.reshape(depth, batch, 6, 1, d)

    rwt = router_w.astype(F32).T
    rb = router_bias.astype(F32).reshape(-1, 1)

    x2 = x.astype(F32).reshape(batch * seq, d)
    for i in range(depth):
        sh1, sc1, g1, sh2, sc2, g2 = [mod[i, :, k] for k in range(6)]
        j = i // 2
        if i % 2 == 0:
            w_qkv, qg, kg, w_o = a_w_qkv[j], a_q_gain[j], a_k_gain[j], a_w_o[j]
        else:
            w_qkv, qg, kg, w_o = b_w_qkv[j], b_q_gain[j], b_k_gain[j], b_w_o[j]
        n_rep = d // qg.shape[0]
        gq = (jnp.tile(qg.astype(F32), n_rep) * (A_HEAD_DIM ** -0.5 * LOG2E)).reshape(d, 1)
        gk = jnp.tile(kg.astype(F32), n_rep).reshape(1, d)
        wb = w_qkv.astype(BF16)
        qt, k, vt = _qkv_proj(x2, norm_mix_g[i].reshape(1, d), sc1, sh1,
                              wb[:, :d].T, wb[:, d:2 * d], wb[:, 2 * d:].T, gq, gk, seq, A_HEAD_DIM)
        if i % 2 == 0:
            lambda_init = 0.8 - 0.6 * math.exp(-0.3 * i)
            o = _attention_a(qt, k, vt, _t5_vectors(t5_bias, ATTN_TILE), a_lambda[j].astype(F32),
                             a_subln_g[j].reshape(1, -1), batch, seq, lambda_init)
        else:
            o = _attention_b(qt, k, vt, _band_vectors(b_rel_bias[j], ATTN_TILE), batch, seq)
        x2, h, route = _out_proj(o, w_o.astype(BF16), x2, g1, norm_ffn_g[i].reshape(1, d),
                                 sc2, sh2, rwt, rb, seq)
        x2 = _moe(h, route, moe_w_gate, moe_w_up, moe_w_down, i, x2, g2, seq)
    return x2.reshape(batch, seq, d)
```

```python
import functools
import math

import numpy as np
import jax
import jax.numpy as jnp
from jax import lax
from jax.experimental import pallas as pl
from jax.experimental.pallas import tpu as pltpu

F32 = jnp.float32
BF16 = jnp.bfloat16

CHUNK = 64
A_HEADS = 8
A_HEAD_DIM = 64
T5_BUCKETS = 32
T5_MAX_DIST = 1024
B_HEADS = 16
B_HEAD_DIM = 64
LEFT_CHUNKS = 8
MAX_REL = 256
N_EXPERTS = 16
N_GROUPS = 4
E_PER_GROUP = N_EXPERTS // N_GROUPS
NORM_EPS = 1e-6
NEG_INF = -1e30
LOG2E = math.log2(math.e)

V7X_LANES = 128
V7X_MXU_DIM = 256

ATTN_TILE = LEFT_CHUNKS * CHUNK
FAR_UNROLL = 4
ROW_TILE = 512
MOE_CHUNK = 16
MOE_FFN_TILE = 512
VMEM_LIMIT = 56 * 1024 * 1024

_NT = (((1,), (1,)), ((), ()))


def _cparams(sem):
    return pltpu.CompilerParams(dimension_semantics=sem, vmem_limit_bytes=VMEM_LIMIT)


def _mod_kernel(c_ref, w_ref, b_ref, o_ref):
    c = c_ref[...]
    s = c * (1.0 / (1.0 + jnp.exp(-c)))
    o_ref[...] = jnp.dot(s, w_ref[...], preferred_element_type=F32,
                         precision=lax.Precision.HIGHEST) + b_ref[...]


def _modulation(c_pad, ada_w, ada_b):
    depth, d, n = ada_w.shape
    rows = c_pad.shape[0]
    tn = 1536
    return pl.pallas_call(
        _mod_kernel,
        out_shape=jax.ShapeDtypeStruct((depth, rows, n), F32),
        grid=(depth, n // tn),
        in_specs=[
            pl.BlockSpec((rows, d), lambda i, j: (0, 0)),
            pl.BlockSpec((None, d, tn), lambda i, j: (i, 0, j)),
            pl.BlockSpec((None, 1, tn), lambda i, j: (i, 0, j)),
        ],
        out_specs=pl.BlockSpec((None, rows, tn), lambda i, j: (i, 0, j)),
        compiler_params=_cparams(("parallel", "parallel")),
        name="adaln_mod",
    )(c_pad, ada_w, ada_b.reshape(depth, 1, n))


def _norm_mod(x, g, sc, sh):
    ms = jnp.mean(x * x, axis=-1, keepdims=True)
    return x * lax.rsqrt(ms + NORM_EPS) * g * (1.0 + sc) + sh


def _qkv_kernel(x_ref, g_ref, sc_ref, sh_ref, wqt_ref, wk_ref, wvt_ref, gq_ref, gk_ref,
                qt_ref, k_ref, vt_ref, *, head_dim):
    h = _norm_mod(x_ref[...], g_ref[...], sc_ref[...], sh_ref[...]).astype(BF16)
    d, tm = qt_ref.shape
    y = lax.dot_general(wqt_ref[...], h, _NT, preferred_element_type=F32)
    y3 = y.reshape(d // head_dim, head_dim, tm)
    ss = jnp.mean(y3 * y3, axis=1, keepdims=True)
    qt_ref[...] = ((y3 * lax.rsqrt(ss + NORM_EPS)).reshape(d, tm) * gq_ref[...]).astype(BF16)
    cw = V7X_MXU_DIM
    r = lax.broadcasted_iota(jnp.int32, (cw, cw), 0) // head_dim
    c = lax.broadcasted_iota(jnp.int32, (cw, cw), 1) // head_dim
    gmat = jnp.where(r == c, 1.0 / head_dim, 0.0).astype(BF16)
    y = jnp.dot(h, wk_ref[...], preferred_element_type=F32)
    ysq = (y * y).astype(BF16)
    for ci in range(d // cw):
        cols = slice(ci * cw, (ci + 1) * cw)
        ss = jnp.dot(ysq[:, cols], gmat, preferred_element_type=F32)
        k_ref[:, cols] = (y[:, cols] * lax.rsqrt(ss + NORM_EPS) * gk_ref[:, cols]).astype(BF16)
    vt_ref[...] = lax.dot_general(wvt_ref[...], h, _NT, preferred_element_type=F32).astype(BF16)


def _qkv_proj(x2, g, sc, sh, wqt, wk, wvt, gq, gk, seq, head_dim):
    t, d = x2.shape
    tm = ROW_TILE
    per_b = seq // tm
    vec = lambda i: (i // per_b, 0, 0)
    full = lambda i: (0, 0)
    return pl.pallas_call(
        functools.partial(_qkv_kernel, head_dim=head_dim),
        out_shape=(jax.ShapeDtypeStruct((d, t), BF16),
                   jax.ShapeDtypeStruct((t, d), BF16),
                   jax.ShapeDtypeStruct((d, t), BF16)),
        grid=(t // tm,),
        in_specs=[
            pl.BlockSpec((tm, d), lambda i: (i, 0)),
            pl.BlockSpec((1, d), full),
            pl.BlockSpec((None, 1, d), vec),
            pl.BlockSpec((None, 1, d), vec),
            pl.BlockSpec((d, d), full),
            pl.BlockSpec((d, d), full),
            pl.BlockSpec((d, d), full),
            pl.BlockSpec((d, 1), full),
            pl.BlockSpec((1, d), full),
        ],
        out_specs=(pl.BlockSpec((d, tm), lambda i: (0, i)),
                   pl.BlockSpec((tm, d), lambda i: (i, 0)),
                   pl.BlockSpec((d, tm), lambda i: (0, i))),
        compiler_params=_cparams(("parallel",)),
        name="qkv_proj",
    )(x2, g, sc, sh, wqt, wk, wvt, gq, gk)


def _softmax_pv(s, vt1, m_ref, acc_ref):
    m_prev = m_ref[...]
    m_new = jnp.maximum(m_prev, jnp.max(s, axis=0, keepdims=True))
    alpha = jnp.exp2(m_prev - m_new)
    p = jnp.exp2(s - m_new).astype(BF16)
    acc_ref[...] = alpha * acc_ref[...] + jnp.dot(vt1, p, preferred_element_type=F32)
    m_ref[...] = m_new


class _TilePipe:
    def __init__(self, k_ref, vt_ref, qa, qb, bufs, state, tk):
        self.k_ref, self.vt_ref, self.qa, self.qb = k_ref, vt_ref, qa, qb
        self.bufs, self.state, self.tk = bufs, state, tk

    def scores(self, j, parity):
        ks = pl.multiple_of(j * self.tk, self.tk)
        k = self.k_ref[pl.ds(ks, self.tk), :]
        buf = self.bufs[parity]
        buf[0] = jnp.dot(k, self.qa, preferred_element_type=F32)
        buf[1] = jnp.dot(k, self.qb, preferred_element_type=F32)

    def consume(self, j, parity, bias_a=None, bias_b=None):
        ks = pl.multiple_of(j * self.tk, self.tk)
        vt = self.vt_ref[:, pl.ds(ks, self.tk)]
        vt1 = jnp.concatenate([vt, jnp.ones((ONES_ROWS, self.tk), vt.dtype)], axis=0)
        buf = self.bufs[parity]
        (ma, acca), (mb, accb) = self.state
        sa, sb = buf[0], buf[1]
        _softmax_pv(sa if bias_a is None else sa + bias_a, vt1, ma, acca)
        _softmax_pv(sb if bias_b is None else sb + bias_b, vt1, mb, accb)

    def step(self, j, parity, bias_a=None, bias_b=None, last=False):
        if not last:
            self.scores(j + 1, 1 - parity)
        self.consume(j, parity, bias_a, bias_b)


def _split_maps(qt, head_dim):
    row = lax.broadcasted_iota(jnp.int32, qt.shape, 0)
    zero = jnp.zeros_like(qt)
    return jnp.where(row < head_dim, qt, zero), jnp.where(row >= head_dim, qt, zero)


ONES_ROWS = 16


def _init_state(refs):
    for m_ref, acc_ref in refs:
        m_ref[...] = jnp.full(m_ref.shape, NEG_INF, F32)
        acc_ref[...] = jnp.zeros(acc_ref.shape, F32)


def _normalized(acc_ref):
    acc = acc_ref[...]
    return acc[:V7X_LANES] / acc[V7X_LANES:V7X_LANES + 1]


def _toeplitz(x_row, n):
    x = jnp.broadcast_to(x_row, (n, x_row.shape[1]))
    return pltpu.roll(x, 0, 1, stride=1, stride_axis=0)[:, :n]


def _chunk_delta(n):
    kc = lax.broadcasted_iota(jnp.int32, (n, n), 0) // CHUNK
    qc = lax.broadcasted_iota(jnp.int32, (n, n), 1) // CHUNK
    return qc - kc


def _attn_a_kernel(qt_ref, k_ref, vt_ref, x_ref, lam_ref, sub_ref, o_ref,
                   tab, buf0, buf1, ma, acca, mb, accb, *, n_off, lambda_init):
    tq = qt_ref.shape[1]
    qi = pl.program_id(2)

    @pl.when(qi == 0)
    def _():
        for o in range(n_off):
            t = _toeplitz(x_ref[o], tq) * LOG2E
            if o == 0:
                t = jnp.where(_chunk_delta(tq) >= 0, t, NEG_INF)
            tab[o] = t

    qa, qb = _split_maps(qt_ref[...], A_HEAD_DIM)
    _init_state(((ma, acca), (mb, accb)))

    pipe = _TilePipe(k_ref, vt_ref, qa, qb, (buf0, buf1), ((ma, acca), (mb, accb)), tq)
    def near_steps(first):
        for o in range(first, -1, -1):
            pipe.step(qi - o, o % 2, tab[o], tab[o], last=(o == 0))

    for q0 in range(n_off - 1):
        @pl.when(qi == q0)
        def _():
            pipe.scores(0, q0 % 2)
            near_steps(q0)

    n_far = jnp.maximum(qi - (n_off - 1), 0)
    rem = n_far % FAR_UNROLL
    for par in range(2):
        @pl.when((qi >= n_off - 1) & (qi % 2 == par))
        def _():
            pipe.scores(0, par)

    def far_pair(j):
        pipe.step(j, (n_off - 1) % 2)
        pipe.step(j + 1, n_off % 2)

    @pl.when(rem % 2 == 1)
    def _():
        pipe.step(0, n_off % 2)

    @pl.when(rem >= 2)
    def _():
        far_pair(rem % 2)

    def far_trip(t, carry):
        for u in range(FAR_UNROLL // 2):
            far_pair(FAR_UNROLL * t + rem + 2 * u)
        return carry
    lax.fori_loop(0, n_far // FAR_UNROLL, far_trip, 0)

    @pl.when(qi >= n_off - 1)
    def _():
        near_steps(n_off - 1)

    lam = lam_ref[...]
    lam_full = (jnp.exp(jnp.sum(lam[0:1] * lam[1:2], axis=-1, keepdims=True))
                - jnp.exp(jnp.sum(lam[2:3] * lam[3:4], axis=-1, keepdims=True)) + lambda_init)
    a = _normalized(acca) - lam_full * _normalized(accb)
    ms = jnp.mean(a * a, axis=0, keepdims=True)
    an = a * lax.rsqrt(ms + NORM_EPS)
    o_ref[...] = (an.T * (sub_ref[...] * (1.0 - lambda_init))).astype(BF16)


def _attn_b_kernel(qt_ref, k_ref, vt_ref, x_ref, o_ref, tab, buf0, buf1, ma, acca, mb, accb):
    tq = qt_ref.shape[1]
    qi = pl.program_id(2)

    @pl.when(qi == 0)
    def _():
        for m in range(2):
            for o in range(2):
                d = _chunk_delta(tq) + o * (tq // CHUNK)
                t = _toeplitz(x_ref[2 * m + o], tq) * LOG2E
                tab[2 * m + o] = jnp.where((d >= 0) & (d <= LEFT_CHUNKS), t, NEG_INF)

    qa, qb = _split_maps(qt_ref[...], B_HEAD_DIM)
    _init_state(((ma, acca), (mb, accb)))

    pipe = _TilePipe(k_ref, vt_ref, qa, qb, (buf0, buf1), ((ma, acca), (mb, accb)), tq)

    @pl.when(qi == 0)
    def _():
        pipe.scores(0, 0)
        pipe.step(0, 0, tab[0], tab[2], last=True)

    @pl.when(qi >= 1)
    def _():
        pipe.scores(qi - 1, 1)
        pipe.step(qi - 1, 1, tab[1], tab[3])
        pipe.step(qi, 0, tab[0], tab[2], last=True)

    oa, ob = _normalized(acca), _normalized(accb)
    row = lax.broadcasted_iota(jnp.int32, oa.shape, 0)
    o_ref[...] = jnp.where(row < B_HEAD_DIM, oa, ob).T.astype(BF16)


def _attn_scratch(n_tab, tq):
    one = [pltpu.VMEM((1, tq), F32), pltpu.VMEM((V7X_LANES + ONES_ROWS, tq), F32)]
    score_buf = pltpu.VMEM((2, tq, tq), F32)
    return [pltpu.VMEM((n_tab, tq, tq), F32), score_buf, score_buf] + one + one


def _attn_specs(batch, seq, nq, tq):
    return dict(
        q=pl.BlockSpec((V7X_LANES, tq), lambda b, h, i: (h, b * nq + i)),
        k=pl.BlockSpec((seq, V7X_LANES), lambda b, h, i: (b, h)),
        v=pl.BlockSpec((V7X_LANES, seq), lambda b, h, i: (h, b)),
        o=pl.BlockSpec((tq, V7X_LANES), lambda b, h, i: (b * nq + i, h)),
    )


def _attention_a(qt, k, vt, xvec, lam, sub_g, batch, seq, lambda_init):
    d, t = qt.shape
    tq = ATTN_TILE
    nq = seq // tq
    n_off = xvec.shape[1]
    sp = _attn_specs(batch, seq, nq, tq)
    return pl.pallas_call(
        functools.partial(_attn_a_kernel, n_off=n_off, lambda_init=lambda_init),
        out_shape=jax.ShapeDtypeStruct((t, d), BF16),
        grid=(batch, d // V7X_LANES, nq),
        in_specs=[sp["q"], sp["k"], sp["v"],
                  pl.BlockSpec((None,) + xvec.shape[1:], lambda b, h, i: (h, 0, 0, 0)),
                  pl.BlockSpec(lam.shape, lambda b, h, i: (0, 0)),
                  pl.BlockSpec(sub_g.shape, lambda b, h, i: (0, 0))],
        out_specs=sp["o"],
        scratch_shapes=_attn_scratch(n_off, tq),
        compiler_params=_cparams(("parallel", "parallel", "arbitrary")),
        name="diff_attention",
    )(qt, k, vt, xvec, lam, sub_g)


def _attention_b(qt, k, vt, xvec, batch, seq):
    d, t = qt.shape
    tq = ATTN_TILE
    nq = seq // tq
    sp = _attn_specs(batch, seq, nq, tq)
    return pl.pallas_call(
        _attn_b_kernel,
        out_shape=jax.ShapeDtypeStruct((t, d), BF16),
        grid=(batch, d // V7X_LANES, nq),
        in_specs=[sp["q"], sp["k"], sp["v"],
                  pl.BlockSpec((None,) + xvec.shape[1:], lambda b, h, i: (h, 0, 0, 0))],
        out_specs=sp["o"],
        scratch_shapes=_attn_scratch(4, tq),
        compiler_params=_cparams(("parallel", "parallel", "arbitrary")),
        name="chunk_attention",
    )(qt, k, vt, xvec)


def _t5_bucket(rel):
    nb = T5_BUCKETS // 2
    ret = jnp.where(rel > 0, nb, 0)
    n = jnp.abs(rel)
    max_exact = nb // 2
    nf = jnp.maximum(n, 1).astype(F32)
    large = max_exact + (jnp.log(nf / max_exact) / math.log(T5_MAX_DIST / max_exact)
                         * (nb - max_exact)).astype(jnp.int32)
    large = jnp.minimum(large, nb - 1)
    return ret + jnp.where(n < max_exact, n, large)


def _t5_const_distance():
    nb = T5_BUCKETS // 2
    max_exact = nb // 2
    n = np.arange(max_exact, 4 * T5_MAX_DIST, dtype=np.float64)
    large = max_exact + np.floor(np.log(n / max_exact) / math.log(T5_MAX_DIST / max_exact) * (nb - max_exact))
    below = np.nonzero(large < nb - 1)[0]
    return int(n[below[-1]]) + 2


def _tile_rel(tile, n_off):
    i = jnp.arange(2 * tile, dtype=jnp.int32)
    rel = jnp.where(i < tile, -i, 2 * tile - i)
    return rel[None, :] - tile * jnp.arange(n_off, dtype=jnp.int32)[:, None]


def _t5_vectors(t5_bias, tile):
    n_off = 1
    while (n_off - 1) * tile + 1 < _t5_const_distance():
        n_off += 1
    tb = t5_bias.astype(F32)
    vals = tb[_t5_bucket(_tile_rel(tile, n_off))] - tb[T5_BUCKETS // 2 - 1]
    return vals.transpose(2, 0, 1)[:, :, None, :]


def _band_vectors(rel_bias, tile):
    idx = jnp.clip(_tile_rel(tile, 2), -MAX_REL, MAX_REL) + MAX_REL
    vals = rel_bias.astype(F32)[:, idx]
    return vals.reshape(rel_bias.shape[0] // 2, 4, 1, 2 * tile)


def _top2_sum4(r0, r1, r2, r3):
    a, b = jnp.maximum(r0, r1), jnp.minimum(r0, r1)
    c, d = jnp.maximum(r2, r3), jnp.minimum(r2, r3)
    return jnp.maximum(a, c) + jnp.maximum(jnp.minimum(a, c), jnp.maximum(b, d))


def _route(logits, rbias):
    n_e, n = logits.shape
    scores = 1.0 / (1.0 + jnp.exp(-logits))
    sel = scores + rbias
    row = lax.broadcasted_iota(jnp.int32, sel.shape, 0)
    best = None
    for g in range(N_GROUPS):
        rows = [sel[g * E_PER_GROUP + i: g * E_PER_GROUP + i + 1, :] for i in range(E_PER_GROUP)]
        gs = _top2_sum4(*rows)
        if best is None:
            best, gidx = gs, jnp.zeros(gs.shape, jnp.int32)
        else:
            gidx = jnp.where(gs > best, g, gidx)
            best = jnp.maximum(best, gs)
    masked = jnp.where(row // E_PER_GROUP == gidx, sel, NEG_INF)
    m1 = jnp.max(masked, axis=0, keepdims=True)
    i1 = jnp.min(jnp.where(masked == m1, row, n_e), axis=0, keepdims=True)
    masked2 = jnp.where(row == i1, -3.0e38, masked)
    m2 = jnp.max(masked2, axis=0, keepdims=True)
    i2 = jnp.min(jnp.where(masked2 == m2, row, n_e), axis=0, keepdims=True)
    w1 = jnp.sum(jnp.where(row == i1, scores, 0.0), axis=0, keepdims=True)
    w2 = jnp.sum(jnp.where(row == i2, scores, 0.0), axis=0, keepdims=True)
    den = w1 + w2
    member = jnp.where((row == i1) | (row == i2), 1.0, 0.0).astype(BF16)
    before = jnp.where(lax.broadcasted_iota(jnp.int32, (n, n), 0) < lax.broadcasted_iota(jnp.int32, (n, n), 1),
                       1.0, 0.0).astype(BF16)
    pos = jnp.dot(member, before, preferred_element_type=F32)
    p1 = jnp.sum(jnp.where(row == i1, pos, 0.0), axis=0, keepdims=True)
    p2 = jnp.sum(jnp.where(row == i2, pos, 0.0), axis=0, keepdims=True)
    out_row = lax.broadcasted_iota(jnp.int32, (8, n), 0)
    out = jnp.zeros((8, n), F32)
    for r, val in enumerate((i1.astype(F32), i2.astype(F32), p1, p2, w1 / den, w2 / den)):
        out = jnp.where(out_row == r, val, out)
    return out


def _oproj_kernel(o_ref, wo_ref, x_ref, g1_ref, gn_ref, sc_ref, sh_ref, rwt_ref, rb_ref,
                  xo_ref, h_ref, route_ref):
    y = jnp.dot(o_ref[...], wo_ref[...], preferred_element_type=F32)
    xn = x_ref[...] + g1_ref[...] * y
    xo_ref[...] = xn
    h = _norm_mod(xn, gn_ref[...], sc_ref[...], sh_ref[...])
    h_hi = h.astype(BF16)
    h_ref[...] = h_hi
    h_lo = (h - h_hi.astype(F32)).astype(BF16)
    n_e = rb_ref.shape[0]
    rw = rwt_ref[...]
    main = lax.dot_general(rw, h_hi, _NT, preferred_element_type=F32)
    corr = lax.dot_general(rw[:n_e], h_lo, _NT, preferred_element_type=F32)
    route_ref[...] = _route(main[:n_e] + main[n_e:] + corr, rb_ref[...])


def _out_proj(o, wo_bf16, x2, g1, gn, sc, sh, rwt, rb, seq):
    t, d = x2.shape
    tm = ROW_TILE
    per_b = seq // tm
    n_e = rb.shape[0]
    vec = lambda i: (i // per_b, 0, 0)
    return pl.pallas_call(
        _oproj_kernel,
        out_shape=(jax.ShapeDtypeStruct((t, d), F32),
                   jax.ShapeDtypeStruct((t, d), BF16),
                   jax.ShapeDtypeStruct((8, t), F32)),
        grid=(t // tm,),
        in_specs=[
            pl.BlockSpec((tm, d), lambda i: (i, 0)),
            pl.BlockSpec((d, d), lambda i: (0, 0)),
            pl.BlockSpec((tm, d), lambda i: (i, 0)),
            pl.BlockSpec((None, 1, d), vec),
            pl.BlockSpec((1, d), lambda i: (0, 0)),
            pl.BlockSpec((None, 1, d), vec),
            pl.BlockSpec((None, 1, d), vec),
            pl.BlockSpec((2 * n_e, d), lambda i: (0, 0)),
            pl.BlockSpec((n_e, 1), lambda i: (0, 0)),
        ],
        out_specs=(pl.BlockSpec((tm, d), lambda i: (i, 0)),
                   pl.BlockSpec((tm, d), lambda i: (i, 0)),
                   pl.BlockSpec((8, tm), lambda i: (0, i))),
        compiler_params=_cparams(("parallel",)),
        name="out_proj_route",
    )(o, wo_bf16, x2, g1, gn, sc, sh, rwt, rb)


def _moe_plan(route, t, tm, n_e):
    n_tiles = t // tm
    ch, ft = MOE_CHUNK, MOE_FFN_TILE
    loc_rows, main_rows, n_sorted = _moe_rows(t, tm, n_e)
    ids = jnp.arange(n_e, dtype=jnp.int32)
    e = route[0:2].astype(jnp.int32)
    pos = route[2:4].astype(jnp.int32)
    oh = (e[:, :, None] == ids).astype(jnp.int32)
    cnt = oh.sum(0).reshape(n_tiles, tm, n_e).sum(1)
    seg = (cnt + ch - 1) // ch * ch
    loc = jnp.cumsum(seg, axis=1) - seg
    tot = seg.sum(0)
    totp = (tot + ft - 1) // ft * ft
    base = jnp.cumsum(totp) - totp
    gdest = base[None, :] + jnp.cumsum(seg, axis=0) - seg
    col = (oh * jnp.repeat(loc, tm, axis=0)[None]).sum(-1) + pos

    row0 = jnp.arange(loc_rows // ch, dtype=jnp.int32) * ch
    ej = (row0[None, :, None] >= (loc + seg)[:, None, :]).sum(-1)
    ohj = (jnp.minimum(ej, n_e - 1)[..., None] == ids).astype(jnp.int32)
    dst = (ohj * (gdest - loc)[:, None, :]).sum(-1) + row0[None, :]
    dump = main_rows + (jnp.arange(n_tiles, dtype=jnp.int32) % 2)[:, None] * loc_rows + row0[None, :]
    dst = jnp.where(ej < n_e, dst, dump) // ch

    r0 = jnp.arange(n_sorted // ft, dtype=jnp.int32) * ft
    ends = base + totp
    te = jnp.minimum((r0[:, None] >= ends[None, :]).sum(-1), n_e - 1)
    tvalid = (r0 < ends[-1]).astype(jnp.int32)
    tfirst = tvalid * (r0 == (((te[:, None] == ids) * base[None, :]).sum(-1))).astype(jnp.int32)
    return col, dst.astype(jnp.int32), te.astype(jnp.int32), tvalid, tfirst


def _moe_rows(t, tm, n_e):
    ch, ft = MOE_CHUNK, MOE_FFN_TILE
    loc_rows = 2 * tm + n_e * ch
    main = 2 * t + (t // tm) * n_e * (ch - 1) + n_e * (ft - 1)
    main = (main + ft - 1) // ft * ft
    return loc_rows, main, (main + 2 * loc_rows + ft - 1) // ft * ft


def _chunk_copies(dst_ref, tile, local_ref, sorted_ref, sem, to_sorted):
    ch = MOE_CHUNK
    copies = []
    for j in range(local_ref.shape[0] // ch):
        start = 0 if tile is None else pl.multiple_of(dst_ref[tile, j] * ch, ch)
        far = sorted_ref.at[pl.ds(start, ch)]
        near = local_ref.at[pl.ds(j * ch, ch)]
        copies.append(pltpu.make_async_copy(near, far, sem) if to_sorted
                      else pltpu.make_async_copy(far, near, sem))
    return copies


def _dispatch_kernel(dst_ref, col_ref, h_ref, init_ref, xs_ref, comp, sem):
    del init_ref
    tile = pl.program_id(0)
    last = pl.num_programs(0) - 1
    slot = tile % 2

    def wait_slot(s):
        for cp in _chunk_copies(dst_ref, None, comp.at[s], xs_ref, sem.at[s], True):
            cp.wait()

    @pl.when(tile >= 2)
    def _():
        wait_slot(slot)

    col = col_ref[...]
    srow = lax.broadcasted_iota(jnp.int32, (comp.shape[1], col.shape[1]), 0)
    sel = jnp.where((srow == col[0:1]) | (srow == col[1:2]), 1.0, 0.0).astype(BF16)
    comp[slot] = jnp.dot(sel, h_ref[...], preferred_element_type=F32).astype(BF16)
    for cp in _chunk_copies(dst_ref, tile, comp.at[slot], xs_ref, sem.at[slot], True):
        cp.start()

    @pl.when(tile == last)
    def _():
        wait_slot(slot)

    @pl.when((tile == last) & (tile >= 1))
    def _():
        wait_slot(1 - slot)


def _dispatch(dst, col, h, n_e):
    t, d = h.shape
    tm = ROW_TILE
    loc_rows, _, n_sorted = _moe_rows(t, tm, n_e)
    return pl.pallas_call(
        _dispatch_kernel,
        out_shape=jax.ShapeDtypeStruct((n_sorted, d), BF16),
        grid_spec=pltpu.PrefetchScalarGridSpec(
            num_scalar_prefetch=1,
            grid=(t // tm,),
            in_specs=[pl.BlockSpec((2, tm), lambda i, dst: (0, i)),
                      pl.BlockSpec((tm, d), lambda i, dst: (i, 0)),
                      pl.BlockSpec(memory_space=pl.ANY)],
            out_specs=pl.BlockSpec(memory_space=pl.ANY),
            scratch_shapes=[pltpu.VMEM((2, loc_rows, d), BF16), pltpu.SemaphoreType.DMA((2,))],
        ),
        input_output_aliases={3: 0},
        compiler_params=_cparams(("arbitrary",)),
        name="moe_dispatch",
    )(dst, col, h, jnp.zeros((n_sorted, d), BF16))


def _expert_kernel(te_ref, tv_ref, tf_ref, xs_ref, wg_ref, wu_ref, wd_ref, ys_ref, wg_b, wu_b, wd_b):
    del te_ref
    s = pl.program_id(0)

    @pl.when(tf_ref[s] == 1)
    def _():
        wg_b[...] = wg_ref[...].astype(BF16)
        wu_b[...] = wu_ref[...].astype(BF16)
        wd_b[...] = wd_ref[...].astype(BF16)

    @pl.when(tv_ref[s] == 1)
    def _():
        xs = xs_ref[...]
        hg = jnp.dot(xs, wg_b[...], preferred_element_type=F32)
        hu = jnp.dot(xs, wu_b[...], preferred_element_type=F32)
        he = hg * (1.0 / (1.0 + jnp.exp(-hg))) * hu
        ys_ref[...] = jnp.dot(he.astype(BF16), wd_b[...], preferred_element_type=F32).astype(BF16)

    @pl.when(tv_ref[s] == 0)
    def _():
        ys_ref[...] = jnp.zeros(ys_ref.shape, ys_ref.dtype)


def _experts(te, tvalid, tfirst, xs, wg, wu, wd, layer):
    n_sorted, d = xs.shape
    f = wg.shape[3]
    ft = MOE_FFN_TILE
    wspec = lambda r, c: pl.BlockSpec((None, None, r, c), lambda s, te, tv, tf: (layer, te[s], 0, 0))
    return pl.pallas_call(
        _expert_kernel,
        out_shape=jax.ShapeDtypeStruct((n_sorted, d), BF16),
        grid_spec=pltpu.PrefetchScalarGridSpec(
            num_scalar_prefetch=3,
            grid=(n_sorted // ft,),
            in_specs=[pl.BlockSpec((ft, d), lambda s, te, tv, tf: (s, 0)),
                      wspec(d, f), wspec(d, f), wspec(f, d)],
            out_specs=pl.BlockSpec((ft, d), lambda s, te, tv, tf: (s, 0)),
            scratch_shapes=[pltpu.VMEM((d, f), BF16), pltpu.VMEM((d, f), BF16), pltpu.VMEM((f, d), BF16)],
        ),
        compiler_params=_cparams(("arbitrary",)),
        name="moe_experts",
    )(te, tvalid, tfirst, xs, wg, wu, wd)


def _combine_kernel(dst_ref, col_ref, w_ref, ys_ref, x_ref, g2_ref, o_ref, comp, sem):
    tile = pl.program_id(0)
    slot = tile % 2

    def fetch(t, s):
        for cp in _chunk_copies(dst_ref, t, comp.at[s], ys_ref, sem.at[s], False):
            cp.start()

    @pl.when(tile == 0)
    def _():
        fetch(tile, slot)

    @pl.when(tile + 1 < pl.num_programs(0))
    def _():
        fetch(tile + 1, 1 - slot)

    col = col_ref[...]
    w = w_ref[...]
    scol = lax.broadcasted_iota(jnp.int32, (col.shape[0], comp.shape[1]), 1)
    sels = [jnp.where(scol == col[:, k:k + 1], 1.0, 0.0).astype(BF16) for k in range(2)]
    for cp in _chunk_copies(dst_ref, None, comp.at[slot], ys_ref, sem.at[slot], False):
        cp.wait()
    yc = comp[slot]
    acc = w[:, 4:5] * jnp.dot(sels[0], yc, preferred_element_type=F32)
    acc = acc + w[:, 5:6] * jnp.dot(sels[1], yc, preferred_element_type=F32)
    o_ref[...] = x_ref[...] + g2_ref[...] * acc


def _combine(dst, col_t, route_t, ys, x2, g2, seq, n_e):
    t, d = x2.shape
    tm = ROW_TILE
    per_b = seq // tm
    loc_rows = _moe_rows(t, tm, n_e)[0]
    return pl.pallas_call(
        _combine_kernel,
        out_shape=jax.ShapeDtypeStruct((t, d), F32),
        grid_spec=pltpu.PrefetchScalarGridSpec(
            num_scalar_prefetch=1,
            grid=(t // tm,),
            in_specs=[pl.BlockSpec((tm, 2), lambda i, dst: (i, 0)),
                      pl.BlockSpec((tm, 8), lambda i, dst: (i, 0)),
                      pl.BlockSpec(memory_space=pl.ANY),
                      pl.BlockSpec((tm, d), lambda i, dst: (i, 0)),
                      pl.BlockSpec((None, 1, d), lambda i, dst: (i // per_b, 0, 0))],
            out_specs=pl.BlockSpec((tm, d), lambda i, dst: (i, 0)),
            scratch_shapes=[pltpu.VMEM((2, loc_rows, d), BF16), pltpu.SemaphoreType.DMA((2,))],
        ),
        compiler_params=_cparams(("arbitrary",)),
        name="moe_combine",
    )(dst, col_t, route_t, ys, x2, g2)


def _moe(h, route, wg, wu, wd, layer, x2, g2, seq):
    t = x2.shape[0]
    n_e = wg.shape[1]
    col, dst, te, tvalid, tfirst = _moe_plan(route, t, ROW_TILE, n_e)
    xs = _dispatch(dst, col, h, n_e)
    ys = _experts(te, tvalid, tfirst, xs, wg, wu, wd, layer)
    return _combine(dst, col.T, route.T, ys, x2, g2, seq, n_e)


def kernel(x, c, ada_w, ada_b, norm_mix_g, norm_ffn_g, t5_bias, a_w_qkv, a_q_gain, a_k_gain, a_lambda, a_subln_g, a_w_o, b_w_qkv, b_q_gain, b_k_gain, b_rel_bias, b_w_o, router_w, router_bias, moe_w_gate, moe_w_up, moe_w_down):
    batch, seq, d = x.shape
    depth = ada_w.shape[0]
    assert seq % ATTN_TILE == 0 and seq % ROW_TILE == 0 and d == A_HEADS * 2 * A_HEAD_DIM
    assert d == B_HEADS * B_HEAD_DIM and A_HEAD_DIM == B_HEAD_DIM

    c_pad = jnp.zeros((8, d), F32).at[:batch].set(c.astype(F32))
    mod = _modulation(c_pad, ada_w.astype(F32), ada_b.astype(F32))[:, :batch]
    mod = mod.reshape(depth, batch, 6, 1, d)

    rwt = router_w.astype(F32).T
    rwt_hi = lax.reduce_precision(rwt, exponent_bits=8, mantissa_bits=7)
    rwt = jnp.concatenate([rwt_hi, rwt - rwt_hi], axis=0).astype(BF16)
    rb = router_bias.astype(F32).reshape(-1, 1)

    x2 = x.astype(F32).reshape(batch * seq, d)
    for i in range(depth):
        sh1, sc1, g1, sh2, sc2, g2 = [mod[i, :, k] for k in range(6)]
        j = i // 2
        if i % 2 == 0:
            w_qkv, qg, kg, w_o = a_w_qkv[j], a_q_gain[j], a_k_gain[j], a_w_o[j]
        else:
            w_qkv, qg, kg, w_o = b_w_qkv[j], b_q_gain[j], b_k_gain[j], b_w_o[j]
        n_rep = d // qg.shape[0]
        gq = (jnp.tile(qg.astype(F32), n_rep) * (A_HEAD_DIM ** -0.5 * LOG2E)).reshape(d, 1)
        gk = jnp.tile(kg.astype(F32), n_rep).reshape(1, d)
        wb = w_qkv.astype(BF16)
        qt, k, vt = _qkv_proj(x2, norm_mix_g[i].reshape(1, d), sc1, sh1,
                              wb[:, :d].T, wb[:, d:2 * d], wb[:, 2 * d:].T, gq, gk, seq, A_HEAD_DIM)
        if i % 2 == 0:
            lambda_init = 0.8 - 0.6 * math.exp(-0.3 * i)
            o = _attention_a(qt, k, vt, _t5_vectors(t5_bias, ATTN_TILE), a_lambda[j].astype(F32),
                             a_subln_g[j].reshape(1, -1), batch, seq, lambda_init)
        else:
            o = _attention_b(qt, k, vt, _band_vectors(b_rel_bias[j], ATTN_TILE), batch, seq)
        x2, h, route = _out_proj(o, w_o.astype(BF16), x2, g1, norm_ffn_g[i].reshape(1, d),
                                 sc2, sh2, rwt, rb, seq)
        x2 = _moe(h, route, moe_w_gate, moe_w_up, moe_w_down, i, x2, g2, seq)
    return x2.reshape(batch, seq, d)
```

```python
import functools
import math

import numpy as np
import jax
import jax.numpy as jnp
from jax import lax
from jax.experimental import pallas as pl
from jax.experimental.pallas import tpu as pltpu

F32 = jnp.float32
BF16 = jnp.bfloat16

CHUNK = 64
A_HEADS = 8
A_HEAD_DIM = 64
T5_BUCKETS = 32
T5_MAX_DIST = 1024
B_HEADS = 16
B_HEAD_DIM = 64
LEFT_CHUNKS = 8
MAX_REL = 256
N_EXPERTS = 16
N_GROUPS = 4
E_PER_GROUP = N_EXPERTS // N_GROUPS
NORM_EPS = 1e-6
NEG_INF = -1e30
LOG2E = math.log2(math.e)

V7X_LANES = 128
V7X_MXU_DIM = 256

ATTN_TILE = LEFT_CHUNKS * CHUNK
FAR_UNROLL = 4
BAND_TILE = 256
BAND_QBLOCKS = 4
ROW_TILE = 512
MOE_CHUNK = 16
MOE_FFN_TILE = 512
VMEM_LIMIT = 56 * 1024 * 1024

_NT = (((1,), (1,)), ((), ()))


def _cparams(sem):
    return pltpu.CompilerParams(dimension_semantics=sem, vmem_limit_bytes=VMEM_LIMIT)


def _mod_kernel(c_ref, w_ref, b_ref, o_ref):
    c = c_ref[...]
    s = c * (1.0 / (1.0 + jnp.exp(-c)))
    o_ref[...] = jnp.dot(s, w_ref[...], preferred_element_type=F32,
                         precision=lax.Precision.HIGHEST) + b_ref[...]


def _modulation(c_pad, ada_w, ada_b):
    depth, d, n = ada_w.shape
    rows = c_pad.shape[0]
    tn = 1536
    return pl.pallas_call(
        _mod_kernel,
        out_shape=jax.ShapeDtypeStruct((depth, rows, n), F32),
        grid=(depth, n // tn),
        in_specs=[
            pl.BlockSpec((rows, d), lambda i, j: (0, 0)),
            pl.BlockSpec((None, d, tn), lambda i, j: (i, 0, j)),
            pl.BlockSpec((None, 1, tn), lambda i, j: (i, 0, j)),
        ],
        out_specs=pl.BlockSpec((None, rows, tn), lambda i, j: (i, 0, j)),
        compiler_params=_cparams(("parallel", "parallel")),
        name="adaln_mod",
    )(c_pad, ada_w, ada_b.reshape(depth, 1, n))


def _norm_mod(x, g, sc, sh):
    ms = jnp.mean(x * x, axis=-1, keepdims=True)
    return x * lax.rsqrt(ms + NORM_EPS) * g * (1.0 + sc) + sh


def _qkv_kernel(x_ref, g_ref, sc_ref, sh_ref, wqt_ref, wk_ref, wvt_ref, gq_ref, gk_ref,
                qt_ref, k_ref, vt_ref, *, head_dim):
    h = _norm_mod(x_ref[...], g_ref[...], sc_ref[...], sh_ref[...]).astype(BF16)
    d, tm = qt_ref.shape
    y = lax.dot_general(wqt_ref[...], h, _NT, preferred_element_type=F32)
    y3 = y.reshape(d // head_dim, head_dim, tm)
    ss = jnp.mean(y3 * y3, axis=1, keepdims=True)
    qt_ref[...] = ((y3 * lax.rsqrt(ss + NORM_EPS)).reshape(d, tm) * gq_ref[...]).astype(BF16)
    cw = V7X_MXU_DIM
    r = lax.broadcasted_iota(jnp.int32, (cw, cw), 0) // head_dim
    c = lax.broadcasted_iota(jnp.int32, (cw, cw), 1) // head_dim
    gmat = jnp.where(r == c, 1.0 / head_dim, 0.0).astype(BF16)
    y = jnp.dot(h, wk_ref[...], preferred_element_type=F32)
    ysq = (y * y).astype(BF16)
    for ci in range(d // cw):
        cols = slice(ci * cw, (ci + 1) * cw)
        ss = jnp.dot(ysq[:, cols], gmat, preferred_element_type=F32)
        k_ref[:, cols] = (y[:, cols] * lax.rsqrt(ss + NORM_EPS) * gk_ref[:, cols]).astype(BF16)
    vt_ref[...] = lax.dot_general(wvt_ref[...], h, _NT, preferred_element_type=F32).astype(BF16)


def _qkv_proj(x2, g, sc, sh, wqt, wk, wvt, gq, gk, seq, head_dim):
    t, d = x2.shape
    tm = ROW_TILE
    per_b = seq // tm
    vec = lambda i: (i // per_b, 0, 0)
    full = lambda i: (0, 0)
    return pl.pallas_call(
        functools.partial(_qkv_kernel, head_dim=head_dim),
        out_shape=(jax.ShapeDtypeStruct((d, t), BF16),
                   jax.ShapeDtypeStruct((t, d), BF16),
                   jax.ShapeDtypeStruct((d, t), BF16)),
        grid=(t // tm,),
        in_specs=[
            pl.BlockSpec((tm, d), lambda i: (i, 0)),
            pl.BlockSpec((1, d), full),
            pl.BlockSpec((None, 1, d), vec),
            pl.BlockSpec((None, 1, d), vec),
            pl.BlockSpec((d, d), full),
            pl.BlockSpec((d, d), full),
            pl.BlockSpec((d, d), full),
            pl.BlockSpec((d, 1), full),
            pl.BlockSpec((1, d), full),
        ],
        out_specs=(pl.BlockSpec((d, tm), lambda i: (0, i)),
                   pl.BlockSpec((tm, d), lambda i: (i, 0)),
                   pl.BlockSpec((d, tm), lambda i: (0, i))),
        compiler_params=_cparams(("parallel",)),
        name="qkv_proj",
    )(x2, g, sc, sh, wqt, wk, wvt, gq, gk)


def _softmax_pv(s, vt1, m_ref, acc_ref):
    m_prev = m_ref[...]
    m_new = jnp.maximum(m_prev, jnp.max(s, axis=0, keepdims=True))
    alpha = jnp.exp2(m_prev - m_new)
    p = jnp.exp2(s - m_new).astype(BF16)
    acc_ref[...] = alpha * acc_ref[...] + jnp.dot(vt1, p, preferred_element_type=F32)
    m_ref[...] = m_new


class _TilePipe:
    def __init__(self, k_ref, vt_ref, qa, qb, bufs, state, tk):
        self.k_ref, self.vt_ref, self.qa, self.qb = k_ref, vt_ref, qa, qb
        self.bufs, self.state, self.tk = bufs, state, tk

    def scores(self, j, parity):
        ks = pl.multiple_of(j * self.tk, self.tk)
        k = self.k_ref[pl.ds(ks, self.tk), :]
        buf = self.bufs[parity]
        buf[0] = jnp.dot(k, self.qa, preferred_element_type=F32)
        buf[1] = jnp.dot(k, self.qb, preferred_element_type=F32)

    def consume(self, j, parity, bias_a=None, bias_b=None):
        ks = pl.multiple_of(j * self.tk, self.tk)
        vt = self.vt_ref[:, pl.ds(ks, self.tk)]
        vt1 = jnp.concatenate([vt, jnp.ones((ONES_ROWS, self.tk), vt.dtype)], axis=0)
        buf = self.bufs[parity]
        (ma, acca), (mb, accb) = self.state
        sa, sb = buf[0], buf[1]
        _softmax_pv(sa if bias_a is None else sa + bias_a, vt1, ma, acca)
        _softmax_pv(sb if bias_b is None else sb + bias_b, vt1, mb, accb)

    def step(self, j, parity, bias_a=None, bias_b=None, last=False):
        if not last:
            self.scores(j + 1, 1 - parity)
        self.consume(j, parity, bias_a, bias_b)


def _split_maps(qt, head_dim):
    row = lax.broadcasted_iota(jnp.int32, qt.shape, 0)
    zero = jnp.zeros_like(qt)
    return jnp.where(row < head_dim, qt, zero), jnp.where(row >= head_dim, qt, zero)


ONES_ROWS = 16


def _init_state(refs):
    for m_ref, acc_ref in refs:
        m_ref[...] = jnp.full(m_ref.shape, NEG_INF, F32)
        acc_ref[...] = jnp.zeros(acc_ref.shape, F32)


def _normalized(acc_ref):
    acc = acc_ref[...]
    return acc[:V7X_LANES] / acc[V7X_LANES:V7X_LANES + 1]


def _toeplitz(x_row, n):
    x = jnp.broadcast_to(x_row, (n, x_row.shape[1]))
    return pltpu.roll(x, 0, 1, stride=1, stride_axis=0)[:, :n]


def _chunk_delta(n):
    kc = lax.broadcasted_iota(jnp.int32, (n, n), 0) // CHUNK
    qc = lax.broadcasted_iota(jnp.int32, (n, n), 1) // CHUNK
    return qc - kc


def _attn_a_kernel(qt_ref, k_ref, vt_ref, x_ref, lam_ref, sub_ref, o_ref,
                   tab, buf0, buf1, ma, acca, mb, accb, *, n_off, lambda_init):
    tq = qt_ref.shape[1]
    qi = pl.program_id(2)

    @pl.when(qi == 0)
    def _():
        for o in range(n_off):
            t = _toeplitz(x_ref[o], tq) * LOG2E
            if o == 0:
                t = jnp.where(_chunk_delta(tq) >= 0, t, NEG_INF)
            tab[o] = t

    qa, qb = _split_maps(qt_ref[...], A_HEAD_DIM)
    _init_state(((ma, acca), (mb, accb)))

    pipe = _TilePipe(k_ref, vt_ref, qa, qb, (buf0, buf1), ((ma, acca), (mb, accb)), tq)
    def near_steps(first):
        for o in range(first, -1, -1):
            pipe.step(qi - o, o % 2, tab[o], tab[o], last=(o == 0))

    for q0 in range(n_off - 1):
        @pl.when(qi == q0)
        def _():
            pipe.scores(0, q0 % 2)
            near_steps(q0)

    n_far = jnp.maximum(qi - (n_off - 1), 0)
    rem = n_far % FAR_UNROLL
    for par in range(2):
        @pl.when((qi >= n_off - 1) & (qi % 2 == par))
        def _():
            pipe.scores(0, par)

    def far_pair(j):
        pipe.step(j, (n_off - 1) % 2)
        pipe.step(j + 1, n_off % 2)

    @pl.when(rem % 2 == 1)
    def _():
        pipe.step(0, n_off % 2)

    @pl.when(rem >= 2)
    def _():
        far_pair(rem % 2)

    def far_trip(t, carry):
        for u in range(FAR_UNROLL // 2):
            far_pair(FAR_UNROLL * t + rem + 2 * u)
        return carry
    lax.fori_loop(0, n_far // FAR_UNROLL, far_trip, 0)

    @pl.when(qi >= n_off - 1)
    def _():
        near_steps(n_off - 1)

    lam = lam_ref[...]
    lam_full = (jnp.exp(jnp.sum(lam[0:1] * lam[1:2], axis=-1, keepdims=True))
                - jnp.exp(jnp.sum(lam[2:3] * lam[3:4], axis=-1, keepdims=True)) + lambda_init)
    a = _normalized(acca) - lam_full * _normalized(accb)
    ms = jnp.mean(a * a, axis=0, keepdims=True)
    an = a * lax.rsqrt(ms + NORM_EPS)
    o_ref[...] = (an.T * (sub_ref[...] * (1.0 - lambda_init))).astype(BF16)


def _attn_b_kernel(qt_ref, k_ref, vt_ref, x_ref, o_ref, tab, sbuf, m_st, acc_st):
    tq = BAND_TILE
    n_blk = qt_ref.shape[1] // tq
    n_off = tab.shape[0] // 2
    step = pl.program_id(2)

    @pl.when(step == 0)
    def _():
        for m in range(2):
            for o in range(n_off):
                d = _chunk_delta(tq) + o * (tq // CHUNK)
                t = _toeplitz(x_ref[n_off * m + o], tq) * LOG2E
                tab[n_off * m + o] = jnp.where((d >= 0) & (d <= LEFT_CHUNKS), t, NEG_INF)

    for c in range(n_blk):
        _init_state(((m_st.at[c, 0], acc_st.at[c, 0]), (m_st.at[c, 1], acc_st.at[c, 1])))
    q_maps = [_split_maps(qt_ref[:, c * tq:(c + 1) * tq], B_HEAD_DIM) for c in range(n_blk)]

    def run(first_step):
        pairs = [(c, o) for c in range(n_blk) for o in range(n_off - 1, -1, -1)
                 if not first_step or c - o >= 0]
        k_tiles, vt_tiles = {}, {}
        for c, o in pairs:
            if c - o not in k_tiles:
                ks = pl.multiple_of((n_blk * step + c - o) * tq, tq)
                k_tiles[c - o] = k_ref[pl.ds(ks, tq), :]
                vt = vt_ref[:, pl.ds(ks, tq)]
                vt_tiles[c - o] = jnp.concatenate([vt, jnp.ones((ONES_ROWS, tq), vt.dtype)], axis=0)
        for c, o in pairs:
            for m in range(2):
                sbuf[n_off * c + o, m] = jnp.dot(k_tiles[c - o], q_maps[c][m], preferred_element_type=F32)
        for c, o in pairs:
            for m in range(2):
                _softmax_pv(sbuf[n_off * c + o, m] + tab[n_off * m + o], vt_tiles[c - o],
                            m_st.at[c, m], acc_st.at[c, m])

    @pl.when(step == 0)
    def _():
        run(True)

    @pl.when(step >= 1)
    def _():
        run(False)

    for c in range(n_blk):
        oa, ob = _normalized(acc_st.at[c, 0]), _normalized(acc_st.at[c, 1])
        row = lax.broadcasted_iota(jnp.int32, oa.shape, 0)
        o_ref[c * tq:(c + 1) * tq, :] = jnp.where(row < B_HEAD_DIM, oa, ob).T.astype(BF16)


def _attn_scratch(n_tab, tq):
    one = [pltpu.VMEM((1, tq), F32), pltpu.VMEM((V7X_LANES + ONES_ROWS, tq), F32)]
    score_buf = pltpu.VMEM((2, tq, tq), F32)
    return [pltpu.VMEM((n_tab, tq, tq), F32), score_buf, score_buf] + one + one


def _attn_specs(batch, seq, nq, tq):
    return dict(
        q=pl.BlockSpec((V7X_LANES, tq), lambda b, h, i: (h, b * nq + i)),
        k=pl.BlockSpec((seq, V7X_LANES), lambda b, h, i: (b, h)),
        v=pl.BlockSpec((V7X_LANES, seq), lambda b, h, i: (h, b)),
        o=pl.BlockSpec((tq, V7X_LANES), lambda b, h, i: (b * nq + i, h)),
    )


def _attention_a(qt, k, vt, xvec, lam, sub_g, batch, seq, lambda_init):
    d, t = qt.shape
    tq = ATTN_TILE
    nq = seq // tq
    n_off = xvec.shape[1]
    sp = _attn_specs(batch, seq, nq, tq)
    return pl.pallas_call(
        functools.partial(_attn_a_kernel, n_off=n_off, lambda_init=lambda_init),
        out_shape=jax.ShapeDtypeStruct((t, d), BF16),
        grid=(batch, d // V7X_LANES, nq),
        in_specs=[sp["q"], sp["k"], sp["v"],
                  pl.BlockSpec((None,) + xvec.shape[1:], lambda b, h, i: (h, 0, 0, 0)),
                  pl.BlockSpec(lam.shape, lambda b, h, i: (0, 0)),
                  pl.BlockSpec(sub_g.shape, lambda b, h, i: (0, 0))],
        out_specs=sp["o"],
        scratch_shapes=_attn_scratch(n_off, tq),
        compiler_params=_cparams(("parallel", "parallel", "arbitrary")),
        name="diff_attention",
    )(qt, k, vt, xvec, lam, sub_g)


def _attention_b(qt, k, vt, xvec, batch, seq):
    d, t = qt.shape
    tq = BAND_TILE
    g = BAND_QBLOCKS
    n_off = xvec.shape[1] // 2
    assert g >= n_off - 1 and seq % (g * tq) == 0
    nq = seq // (g * tq)
    sp = _attn_specs(batch, seq, nq, g * tq)
    rows = V7X_LANES + ONES_ROWS
    return pl.pallas_call(
        _attn_b_kernel,
        out_shape=jax.ShapeDtypeStruct((t, d), BF16),
        grid=(batch, d // V7X_LANES, nq),
        in_specs=[sp["q"], sp["k"], sp["v"],
                  pl.BlockSpec((None,) + xvec.shape[1:], lambda b, h, i: (h, 0, 0, 0))],
        out_specs=sp["o"],
        scratch_shapes=[pltpu.VMEM((2 * n_off, tq, tq), F32), pltpu.VMEM((n_off * g, 2, tq, tq), F32),
                        pltpu.VMEM((g, 2, 1, tq), F32), pltpu.VMEM((g, 2, rows, tq), F32)],
        compiler_params=_cparams(("parallel", "parallel", "arbitrary")),
        name="chunk_attention",
    )(qt, k, vt, xvec)


def _t5_bucket(rel):
    nb = T5_BUCKETS // 2
    ret = jnp.where(rel > 0, nb, 0)
    n = jnp.abs(rel)
    max_exact = nb // 2
    nf = jnp.maximum(n, 1).astype(F32)
    large = max_exact + (jnp.log(nf / max_exact) / math.log(T5_MAX_DIST / max_exact)
                         * (nb - max_exact)).astype(jnp.int32)
    large = jnp.minimum(large, nb - 1)
    return ret + jnp.where(n < max_exact, n, large)


def _t5_const_distance():
    nb = T5_BUCKETS // 2
    max_exact = nb // 2
    n = np.arange(max_exact, 4 * T5_MAX_DIST, dtype=np.float64)
    large = max_exact + np.floor(np.log(n / max_exact) / math.log(T5_MAX_DIST / max_exact) * (nb - max_exact))
    below = np.nonzero(large < nb - 1)[0]
    return int(n[below[-1]]) + 2


def _tile_rel(tile, n_off):
    i = jnp.arange(2 * tile, dtype=jnp.int32)
    rel = jnp.where(i < tile, -i, 2 * tile - i)
    return rel[None, :] - tile * jnp.arange(n_off, dtype=jnp.int32)[:, None]


def _t5_vectors(t5_bias, tile):
    n_off = 1
    while (n_off - 1) * tile + 1 < _t5_const_distance():
        n_off += 1
    tb = t5_bias.astype(F32)
    vals = tb[_t5_bucket(_tile_rel(tile, n_off))] - tb[T5_BUCKETS // 2 - 1]
    return vals.transpose(2, 0, 1)[:, :, None, :]


def _band_vectors(rel_bias, tile):
    n_off = LEFT_CHUNKS * CHUNK // tile + 1
    idx = jnp.clip(_tile_rel(tile, n_off), -MAX_REL, MAX_REL) + MAX_REL
    vals = rel_bias.astype(F32)[:, idx]
    return vals.reshape(rel_bias.shape[0] // 2, 2 * n_off, 1, 2 * tile)


def _top2_sum4(r0, r1, r2, r3):
    a, b = jnp.maximum(r0, r1), jnp.minimum(r0, r1)
    c, d = jnp.maximum(r2, r3), jnp.minimum(r2, r3)
    return jnp.maximum(a, c) + jnp.maximum(jnp.minimum(a, c), jnp.maximum(b, d))


def _route(logits, rbias):
    n_e, n = logits.shape
    scores = 1.0 / (1.0 + jnp.exp(-logits))
    sel = scores + rbias
    row = lax.broadcasted_iota(jnp.int32, sel.shape, 0)
    best = None
    for g in range(N_GROUPS):
        rows = [sel[g * E_PER_GROUP + i: g * E_PER_GROUP + i + 1, :] for i in range(E_PER_GROUP)]
        gs = _top2_sum4(*rows)
        if best is None:
            best, gidx = gs, jnp.zeros(gs.shape, jnp.int32)
        else:
            gidx = jnp.where(gs > best, g, gidx)
            best = jnp.maximum(best, gs)
    masked = jnp.where(row // E_PER_GROUP == gidx, sel, NEG_INF)
    m1 = jnp.max(masked, axis=0, keepdims=True)
    i1 = jnp.min(jnp.where(masked == m1, row, n_e), axis=0, keepdims=True)
    masked2 = jnp.where(row == i1, -3.0e38, masked)
    m2 = jnp.max(masked2, axis=0, keepdims=True)
    i2 = jnp.min(jnp.where(masked2 == m2, row, n_e), axis=0, keepdims=True)
    w1 = jnp.sum(jnp.where(row == i1, scores, 0.0), axis=0, keepdims=True)
    w2 = jnp.sum(jnp.where(row == i2, scores, 0.0), axis=0, keepdims=True)
    den = w1 + w2
    member = jnp.where((row == i1) | (row == i2), 1.0, 0.0).astype(BF16)
    before = jnp.where(lax.broadcasted_iota(jnp.int32, (n, n), 0) < lax.broadcasted_iota(jnp.int32, (n, n), 1),
                       1.0, 0.0).astype(BF16)
    pos = jnp.dot(member, before, preferred_element_type=F32)
    p1 = jnp.sum(jnp.where(row == i1, pos, 0.0), axis=0, keepdims=True)
    p2 = jnp.sum(jnp.where(row == i2, pos, 0.0), axis=0, keepdims=True)
    out_row = lax.broadcasted_iota(jnp.int32, (8, n), 0)
    out = jnp.zeros((8, n), F32)
    for r, val in enumerate((i1.astype(F32), i2.astype(F32), p1, p2, w1 / den, w2 / den)):
        out = jnp.where(out_row == r, val, out)
    return out


def _oproj_kernel(o_ref, wo_ref, x_ref, g1_ref, gn_ref, sc_ref, sh_ref, rwt_ref, rb_ref,
                  xo_ref, h_ref, route_ref):
    y = jnp.dot(o_ref[...], wo_ref[...], preferred_element_type=F32)
    xn = x_ref[...] + g1_ref[...] * y
    xo_ref[...] = xn
    h = _norm_mod(xn, gn_ref[...], sc_ref[...], sh_ref[...])
    h_hi = h.astype(BF16)
    h_ref[...] = h_hi
    h_lo = (h - h_hi.astype(F32)).astype(BF16)
    n_e = rb_ref.shape[0]
    rw = rwt_ref[...]
    main = lax.dot_general(rw, h_hi, _NT, preferred_element_type=F32)
    corr = lax.dot_general(rw[:n_e], h_lo, _NT, preferred_element_type=F32)
    route_ref[...] = _route(main[:n_e] + main[n_e:] + corr, rb_ref[...])


def _out_proj(o, wo_bf16, x2, g1, gn, sc, sh, rwt, rb, seq):
    t, d = x2.shape
    tm = ROW_TILE
    per_b = seq // tm
    n_e = rb.shape[0]
    vec = lambda i: (i // per_b, 0, 0)
    return pl.pallas_call(
        _oproj_kernel,
        out_shape=(jax.ShapeDtypeStruct((t, d), F32),
                   jax.ShapeDtypeStruct((t, d), BF16),
                   jax.ShapeDtypeStruct((8, t), F32)),
        grid=(t // tm,),
        in_specs=[
            pl.BlockSpec((tm, d), lambda i: (i, 0)),
            pl.BlockSpec((d, d), lambda i: (0, 0)),
            pl.BlockSpec((tm, d), lambda i: (i, 0)),
            pl.BlockSpec((None, 1, d), vec),
            pl.BlockSpec((1, d), lambda i: (0, 0)),
            pl.BlockSpec((None, 1, d), vec),
            pl.BlockSpec((None, 1, d), vec),
            pl.BlockSpec((2 * n_e, d), lambda i: (0, 0)),
            pl.BlockSpec((n_e, 1), lambda i: (0, 0)),
        ],
        out_specs=(pl.BlockSpec((tm, d), lambda i: (i, 0)),
                   pl.BlockSpec((tm, d), lambda i: (i, 0)),
                   pl.BlockSpec((8, tm), lambda i: (0, i))),
        compiler_params=_cparams(("parallel",)),
        name="out_proj_route",
    )(o, wo_bf16, x2, g1, gn, sc, sh, rwt, rb)


def _moe_plan(route, t, tm, n_e):
    n_tiles = t // tm
    ch, ft = MOE_CHUNK, MOE_FFN_TILE
    loc_rows, main_rows, n_sorted = _moe_rows(t, tm, n_e)
    ids = jnp.arange(n_e, dtype=jnp.int32)
    e = route[0:2].astype(jnp.int32)
    pos = route[2:4].astype(jnp.int32)
    oh = (e[:, :, None] == ids).astype(jnp.int32)
    cnt = oh.sum(0).reshape(n_tiles, tm, n_e).sum(1)
    seg = (cnt + ch - 1) // ch * ch
    loc = jnp.cumsum(seg, axis=1) - seg
    tot = seg.sum(0)
    totp = (tot + ft - 1) // ft * ft
    base = jnp.cumsum(totp) - totp
    gdest = base[None, :] + jnp.cumsum(seg, axis=0) - seg
    col = (oh * jnp.repeat(loc, tm, axis=0)[None]).sum(-1) + pos

    row0 = jnp.arange(loc_rows // ch, dtype=jnp.int32) * ch
    ej = (row0[None, :, None] >= (loc + seg)[:, None, :]).sum(-1)
    ohj = (jnp.minimum(ej, n_e - 1)[..., None] == ids).astype(jnp.int32)
    dst = (ohj * (gdest - loc)[:, None, :]).sum(-1) + row0[None, :]
    dump = main_rows + (jnp.arange(n_tiles, dtype=jnp.int32) % 2)[:, None] * loc_rows + row0[None, :]
    dst = jnp.where(ej < n_e, dst, dump) // ch

    r0 = jnp.arange(n_sorted // ft, dtype=jnp.int32) * ft
    ends = base + totp
    te = jnp.minimum((r0[:, None] >= ends[None, :]).sum(-1), n_e - 1)
    tvalid = (r0 < ends[-1]).astype(jnp.int32)
    tfirst = tvalid * (r0 == (((te[:, None] == ids) * base[None, :]).sum(-1))).astype(jnp.int32)
    return col, dst.astype(jnp.int32), te.astype(jnp.int32), tvalid, tfirst


def _moe_rows(t, tm, n_e):
    ch, ft = MOE_CHUNK, MOE_FFN_TILE
    loc_rows = 2 * tm + n_e * ch
    main = 2 * t + (t // tm) * n_e * (ch - 1) + n_e * (ft - 1)
    main = (main + ft - 1) // ft * ft
    return loc_rows, main, (main + 2 * loc_rows + ft - 1) // ft * ft


def _chunk_copies(dst_ref, tile, local_ref, sorted_ref, sem, to_sorted):
    ch = MOE_CHUNK
    copies = []
    for j in range(local_ref.shape[0] // ch):
        start = 0 if tile is None else pl.multiple_of(dst_ref[tile, j] * ch, ch)
        far = sorted_ref.at[pl.ds(start, ch)]
        near = local_ref.at[pl.ds(j * ch, ch)]
        copies.append(pltpu.make_async_copy(near, far, sem) if to_sorted
                      else pltpu.make_async_copy(far, near, sem))
    return copies


def _dispatch_kernel(dst_ref, col_ref, h_ref, init_ref, xs_ref, comp, sem):
    del init_ref
    tile = pl.program_id(0)
    last = pl.num_programs(0) - 1
    slot = tile % 2

    def wait_slot(s):
        for cp in _chunk_copies(dst_ref, None, comp.at[s], xs_ref, sem.at[s], True):
            cp.wait()

    @pl.when(tile >= 2)
    def _():
        wait_slot(slot)

    col = col_ref[...]
    srow = lax.broadcasted_iota(jnp.int32, (comp.shape[1], col.shape[1]), 0)
    sel = jnp.where((srow == col[0:1]) | (srow == col[1:2]), 1.0, 0.0).astype(BF16)
    comp[slot] = jnp.dot(sel, h_ref[...], preferred_element_type=F32).astype(BF16)
    for cp in _chunk_copies(dst_ref, tile, comp.at[slot], xs_ref, sem.at[slot], True):
        cp.start()

    @pl.when(tile == last)
    def _():
        wait_slot(slot)

    @pl.when((tile == last) & (tile >= 1))
    def _():
        wait_slot(1 - slot)


def _dispatch(dst, col, h, n_e):
    t, d = h.shape
    tm = ROW_TILE
    loc_rows, _, n_sorted = _moe_rows(t, tm, n_e)
    return pl.pallas_call(
        _dispatch_kernel,
        out_shape=jax.ShapeDtypeStruct((n_sorted, d), BF16),
        grid_spec=pltpu.PrefetchScalarGridSpec(
            num_scalar_prefetch=1,
            grid=(t // tm,),
            in_specs=[pl.BlockSpec((2, tm), lambda i, dst: (0, i)),
                      pl.BlockSpec((tm, d), lambda i, dst: (i, 0)),
                      pl.BlockSpec(memory_space=pl.ANY)],
            out_specs=pl.BlockSpec(memory_space=pl.ANY),
            scratch_shapes=[pltpu.VMEM((2, loc_rows, d), BF16), pltpu.SemaphoreType.DMA((2,))],
        ),
        input_output_aliases={3: 0},
        compiler_params=_cparams(("arbitrary",)),
        name="moe_dispatch",
    )(dst, col, h, jnp.zeros((n_sorted, d), BF16))


def _expert_kernel(te_ref, tv_ref, tf_ref, xs_ref, wg_ref, wu_ref, wd_ref, ys_ref, wg_b, wu_b, wd_b):
    del te_ref
    s = pl.program_id(0)

    @pl.when(tf_ref[s] == 1)
    def _():
        wg_b[...] = wg_ref[...].astype(BF16)
        wu_b[...] = wu_ref[...].astype(BF16)
        wd_b[...] = wd_ref[...].astype(BF16)

    @pl.when(tv_ref[s] == 1)
    def _():
        xs = xs_ref[...]
        hg = jnp.dot(xs, wg_b[...], preferred_element_type=F32)
        hu = jnp.dot(xs, wu_b[...], preferred_element_type=F32)
        he = hg * (1.0 / (1.0 + jnp.exp(-hg))) * hu
        ys_ref[...] = jnp.dot(he.astype(BF16), wd_b[...], preferred_element_type=F32).astype(BF16)

    @pl.when(tv_ref[s] == 0)
    def _():
        ys_ref[...] = jnp.zeros(ys_ref.shape, ys_ref.dtype)


def _experts(te, tvalid, tfirst, xs, wg, wu, wd, layer):
    n_sorted, d = xs.shape
    f = wg.shape[3]
    ft = MOE_FFN_TILE
    wspec = lambda r, c: pl.BlockSpec((None, None, r, c), lambda s, te, tv, tf: (layer, te[s], 0, 0))
    return pl.pallas_call(
        _expert_kernel,
        out_shape=jax.ShapeDtypeStruct((n_sorted, d), BF16),
        grid_spec=pltpu.PrefetchScalarGridSpec(
            num_scalar_prefetch=3,
            grid=(n_sorted // ft,),
            in_specs=[pl.BlockSpec((ft, d), lambda s, te, tv, tf: (s, 0)),
                      wspec(d, f), wspec(d, f), wspec(f, d)],
            out_specs=pl.BlockSpec((ft, d), lambda s, te, tv, tf: (s, 0)),
            scratch_shapes=[pltpu.VMEM((d, f), BF16), pltpu.VMEM((d, f), BF16), pltpu.VMEM((f, d), BF16)],
        ),
        compiler_params=_cparams(("arbitrary",)),
        name="moe_experts",
    )(te, tvalid, tfirst, xs, wg, wu, wd)


def _combine_kernel(dst_ref, col_ref, w_ref, ys_ref, x_ref, g2_ref, o_ref, comp, sem):
    tile = pl.program_id(0)
    slot = tile % 2

    def fetch(t, s):
        for cp in _chunk_copies(dst_ref, t, comp.at[s], ys_ref, sem.at[s], False):
            cp.start()

    @pl.when(tile == 0)
    def _():
        fetch(tile, slot)

    @pl.when(tile + 1 < pl.num_programs(0))
    def _():
        fetch(tile + 1, 1 - slot)

    col = col_ref[...]
    w = w_ref[...]
    scol = lax.broadcasted_iota(jnp.int32, (col.shape[0], comp.shape[1]), 1)
    sels = [jnp.where(scol == col[:, k:k + 1], 1.0, 0.0).astype(BF16) for k in range(2)]
    for cp in _chunk_copies(dst_ref, None, comp.at[slot], ys_ref, sem.at[slot], False):
        cp.wait()
    yc = comp[slot]
    acc = w[:, 4:5] * jnp.dot(sels[0], yc, preferred_element_type=F32)
    acc = acc + w[:, 5:6] * jnp.dot(sels[1], yc, preferred_element_type=F32)
    o_ref[...] = x_ref[...] + g2_ref[...] * acc


def _combine(dst, col_t, route_t, ys, x2, g2, seq, n_e):
    t, d = x2.shape
    tm = ROW_TILE
    per_b = seq // tm
    loc_rows = _moe_rows(t, tm, n_e)[0]
    return pl.pallas_call(
        _combine_kernel,
        out_shape=jax.ShapeDtypeStruct((t, d), F32),
        grid_spec=pltpu.PrefetchScalarGridSpec(
            num_scalar_prefetch=1,
            grid=(t // tm,),
            in_specs=[pl.BlockSpec((tm, 2), lambda i, dst: (i, 0)),
                      pl.BlockSpec((tm, 8), lambda i, dst: (i, 0)),
                      pl.BlockSpec(memory_space=pl.ANY),
                      pl.BlockSpec((tm, d), lambda i, dst: (i, 0)),
                      pl.BlockSpec((None, 1, d), lambda i, dst: (i // per_b, 0, 0))],
            out_specs=pl.BlockSpec((tm, d), lambda i, dst: (i, 0)),
            scratch_shapes=[pltpu.VMEM((2, loc_rows, d), BF16), pltpu.SemaphoreType.DMA((2,))],
        ),
        compiler_params=_cparams(("arbitrary",)),
        name="moe_combine",
    )(dst, col_t, route_t, ys, x2, g2)


def _moe(h, route, wg, wu, wd, layer, x2, g2, seq):
    t = x2.shape[0]
    n_e = wg.shape[1]
    col, dst, te, tvalid, tfirst = _moe_plan(route, t, ROW_TILE, n_e)
    xs = _dispatch(dst, col, h, n_e)
    ys = _experts(te, tvalid, tfirst, xs, wg, wu, wd, layer)
    return _combine(dst, col.T, route.T, ys, x2, g2, seq, n_e)


def kernel(x, c, ada_w, ada_b, norm_mix_g, norm_ffn_g, t5_bias, a_w_qkv, a_q_gain, a_k_gain, a_lambda, a_subln_g, a_w_o, b_w_qkv, b_q_gain, b_k_gain, b_rel_bias, b_w_o, router_w, router_bias, moe_w_gate, moe_w_up, moe_w_down):
    batch, seq, d = x.shape
    depth = ada_w.shape[0]
    assert seq % ATTN_TILE == 0 and seq % ROW_TILE == 0 and d == A_HEADS * 2 * A_HEAD_DIM
    assert d == B_HEADS * B_HEAD_DIM and A_HEAD_DIM == B_HEAD_DIM

    c_pad = jnp.zeros((8, d), F32).at[:batch].set(c.astype(F32))
    mod = _modulation(c_pad, ada_w.astype(F32), ada_b.astype(F32))[:, :batch]
    mod = mod.reshape(depth, batch, 6, 1, d)

    rwt = router_w.astype(F32).T
    rwt_hi = lax.reduce_precision(rwt, exponent_bits=8, mantissa_bits=7)
    rwt = jnp.concatenate([rwt_hi, rwt - rwt_hi], axis=0).astype(BF16)
    rb = router_bias.astype(F32).reshape(-1, 1)

    x2 = x.astype(F32).reshape(batch * seq, d)
    for i in range(depth):
        sh1, sc1, g1, sh2, sc2, g2 = [mod[i, :, k] for k in range(6)]
        j = i // 2
        if i % 2 == 0:
            w_qkv, qg, kg, w_o = a_w_qkv[j], a_q_gain[j], a_k_gain[j], a_w_o[j]
        else:
            w_qkv, qg, kg, w_o = b_w_qkv[j], b_q_gain[j], b_k_gain[j], b_w_o[j]
        n_rep = d // qg.shape[0]
        gq = (jnp.tile(qg.astype(F32), n_rep) * (A_HEAD_DIM ** -0.5 * LOG2E)).reshape(d, 1)
        gk = jnp.tile(kg.astype(F32), n_rep).reshape(1, d)
        wb = w_qkv.astype(BF16)
        qt, k, vt = _qkv_proj(x2, norm_mix_g[i].reshape(1, d), sc1, sh1,
                              wb[:, :d].T, wb[:, d:2 * d], wb[:, 2 * d:].T, gq, gk, seq, A_HEAD_DIM)
        if i % 2 == 0:
            lambda_init = 0.8 - 0.6 * math.exp(-0.3 * i)
            o = _attention_a(qt, k, vt, _t5_vectors(t5_bias, ATTN_TILE), a_lambda[j].astype(F32),
                             a_subln_g[j].reshape(1, -1), batch, seq, lambda_init)
        else:
            o = _attention_b(qt, k, vt, _band_vectors(b_rel_bias[j], BAND_TILE), batch, seq)
        x2, h, route = _out_proj(o, w_o.astype(BF16), x2, g1, norm_ffn_g[i].reshape(1, d),
                                 sc2, sh2, rwt, rb, seq)
        x2 = _moe(h, route, moe_w_gate, moe_w_up, moe_w_down, i, x2, g2, seq)
    return x2.reshape(batch, seq, d)
```

```python
import functools
import math

import numpy as np
import jax
import jax.numpy as jnp
from jax import lax
from jax.experimental import pallas as pl
from jax.experimental.pallas import tpu as pltpu

F32 = jnp.float32
BF16 = jnp.bfloat16

CHUNK = 64
A_HEADS = 8
A_HEAD_DIM = 64
T5_BUCKETS = 32
T5_MAX_DIST = 1024
B_HEADS = 16
B_HEAD_DIM = 64
LEFT_CHUNKS = 8
MAX_REL = 256
N_EXPERTS = 16
N_GROUPS = 4
E_PER_GROUP = N_EXPERTS // N_GROUPS
NORM_EPS = 1e-6
NEG_INF = -1e30
LOG2E = math.log2(math.e)

V7X_LANES = 128
V7X_MXU_DIM = 256

ATTN_TILE = LEFT_CHUNKS * CHUNK
FAR_UNROLL = 4
BAND_TILE = 256
BAND_QBLOCKS = 4
ROW_TILE = 512
MOE_CHUNK = 16
MOE_FFN_TILE = 512
VMEM_LIMIT = 56 * 1024 * 1024

_NT = (((1,), (1,)), ((), ()))


def _cparams(sem):
    return pltpu.CompilerParams(dimension_semantics=sem, vmem_limit_bytes=VMEM_LIMIT)


def _mod_kernel(c_ref, w_ref, b_ref, o_ref):
    c = c_ref[...]
    s = c * (1.0 / (1.0 + jnp.exp(-c)))
    o_ref[...] = jnp.dot(s, w_ref[...], preferred_element_type=F32,
                         precision=lax.Precision.HIGHEST) + b_ref[...]


def _modulation(c_pad, ada_w, ada_b):
    depth, d, n = ada_w.shape
    rows = c_pad.shape[0]
    tn = 1536
    return pl.pallas_call(
        _mod_kernel,
        out_shape=jax.ShapeDtypeStruct((depth, rows, n), F32),
        grid=(depth, n // tn),
        in_specs=[
            pl.BlockSpec((rows, d), lambda i, j: (0, 0)),
            pl.BlockSpec((None, d, tn), lambda i, j: (i, 0, j)),
            pl.BlockSpec((None, 1, tn), lambda i, j: (i, 0, j)),
        ],
        out_specs=pl.BlockSpec((None, rows, tn), lambda i, j: (i, 0, j)),
        compiler_params=_cparams(("parallel", "parallel")),
        name="adaln_mod",
    )(c_pad, ada_w, ada_b.reshape(depth, 1, n))


def _norm_mod(x, g, sc, sh):
    ms = jnp.mean(x * x, axis=-1, keepdims=True)
    return x * lax.rsqrt(ms + NORM_EPS) * g * (1.0 + sc) + sh


def _qkv_kernel(x_ref, g_ref, sc_ref, sh_ref, wqt_ref, wk_ref, wvt_ref, gq_ref, gk_ref,
                qt_ref, k_ref, vt_ref, *, head_dim):
    h = _norm_mod(x_ref[...], g_ref[...], sc_ref[...], sh_ref[...]).astype(BF16)
    d, tm = qt_ref.shape
    y = lax.dot_general(wqt_ref[...], h, _NT, preferred_element_type=F32)
    y3 = y.reshape(d // head_dim, head_dim, tm)
    ss = jnp.mean(y3 * y3, axis=1, keepdims=True)
    qt_ref[...] = ((y3 * lax.rsqrt(ss + NORM_EPS)).reshape(d, tm) * gq_ref[...]).astype(BF16)
    cw = V7X_MXU_DIM
    r = lax.broadcasted_iota(jnp.int32, (cw, cw), 0) // head_dim
    c = lax.broadcasted_iota(jnp.int32, (cw, cw), 1) // head_dim
    gmat = jnp.where(r == c, 1.0 / head_dim, 0.0).astype(BF16)
    y = jnp.dot(h, wk_ref[...], preferred_element_type=F32)
    ysq = (y * y).astype(BF16)
    for ci in range(d // cw):
        cols = slice(ci * cw, (ci + 1) * cw)
        ss = jnp.dot(ysq[:, cols], gmat, preferred_element_type=F32)
        k_ref[:, cols] = (y[:, cols] * lax.rsqrt(ss + NORM_EPS) * gk_ref[:, cols]).astype(BF16)
    vt_ref[...] = lax.dot_general(wvt_ref[...], h, _NT, preferred_element_type=F32).astype(BF16)


def _qkv_proj(x2, g, sc, sh, wqt, wk, wvt, gq, gk, seq, head_dim):
    t, d = x2.shape
    tm = ROW_TILE
    per_b = seq // tm
    vec = lambda i: (i // per_b, 0, 0)
    full = lambda i: (0, 0)
    return pl.pallas_call(
        functools.partial(_qkv_kernel, head_dim=head_dim),
        out_shape=(jax.ShapeDtypeStruct((d, t), BF16),
                   jax.ShapeDtypeStruct((t, d), BF16),
                   jax.ShapeDtypeStruct((d, t), BF16)),
        grid=(t // tm,),
        in_specs=[
            pl.BlockSpec((tm, d), lambda i: (i, 0)),
            pl.BlockSpec((1, d), full),
            pl.BlockSpec((None, 1, d), vec),
            pl.BlockSpec((None, 1, d), vec),
            pl.BlockSpec((d, d), full),
            pl.BlockSpec((d, d), full),
            pl.BlockSpec((d, d), full),
            pl.BlockSpec((d, 1), full),
            pl.BlockSpec((1, d), full),
        ],
        out_specs=(pl.BlockSpec((d, tm), lambda i: (0, i)),
                   pl.BlockSpec((tm, d), lambda i: (i, 0)),
                   pl.BlockSpec((d, tm), lambda i: (0, i))),
        compiler_params=_cparams(("parallel",)),
        name="qkv_proj",
    )(x2, g, sc, sh, wqt, wk, wvt, gq, gk)


def _softmax_pv(s, vt1, m_ref, acc_ref):
    m_prev = m_ref[...]
    m_new = jnp.maximum(m_prev, jnp.max(s, axis=0, keepdims=True))
    alpha = jnp.exp2(m_prev - m_new)
    p = jnp.exp2(s - m_new).astype(BF16)
    acc_ref[...] = alpha * acc_ref[...] + jnp.dot(vt1, p, preferred_element_type=F32)
    m_ref[...] = m_new


def _split_maps(qt, head_dim):
    row = lax.broadcasted_iota(jnp.int32, qt.shape, 0)
    zero = jnp.zeros_like(qt)
    return jnp.where(row < head_dim, qt, zero), jnp.where(row >= head_dim, qt, zero)


ONES_ROWS = 16


def _init_state(refs):
    for m_ref, acc_ref in refs:
        m_ref[...] = jnp.full(m_ref.shape, NEG_INF, F32)
        acc_ref[...] = jnp.zeros(acc_ref.shape, F32)


def _normalized(acc_ref):
    acc = acc_ref[...]
    return acc[:V7X_LANES] / acc[V7X_LANES:V7X_LANES + 1]


def _toeplitz(x_row, n):
    x = jnp.broadcast_to(x_row, (n, x_row.shape[1]))
    return pltpu.roll(x, 0, 1, stride=1, stride_axis=0)[:, :n]


def _chunk_delta(n):
    kc = lax.broadcasted_iota(jnp.int32, (n, n), 0) // CHUNK
    qc = lax.broadcasted_iota(jnp.int32, (n, n), 1) // CHUNK
    return qc - kc


def _attn_a_kernel(qt_ref, k_ref, vt_ref, x_ref, lam_ref, sub_ref, o_ref,
                   tab, fbuf, nbuf, m_st, acc_st, *, n_off, lambda_init):
    tq = ATTN_TILE
    step = pl.program_id(2)

    @pl.when(step == 0)
    def _():
        for o in range(n_off):
            t = _toeplitz(x_ref[o], tq) * LOG2E
            if o == 0:
                t = jnp.where(_chunk_delta(tq) >= 0, t, NEG_INF)
            tab[o] = t

    for c in range(2):
        _init_state(((m_st.at[c, 0], acc_st.at[c, 0]), (m_st.at[c, 1], acc_st.at[c, 1])))
    q_maps = [_split_maps(qt_ref[:, c * tq:(c + 1) * tq], A_HEAD_DIM) for c in range(2)]

    def k_tile(j):
        return k_ref[pl.ds(pl.multiple_of(j * tq, tq), tq), :]

    def vt1_tile(j):
        vt = vt_ref[:, pl.ds(pl.multiple_of(j * tq, tq), tq)]
        return jnp.concatenate([vt, jnp.ones((ONES_ROWS, tq), vt.dtype)], axis=0)

    def scores(buf, k, c):
        for m in range(2):
            buf[c, m] = jnp.dot(k, q_maps[c][m], preferred_element_type=F32)

    def consume(buf, vt1, c, o):
        for m in range(2):
            s = buf[c, m]
            _softmax_pv(s if o >= n_off else s + tab[o], vt1, m_st.at[c, m], acc_st.at[c, m])

    def near_block(first_tile, n_tiles, first_scored):
        units = [(s, c) for s in range(n_tiles) for c in range(2) if n_tiles - 2 + c - s >= 0]
        kts = [k_tile(first_tile + s) for s in range(n_tiles)]
        buf_of = lambda s: fbuf.at[0] if s == 0 else nbuf.at[s - 1]
        for s, c in units:
            if not (first_scored and s == 0):
                scores(buf_of(s), kts[s], c)
        vts = [vt1_tile(first_tile + s) for s in range(n_tiles)]
        for s, c in units:
            consume(buf_of(s), vts[s], c, n_tiles - 2 + c - s)

    first_general = (n_off - 1) // 2
    for i0 in range(first_general):
        @pl.when(step == i0)
        def _():
            near_block(0, 2 * i0 + 2, False)

    def far_step(j, parity):
        k = k_tile(j + 1)
        for c in range(2):
            scores(fbuf.at[1 - parity], k, c)
        vt1 = vt1_tile(j)
        for c in range(2):
            consume(fbuf.at[parity], vt1, c, n_off)

    @pl.when(step >= first_general)
    def _():
        k = k_tile(0)
        for c in range(2):
            scores(fbuf.at[0], k, c)

    n_far = jnp.maximum(2 * step - (n_off - 1), 0)
    rem = n_far % FAR_UNROLL

    @pl.when(rem >= 2)
    def _():
        far_step(0, 0)
        far_step(1, 1)

    def far_trip(t, carry):
        for u in range(FAR_UNROLL):
            far_step(FAR_UNROLL * t + rem + u, u % 2)
        return carry
    lax.fori_loop(0, n_far // FAR_UNROLL, far_trip, 0)

    @pl.when(step >= first_general)
    def _():
        near_block(n_far, n_off + 1, True)

    lam = lam_ref[...]
    lam_full = (jnp.exp(jnp.sum(lam[0:1] * lam[1:2], axis=-1, keepdims=True))
                - jnp.exp(jnp.sum(lam[2:3] * lam[3:4], axis=-1, keepdims=True)) + lambda_init)
    for c in range(2):
        a = _normalized(acc_st.at[c, 0]) - lam_full * _normalized(acc_st.at[c, 1])
        ms = jnp.mean(a * a, axis=0, keepdims=True)
        an = a * lax.rsqrt(ms + NORM_EPS)
        o_ref[c * tq:(c + 1) * tq, :] = (an.T * (sub_ref[...] * (1.0 - lambda_init))).astype(BF16)


def _attn_b_kernel(qt_ref, k_ref, vt_ref, x_ref, o_ref, tab, sbuf, m_st, acc_st):
    tq = BAND_TILE
    n_blk = qt_ref.shape[1] // tq
    n_off = tab.shape[0] // 2
    step = pl.program_id(2)

    @pl.when(step == 0)
    def _():
        for m in range(2):
            for o in range(n_off):
                d = _chunk_delta(tq) + o * (tq // CHUNK)
                t = _toeplitz(x_ref[n_off * m + o], tq) * LOG2E
                tab[n_off * m + o] = jnp.where((d >= 0) & (d <= LEFT_CHUNKS), t, NEG_INF)

    for c in range(n_blk):
        _init_state(((m_st.at[c, 0], acc_st.at[c, 0]), (m_st.at[c, 1], acc_st.at[c, 1])))
    q_maps = [_split_maps(qt_ref[:, c * tq:(c + 1) * tq], B_HEAD_DIM) for c in range(n_blk)]

    def run(first_step):
        pairs = [(c, o) for c in range(n_blk) for o in range(n_off - 1, -1, -1)
                 if not first_step or c - o >= 0]
        k_tiles, vt_tiles = {}, {}
        for c, o in pairs:
            if c - o not in k_tiles:
                ks = pl.multiple_of((n_blk * step + c - o) * tq, tq)
                k_tiles[c - o] = k_ref[pl.ds(ks, tq), :]
                vt = vt_ref[:, pl.ds(ks, tq)]
                vt_tiles[c - o] = jnp.concatenate([vt, jnp.ones((ONES_ROWS, tq), vt.dtype)], axis=0)
        for c, o in pairs:
            for m in range(2):
                sbuf[n_off * c + o, m] = jnp.dot(k_tiles[c - o], q_maps[c][m], preferred_element_type=F32)
        for c, o in pairs:
            for m in range(2):
                _softmax_pv(sbuf[n_off * c + o, m] + tab[n_off * m + o], vt_tiles[c - o],
                            m_st.at[c, m], acc_st.at[c, m])

    @pl.when(step == 0)
    def _():
        run(True)

    @pl.when(step >= 1)
    def _():
        run(False)

    for c in range(n_blk):
        oa, ob = _normalized(acc_st.at[c, 0]), _normalized(acc_st.at[c, 1])
        row = lax.broadcasted_iota(jnp.int32, oa.shape, 0)
        o_ref[c * tq:(c + 1) * tq, :] = jnp.where(row < B_HEAD_DIM, oa, ob).T.astype(BF16)


def _attn_specs(batch, seq, nq, tq):
    return dict(
        q=pl.BlockSpec((V7X_LANES, tq), lambda b, h, i: (h, b * nq + i)),
        k=pl.BlockSpec((seq, V7X_LANES), lambda b, h, i: (b, h)),
        v=pl.BlockSpec((V7X_LANES, seq), lambda b, h, i: (h, b)),
        o=pl.BlockSpec((tq, V7X_LANES), lambda b, h, i: (b * nq + i, h)),
    )


def _attention_a(qt, k, vt, xvec, lam, sub_g, batch, seq, lambda_init):
    d, t = qt.shape
    tq = ATTN_TILE
    n_off = xvec.shape[1]
    assert n_off % 2 == 1 and seq % (2 * tq) == 0
    nq = seq // (2 * tq)
    sp = _attn_specs(batch, seq, nq, 2 * tq)
    rows = V7X_LANES + ONES_ROWS
    return pl.pallas_call(
        functools.partial(_attn_a_kernel, n_off=n_off, lambda_init=lambda_init),
        out_shape=jax.ShapeDtypeStruct((t, d), BF16),
        grid=(batch, d // V7X_LANES, nq),
        in_specs=[sp["q"], sp["k"], sp["v"],
                  pl.BlockSpec((None,) + xvec.shape[1:], lambda b, h, i: (h, 0, 0, 0)),
                  pl.BlockSpec(lam.shape, lambda b, h, i: (0, 0)),
                  pl.BlockSpec(sub_g.shape, lambda b, h, i: (0, 0))],
        out_specs=sp["o"],
        scratch_shapes=[pltpu.VMEM((n_off, tq, tq), F32), pltpu.VMEM((2, 2, 2, tq, tq), F32),
                        pltpu.VMEM((n_off, 2, 2, tq, tq), F32),
                        pltpu.VMEM((2, 2, 1, tq), F32), pltpu.VMEM((2, 2, rows, tq), F32)],
        compiler_params=_cparams(("parallel", "parallel", "arbitrary")),
        name="diff_attention",
    )(qt, k, vt, xvec, lam, sub_g)


def _attention_b(qt, k, vt, xvec, batch, seq):
    d, t = qt.shape
    tq = BAND_TILE
    g = BAND_QBLOCKS
    n_off = xvec.shape[1] // 2
    assert g >= n_off - 1 and seq % (g * tq) == 0
    nq = seq // (g * tq)
    sp = _attn_specs(batch, seq, nq, g * tq)
    rows = V7X_LANES + ONES_ROWS
    return pl.pallas_call(
        _attn_b_kernel,
        out_shape=jax.ShapeDtypeStruct((t, d), BF16),
        grid=(batch, d // V7X_LANES, nq),
        in_specs=[sp["q"], sp["k"], sp["v"],
                  pl.BlockSpec((None,) + xvec.shape[1:], lambda b, h, i: (h, 0, 0, 0))],
        out_specs=sp["o"],
        scratch_shapes=[pltpu.VMEM((2 * n_off, tq, tq), F32), pltpu.VMEM((n_off * g, 2, tq, tq), F32),
                        pltpu.VMEM((g, 2, 1, tq), F32), pltpu.VMEM((g, 2, rows, tq), F32)],
        compiler_params=_cparams(("parallel", "parallel", "arbitrary")),
        name="chunk_attention",
    )(qt, k, vt, xvec)


def _t5_bucket(rel):
    nb = T5_BUCKETS // 2
    ret = jnp.where(rel > 0, nb, 0)
    n = jnp.abs(rel)
    max_exact = nb // 2
    nf = jnp.maximum(n, 1).astype(F32)
    large = max_exact + (jnp.log(nf / max_exact) / math.log(T5_MAX_DIST / max_exact)
                         * (nb - max_exact)).astype(jnp.int32)
    large = jnp.minimum(large, nb - 1)
    return ret + jnp.where(n < max_exact, n, large)


def _t5_const_distance():
    nb = T5_BUCKETS // 2
    max_exact = nb // 2
    n = np.arange(max_exact, 4 * T5_MAX_DIST, dtype=np.float64)
    large = max_exact + np.floor(np.log(n / max_exact) / math.log(T5_MAX_DIST / max_exact) * (nb - max_exact))
    below = np.nonzero(large < nb - 1)[0]
    return int(n[below[-1]]) + 2


def _tile_rel(tile, n_off):
    i = jnp.arange(2 * tile, dtype=jnp.int32)
    rel = jnp.where(i < tile, -i, 2 * tile - i)
    return rel[None, :] - tile * jnp.arange(n_off, dtype=jnp.int32)[:, None]


def _t5_vectors(t5_bias, tile):
    n_off = 1
    while (n_off - 1) * tile + 1 < _t5_const_distance():
        n_off += 1
    tb = t5_bias.astype(F32)
    vals = tb[_t5_bucket(_tile_rel(tile, n_off))] - tb[T5_BUCKETS // 2 - 1]
    return vals.transpose(2, 0, 1)[:, :, None, :]


def _band_vectors(rel_bias, tile):
    n_off = LEFT_CHUNKS * CHUNK // tile + 1
    idx = jnp.clip(_tile_rel(tile, n_off), -MAX_REL, MAX_REL) + MAX_REL
    vals = rel_bias.astype(F32)[:, idx]
    return vals.reshape(rel_bias.shape[0] // 2, 2 * n_off, 1, 2 * tile)


def _top2_sum4(r0, r1, r2, r3):
    a, b = jnp.maximum(r0, r1), jnp.minimum(r0, r1)
    c, d = jnp.maximum(r2, r3), jnp.minimum(r2, r3)
    return jnp.maximum(a, c) + jnp.maximum(jnp.minimum(a, c), jnp.maximum(b, d))


def _route(logits, rbias):
    n_e, n = logits.shape
    scores = 1.0 / (1.0 + jnp.exp(-logits))
    sel = scores + rbias
    row = lax.broadcasted_iota(jnp.int32, sel.shape, 0)
    best = None
    for g in range(N_GROUPS):
        rows = [sel[g * E_PER_GROUP + i: g * E_PER_GROUP + i + 1, :] for i in range(E_PER_GROUP)]
        gs = _top2_sum4(*rows)
        if best is None:
            best, gidx = gs, jnp.zeros(gs.shape, jnp.int32)
        else:
            gidx = jnp.where(gs > best, g, gidx)
            best = jnp.maximum(best, gs)
    masked = jnp.where(row // E_PER_GROUP == gidx, sel, NEG_INF)
    m1 = jnp.max(masked, axis=0, keepdims=True)
    i1 = jnp.min(jnp.where(masked == m1, row, n_e), axis=0, keepdims=True)
    masked2 = jnp.where(row == i1, -3.0e38, masked)
    m2 = jnp.max(masked2, axis=0, keepdims=True)
    i2 = jnp.min(jnp.where(masked2 == m2, row, n_e), axis=0, keepdims=True)
    w1 = jnp.sum(jnp.where(row == i1, scores, 0.0), axis=0, keepdims=True)
    w2 = jnp.sum(jnp.where(row == i2, scores, 0.0), axis=0, keepdims=True)
    den = w1 + w2
    member = jnp.where((row == i1) | (row == i2), 1.0, 0.0).astype(BF16)
    before = jnp.where(lax.broadcasted_iota(jnp.int32, (n, n), 0) < lax.broadcasted_iota(jnp.int32, (n, n), 1),
                       1.0, 0.0).astype(BF16)
    pos = jnp.dot(member, before, preferred_element_type=F32)
    p1 = jnp.sum(jnp.where(row == i1, pos, 0.0), axis=0, keepdims=True)
    p2 = jnp.sum(jnp.where(row == i2, pos, 0.0), axis=0, keepdims=True)
    out_row = lax.broadcasted_iota(jnp.int32, (8, n), 0)
    out = jnp.zeros((8, n), F32)
    for r, val in enumerate((i1.astype(F32), i2.astype(F32), p1, p2, w1 / den, w2 / den)):
        out = jnp.where(out_row == r, val, out)
    return out


def _oproj_kernel(o_ref, wo_ref, x_ref, g1_ref, gn_ref, sc_ref, sh_ref, rwt_ref, rb_ref,
                  xo_ref, h_ref, route_ref):
    y = jnp.dot(o_ref[...], wo_ref[...], preferred_element_type=F32)
    xn = x_ref[...] + g1_ref[...] * y
    xo_ref[...] = xn
    h = _norm_mod(xn, gn_ref[...], sc_ref[...], sh_ref[...])
    h_hi = h.astype(BF16)
    h_ref[...] = h_hi
    h_lo = (h - h_hi.astype(F32)).astype(BF16)
    n_e = rb_ref.shape[0]
    rw = rwt_ref[...]
    main = lax.dot_general(rw, h_hi, _NT, preferred_element_type=F32)
    corr = lax.dot_general(rw[:n_e], h_lo, _NT, preferred_element_type=F32)
    route_ref[...] = _route(main[:n_e] + main[n_e:] + corr, rb_ref[...])


def _out_proj(o, wo_bf16, x2, g1, gn, sc, sh, rwt, rb, seq):
    t, d = x2.shape
    tm = ROW_TILE
    per_b = seq // tm
    n_e = rb.shape[0]
    vec = lambda i: (i // per_b, 0, 0)
    return pl.pallas_call(
        _oproj_kernel,
        out_shape=(jax.ShapeDtypeStruct((t, d), F32),
                   jax.ShapeDtypeStruct((t, d), BF16),
                   jax.ShapeDtypeStruct((8, t), F32)),
        grid=(t // tm,),
        in_specs=[
            pl.BlockSpec((tm, d), lambda i: (i, 0)),
            pl.BlockSpec((d, d), lambda i: (0, 0)),
            pl.BlockSpec((tm, d), lambda i: (i, 0)),
            pl.BlockSpec((None, 1, d), vec),
            pl.BlockSpec((1, d), lambda i: (0, 0)),
            pl.BlockSpec((None, 1, d), vec),
            pl.BlockSpec((None, 1, d), vec),
            pl.BlockSpec((2 * n_e, d), lambda i: (0, 0)),
            pl.BlockSpec((n_e, 1), lambda i: (0, 0)),
        ],
        out_specs=(pl.BlockSpec((tm, d), lambda i: (i, 0)),
                   pl.BlockSpec((tm, d), lambda i: (i, 0)),
                   pl.BlockSpec((8, tm), lambda i: (0, i))),
        compiler_params=_cparams(("parallel",)),
        name="out_proj_route",
    )(o, wo_bf16, x2, g1, gn, sc, sh, rwt, rb)


def _moe_plan(route, t, tm, n_e):
    n_tiles = t // tm
    ch, ft = MOE_CHUNK, MOE_FFN_TILE
    loc_rows, main_rows, n_sorted = _moe_rows(t, tm, n_e)
    ids = jnp.arange(n_e, dtype=jnp.int32)
    e = route[0:2].astype(jnp.int32)
    pos = route[2:4].astype(jnp.int32)
    oh = (e[:, :, None] == ids).astype(jnp.int32)
    cnt = oh.sum(0).reshape(n_tiles, tm, n_e).sum(1)
    seg = (cnt + ch - 1) // ch * ch
    loc = jnp.cumsum(seg, axis=1) - seg
    tot = seg.sum(0)
    totp = (tot + ft - 1) // ft * ft
    base = jnp.cumsum(totp) - totp
    gdest = base[None, :] + jnp.cumsum(seg, axis=0) - seg
    col = (oh * jnp.repeat(loc, tm, axis=0)[None]).sum(-1) + pos

    row0 = jnp.arange(loc_rows // ch, dtype=jnp.int32) * ch
    ej = (row0[None, :, None] >= (loc + seg)[:, None, :]).sum(-1)
    ohj = (jnp.minimum(ej, n_e - 1)[..., None] == ids).astype(jnp.int32)
    dst = (ohj * (gdest - loc)[:, None, :]).sum(-1) + row0[None, :]
    dump = main_rows + (jnp.arange(n_tiles, dtype=jnp.int32) % 2)[:, None] * loc_rows + row0[None, :]
    dst = jnp.where(ej < n_e, dst, dump) // ch

    r0 = jnp.arange(n_sorted // ft, dtype=jnp.int32) * ft
    ends = base + totp
    te = jnp.minimum((r0[:, None] >= ends[None, :]).sum(-1), n_e - 1)
    tvalid = (r0 < ends[-1]).astype(jnp.int32)
    tfirst = tvalid * (r0 == (((te[:, None] == ids) * base[None, :]).sum(-1))).astype(jnp.int32)
    return col, dst.astype(jnp.int32), te.astype(jnp.int32), tvalid, tfirst


def _moe_rows(t, tm, n_e):
    ch, ft = MOE_CHUNK, MOE_FFN_TILE
    loc_rows = 2 * tm + n_e * ch
    main = 2 * t + (t // tm) * n_e * (ch - 1) + n_e * (ft - 1)
    main = (main + ft - 1) // ft * ft
    return loc_rows, main, (main + 2 * loc_rows + ft - 1) // ft * ft


def _chunk_copies(dst_ref, tile, local_ref, sorted_ref, sem, to_sorted):
    ch = MOE_CHUNK
    copies = []
    for j in range(local_ref.shape[0] // ch):
        start = 0 if tile is None else pl.multiple_of(dst_ref[tile, j] * ch, ch)
        far = sorted_ref.at[pl.ds(start, ch)]
        near = local_ref.at[pl.ds(j * ch, ch)]
        copies.append(pltpu.make_async_copy(near, far, sem) if to_sorted
                      else pltpu.make_async_copy(far, near, sem))
    return copies


def _dispatch_kernel(dst_ref, col_ref, h_ref, init_ref, xs_ref, comp, sem):
    del init_ref
    tile = pl.program_id(0)
    last = pl.num_programs(0) - 1
    slot = tile % 2

    def wait_slot(s):
        for cp in _chunk_copies(dst_ref, None, comp.at[s], xs_ref, sem.at[s], True):
            cp.wait()

    @pl.when(tile >= 2)
    def _():
        wait_slot(slot)

    col = col_ref[...]
    srow = lax.broadcasted_iota(jnp.int32, (comp.shape[1], col.shape[1]), 0)
    sel = jnp.where((srow == col[0:1]) | (srow == col[1:2]), 1.0, 0.0).astype(BF16)
    comp[slot] = jnp.dot(sel, h_ref[...], preferred_element_type=F32).astype(BF16)
    for cp in _chunk_copies(dst_ref, tile, comp.at[slot], xs_ref, sem.at[slot], True):
        cp.start()

    @pl.when(tile == last)
    def _():
        wait_slot(slot)

    @pl.when((tile == last) & (tile >= 1))
    def _():
        wait_slot(1 - slot)


def _dispatch(dst, col, h, n_e):
    t, d = h.shape
    tm = ROW_TILE
    loc_rows, _, n_sorted = _moe_rows(t, tm, n_e)
    return pl.pallas_call(
        _dispatch_kernel,
        out_shape=jax.ShapeDtypeStruct((n_sorted, d), BF16),
        grid_spec=pltpu.PrefetchScalarGridSpec(
            num_scalar_prefetch=1,
            grid=(t // tm,),
            in_specs=[pl.BlockSpec((2, tm), lambda i, dst: (0, i)),
                      pl.BlockSpec((tm, d), lambda i, dst: (i, 0)),
                      pl.BlockSpec(memory_space=pl.ANY)],
            out_specs=pl.BlockSpec(memory_space=pl.ANY),
            scratch_shapes=[pltpu.VMEM((2, loc_rows, d), BF16), pltpu.SemaphoreType.DMA((2,))],
        ),
        input_output_aliases={3: 0},
        compiler_params=_cparams(("arbitrary",)),
        name="moe_dispatch",
    )(dst, col, h, jnp.zeros((n_sorted, d), BF16))


def _expert_kernel(te_ref, tv_ref, tf_ref, xs_ref, wg_ref, wu_ref, wd_ref, ys_ref, wg_b, wu_b, wd_b):
    del te_ref
    s = pl.program_id(0)

    @pl.when(tf_ref[s] == 1)
    def _():
        wg_b[...] = wg_ref[...].astype(BF16)
        wu_b[...] = wu_ref[...].astype(BF16)
        wd_b[...] = wd_ref[...].astype(BF16)

    @pl.when(tv_ref[s] == 1)
    def _():
        xs = xs_ref[...]
        hg = jnp.dot(xs, wg_b[...], preferred_element_type=F32)
        hu = jnp.dot(xs, wu_b[...], preferred_element_type=F32)
        he = hg * (1.0 / (1.0 + jnp.exp(-hg))) * hu
        ys_ref[...] = jnp.dot(he.astype(BF16), wd_b[...], preferred_element_type=F32).astype(BF16)

    @pl.when(tv_ref[s] == 0)
    def _():
        ys_ref[...] = jnp.zeros(ys_ref.shape, ys_ref.dtype)


def _experts(te, tvalid, tfirst, xs, wg, wu, wd, layer):
    n_sorted, d = xs.shape
    f = wg.shape[3]
    ft = MOE_FFN_TILE
    wspec = lambda r, c: pl.BlockSpec((None, None, r, c), lambda s, te, tv, tf: (layer, te[s], 0, 0))
    return pl.pallas_call(
        _expert_kernel,
        out_shape=jax.ShapeDtypeStruct((n_sorted, d), BF16),
        grid_spec=pltpu.PrefetchScalarGridSpec(
            num_scalar_prefetch=3,
            grid=(n_sorted // ft,),
            in_specs=[pl.BlockSpec((ft, d), lambda s, te, tv, tf: (s, 0)),
                      wspec(d, f), wspec(d, f), wspec(f, d)],
            out_specs=pl.BlockSpec((ft, d), lambda s, te, tv, tf: (s, 0)),
            scratch_shapes=[pltpu.VMEM((d, f), BF16), pltpu.VMEM((d, f), BF16), pltpu.VMEM((f, d), BF16)],
        ),
        compiler_params=_cparams(("arbitrary",)),
        name="moe_experts",
    )(te, tvalid, tfirst, xs, wg, wu, wd)


def _combine_kernel(dst_ref, col_ref, w_ref, ys_ref, x_ref, g2_ref, o_ref, comp, sem):
    tile = pl.program_id(0)
    slot = tile % 2

    def fetch(t, s):
        for cp in _chunk_copies(dst_ref, t, comp.at[s], ys_ref, sem.at[s], False):
            cp.start()

    @pl.when(tile == 0)
    def _():
        fetch(tile, slot)

    @pl.when(tile + 1 < pl.num_programs(0))
    def _():
        fetch(tile + 1, 1 - slot)

    col = col_ref[...]
    w = w_ref[...]
    scol = lax.broadcasted_iota(jnp.int32, (col.shape[0], comp.shape[1]), 1)
    sels = [jnp.where(scol == col[:, k:k + 1], 1.0, 0.0).astype(BF16) for k in range(2)]
    for cp in _chunk_copies(dst_ref, None, comp.at[slot], ys_ref, sem.at[slot], False):
        cp.wait()
    yc = comp[slot]
    acc = w[:, 4:5] * jnp.dot(sels[0], yc, preferred_element_type=F32)
    acc = acc + w[:, 5:6] * jnp.dot(sels[1], yc, preferred_element_type=F32)
    o_ref[...] = x_ref[...] + g2_ref[...] * acc


def _combine(dst, col_t, route_t, ys, x2, g2, seq, n_e):
    t, d = x2.shape
    tm = ROW_TILE
    per_b = seq // tm
    loc_rows = _moe_rows(t, tm, n_e)[0]
    return pl.pallas_call(
        _combine_kernel,
        out_shape=jax.ShapeDtypeStruct((t, d), F32),
        grid_spec=pltpu.PrefetchScalarGridSpec(
            num_scalar_prefetch=1,
            grid=(t // tm,),
            in_specs=[pl.BlockSpec((tm, 2), lambda i, dst: (i, 0)),
                      pl.BlockSpec((tm, 8), lambda i, dst: (i, 0)),
                      pl.BlockSpec(memory_space=pl.ANY),
                      pl.BlockSpec((tm, d), lambda i, dst: (i, 0)),
                      pl.BlockSpec((None, 1, d), lambda i, dst: (i // per_b, 0, 0))],
            out_specs=pl.BlockSpec((tm, d), lambda i, dst: (i, 0)),
            scratch_shapes=[pltpu.VMEM((2, loc_rows, d), BF16), pltpu.SemaphoreType.DMA((2,))],
        ),
        compiler_params=_cparams(("arbitrary",)),
        name="moe_combine",
    )(dst, col_t, route_t, ys, x2, g2)


def _moe(h, route, wg, wu, wd, layer, x2, g2, seq):
    t = x2.shape[0]
    n_e = wg.shape[1]
    col, dst, te, tvalid, tfirst = _moe_plan(route, t, ROW_TILE, n_e)
    xs = _dispatch(dst, col, h, n_e)
    ys = _experts(te, tvalid, tfirst, xs, wg, wu, wd, layer)
    return _combine(dst, col.T, route.T, ys, x2, g2, seq, n_e)


def kernel(x, c, ada_w, ada_b, norm_mix_g, norm_ffn_g, t5_bias, a_w_qkv, a_q_gain, a_k_gain, a_lambda, a_subln_g, a_w_o, b_w_qkv, b_q_gain, b_k_gain, b_rel_bias, b_w_o, router_w, router_bias, moe_w_gate, moe_w_up, moe_w_down):
    batch, seq, d = x.shape
    depth = ada_w.shape[0]
    assert seq % ATTN_TILE == 0 and seq % ROW_TILE == 0 and d == A_HEADS * 2 * A_HEAD_DIM
    assert d == B_HEADS * B_HEAD_DIM and A_HEAD_DIM == B_HEAD_DIM

    c_pad = jnp.zeros((8, d), F32).at[:batch].set(c.astype(F32))
    mod = _modulation(c_pad, ada_w.astype(F32), ada_b.astype(F32))[:, :batch]
    mod = mod.reshape(depth, batch, 6, 1, d)

    rwt = router_w.astype(F32).T
    rwt_hi = lax.reduce_precision(rwt, exponent_bits=8, mantissa_bits=7)
    rwt = jnp.concatenate([rwt_hi, rwt - rwt_hi], axis=0).astype(BF16)
    rb = router_bias.astype(F32).reshape(-1, 1)

    x2 = x.astype(F32).reshape(batch * seq, d)
    for i in range(depth):
        sh1, sc1, g1, sh2, sc2, g2 = [mod[i, :, k] for k in range(6)]
        j = i // 2
        if i % 2 == 0:
            w_qkv, qg, kg, w_o = a_w_qkv[j], a_q_gain[j], a_k_gain[j], a_w_o[j]
        else:
            w_qkv, qg, kg, w_o = b_w_qkv[j], b_q_gain[j], b_k_gain[j], b_w_o[j]
        n_rep = d // qg.shape[0]
        gq = (jnp.tile(qg.astype(F32), n_rep) * (A_HEAD_DIM ** -0.5 * LOG2E)).reshape(d, 1)
        gk = jnp.tile(kg.astype(F32), n_rep).reshape(1, d)
        wb = w_qkv.astype(BF16)
        qt, k, vt = _qkv_proj(x2, norm_mix_g[i].reshape(1, d), sc1, sh1,
                              wb[:, :d].T, wb[:, d:2 * d], wb[:, 2 * d:].T, gq, gk, seq, A_HEAD_DIM)
        if i % 2 == 0:
            lambda_init = 0.8 - 0.6 * math.exp(-0.3 * i)
            o = _attention_a(qt, k, vt, _t5_vectors(t5_bias, ATTN_TILE), a_lambda[j].astype(F32),
                             a_subln_g[j].reshape(1, -1), batch, seq, lambda_init)
        else:
            o = _attention_b(qt, k, vt, _band_vectors(b_rel_bias[j], BAND_TILE), batch, seq)
        x2, h, route = _out_proj(o, w_o.astype(BF16), x2, g1, norm_ffn_g[i].reshape(1, d),
                                 sc2, sh2, rwt, rb, seq)
        x2 = _moe(h, route, moe_w_gate, moe_w_up, moe_w_down, i, x2, g2, seq)
    return x2.reshape(batch, seq, d)
```

```python
import functools
import math

import numpy as np
import jax
import jax.numpy as jnp
from jax import lax
from jax.experimental import pallas as pl
from jax.experimental.pallas import tpu as pltpu

F32 = jnp.float32
BF16 = jnp.bfloat16

CHUNK = 64
A_HEADS = 8
A_HEAD_DIM = 64
T5_BUCKETS = 32
T5_MAX_DIST = 1024
B_HEADS = 16
B_HEAD_DIM = 64
LEFT_CHUNKS = 8
MAX_REL = 256
N_EXPERTS = 16
N_GROUPS = 4
E_PER_GROUP = N_EXPERTS // N_GROUPS
NORM_EPS = 1e-6
NEG_INF = -1e30
LOG2E = math.log2(math.e)

V7X_LANES = 128
V7X_MXU_DIM = 256

ATTN_TILE = LEFT_CHUNKS * CHUNK
FAR_UNROLL = 4
NEAR_AHEAD = 2
BAND_TILE = 256
BAND_QBLOCKS = 4
ROW_TILE = 512
MOE_TILE = 512
OPROJ_SPLIT = 4
MOE_CHUNK = 16
MOE_FFN_TILE = 512
MOE_FFN_SPLIT = 2
VMEM_LIMIT = 56 * 1024 * 1024

_NT = (((1,), (1,)), ((), ()))


def _cparams(sem):
    return pltpu.CompilerParams(dimension_semantics=sem, vmem_limit_bytes=VMEM_LIMIT)


def _mod_kernel(c_ref, w_ref, b_ref, o_ref):
    c = c_ref[...]
    s = c * (1.0 / (1.0 + jnp.exp(-c)))
    o_ref[...] = jnp.dot(s, w_ref[...], preferred_element_type=F32,
                         precision=lax.Precision.HIGHEST) + b_ref[...]


def _modulation(c_pad, ada_w, ada_b):
    depth, d, n = ada_w.shape
    rows = c_pad.shape[0]
    tn = 1536
    return pl.pallas_call(
        _mod_kernel,
        out_shape=jax.ShapeDtypeStruct((depth, rows, n), F32),
        grid=(depth, n // tn),
        in_specs=[
            pl.BlockSpec((rows, d), lambda i, j: (0, 0)),
            pl.BlockSpec((None, d, tn), lambda i, j: (i, 0, j)),
            pl.BlockSpec((None, 1, tn), lambda i, j: (i, 0, j)),
        ],
        out_specs=pl.BlockSpec((None, rows, tn), lambda i, j: (i, 0, j)),
        compiler_params=_cparams(("parallel", "parallel")),
        name="adaln_mod",
    )(c_pad, ada_w, ada_b.reshape(depth, 1, n))


def _norm_mod(x, g, sc, sh):
    ms = jnp.mean(x * x, axis=-1, keepdims=True)
    return x * lax.rsqrt(ms + NORM_EPS) * g * (1.0 + sc) + sh


def _qkv_kernel(x_ref, g_ref, sc_ref, sh_ref, wqt_ref, wk_ref, wvt_ref, gq_ref, gk_ref,
                qt_ref, k_ref, vt_ref, *, head_dim):
    h = _norm_mod(x_ref[...], g_ref[...], sc_ref[...], sh_ref[...]).astype(BF16)
    d, tm = qt_ref.shape
    y = lax.dot_general(wqt_ref[...], h, _NT, preferred_element_type=F32)
    y3 = y.reshape(d // head_dim, head_dim, tm)
    ss = jnp.mean(y3 * y3, axis=1, keepdims=True)
    qt_ref[...] = ((y3 * lax.rsqrt(ss + NORM_EPS)).reshape(d, tm) * gq_ref[...]).astype(BF16)
    cw = V7X_MXU_DIM
    r = lax.broadcasted_iota(jnp.int32, (cw, cw), 0) // head_dim
    c = lax.broadcasted_iota(jnp.int32, (cw, cw), 1) // head_dim
    gmat = jnp.where(r == c, 1.0 / head_dim, 0.0).astype(BF16)
    y = jnp.dot(h, wk_ref[...], preferred_element_type=F32)
    ysq = (y * y).astype(BF16)
    for ci in range(d // cw):
        cols = slice(ci * cw, (ci + 1) * cw)
        ss = jnp.dot(ysq[:, cols], gmat, preferred_element_type=F32)
        k_ref[:, cols] = (y[:, cols] * lax.rsqrt(ss + NORM_EPS) * gk_ref[:, cols]).astype(BF16)
    vt_ref[...] = lax.dot_general(wvt_ref[...], h, _NT, preferred_element_type=F32).astype(BF16)


def _qkv_proj(x2, g, sc, sh, wqt, wk, wvt, gq, gk, seq, head_dim):
    t, d = x2.shape
    tm = ROW_TILE
    per_b = seq // tm
    vec = lambda i: (i // per_b, 0, 0)
    full = lambda i: (0, 0)
    return pl.pallas_call(
        functools.partial(_qkv_kernel, head_dim=head_dim),
        out_shape=(jax.ShapeDtypeStruct((d, t), BF16),
                   jax.ShapeDtypeStruct((t, d), BF16),
                   jax.ShapeDtypeStruct((d, t), BF16)),
        grid=(t // tm,),
        in_specs=[
            pl.BlockSpec((tm, d), lambda i: (i, 0)),
            pl.BlockSpec((1, d), full),
            pl.BlockSpec((None, 1, d), vec),
            pl.BlockSpec((None, 1, d), vec),
            pl.BlockSpec((d, d), full),
            pl.BlockSpec((d, d), full),
            pl.BlockSpec((d, d), full),
            pl.BlockSpec((d, 1), full),
            pl.BlockSpec((1, d), full),
        ],
        out_specs=(pl.BlockSpec((d, tm), lambda i: (0, i)),
                   pl.BlockSpec((tm, d), lambda i: (i, 0)),
                   pl.BlockSpec((d, tm), lambda i: (0, i))),
        compiler_params=_cparams(("parallel",)),
        name="qkv_proj",
    )(x2, g, sc, sh, wqt, wk, wvt, gq, gk)


def _softmax_pv(s, vt1, m_ref, acc_ref):
    m_prev = m_ref[...]
    m_new = jnp.maximum(m_prev, jnp.max(s, axis=0, keepdims=True))
    alpha = jnp.exp2(m_prev - m_new)
    p = jnp.exp2(s - m_new).astype(BF16)
    acc_ref[...] = alpha * acc_ref[...] + jnp.dot(vt1, p, preferred_element_type=F32)
    m_ref[...] = m_new


def _split_maps(qt, head_dim):
    row = lax.broadcasted_iota(jnp.int32, qt.shape, 0)
    zero = jnp.zeros_like(qt)
    return jnp.where(row < head_dim, qt, zero), jnp.where(row >= head_dim, qt, zero)


ONES_ROWS = 16


def _init_state(refs):
    for m_ref, acc_ref in refs:
        m_ref[...] = jnp.full(m_ref.shape, NEG_INF, F32)
        acc_ref[...] = jnp.zeros(acc_ref.shape, F32)


def _normalized(acc_ref):
    acc = acc_ref[...]
    return acc[:V7X_LANES] / acc[V7X_LANES:V7X_LANES + 1]


def _toeplitz(x_row, n):
    x = jnp.broadcast_to(x_row, (n, x_row.shape[1]))
    return pltpu.roll(x, 0, 1, stride=1, stride_axis=0)[:, :n]


def _chunk_delta(n):
    kc = lax.broadcasted_iota(jnp.int32, (n, n), 0) // CHUNK
    qc = lax.broadcasted_iota(jnp.int32, (n, n), 1) // CHUNK
    return qc - kc


def _attn_a_kernel(qt_ref, k_ref, vt_ref, x_ref, lam_ref, sub_ref, o_ref,
                   tab, fbuf, nbuf, m_st, acc_st, *, n_off, lambda_init):
    tq = ATTN_TILE
    step = pl.program_id(2)

    @pl.when(step == 0)
    def _():
        for o in range(n_off):
            t = _toeplitz(x_ref[o], tq) * LOG2E
            if o == 0:
                t = jnp.where(_chunk_delta(tq) >= 0, t, NEG_INF)
            tab[o] = t

    for c in range(2):
        _init_state(((m_st.at[c, 0], acc_st.at[c, 0]), (m_st.at[c, 1], acc_st.at[c, 1])))
    q_maps = [_split_maps(qt_ref[:, c * tq:(c + 1) * tq], A_HEAD_DIM) for c in range(2)]

    def k_tile(j):
        return k_ref[pl.ds(pl.multiple_of(j * tq, tq), tq), :]

    def vt1_tile(j):
        vt = vt_ref[:, pl.ds(pl.multiple_of(j * tq, tq), tq)]
        return jnp.concatenate([vt, jnp.ones((ONES_ROWS, tq), vt.dtype)], axis=0)

    def scores(buf, k, c):
        for m in range(2):
            buf[c, m] = jnp.dot(k, q_maps[c][m], preferred_element_type=F32)

    def consume(buf, vt1, c, o):
        for m in range(2):
            s = buf[c, m]
            _softmax_pv(s if o >= n_off else s + tab[o], vt1, m_st.at[c, m], acc_st.at[c, m])

    def near_block(first_tile, n_tiles, first_scored):
        units = [[c for c in range(2) if n_tiles - 2 + c - s >= 0] for s in range(n_tiles)]
        buf_of = lambda s: fbuf.at[0] if s == 0 else nbuf.at[s - 1]

        def tile_scores(s):
            k = k_tile(first_tile + s)
            for c in units[s]:
                scores(buf_of(s), k, c)

        if not first_scored:
            tile_scores(0)
        for s in range(1, min(NEAR_AHEAD, n_tiles)):
            tile_scores(s)
        for s in range(n_tiles):
            if s + NEAR_AHEAD < n_tiles:
                tile_scores(s + NEAR_AHEAD)
            vt1 = vt1_tile(first_tile + s)
            for c in units[s]:
                consume(buf_of(s), vt1, c, n_tiles - 2 + c - s)

    first_general = (n_off - 1) // 2
    for i0 in range(first_general):
        @pl.when(step == i0)
        def _():
            near_block(0, 2 * i0 + 2, False)

    def far_step(j, parity):
        k = k_tile(j + 1)
        for c in range(2):
            scores(fbuf.at[1 - parity], k, c)
        vt1 = vt1_tile(j)
        for c in range(2):
            consume(fbuf.at[parity], vt1, c, n_off)

    @pl.when(step >= first_general)
    def _():
        k = k_tile(0)
        for c in range(2):
            scores(fbuf.at[0], k, c)

    n_far = jnp.maximum(2 * step - (n_off - 1), 0)
    rem = n_far % FAR_UNROLL

    @pl.when(rem >= 2)
    def _():
        far_step(0, 0)
        far_step(1, 1)

    def far_trip(t, carry):
        for u in range(FAR_UNROLL):
            far_step(FAR_UNROLL * t + rem + u, u % 2)
        return carry
    lax.fori_loop(0, n_far // FAR_UNROLL, far_trip, 0)

    @pl.when(step >= first_general)
    def _():
        near_block(n_far, n_off + 1, True)

    lam = lam_ref[...]
    lam_full = (jnp.exp(jnp.sum(lam[0:1] * lam[1:2], axis=-1, keepdims=True))
                - jnp.exp(jnp.sum(lam[2:3] * lam[3:4], axis=-1, keepdims=True)) + lambda_init)
    for c in range(2):
        a = _normalized(acc_st.at[c, 0]) - lam_full * _normalized(acc_st.at[c, 1])
        ms = jnp.mean(a * a, axis=0, keepdims=True)
        an = a * lax.rsqrt(ms + NORM_EPS)
        o_ref[c * tq:(c + 1) * tq, :] = (an.T * (sub_ref[...] * (1.0 - lambda_init))).astype(BF16)


def _attn_b_kernel(qt_ref, k_ref, vt_ref, x_ref, o_ref, tab, sbuf):
    tq = BAND_TILE
    n_blk = qt_ref.shape[1] // tq
    n_off = tab.shape[0] // 2
    step = pl.program_id(2)

    @pl.when(step == 0)
    def _():
        for m in range(2):
            for o in range(n_off):
                d = _chunk_delta(tq) + o * (tq // CHUNK)
                t = _toeplitz(x_ref[n_off * m + o], tq) * LOG2E
                tab[n_off * m + o] = jnp.where((d >= 0) & (d <= LEFT_CHUNKS), t, NEG_INF)

    q_maps = [_split_maps(qt_ref[:, c * tq:(c + 1) * tq], B_HEAD_DIM) for c in range(n_blk)]

    def run(first_step):
        pairs = [(c, o) for c in range(n_blk) for o in range(n_off - 1, -1, -1)
                 if not first_step or c - o >= 0]
        k_tiles, vt_tiles = {}, {}
        for c, o in pairs:
            if c - o not in k_tiles:
                ks = pl.multiple_of((n_blk * step + c - o) * tq, tq)
                k_tiles[c - o] = k_ref[pl.ds(ks, tq), :]
                vt = vt_ref[:, pl.ds(ks, tq)]
                vt_tiles[c - o] = jnp.concatenate([vt, jnp.ones((ONES_ROWS, tq), vt.dtype)], axis=0)
        def tile_scores(c):
            for o in [o for cc, o in pairs if cc == c]:
                for m in range(2):
                    sbuf[n_off * c + o, m] = jnp.dot(k_tiles[c - o], q_maps[c][m], preferred_element_type=F32)

        for c in range(min(NEAR_AHEAD, n_blk)):
            tile_scores(c)
        for c in range(n_blk):
            if c + NEAR_AHEAD < n_blk:
                tile_scores(c + NEAR_AHEAD)
            outs = []
            for m in range(2):
                offs = [o for cc, o in pairs if cc == c]
                ss = [sbuf[n_off * c + o, m] + tab[n_off * m + o] for o in offs]
                mx = functools.reduce(jnp.maximum, [jnp.max(s, axis=0, keepdims=True) for s in ss])
                acc = sum(jnp.dot(vt_tiles[c - o], jnp.exp2(s - mx).astype(BF16), preferred_element_type=F32)
                          for o, s in zip(offs, ss))
                outs.append(acc[:V7X_LANES] / acc[V7X_LANES:V7X_LANES + 1])
            row = lax.broadcasted_iota(jnp.int32, outs[0].shape, 0)
            o_ref[c * tq:(c + 1) * tq, :] = jnp.where(row < B_HEAD_DIM, outs[0], outs[1]).T.astype(BF16)

    @pl.when(step == 0)
    def _():
        run(True)

    @pl.when(step >= 1)
    def _():
        run(False)


def _attn_specs(batch, seq, nq, tq):
    return dict(
        q=pl.BlockSpec((V7X_LANES, tq), lambda b, h, i: (h, b * nq + i)),
        k=pl.BlockSpec((seq, V7X_LANES), lambda b, h, i: (b, h)),
        v=pl.BlockSpec((V7X_LANES, seq), lambda b, h, i: (h, b)),
        o=pl.BlockSpec((tq, V7X_LANES), lambda b, h, i: (b * nq + i, h)),
    )


def _attention_a(qt, k, vt, xvec, lam, sub_g, batch, seq, lambda_init):
    d, t = qt.shape
    tq = ATTN_TILE
    n_off = xvec.shape[1]
    assert n_off % 2 == 1 and seq % (2 * tq) == 0
    nq = seq // (2 * tq)
    sp = _attn_specs(batch, seq, nq, 2 * tq)
    rows = V7X_LANES + ONES_ROWS
    return pl.pallas_call(
        functools.partial(_attn_a_kernel, n_off=n_off, lambda_init=lambda_init),
        out_shape=jax.ShapeDtypeStruct((t, d), BF16),
        grid=(batch, d // V7X_LANES, nq),
        in_specs=[sp["q"], sp["k"], sp["v"],
                  pl.BlockSpec((None,) + xvec.shape[1:], lambda b, h, i: (h, 0, 0, 0)),
                  pl.BlockSpec(lam.shape, lambda b, h, i: (0, 0)),
                  pl.BlockSpec(sub_g.shape, lambda b, h, i: (0, 0))],
        out_specs=sp["o"],
        scratch_shapes=[pltpu.VMEM((n_off, tq, tq), F32), pltpu.VMEM((2, 2, 2, tq, tq), F32),
                        pltpu.VMEM((n_off, 2, 2, tq, tq), F32),
                        pltpu.VMEM((2, 2, 1, tq), F32), pltpu.VMEM((2, 2, rows, tq), F32)],
        compiler_params=_cparams(("parallel", "parallel", "arbitrary")),
        name="diff_attention",
    )(qt, k, vt, xvec, lam, sub_g)


def _attention_b(qt, k, vt, xvec, batch, seq):
    d, t = qt.shape
    tq = BAND_TILE
    g = BAND_QBLOCKS
    n_off = xvec.shape[1] // 2
    assert g >= n_off - 1 and seq % (g * tq) == 0
    nq = seq // (g * tq)
    sp = _attn_specs(batch, seq, nq, g * tq)
    rows = V7X_LANES + ONES_ROWS
    return pl.pallas_call(
        _attn_b_kernel,
        out_shape=jax.ShapeDtypeStruct((t, d), BF16),
        grid=(batch, d // V7X_LANES, nq),
        in_specs=[sp["q"], sp["k"], sp["v"],
                  pl.BlockSpec((None,) + xvec.shape[1:], lambda b, h, i: (h, 0, 0, 0))],
        out_specs=sp["o"],
        scratch_shapes=[pltpu.VMEM((2 * n_off, tq, tq), F32), pltpu.VMEM((n_off * g, 2, tq, tq), F32)],
        compiler_params=_cparams(("parallel", "parallel", "arbitrary")),
        name="chunk_attention",
    )(qt, k, vt, xvec)


def _t5_bucket(rel):
    nb = T5_BUCKETS // 2
    ret = jnp.where(rel > 0, nb, 0)
    n = jnp.abs(rel)
    max_exact = nb // 2
    nf = jnp.maximum(n, 1).astype(F32)
    large = max_exact + (jnp.log(nf / max_exact) / math.log(T5_MAX_DIST / max_exact)
                         * (nb - max_exact)).astype(jnp.int32)
    large = jnp.minimum(large, nb - 1)
    return ret + jnp.where(n < max_exact, n, large)


def _t5_const_distance():
    nb = T5_BUCKETS // 2
    max_exact = nb // 2
    n = np.arange(max_exact, 4 * T5_MAX_DIST, dtype=np.float64)
    large = max_exact + np.floor(np.log(n / max_exact) / math.log(T5_MAX_DIST / max_exact) * (nb - max_exact))
    below = np.nonzero(large < nb - 1)[0]
    return int(n[below[-1]]) + 2


def _tile_rel(tile, n_off):
    i = jnp.arange(2 * tile, dtype=jnp.int32)
    rel = jnp.where(i < tile, -i, 2 * tile - i)
    return rel[None, :] - tile * jnp.arange(n_off, dtype=jnp.int32)[:, None]


def _t5_vectors(t5_bias, tile):
    n_off = 1
    while (n_off - 1) * tile + 1 < _t5_const_distance():
        n_off += 1
    tb = t5_bias.astype(F32)
    vals = tb[_t5_bucket(_tile_rel(tile, n_off))] - tb[T5_BUCKETS // 2 - 1]
    return vals.transpose(2, 0, 1)[:, :, None, :]


def _band_vectors(rel_bias, tile):
    n_off = LEFT_CHUNKS * CHUNK // tile + 1
    idx = jnp.clip(_tile_rel(tile, n_off), -MAX_REL, MAX_REL) + MAX_REL
    vals = rel_bias.astype(F32)[:, idx]
    return vals.reshape(rel_bias.shape[0] // 2, 2 * n_off, 1, 2 * tile)


def _top2_sum4(r0, r1, r2, r3):
    a, b = jnp.maximum(r0, r1), jnp.minimum(r0, r1)
    c, d = jnp.maximum(r2, r3), jnp.minimum(r2, r3)
    return jnp.maximum(a, c) + jnp.maximum(jnp.minimum(a, c), jnp.maximum(b, d))


def _route(logits, rbias):
    n_e, n = logits.shape
    scores = 1.0 / (1.0 + jnp.exp(-logits))
    sel = scores + rbias
    row = lax.broadcasted_iota(jnp.int32, sel.shape, 0)
    best = None
    for g in range(N_GROUPS):
        rows = [sel[g * E_PER_GROUP + i: g * E_PER_GROUP + i + 1, :] for i in range(E_PER_GROUP)]
        gs = _top2_sum4(*rows)
        if best is None:
            best, gidx = gs, jnp.zeros(gs.shape, jnp.int32)
        else:
            gidx = jnp.where(gs > best, g, gidx)
            best = jnp.maximum(best, gs)
    masked = jnp.where(row // E_PER_GROUP == gidx, sel, NEG_INF)
    m1 = jnp.max(masked, axis=0, keepdims=True)
    i1 = jnp.min(jnp.where(masked == m1, row, n_e), axis=0, keepdims=True)
    masked2 = jnp.where(row == i1, -3.0e38, masked)
    m2 = jnp.max(masked2, axis=0, keepdims=True)
    i2 = jnp.min(jnp.where(masked2 == m2, row, n_e), axis=0, keepdims=True)
    w1 = jnp.sum(jnp.where(row == i1, scores, 0.0), axis=0, keepdims=True)
    w2 = jnp.sum(jnp.where(row == i2, scores, 0.0), axis=0, keepdims=True)
    den = w1 + w2
    member = jnp.where((row == i1) | (row == i2), 1.0, 0.0).astype(BF16)
    t_from = lax.broadcasted_iota(jnp.int32, (n, n), 0)
    t_to = lax.broadcasted_iota(jnp.int32, (n, n), 1)
    before = jnp.where((t_from < t_to) & (t_from // MOE_TILE == t_to // MOE_TILE), 1.0, 0.0).astype(BF16)
    pos = jnp.dot(member, before, preferred_element_type=F32)
    p1 = jnp.sum(jnp.where(row == i1, pos, 0.0), axis=0, keepdims=True)
    p2 = jnp.sum(jnp.where(row == i2, pos, 0.0), axis=0, keepdims=True)
    out_row = lax.broadcasted_iota(jnp.int32, (8, n), 0)
    out = jnp.zeros((8, n), F32)
    for r, val in enumerate((i1.astype(F32), i2.astype(F32), p1, p2, w1 / den, w2 / den)):
        out = jnp.where(out_row == r, val, out)
    return out


def _oproj_kernel(o_ref, wo_ref, x_ref, g1_ref, gn_ref, sc_ref, sh_ref, rwt_ref, rb_ref,
                  xo_ref, h_ref, route_ref):
    n_e = rb_ref.shape[0]
    rw = rwt_ref[...]
    rows = o_ref.shape[0] // OPROJ_SPLIT
    groups = [slice(r * rows, (r + 1) * rows) for r in range(OPROJ_SPLIT)]
    ys = [jnp.dot(o_ref[g, :], wo_ref[...], preferred_element_type=F32) for g in groups]
    logits = []
    for g, y in zip(groups, ys):
        xn = x_ref[g, :] + g1_ref[...] * y
        xo_ref[g, :] = xn
        h = _norm_mod(xn, gn_ref[...], sc_ref[...], sh_ref[...])
        h_hi = h.astype(BF16)
        h_ref[g, :] = h_hi
        h_lo = (h - h_hi.astype(F32)).astype(BF16)
        main = lax.dot_general(rw, h_hi, _NT, preferred_element_type=F32)
        corr = lax.dot_general(rw[:n_e], h_lo, _NT, preferred_element_type=F32)
        logits.append(main[:n_e] + main[n_e:] + corr)
    route_ref[...] = _route(jnp.concatenate(logits, axis=1), rb_ref[...])


def _out_proj(o, wo_bf16, x2, g1, gn, sc, sh, rwt, rb, seq):
    t, d = x2.shape
    tm = ROW_TILE
    per_b = seq // tm
    n_e = rb.shape[0]
    vec = lambda i: (i // per_b, 0, 0)
    return pl.pallas_call(
        _oproj_kernel,
        out_shape=(jax.ShapeDtypeStruct((t, d), F32),
                   jax.ShapeDtypeStruct((t, d), BF16),
                   jax.ShapeDtypeStruct((8, t), F32)),
        grid=(t // tm,),
        in_specs=[
            pl.BlockSpec((tm, d), lambda i: (i, 0)),
            pl.BlockSpec((d, d), lambda i: (0, 0)),
            pl.BlockSpec((tm, d), lambda i: (i, 0)),
            pl.BlockSpec((None, 1, d), vec),
            pl.BlockSpec((1, d), lambda i: (0, 0)),
            pl.BlockSpec((None, 1, d), vec),
            pl.BlockSpec((None, 1, d), vec),
            pl.BlockSpec((2 * n_e, d), lambda i: (0, 0)),
            pl.BlockSpec((n_e, 1), lambda i: (0, 0)),
        ],
        out_specs=(pl.BlockSpec((tm, d), lambda i: (i, 0)),
                   pl.BlockSpec((tm, d), lambda i: (i, 0)),
                   pl.BlockSpec((8, tm), lambda i: (0, i))),
        compiler_params=_cparams(("parallel",)),
        name="out_proj_route",
    )(o, wo_bf16, x2, g1, gn, sc, sh, rwt, rb)


def _moe_plan(route, t, tm, n_e):
    n_tiles = t // tm
    ch, ft = MOE_CHUNK, MOE_FFN_TILE
    loc_rows, main_rows, n_sorted = _moe_rows(t, tm, n_e)
    ids = jnp.arange(n_e, dtype=jnp.int32)
    e = route[0:2].astype(jnp.int32)
    pos = route[2:4].astype(jnp.int32)
    oh = (e[:, :, None] == ids).astype(jnp.int32)
    cnt = oh.sum(0).reshape(n_tiles, tm, n_e).sum(1)
    seg = (cnt + ch - 1) // ch * ch
    loc = jnp.cumsum(seg, axis=1) - seg
    tot = seg.sum(0)
    totp = (tot + ft - 1) // ft * ft
    base = jnp.cumsum(totp) - totp
    gdest = base[None, :] + jnp.cumsum(seg, axis=0) - seg
    col = (oh * jnp.repeat(loc, tm, axis=0)[None]).sum(-1) + pos

    row0 = jnp.arange(loc_rows // ch, dtype=jnp.int32) * ch
    ej = (row0[None, :, None] >= (loc + seg)[:, None, :]).sum(-1)
    ohj = (jnp.minimum(ej, n_e - 1)[..., None] == ids).astype(jnp.int32)
    dst = (ohj * (gdest - loc)[:, None, :]).sum(-1) + row0[None, :]
    dump = main_rows + (jnp.arange(n_tiles, dtype=jnp.int32) % 2)[:, None] * loc_rows + row0[None, :]
    dst = jnp.where(ej < n_e, dst, dump) // ch

    r0 = jnp.arange(n_sorted // ft, dtype=jnp.int32) * ft
    ends = base + totp
    te = jnp.minimum((r0[:, None] >= ends[None, :]).sum(-1), n_e - 1)
    tvalid = (r0 < ends[-1]).astype(jnp.int32)
    tfirst = tvalid * (r0 == (((te[:, None] == ids) * base[None, :]).sum(-1))).astype(jnp.int32)
    return col, dst.astype(jnp.int32), te.astype(jnp.int32), tvalid, tfirst


def _moe_rows(t, tm, n_e):
    ch, ft = MOE_CHUNK, MOE_FFN_TILE
    loc_rows = 2 * tm + n_e * ch
    main = 2 * t + (t // tm) * n_e * (ch - 1) + n_e * (ft - 1)
    main = (main + ft - 1) // ft * ft
    return loc_rows, main, (main + 2 * loc_rows + ft - 1) // ft * ft


def _chunk_copies(dst_ref, tile, local_ref, sorted_ref, sem, to_sorted):
    ch = MOE_CHUNK
    copies = []
    for j in range(local_ref.shape[0] // ch):
        start = 0 if tile is None else pl.multiple_of(dst_ref[tile, j] * ch, ch)
        far = sorted_ref.at[pl.ds(start, ch)]
        near = local_ref.at[pl.ds(j * ch, ch)]
        copies.append(pltpu.make_async_copy(near, far, sem) if to_sorted
                      else pltpu.make_async_copy(far, near, sem))
    return copies


def _dispatch_kernel(dst_ref, col_ref, h_ref, init_ref, xs_ref, comp, sem):
    del init_ref
    tile = pl.program_id(0)
    last = pl.num_programs(0) - 1
    slot = tile % 2

    def wait_slot(s):
        for cp in _chunk_copies(dst_ref, None, comp.at[s], xs_ref, sem.at[s], True):
            cp.wait()

    @pl.when(tile >= 2)
    def _():
        wait_slot(slot)

    col = col_ref[...]
    srow = lax.broadcasted_iota(jnp.int32, (comp.shape[1], col.shape[1]), 0)
    sel = jnp.where((srow == col[0:1]) | (srow == col[1:2]), 1.0, 0.0).astype(BF16)
    comp[slot] = jnp.dot(sel, h_ref[...], preferred_element_type=F32).astype(BF16)
    for cp in _chunk_copies(dst_ref, tile, comp.at[slot], xs_ref, sem.at[slot], True):
        cp.start()

    @pl.when(tile == last)
    def _():
        wait_slot(slot)

    @pl.when((tile == last) & (tile >= 1))
    def _():
        wait_slot(1 - slot)


def _dispatch(dst, col, h, n_e):
    t, d = h.shape
    tm = MOE_TILE
    loc_rows, _, n_sorted = _moe_rows(t, tm, n_e)
    return pl.pallas_call(
        _dispatch_kernel,
        out_shape=jax.ShapeDtypeStruct((n_sorted, d), BF16),
        grid_spec=pltpu.PrefetchScalarGridSpec(
            num_scalar_prefetch=1,
            grid=(t // tm,),
            in_specs=[pl.BlockSpec((2, tm), lambda i, dst: (0, i)),
                      pl.BlockSpec((tm, d), lambda i, dst: (i, 0)),
                      pl.BlockSpec(memory_space=pl.ANY)],
            out_specs=pl.BlockSpec(memory_space=pl.ANY),
            scratch_shapes=[pltpu.VMEM((2, loc_rows, d), BF16), pltpu.SemaphoreType.DMA((2,))],
        ),
        input_output_aliases={3: 0},
        compiler_params=_cparams(("arbitrary",)),
        name="moe_dispatch",
    )(dst, col, h, jnp.zeros((n_sorted, d), BF16))


def _expert_kernel(te_ref, tv_ref, tf_ref, xs_ref, wg_ref, wu_ref, wd_ref, ys_ref, wg_b, wu_b, wd_b):
    del te_ref
    s = pl.program_id(0)

    @pl.when(tf_ref[s] == 1)
    def _():
        wg_b[...] = wg_ref[...].astype(BF16)
        wu_b[...] = wu_ref[...].astype(BF16)
        wd_b[...] = wd_ref[...].astype(BF16)

    @pl.when(tv_ref[s] == 1)
    def _():
        rows = ys_ref.shape[0] // MOE_FFN_SPLIT
        hs = []
        for r in range(MOE_FFN_SPLIT):
            xs = xs_ref[r * rows:(r + 1) * rows, :]
            hg = jnp.dot(xs, wg_b[...], preferred_element_type=F32)
            hu = jnp.dot(xs, wu_b[...], preferred_element_type=F32)
            hs.append((hg, hu))
        for r, (hg, hu) in enumerate(hs):
            he = hg * (1.0 / (1.0 + jnp.exp(-hg))) * hu
            ys_ref[r * rows:(r + 1) * rows, :] = jnp.dot(
                he.astype(BF16), wd_b[...], preferred_element_type=F32).astype(BF16)

    @pl.when(tv_ref[s] == 0)
    def _():
        ys_ref[...] = jnp.zeros(ys_ref.shape, ys_ref.dtype)


def _experts(te, tvalid, tfirst, xs, wg, wu, wd, layer):
    n_sorted, d = xs.shape
    f = wg.shape[3]
    ft = MOE_FFN_TILE
    wspec = lambda r, c: pl.BlockSpec((None, None, r, c), lambda s, te, tv, tf: (layer, te[s], 0, 0))
    return pl.pallas_call(
        _expert_kernel,
        out_shape=jax.ShapeDtypeStruct((n_sorted, d), BF16),
        grid_spec=pltpu.PrefetchScalarGridSpec(
            num_scalar_prefetch=3,
            grid=(n_sorted // ft,),
            in_specs=[pl.BlockSpec((ft, d), lambda s, te, tv, tf: (s, 0)),
                      wspec(d, f), wspec(d, f), wspec(f, d)],
            out_specs=pl.BlockSpec((ft, d), lambda s, te, tv, tf: (s, 0)),
            scratch_shapes=[pltpu.VMEM((d, f), BF16), pltpu.VMEM((d, f), BF16), pltpu.VMEM((f, d), BF16)],
        ),
        compiler_params=_cparams(("arbitrary",)),
        name="moe_experts",
    )(te, tvalid, tfirst, xs, wg, wu, wd)


def _combine_kernel(dst_ref, col_ref, w_ref, ys_ref, x_ref, g2_ref, o_ref, comp, sem):
    tile = pl.program_id(0)
    slot = tile % 2

    def fetch(t, s):
        for cp in _chunk_copies(dst_ref, t, comp.at[s], ys_ref, sem.at[s], False):
            cp.start()

    @pl.when(tile == 0)
    def _():
        fetch(tile, slot)

    @pl.when(tile + 1 < pl.num_programs(0))
    def _():
        fetch(tile + 1, 1 - slot)

    col = col_ref[...]
    w = w_ref[...]
    scol = lax.broadcasted_iota(jnp.int32, (col.shape[0], comp.shape[1]), 1)
    sels = [jnp.where(scol == col[:, k:k + 1], 1.0, 0.0).astype(BF16) for k in range(2)]
    for cp in _chunk_copies(dst_ref, None, comp.at[slot], ys_ref, sem.at[slot], False):
        cp.wait()
    yc = comp[slot]
    acc = w[:, 4:5] * jnp.dot(sels[0], yc, preferred_element_type=F32)
    acc = acc + w[:, 5:6] * jnp.dot(sels[1], yc, preferred_element_type=F32)
    o_ref[...] = x_ref[...] + g2_ref[...] * acc


def _combine(dst, col_t, route_t, ys, x2, g2, seq, n_e):
    t, d = x2.shape
    tm = MOE_TILE
    per_b = seq // tm
    loc_rows = _moe_rows(t, tm, n_e)[0]
    return pl.pallas_call(
        _combine_kernel,
        out_shape=jax.ShapeDtypeStruct((t, d), F32),
        grid_spec=pltpu.PrefetchScalarGridSpec(
            num_scalar_prefetch=1,
            grid=(t // tm,),
            in_specs=[pl.BlockSpec((tm, 2), lambda i, dst: (i, 0)),
                      pl.BlockSpec((tm, 8), lambda i, dst: (i, 0)),
                      pl.BlockSpec(memory_space=pl.ANY),
                      pl.BlockSpec((tm, d), lambda i, dst: (i, 0)),
                      pl.BlockSpec((None, 1, d), lambda i, dst: (i // per_b, 0, 0))],
            out_specs=pl.BlockSpec((tm, d), lambda i, dst: (i, 0)),
            scratch_shapes=[pltpu.VMEM((2, loc_rows, d), BF16), pltpu.SemaphoreType.DMA((2,))],
        ),
        compiler_params=_cparams(("arbitrary",)),
        name="moe_combine",
    )(dst, col_t, route_t, ys, x2, g2)


def _moe(h, route, wg, wu, wd, layer, x2, g2, seq):
    t = x2.shape[0]
    n_e = wg.shape[1]
    col, dst, te, tvalid, tfirst = _moe_plan(route, t, MOE_TILE, n_e)
    xs = _dispatch(dst, col, h, n_e)
    ys = _experts(te, tvalid, tfirst, xs, wg, wu, wd, layer)
    return _combine(dst, col.T, route.T, ys, x2, g2, seq, n_e)


def kernel(x, c, ada_w, ada_b, norm_mix_g, norm_ffn_g, t5_bias, a_w_qkv, a_q_gain, a_k_gain, a_lambda, a_subln_g, a_w_o, b_w_qkv, b_q_gain, b_k_gain, b_rel_bias, b_w_o, router_w, router_bias, moe_w_gate, moe_w_up, moe_w_down):
    batch, seq, d = x.shape
    depth = ada_w.shape[0]
    assert seq % ATTN_TILE == 0 and seq % ROW_TILE == 0 and d == A_HEADS * 2 * A_HEAD_DIM
    assert d == B_HEADS * B_HEAD_DIM and A_HEAD_DIM == B_HEAD_DIM

    c_pad = jnp.zeros((8, d), F32).at[:batch].set(c.astype(F32))
    mod = _modulation(c_pad, ada_w.astype(F32), ada_b.astype(F32))[:, :batch]
    mod = mod.reshape(depth, batch, 6, 1, d)

    rwt = router_w.astype(F32).T
    rwt_hi = lax.reduce_precision(rwt, exponent_bits=8, mantissa_bits=7)
    rwt = jnp.concatenate([rwt_hi, rwt - rwt_hi], axis=0).astype(BF16)
    rb = router_bias.astype(F32).reshape(-1, 1)

    x2 = x.astype(F32).reshape(batch * seq, d)
    for i in range(depth):
        sh1, sc1, g1, sh2, sc2, g2 = [mod[i, :, k] for k in range(6)]
        j = i // 2
        if i % 2 == 0:
            w_qkv, qg, kg, w_o = a_w_qkv[j], a_q_gain[j], a_k_gain[j], a_w_o[j]
        else:
            w_qkv, qg, kg, w_o = b_w_qkv[j], b_q_gain[j], b_k_gain[j], b_w_o[j]
        n_rep = d // qg.shape[0]
        gq = (jnp.tile(qg.astype(F32), n_rep) * (A_HEAD_DIM ** -0.5 * LOG2E)).reshape(d, 1)
        gk = jnp.tile(kg.astype(F32), n_rep).reshape(1, d)
        wb = w_qkv.astype(BF16)
        qt, k, vt = _qkv_proj(x2, norm_mix_g[i].reshape(1, d), sc1, sh1,
                              wb[:, :d].T, wb[:, d:2 * d], wb[:, 2 * d:].T, gq, gk, seq, A_HEAD_DIM)
        if i % 2 == 0:
            lambda_init = 0.8 - 0.6 * math.exp(-0.3 * i)
            o = _attention_a(qt, k, vt, _t5_vectors(t5_bias, ATTN_TILE), a_lambda[j].astype(F32),
                             a_subln_g[j].reshape(1, -1), batch, seq, lambda_init)
        else:
            o = _attention_b(qt, k, vt, _band_vectors(b_rel_bias[j], BAND_TILE), batch, seq)
        x2, h, route = _out_proj(o, w_o.astype(BF16), x2, g1, norm_ffn_g[i].reshape(1, d),
                                 sc2, sh2, rwt, rb, seq)
        x2 = _moe(h, route, moe_w_gate, moe_w_up, moe_w_down, i, x2, g2, seq)
    return x2.reshape(batch, seq, d)
```

```python
import functools
import math

import numpy as np
import jax
import jax.numpy as jnp
from jax import lax
from jax.experimental import pallas as pl
from jax.experimental.pallas import tpu as pltpu

F32 = jnp.float32
BF16 = jnp.bfloat16

CHUNK = 64
A_HEADS = 8
A_HEAD_DIM = 64
T5_BUCKETS = 32
T5_MAX_DIST = 1024
B_HEADS = 16
B_HEAD_DIM = 64
LEFT_CHUNKS = 8
MAX_REL = 256
N_EXPERTS = 16
N_GROUPS = 4
E_PER_GROUP = N_EXPERTS // N_GROUPS
NORM_EPS = 1e-6
NEG_INF = -1e30
LOG2E = math.log2(math.e)

V7X_LANES = 128
V7X_MXU_DIM = 256

ATTN_TILE = 512
MOD_COL_TILE = 1536
FAR_UNROLL = 4
NEAR_AHEAD = 2
BAND_TILE = 256
BAND_QBLOCKS = 4
ROW_TILE = 512
MOE_TILE = 512
OPROJ_SPLIT = 4
MOE_CHUNK = 16
MOE_FFN_TILE = 512
MOE_FFN_SPLIT = 2
VMEM_LIMIT = 56 * 1024 * 1024

_NT = (((1,), (1,)), ((), ()))


def _cparams(sem):
    return pltpu.CompilerParams(dimension_semantics=sem, vmem_limit_bytes=VMEM_LIMIT)


def _mod_kernel(c_ref, w_ref, b_ref, o_ref):
    c = c_ref[...]
    s = c * (1.0 / (1.0 + jnp.exp(-c)))
    o_ref[...] = jnp.dot(s, w_ref[...], preferred_element_type=F32,
                         precision=lax.Precision.HIGHEST) + b_ref[...]


def _modulation(c_pad, ada_w, ada_b):
    depth, d, n = ada_w.shape
    rows = c_pad.shape[0]
    tn = MOD_COL_TILE
    return pl.pallas_call(
        _mod_kernel,
        out_shape=jax.ShapeDtypeStruct((depth, rows, n), F32),
        grid=(depth, n // tn),
        in_specs=[
            pl.BlockSpec((rows, d), lambda i, j: (0, 0)),
            pl.BlockSpec((None, d, tn), lambda i, j: (i, 0, j)),
            pl.BlockSpec((None, 1, tn), lambda i, j: (i, 0, j)),
        ],
        out_specs=pl.BlockSpec((None, rows, tn), lambda i, j: (i, 0, j)),
        compiler_params=_cparams(("parallel", "parallel")),
        name="adaln_mod",
    )(c_pad, ada_w, ada_b.reshape(depth, 1, n))


def _norm_mod(x, g, sc, sh):
    ms = jnp.mean(x * x, axis=-1, keepdims=True)
    return x * lax.rsqrt(ms + NORM_EPS) * g * (1.0 + sc) + sh


def _qkv_kernel(x_ref, g_ref, sc_ref, sh_ref, w_ref, gq_ref, gk_ref,
                qt_ref, k_ref, vt_ref, wqt_ref, wvt_ref, *, head_dim):
    d, tm = qt_ref.shape

    @pl.when(pl.program_id(0) == 0)
    def _():
        wqt_ref[...] = w_ref[:, :d].T
        wvt_ref[...] = w_ref[:, 2 * d:].T

    h = _norm_mod(x_ref[...], g_ref[...], sc_ref[...], sh_ref[...]).astype(BF16)
    y = lax.dot_general(wqt_ref[...], h, _NT, preferred_element_type=F32)
    y3 = y.reshape(d // head_dim, head_dim, tm)
    ss = jnp.mean(y3 * y3, axis=1, keepdims=True)
    qt_ref[...] = ((y3 * lax.rsqrt(ss + NORM_EPS)).reshape(d, tm) * gq_ref[...]).astype(BF16)
    cw = V7X_MXU_DIM
    r = lax.broadcasted_iota(jnp.int32, (cw, cw), 0) // head_dim
    c = lax.broadcasted_iota(jnp.int32, (cw, cw), 1) // head_dim
    gmat = jnp.where(r == c, 1.0 / head_dim, 0.0).astype(BF16)
    y = jnp.dot(h, w_ref[:, d:2 * d], preferred_element_type=F32)
    ysq = (y * y).astype(BF16)
    for ci in range(d // cw):
        cols = slice(ci * cw, (ci + 1) * cw)
        ss = jnp.dot(ysq[:, cols], gmat, preferred_element_type=F32)
        k_ref[:, cols] = (y[:, cols] * lax.rsqrt(ss + NORM_EPS) * gk_ref[:, cols]).astype(BF16)
    vt_ref[...] = lax.dot_general(wvt_ref[...], h, _NT, preferred_element_type=F32).astype(BF16)


def _qkv_proj(x2, g, sc, sh, w, gq, gk, seq, head_dim):
    t, d = x2.shape
    tm = ROW_TILE
    per_b = seq // tm
    vec = lambda i: (i // per_b, 0, 0)
    full = lambda i: (0, 0)
    return pl.pallas_call(
        functools.partial(_qkv_kernel, head_dim=head_dim),
        out_shape=(jax.ShapeDtypeStruct((d, t), BF16),
                   jax.ShapeDtypeStruct((t, d), BF16),
                   jax.ShapeDtypeStruct((d, t), BF16)),
        grid=(t // tm,),
        in_specs=[
            pl.BlockSpec((tm, d), lambda i: (i, 0)),
            pl.BlockSpec((1, d), full),
            pl.BlockSpec((None, 1, d), vec),
            pl.BlockSpec((None, 1, d), vec),
            pl.BlockSpec((d, 3 * d), full),
            pl.BlockSpec((d, 1), full),
            pl.BlockSpec((1, d), full),
        ],
        out_specs=(pl.BlockSpec((d, tm), lambda i: (0, i)),
                   pl.BlockSpec((tm, d), lambda i: (i, 0)),
                   pl.BlockSpec((d, tm), lambda i: (0, i))),
        scratch_shapes=[pltpu.VMEM((d, d), BF16), pltpu.VMEM((d, d), BF16)],
        compiler_params=_cparams(("arbitrary",)),
        name="qkv_proj",
    )(x2, g, sc, sh, w, gq, gk)


def _softmax_pv(s, vt1, m_ref, acc_ref):
    m_prev = m_ref[...]
    m_new = jnp.maximum(m_prev, jnp.max(s, axis=0, keepdims=True))
    alpha = jnp.exp2(m_prev - m_new)
    p = jnp.exp2(s - m_new).astype(BF16)
    acc_ref[...] = alpha * acc_ref[...] + jnp.dot(vt1, p, preferred_element_type=F32)
    m_ref[...] = m_new


def _split_maps(qt, head_dim):
    row = lax.broadcasted_iota(jnp.int32, qt.shape, 0)
    zero = jnp.zeros_like(qt)
    return jnp.where(row < head_dim, qt, zero), jnp.where(row >= head_dim, qt, zero)


ONES_ROWS = 16


def _init_state(refs):
    for m_ref, acc_ref in refs:
        m_ref[...] = jnp.full(m_ref.shape, NEG_INF, F32)
        acc_ref[...] = jnp.zeros(acc_ref.shape, F32)


def _normalized(acc_ref):
    acc = acc_ref[...]
    return acc[:V7X_LANES] / acc[V7X_LANES:V7X_LANES + 1]


def _toeplitz(x_row, n):
    x = jnp.broadcast_to(x_row, (n, x_row.shape[1]))
    return pltpu.roll(x, 0, 1, stride=1, stride_axis=0)[:, :n]


def _chunk_delta(n):
    kc = lax.broadcasted_iota(jnp.int32, (n, n), 0) // CHUNK
    qc = lax.broadcasted_iota(jnp.int32, (n, n), 1) // CHUNK
    return qc - kc


def _attn_a_kernel(qt_ref, k_ref, vt_ref, x_ref, lam_ref, sub_ref, o_ref,
                   tab, fbuf, nbuf, m_st, acc_st, *, n_off, lambda_init):
    tq = ATTN_TILE
    step = pl.program_id(2)

    @pl.when(step == 0)
    def _():
        for o in range(n_off):
            t = _toeplitz(x_ref[o], tq) * LOG2E
            if o == 0:
                t = jnp.where(_chunk_delta(tq) >= 0, t, NEG_INF)
            tab[o] = t

    for c in range(2):
        _init_state(((m_st.at[c, 0], acc_st.at[c, 0]), (m_st.at[c, 1], acc_st.at[c, 1])))
    q_maps = [_split_maps(qt_ref[:, c * tq:(c + 1) * tq], A_HEAD_DIM) for c in range(2)]

    def k_tile(j):
        return k_ref[pl.ds(pl.multiple_of(j * tq, tq), tq), :]

    def vt1_tile(j):
        vt = vt_ref[:, pl.ds(pl.multiple_of(j * tq, tq), tq)]
        return jnp.concatenate([vt, jnp.ones((ONES_ROWS, tq), vt.dtype)], axis=0)

    def scores(buf, k, c):
        for m in range(2):
            buf[c, m] = jnp.dot(k, q_maps[c][m], preferred_element_type=F32)

    def consume(buf, vt1, c, o):
        for m in range(2):
            s = buf[c, m]
            _softmax_pv(s if o >= n_off else s + tab[o], vt1, m_st.at[c, m], acc_st.at[c, m])

    def near_block(first_tile, n_tiles, first_scored):
        units = [[c for c in range(2) if n_tiles - 2 + c - s >= 0] for s in range(n_tiles)]
        buf_of = lambda s: fbuf.at[0] if s == 0 else nbuf.at[s - 1]

        def tile_scores(s):
            k = k_tile(first_tile + s)
            for c in units[s]:
                scores(buf_of(s), k, c)

        if not first_scored:
            tile_scores(0)
        for s in range(1, min(NEAR_AHEAD, n_tiles)):
            tile_scores(s)
        for s in range(n_tiles):
            if s + NEAR_AHEAD < n_tiles:
                tile_scores(s + NEAR_AHEAD)
            vt1 = vt1_tile(first_tile + s)
            for c in units[s]:
                consume(buf_of(s), vt1, c, n_tiles - 2 + c - s)

    first_general = (n_off - 1) // 2
    for i0 in range(first_general):
        @pl.when(step == i0)
        def _():
            near_block(0, 2 * i0 + 2, False)

    def far_step(j, parity):
        k = k_tile(j + 1)
        for c in range(2):
            scores(fbuf.at[1 - parity], k, c)
        vt1 = vt1_tile(j)
        for c in range(2):
            consume(fbuf.at[parity], vt1, c, n_off)

    @pl.when(step >= first_general)
    def _():
        k = k_tile(0)
        for c in range(2):
            scores(fbuf.at[0], k, c)

    n_far = jnp.maximum(2 * step - (n_off - 1), 0)
    rem = n_far % FAR_UNROLL

    @pl.when(rem >= 2)
    def _():
        far_step(0, 0)
        far_step(1, 1)

    def far_trip(t, carry):
        for u in range(FAR_UNROLL):
            far_step(FAR_UNROLL * t + rem + u, u % 2)
        return carry
    lax.fori_loop(0, n_far // FAR_UNROLL, far_trip, 0)

    @pl.when(step >= first_general)
    def _():
        near_block(n_far, n_off + 1, True)

    lam = lam_ref[...]
    lam_full = (jnp.exp(jnp.sum(lam[0:1] * lam[1:2], axis=-1, keepdims=True))
                - jnp.exp(jnp.sum(lam[2:3] * lam[3:4], axis=-1, keepdims=True)) + lambda_init)
    for c in range(2):
        a = _normalized(acc_st.at[c, 0]) - lam_full * _normalized(acc_st.at[c, 1])
        ms = jnp.mean(a * a, axis=0, keepdims=True)
        an = a * lax.rsqrt(ms + NORM_EPS)
        o_ref[c * tq:(c + 1) * tq, :] = (an.T * (sub_ref[...] * (1.0 - lambda_init))).astype(BF16)


def _attn_b_kernel(qt_ref, k_ref, vt_ref, x_ref, o_ref, tab, sbuf):
    tq = BAND_TILE
    n_blk = qt_ref.shape[1] // tq
    n_off = tab.shape[0] // 2
    step = pl.program_id(2)

    @pl.when(step == 0)
    def _():
        for m in range(2):
            for o in range(n_off):
                d = _chunk_delta(tq) + o * (tq // CHUNK)
                t = _toeplitz(x_ref[n_off * m + o], tq) * LOG2E
                tab[n_off * m + o] = jnp.where((d >= 0) & (d <= LEFT_CHUNKS), t, NEG_INF)

    q_maps = [_split_maps(qt_ref[:, c * tq:(c + 1) * tq], B_HEAD_DIM) for c in range(n_blk)]

    def run(first_step):
        pairs = [(c, o) for c in range(n_blk) for o in range(n_off - 1, -1, -1)
                 if not first_step or c - o >= 0]
        k_tiles, vt_tiles = {}, {}
        for c, o in pairs:
            if c - o not in k_tiles:
                ks = pl.multiple_of((n_blk * step + c - o) * tq, tq)
                k_tiles[c - o] = k_ref[pl.ds(ks, tq), :]
                vt = vt_ref[:, pl.ds(ks, tq)]
                vt_tiles[c - o] = jnp.concatenate([vt, jnp.ones((ONES_ROWS, tq), vt.dtype)], axis=0)
        def tile_scores(c):
            for o in [o for cc, o in pairs if cc == c]:
                for m in range(2):
                    sbuf[n_off * c + o, m] = jnp.dot(k_tiles[c - o], q_maps[c][m], preferred_element_type=F32)

        for c in range(min(NEAR_AHEAD, n_blk)):
            tile_scores(c)
        for c in range(n_blk):
            if c + NEAR_AHEAD < n_blk:
                tile_scores(c + NEAR_AHEAD)
            outs = []
            for m in range(2):
                offs = [o for cc, o in pairs if cc == c]
                ss = [sbuf[n_off * c + o, m] + tab[n_off * m + o] for o in offs]
                mx = functools.reduce(jnp.maximum, [jnp.max(s, axis=0, keepdims=True) for s in ss])
                acc = sum(jnp.dot(vt_tiles[c - o], jnp.exp2(s - mx).astype(BF16), preferred_element_type=F32)
                          for o, s in zip(offs, ss))
                outs.append(acc[:V7X_LANES] / acc[V7X_LANES:V7X_LANES + 1])
            row = lax.broadcasted_iota(jnp.int32, outs[0].shape, 0)
            o_ref[c * tq:(c + 1) * tq, :] = jnp.where(row < B_HEAD_DIM, outs[0], outs[1]).T.astype(BF16)

    @pl.when(step == 0)
    def _():
        run(True)

    @pl.when(step >= 1)
    def _():
        run(False)


def _attn_specs(batch, seq, nq, tq):
    return dict(
        q=pl.BlockSpec((V7X_LANES, tq), lambda b, h, i: (h, b * nq + i)),
        k=pl.BlockSpec((seq, V7X_LANES), lambda b, h, i: (b, h)),
        v=pl.BlockSpec((V7X_LANES, seq), lambda b, h, i: (h, b)),
        o=pl.BlockSpec((tq, V7X_LANES), lambda b, h, i: (b * nq + i, h)),
    )


def _attention_a(qt, k, vt, xvec, lam, sub_g, batch, seq, lambda_init):
    d, t = qt.shape
    tq = ATTN_TILE
    n_off = xvec.shape[1]
    assert n_off % 2 == 1 and seq % (2 * tq) == 0
    nq = seq // (2 * tq)
    sp = _attn_specs(batch, seq, nq, 2 * tq)
    rows = V7X_LANES + ONES_ROWS
    return pl.pallas_call(
        functools.partial(_attn_a_kernel, n_off=n_off, lambda_init=lambda_init),
        out_shape=jax.ShapeDtypeStruct((t, d), BF16),
        grid=(batch, d // V7X_LANES, nq),
        in_specs=[sp["q"], sp["k"], sp["v"],
                  pl.BlockSpec((None,) + xvec.shape[1:], lambda b, h, i: (h, 0, 0, 0)),
                  pl.BlockSpec(lam.shape, lambda b, h, i: (0, 0)),
                  pl.BlockSpec(sub_g.shape, lambda b, h, i: (0, 0))],
        out_specs=sp["o"],
        scratch_shapes=[pltpu.VMEM((n_off, tq, tq), F32), pltpu.VMEM((2, 2, 2, tq, tq), F32),
                        pltpu.VMEM((n_off, 2, 2, tq, tq), F32),
                        pltpu.VMEM((2, 2, 1, tq), F32), pltpu.VMEM((2, 2, rows, tq), F32)],
        compiler_params=_cparams(("parallel", "parallel", "arbitrary")),
        name="diff_attention",
    )(qt, k, vt, xvec, lam, sub_g)


def _attention_b(qt, k, vt, xvec, batch, seq):
    d, t = qt.shape
    tq = BAND_TILE
    g = BAND_QBLOCKS
    n_off = xvec.shape[1] // 2
    assert g >= n_off - 1 and seq % (g * tq) == 0
    nq = seq // (g * tq)
    sp = _attn_specs(batch, seq, nq, g * tq)
    return pl.pallas_call(
        _attn_b_kernel,
        out_shape=jax.ShapeDtypeStruct((t, d), BF16),
        grid=(batch, d // V7X_LANES, nq),
        in_specs=[sp["q"], sp["k"], sp["v"],
                  pl.BlockSpec((None,) + xvec.shape[1:], lambda b, h, i: (h, 0, 0, 0))],
        out_specs=sp["o"],
        scratch_shapes=[pltpu.VMEM((2 * n_off, tq, tq), F32), pltpu.VMEM((n_off * g, 2, tq, tq), F32)],
        compiler_params=_cparams(("parallel", "parallel", "arbitrary")),
        name="chunk_attention",
    )(qt, k, vt, xvec)


def _t5_bucket(rel):
    nb = T5_BUCKETS // 2
    ret = jnp.where(rel > 0, nb, 0)
    n = jnp.abs(rel)
    max_exact = nb // 2
    nf = jnp.maximum(n, 1).astype(F32)
    large = max_exact + (jnp.log(nf / max_exact) / math.log(T5_MAX_DIST / max_exact)
                         * (nb - max_exact)).astype(jnp.int32)
    large = jnp.minimum(large, nb - 1)
    return ret + jnp.where(n < max_exact, n, large)


def _t5_const_distance():
    nb = T5_BUCKETS // 2
    max_exact = nb // 2
    n = np.arange(max_exact, 4 * T5_MAX_DIST, dtype=np.float64)
    large = max_exact + np.floor(np.log(n / max_exact) / math.log(T5_MAX_DIST / max_exact) * (nb - max_exact))
    below = np.nonzero(large < nb - 1)[0]
    return int(n[below[-1]]) + 2


def _tile_rel(tile, n_off):
    i = jnp.arange(2 * tile, dtype=jnp.int32)
    rel = jnp.where(i < tile, -i, 2 * tile - i)
    return rel[None, :] - tile * jnp.arange(n_off, dtype=jnp.int32)[:, None]


def _t5_vectors(t5_bias, tile):
    n_off = 1
    while (n_off - 1) * tile + 1 < _t5_const_distance():
        n_off += 1
    tb = t5_bias.astype(F32)
    vals = tb[_t5_bucket(_tile_rel(tile, n_off))] - tb[T5_BUCKETS // 2 - 1]
    return vals.transpose(2, 0, 1)[:, :, None, :]


def _band_vectors(rel_bias, tile):
    n_off = LEFT_CHUNKS * CHUNK // tile + 1
    idx = jnp.clip(_tile_rel(tile, n_off), -MAX_REL, MAX_REL) + MAX_REL
    vals = rel_bias.astype(F32)[:, idx]
    return vals.reshape(rel_bias.shape[0] // 2, 2 * n_off, 1, 2 * tile)


def _top2_sum4(r0, r1, r2, r3):
    a, b = jnp.maximum(r0, r1), jnp.minimum(r0, r1)
    c, d = jnp.maximum(r2, r3), jnp.minimum(r2, r3)
    return jnp.maximum(a, c) + jnp.maximum(jnp.minimum(a, c), jnp.maximum(b, d))


def _route(logits, rbias):
    n_e, n = logits.shape
    scores = 1.0 / (1.0 + jnp.exp(-logits))
    sel = scores + rbias
    row = lax.broadcasted_iota(jnp.int32, sel.shape, 0)
    best = None
    for g in range(N_GROUPS):
        rows = [sel[g * E_PER_GROUP + i: g * E_PER_GROUP + i + 1, :] for i in range(E_PER_GROUP)]
        gs = _top2_sum4(*rows)
        if best is None:
            best, gidx = gs, jnp.zeros(gs.shape, jnp.int32)
        else:
            gidx = jnp.where(gs > best, g, gidx)
            best = jnp.maximum(best, gs)
    masked = jnp.where(row // E_PER_GROUP == gidx, sel, NEG_INF)
    m1 = jnp.max(masked, axis=0, keepdims=True)
    i1 = jnp.min(jnp.where(masked == m1, row, n_e), axis=0, keepdims=True)
    masked2 = jnp.where(row == i1, -3.0e38, masked)
    m2 = jnp.max(masked2, axis=0, keepdims=True)
    i2 = jnp.min(jnp.where(masked2 == m2, row, n_e), axis=0, keepdims=True)
    w1 = jnp.sum(jnp.where(row == i1, scores, 0.0), axis=0, keepdims=True)
    w2 = jnp.sum(jnp.where(row == i2, scores, 0.0), axis=0, keepdims=True)
    den = w1 + w2
    member = jnp.where((row == i1) | (row == i2), 1.0, 0.0).astype(BF16)
    t_from = lax.broadcasted_iota(jnp.int32, (n, n), 0)
    t_to = lax.broadcasted_iota(jnp.int32, (n, n), 1)
    before = jnp.where((t_from < t_to) & (t_from // MOE_TILE == t_to // MOE_TILE), 1.0, 0.0).astype(BF16)
    pos = jnp.dot(member, before, preferred_element_type=F32)
    p1 = jnp.sum(jnp.where(row == i1, pos, 0.0), axis=0, keepdims=True)
    p2 = jnp.sum(jnp.where(row == i2, pos, 0.0), axis=0, keepdims=True)
    out_row = lax.broadcasted_iota(jnp.int32, (8, n), 0)
    out = jnp.zeros((8, n), F32)
    for r, val in enumerate((i1.astype(F32), i2.astype(F32), p1, p2, w1 / den, w2 / den)):
        out = jnp.where(out_row == r, val, out)
    return out


def _oproj_kernel(o_ref, wo_ref, x_ref, g1_ref, gn_ref, sc_ref, sh_ref, rwt_ref, rb_ref,
                  xo_ref, h_ref, route_ref):
    n_e = rb_ref.shape[0]
    rw = rwt_ref[...]
    rows = o_ref.shape[0] // OPROJ_SPLIT
    groups = [slice(r * rows, (r + 1) * rows) for r in range(OPROJ_SPLIT)]
    ys = [jnp.dot(o_ref[g, :], wo_ref[...], preferred_element_type=F32) for g in groups]
    logits = []
    for g, y in zip(groups, ys):
        xn = x_ref[g, :] + g1_ref[...] * y
        xo_ref[g, :] = xn
        h = _norm_mod(xn, gn_ref[...], sc_ref[...], sh_ref[...])
        h_hi = h.astype(BF16)
        h_ref[g, :] = h_hi
        h_lo = (h - h_hi.astype(F32)).astype(BF16)
        main = lax.dot_general(rw, h_hi, _NT, preferred_element_type=F32)
        corr = lax.dot_general(rw[:n_e], h_lo, _NT, preferred_element_type=F32)
        logits.append(main[:n_e] + main[n_e:] + corr)
    route_ref[...] = _route(jnp.concatenate(logits, axis=1), rb_ref[...])


def _out_proj(o, wo_bf16, x2, g1, gn, sc, sh, rwt, rb, seq):
    t, d = x2.shape
    tm = ROW_TILE
    per_b = seq // tm
    n_e = rb.shape[0]
    vec = lambda i: (i // per_b, 0, 0)
    return pl.pallas_call(
        _oproj_kernel,
        out_shape=(jax.ShapeDtypeStruct((t, d), F32),
                   jax.ShapeDtypeStruct((t, d), BF16),
                   jax.ShapeDtypeStruct((8, t), F32)),
        grid=(t // tm,),
        in_specs=[
            pl.BlockSpec((tm, d), lambda i: (i, 0)),
            pl.BlockSpec((d, d), lambda i: (0, 0)),
            pl.BlockSpec((tm, d), lambda i: (i, 0)),
            pl.BlockSpec((None, 1, d), vec),
            pl.BlockSpec((1, d), lambda i: (0, 0)),
            pl.BlockSpec((None, 1, d), vec),
            pl.BlockSpec((None, 1, d), vec),
            pl.BlockSpec((2 * n_e, d), lambda i: (0, 0)),
            pl.BlockSpec((n_e, 1), lambda i: (0, 0)),
        ],
        out_specs=(pl.BlockSpec((tm, d), lambda i: (i, 0)),
                   pl.BlockSpec((tm, d), lambda i: (i, 0)),
                   pl.BlockSpec((8, tm), lambda i: (0, i))),
        compiler_params=_cparams(("parallel",)),
        name="out_proj_route",
    )(o, wo_bf16, x2, g1, gn, sc, sh, rwt, rb)


def _moe_plan(route, t, tm, n_e):
    n_tiles = t // tm
    ch, ft = MOE_CHUNK, MOE_FFN_TILE
    loc_rows, main_rows, n_sorted = _moe_rows(t, tm, n_e)
    ids = jnp.arange(n_e, dtype=jnp.int32)
    e = route[0:2].astype(jnp.int32)
    pos = route[2:4].astype(jnp.int32)
    oh = (e[:, :, None] == ids).astype(jnp.int32)
    cnt = oh.sum(0).reshape(n_tiles, tm, n_e).sum(1)
    seg = (cnt + ch - 1) // ch * ch
    loc = jnp.cumsum(seg, axis=1) - seg
    tot = seg.sum(0)
    totp = (tot + ft - 1) // ft * ft
    base = jnp.cumsum(totp) - totp
    gdest = base[None, :] + jnp.cumsum(seg, axis=0) - seg
    col = (oh * jnp.repeat(loc, tm, axis=0)[None]).sum(-1) + pos

    row0 = jnp.arange(loc_rows // ch, dtype=jnp.int32) * ch
    ej = (row0[None, :, None] >= (loc + seg)[:, None, :]).sum(-1)
    ohj = (jnp.minimum(ej, n_e - 1)[..., None] == ids).astype(jnp.int32)
    dst = (ohj * (gdest - loc)[:, None, :]).sum(-1) + row0[None, :]
    dump = main_rows + (jnp.arange(n_tiles, dtype=jnp.int32) % 2)[:, None] * loc_rows + row0[None, :]
    dst = jnp.where(ej < n_e, dst, dump) // ch

    r0 = jnp.arange(n_sorted // ft, dtype=jnp.int32) * ft
    ends = base + totp
    te = jnp.minimum((r0[:, None] >= ends[None, :]).sum(-1), n_e - 1)
    tvalid = (r0 < ends[-1]).astype(jnp.int32)
    tfirst = tvalid * (r0 == (((te[:, None] == ids) * base[None, :]).sum(-1))).astype(jnp.int32)
    return col, dst.astype(jnp.int32), te.astype(jnp.int32), tvalid, tfirst


def _moe_rows(t, tm, n_e):
    ch, ft = MOE_CHUNK, MOE_FFN_TILE
    loc_rows = 2 * tm + n_e * ch
    main = 2 * t + (t // tm) * n_e * (ch - 1) + n_e * (ft - 1)
    main = (main + ft - 1) // ft * ft
    return loc_rows, main, (main + 2 * loc_rows + ft - 1) // ft * ft


def _chunk_copies(dst_ref, tile, local_ref, sorted_ref, sem, to_sorted):
    ch = MOE_CHUNK
    copies = []
    for j in range(local_ref.shape[0] // ch):
        far = sorted_ref.at[pl.ds(pl.multiple_of(dst_ref[tile, j] * ch, ch), ch)]
        near = local_ref.at[pl.ds(j * ch, ch)]
        copies.append(pltpu.make_async_copy(near, far, sem) if to_sorted
                      else pltpu.make_async_copy(far, near, sem))
    return copies


def _dispatch_kernel(dst_ref, col_ref, h_ref, init_ref, xs_ref, comp, sem):
    del init_ref
    tile = pl.program_id(0)
    last = pl.num_programs(0) - 1
    slot = tile % 2

    def wait_slot(s):
        pltpu.make_async_copy(comp.at[s], xs_ref.at[pl.ds(0, comp.shape[1])], sem.at[s]).wait()

    @pl.when(tile >= 2)
    def _():
        wait_slot(slot)

    col = col_ref[...]
    srow = lax.broadcasted_iota(jnp.int32, (comp.shape[1], col.shape[1]), 0)
    sel = jnp.where((srow == col[0:1]) | (srow == col[1:2]), 1.0, 0.0).astype(BF16)
    comp[slot] = jnp.dot(sel, h_ref[...], preferred_element_type=F32).astype(BF16)
    for cp in _chunk_copies(dst_ref, tile, comp.at[slot], xs_ref, sem.at[slot], True):
        cp.start()

    @pl.when(tile == last)
    def _():
        wait_slot(slot)

    @pl.when((tile == last) & (tile >= 1))
    def _():
        wait_slot(1 - slot)


def _dispatch(dst, col, h, n_e):
    t, d = h.shape
    tm = MOE_TILE
    loc_rows, _, n_sorted = _moe_rows(t, tm, n_e)
    return pl.pallas_call(
        _dispatch_kernel,
        out_shape=jax.ShapeDtypeStruct((n_sorted, d), BF16),
        grid_spec=pltpu.PrefetchScalarGridSpec(
            num_scalar_prefetch=1,
            grid=(t // tm,),
            in_specs=[pl.BlockSpec((2, tm), lambda i, dst: (0, i)),
                      pl.BlockSpec((tm, d), lambda i, dst: (i, 0)),
                      pl.BlockSpec(memory_space=pl.ANY)],
            out_specs=pl.BlockSpec(memory_space=pl.ANY),
            scratch_shapes=[pltpu.VMEM((2, loc_rows, d), BF16), pltpu.SemaphoreType.DMA((2,))],
        ),
        input_output_aliases={3: 0},
        compiler_params=_cparams(("arbitrary",)),
        name="moe_dispatch",
    )(dst, col, h, jnp.zeros((n_sorted, d), BF16))


def _expert_kernel(te_ref, tv_ref, tf_ref, xs_ref, wg_ref, wu_ref, wd_ref, ys_ref, wg_b, wu_b, wd_b):
    del te_ref
    s = pl.program_id(0)

    @pl.when(tf_ref[s] == 1)
    def _():
        wg_b[...] = wg_ref[...].astype(BF16)
        wu_b[...] = wu_ref[...].astype(BF16)
        wd_b[...] = wd_ref[...].astype(BF16)

    @pl.when(tv_ref[s] == 1)
    def _():
        rows = ys_ref.shape[0] // MOE_FFN_SPLIT
        hs = []
        for r in range(MOE_FFN_SPLIT):
            xs = xs_ref[r * rows:(r + 1) * rows, :]
            hg = jnp.dot(xs, wg_b[...], preferred_element_type=F32)
            hu = jnp.dot(xs, wu_b[...], preferred_element_type=F32)
            hs.append((hg, hu))
        for r, (hg, hu) in enumerate(hs):
            he = hg * (1.0 / (1.0 + jnp.exp(-hg))) * hu
            ys_ref[r * rows:(r + 1) * rows, :] = jnp.dot(
                he.astype(BF16), wd_b[...], preferred_element_type=F32).astype(BF16)

    @pl.when(tv_ref[s] == 0)
    def _():
        ys_ref[...] = jnp.zeros(ys_ref.shape, ys_ref.dtype)


def _experts(te, tvalid, tfirst, xs, wg, wu, wd, layer):
    n_sorted, d = xs.shape
    f = wg.shape[3]
    ft = MOE_FFN_TILE
    wspec = lambda r, c: pl.BlockSpec((None, None, r, c), lambda s, te, tv, tf: (layer, te[s], 0, 0))
    return pl.pallas_call(
        _expert_kernel,
        out_shape=jax.ShapeDtypeStruct((n_sorted, d), BF16),
        grid_spec=pltpu.PrefetchScalarGridSpec(
            num_scalar_prefetch=3,
            grid=(n_sorted // ft,),
            in_specs=[pl.BlockSpec((ft, d), lambda s, te, tv, tf: (s, 0)),
                      wspec(d, f), wspec(d, f), wspec(f, d)],
            out_specs=pl.BlockSpec((ft, d), lambda s, te, tv, tf: (s, 0)),
            scratch_shapes=[pltpu.VMEM((d, f), BF16), pltpu.VMEM((d, f), BF16), pltpu.VMEM((f, d), BF16)],
        ),
        compiler_params=_cparams(("arbitrary",)),
        name="moe_experts",
    )(te, tvalid, tfirst, xs, wg, wu, wd)


def _combine_kernel(dst_ref, col_ref, w_ref, ys_ref, x_ref, g2_ref, o_ref, comp, sem):
    tile = pl.program_id(0)
    slot = tile % 2

    def fetch(t, s):
        for cp in _chunk_copies(dst_ref, t, comp.at[s], ys_ref, sem.at[s], False):
            cp.start()

    def wait_slot(s):
        pltpu.make_async_copy(ys_ref.at[pl.ds(0, comp.shape[1])], comp.at[s], sem.at[s]).wait()

    last = pl.num_programs(0) - 1

    @pl.when(tile == 0)
    def _():
        fetch(tile, slot)

    fetch(jnp.minimum(tile + 1, last), 1 - slot)

    col = col_ref[...]
    w = w_ref[...]
    scol = lax.broadcasted_iota(jnp.int32, (col.shape[0], comp.shape[1]), 1)
    selw = (jnp.where(scol == col[:, 0:1], w[:, 4:5], 0.0)
            + jnp.where(scol == col[:, 1:2], w[:, 5:6], 0.0)).astype(BF16)
    wait_slot(slot)
    acc = jnp.dot(selw, comp[slot], preferred_element_type=F32)
    o_ref[...] = x_ref[...] + g2_ref[...] * acc

    @pl.when(tile == last)
    def _():
        wait_slot(1 - slot)


def _combine(dst, col_t, route_t, ys, x2, g2, seq, n_e):
    t, d = x2.shape
    tm = MOE_TILE
    per_b = seq // tm
    loc_rows = _moe_rows(t, tm, n_e)[0]
    return pl.pallas_call(
        _combine_kernel,
        out_shape=jax.ShapeDtypeStruct((t, d), F32),
        grid_spec=pltpu.PrefetchScalarGridSpec(
            num_scalar_prefetch=1,
            grid=(t // tm,),
            in_specs=[pl.BlockSpec((tm, 2), lambda i, dst: (i, 0)),
                      pl.BlockSpec((tm, 8), lambda i, dst: (i, 0)),
                      pl.BlockSpec(memory_space=pl.ANY),
                      pl.BlockSpec((tm, d), lambda i, dst: (i, 0)),
                      pl.BlockSpec((None, 1, d), lambda i, dst: (i // per_b, 0, 0))],
            out_specs=pl.BlockSpec((tm, d), lambda i, dst: (i, 0)),
            scratch_shapes=[pltpu.VMEM((2, loc_rows, d), BF16), pltpu.SemaphoreType.DMA((2,))],
        ),
        compiler_params=_cparams(("arbitrary",)),
        name="moe_combine",
    )(dst, col_t, route_t, ys, x2, g2)


def _moe(h, route, wg, wu, wd, layer, x2, g2, seq):
    t = x2.shape[0]
    n_e = wg.shape[1]
    col, dst, te, tvalid, tfirst = _moe_plan(route, t, MOE_TILE, n_e)
    xs = _dispatch(dst, col, h, n_e)
    ys = _experts(te, tvalid, tfirst, xs, wg, wu, wd, layer)
    return _combine(dst, col.T, route.T, ys, x2, g2, seq, n_e)


def kernel(x, c, ada_w, ada_b, norm_mix_g, norm_ffn_g, t5_bias, a_w_qkv, a_q_gain, a_k_gain, a_lambda, a_subln_g, a_w_o, b_w_qkv, b_q_gain, b_k_gain, b_rel_bias, b_w_o, router_w, router_bias, moe_w_gate, moe_w_up, moe_w_down):
    batch, seq, d = x.shape
    depth = ada_w.shape[0]
    assert seq % ATTN_TILE == 0 and seq % ROW_TILE == 0 and d == A_HEADS * 2 * A_HEAD_DIM
    assert d == B_HEADS * B_HEAD_DIM and A_HEAD_DIM == B_HEAD_DIM

    c_pad = jnp.zeros((8, d), F32).at[:batch].set(c.astype(F32))
    mod = _modulation(c_pad, ada_w.astype(F32), ada_b.astype(F32))[:, :batch]
    mod = mod.reshape(depth, batch, 6, 1, d)

    rwt = router_w.astype(F32).T
    rwt_hi = lax.reduce_precision(rwt, exponent_bits=8, mantissa_bits=7)
    rwt = jnp.concatenate([rwt_hi, rwt - rwt_hi], axis=0).astype(BF16)
    rb = router_bias.astype(F32).reshape(-1, 1)

    x2 = x.astype(F32).reshape(batch * seq, d)
    for i in range(depth):
        sh1, sc1, g1, sh2, sc2, g2 = [mod[i, :, k] for k in range(6)]
        j = i // 2
        if i % 2 == 0:
            w_qkv, qg, kg, w_o = a_w_qkv[j], a_q_gain[j], a_k_gain[j], a_w_o[j]
        else:
            w_qkv, qg, kg, w_o = b_w_qkv[j], b_q_gain[j], b_k_gain[j], b_w_o[j]
        n_rep = d // qg.shape[0]
        gq = (jnp.tile(qg.astype(F32), n_rep) * (A_HEAD_DIM ** -0.5 * LOG2E)).reshape(d, 1)
        gk = jnp.tile(kg.astype(F32), n_rep).reshape(1, d)
        qt, k, vt = _qkv_proj(x2, norm_mix_g[i].reshape(1, d), sc1, sh1, w_qkv.astype(BF16),
                              gq, gk, seq, A_HEAD_DIM)
        if i % 2 == 0:
            lambda_init = 0.8 - 0.6 * math.exp(-0.3 * i)
            o = _attention_a(qt, k, vt, _t5_vectors(t5_bias, ATTN_TILE), a_lambda[j].astype(F32),
                             a_subln_g[j].reshape(1, -1), batch, seq, lambda_init)
        else:
            o = _attention_b(qt, k, vt, _band_vectors(b_rel_bias[j], BAND_TILE), batch, seq)
        x2, h, route = _out_proj(o, w_o.astype(BF16), x2, g1, norm_ffn_g[i].reshape(1, d),
                                 sc2, sh2, rwt, rb, seq)
        x2 = _moe(h, route, moe_w_gate, moe_w_up, moe_w_down, i, x2, g2, seq)
    return x2.reshape(batch, seq, d)
```

```python
import functools
import math

import numpy as np
import jax
import jax.numpy as jnp
from jax import lax
from jax.experimental import pallas as pl
from jax.experimental.pallas import tpu as pltpu

F32 = jnp.float32
BF16 = jnp.bfloat16

CHUNK = 64
A_HEADS = 8
A_HEAD_DIM = 64
T5_BUCKETS = 32
T5_MAX_DIST = 1024
B_HEADS = 16
B_HEAD_DIM = 64
LEFT_CHUNKS = 8
MAX_REL = 256
N_EXPERTS = 16
N_GROUPS = 4
E_PER_GROUP = N_EXPERTS // N_GROUPS
NORM_EPS = 1e-6
NEG_INF = -1e30
LOG2E = math.log2(math.e)

V7X_LANES = 128
V7X_MXU_DIM = 256

ATTN_TILE = 512
MOD_COL_TILE = 1536
FAR_UNROLL = 4
NEAR_AHEAD = 2
BAND_TILE = 256
BAND_QBLOCKS = 4
ROW_TILE = 512
MOE_TILE = 512
OPROJ_SPLIT = 4
MOE_CHUNK = 16
MOE_FFN_TILE = 512
MOE_FFN_SPLIT = 2
VMEM_LIMIT = 56 * 1024 * 1024

_NT = (((1,), (1,)), ((), ()))


def _cparams(sem):
    return pltpu.CompilerParams(dimension_semantics=sem, vmem_limit_bytes=VMEM_LIMIT)


def _mod_kernel(c_ref, w_ref, b_ref, o_ref):
    c = c_ref[...]
    s = c * (1.0 / (1.0 + jnp.exp(-c)))
    o_ref[...] = jnp.dot(s, w_ref[...], preferred_element_type=F32,
                         precision=lax.Precision.HIGHEST) + b_ref[...]


def _modulation(c_pad, ada_w, ada_b):
    depth, d, n = ada_w.shape
    rows = c_pad.shape[0]
    tn = MOD_COL_TILE
    return pl.pallas_call(
        _mod_kernel,
        out_shape=jax.ShapeDtypeStruct((depth, rows, n), F32),
        grid=(depth, n // tn),
        in_specs=[
            pl.BlockSpec((rows, d), lambda i, j: (0, 0)),
            pl.BlockSpec((None, d, tn), lambda i, j: (i, 0, j)),
            pl.BlockSpec((None, 1, tn), lambda i, j: (i, 0, j)),
        ],
        out_specs=pl.BlockSpec((None, rows, tn), lambda i, j: (i, 0, j)),
        compiler_params=_cparams(("parallel", "parallel")),
        name="adaln_mod",
    )(c_pad, ada_w, ada_b.reshape(depth, 1, n))


def _norm_mod(x, g, sc, sh):
    ms = jnp.mean(x * x, axis=-1, keepdims=True)
    return x * lax.rsqrt(ms + NORM_EPS) * g * (1.0 + sc) + sh


def _qkv_kernel(x_ref, g_ref, sc_ref, sh_ref, w_ref, gq_ref, gk_ref,
                qt_ref, k_ref, vt_ref, wqt_ref, wvt_ref, *, head_dim):
    d, tm = qt_ref.shape

    @pl.when(pl.program_id(0) == 0)
    def _():
        wqt_ref[...] = w_ref[:, :d].T
        wvt_ref[...] = w_ref[:, 2 * d:].T

    h = _norm_mod(x_ref[...], g_ref[...], sc_ref[...], sh_ref[...]).astype(BF16)
    y = lax.dot_general(wqt_ref[...], h, _NT, preferred_element_type=F32)
    y3 = y.reshape(d // head_dim, head_dim, tm)
    ss = jnp.mean(y3 * y3, axis=1, keepdims=True)
    qt_ref[...] = ((y3 * lax.rsqrt(ss + NORM_EPS)).reshape(d, tm) * gq_ref[...]).astype(BF16)
    cw = V7X_MXU_DIM
    r = lax.broadcasted_iota(jnp.int32, (cw, cw), 0) // head_dim
    c = lax.broadcasted_iota(jnp.int32, (cw, cw), 1) // head_dim
    gmat = jnp.where(r == c, 1.0 / head_dim, 0.0).astype(BF16)
    y = jnp.dot(h, w_ref[:, d:2 * d], preferred_element_type=F32)
    ysq = (y * y).astype(BF16)
    for ci in range(d // cw):
        cols = slice(ci * cw, (ci + 1) * cw)
        ss = jnp.dot(ysq[:, cols], gmat, preferred_element_type=F32)
        k_ref[:, cols] = (y[:, cols] * lax.rsqrt(ss + NORM_EPS) * gk_ref[:, cols]).astype(BF16)
    vt_ref[...] = lax.dot_general(wvt_ref[...], h, _NT, preferred_element_type=F32).astype(BF16)


def _qkv_proj(x2, g, sc, sh, w, gq, gk, seq, head_dim):
    t, d = x2.shape
    tm = ROW_TILE
    per_b = seq // tm
    vec = lambda i: (i // per_b, 0, 0)
    full = lambda i: (0, 0)
    return pl.pallas_call(
        functools.partial(_qkv_kernel, head_dim=head_dim),
        out_shape=(jax.ShapeDtypeStruct((d, t), BF16),
                   jax.ShapeDtypeStruct((t, d), BF16),
                   jax.ShapeDtypeStruct((d, t), BF16)),
        grid=(t // tm,),
        in_specs=[
            pl.BlockSpec((tm, d), lambda i: (i, 0)),
            pl.BlockSpec((1, d), full),
            pl.BlockSpec((None, 1, d), vec),
            pl.BlockSpec((None, 1, d), vec),
            pl.BlockSpec((d, 3 * d), full),
            pl.BlockSpec((d, 1), full),
            pl.BlockSpec((1, d), full),
        ],
        out_specs=(pl.BlockSpec((d, tm), lambda i: (0, i)),
                   pl.BlockSpec((tm, d), lambda i: (i, 0)),
                   pl.BlockSpec((d, tm), lambda i: (0, i))),
        scratch_shapes=[pltpu.VMEM((d, d), BF16), pltpu.VMEM((d, d), BF16)],
        compiler_params=_cparams(("arbitrary",)),
        name="qkv_proj",
    )(x2, g, sc, sh, w, gq, gk)


def _softmax_pv(s, vt1, m_ref, acc_ref):
    m_prev = m_ref[...]
    m_new = jnp.maximum(m_prev, jnp.max(s, axis=0, keepdims=True))
    alpha = jnp.exp2(m_prev - m_new)
    p = jnp.exp2(s - m_new).astype(BF16)
    acc_ref[...] = alpha * acc_ref[...] + jnp.dot(vt1, p, preferred_element_type=F32)
    m_ref[...] = m_new


def _split_maps(qt, head_dim):
    row = lax.broadcasted_iota(jnp.int32, qt.shape, 0)
    zero = jnp.zeros_like(qt)
    return jnp.where(row < head_dim, qt, zero), jnp.where(row >= head_dim, qt, zero)


ONES_ROWS = 16


def _init_state(refs):
    for m_ref, acc_ref in refs:
        m_ref[...] = jnp.full(m_ref.shape, NEG_INF, F32)
        acc_ref[...] = jnp.zeros(acc_ref.shape, F32)


def _normalized(acc_ref):
    acc = acc_ref[...]
    return acc[:V7X_LANES] / acc[V7X_LANES:V7X_LANES + 1]


def _toeplitz(x_row, n):
    x = jnp.broadcast_to(x_row, (n, x_row.shape[1]))
    return pltpu.roll(x, 0, 1, stride=1, stride_axis=0)[:, :n]


def _chunk_delta(n):
    kc = lax.broadcasted_iota(jnp.int32, (n, n), 0) // CHUNK
    qc = lax.broadcasted_iota(jnp.int32, (n, n), 1) // CHUNK
    return qc - kc


def _attn_a_kernel(qt_ref, k_ref, vt_ref, x_ref, lam_ref, sub_ref, o_ref,
                   tab, fbuf, nbuf, m_st, acc_st, *, n_off, lambda_init):
    tq = ATTN_TILE
    step = pl.program_id(2)

    @pl.when(step == 0)
    def _():
        for o in range(n_off):
            t = _toeplitz(x_ref[o], tq) * LOG2E
            if o == 0:
                t = jnp.where(_chunk_delta(tq) >= 0, t, NEG_INF)
            tab[o] = t

    for c in range(2):
        _init_state(((m_st.at[c, 0], acc_st.at[c, 0]), (m_st.at[c, 1], acc_st.at[c, 1])))
    q_maps = [_split_maps(qt_ref[:, c * tq:(c + 1) * tq], A_HEAD_DIM) for c in range(2)]

    def k_tile(j):
        return k_ref[pl.ds(pl.multiple_of(j * tq, tq), tq), :]

    def vt1_tile(j):
        vt = vt_ref[:, pl.ds(pl.multiple_of(j * tq, tq), tq)]
        return jnp.concatenate([vt, jnp.ones((ONES_ROWS, tq), vt.dtype)], axis=0)

    def scores(buf, k, c):
        for m in range(2):
            buf[c, m] = jnp.dot(k, q_maps[c][m], preferred_element_type=F32)

    def consume(buf, vt1, c, o):
        for m in range(2):
            s = buf[c, m]
            _softmax_pv(s if o >= n_off else s + tab[o], vt1, m_st.at[c, m], acc_st.at[c, m])

    def near_block(first_tile, n_tiles, first_scored):
        units = [[c for c in range(2) if n_tiles - 2 + c - s >= 0] for s in range(n_tiles)]
        buf_of = lambda s: fbuf.at[0] if s == 0 else nbuf.at[s - 1]

        def tile_scores(s):
            k = k_tile(first_tile + s)
            for c in units[s]:
                scores(buf_of(s), k, c)

        if not first_scored:
            tile_scores(0)
        for s in range(1, min(NEAR_AHEAD, n_tiles)):
            tile_scores(s)
        for s in range(n_tiles):
            if s + NEAR_AHEAD < n_tiles:
                tile_scores(s + NEAR_AHEAD)
            vt1 = vt1_tile(first_tile + s)
            for c in units[s]:
                consume(buf_of(s), vt1, c, n_tiles - 2 + c - s)

    first_general = (n_off - 1) // 2
    for i0 in range(first_general):
        @pl.when(step == i0)
        def _():
            near_block(0, 2 * i0 + 2, False)

    def far_step(j, parity):
        k = k_tile(j + 1)
        for c in range(2):
            scores(fbuf.at[1 - parity], k, c)
        vt1 = vt1_tile(j)
        for c in range(2):
            consume(fbuf.at[parity], vt1, c, n_off)

    @pl.when(step >= first_general)
    def _():
        k = k_tile(0)
        for c in range(2):
            scores(fbuf.at[0], k, c)

    n_far = jnp.maximum(2 * step - (n_off - 1), 0)
    rem = n_far % FAR_UNROLL

    @pl.when(rem >= 2)
    def _():
        far_step(0, 0)
        far_step(1, 1)

    def far_trip(t, carry):
        for u in range(FAR_UNROLL):
            far_step(FAR_UNROLL * t + rem + u, u % 2)
        return carry
    lax.fori_loop(0, n_far // FAR_UNROLL, far_trip, 0)

    @pl.when(step >= first_general)
    def _():
        near_block(n_far, n_off + 1, True)

    lam = lam_ref[...]
    lam_full = (jnp.exp(jnp.sum(lam[0:1] * lam[1:2], axis=-1, keepdims=True))
                - jnp.exp(jnp.sum(lam[2:3] * lam[3:4], axis=-1, keepdims=True)) + lambda_init)
    for c in range(2):
        a = _normalized(acc_st.at[c, 0]) - lam_full * _normalized(acc_st.at[c, 1])
        ms = jnp.mean(a * a, axis=0, keepdims=True)
        an = a * lax.rsqrt(ms + NORM_EPS)
        o_ref[c * tq:(c + 1) * tq, :] = (an.T * (sub_ref[...] * (1.0 - lambda_init))).astype(BF16)


def _attn_b_kernel(qt_ref, k_ref, vt_ref, x_ref, o_ref, tab, sbuf):
    tq = BAND_TILE
    n_blk = qt_ref.shape[1] // tq
    n_off = tab.shape[0] // 2
    step = pl.program_id(2)

    @pl.when(step == 0)
    def _():
        for m in range(2):
            for o in range(n_off):
                d = _chunk_delta(tq) + o * (tq // CHUNK)
                t = _toeplitz(x_ref[n_off * m + o], tq) * LOG2E
                tab[n_off * m + o] = jnp.where((d >= 0) & (d <= LEFT_CHUNKS), t, NEG_INF)

    q_maps = [_split_maps(qt_ref[:, c * tq:(c + 1) * tq], B_HEAD_DIM) for c in range(n_blk)]

    def run(first_step):
        pairs = [(c, o) for c in range(n_blk) for o in range(n_off - 1, -1, -1)
                 if not first_step or c - o >= 0]
        k_tiles, vt_tiles = {}, {}
        for c, o in pairs:
            if c - o not in k_tiles:
                ks = pl.multiple_of((n_blk * step + c - o) * tq, tq)
                k_tiles[c - o] = k_ref[pl.ds(ks, tq), :]
                vt = vt_ref[:, pl.ds(ks, tq)]
                vt_tiles[c - o] = jnp.concatenate([vt, jnp.ones((ONES_ROWS, tq), vt.dtype)], axis=0)
        def tile_scores(c):
            for o in [o for cc, o in pairs if cc == c]:
                for m in range(2):
                    sbuf[n_off * c + o, m] = jnp.dot(k_tiles[c - o], q_maps[c][m], preferred_element_type=F32)

        for c in range(min(NEAR_AHEAD, n_blk)):
            tile_scores(c)
        for c in range(n_blk):
            if c + NEAR_AHEAD < n_blk:
                tile_scores(c + NEAR_AHEAD)
            outs = []
            for m in range(2):
                offs = [o for cc, o in pairs if cc == c]
                ss = [sbuf[n_off * c + o, m] + tab[n_off * m + o] for o in offs]
                mx = functools.reduce(jnp.maximum, [jnp.max(s, axis=0, keepdims=True) for s in ss])
                acc = sum(jnp.dot(vt_tiles[c - o], jnp.exp2(s - mx).astype(BF16), preferred_element_type=F32)
                          for o, s in zip(offs, ss))
                outs.append(acc[:V7X_LANES] / acc[V7X_LANES:V7X_LANES + 1])
            row = lax.broadcasted_iota(jnp.int32, outs[0].shape, 0)
            o_ref[c * tq:(c + 1) * tq, :] = jnp.where(row < B_HEAD_DIM, outs[0], outs[1]).T.astype(BF16)

    @pl.when(step == 0)
    def _():
        run(True)

    @pl.when(step >= 1)
    def _():
        run(False)


def _attn_specs(batch, seq, nq, tq):
    return dict(
        q=pl.BlockSpec((V7X_LANES, tq), lambda b, h, i: (h, b * nq + i)),
        k=pl.BlockSpec((seq, V7X_LANES), lambda b, h, i: (b, h)),
        v=pl.BlockSpec((V7X_LANES, seq), lambda b, h, i: (h, b)),
        o=pl.BlockSpec((tq, V7X_LANES), lambda b, h, i: (b * nq + i, h)),
    )


def _attention_a(qt, k, vt, xvec, lam, sub_g, batch, seq, lambda_init):
    d, t = qt.shape
    tq = ATTN_TILE
    n_off = xvec.shape[1]
    assert n_off % 2 == 1 and seq % (2 * tq) == 0
    nq = seq // (2 * tq)
    sp = _attn_specs(batch, seq, nq, 2 * tq)
    rows = V7X_LANES + ONES_ROWS
    return pl.pallas_call(
        functools.partial(_attn_a_kernel, n_off=n_off, lambda_init=lambda_init),
        out_shape=jax.ShapeDtypeStruct((t, d), BF16),
        grid=(batch, d // V7X_LANES, nq),
        in_specs=[sp["q"], sp["k"], sp["v"],
                  pl.BlockSpec((None,) + xvec.shape[1:], lambda b, h, i: (h, 0, 0, 0)),
                  pl.BlockSpec(lam.shape, lambda b, h, i: (0, 0)),
                  pl.BlockSpec(sub_g.shape, lambda b, h, i: (0, 0))],
        out_specs=sp["o"],
        scratch_shapes=[pltpu.VMEM((n_off, tq, tq), F32), pltpu.VMEM((2, 2, 2, tq, tq), F32),
                        pltpu.VMEM((n_off, 2, 2, tq, tq), F32),
                        pltpu.VMEM((2, 2, 1, tq), F32), pltpu.VMEM((2, 2, rows, tq), F32)],
        compiler_params=_cparams(("parallel", "parallel", "arbitrary")),
        name="diff_attention",
    )(qt, k, vt, xvec, lam, sub_g)


def _attention_b(qt, k, vt, xvec, batch, seq):
    d, t = qt.shape
    tq = BAND_TILE
    g = BAND_QBLOCKS
    n_off = xvec.shape[1] // 2
    assert g >= n_off - 1 and seq % (g * tq) == 0
    nq = seq // (g * tq)
    sp = _attn_specs(batch, seq, nq, g * tq)
    return pl.pallas_call(
        _attn_b_kernel,
        out_shape=jax.ShapeDtypeStruct((t, d), BF16),
        grid=(batch, d // V7X_LANES, nq),
        in_specs=[sp["q"], sp["k"], sp["v"],
                  pl.BlockSpec((None,) + xvec.shape[1:], lambda b, h, i: (h, 0, 0, 0))],
        out_specs=sp["o"],
        scratch_shapes=[pltpu.VMEM((2 * n_off, tq, tq), F32), pltpu.VMEM((n_off * g, 2, tq, tq), F32)],
        compiler_params=_cparams(("parallel", "parallel", "arbitrary")),
        name="chunk_attention",
    )(qt, k, vt, xvec)


def _t5_bucket(rel):
    nb = T5_BUCKETS // 2
    ret = jnp.where(rel > 0, nb, 0)
    n = jnp.abs(rel)
    max_exact = nb // 2
    nf = jnp.maximum(n, 1).astype(F32)
    large = max_exact + (jnp.log(nf / max_exact) / math.log(T5_MAX_DIST / max_exact)
                         * (nb - max_exact)).astype(jnp.int32)
    large = jnp.minimum(large, nb - 1)
    return ret + jnp.where(n < max_exact, n, large)


def _t5_const_distance():
    nb = T5_BUCKETS // 2
    max_exact = nb // 2
    n = np.arange(max_exact, 4 * T5_MAX_DIST, dtype=np.float64)
    large = max_exact + np.floor(np.log(n / max_exact) / math.log(T5_MAX_DIST / max_exact) * (nb - max_exact))
    below = np.nonzero(large < nb - 1)[0]
    return int(n[below[-1]]) + 2


def _tile_rel(tile, n_off):
    i = jnp.arange(2 * tile, dtype=jnp.int32)
    rel = jnp.where(i < tile, -i, 2 * tile - i)
    return rel[None, :] - tile * jnp.arange(n_off, dtype=jnp.int32)[:, None]


def _t5_vectors(t5_bias, tile):
    n_off = 1
    while (n_off - 1) * tile + 1 < _t5_const_distance():
        n_off += 1
    tb = t5_bias.astype(F32)
    vals = tb[_t5_bucket(_tile_rel(tile, n_off))] - tb[T5_BUCKETS // 2 - 1]
    return vals.transpose(2, 0, 1)[:, :, None, :]


def _band_vectors(rel_bias, tile):
    n_off = LEFT_CHUNKS * CHUNK // tile + 1
    idx = jnp.clip(_tile_rel(tile, n_off), -MAX_REL, MAX_REL) + MAX_REL
    vals = rel_bias.astype(F32)[:, idx]
    return vals.reshape(rel_bias.shape[0] // 2, 2 * n_off, 1, 2 * tile)


def _top2_sum4(r0, r1, r2, r3):
    a, b = jnp.maximum(r0, r1), jnp.minimum(r0, r1)
    c, d = jnp.maximum(r2, r3), jnp.minimum(r2, r3)
    return jnp.maximum(a, c) + jnp.maximum(jnp.minimum(a, c), jnp.maximum(b, d))


def _route(logits, rbias):
    n_e, n = logits.shape
    scores = 1.0 / (1.0 + jnp.exp(-logits))
    sel = scores + rbias
    row = lax.broadcasted_iota(jnp.int32, sel.shape, 0)
    best = None
    for g in range(N_GROUPS):
        rows = [sel[g * E_PER_GROUP + i: g * E_PER_GROUP + i + 1, :] for i in range(E_PER_GROUP)]
        gs = _top2_sum4(*rows)
        if best is None:
            best, gidx = gs, jnp.zeros(gs.shape, jnp.int32)
        else:
            gidx = jnp.where(gs > best, g, gidx)
            best = jnp.maximum(best, gs)
    masked = jnp.where(row // E_PER_GROUP == gidx, sel, NEG_INF)
    m1 = jnp.max(masked, axis=0, keepdims=True)
    i1 = jnp.min(jnp.where(masked == m1, row, n_e), axis=0, keepdims=True)
    masked2 = jnp.where(row == i1, -3.0e38, masked)
    m2 = jnp.max(masked2, axis=0, keepdims=True)
    i2 = jnp.min(jnp.where(masked2 == m2, row, n_e), axis=0, keepdims=True)
    w1 = jnp.sum(jnp.where(row == i1, scores, 0.0), axis=0, keepdims=True)
    w2 = jnp.sum(jnp.where(row == i2, scores, 0.0), axis=0, keepdims=True)
    den = w1 + w2
    member = jnp.where((row == i1) | (row == i2), 1.0, 0.0).astype(BF16)
    t_from = lax.broadcasted_iota(jnp.int32, (n, n), 0)
    t_to = lax.broadcasted_iota(jnp.int32, (n, n), 1)
    before = jnp.where((t_from < t_to) & (t_from // MOE_TILE == t_to // MOE_TILE), 1.0, 0.0).astype(BF16)
    pos = jnp.dot(member, before, preferred_element_type=F32)
    p1 = jnp.sum(jnp.where(row == i1, pos, 0.0), axis=0, keepdims=True)
    p2 = jnp.sum(jnp.where(row == i2, pos, 0.0), axis=0, keepdims=True)
    out_row = lax.broadcasted_iota(jnp.int32, (8, n), 0)
    out = jnp.zeros((8, n), F32)
    for r, val in enumerate((i1.astype(F32), i2.astype(F32), p1, p2, w1 / den, w2 / den)):
        out = jnp.where(out_row == r, val, out)
    return out


def _oproj_kernel(o_ref, wo_ref, x_ref, g1_ref, gn_ref, sc_ref, sh_ref, rwt_ref, rb_ref,
                  xo_ref, h_ref, route_ref):
    n_e = rb_ref.shape[0]
    rw = rwt_ref[...]
    rows = o_ref.shape[0] // OPROJ_SPLIT
    groups = [slice(r * rows, (r + 1) * rows) for r in range(OPROJ_SPLIT)]
    ys = [jnp.dot(o_ref[g, :], wo_ref[...], preferred_element_type=F32) for g in groups]
    logits = []
    for g, y in zip(groups, ys):
        xn = x_ref[g, :] + g1_ref[...] * y
        xo_ref[g, :] = xn
        h = _norm_mod(xn, gn_ref[...], sc_ref[...], sh_ref[...])
        h_hi = h.astype(BF16)
        h_ref[g, :] = h_hi
        h_lo = (h - h_hi.astype(F32)).astype(BF16)
        main = lax.dot_general(rw, h_hi, _NT, preferred_element_type=F32)
        corr = lax.dot_general(rw[:n_e], h_lo, _NT, preferred_element_type=F32)
        logits.append(main[:n_e] + main[n_e:] + corr)
    route_ref[...] = _route(jnp.concatenate(logits, axis=1), rb_ref[...])


def _out_proj(o, wo_bf16, x2, g1, gn, sc, sh, rwt, rb, seq):
    t, d = x2.shape
    tm = ROW_TILE
    per_b = seq // tm
    n_e = rb.shape[0]
    vec = lambda i: (i // per_b, 0, 0)
    return pl.pallas_call(
        _oproj_kernel,
        out_shape=(jax.ShapeDtypeStruct((t, d), F32),
                   jax.ShapeDtypeStruct((t, d), BF16),
                   jax.ShapeDtypeStruct((8, t), F32)),
        grid=(t // tm,),
        in_specs=[
            pl.BlockSpec((tm, d), lambda i: (i, 0)),
            pl.BlockSpec((d, d), lambda i: (0, 0)),
            pl.BlockSpec((tm, d), lambda i: (i, 0)),
            pl.BlockSpec((None, 1, d), vec),
            pl.BlockSpec((1, d), lambda i: (0, 0)),
            pl.BlockSpec((None, 1, d), vec),
            pl.BlockSpec((None, 1, d), vec),
            pl.BlockSpec((2 * n_e, d), lambda i: (0, 0)),
            pl.BlockSpec((n_e, 1), lambda i: (0, 0)),
        ],
        out_specs=(pl.BlockSpec((tm, d), lambda i: (i, 0)),
                   pl.BlockSpec((tm, d), lambda i: (i, 0)),
                   pl.BlockSpec((8, tm), lambda i: (0, i))),
        compiler_params=_cparams(("parallel",)),
        name="out_proj_route",
    )(o, wo_bf16, x2, g1, gn, sc, sh, rwt, rb)


def _moe_plan(route, t, tm, n_e):
    n_tiles = t // tm
    ch, ft = MOE_CHUNK, MOE_FFN_TILE
    loc_rows, main_rows, n_sorted = _moe_rows(t, tm, n_e)
    ids = jnp.arange(n_e, dtype=jnp.int32)
    e = route[0:2].astype(jnp.int32)
    pos = route[2:4].astype(jnp.int32)
    oh = (e[:, :, None] == ids).astype(jnp.int32)
    cnt = oh.sum(0).reshape(n_tiles, tm, n_e).sum(1)
    seg = (cnt + ch - 1) // ch * ch
    loc = jnp.cumsum(seg, axis=1) - seg
    tot = seg.sum(0)
    totp = (tot + ft - 1) // ft * ft
    base = jnp.cumsum(totp) - totp
    gdest = base[None, :] + jnp.cumsum(seg, axis=0) - seg
    col = (oh * jnp.repeat(loc, tm, axis=0)[None]).sum(-1) + pos

    row0 = jnp.arange(loc_rows // ch, dtype=jnp.int32) * ch
    ej = (row0[None, :, None] >= (loc + seg)[:, None, :]).sum(-1)
    ohj = (jnp.minimum(ej, n_e - 1)[..., None] == ids).astype(jnp.int32)
    dst = (ohj * (gdest - loc)[:, None, :]).sum(-1) + row0[None, :]
    dump = main_rows + (jnp.arange(n_tiles, dtype=jnp.int32) % 2)[:, None] * loc_rows + row0[None, :]
    dst = jnp.where(ej < n_e, dst, dump) // ch

    r0 = jnp.arange(n_sorted // ft, dtype=jnp.int32) * ft
    ends = base + totp
    te = jnp.minimum((r0[:, None] >= ends[None, :]).sum(-1), n_e - 1)
    tvalid = (r0 < ends[-1]).astype(jnp.int32)
    tfirst = tvalid * (r0 == (((te[:, None] == ids) * base[None, :]).sum(-1))).astype(jnp.int32)
    return col, dst.astype(jnp.int32), te.astype(jnp.int32), tvalid, tfirst


def _moe_rows(t, tm, n_e):
    ch, ft = MOE_CHUNK, MOE_FFN_TILE
    loc_rows = 2 * tm + n_e * ch
    main = 2 * t + (t // tm) * n_e * (ch - 1) + n_e * (ft - 1)
    main = (main + ft - 1) // ft * ft
    return loc_rows, main, (main + 2 * loc_rows + ft - 1) // ft * ft


def _chunk_copies(dst_ref, tile, local_ref, sorted_ref, sem, to_sorted):
    ch = MOE_CHUNK
    copies = []
    for j in range(local_ref.shape[0] // ch):
        far = sorted_ref.at[pl.ds(pl.multiple_of(dst_ref[tile, j] * ch, ch), ch)]
        near = local_ref.at[pl.ds(j * ch, ch)]
        copies.append(pltpu.make_async_copy(near, far, sem) if to_sorted
                      else pltpu.make_async_copy(far, near, sem))
    return copies


def _dispatch_kernel(dst_ref, col_ref, h_ref, init_ref, xs_ref, comp, sem):
    del init_ref
    tile = pl.program_id(0)
    last = pl.num_programs(0) - 1
    slot = tile % 2

    def wait_slot(s):
        pltpu.make_async_copy(comp.at[s], xs_ref.at[pl.ds(0, comp.shape[1])], sem.at[s]).wait()

    @pl.when(tile >= 2)
    def _():
        wait_slot(slot)

    col = col_ref[...]
    srow = lax.broadcasted_iota(jnp.int32, (comp.shape[1], col.shape[1]), 0)
    sel = jnp.where((srow == col[0:1]) | (srow == col[1:2]), 1.0, 0.0).astype(BF16)
    comp[slot] = jnp.dot(sel, h_ref[...], preferred_element_type=F32).astype(BF16)
    for cp in _chunk_copies(dst_ref, tile, comp.at[slot], xs_ref, sem.at[slot], True):
        cp.start()

    @pl.when(tile == last)
    def _():
        wait_slot(slot)

    @pl.when((tile == last) & (tile >= 1))
    def _():
        wait_slot(1 - slot)


def _dispatch(dst, col, h, n_e, init):
    t, d = h.shape
    tm = MOE_TILE
    loc_rows, _, n_sorted = _moe_rows(t, tm, n_e)
    if init is None:
        init = jnp.zeros((n_sorted, d), BF16)
    return pl.pallas_call(
        _dispatch_kernel,
        out_shape=jax.ShapeDtypeStruct((n_sorted, d), BF16),
        grid_spec=pltpu.PrefetchScalarGridSpec(
            num_scalar_prefetch=1,
            grid=(t // tm,),
            in_specs=[pl.BlockSpec((2, tm), lambda i, dst: (0, i)),
                      pl.BlockSpec((tm, d), lambda i, dst: (i, 0)),
                      pl.BlockSpec(memory_space=pl.ANY)],
            out_specs=pl.BlockSpec(memory_space=pl.ANY),
            scratch_shapes=[pltpu.VMEM((2, loc_rows, d), BF16), pltpu.SemaphoreType.DMA((2,))],
        ),
        input_output_aliases={3: 0},
        compiler_params=_cparams(("arbitrary",)),
        name="moe_dispatch",
    )(dst, col, h, init)


def _expert_kernel(te_ref, tv_ref, tf_ref, xs_ref, wg_ref, wu_ref, wd_ref, ys_ref, wg_b, wu_b, wd_b):
    del te_ref
    s = pl.program_id(0)

    @pl.when(tf_ref[s] == 1)
    def _():
        wg_b[...] = wg_ref[...].astype(BF16)
        wu_b[...] = wu_ref[...].astype(BF16)
        wd_b[...] = wd_ref[...].astype(BF16)

    @pl.when(tv_ref[s] == 1)
    def _():
        rows = ys_ref.shape[0] // MOE_FFN_SPLIT
        hs = []
        for r in range(MOE_FFN_SPLIT):
            xs = xs_ref[r * rows:(r + 1) * rows, :]
            hg = jnp.dot(xs, wg_b[...], preferred_element_type=F32)
            hu = jnp.dot(xs, wu_b[...], preferred_element_type=F32)
            hs.append((hg, hu))
        for r, (hg, hu) in enumerate(hs):
            he = hg * (1.0 / (1.0 + jnp.exp(-hg))) * hu
            ys_ref[r * rows:(r + 1) * rows, :] = jnp.dot(
                he.astype(BF16), wd_b[...], preferred_element_type=F32).astype(BF16)

    @pl.when(tv_ref[s] == 0)
    def _():
        ys_ref[...] = jnp.zeros(ys_ref.shape, ys_ref.dtype)


def _experts(te, tvalid, tfirst, xs, wg, wu, wd, layer):
    n_sorted, d = xs.shape
    f = wg.shape[3]
    ft = MOE_FFN_TILE
    wspec = lambda r, c: pl.BlockSpec((None, None, r, c), lambda s, te, tv, tf: (layer, te[s], 0, 0))
    return pl.pallas_call(
        _expert_kernel,
        out_shape=jax.ShapeDtypeStruct((n_sorted, d), BF16),
        grid_spec=pltpu.PrefetchScalarGridSpec(
            num_scalar_prefetch=3,
            grid=(n_sorted // ft,),
            in_specs=[pl.BlockSpec((ft, d), lambda s, te, tv, tf: (s, 0)),
                      wspec(d, f), wspec(d, f), wspec(f, d)],
            out_specs=pl.BlockSpec((ft, d), lambda s, te, tv, tf: (s, 0)),
            scratch_shapes=[pltpu.VMEM((d, f), BF16), pltpu.VMEM((d, f), BF16), pltpu.VMEM((f, d), BF16)],
        ),
        compiler_params=_cparams(("arbitrary",)),
        name="moe_experts",
    )(te, tvalid, tfirst, xs, wg, wu, wd)


def _combine_kernel(dst_ref, col_ref, w_ref, ys_ref, x_ref, g2_ref, o_ref, comp, sem):
    tile = pl.program_id(0)
    slot = tile % 2

    def fetch(t, s):
        for cp in _chunk_copies(dst_ref, t, comp.at[s], ys_ref, sem.at[s], False):
            cp.start()

    def wait_slot(s):
        pltpu.make_async_copy(ys_ref.at[pl.ds(0, comp.shape[1])], comp.at[s], sem.at[s]).wait()

    last = pl.num_programs(0) - 1

    @pl.when(tile == 0)
    def _():
        fetch(tile, slot)

    fetch(jnp.minimum(tile + 1, last), 1 - slot)

    col = col_ref[...]
    w = w_ref[...]
    scol = lax.broadcasted_iota(jnp.int32, (col.shape[0], comp.shape[1]), 1)
    selw = (jnp.where(scol == col[:, 0:1], w[:, 4:5], 0.0)
            + jnp.where(scol == col[:, 1:2], w[:, 5:6], 0.0)).astype(BF16)
    wait_slot(slot)
    acc = jnp.dot(selw, comp[slot], preferred_element_type=F32)
    o_ref[...] = x_ref[...] + g2_ref[...] * acc

    @pl.when(tile == last)
    def _():
        wait_slot(1 - slot)


def _combine(dst, col_t, route_t, ys, x2, g2, seq, n_e):
    t, d = x2.shape
    tm = MOE_TILE
    per_b = seq // tm
    loc_rows = _moe_rows(t, tm, n_e)[0]
    return pl.pallas_call(
        _combine_kernel,
        out_shape=jax.ShapeDtypeStruct((t, d), F32),
        grid_spec=pltpu.PrefetchScalarGridSpec(
            num_scalar_prefetch=1,
            grid=(t // tm,),
            in_specs=[pl.BlockSpec((tm, 2), lambda i, dst: (i, 0)),
                      pl.BlockSpec((tm, 8), lambda i, dst: (i, 0)),
                      pl.BlockSpec(memory_space=pl.ANY),
                      pl.BlockSpec((tm, d), lambda i, dst: (i, 0)),
                      pl.BlockSpec((None, 1, d), lambda i, dst: (i // per_b, 0, 0))],
            out_specs=pl.BlockSpec((tm, d), lambda i, dst: (i, 0)),
            scratch_shapes=[pltpu.VMEM((2, loc_rows, d), BF16), pltpu.SemaphoreType.DMA((2,))],
        ),
        compiler_params=_cparams(("arbitrary",)),
        name="moe_combine",
    )(dst, col_t, route_t, ys, x2, g2)


def _moe(h, route, wg, wu, wd, layer, x2, g2, seq, sorted_init):
    t = x2.shape[0]
    n_e = wg.shape[1]
    col, dst, te, tvalid, tfirst = _moe_plan(route, t, MOE_TILE, n_e)
    xs = _dispatch(dst, col, h, n_e, sorted_init)
    ys = _experts(te, tvalid, tfirst, xs, wg, wu, wd, layer)
    return _combine(dst, col.T, route.T, ys, x2, g2, seq, n_e), ys


def kernel(x, c, ada_w, ada_b, norm_mix_g, norm_ffn_g, t5_bias, a_w_qkv, a_q_gain, a_k_gain, a_lambda, a_subln_g, a_w_o, b_w_qkv, b_q_gain, b_k_gain, b_rel_bias, b_w_o, router_w, router_bias, moe_w_gate, moe_w_up, moe_w_down):
    batch, seq, d = x.shape
    depth = ada_w.shape[0]
    assert seq % ATTN_TILE == 0 and seq % ROW_TILE == 0 and d == A_HEADS * 2 * A_HEAD_DIM
    assert d == B_HEADS * B_HEAD_DIM and A_HEAD_DIM == B_HEAD_DIM

    c_pad = jnp.zeros((8, d), F32).at[:batch].set(c.astype(F32))
    mod = _modulation(c_pad, ada_w.astype(F32), ada_b.astype(F32))[:, :batch]
    mod = mod.reshape(depth, batch, 6, 1, d)

    rwt = router_w.astype(F32).T
    rwt_hi = lax.reduce_precision(rwt, exponent_bits=8, mantissa_bits=7)
    rwt = jnp.concatenate([rwt_hi, rwt - rwt_hi], axis=0).astype(BF16)
    rb = router_bias.astype(F32).reshape(-1, 1)

    x2 = x.astype(F32).reshape(batch * seq, d)
    sorted_buf = None
    for i in range(depth):
        sh1, sc1, g1, sh2, sc2, g2 = [mod[i, :, k] for k in range(6)]
        j = i // 2
        if i % 2 == 0:
            w_qkv, qg, kg, w_o = a_w_qkv[j], a_q_gain[j], a_k_gain[j], a_w_o[j]
        else:
            w_qkv, qg, kg, w_o = b_w_qkv[j], b_q_gain[j], b_k_gain[j], b_w_o[j]
        n_rep = d // qg.shape[0]
        gq = (jnp.tile(qg.astype(F32), n_rep) * (A_HEAD_DIM ** -0.5 * LOG2E)).reshape(d, 1)
        gk = jnp.tile(kg.astype(F32), n_rep).reshape(1, d)
        qt, k, vt = _qkv_proj(x2, norm_mix_g[i].reshape(1, d), sc1, sh1, w_qkv.astype(BF16),
                              gq, gk, seq, A_HEAD_DIM)
        if i % 2 == 0:
            lambda_init = 0.8 - 0.6 * math.exp(-0.3 * i)
            o = _attention_a(qt, k, vt, _t5_vectors(t5_bias, ATTN_TILE), a_lambda[j].astype(F32),
                             a_subln_g[j].reshape(1, -1), batch, seq, lambda_init)
        else:
            o = _attention_b(qt, k, vt, _band_vectors(b_rel_bias[j], BAND_TILE), batch, seq)
        x2, h, route = _out_proj(o, w_o.astype(BF16), x2, g1, norm_ffn_g[i].reshape(1, d),
                                 sc2, sh2, rwt, rb, seq)
        x2, sorted_buf = _moe(h, route, moe_w_gate, moe_w_up, moe_w_down, i, x2, g2, seq, sorted_buf)
    return x2.reshape(batch, seq, d)
```

```python
import functools
import math

import numpy as np
import jax
import jax.numpy as jnp
from jax import lax
from jax.experimental import pallas as pl
from jax.experimental.pallas import tpu as pltpu

F32 = jnp.float32
BF16 = jnp.bfloat16

CHUNK = 64
A_HEADS = 8
A_HEAD_DIM = 64
T5_BUCKETS = 32
T5_MAX_DIST = 1024
B_HEADS = 16
B_HEAD_DIM = 64
LEFT_CHUNKS = 8
MAX_REL = 256
N_EXPERTS = 16
N_GROUPS = 4
E_PER_GROUP = N_EXPERTS // N_GROUPS
NORM_EPS = 1e-6
NEG_INF = -1e30
LOG2E = math.log2(math.e)

V7X_LANES = 128
V7X_MXU_DIM = 256

ATTN_TILE = 512
MOD_COL_TILE = 1536
FAR_UNROLL = 4
NEAR_AHEAD = 2
BAND_TILE = 256
BAND_QBLOCKS = 16
ROW_TILE = 512
MOE_TILE = 512
OPROJ_SPLIT = 4
MOE_CHUNK = 16
MOE_FFN_TILE = 512
MOE_FFN_SPLIT = 2
VMEM_LIMIT = 56 * 1024 * 1024

_NT = (((1,), (1,)), ((), ()))


def _cparams(sem):
    return pltpu.CompilerParams(dimension_semantics=sem, vmem_limit_bytes=VMEM_LIMIT)


def _mod_kernel(c_ref, w_ref, b_ref, o_ref):
    c = c_ref[...]
    s = c * (1.0 / (1.0 + jnp.exp(-c)))
    o_ref[...] = jnp.dot(s, w_ref[...], preferred_element_type=F32,
                         precision=lax.Precision.HIGHEST) + b_ref[...]


def _modulation(c_pad, ada_w, ada_b):
    depth, d, n = ada_w.shape
    rows = c_pad.shape[0]
    tn = MOD_COL_TILE
    return pl.pallas_call(
        _mod_kernel,
        out_shape=jax.ShapeDtypeStruct((depth, rows, n), F32),
        grid=(depth, n // tn),
        in_specs=[
            pl.BlockSpec((rows, d), lambda i, j: (0, 0)),
            pl.BlockSpec((None, d, tn), lambda i, j: (i, 0, j)),
            pl.BlockSpec((None, 1, tn), lambda i, j: (i, 0, j)),
        ],
        out_specs=pl.BlockSpec((None, rows, tn), lambda i, j: (i, 0, j)),
        compiler_params=_cparams(("parallel", "parallel")),
        name="adaln_mod",
    )(c_pad, ada_w, ada_b.reshape(depth, 1, n))


def _norm_mod(x, g, sc, sh):
    ms = jnp.mean(x * x, axis=-1, keepdims=True)
    return x * lax.rsqrt(ms + NORM_EPS) * g * (1.0 + sc) + sh


def _qkv_kernel(x_ref, g_ref, sc_ref, sh_ref, w_ref, gq_ref, gk_ref,
                qt_ref, k_ref, vt_ref, wqt_ref, wvt_ref, *, head_dim):
    d, tm = qt_ref.shape

    @pl.when(pl.program_id(0) == 0)
    def _():
        wqt_ref[...] = w_ref[:, :d].T
        wvt_ref[...] = w_ref[:, 2 * d:].T

    h = _norm_mod(x_ref[...], g_ref[...], sc_ref[...], sh_ref[...]).astype(BF16)
    y = lax.dot_general(wqt_ref[...], h, _NT, preferred_element_type=F32)
    y3 = y.reshape(d // head_dim, head_dim, tm)
    ss = jnp.mean(y3 * y3, axis=1, keepdims=True)
    qt_ref[...] = ((y3 * lax.rsqrt(ss + NORM_EPS)).reshape(d, tm) * gq_ref[...]).astype(BF16)
    cw = V7X_MXU_DIM
    r = lax.broadcasted_iota(jnp.int32, (cw, cw), 0) // head_dim
    c = lax.broadcasted_iota(jnp.int32, (cw, cw), 1) // head_dim
    gmat = jnp.where(r == c, 1.0 / head_dim, 0.0).astype(BF16)
    y = jnp.dot(h, w_ref[:, d:2 * d], preferred_element_type=F32)
    ysq = (y * y).astype(BF16)
    for ci in range(d // cw):
        cols = slice(ci * cw, (ci + 1) * cw)
        ss = jnp.dot(ysq[:, cols], gmat, preferred_element_type=F32)
        k_ref[:, cols] = (y[:, cols] * lax.rsqrt(ss + NORM_EPS) * gk_ref[:, cols]).astype(BF16)
    vt_ref[...] = lax.dot_general(wvt_ref[...], h, _NT, preferred_element_type=F32).astype(BF16)


def _qkv_proj(x2, g, sc, sh, w, gq, gk, seq, head_dim):
    t, d = x2.shape
    tm = ROW_TILE
    per_b = seq // tm
    vec = lambda i: (i // per_b, 0, 0)
    full = lambda i: (0, 0)
    return pl.pallas_call(
        functools.partial(_qkv_kernel, head_dim=head_dim),
        out_shape=(jax.ShapeDtypeStruct((d, t), BF16),
                   jax.ShapeDtypeStruct((t, d), BF16),
                   jax.ShapeDtypeStruct((d, t), BF16)),
        grid=(t // tm,),
        in_specs=[
            pl.BlockSpec((tm, d), lambda i: (i, 0)),
            pl.BlockSpec((1, d), full),
            pl.BlockSpec((None, 1, d), vec),
            pl.BlockSpec((None, 1, d), vec),
            pl.BlockSpec((d, 3 * d), full),
            pl.BlockSpec((d, 1), full),
            pl.BlockSpec((1, d), full),
        ],
        out_specs=(pl.BlockSpec((d, tm), lambda i: (0, i)),
                   pl.BlockSpec((tm, d), lambda i: (i, 0)),
                   pl.BlockSpec((d, tm), lambda i: (0, i))),
        scratch_shapes=[pltpu.VMEM((d, d), BF16), pltpu.VMEM((d, d), BF16)],
        compiler_params=_cparams(("arbitrary",)),
        name="qkv_proj",
    )(x2, g, sc, sh, w, gq, gk)


def _softmax_pv(s, vt1, m_ref, acc_ref):
    m_prev = m_ref[...]
    m_new = jnp.maximum(m_prev, jnp.max(s, axis=0, keepdims=True))
    alpha = jnp.exp2(m_prev - m_new)
    p = jnp.exp2(s - m_new).astype(BF16)
    acc_ref[...] = alpha * acc_ref[...] + jnp.dot(vt1, p, preferred_element_type=F32)
    m_ref[...] = m_new


def _split_maps(qt, head_dim):
    row = lax.broadcasted_iota(jnp.int32, qt.shape, 0)
    zero = jnp.zeros_like(qt)
    return jnp.where(row < head_dim, qt, zero), jnp.where(row >= head_dim, qt, zero)


ONES_ROWS = 16


def _init_state(refs):
    for m_ref, acc_ref in refs:
        m_ref[...] = jnp.full(m_ref.shape, NEG_INF, F32)
        acc_ref[...] = jnp.zeros(acc_ref.shape, F32)


def _normalized(acc_ref):
    acc = acc_ref[...]
    return acc[:V7X_LANES] / acc[V7X_LANES:V7X_LANES + 1]


def _toeplitz(x_row, n):
    x = jnp.broadcast_to(x_row, (n, x_row.shape[1]))
    return pltpu.roll(x, 0, 1, stride=1, stride_axis=0)[:, :n]


def _chunk_delta(n):
    kc = lax.broadcasted_iota(jnp.int32, (n, n), 0) // CHUNK
    qc = lax.broadcasted_iota(jnp.int32, (n, n), 1) // CHUNK
    return qc - kc


def _attn_a_kernel(qt_ref, k_ref, vt_ref, x_ref, lam_ref, sub_ref, o_ref,
                   tab, fbuf, nbuf, m_st, acc_st, *, n_off, lambda_init):
    tq = ATTN_TILE
    step = pl.program_id(2)

    @pl.when(step == 0)
    def _():
        for o in range(n_off):
            t = _toeplitz(x_ref[o], tq) * LOG2E
            if o == 0:
                t = jnp.where(_chunk_delta(tq) >= 0, t, NEG_INF)
            tab[o] = t

    for c in range(2):
        _init_state(((m_st.at[c, 0], acc_st.at[c, 0]), (m_st.at[c, 1], acc_st.at[c, 1])))
    q_maps = [_split_maps(qt_ref[:, c * tq:(c + 1) * tq], A_HEAD_DIM) for c in range(2)]

    def k_tile(j):
        return k_ref[pl.ds(pl.multiple_of(j * tq, tq), tq), :]

    def vt1_tile(j):
        vt = vt_ref[:, pl.ds(pl.multiple_of(j * tq, tq), tq)]
        return jnp.concatenate([vt, jnp.ones((ONES_ROWS, tq), vt.dtype)], axis=0)

    def scores(buf, k, c):
        for m in range(2):
            buf[c, m] = jnp.dot(k, q_maps[c][m], preferred_element_type=F32)

    def consume(buf, vt1, c, o):
        for m in range(2):
            s = buf[c, m]
            _softmax_pv(s if o >= n_off else s + tab[o], vt1, m_st.at[c, m], acc_st.at[c, m])

    def near_block(first_tile, n_tiles, first_scored):
        units = [[c for c in range(2) if n_tiles - 2 + c - s >= 0] for s in range(n_tiles)]
        buf_of = lambda s: fbuf.at[0] if s == 0 else nbuf.at[s - 1]

        def tile_scores(s):
            k = k_tile(first_tile + s)
            for c in units[s]:
                scores(buf_of(s), k, c)

        if not first_scored:
            tile_scores(0)
        for s in range(1, min(NEAR_AHEAD, n_tiles)):
            tile_scores(s)
        for s in range(n_tiles):
            if s + NEAR_AHEAD < n_tiles:
                tile_scores(s + NEAR_AHEAD)
            vt1 = vt1_tile(first_tile + s)
            for c in units[s]:
                consume(buf_of(s), vt1, c, n_tiles - 2 + c - s)

    first_general = (n_off - 1) // 2
    for i0 in range(first_general):
        @pl.when(step == i0)
        def _():
            near_block(0, 2 * i0 + 2, False)

    def far_step(j, parity):
        k = k_tile(j + 1)
        for c in range(2):
            scores(fbuf.at[1 - parity], k, c)
        vt1 = vt1_tile(j)
        for c in range(2):
            consume(fbuf.at[parity], vt1, c, n_off)

    @pl.when(step >= first_general)
    def _():
        k = k_tile(0)
        for c in range(2):
            scores(fbuf.at[0], k, c)

    n_far = jnp.maximum(2 * step - (n_off - 1), 0)
    rem = n_far % FAR_UNROLL

    @pl.when(rem >= 2)
    def _():
        far_step(0, 0)
        far_step(1, 1)

    def far_trip(t, carry):
        for u in range(FAR_UNROLL):
            far_step(FAR_UNROLL * t + rem + u, u % 2)
        return carry
    lax.fori_loop(0, n_far // FAR_UNROLL, far_trip, 0)

    @pl.when(step >= first_general)
    def _():
        near_block(n_far, n_off + 1, True)

    lam = lam_ref[...]
    lam_full = (jnp.exp(jnp.sum(lam[0:1] * lam[1:2], axis=-1, keepdims=True))
                - jnp.exp(jnp.sum(lam[2:3] * lam[3:4], axis=-1, keepdims=True)) + lambda_init)
    for c in range(2):
        a = _normalized(acc_st.at[c, 0]) - lam_full * _normalized(acc_st.at[c, 1])
        ms = jnp.mean(a * a, axis=0, keepdims=True)
        an = a * lax.rsqrt(ms + NORM_EPS)
        o_ref[c * tq:(c + 1) * tq, :] = (an.T * (sub_ref[...] * (1.0 - lambda_init))).astype(BF16)


def _attn_b_kernel(qt_ref, k_ref, vt_ref, x_ref, o_ref, tab, sbuf):
    tq = BAND_TILE
    n_blk = qt_ref.shape[1] // tq
    n_off = tab.shape[0] // 2
    step = pl.program_id(2)

    @pl.when(step == 0)
    def _():
        for m in range(2):
            for o in range(n_off):
                d = _chunk_delta(tq) + o * (tq // CHUNK)
                t = _toeplitz(x_ref[n_off * m + o], tq) * LOG2E
                tab[n_off * m + o] = jnp.where((d >= 0) & (d <= LEFT_CHUNKS), t, NEG_INF)

    q_maps = [_split_maps(qt_ref[:, c * tq:(c + 1) * tq], B_HEAD_DIM) for c in range(n_blk)]

    def run(first_step):
        pairs = [(c, o) for c in range(n_blk) for o in range(n_off - 1, -1, -1)
                 if not first_step or c - o >= 0]
        k_tiles, vt_tiles = {}, {}
        for c, o in pairs:
            if c - o not in k_tiles:
                ks = pl.multiple_of((n_blk * step + c - o) * tq, tq)
                k_tiles[c - o] = k_ref[pl.ds(ks, tq), :]
                vt = vt_ref[:, pl.ds(ks, tq)]
                vt_tiles[c - o] = jnp.concatenate([vt, jnp.ones((ONES_ROWS, tq), vt.dtype)], axis=0)
        def tile_scores(c):
            for o in [o for cc, o in pairs if cc == c]:
                for m in range(2):
                    sbuf[n_off * c + o, m] = jnp.dot(k_tiles[c - o], q_maps[c][m], preferred_element_type=F32)

        for c in range(min(NEAR_AHEAD, n_blk)):
            tile_scores(c)
        for c in range(n_blk):
            if c + NEAR_AHEAD < n_blk:
                tile_scores(c + NEAR_AHEAD)
            outs = []
            for m in range(2):
                offs = [o for cc, o in pairs if cc == c]
                ss = [sbuf[n_off * c + o, m] + tab[n_off * m + o] for o in offs]
                mx = functools.reduce(jnp.maximum, [jnp.max(s, axis=0, keepdims=True) for s in ss])
                acc = sum(jnp.dot(vt_tiles[c - o], jnp.exp2(s - mx).astype(BF16), preferred_element_type=F32)
                          for o, s in zip(offs, ss))
                outs.append(acc[:V7X_LANES] / acc[V7X_LANES:V7X_LANES + 1])
            row = lax.broadcasted_iota(jnp.int32, outs[0].shape, 0)
            o_ref[c * tq:(c + 1) * tq, :] = jnp.where(row < B_HEAD_DIM, outs[0], outs[1]).T.astype(BF16)

    @pl.when(step == 0)
    def _():
        run(True)

    @pl.when(step >= 1)
    def _():
        run(False)


def _attn_specs(batch, seq, nq, tq):
    return dict(
        q=pl.BlockSpec((V7X_LANES, tq), lambda b, h, i: (h, b * nq + i)),
        k=pl.BlockSpec((seq, V7X_LANES), lambda b, h, i: (b, h)),
        v=pl.BlockSpec((V7X_LANES, seq), lambda b, h, i: (h, b)),
        o=pl.BlockSpec((tq, V7X_LANES), lambda b, h, i: (b * nq + i, h)),
    )


def _attention_a(qt, k, vt, xvec, lam, sub_g, batch, seq, lambda_init):
    d, t = qt.shape
    tq = ATTN_TILE
    n_off = xvec.shape[1]
    assert n_off % 2 == 1 and seq % (2 * tq) == 0
    nq = seq // (2 * tq)
    sp = _attn_specs(batch, seq, nq, 2 * tq)
    rows = V7X_LANES + ONES_ROWS
    return pl.pallas_call(
        functools.partial(_attn_a_kernel, n_off=n_off, lambda_init=lambda_init),
        out_shape=jax.ShapeDtypeStruct((t, d), BF16),
        grid=(batch, d // V7X_LANES, nq),
        in_specs=[sp["q"], sp["k"], sp["v"],
                  pl.BlockSpec((None,) + xvec.shape[1:], lambda b, h, i: (h, 0, 0, 0)),
                  pl.BlockSpec(lam.shape, lambda b, h, i: (0, 0)),
                  pl.BlockSpec(sub_g.shape, lambda b, h, i: (0, 0))],
        out_specs=sp["o"],
        scratch_shapes=[pltpu.VMEM((n_off, tq, tq), F32), pltpu.VMEM((2, 2, 2, tq, tq), F32),
                        pltpu.VMEM((n_off, 2, 2, tq, tq), F32),
                        pltpu.VMEM((2, 2, 1, tq), F32), pltpu.VMEM((2, 2, rows, tq), F32)],
        compiler_params=_cparams(("parallel", "parallel", "arbitrary")),
        name="diff_attention",
    )(qt, k, vt, xvec, lam, sub_g)


def _attention_b(qt, k, vt, xvec, batch, seq):
    d, t = qt.shape
    tq = BAND_TILE
    g = BAND_QBLOCKS
    n_off = xvec.shape[1] // 2
    assert g >= n_off - 1 and seq % (g * tq) == 0
    nq = seq // (g * tq)
    sp = _attn_specs(batch, seq, nq, g * tq)
    return pl.pallas_call(
        _attn_b_kernel,
        out_shape=jax.ShapeDtypeStruct((t, d), BF16),
        grid=(batch, d // V7X_LANES, nq),
        in_specs=[sp["q"], sp["k"], sp["v"],
                  pl.BlockSpec((None,) + xvec.shape[1:], lambda b, h, i: (h, 0, 0, 0))],
        out_specs=sp["o"],
        scratch_shapes=[pltpu.VMEM((2 * n_off, tq, tq), F32), pltpu.VMEM((n_off * g, 2, tq, tq), F32)],
        compiler_params=_cparams(("parallel", "parallel", "arbitrary")),
        name="chunk_attention",
    )(qt, k, vt, xvec)


def _t5_bucket(rel):
    nb = T5_BUCKETS // 2
    ret = jnp.where(rel > 0, nb, 0)
    n = jnp.abs(rel)
    max_exact = nb // 2
    nf = jnp.maximum(n, 1).astype(F32)
    large = max_exact + (jnp.log(nf / max_exact) / math.log(T5_MAX_DIST / max_exact)
                         * (nb - max_exact)).astype(jnp.int32)
    large = jnp.minimum(large, nb - 1)
    return ret + jnp.where(n < max_exact, n, large)


def _t5_const_distance():
    nb = T5_BUCKETS // 2
    max_exact = nb // 2
    n = np.arange(max_exact, 4 * T5_MAX_DIST, dtype=np.float64)
    large = max_exact + np.floor(np.log(n / max_exact) / math.log(T5_MAX_DIST / max_exact) * (nb - max_exact))
    below = np.nonzero(large < nb - 1)[0]
    return int(n[below[-1]]) + 2


def _tile_rel(tile, n_off):
    i = jnp.arange(2 * tile, dtype=jnp.int32)
    rel = jnp.where(i < tile, -i, 2 * tile - i)
    return rel[None, :] - tile * jnp.arange(n_off, dtype=jnp.int32)[:, None]


def _t5_vectors(t5_bias, tile):
    n_off = 1
    while (n_off - 1) * tile + 1 < _t5_const_distance():
        n_off += 1
    tb = t5_bias.astype(F32)
    vals = tb[_t5_bucket(_tile_rel(tile, n_off))] - tb[T5_BUCKETS // 2 - 1]
    return vals.transpose(2, 0, 1)[:, :, None, :]


def _band_vectors(rel_bias, tile):
    n_off = LEFT_CHUNKS * CHUNK // tile + 1
    idx = jnp.clip(_tile_rel(tile, n_off), -MAX_REL, MAX_REL) + MAX_REL
    vals = rel_bias.astype(F32)[:, idx]
    return vals.reshape(rel_bias.shape[0] // 2, 2 * n_off, 1, 2 * tile)


def _top2_sum4(r0, r1, r2, r3):
    a, b = jnp.maximum(r0, r1), jnp.minimum(r0, r1)
    c, d = jnp.maximum(r2, r3), jnp.minimum(r2, r3)
    return jnp.maximum(a, c) + jnp.maximum(jnp.minimum(a, c), jnp.maximum(b, d))


def _route(logits, rbias):
    n_e, n = logits.shape
    scores = 1.0 / (1.0 + jnp.exp(-logits))
    sel = scores + rbias
    row = lax.broadcasted_iota(jnp.int32, sel.shape, 0)
    best = None
    for g in range(N_GROUPS):
        rows = [sel[g * E_PER_GROUP + i: g * E_PER_GROUP + i + 1, :] for i in range(E_PER_GROUP)]
        gs = _top2_sum4(*rows)
        if best is None:
            best, gidx = gs, jnp.zeros(gs.shape, jnp.int32)
        else:
            gidx = jnp.where(gs > best, g, gidx)
            best = jnp.maximum(best, gs)
    masked = jnp.where(row // E_PER_GROUP == gidx, sel, NEG_INF)
    m1 = jnp.max(masked, axis=0, keepdims=True)
    i1 = jnp.min(jnp.where(masked == m1, row, n_e), axis=0, keepdims=True)
    masked2 = jnp.where(row == i1, -3.0e38, masked)
    m2 = jnp.max(masked2, axis=0, keepdims=True)
    i2 = jnp.min(jnp.where(masked2 == m2, row, n_e), axis=0, keepdims=True)
    w1 = jnp.sum(jnp.where(row == i1, scores, 0.0), axis=0, keepdims=True)
    w2 = jnp.sum(jnp.where(row == i2, scores, 0.0), axis=0, keepdims=True)
    den = w1 + w2
    member = jnp.where((row == i1) | (row == i2), 1.0, 0.0).astype(BF16)
    t_from = lax.broadcasted_iota(jnp.int32, (n, n), 0)
    t_to = lax.broadcasted_iota(jnp.int32, (n, n), 1)
    before = jnp.where((t_from < t_to) & (t_from // MOE_TILE == t_to // MOE_TILE), 1.0, 0.0).astype(BF16)
    pos = jnp.dot(member, before, preferred_element_type=F32)
    p1 = jnp.sum(jnp.where(row == i1, pos, 0.0), axis=0, keepdims=True)
    p2 = jnp.sum(jnp.where(row == i2, pos, 0.0), axis=0, keepdims=True)
    out_row = lax.broadcasted_iota(jnp.int32, (8, n), 0)
    out = jnp.zeros((8, n), F32)
    for r, val in enumerate((i1.astype(F32), i2.astype(F32), p1, p2, w1 / den, w2 / den)):
        out = jnp.where(out_row == r, val, out)
    return out


def _oproj_kernel(o_ref, wo_ref, x_ref, g1_ref, gn_ref, sc_ref, sh_ref, rwt_ref, rb_ref,
                  xo_ref, h_ref, route_ref):
    n_e = rb_ref.shape[0]
    rw = rwt_ref[...]
    rows = o_ref.shape[0] // OPROJ_SPLIT
    groups = [slice(r * rows, (r + 1) * rows) for r in range(OPROJ_SPLIT)]
    ys = [jnp.dot(o_ref[g, :], wo_ref[...], preferred_element_type=F32) for g in groups]
    logits = []
    for g, y in zip(groups, ys):
        xn = x_ref[g, :] + g1_ref[...] * y
        xo_ref[g, :] = xn
        h = _norm_mod(xn, gn_ref[...], sc_ref[...], sh_ref[...])
        h_hi = h.astype(BF16)
        h_ref[g, :] = h_hi
        h_lo = (h - h_hi.astype(F32)).astype(BF16)
        main = lax.dot_general(rw, h_hi, _NT, preferred_element_type=F32)
        corr = lax.dot_general(rw[:n_e], h_lo, _NT, preferred_element_type=F32)
        logits.append(main[:n_e] + main[n_e:] + corr)
    route_ref[...] = _route(jnp.concatenate(logits, axis=1), rb_ref[...])


def _out_proj(o, wo_bf16, x2, g1, gn, sc, sh, rwt, rb, seq):
    t, d = x2.shape
    tm = ROW_TILE
    per_b = seq // tm
    n_e = rb.shape[0]
    vec = lambda i: (i // per_b, 0, 0)
    return pl.pallas_call(
        _oproj_kernel,
        out_shape=(jax.ShapeDtypeStruct((t, d), F32),
                   jax.ShapeDtypeStruct((t, d), BF16),
                   jax.ShapeDtypeStruct((8, t), F32)),
        grid=(t // tm,),
        in_specs=[
            pl.BlockSpec((tm, d), lambda i: (i, 0)),
            pl.BlockSpec((d, d), lambda i: (0, 0)),
            pl.BlockSpec((tm, d), lambda i: (i, 0)),
            pl.BlockSpec((None, 1, d), vec),
            pl.BlockSpec((1, d), lambda i: (0, 0)),
            pl.BlockSpec((None, 1, d), vec),
            pl.BlockSpec((None, 1, d), vec),
            pl.BlockSpec((2 * n_e, d), lambda i: (0, 0)),
            pl.BlockSpec((n_e, 1), lambda i: (0, 0)),
        ],
        out_specs=(pl.BlockSpec((tm, d), lambda i: (i, 0)),
                   pl.BlockSpec((tm, d), lambda i: (i, 0)),
                   pl.BlockSpec((8, tm), lambda i: (0, i))),
        compiler_params=_cparams(("parallel",)),
        name="out_proj_route",
    )(o, wo_bf16, x2, g1, gn, sc, sh, rwt, rb)


def _moe_plan(route, t, tm, n_e):
    n_tiles = t // tm
    ch, ft = MOE_CHUNK, MOE_FFN_TILE
    loc_rows, main_rows, n_sorted = _moe_rows(t, tm, n_e)
    ids = jnp.arange(n_e, dtype=jnp.int32)
    e = route[0:2].astype(jnp.int32)
    pos = route[2:4].astype(jnp.int32)
    oh = (e[:, :, None] == ids).astype(jnp.int32)
    cnt = oh.sum(0).reshape(n_tiles, tm, n_e).sum(1)
    seg = (cnt + ch - 1) // ch * ch
    loc = jnp.cumsum(seg, axis=1) - seg
    tot = seg.sum(0)
    totp = (tot + ft - 1) // ft * ft
    base = jnp.cumsum(totp) - totp
    gdest = base[None, :] + jnp.cumsum(seg, axis=0) - seg
    col = (oh * jnp.repeat(loc, tm, axis=0)[None]).sum(-1) + pos

    row0 = jnp.arange(loc_rows // ch, dtype=jnp.int32) * ch
    ej = (row0[None, :, None] >= (loc + seg)[:, None, :]).sum(-1)
    ohj = (jnp.minimum(ej, n_e - 1)[..., None] == ids).astype(jnp.int32)
    dst = (ohj * (gdest - loc)[:, None, :]).sum(-1) + row0[None, :]
    dump = main_rows + (jnp.arange(n_tiles, dtype=jnp.int32) % 2)[:, None] * loc_rows + row0[None, :]
    dst = jnp.where(ej < n_e, dst, dump) // ch

    r0 = jnp.arange(n_sorted // ft, dtype=jnp.int32) * ft
    ends = base + totp
    te = jnp.minimum((r0[:, None] >= ends[None, :]).sum(-1), n_e - 1)
    tvalid = (r0 < ends[-1]).astype(jnp.int32)
    tfirst = tvalid * (r0 == (((te[:, None] == ids) * base[None, :]).sum(-1))).astype(jnp.int32)
    return col, dst.astype(jnp.int32), te.astype(jnp.int32), tvalid, tfirst


def _moe_rows(t, tm, n_e):
    ch, ft = MOE_CHUNK, MOE_FFN_TILE
    loc_rows = 2 * tm + n_e * ch
    main = 2 * t + (t // tm) * n_e * (ch - 1) + n_e * (ft - 1)
    main = (main + ft - 1) // ft * ft
    return loc_rows, main, (main + 2 * loc_rows + ft - 1) // ft * ft


def _chunk_copies(dst_ref, tile, local_ref, sorted_ref, sem, to_sorted):
    ch = MOE_CHUNK
    copies = []
    for j in range(local_ref.shape[0] // ch):
        far = sorted_ref.at[pl.ds(pl.multiple_of(dst_ref[tile, j] * ch, ch), ch)]
        near = local_ref.at[pl.ds(j * ch, ch)]
        copies.append(pltpu.make_async_copy(near, far, sem) if to_sorted
                      else pltpu.make_async_copy(far, near, sem))
    return copies


def _dispatch_kernel(dst_ref, col_ref, h_ref, init_ref, xs_ref, comp, sem):
    del init_ref
    tile = pl.program_id(0)
    last = pl.num_programs(0) - 1
    slot = tile % 2

    def wait_slot(s):
        pltpu.make_async_copy(comp.at[s], xs_ref.at[pl.ds(0, comp.shape[1])], sem.at[s]).wait()

    @pl.when(tile >= 2)
    def _():
        wait_slot(slot)

    col = col_ref[...]
    srow = lax.broadcasted_iota(jnp.int32, (comp.shape[1], col.shape[1]), 0)
    sel = jnp.where((srow == col[0:1]) | (srow == col[1:2]), 1.0, 0.0).astype(BF16)
    comp[slot] = jnp.dot(sel, h_ref[...], preferred_element_type=F32).astype(BF16)
    for cp in _chunk_copies(dst_ref, tile, comp.at[slot], xs_ref, sem.at[slot], True):
        cp.start()

    @pl.when(tile == last)
    def _():
        wait_slot(slot)

    @pl.when((tile == last) & (tile >= 1))
    def _():
        wait_slot(1 - slot)


def _dispatch(dst, col, h, n_e, init):
    t, d = h.shape
    tm = MOE_TILE
    loc_rows, _, n_sorted = _moe_rows(t, tm, n_e)
    if init is None:
        init = jnp.zeros((n_sorted, d), BF16)
    return pl.pallas_call(
        _dispatch_kernel,
        out_shape=jax.ShapeDtypeStruct((n_sorted, d), BF16),
        grid_spec=pltpu.PrefetchScalarGridSpec(
            num_scalar_prefetch=1,
            grid=(t // tm,),
            in_specs=[pl.BlockSpec((2, tm), lambda i, dst: (0, i)),
                      pl.BlockSpec((tm, d), lambda i, dst: (i, 0)),
                      pl.BlockSpec(memory_space=pl.ANY)],
            out_specs=pl.BlockSpec(memory_space=pl.ANY),
            scratch_shapes=[pltpu.VMEM((2, loc_rows, d), BF16), pltpu.SemaphoreType.DMA((2,))],
        ),
        input_output_aliases={3: 0},
        compiler_params=_cparams(("arbitrary",)),
        name="moe_dispatch",
    )(dst, col, h, init)


def _expert_kernel(te_ref, tv_ref, tf_ref, xs_ref, wg_ref, wu_ref, wd_ref, ys_ref, wg_b, wu_b, wd_b):
    del te_ref
    s = pl.program_id(0)

    @pl.when(tf_ref[s] == 1)
    def _():
        wg_b[...] = wg_ref[...].astype(BF16)
        wu_b[...] = wu_ref[...].astype(BF16)
        wd_b[...] = wd_ref[...].astype(BF16)

    @pl.when(tv_ref[s] == 1)
    def _():
        rows = ys_ref.shape[0] // MOE_FFN_SPLIT
        hs = []
        for r in range(MOE_FFN_SPLIT):
            xs = xs_ref[r * rows:(r + 1) * rows, :]
            hg = jnp.dot(xs, wg_b[...], preferred_element_type=F32)
            hu = jnp.dot(xs, wu_b[...], preferred_element_type=F32)
            hs.append((hg, hu))
        for r, (hg, hu) in enumerate(hs):
            he = hg * (1.0 / (1.0 + jnp.exp(-hg))) * hu
            ys_ref[r * rows:(r + 1) * rows, :] = jnp.dot(
                he.astype(BF16), wd_b[...], preferred_element_type=F32).astype(BF16)

    @pl.when(tv_ref[s] == 0)
    def _():
        ys_ref[...] = jnp.zeros(ys_ref.shape, ys_ref.dtype)


def _experts(te, tvalid, tfirst, xs, wg, wu, wd, layer):
    n_sorted, d = xs.shape
    f = wg.shape[3]
    ft = MOE_FFN_TILE
    wspec = lambda r, c: pl.BlockSpec((None, None, r, c), lambda s, te, tv, tf: (layer, te[s], 0, 0))
    return pl.pallas_call(
        _expert_kernel,
        out_shape=jax.ShapeDtypeStruct((n_sorted, d), BF16),
        grid_spec=pltpu.PrefetchScalarGridSpec(
            num_scalar_prefetch=3,
            grid=(n_sorted // ft,),
            in_specs=[pl.BlockSpec((ft, d), lambda s, te, tv, tf: (s, 0)),
                      wspec(d, f), wspec(d, f), wspec(f, d)],
            out_specs=pl.BlockSpec((ft, d), lambda s, te, tv, tf: (s, 0)),
            scratch_shapes=[pltpu.VMEM((d, f), BF16), pltpu.VMEM((d, f), BF16), pltpu.VMEM((f, d), BF16)],
        ),
        compiler_params=_cparams(("arbitrary",)),
        name="moe_experts",
    )(te, tvalid, tfirst, xs, wg, wu, wd)


def _combine_kernel(dst_ref, col_ref, w_ref, ys_ref, x_ref, g2_ref, o_ref, comp, sem):
    tile = pl.program_id(0)
    slot = tile % 2

    def fetch(t, s):
        for cp in _chunk_copies(dst_ref, t, comp.at[s], ys_ref, sem.at[s], False):
            cp.start()

    def wait_slot(s):
        pltpu.make_async_copy(ys_ref.at[pl.ds(0, comp.shape[1])], comp.at[s], sem.at[s]).wait()

    last = pl.num_programs(0) - 1

    @pl.when(tile == 0)
    def _():
        fetch(tile, slot)

    fetch(jnp.minimum(tile + 1, last), 1 - slot)

    col = col_ref[...]
    w = w_ref[...]
    scol = lax.broadcasted_iota(jnp.int32, (col.shape[0], comp.shape[1]), 1)
    selw = (jnp.where(scol == col[:, 0:1], w[:, 4:5], 0.0)
            + jnp.where(scol == col[:, 1:2], w[:, 5:6], 0.0)).astype(BF16)
    wait_slot(slot)
    acc = jnp.dot(selw, comp[slot], preferred_element_type=F32)
    o_ref[...] = x_ref[...] + g2_ref[...] * acc

    @pl.when(tile == last)
    def _():
        wait_slot(1 - slot)


def _combine(dst, col_t, route_t, ys, x2, g2, seq, n_e):
    t, d = x2.shape
    tm = MOE_TILE
    per_b = seq // tm
    loc_rows = _moe_rows(t, tm, n_e)[0]
    return pl.pallas_call(
        _combine_kernel,
        out_shape=jax.ShapeDtypeStruct((t, d), F32),
        grid_spec=pltpu.PrefetchScalarGridSpec(
            num_scalar_prefetch=1,
            grid=(t // tm,),
            in_specs=[pl.BlockSpec((tm, 2), lambda i, dst: (i, 0)),
                      pl.BlockSpec((tm, 8), lambda i, dst: (i, 0)),
                      pl.BlockSpec(memory_space=pl.ANY),
                      pl.BlockSpec((tm, d), lambda i, dst: (i, 0)),
                      pl.BlockSpec((None, 1, d), lambda i, dst: (i // per_b, 0, 0))],
            out_specs=pl.BlockSpec((tm, d), lambda i, dst: (i, 0)),
            scratch_shapes=[pltpu.VMEM((2, loc_rows, d), BF16), pltpu.SemaphoreType.DMA((2,))],
        ),
        compiler_params=_cparams(("arbitrary",)),
        name="moe_combine",
    )(dst, col_t, route_t, ys, x2, g2)


def _moe(h, route, wg, wu, wd, layer, x2, g2, seq, sorted_init):
    t = x2.shape[0]
    n_e = wg.shape[1]
    col, dst, te, tvalid, tfirst = _moe_plan(route, t, MOE_TILE, n_e)
    xs = _dispatch(dst, col, h, n_e, sorted_init)
    ys = _experts(te, tvalid, tfirst, xs, wg, wu, wd, layer)
    return _combine(dst, col.T, route.T, ys, x2, g2, seq, n_e), ys


def kernel(x, c, ada_w, ada_b, norm_mix_g, norm_ffn_g, t5_bias, a_w_qkv, a_q_gain, a_k_gain, a_lambda, a_subln_g, a_w_o, b_w_qkv, b_q_gain, b_k_gain, b_rel_bias, b_w_o, router_w, router_bias, moe_w_gate, moe_w_up, moe_w_down):
    batch, seq, d = x.shape
    depth = ada_w.shape[0]
    assert seq % ATTN_TILE == 0 and seq % ROW_TILE == 0 and d == A_HEADS * 2 * A_HEAD_DIM
    assert d == B_HEADS * B_HEAD_DIM and A_HEAD_DIM == B_HEAD_DIM

    c_pad = jnp.zeros((8, d), F32).at[:batch].set(c.astype(F32))
    mod = _modulation(c_pad, ada_w.astype(F32), ada_b.astype(F32))[:, :batch]
    mod = mod.reshape(depth, batch, 6, 1, d)

    rwt = router_w.astype(F32).T
    rwt_hi = lax.bitcast_convert_type(
        lax.bitcast_convert_type(rwt, jnp.uint32) & jnp.uint32(0xFFFF0000), F32)
    rwt = jnp.concatenate([rwt_hi, rwt - rwt_hi], axis=0).astype(BF16)
    rb = router_bias.astype(F32).reshape(-1, 1)

    x2 = x.astype(F32).reshape(batch * seq, d)
    sorted_buf = None
    for i in range(depth):
        sh1, sc1, g1, sh2, sc2, g2 = [mod[i, :, k] for k in range(6)]
        j = i // 2
        if i % 2 == 0:
            w_qkv, qg, kg, w_o = a_w_qkv[j], a_q_gain[j], a_k_gain[j], a_w_o[j]
        else:
            w_qkv, qg, kg, w_o = b_w_qkv[j], b_q_gain[j], b_k_gain[j], b_w_o[j]
        n_rep = d // qg.shape[0]
        gq = (jnp.tile(qg.astype(F32), n_rep) * (A_HEAD_DIM ** -0.5 * LOG2E)).reshape(d, 1)
        gk = jnp.tile(kg.astype(F32), n_rep).reshape(1, d)
        qt, k, vt = _qkv_proj(x2, norm_mix_g[i].reshape(1, d), sc1, sh1, w_qkv.astype(BF16),
                              gq, gk, seq, A_HEAD_DIM)
        if i % 2 == 0:
            lambda_init = 0.8 - 0.6 * math.exp(-0.3 * i)
            o = _attention_a(qt, k, vt, _t5_vectors(t5_bias, ATTN_TILE), a_lambda[j].astype(F32),
                             a_subln_g[j].reshape(1, -1), batch, seq, lambda_init)
        else:
            o = _attention_b(qt, k, vt, _band_vectors(b_rel_bias[j], BAND_TILE), batch, seq)
        x2, h, route = _out_proj(o, w_o.astype(BF16), x2, g1, norm_ffn_g[i].reshape(1, d),
                                 sc2, sh2, rwt, rb, seq)
        x2, sorted_buf = _moe(h, route, moe_w_gate, moe_w_up, moe_w_down, i, x2, g2, seq, sorted_buf)
    return x2.reshape(batch, seq, d)
```

```python
import functools
import math

import numpy as np
import jax
import jax.numpy as jnp
from jax import lax
from jax.experimental import pallas as pl
from jax.experimental.pallas import tpu as pltpu

F32 = jnp.float32
BF16 = jnp.bfloat16

CHUNK = 64
A_HEADS = 8
A_HEAD_DIM = 64
T5_BUCKETS = 32
T5_MAX_DIST = 1024
B_HEADS = 16
B_HEAD_DIM = 64
LEFT_CHUNKS = 8
MAX_REL = 256
N_EXPERTS = 16
N_GROUPS = 4
E_PER_GROUP = N_EXPERTS // N_GROUPS
NORM_EPS = 1e-6
NEG_INF = -1e30
LOG2E = math.log2(math.e)

V7X_LANES = 128
V7X_MXU_DIM = 256

ATTN_TILE = 512
MOD_COL_TILE = 1536
FAR_UNROLL = 4
NEAR_AHEAD = 2
BAND_TILE = 256
BAND_QBLOCKS = 16
ROW_TILE = 512
MOE_TILE = 512
QKV_SPLIT = 2
OPROJ_ROW_TILE = 1024
OPROJ_SPLIT = 8
MOE_CHUNK = 16
MOE_FFN_TILE = 512
MOE_FFN_SPLIT = 2
VMEM_LIMIT = 56 * 1024 * 1024

_NT = (((1,), (1,)), ((), ()))


def _cparams(sem):
    return pltpu.CompilerParams(dimension_semantics=sem, vmem_limit_bytes=VMEM_LIMIT)


def _mod_kernel(c_ref, w_ref, b_ref, o_ref):
    c = c_ref[...]
    s = c * (1.0 / (1.0 + jnp.exp(-c)))
    o_ref[...] = jnp.dot(s, w_ref[...], preferred_element_type=F32,
                         precision=lax.Precision.HIGHEST) + b_ref[...]


def _modulation(c_pad, ada_w, ada_b):
    depth, d, n = ada_w.shape
    rows = c_pad.shape[0]
    tn = MOD_COL_TILE
    return pl.pallas_call(
        _mod_kernel,
        out_shape=jax.ShapeDtypeStruct((depth, rows, n), F32),
        grid=(depth, n // tn),
        in_specs=[
            pl.BlockSpec((rows, d), lambda i, j: (0, 0)),
            pl.BlockSpec((None, d, tn), lambda i, j: (i, 0, j)),
            pl.BlockSpec((None, 1, tn), lambda i, j: (i, 0, j)),
        ],
        out_specs=pl.BlockSpec((None, rows, tn), lambda i, j: (i, 0, j)),
        compiler_params=_cparams(("parallel", "parallel")),
        name="adaln_mod",
    )(c_pad, ada_w, ada_b.reshape(depth, 1, n))


def _norm_mod(x, g, sc, sh):
    ms = jnp.mean(x * x, axis=-1, keepdims=True)
    return x * lax.rsqrt(ms + NORM_EPS) * g * (1.0 + sc) + sh


def _qkv_kernel(x_ref, g_ref, sc_ref, sh_ref, w_ref, gq_ref, gk_ref,
                qt_ref, k_ref, vt_ref, wqt_ref, wvt_ref, *, head_dim):
    d, tm = qt_ref.shape

    @pl.when(pl.program_id(0) == 0)
    def _():
        wqt_ref[...] = w_ref[:, :d].T
        wvt_ref[...] = w_ref[:, 2 * d:].T

    cw = V7X_MXU_DIM
    r = lax.broadcasted_iota(jnp.int32, (cw, cw), 0) // head_dim
    c = lax.broadcasted_iota(jnp.int32, (cw, cw), 1) // head_dim
    gmat = jnp.where(r == c, 1.0 / head_dim, 0.0).astype(BF16)
    tg = tm // QKV_SPLIT
    for gi in range(QKV_SPLIT):
        rows = slice(gi * tg, (gi + 1) * tg)
        h = _norm_mod(x_ref[rows, :], g_ref[...], sc_ref[...], sh_ref[...]).astype(BF16)
        y = lax.dot_general(wqt_ref[...], h, _NT, preferred_element_type=F32)
        y3 = y.reshape(d // head_dim, head_dim, tg)
        ss = jnp.mean(y3 * y3, axis=1, keepdims=True)
        qt_ref[:, rows] = ((y3 * lax.rsqrt(ss + NORM_EPS)).reshape(d, tg) * gq_ref[...]).astype(BF16)
        y = jnp.dot(h, w_ref[:, d:2 * d], preferred_element_type=F32)
        ysq = (y * y).astype(BF16)
        for ci in range(d // cw):
            cols = slice(ci * cw, (ci + 1) * cw)
            ss = jnp.dot(ysq[:, cols], gmat, preferred_element_type=F32)
            k_ref[rows, cols] = (y[:, cols] * lax.rsqrt(ss + NORM_EPS) * gk_ref[:, cols]).astype(BF16)
        vt_ref[:, rows] = lax.dot_general(wvt_ref[...], h, _NT, preferred_element_type=F32).astype(BF16)


def _qkv_proj(x2, g, sc, sh, w, gq, gk, seq, head_dim):
    t, d = x2.shape
    tm = ROW_TILE * QKV_SPLIT
    per_b = seq // tm
    vec = lambda i: (i // per_b, 0, 0)
    full = lambda i: (0, 0)
    return pl.pallas_call(
        functools.partial(_qkv_kernel, head_dim=head_dim),
        out_shape=(jax.ShapeDtypeStruct((d, t), BF16),
                   jax.ShapeDtypeStruct((t, d), BF16),
                   jax.ShapeDtypeStruct((d, t), BF16)),
        grid=(t // tm,),
        in_specs=[
            pl.BlockSpec((tm, d), lambda i: (i, 0)),
            pl.BlockSpec((1, d), full),
            pl.BlockSpec((None, 1, d), vec),
            pl.BlockSpec((None, 1, d), vec),
            pl.BlockSpec((d, 3 * d), full),
            pl.BlockSpec((d, 1), full),
            pl.BlockSpec((1, d), full),
        ],
        out_specs=(pl.BlockSpec((d, tm), lambda i: (0, i)),
                   pl.BlockSpec((tm, d), lambda i: (i, 0)),
                   pl.BlockSpec((d, tm), lambda i: (0, i))),
        scratch_shapes=[pltpu.VMEM((d, d), BF16), pltpu.VMEM((d, d), BF16)],
        compiler_params=_cparams(("arbitrary",)),
        name="qkv_proj",
    )(x2, g, sc, sh, w, gq, gk)


def _softmax_pv(s, vt1, m_ref, acc_ref):
    m_prev = m_ref[...]
    m_new = jnp.maximum(m_prev, jnp.max(s, axis=0, keepdims=True))
    alpha = jnp.exp2(m_prev - m_new)
    p = jnp.exp2(s - m_new).astype(BF16)
    acc_ref[...] = alpha * acc_ref[...] + jnp.dot(vt1, p, preferred_element_type=F32)
    m_ref[...] = m_new


def _split_maps(qt, head_dim):
    row = lax.broadcasted_iota(jnp.int32, qt.shape, 0)
    zero = jnp.zeros_like(qt)
    return jnp.where(row < head_dim, qt, zero), jnp.where(row >= head_dim, qt, zero)


ONES_ROWS = 16


def _init_state(refs):
    for m_ref, acc_ref in refs:
        m_ref[...] = jnp.full(m_ref.shape, NEG_INF, F32)
        acc_ref[...] = jnp.zeros(acc_ref.shape, F32)


def _normalized(acc_ref):
    acc = acc_ref[...]
    return acc[:V7X_LANES] / acc[V7X_LANES:V7X_LANES + 1]


def _toeplitz(x_row, n):
    x = jnp.broadcast_to(x_row, (n, x_row.shape[1]))
    return pltpu.roll(x, 0, 1, stride=1, stride_axis=0)[:, :n]


def _chunk_delta(n):
    kc = lax.broadcasted_iota(jnp.int32, (n, n), 0) // CHUNK
    qc = lax.broadcasted_iota(jnp.int32, (n, n), 1) // CHUNK
    return qc - kc


def _attn_a_kernel(qt_ref, k_ref, vt_ref, x_ref, lam_ref, sub_ref, o_ref,
                   tab, fbuf, nbuf, m_st, acc_st, *, n_off, lambda_init):
    tq = ATTN_TILE
    step = pl.program_id(2)

    @pl.when(step == 0)
    def _():
        for o in range(n_off):
            t = _toeplitz(x_ref[o], tq) * LOG2E
            if o == 0:
                t = jnp.where(_chunk_delta(tq) >= 0, t, NEG_INF)
            tab[o] = t

    for c in range(2):
        _init_state(((m_st.at[c, 0], acc_st.at[c, 0]), (m_st.at[c, 1], acc_st.at[c, 1])))
    q_maps = [_split_maps(qt_ref[:, c * tq:(c + 1) * tq], A_HEAD_DIM) for c in range(2)]

    def k_tile(j):
        return k_ref[pl.ds(pl.multiple_of(j * tq, tq), tq), :]

    def vt1_tile(j):
        vt = vt_ref[:, pl.ds(pl.multiple_of(j * tq, tq), tq)]
        return jnp.concatenate([vt, jnp.ones((ONES_ROWS, tq), vt.dtype)], axis=0)

    def scores(buf, k, c):
        for m in range(2):
            buf[c, m] = jnp.dot(k, q_maps[c][m], preferred_element_type=F32)

    def consume(buf, vt1, c, o):
        for m in range(2):
            s = buf[c, m]
            _softmax_pv(s if o >= n_off else s + tab[o], vt1, m_st.at[c, m], acc_st.at[c, m])

    def near_block(first_tile, n_tiles, first_scored):
        units = [[c for c in range(2) if n_tiles - 2 + c - s >= 0] for s in range(n_tiles)]
        buf_of = lambda s: fbuf.at[0] if s == 0 else nbuf.at[s - 1]

        def tile_scores(s):
            k = k_tile(first_tile + s)
            for c in units[s]:
                scores(buf_of(s), k, c)

        if not first_scored:
            tile_scores(0)
        for s in range(1, min(NEAR_AHEAD, n_tiles)):
            tile_scores(s)
        for s in range(n_tiles):
            if s + NEAR_AHEAD < n_tiles:
                tile_scores(s + NEAR_AHEAD)
            vt1 = vt1_tile(first_tile + s)
            for c in units[s]:
                consume(buf_of(s), vt1, c, n_tiles - 2 + c - s)

    first_general = (n_off - 1) // 2
    for i0 in range(first_general):
        @pl.when(step == i0)
        def _():
            near_block(0, 2 * i0 + 2, False)

    def far_step(j, parity):
        k = k_tile(j + 1)
        for c in range(2):
            scores(fbuf.at[1 - parity], k, c)
        vt1 = vt1_tile(j)
        for c in range(2):
            consume(fbuf.at[parity], vt1, c, n_off)

    @pl.when(step >= first_general)
    def _():
        k = k_tile(0)
        for c in range(2):
            scores(fbuf.at[0], k, c)

    n_far = jnp.maximum(2 * step - (n_off - 1), 0)
    rem = n_far % FAR_UNROLL

    @pl.when(rem >= 2)
    def _():
        far_step(0, 0)
        far_step(1, 1)

    def far_trip(t, carry):
        for u in range(FAR_UNROLL):
            far_step(FAR_UNROLL * t + rem + u, u % 2)
        return carry
    lax.fori_loop(0, n_far // FAR_UNROLL, far_trip, 0)

    @pl.when(step >= first_general)
    def _():
        near_block(n_far, n_off + 1, True)

    lam = lam_ref[...]
    lam_full = (jnp.exp(jnp.sum(lam[0:1] * lam[1:2], axis=-1, keepdims=True))
                - jnp.exp(jnp.sum(lam[2:3] * lam[3:4], axis=-1, keepdims=True)) + lambda_init)
    for c in range(2):
        a = _normalized(acc_st.at[c, 0]) - lam_full * _normalized(acc_st.at[c, 1])
        ms = jnp.mean(a * a, axis=0, keepdims=True)
        an = a * lax.rsqrt(ms + NORM_EPS)
        o_ref[c * tq:(c + 1) * tq, :] = (an.T * (sub_ref[...] * (1.0 - lambda_init))).astype(BF16)


def _attn_b_kernel(qt_ref, k_ref, vt_ref, x_ref, o_ref, tab, sbuf):
    tq = BAND_TILE
    n_blk = qt_ref.shape[1] // tq
    n_off = tab.shape[0] // 2
    step = pl.program_id(2)

    @pl.when(step == 0)
    def _():
        for m in range(2):
            for o in range(n_off):
                d = _chunk_delta(tq) + o * (tq // CHUNK)
                t = _toeplitz(x_ref[n_off * m + o], tq) * LOG2E
                tab[n_off * m + o] = jnp.where((d >= 0) & (d <= LEFT_CHUNKS), t, NEG_INF)

    q_maps = [_split_maps(qt_ref[:, c * tq:(c + 1) * tq], B_HEAD_DIM) for c in range(n_blk)]

    def run(first_step):
        pairs = [(c, o) for c in range(n_blk) for o in range(n_off - 1, -1, -1)
                 if not first_step or c - o >= 0]
        k_tiles, vt_tiles = {}, {}
        for c, o in pairs:
            if c - o not in k_tiles:
                ks = pl.multiple_of((n_blk * step + c - o) * tq, tq)
                k_tiles[c - o] = k_ref[pl.ds(ks, tq), :]
                vt = vt_ref[:, pl.ds(ks, tq)]
                vt_tiles[c - o] = jnp.concatenate([vt, jnp.ones((ONES_ROWS, tq), vt.dtype)], axis=0)
        def tile_scores(c):
            for o in [o for cc, o in pairs if cc == c]:
                for m in range(2):
                    sbuf[n_off * c + o, m] = jnp.dot(k_tiles[c - o], q_maps[c][m], preferred_element_type=F32)

        for c in range(min(NEAR_AHEAD, n_blk)):
            tile_scores(c)
        for c in range(n_blk):
            if c + NEAR_AHEAD < n_blk:
                tile_scores(c + NEAR_AHEAD)
            outs = []
            for m in range(2):
                offs = [o for cc, o in pairs if cc == c]
                ss = [sbuf[n_off * c + o, m] + tab[n_off * m + o] for o in offs]
                mx = functools.reduce(jnp.maximum, [jnp.max(s, axis=0, keepdims=True) for s in ss])
                acc = sum(jnp.dot(vt_tiles[c - o], jnp.exp2(s - mx).astype(BF16), preferred_element_type=F32)
                          for o, s in zip(offs, ss))
                outs.append(acc[:V7X_LANES] / acc[V7X_LANES:V7X_LANES + 1])
            row = lax.broadcasted_iota(jnp.int32, outs[0].shape, 0)
            o_ref[c * tq:(c + 1) * tq, :] = jnp.where(row < B_HEAD_DIM, outs[0], outs[1]).T.astype(BF16)

    @pl.when(step == 0)
    def _():
        run(True)

    @pl.when(step >= 1)
    def _():
        run(False)


def _attn_specs(batch, seq, nq, tq):
    return dict(
        q=pl.BlockSpec((V7X_LANES, tq), lambda b, h, i: (h, b * nq + i)),
        k=pl.BlockSpec((seq, V7X_LANES), lambda b, h, i: (b, h)),
        v=pl.BlockSpec((V7X_LANES, seq), lambda b, h, i: (h, b)),
        o=pl.BlockSpec((tq, V7X_LANES), lambda b, h, i: (b * nq + i, h)),
    )


def _attention_a(qt, k, vt, xvec, lam, sub_g, batch, seq, lambda_init):
    d, t = qt.shape
    tq = ATTN_TILE
    n_off = xvec.shape[1]
    assert n_off % 2 == 1 and seq % (2 * tq) == 0
    nq = seq // (2 * tq)
    sp = _attn_specs(batch, seq, nq, 2 * tq)
    rows = V7X_LANES + ONES_ROWS
    return pl.pallas_call(
        functools.partial(_attn_a_kernel, n_off=n_off, lambda_init=lambda_init),
        out_shape=jax.ShapeDtypeStruct((t, d), BF16),
        grid=(batch, d // V7X_LANES, nq),
        in_specs=[sp["q"], sp["k"], sp["v"],
                  pl.BlockSpec((None,) + xvec.shape[1:], lambda b, h, i: (h, 0, 0, 0)),
                  pl.BlockSpec(lam.shape, lambda b, h, i: (0, 0)),
                  pl.BlockSpec(sub_g.shape, lambda b, h, i: (0, 0))],
        out_specs=sp["o"],
        scratch_shapes=[pltpu.VMEM((n_off, tq, tq), F32), pltpu.VMEM((2, 2, 2, tq, tq), F32),
                        pltpu.VMEM((n_off, 2, 2, tq, tq), F32),
                        pltpu.VMEM((2, 2, 1, tq), F32), pltpu.VMEM((2, 2, rows, tq), F32)],
        compiler_params=_cparams(("parallel", "parallel", "arbitrary")),
        name="diff_attention",
    )(qt, k, vt, xvec, lam, sub_g)


def _attention_b(qt, k, vt, xvec, batch, seq):
    d, t = qt.shape
    tq = BAND_TILE
    g = BAND_QBLOCKS
    n_off = xvec.shape[1] // 2
    assert g >= n_off - 1 and seq % (g * tq) == 0
    nq = seq // (g * tq)
    sp = _attn_specs(batch, seq, nq, g * tq)
    return pl.pallas_call(
        _attn_b_kernel,
        out_shape=jax.ShapeDtypeStruct((t, d), BF16),
        grid=(batch, d // V7X_LANES, nq),
        in_specs=[sp["q"], sp["k"], sp["v"],
                  pl.BlockSpec((None,) + xvec.shape[1:], lambda b, h, i: (h, 0, 0, 0))],
        out_specs=sp["o"],
        scratch_shapes=[pltpu.VMEM((2 * n_off, tq, tq), F32), pltpu.VMEM((n_off * g, 2, tq, tq), F32)],
        compiler_params=_cparams(("parallel", "parallel", "arbitrary")),
        name="chunk_attention",
    )(qt, k, vt, xvec)


def _t5_bucket(rel):
    nb = T5_BUCKETS // 2
    ret = jnp.where(rel > 0, nb, 0)
    n = jnp.abs(rel)
    max_exact = nb // 2
    nf = jnp.maximum(n, 1).astype(F32)
    large = max_exact + (jnp.log(nf / max_exact) / math.log(T5_MAX_DIST / max_exact)
                         * (nb - max_exact)).astype(jnp.int32)
    large = jnp.minimum(large, nb - 1)
    return ret + jnp.where(n < max_exact, n, large)


def _t5_const_distance():
    nb = T5_BUCKETS // 2
    max_exact = nb // 2
    n = np.arange(max_exact, 4 * T5_MAX_DIST, dtype=np.float64)
    large = max_exact + np.floor(np.log(n / max_exact) / math.log(T5_MAX_DIST / max_exact) * (nb - max_exact))
    below = np.nonzero(large < nb - 1)[0]
    return int(n[below[-1]]) + 2


def _tile_rel(tile, n_off):
    i = jnp.arange(2 * tile, dtype=jnp.int32)
    rel = jnp.where(i < tile, -i, 2 * tile - i)
    return rel[None, :] - tile * jnp.arange(n_off, dtype=jnp.int32)[:, None]


def _t5_vectors(t5_bias, tile):
    n_off = 1
    while (n_off - 1) * tile + 1 < _t5_const_distance():
        n_off += 1
    tb = t5_bias.astype(F32)
    vals = tb[_t5_bucket(_tile_rel(tile, n_off))] - tb[T5_BUCKETS // 2 - 1]
    return vals.transpose(2, 0, 1)[:, :, None, :]


def _band_vectors(rel_bias, tile):
    n_off = LEFT_CHUNKS * CHUNK // tile + 1
    idx = jnp.clip(_tile_rel(tile, n_off), -MAX_REL, MAX_REL) + MAX_REL
    vals = rel_bias.astype(F32)[:, idx]
    return vals.reshape(rel_bias.shape[0] // 2, 2 * n_off, 1, 2 * tile)


def _top2_sum4(r0, r1, r2, r3):
    a, b = jnp.maximum(r0, r1), jnp.minimum(r0, r1)
    c, d = jnp.maximum(r2, r3), jnp.minimum(r2, r3)
    return jnp.maximum(a, c) + jnp.maximum(jnp.minimum(a, c), jnp.maximum(b, d))


def _route(logits, rbias):
    n_e, n = logits.shape
    scores = 1.0 / (1.0 + jnp.exp(-logits))
    sel = scores + rbias
    row = lax.broadcasted_iota(jnp.int32, sel.shape, 0)
    best = None
    for g in range(N_GROUPS):
        rows = [sel[g * E_PER_GROUP + i: g * E_PER_GROUP + i + 1, :] for i in range(E_PER_GROUP)]
        gs = _top2_sum4(*rows)
        if best is None:
            best, gidx = gs, jnp.zeros(gs.shape, jnp.int32)
        else:
            gidx = jnp.where(gs > best, g, gidx)
            best = jnp.maximum(best, gs)
    masked = jnp.where(row // E_PER_GROUP == gidx, sel, NEG_INF)
    m1 = jnp.max(masked, axis=0, keepdims=True)
    i1 = jnp.min(jnp.where(masked == m1, row, n_e), axis=0, keepdims=True)
    masked2 = jnp.where(row == i1, -3.0e38, masked)
    m2 = jnp.max(masked2, axis=0, keepdims=True)
    i2 = jnp.min(jnp.where(masked2 == m2, row, n_e), axis=0, keepdims=True)
    w1 = jnp.sum(jnp.where(row == i1, scores, 0.0), axis=0, keepdims=True)
    w2 = jnp.sum(jnp.where(row == i2, scores, 0.0), axis=0, keepdims=True)
    den = w1 + w2
    member = jnp.where((row == i1) | (row == i2), 1.0, 0.0).astype(BF16)
    t_from = lax.broadcasted_iota(jnp.int32, (n, n), 0)
    t_to = lax.broadcasted_iota(jnp.int32, (n, n), 1)
    before = jnp.where((t_from < t_to) & (t_from // MOE_TILE == t_to // MOE_TILE), 1.0, 0.0).astype(BF16)
    pos = jnp.dot(member, before, preferred_element_type=F32)
    p1 = jnp.sum(jnp.where(row == i1, pos, 0.0), axis=0, keepdims=True)
    p2 = jnp.sum(jnp.where(row == i2, pos, 0.0), axis=0, keepdims=True)
    out_row = lax.broadcasted_iota(jnp.int32, (8, n), 0)
    out = jnp.zeros((8, n), F32)
    for r, val in enumerate((i1.astype(F32), i2.astype(F32), p1, p2, w1 / den, w2 / den)):
        out = jnp.where(out_row == r, val, out)
    return out


def _oproj_kernel(o_ref, wo_ref, x_ref, g1_ref, gn_ref, sc_ref, sh_ref, rwt_ref, rb_ref,
                  xo_ref, h_ref, route_ref):
    n_e = rb_ref.shape[0]
    rw = rwt_ref[...]
    rows = o_ref.shape[0] // OPROJ_SPLIT
    groups = [slice(r * rows, (r + 1) * rows) for r in range(OPROJ_SPLIT)]
    ys = [jnp.dot(o_ref[g, :], wo_ref[...], preferred_element_type=F32) for g in groups]
    logits = []
    for g, y in zip(groups, ys):
        xn = x_ref[g, :] + g1_ref[...] * y
        xo_ref[g, :] = xn
        h = _norm_mod(xn, gn_ref[...], sc_ref[...], sh_ref[...])
        h_hi = h.astype(BF16)
        h_ref[g, :] = h_hi
        h_lo = (h - h_hi.astype(F32)).astype(BF16)
        main = lax.dot_general(rw, h_hi, _NT, preferred_element_type=F32)
        corr = lax.dot_general(rw[:n_e], h_lo, _NT, preferred_element_type=F32)
        logits.append(main[:n_e] + main[n_e:] + corr)
    route_ref[...] = _route(jnp.concatenate(logits, axis=1), rb_ref[...])


def _out_proj(o, wo_bf16, x2, g1, gn, sc, sh, rwt, rb, seq):
    t, d = x2.shape
    tm = OPROJ_ROW_TILE
    per_b = seq // tm
    n_e = rb.shape[0]
    vec = lambda i: (i // per_b, 0, 0)
    return pl.pallas_call(
        _oproj_kernel,
        out_shape=(jax.ShapeDtypeStruct((t, d), F32),
                   jax.ShapeDtypeStruct((t, d), BF16),
                   jax.ShapeDtypeStruct((8, t), F32)),
        grid=(t // tm,),
        in_specs=[
            pl.BlockSpec((tm, d), lambda i: (i, 0)),
            pl.BlockSpec((d, d), lambda i: (0, 0)),
            pl.BlockSpec((tm, d), lambda i: (i, 0)),
            pl.BlockSpec((None, 1, d), vec),
            pl.BlockSpec((1, d), lambda i: (0, 0)),
            pl.BlockSpec((None, 1, d), vec),
            pl.BlockSpec((None, 1, d), vec),
            pl.BlockSpec((2 * n_e, d), lambda i: (0, 0)),
            pl.BlockSpec((n_e, 1), lambda i: (0, 0)),
        ],
        out_specs=(pl.BlockSpec((tm, d), lambda i: (i, 0)),
                   pl.BlockSpec((tm, d), lambda i: (i, 0)),
                   pl.BlockSpec((8, tm), lambda i: (0, i))),
        compiler_params=_cparams(("parallel",)),
        name="out_proj_route",
    )(o, wo_bf16, x2, g1, gn, sc, sh, rwt, rb)


def _moe_plan(route, t, tm, n_e):
    n_tiles = t // tm
    ch, ft = MOE_CHUNK, MOE_FFN_TILE
    loc_rows, main_rows, n_sorted = _moe_rows(t, tm, n_e)
    ids = jnp.arange(n_e, dtype=jnp.int32)
    e = route[0:2].astype(jnp.int32)
    pos = route[2:4].astype(jnp.int32)
    oh = (e[:, :, None] == ids).astype(jnp.int32)
    cnt = oh.sum(0).reshape(n_tiles, tm, n_e).sum(1)
    seg = (cnt + ch - 1) // ch * ch
    loc = jnp.cumsum(seg, axis=1) - seg
    tot = seg.sum(0)
    totp = (tot + ft - 1) // ft * ft
    base = jnp.cumsum(totp) - totp
    gdest = base[None, :] + jnp.cumsum(seg, axis=0) - seg
    col = (oh * jnp.repeat(loc, tm, axis=0)[None]).sum(-1) + pos

    row0 = jnp.arange(loc_rows // ch, dtype=jnp.int32) * ch
    ej = (row0[None, :, None] >= (loc + seg)[:, None, :]).sum(-1)
    ohj = (jnp.minimum(ej, n_e - 1)[..., None] == ids).astype(jnp.int32)
    dst = (ohj * (gdest - loc)[:, None, :]).sum(-1) + row0[None, :]
    dump = main_rows + (jnp.arange(n_tiles, dtype=jnp.int32) % 2)[:, None] * loc_rows + row0[None, :]
    dst = jnp.where(ej < n_e, dst, dump) // ch

    r0 = jnp.arange(n_sorted // ft, dtype=jnp.int32) * ft
    ends = base + totp
    te = jnp.minimum((r0[:, None] >= ends[None, :]).sum(-1), n_e - 1)
    tvalid = (r0 < ends[-1]).astype(jnp.int32)
    tfirst = tvalid * (r0 == (((te[:, None] == ids) * base[None, :]).sum(-1))).astype(jnp.int32)
    return col, dst.astype(jnp.int32), te.astype(jnp.int32), tvalid, tfirst


def _moe_rows(t, tm, n_e):
    ch, ft = MOE_CHUNK, MOE_FFN_TILE
    loc_rows = 2 * tm + n_e * ch
    main = 2 * t + (t // tm) * n_e * (ch - 1) + n_e * (ft - 1)
    main = (main + ft - 1) // ft * ft
    return loc_rows, main, (main + 2 * loc_rows + ft - 1) // ft * ft


def _chunk_copies(dst_ref, tile, local_ref, sorted_ref, sem, to_sorted):
    ch = MOE_CHUNK
    copies = []
    for j in range(local_ref.shape[0] // ch):
        far = sorted_ref.at[pl.ds(pl.multiple_of(dst_ref[tile, j] * ch, ch), ch)]
        near = local_ref.at[pl.ds(j * ch, ch)]
        copies.append(pltpu.make_async_copy(near, far, sem) if to_sorted
                      else pltpu.make_async_copy(far, near, sem))
    return copies


def _dispatch_kernel(dst_ref, col_ref, h_ref, init_ref, xs_ref, comp, sem):
    del init_ref
    tile = pl.program_id(0)
    last = pl.num_programs(0) - 1
    slot = tile % 2

    def wait_slot(s):
        pltpu.make_async_copy(comp.at[s], xs_ref.at[pl.ds(0, comp.shape[1])], sem.at[s]).wait()

    @pl.when(tile >= 2)
    def _():
        wait_slot(slot)

    col = col_ref[...]
    srow = lax.broadcasted_iota(jnp.int32, (comp.shape[1], col.shape[1]), 0)
    sel = jnp.where((srow == col[0:1]) | (srow == col[1:2]), 1.0, 0.0).astype(BF16)
    comp[slot] = jnp.dot(sel, h_ref[...], preferred_element_type=F32).astype(BF16)
    for cp in _chunk_copies(dst_ref, tile, comp.at[slot], xs_ref, sem.at[slot], True):
        cp.start()

    @pl.when(tile == last)
    def _():
        wait_slot(slot)

    @pl.when((tile == last) & (tile >= 1))
    def _():
        wait_slot(1 - slot)


def _dispatch(dst, col, h, n_e, init):
    t, d = h.shape
    tm = MOE_TILE
    loc_rows, _, n_sorted = _moe_rows(t, tm, n_e)
    if init is None:
        init = jnp.zeros((n_sorted, d), BF16)
    return pl.pallas_call(
        _dispatch_kernel,
        out_shape=jax.ShapeDtypeStruct((n_sorted, d), BF16),
        grid_spec=pltpu.PrefetchScalarGridSpec(
            num_scalar_prefetch=1,
            grid=(t // tm,),
            in_specs=[pl.BlockSpec((2, tm), lambda i, dst: (0, i)),
                      pl.BlockSpec((tm, d), lambda i, dst: (i, 0)),
                      pl.BlockSpec(memory_space=pl.ANY)],
            out_specs=pl.BlockSpec(memory_space=pl.ANY),
            scratch_shapes=[pltpu.VMEM((2, loc_rows, d), BF16), pltpu.SemaphoreType.DMA((2,))],
        ),
        input_output_aliases={3: 0},
        compiler_params=_cparams(("arbitrary",)),
        name="moe_dispatch",
    )(dst, col, h, init)


def _expert_kernel(te_ref, tv_ref, tf_ref, xs_ref, wg_ref, wu_ref, wd_ref, ys_ref, wg_b, wu_b, wd_b):
    del te_ref
    s = pl.program_id(0)

    @pl.when(tf_ref[s] == 1)
    def _():
        wg_b[...] = wg_ref[...].astype(BF16)
        wu_b[...] = wu_ref[...].astype(BF16)
        wd_b[...] = wd_ref[...].astype(BF16)

    @pl.when(tv_ref[s] == 1)
    def _():
        rows = ys_ref.shape[0] // MOE_FFN_SPLIT
        hs = []
        for r in range(MOE_FFN_SPLIT):
            xs = xs_ref[r * rows:(r + 1) * rows, :]
            hg = jnp.dot(xs, wg_b[...], preferred_element_type=F32)
            hu = jnp.dot(xs, wu_b[...], preferred_element_type=F32)
            hs.append((hg, hu))
        for r, (hg, hu) in enumerate(hs):
            he = hg * (1.0 / (1.0 + jnp.exp(-hg))) * hu
            ys_ref[r * rows:(r + 1) * rows, :] = jnp.dot(
                he.astype(BF16), wd_b[...], preferred_element_type=F32).astype(BF16)

    @pl.when(tv_ref[s] == 0)
    def _():
        ys_ref[...] = jnp.zeros(ys_ref.shape, ys_ref.dtype)


def _experts(te, tvalid, tfirst, xs, wg, wu, wd, layer):
    n_sorted, d = xs.shape
    f = wg.shape[3]
    ft = MOE_FFN_TILE
    wspec = lambda r, c: pl.BlockSpec((None, None, r, c), lambda s, te, tv, tf: (layer, te[s], 0, 0))
    return pl.pallas_call(
        _expert_kernel,
        out_shape=jax.ShapeDtypeStruct((n_sorted, d), BF16),
        grid_spec=pltpu.PrefetchScalarGridSpec(
            num_scalar_prefetch=3,
            grid=(n_sorted // ft,),
            in_specs=[pl.BlockSpec((ft, d), lambda s, te, tv, tf: (s, 0)),
                      wspec(d, f), wspec(d, f), wspec(f, d)],
            out_specs=pl.BlockSpec((ft, d), lambda s, te, tv, tf: (s, 0)),
            scratch_shapes=[pltpu.VMEM((d, f), BF16), pltpu.VMEM((d, f), BF16), pltpu.VMEM((f, d), BF16)],
        ),
        compiler_params=_cparams(("arbitrary",)),
        name="moe_experts",
    )(te, tvalid, tfirst, xs, wg, wu, wd)


def _combine_kernel(dst_ref, col_ref, w_ref, ys_ref, x_ref, g2_ref, o_ref, comp, sem):
    tile = pl.program_id(0)
    slot = tile % 2

    def fetch(t, s):
        for cp in _chunk_copies(dst_ref, t, comp.at[s], ys_ref, sem.at[s], False):
            cp.start()

    def wait_slot(s):
        pltpu.make_async_copy(ys_ref.at[pl.ds(0, comp.shape[1])], comp.at[s], sem.at[s]).wait()

    last = pl.num_programs(0) - 1

    @pl.when(tile == 0)
    def _():
        fetch(tile, slot)

    fetch(jnp.minimum(tile + 1, last), 1 - slot)

    col = col_ref[...]
    w = w_ref[...]
    scol = lax.broadcasted_iota(jnp.int32, (col.shape[0], comp.shape[1]), 1)
    selw = (jnp.where(scol == col[:, 0:1], w[:, 4:5], 0.0)
            + jnp.where(scol == col[:, 1:2], w[:, 5:6], 0.0)).astype(BF16)
    wait_slot(slot)
    acc = jnp.dot(selw, comp[slot], preferred_element_type=F32)
    o_ref[...] = x_ref[...] + g2_ref[...] * acc

    @pl.when(tile == last)
    def _():
        wait_slot(1 - slot)


def _combine(dst, col_t, route_t, ys, x2, g2, seq, n_e):
    t, d = x2.shape
    tm = MOE_TILE
    per_b = seq // tm
    loc_rows = _moe_rows(t, tm, n_e)[0]
    return pl.pallas_call(
        _combine_kernel,
        out_shape=jax.ShapeDtypeStruct((t, d), F32),
        grid_spec=pltpu.PrefetchScalarGridSpec(
            num_scalar_prefetch=1,
            grid=(t // tm,),
            in_specs=[pl.BlockSpec((tm, 2), lambda i, dst: (i, 0)),
                      pl.BlockSpec((tm, 8), lambda i, dst: (i, 0)),
                      pl.BlockSpec(memory_space=pl.ANY),
                      pl.BlockSpec((tm, d), lambda i, dst: (i, 0)),
                      pl.BlockSpec((None, 1, d), lambda i, dst: (i // per_b, 0, 0))],
            out_specs=pl.BlockSpec((tm, d), lambda i, dst: (i, 0)),
            scratch_shapes=[pltpu.VMEM((2, loc_rows, d), BF16), pltpu.SemaphoreType.DMA((2,))],
        ),
        compiler_params=_cparams(("arbitrary",)),
        name="moe_combine",
    )(dst, col_t, route_t, ys, x2, g2)


def _moe(h, route, wg, wu, wd, layer, x2, g2, seq, sorted_init):
    t = x2.shape[0]
    n_e = wg.shape[1]
    col, dst, te, tvalid, tfirst = _moe_plan(route, t, MOE_TILE, n_e)
    xs = _dispatch(dst, col, h, n_e, sorted_init)
    ys = _experts(te, tvalid, tfirst, xs, wg, wu, wd, layer)
    return _combine(dst, col.T, route.T, ys, x2, g2, seq, n_e), ys


def kernel(x, c, ada_w, ada_b, norm_mix_g, norm_ffn_g, t5_bias, a_w_qkv, a_q_gain, a_k_gain, a_lambda, a_subln_g, a_w_o, b_w_qkv, b_q_gain, b_k_gain, b_rel_bias, b_w_o, router_w, router_bias, moe_w_gate, moe_w_up, moe_w_down):
    batch, seq, d = x.shape
    depth = ada_w.shape[0]
    assert seq % ATTN_TILE == 0 and seq % ROW_TILE == 0 and d == A_HEADS * 2 * A_HEAD_DIM
    assert d == B_HEADS * B_HEAD_DIM and A_HEAD_DIM == B_HEAD_DIM

    c_pad = jnp.zeros((8, d), F32).at[:batch].set(c.astype(F32))
    mod = _modulation(c_pad, ada_w.astype(F32), ada_b.astype(F32))[:, :batch]
    mod = mod.reshape(depth, batch, 6, 1, d)

    rwt = router_w.astype(F32).T
    rwt_hi = lax.bitcast_convert_type(
        lax.bitcast_convert_type(rwt, jnp.uint32) & jnp.uint32(0xFFFF0000), F32)
    rwt = jnp.concatenate([rwt_hi, rwt - rwt_hi], axis=0).astype(BF16)
    rb = router_bias.astype(F32).reshape(-1, 1)

    x2 = x.astype(F32).reshape(batch * seq, d)
    sorted_buf = None
    for i in range(depth):
        sh1, sc1, g1, sh2, sc2, g2 = [mod[i, :, k] for k in range(6)]
        j = i // 2
        if i % 2 == 0:
            w_qkv, qg, kg, w_o = a_w_qkv[j], a_q_gain[j], a_k_gain[j], a_w_o[j]
        else:
            w_qkv, qg, kg, w_o = b_w_qkv[j], b_q_gain[j], b_k_gain[j], b_w_o[j]
        n_rep = d // qg.shape[0]
        gq = (jnp.tile(qg.astype(F32), n_rep) * (A_HEAD_DIM ** -0.5 * LOG2E)).reshape(d, 1)
        gk = jnp.tile(kg.astype(F32), n_rep).reshape(1, d)
        qt, k, vt = _qkv_proj(x2, norm_mix_g[i].reshape(1, d), sc1, sh1, w_qkv.astype(BF16),
                              gq, gk, seq, A_HEAD_DIM)
        if i % 2 == 0:
            lambda_init = 0.8 - 0.6 * math.exp(-0.3 * i)
            o = _attention_a(qt, k, vt, _t5_vectors(t5_bias, ATTN_TILE), a_lambda[j].astype(F32),
                             a_subln_g[j].reshape(1, -1), batch, seq, lambda_init)
        else:
            o = _attention_b(qt, k, vt, _band_vectors(b_rel_bias[j], BAND_TILE), batch, seq)
        x2, h, route = _out_proj(o, w_o.astype(BF16), x2, g1, norm_ffn_g[i].reshape(1, d),
                                 sc2, sh2, rwt, rb, seq)
        x2, sorted_buf = _moe(h, route, moe_w_gate, moe_w_up, moe_w_down, i, x2, g2, seq, sorted_buf)
    return x2.reshape(batch, seq, d)
```

```python
import functools
import math

import numpy as np
import jax
import jax.numpy as jnp
from jax import lax
from jax.experimental import pallas as pl
from jax.experimental.pallas import tpu as pltpu

F32 = jnp.float32
BF16 = jnp.bfloat16

CHUNK = 64
A_HEADS = 8
A_HEAD_DIM = 64
T5_BUCKETS = 32
T5_MAX_DIST = 1024
B_HEADS = 16
B_HEAD_DIM = 64
LEFT_CHUNKS = 8
MAX_REL = 256
N_EXPERTS = 16
N_GROUPS = 4
E_PER_GROUP = N_EXPERTS // N_GROUPS
NORM_EPS = 1e-6
NEG_INF = -1e30
LOG2E = math.log2(math.e)

V7X_LANES = 128
V7X_MXU_DIM = 256

ATTN_TILE = 512
MOD_COL_TILE = 1536
FAR_UNROLL = 4
BAND_TILE = 256
BAND_QBLOCKS = 16
BAND_AHEAD = 2
ROW_TILE = 512
QKV_SPLIT = 2
MOE_TILE = 512
OPROJ_ROW_TILE = 1024
OPROJ_SPLIT = 8
MOE_CHUNK = 16
MOE_FFN_TILE = 512
MOE_FFN_SPLIT = 2
VMEM_LIMIT = 56 * 1024 * 1024

_NT = (((1,), (1,)), ((), ()))


def _cparams(sem):
    return pltpu.CompilerParams(dimension_semantics=sem, vmem_limit_bytes=VMEM_LIMIT)


def _mod_kernel(c_ref, w_ref, b_ref, o_ref):
    c = c_ref[...]
    s = c * (1.0 / (1.0 + jnp.exp(-c)))
    o_ref[...] = jnp.dot(s, w_ref[...], preferred_element_type=F32,
                         precision=lax.Precision.HIGHEST) + b_ref[...]


def _modulation(c_pad, ada_w, ada_b):
    depth, d, n = ada_w.shape
    rows = c_pad.shape[0]
    tn = MOD_COL_TILE
    return pl.pallas_call(
        _mod_kernel,
        out_shape=jax.ShapeDtypeStruct((depth, rows, n), F32),
        grid=(depth, n // tn),
        in_specs=[
            pl.BlockSpec((rows, d), lambda i, j: (0, 0)),
            pl.BlockSpec((None, d, tn), lambda i, j: (i, 0, j)),
            pl.BlockSpec((None, 1, tn), lambda i, j: (i, 0, j)),
        ],
        out_specs=pl.BlockSpec((None, rows, tn), lambda i, j: (i, 0, j)),
        compiler_params=_cparams(("parallel", "parallel")),
        name="adaln_mod",
    )(c_pad, ada_w, ada_b.reshape(depth, 1, n))


def _norm_mod(x, g, sc, sh):
    ms = jnp.mean(x * x, axis=-1, keepdims=True)
    return x * lax.rsqrt(ms + NORM_EPS) * g * (1.0 + sc) + sh


def _qkv_kernel(x_ref, g_ref, sc_ref, sh_ref, w_ref, gq_ref, gk_ref,
                qt_ref, k_ref, vt_ref, wqt_ref, wvt_ref, *, head_dim):
    d, tm = qt_ref.shape

    @pl.when(pl.program_id(0) == 0)
    def _():
        wqt_ref[...] = w_ref[:, :d].T
        wvt_ref[...] = w_ref[:, 2 * d:].T

    cw = V7X_MXU_DIM
    r = lax.broadcasted_iota(jnp.int32, (cw, cw), 0) // head_dim
    c = lax.broadcasted_iota(jnp.int32, (cw, cw), 1) // head_dim
    gmat = jnp.where(r == c, 1.0 / head_dim, 0.0).astype(BF16)
    tg = tm // QKV_SPLIT
    for gi in range(QKV_SPLIT):
        rows = slice(gi * tg, (gi + 1) * tg)
        h = _norm_mod(x_ref[rows, :], g_ref[...], sc_ref[...], sh_ref[...]).astype(BF16)
        y = lax.dot_general(wqt_ref[...], h, _NT, preferred_element_type=F32)
        y3 = y.reshape(d // head_dim, head_dim, tg)
        ss = jnp.mean(y3 * y3, axis=1, keepdims=True)
        qt_ref[:, rows] = ((y3 * lax.rsqrt(ss + NORM_EPS)).reshape(d, tg) * gq_ref[...]).astype(BF16)
        y = jnp.dot(h, w_ref[:, d:2 * d], preferred_element_type=F32)
        ysq = (y * y).astype(BF16)
        for ci in range(d // cw):
            cols = slice(ci * cw, (ci + 1) * cw)
            ss = jnp.dot(ysq[:, cols], gmat, preferred_element_type=F32)
            k_ref[rows, cols] = (y[:, cols] * lax.rsqrt(ss + NORM_EPS) * gk_ref[:, cols]).astype(BF16)
        vt_ref[:, rows] = lax.dot_general(wvt_ref[...], h, _NT, preferred_element_type=F32).astype(BF16)


def _qkv_proj(x2, g, sc, sh, w, gq, gk, seq, head_dim):
    t, d = x2.shape
    tm = ROW_TILE * QKV_SPLIT
    per_b = seq // tm
    vec = lambda i: (i // per_b, 0, 0)
    full = lambda i: (0, 0)
    return pl.pallas_call(
        functools.partial(_qkv_kernel, head_dim=head_dim),
        out_shape=(jax.ShapeDtypeStruct((d, t), BF16),
                   jax.ShapeDtypeStruct((t, d), BF16),
                   jax.ShapeDtypeStruct((d, t), BF16)),
        grid=(t // tm,),
        in_specs=[
            pl.BlockSpec((tm, d), lambda i: (i, 0)),
            pl.BlockSpec((1, d), full),
            pl.BlockSpec((None, 1, d), vec),
            pl.BlockSpec((None, 1, d), vec),
            pl.BlockSpec((d, 3 * d), full),
            pl.BlockSpec((d, 1), full),
            pl.BlockSpec((1, d), full),
        ],
        out_specs=(pl.BlockSpec((d, tm), lambda i: (0, i)),
                   pl.BlockSpec((tm, d), lambda i: (i, 0)),
                   pl.BlockSpec((d, tm), lambda i: (0, i))),
        scratch_shapes=[pltpu.VMEM((d, d), BF16), pltpu.VMEM((d, d), BF16)],
        compiler_params=_cparams(("arbitrary",)),
        name="qkv_proj",
    )(x2, g, sc, sh, w, gq, gk)


def _softmax_pv(s, vt1, m_ref, acc_ref):
    m_prev = m_ref[...]
    m_new = jnp.maximum(m_prev, jnp.max(s, axis=0, keepdims=True))
    alpha = jnp.exp2(m_prev - m_new)
    p = jnp.exp2(s - m_new).astype(BF16)
    acc_ref[...] = alpha * acc_ref[...] + jnp.dot(vt1, p, preferred_element_type=F32)
    m_ref[...] = m_new


def _split_maps(qt, head_dim):
    row = lax.broadcasted_iota(jnp.int32, qt.shape, 0)
    zero = jnp.zeros_like(qt)
    return jnp.where(row < head_dim, qt, zero), jnp.where(row >= head_dim, qt, zero)


ONES_ROWS = 16


def _init_state(refs):
    for m_ref, acc_ref in refs:
        m_ref[...] = jnp.full(m_ref.shape, NEG_INF, F32)
        acc_ref[...] = jnp.zeros(acc_ref.shape, F32)


def _normalized(acc_ref):
    acc = acc_ref[...]
    return acc[:V7X_LANES] / acc[V7X_LANES:V7X_LANES + 1]


def _toeplitz(x_row, n):
    x = jnp.broadcast_to(x_row, (n, x_row.shape[1]))
    return pltpu.roll(x, 0, 1, stride=1, stride_axis=0)[:, :n]


def _chunk_delta(n):
    kc = lax.broadcasted_iota(jnp.int32, (n, n), 0) // CHUNK
    qc = lax.broadcasted_iota(jnp.int32, (n, n), 1) // CHUNK
    return qc - kc


def _attn_a_kernel(qt_ref, k_ref, vt_ref, x_ref, lam_ref, sub_ref, o_ref,
                   tab, fbuf, nbuf, m_st, acc_st, *, n_off, lambda_init):
    tq = ATTN_TILE
    step = pl.program_id(2)

    @pl.when(step == 0)
    def _():
        for o in range(n_off):
            t = _toeplitz(x_ref[o], tq) * LOG2E
            if o == 0:
                t = jnp.where(_chunk_delta(tq) >= 0, t, NEG_INF)
            tab[o] = t

    for c in range(2):
        _init_state(((m_st.at[c, 0], acc_st.at[c, 0]), (m_st.at[c, 1], acc_st.at[c, 1])))
    q_maps = [_split_maps(qt_ref[:, c * tq:(c + 1) * tq], A_HEAD_DIM) for c in range(2)]

    def k_tile(j):
        return k_ref[pl.ds(pl.multiple_of(j * tq, tq), tq), :]

    def vt1_tile(j):
        vt = vt_ref[:, pl.ds(pl.multiple_of(j * tq, tq), tq)]
        return jnp.concatenate([vt, jnp.ones((ONES_ROWS, tq), vt.dtype)], axis=0)

    def scores(buf, k, c):
        for m in range(2):
            buf[c, m] = jnp.dot(k, q_maps[c][m], preferred_element_type=F32)

    def consume(buf, vt1, c, o):
        for m in range(2):
            s = buf[c, m]
            _softmax_pv(s if o >= n_off else s + tab[o], vt1, m_st.at[c, m], acc_st.at[c, m])

    def near_block(first_tile, n_tiles, first_scored):
        units = [[c for c in range(2) if n_tiles - 2 + c - s >= 0] for s in range(n_tiles)]
        buf_of = lambda s: fbuf.at[0] if s == 0 else nbuf.at[s - 1]

        for s in range(1 if first_scored else 0, n_tiles):
            k = k_tile(first_tile + s)
            for c in units[s]:
                scores(buf_of(s), k, c)
        for s in range(n_tiles):
            vt1 = vt1_tile(first_tile + s)
            for c in units[s]:
                consume(buf_of(s), vt1, c, n_tiles - 2 + c - s)

    first_general = (n_off - 1) // 2
    for i0 in range(first_general):
        @pl.when(step == i0)
        def _():
            near_block(0, 2 * i0 + 2, False)

    def far_step(j, parity):
        k = k_tile(j + 1)
        for c in range(2):
            scores(fbuf.at[1 - parity], k, c)
        vt1 = vt1_tile(j)
        for c in range(2):
            consume(fbuf.at[parity], vt1, c, n_off)

    @pl.when(step >= first_general)
    def _():
        k = k_tile(0)
        for c in range(2):
            scores(fbuf.at[0], k, c)

    n_far = jnp.maximum(2 * step - (n_off - 1), 0)
    rem = n_far % FAR_UNROLL

    @pl.when(rem >= 2)
    def _():
        far_step(0, 0)
        far_step(1, 1)

    def far_trip(t, carry):
        for u in range(FAR_UNROLL):
            far_step(FAR_UNROLL * t + rem + u, u % 2)
        return carry
    lax.fori_loop(0, n_far // FAR_UNROLL, far_trip, 0)

    @pl.when(step >= first_general)
    def _():
        near_block(n_far, n_off + 1, True)

    lam = lam_ref[...]
    lam_full = (jnp.exp(jnp.sum(lam[0:1] * lam[1:2], axis=-1, keepdims=True))
                - jnp.exp(jnp.sum(lam[2:3] * lam[3:4], axis=-1, keepdims=True)) + lambda_init)
    for c in range(2):
        a = _normalized(acc_st.at[c, 0]) - lam_full * _normalized(acc_st.at[c, 1])
        ms = jnp.mean(a * a, axis=0, keepdims=True)
        an = a * lax.rsqrt(ms + NORM_EPS)
        o_ref[c * tq:(c + 1) * tq, :] = (an.T * (sub_ref[...] * (1.0 - lambda_init))).astype(BF16)


def _attn_b_kernel(qt_ref, k_ref, vt_ref, x_ref, o_ref, tab, sbuf):
    tq = BAND_TILE
    n_blk = qt_ref.shape[1] // tq
    n_off = tab.shape[0] // 2
    step = pl.program_id(2)

    @pl.when(step == 0)
    def _():
        for m in range(2):
            for o in range(n_off):
                d = _chunk_delta(tq) + o * (tq // CHUNK)
                t = _toeplitz(x_ref[n_off * m + o], tq) * LOG2E
                tab[n_off * m + o] = jnp.where((d >= 0) & (d <= LEFT_CHUNKS), t, NEG_INF)

    q_maps = [_split_maps(qt_ref[:, c * tq:(c + 1) * tq], B_HEAD_DIM) for c in range(n_blk)]

    def run(first_step):
        pairs = [(c, o) for c in range(n_blk) for o in range(n_off - 1, -1, -1)
                 if not first_step or c - o >= 0]
        k_tiles, vt_tiles = {}, {}
        for c, o in pairs:
            if c - o not in k_tiles:
                ks = pl.multiple_of((n_blk * step + c - o) * tq, tq)
                k_tiles[c - o] = k_ref[pl.ds(ks, tq), :]
                vt = vt_ref[:, pl.ds(ks, tq)]
                vt_tiles[c - o] = jnp.concatenate([vt, jnp.ones((ONES_ROWS, tq), vt.dtype)], axis=0)
        def tile_scores(c):
            for o in [o for cc, o in pairs if cc == c]:
                for m in range(2):
                    sbuf[n_off * c + o, m] = jnp.dot(k_tiles[c - o], q_maps[c][m], preferred_element_type=F32)

        for c in range(min(BAND_AHEAD, n_blk)):
            tile_scores(c)
        for c in range(n_blk):
            if c + BAND_AHEAD < n_blk:
                tile_scores(c + BAND_AHEAD)
            outs = []
            for m in range(2):
                offs = [o for cc, o in pairs if cc == c]
                ss = [sbuf[n_off * c + o, m] + tab[n_off * m + o] for o in offs]
                mx = functools.reduce(jnp.maximum, [jnp.max(s, axis=0, keepdims=True) for s in ss])
                acc = sum(jnp.dot(vt_tiles[c - o], jnp.exp2(s - mx).astype(BF16), preferred_element_type=F32)
                          for o, s in zip(offs, ss))
                outs.append(acc[:V7X_LANES] / acc[V7X_LANES:V7X_LANES + 1])
            row = lax.broadcasted_iota(jnp.int32, outs[0].shape, 0)
            o_ref[c * tq:(c + 1) * tq, :] = jnp.where(row < B_HEAD_DIM, outs[0], outs[1]).T.astype(BF16)

    @pl.when(step == 0)
    def _():
        run(True)

    @pl.when(step >= 1)
    def _():
        run(False)


def _attn_specs(batch, seq, nq, tq):
    return dict(
        q=pl.BlockSpec((V7X_LANES, tq), lambda b, h, i: (h, b * nq + i)),
        k=pl.BlockSpec((seq, V7X_LANES), lambda b, h, i: (b, h)),
        v=pl.BlockSpec((V7X_LANES, seq), lambda b, h, i: (h, b)),
        o=pl.BlockSpec((tq, V7X_LANES), lambda b, h, i: (b * nq + i, h)),
    )


def _attention_a(qt, k, vt, xvec, lam, sub_g, batch, seq, lambda_init):
    d, t = qt.shape
    tq = ATTN_TILE
    n_off = xvec.shape[1]
    assert n_off % 2 == 1 and seq % (2 * tq) == 0
    nq = seq // (2 * tq)
    sp = _attn_specs(batch, seq, nq, 2 * tq)
    rows = V7X_LANES + ONES_ROWS
    return pl.pallas_call(
        functools.partial(_attn_a_kernel, n_off=n_off, lambda_init=lambda_init),
        out_shape=jax.ShapeDtypeStruct((t, d), BF16),
        grid=(batch, d // V7X_LANES, nq),
        in_specs=[sp["q"], sp["k"], sp["v"],
                  pl.BlockSpec((None,) + xvec.shape[1:], lambda b, h, i: (h, 0, 0, 0)),
                  pl.BlockSpec(lam.shape, lambda b, h, i: (0, 0)),
                  pl.BlockSpec(sub_g.shape, lambda b, h, i: (0, 0))],
        out_specs=sp["o"],
        scratch_shapes=[pltpu.VMEM((n_off, tq, tq), F32), pltpu.VMEM((2, 2, 2, tq, tq), F32),
                        pltpu.VMEM((n_off, 2, 2, tq, tq), F32),
                        pltpu.VMEM((2, 2, 1, tq), F32), pltpu.VMEM((2, 2, rows, tq), F32)],
        compiler_params=_cparams(("parallel", "parallel", "arbitrary")),
        name="diff_attention",
    )(qt, k, vt, xvec, lam, sub_g)


def _attention_b(qt, k, vt, xvec, batch, seq):
    d, t = qt.shape
    tq = BAND_TILE
    g = BAND_QBLOCKS
    n_off = xvec.shape[1] // 2
    assert g >= n_off - 1 and seq % (g * tq) == 0
    nq = seq // (g * tq)
    sp = _attn_specs(batch, seq, nq, g * tq)
    return pl.pallas_call(
        _attn_b_kernel,
        out_shape=jax.ShapeDtypeStruct((t, d), BF16),
        grid=(batch, d // V7X_LANES, nq),
        in_specs=[sp["q"], sp["k"], sp["v"],
                  pl.BlockSpec((None,) + xvec.shape[1:], lambda b, h, i: (h, 0, 0, 0))],
        out_specs=sp["o"],
        scratch_shapes=[pltpu.VMEM((2 * n_off, tq, tq), F32), pltpu.VMEM((n_off * g, 2, tq, tq), F32)],
        compiler_params=_cparams(("parallel", "parallel", "arbitrary")),
        name="chunk_attention",
    )(qt, k, vt, xvec)


def _t5_bucket(rel):
    nb = T5_BUCKETS // 2
    ret = jnp.where(rel > 0, nb, 0)
    n = jnp.abs(rel)
    max_exact = nb // 2
    nf = jnp.maximum(n, 1).astype(F32)
    large = max_exact + (jnp.log(nf / max_exact) / math.log(T5_MAX_DIST / max_exact)
                         * (nb - max_exact)).astype(jnp.int32)
    large = jnp.minimum(large, nb - 1)
    return ret + jnp.where(n < max_exact, n, large)


def _t5_const_distance():
    nb = T5_BUCKETS // 2
    max_exact = nb // 2
    n = np.arange(max_exact, 4 * T5_MAX_DIST, dtype=np.float64)
    large = max_exact + np.floor(np.log(n / max_exact) / math.log(T5_MAX_DIST / max_exact) * (nb - max_exact))
    below = np.nonzero(large < nb - 1)[0]
    return int(n[below[-1]]) + 2


def _tile_rel(tile, n_off):
    i = jnp.arange(2 * tile, dtype=jnp.int32)
    rel = jnp.where(i < tile, -i, 2 * tile - i)
    return rel[None, :] - tile * jnp.arange(n_off, dtype=jnp.int32)[:, None]


def _t5_vectors(t5_bias, tile):
    n_off = 1
    while (n_off - 1) * tile + 1 < _t5_const_distance():
        n_off += 1
    tb = t5_bias.astype(F32)
    vals = tb[_t5_bucket(_tile_rel(tile, n_off))] - tb[T5_BUCKETS // 2 - 1]
    return vals.transpose(2, 0, 1)[:, :, None, :]


def _band_vectors(rel_bias, tile):
    n_off = LEFT_CHUNKS * CHUNK // tile + 1
    idx = jnp.clip(_tile_rel(tile, n_off), -MAX_REL, MAX_REL) + MAX_REL
    vals = rel_bias.astype(F32)[:, idx]
    return vals.reshape(rel_bias.shape[0] // 2, 2 * n_off, 1, 2 * tile)


def _top2_sum4(r0, r1, r2, r3):
    a, b = jnp.maximum(r0, r1), jnp.minimum(r0, r1)
    c, d = jnp.maximum(r2, r3), jnp.minimum(r2, r3)
    return jnp.maximum(a, c) + jnp.maximum(jnp.minimum(a, c), jnp.maximum(b, d))


def _route(logits, rbias):
    n_e, n = logits.shape
    scores = 1.0 / (1.0 + jnp.exp(-logits))
    sel = scores + rbias
    row = lax.broadcasted_iota(jnp.int32, sel.shape, 0)
    best = None
    for g in range(N_GROUPS):
        rows = [sel[g * E_PER_GROUP + i: g * E_PER_GROUP + i + 1, :] for i in range(E_PER_GROUP)]
        gs = _top2_sum4(*rows)
        if best is None:
            best, gidx = gs, jnp.zeros(gs.shape, jnp.int32)
        else:
            gidx = jnp.where(gs > best, g, gidx)
            best = jnp.maximum(best, gs)
    masked = jnp.where(row // E_PER_GROUP == gidx, sel, NEG_INF)
    m1 = jnp.max(masked, axis=0, keepdims=True)
    i1 = jnp.min(jnp.where(masked == m1, row, n_e), axis=0, keepdims=True)
    masked2 = jnp.where(row == i1, -3.0e38, masked)
    m2 = jnp.max(masked2, axis=0, keepdims=True)
    i2 = jnp.min(jnp.where(masked2 == m2, row, n_e), axis=0, keepdims=True)
    w1 = jnp.sum(jnp.where(row == i1, scores, 0.0), axis=0, keepdims=True)
    w2 = jnp.sum(jnp.where(row == i2, scores, 0.0), axis=0, keepdims=True)
    den = w1 + w2
    member = jnp.where((row == i1) | (row == i2), 1.0, 0.0).astype(BF16)
    t_from = lax.broadcasted_iota(jnp.int32, (n, n), 0)
    t_to = lax.broadcasted_iota(jnp.int32, (n, n), 1)
    before = jnp.where((t_from < t_to) & (t_from // MOE_TILE == t_to // MOE_TILE), 1.0, 0.0).astype(BF16)
    pos = jnp.dot(member, before, preferred_element_type=F32)
    p1 = jnp.sum(jnp.where(row == i1, pos, 0.0), axis=0, keepdims=True)
    p2 = jnp.sum(jnp.where(row == i2, pos, 0.0), axis=0, keepdims=True)
    out_row = lax.broadcasted_iota(jnp.int32, (8, n), 0)
    out = jnp.zeros((8, n), F32)
    for r, val in enumerate((i1.astype(F32), i2.astype(F32), p1, p2, w1 / den, w2 / den)):
        out = jnp.where(out_row == r, val, out)
    return out


def _oproj_kernel(o_ref, wo_ref, x_ref, g1_ref, gn_ref, sc_ref, sh_ref, rwt_ref, rb_ref,
                  xo_ref, h_ref, route_ref):
    n_e = rb_ref.shape[0]
    rw = rwt_ref[...]
    rows = o_ref.shape[0] // OPROJ_SPLIT
    groups = [slice(r * rows, (r + 1) * rows) for r in range(OPROJ_SPLIT)]
    ys = [jnp.dot(o_ref[g, :], wo_ref[...], preferred_element_type=F32) for g in groups]
    logits = []
    for g, y in zip(groups, ys):
        xn = x_ref[g, :] + g1_ref[...] * y
        xo_ref[g, :] = xn
        h = _norm_mod(xn, gn_ref[...], sc_ref[...], sh_ref[...])
        h_hi = h.astype(BF16)
        h_ref[g, :] = h_hi
        h_lo = (h - h_hi.astype(F32)).astype(BF16)
        main = lax.dot_general(rw, h_hi, _NT, preferred_element_type=F32)
        corr = lax.dot_general(rw[:n_e], h_lo, _NT, preferred_element_type=F32)
        logits.append(main[:n_e] + main[n_e:] + corr)
    route_ref[...] = _route(jnp.concatenate(logits, axis=1), rb_ref[...])


def _out_proj(o, wo_bf16, x2, g1, gn, sc, sh, rwt, rb, seq):
    t, d = x2.shape
    tm = OPROJ_ROW_TILE
    per_b = seq // tm
    n_e = rb.shape[0]
    vec = lambda i: (i // per_b, 0, 0)
    return pl.pallas_call(
        _oproj_kernel,
        out_shape=(jax.ShapeDtypeStruct((t, d), F32),
                   jax.ShapeDtypeStruct((t, d), BF16),
                   jax.ShapeDtypeStruct((8, t), F32)),
        grid=(t // tm,),
        in_specs=[
            pl.BlockSpec((tm, d), lambda i: (i, 0)),
            pl.BlockSpec((d, d), lambda i: (0, 0)),
            pl.BlockSpec((tm, d), lambda i: (i, 0)),
            pl.BlockSpec((None, 1, d), vec),
            pl.BlockSpec((1, d), lambda i: (0, 0)),
            pl.BlockSpec((None, 1, d), vec),
            pl.BlockSpec((None, 1, d), vec),
            pl.BlockSpec((2 * n_e, d), lambda i: (0, 0)),
            pl.BlockSpec((n_e, 1), lambda i: (0, 0)),
        ],
        out_specs=(pl.BlockSpec((tm, d), lambda i: (i, 0)),
                   pl.BlockSpec((tm, d), lambda i: (i, 0)),
                   pl.BlockSpec((8, tm), lambda i: (0, i))),
        compiler_params=_cparams(("parallel",)),
        name="out_proj_route",
    )(o, wo_bf16, x2, g1, gn, sc, sh, rwt, rb)


def _moe_plan(route, t, tm, n_e):
    n_tiles = t // tm
    ch, ft = MOE_CHUNK, MOE_FFN_TILE
    loc_rows, main_rows, n_sorted = _moe_rows(t, tm, n_e)
    ids = jnp.arange(n_e, dtype=jnp.int32)
    e = route[0:2].astype(jnp.int32)
    pos = route[2:4].astype(jnp.int32)
    oh = (e[:, :, None] == ids).astype(jnp.int32)
    cnt = oh.sum(0).reshape(n_tiles, tm, n_e).sum(1)
    seg = (cnt + ch - 1) // ch * ch
    loc = jnp.cumsum(seg, axis=1) - seg
    tot = seg.sum(0)
    totp = (tot + ft - 1) // ft * ft
    base = jnp.cumsum(totp) - totp
    gdest = base[None, :] + jnp.cumsum(seg, axis=0) - seg
    col = (oh * jnp.repeat(loc, tm, axis=0)[None]).sum(-1) + pos

    row0 = jnp.arange(loc_rows // ch, dtype=jnp.int32) * ch
    ej = (row0[None, :, None] >= (loc + seg)[:, None, :]).sum(-1)
    ohj = (jnp.minimum(ej, n_e - 1)[..., None] == ids).astype(jnp.int32)
    dst = (ohj * (gdest - loc)[:, None, :]).sum(-1) + row0[None, :]
    dump = main_rows + (jnp.arange(n_tiles, dtype=jnp.int32) % 2)[:, None] * loc_rows + row0[None, :]
    dst = jnp.where(ej < n_e, dst, dump) // ch

    r0 = jnp.arange(n_sorted // ft, dtype=jnp.int32) * ft
    ends = base + totp
    te = jnp.minimum((r0[:, None] >= ends[None, :]).sum(-1), n_e - 1)
    tvalid = (r0 < ends[-1]).astype(jnp.int32)
    tfirst = tvalid * (r0 == (((te[:, None] == ids) * base[None, :]).sum(-1))).astype(jnp.int32)
    return col, dst.astype(jnp.int32), te.astype(jnp.int32), tvalid, tfirst


def _moe_rows(t, tm, n_e):
    ch, ft = MOE_CHUNK, MOE_FFN_TILE
    loc_rows = 2 * tm + n_e * ch
    main = 2 * t + (t // tm) * n_e * (ch - 1) + n_e * (ft - 1)
    main = (main + ft - 1) // ft * ft
    return loc_rows, main, (main + 2 * loc_rows + ft - 1) // ft * ft


def _chunk_copies(dst_ref, tile, local_ref, sorted_ref, sem, to_sorted):
    ch = MOE_CHUNK
    copies = []
    for j in range(local_ref.shape[0] // ch):
        far = sorted_ref.at[pl.ds(pl.multiple_of(dst_ref[tile, j] * ch, ch), ch)]
        near = local_ref.at[pl.ds(j * ch, ch)]
        copies.append(pltpu.make_async_copy(near, far, sem) if to_sorted
                      else pltpu.make_async_copy(far, near, sem))
    return copies


def _dispatch_kernel(dst_ref, col_ref, h_ref, init_ref, xs_ref, comp, sem):
    del init_ref
    tile = pl.program_id(0)
    last = pl.num_programs(0) - 1
    slot = tile % 2

    def wait_slot(s):
        pltpu.make_async_copy(comp.at[s], xs_ref.at[pl.ds(0, comp.shape[1])], sem.at[s]).wait()

    @pl.when(tile >= 2)
    def _():
        wait_slot(slot)

    col = col_ref[...]
    srow = lax.broadcasted_iota(jnp.int32, (comp.shape[1], col.shape[1]), 0)
    sel = jnp.where((srow == col[0:1]) | (srow == col[1:2]), 1.0, 0.0).astype(BF16)
    comp[slot] = jnp.dot(sel, h_ref[...], preferred_element_type=F32).astype(BF16)
    for cp in _chunk_copies(dst_ref, tile, comp.at[slot], xs_ref, sem.at[slot], True):
        cp.start()

    @pl.when(tile == last)
    def _():
        wait_slot(slot)

    @pl.when((tile == last) & (tile >= 1))
    def _():
        wait_slot(1 - slot)


def _dispatch(dst, col, h, n_e, init):
    t, d = h.shape
    tm = MOE_TILE
    loc_rows, _, n_sorted = _moe_rows(t, tm, n_e)
    if init is None:
        init = jnp.zeros((n_sorted, d), BF16)
    return pl.pallas_call(
        _dispatch_kernel,
        out_shape=jax.ShapeDtypeStruct((n_sorted, d), BF16),
        grid_spec=pltpu.PrefetchScalarGridSpec(
            num_scalar_prefetch=1,
            grid=(t // tm,),
            in_specs=[pl.BlockSpec((2, tm), lambda i, dst: (0, i)),
                      pl.BlockSpec((tm, d), lambda i, dst: (i, 0)),
                      pl.BlockSpec(memory_space=pl.ANY)],
            out_specs=pl.BlockSpec(memory_space=pl.ANY),
            scratch_shapes=[pltpu.VMEM((2, loc_rows, d), BF16), pltpu.SemaphoreType.DMA((2,))],
        ),
        input_output_aliases={3: 0},
        compiler_params=_cparams(("arbitrary",)),
        name="moe_dispatch",
    )(dst, col, h, init)


def _expert_kernel(te_ref, tv_ref, tf_ref, xs_ref, wg_ref, wu_ref, wd_ref, ys_ref, wg_b, wu_b, wd_b):
    del te_ref
    s = pl.program_id(0)

    @pl.when(tf_ref[s] == 1)
    def _():
        wg_b[...] = wg_ref[...].astype(BF16)
        wu_b[...] = wu_ref[...].astype(BF16)
        wd_b[...] = wd_ref[...].astype(BF16)

    @pl.when(tv_ref[s] == 1)
    def _():
        rows = ys_ref.shape[0] // MOE_FFN_SPLIT
        hs = []
        for r in range(MOE_FFN_SPLIT):
            xs = xs_ref[r * rows:(r + 1) * rows, :]
            hg = jnp.dot(xs, wg_b[...], preferred_element_type=F32)
            hu = jnp.dot(xs, wu_b[...], preferred_element_type=F32)
            hs.append((hg, hu))
        for r, (hg, hu) in enumerate(hs):
            he = hg * (1.0 / (1.0 + jnp.exp(-hg))) * hu
            ys_ref[r * rows:(r + 1) * rows, :] = jnp.dot(
                he.astype(BF16), wd_b[...], preferred_element_type=F32).astype(BF16)

    @pl.when(tv_ref[s] == 0)
    def _():
        ys_ref[...] = jnp.zeros(ys_ref.shape, ys_ref.dtype)


def _experts(te, tvalid, tfirst, xs, wg, wu, wd, layer):
    n_sorted, d = xs.shape
    f = wg.shape[3]
    ft = MOE_FFN_TILE
    wspec = lambda r, c: pl.BlockSpec((None, None, r, c), lambda s, te, tv, tf: (layer, te[s], 0, 0))
    return pl.pallas_call(
        _expert_kernel,
        out_shape=jax.ShapeDtypeStruct((n_sorted, d), BF16),
        grid_spec=pltpu.PrefetchScalarGridSpec(
            num_scalar_prefetch=3,
            grid=(n_sorted // ft,),
            in_specs=[pl.BlockSpec((ft, d), lambda s, te, tv, tf: (s, 0)),
                      wspec(d, f), wspec(d, f), wspec(f, d)],
            out_specs=pl.BlockSpec((ft, d), lambda s, te, tv, tf: (s, 0)),
            scratch_shapes=[pltpu.VMEM((d, f), BF16), pltpu.VMEM((d, f), BF16), pltpu.VMEM((f, d), BF16)],
        ),
        compiler_params=_cparams(("arbitrary",)),
        name="moe_experts",
    )(te, tvalid, tfirst, xs, wg, wu, wd)


def _combine_kernel(dst_ref, col_ref, w_ref, ys_ref, x_ref, g2_ref, o_ref, comp, sem):
    tile = pl.program_id(0)
    slot = tile % 2

    def fetch(t, s):
        for cp in _chunk_copies(dst_ref, t, comp.at[s], ys_ref, sem.at[s], False):
            cp.start()

    def wait_slot(s):
        pltpu.make_async_copy(ys_ref.at[pl.ds(0, comp.shape[1])], comp.at[s], sem.at[s]).wait()

    last = pl.num_programs(0) - 1

    @pl.when(tile == 0)
    def _():
        fetch(tile, slot)

    fetch(jnp.minimum(tile + 1, last), 1 - slot)

    col = col_ref[...]
    w = w_ref[...]
    scol = lax.broadcasted_iota(jnp.int32, (col.shape[0], comp.shape[1]), 1)
    selw = (jnp.where(scol == col[:, 0:1], w[:, 4:5], 0.0)
            + jnp.where(scol == col[:, 1:2], w[:, 5:6], 0.0)).astype(BF16)
    wait_slot(slot)
    acc = jnp.dot(selw, comp[slot], preferred_element_type=F32)
    o_ref[...] = x_ref[...] + g2_ref[...] * acc

    @pl.when(tile == last)
    def _():
        wait_slot(1 - slot)


def _combine(dst, col_t, route_t, ys, x2, g2, seq, n_e):
    t, d = x2.shape
    tm = MOE_TILE
    per_b = seq // tm
    loc_rows = _moe_rows(t, tm, n_e)[0]
    return pl.pallas_call(
        _combine_kernel,
        out_shape=jax.ShapeDtypeStruct((t, d), F32),
        grid_spec=pltpu.PrefetchScalarGridSpec(
            num_scalar_prefetch=1,
            grid=(t // tm,),
            in_specs=[pl.BlockSpec((tm, 2), lambda i, dst: (i, 0)),
                      pl.BlockSpec((tm, 8), lambda i, dst: (i, 0)),
                      pl.BlockSpec(memory_space=pl.ANY),
                      pl.BlockSpec((tm, d), lambda i, dst: (i, 0)),
                      pl.BlockSpec((None, 1, d), lambda i, dst: (i // per_b, 0, 0))],
            out_specs=pl.BlockSpec((tm, d), lambda i, dst: (i, 0)),
            scratch_shapes=[pltpu.VMEM((2, loc_rows, d), BF16), pltpu.SemaphoreType.DMA((2,))],
        ),
        compiler_params=_cparams(("arbitrary",)),
        name="moe_combine",
    )(dst, col_t, route_t, ys, x2, g2)


def _moe(h, route, wg, wu, wd, layer, x2, g2, seq, sorted_init):
    t = x2.shape[0]
    n_e = wg.shape[1]
    col, dst, te, tvalid, tfirst = _moe_plan(route, t, MOE_TILE, n_e)
    xs = _dispatch(dst, col, h, n_e, sorted_init)
    ys = _experts(te, tvalid, tfirst, xs, wg, wu, wd, layer)
    return _combine(dst, col.T, route.T, ys, x2, g2, seq, n_e), ys


def kernel(x, c, ada_w, ada_b, norm_mix_g, norm_ffn_g, t5_bias, a_w_qkv, a_q_gain, a_k_gain, a_lambda, a_subln_g, a_w_o, b_w_qkv, b_q_gain, b_k_gain, b_rel_bias, b_w_o, router_w, router_bias, moe_w_gate, moe_w_up, moe_w_down):
    batch, seq, d = x.shape
    depth = ada_w.shape[0]
    assert seq % ATTN_TILE == 0 and seq % ROW_TILE == 0 and d == A_HEADS * 2 * A_HEAD_DIM
    assert d == B_HEADS * B_HEAD_DIM and A_HEAD_DIM == B_HEAD_DIM

    c_pad = jnp.zeros((8, d), F32).at[:batch].set(c.astype(F32))
    mod = _modulation(c_pad, ada_w.astype(F32), ada_b.astype(F32))[:, :batch]
    mod = mod.reshape(depth, batch, 6, 1, d)

    rwt = router_w.astype(F32).T
    rwt_hi = lax.bitcast_convert_type(
        lax.bitcast_convert_type(rwt, jnp.uint32) & jnp.uint32(0xFFFF0000), F32)
    rwt = jnp.concatenate([rwt_hi, rwt - rwt_hi], axis=0).astype(BF16)
    rb = router_bias.astype(F32).reshape(-1, 1)

    x2 = x.astype(F32).reshape(batch * seq, d)
    sorted_buf = None
    for i in range(depth):
        sh1, sc1, g1, sh2, sc2, g2 = [mod[i, :, k] for k in range(6)]
        j = i // 2
        if i % 2 == 0:
            w_qkv, qg, kg, w_o = a_w_qkv[j], a_q_gain[j], a_k_gain[j], a_w_o[j]
        else:
            w_qkv, qg, kg, w_o = b_w_qkv[j], b_q_gain[j], b_k_gain[j], b_w_o[j]
        n_rep = d // qg.shape[0]
        gq = (jnp.tile(qg.astype(F32), n_rep) * (A_HEAD_DIM ** -0.5 * LOG2E)).reshape(d, 1)
        gk = jnp.tile(kg.astype(F32), n_rep).reshape(1, d)
        qt, k, vt = _qkv_proj(x2, norm_mix_g[i].reshape(1, d), sc1, sh1, w_qkv.astype(BF16),
                              gq, gk, seq, A_HEAD_DIM)
        if i % 2 == 0:
            lambda_init = 0.8 - 0.6 * math.exp(-0.3 * i)
            o = _attention_a(qt, k, vt, _t5_vectors(t5_bias, ATTN_TILE), a_lambda[j].astype(F32),
                             a_subln_g[j].reshape(1, -1), batch, seq, lambda_init)
        else:
            o = _attention_b(qt, k, vt, _band_vectors(b_rel_bias[j], BAND_TILE), batch, seq)
        x2, h, route = _out_proj(o, w_o.astype(BF16), x2, g1, norm_ffn_g[i].reshape(1, d),
                                 sc2, sh2, rwt, rb, seq)
        x2, sorted_buf = _moe(h, route, moe_w_gate, moe_w_up, moe_w_down, i, x2, g2, seq, sorted_buf)
    return x2.reshape(batch, seq, d)
```

```python
import functools
import math

import numpy as np
import jax
import jax.numpy as jnp
from jax import lax
from jax.experimental import pallas as pl
from jax.experimental.pallas import tpu as pltpu

F32 = jnp.float32
BF16 = jnp.bfloat16

CHUNK = 64
A_HEADS = 8
A_HEAD_DIM = 64
T5_BUCKETS = 32
T5_MAX_DIST = 1024
B_HEADS = 16
B_HEAD_DIM = 64
LEFT_CHUNKS = 8
MAX_REL = 256
N_EXPERTS = 16
N_GROUPS = 4
E_PER_GROUP = N_EXPERTS // N_GROUPS
NORM_EPS = 1e-6
NEG_INF = -1e30
LOG2E = math.log2(math.e)

V7X_LANES = 128
V7X_MXU_DIM = 256

ATTN_TILE = 512
MOD_COL_TILE = 1536
FAR_UNROLL = 4
BAND_TILE = 256
BAND_QBLOCKS = 16
BAND_AHEAD = 2
ROW_TILE = 512
QKV_SPLIT = 2
MOE_TILE = 512
OPROJ_ROW_TILE = 1024
OPROJ_SPLIT = 8
MOE_CHUNK = 16
MOE_FFN_TILE = 512
MOE_FFN_SPLIT = 2
VMEM_LIMIT = 56 * 1024 * 1024

_NT = (((1,), (1,)), ((), ()))


def _cparams(sem):
    return pltpu.CompilerParams(dimension_semantics=sem, vmem_limit_bytes=VMEM_LIMIT)


def _mod_kernel(c_ref, w_ref, b_ref, o_ref):
    c = c_ref[...]
    s = c * (1.0 / (1.0 + jnp.exp(-c)))
    o_ref[...] = jnp.dot(s, w_ref[...], preferred_element_type=F32,
                         precision=lax.Precision.HIGHEST) + b_ref[...]


def _modulation(c_pad, ada_w, ada_b):
    depth, d, n = ada_w.shape
    rows = c_pad.shape[0]
    tn = MOD_COL_TILE
    return pl.pallas_call(
        _mod_kernel,
        out_shape=jax.ShapeDtypeStruct((depth, rows, n), F32),
        grid=(depth, n // tn),
        in_specs=[
            pl.BlockSpec((rows, d), lambda i, j: (0, 0)),
            pl.BlockSpec((None, d, tn), lambda i, j: (i, 0, j)),
            pl.BlockSpec((None, 1, tn), lambda i, j: (i, 0, j)),
        ],
        out_specs=pl.BlockSpec((None, rows, tn), lambda i, j: (i, 0, j)),
        compiler_params=_cparams(("parallel", "parallel")),
        name="adaln_mod",
    )(c_pad, ada_w, ada_b.reshape(depth, 1, n))


def _norm_mod(x, g, sc, sh):
    ms = jnp.mean(x * x, axis=-1, keepdims=True)
    return x * lax.rsqrt(ms + NORM_EPS) * g * (1.0 + sc) + sh


def _qkv_kernel(x_ref, g_ref, sc_ref, sh_ref, w_ref, gq_ref, gk_ref,
                qt_ref, k_ref, vt_ref, wqt_ref, wvt_ref, *, head_dim):
    d, tm = qt_ref.shape

    @pl.when(pl.program_id(0) == 0)
    def _():
        wqt_ref[...] = w_ref[:, :d].T
        wvt_ref[...] = w_ref[:, 2 * d:].T

    cw = V7X_MXU_DIM
    r = lax.broadcasted_iota(jnp.int32, (cw, cw), 0) // head_dim
    c = lax.broadcasted_iota(jnp.int32, (cw, cw), 1) // head_dim
    gmat = jnp.where(r == c, 1.0 / head_dim, 0.0).astype(BF16)
    tg = tm // QKV_SPLIT
    for gi in range(QKV_SPLIT):
        rows = slice(gi * tg, (gi + 1) * tg)
        h = _norm_mod(x_ref[rows, :], g_ref[...], sc_ref[...], sh_ref[...]).astype(BF16)
        y = lax.dot_general(wqt_ref[...], h, _NT, preferred_element_type=F32)
        y3 = y.reshape(d // head_dim, head_dim, tg)
        ss = jnp.mean(y3 * y3, axis=1, keepdims=True)
        qt_ref[:, rows] = ((y3 * lax.rsqrt(ss + NORM_EPS)).reshape(d, tg) * gq_ref[...]).astype(BF16)
        y = jnp.dot(h, w_ref[:, d:2 * d], preferred_element_type=F32)
        ysq = (y * y).astype(BF16)
        for ci in range(d // cw):
            cols = slice(ci * cw, (ci + 1) * cw)
            ss = jnp.dot(ysq[:, cols], gmat, preferred_element_type=F32)
            k_ref[rows, cols] = (y[:, cols] * lax.rsqrt(ss + NORM_EPS) * gk_ref[:, cols]).astype(BF16)
        vt_ref[:, rows] = lax.dot_general(wvt_ref[...], h, _NT, preferred_element_type=F32).astype(BF16)


def _qkv_proj(x2, g, sc, sh, w, gq, gk, seq, head_dim):
    t, d = x2.shape
    tm = ROW_TILE * QKV_SPLIT
    per_b = seq // tm
    vec = lambda i: (i // per_b, 0, 0)
    full = lambda i: (0, 0)
    return pl.pallas_call(
        functools.partial(_qkv_kernel, head_dim=head_dim),
        out_shape=(jax.ShapeDtypeStruct((d, t), BF16),
                   jax.ShapeDtypeStruct((t, d), BF16),
                   jax.ShapeDtypeStruct((d, t), BF16)),
        grid=(t // tm,),
        in_specs=[
            pl.BlockSpec((tm, d), lambda i: (i, 0)),
            pl.BlockSpec((1, d), full),
            pl.BlockSpec((None, 1, d), vec),
            pl.BlockSpec((None, 1, d), vec),
            pl.BlockSpec((d, 3 * d), full),
            pl.BlockSpec((d, 1), full),
            pl.BlockSpec((1, d), full),
        ],
        out_specs=(pl.BlockSpec((d, tm), lambda i: (0, i)),
                   pl.BlockSpec((tm, d), lambda i: (i, 0)),
                   pl.BlockSpec((d, tm), lambda i: (0, i))),
        scratch_shapes=[pltpu.VMEM((d, d), BF16), pltpu.VMEM((d, d), BF16)],
        compiler_params=_cparams(("arbitrary",)),
        name="qkv_proj",
    )(x2, g, sc, sh, w, gq, gk)


def _softmax_pv(s, vt1, m_ref, acc_ref):
    m_prev = m_ref[...]
    m_new = jnp.maximum(m_prev, jnp.max(s, axis=0, keepdims=True))
    alpha = jnp.exp2(m_prev - m_new)
    p = jnp.exp2(s - m_new).astype(BF16)
    acc_ref[...] = alpha * acc_ref[...] + jnp.dot(vt1, p, preferred_element_type=F32)
    m_ref[...] = m_new


def _split_maps(qt, head_dim):
    row = lax.broadcasted_iota(jnp.int32, qt.shape, 0)
    zero = jnp.zeros_like(qt)
    return jnp.where(row < head_dim, qt, zero), jnp.where(row >= head_dim, qt, zero)


ONES_ROWS = 16


def _init_state(refs):
    for m_ref, acc_ref in refs:
        m_ref[...] = jnp.full(m_ref.shape, NEG_INF, F32)
        acc_ref[...] = jnp.zeros(acc_ref.shape, F32)


def _normalized(acc_ref):
    acc = acc_ref[...]
    return acc[:V7X_LANES] / acc[V7X_LANES:V7X_LANES + 1]


def _toeplitz(x_row, n):
    x = jnp.broadcast_to(x_row, (n, x_row.shape[1]))
    return pltpu.roll(x, 0, 1, stride=1, stride_axis=0)[:, :n]


def _chunk_delta(n):
    kc = lax.broadcasted_iota(jnp.int32, (n, n), 0) // CHUNK
    qc = lax.broadcasted_iota(jnp.int32, (n, n), 1) // CHUNK
    return qc - kc


def _attn_a_kernel(qt_ref, qn_ref, k_ref, vt_ref, x_ref, lam_ref, sub_ref, o_ref,
                   tab, fbuf, nbuf, m_st, acc_st, *, n_off, lambda_init):
    tq = ATTN_TILE
    step = pl.program_id(2)

    @pl.when(step == 0)
    def _():
        for o in range(n_off):
            t = _toeplitz(x_ref[o], tq) * LOG2E
            if o == 0:
                t = jnp.where(_chunk_delta(tq) >= 0, t, NEG_INF)
            tab[o] = t

    for c in range(2):
        _init_state(((m_st.at[c, 0], acc_st.at[c, 0]), (m_st.at[c, 1], acc_st.at[c, 1])))
    q_maps = [_split_maps(qt_ref[:, c * tq:(c + 1) * tq], A_HEAD_DIM) for c in range(2)]

    def k_tile(j):
        return k_ref[pl.ds(pl.multiple_of(j * tq, tq), tq), :]

    def vt1_tile(j):
        vt = vt_ref[:, pl.ds(pl.multiple_of(j * tq, tq), tq)]
        return jnp.concatenate([vt, jnp.ones((ONES_ROWS, tq), vt.dtype)], axis=0)

    def scores(buf, k, c):
        for m in range(2):
            buf[c, m] = jnp.dot(k, q_maps[c][m], preferred_element_type=F32)

    def consume(buf, vt1, c, o):
        for m in range(2):
            s = buf[c, m]
            _softmax_pv(s if o >= n_off else s + tab[o], vt1, m_st.at[c, m], acc_st.at[c, m])

    def near_block(first_tile, n_tiles, first_scored):
        units = [[c for c in range(2) if n_tiles - 2 + c - s >= 0] for s in range(n_tiles)]
        buf_of = lambda s: fbuf.at[0] if s == 0 else nbuf.at[s - 1]

        for s in range(1 if first_scored else 0, n_tiles):
            k = k_tile(first_tile + s)
            for c in units[s]:
                scores(buf_of(s), k, c)
        for s in range(n_tiles):
            vt1 = vt1_tile(first_tile + s)
            for c in units[s]:
                consume(buf_of(s), vt1, c, n_tiles - 2 + c - s)

    first_general = (n_off - 1) // 2
    for i0 in range(first_general):
        @pl.when(step == i0)
        def _():
            near_block(0, 2 * i0 + 2, False)

    def far_step(j, parity):
        k = k_tile(j + 1)
        for c in range(2):
            scores(fbuf.at[1 - parity], k, c)
        vt1 = vt1_tile(j)
        for c in range(2):
            consume(fbuf.at[parity], vt1, c, n_off)

    n_far = jnp.maximum(2 * step - (n_off - 1), 0)
    rem = n_far % FAR_UNROLL

    @pl.when(rem >= 2)
    def _():
        far_step(0, 0)
        far_step(1, 1)

    def far_trip(t, carry):
        for u in range(FAR_UNROLL):
            far_step(FAR_UNROLL * t + rem + u, u % 2)
        return carry
    lax.fori_loop(0, n_far // FAR_UNROLL, far_trip, 0)

    @pl.when(step >= first_general)
    def _():
        near_block(n_far, n_off + 1, True)

    def epilogue():
        lam = lam_ref[...]
        lam_full = (jnp.exp(jnp.sum(lam[0:1] * lam[1:2], axis=-1, keepdims=True))
                    - jnp.exp(jnp.sum(lam[2:3] * lam[3:4], axis=-1, keepdims=True)) + lambda_init)
        for c in range(2):
            a = _normalized(acc_st.at[c, 0]) - lam_full * _normalized(acc_st.at[c, 1])
            ms = jnp.mean(a * a, axis=0, keepdims=True)
            an = a * lax.rsqrt(ms + NORM_EPS)
            o_ref[c * tq:(c + 1) * tq, :] = (an.T * (sub_ref[...] * (1.0 - lambda_init))).astype(BF16)

    last = pl.num_programs(2) - 1

    @pl.when(step < last)
    def _():
        k = k_tile(0)
        for c in range(2):
            qa, qb = _split_maps(qn_ref[:, c * tq:(c + 1) * tq], A_HEAD_DIM)
            fbuf[0, c, 0] = jnp.dot(k, qa, preferred_element_type=F32)
            fbuf[0, c, 1] = jnp.dot(k, qb, preferred_element_type=F32)
        epilogue()

    @pl.when(step == last)
    def _():
        epilogue()


def _attn_b_kernel(qt_ref, k_ref, vt_ref, x_ref, o_ref, tab, sbuf):
    tq = BAND_TILE
    n_blk = qt_ref.shape[1] // tq
    n_off = tab.shape[0] // 2
    step = pl.program_id(2)

    @pl.when(step == 0)
    def _():
        for m in range(2):
            for o in range(n_off):
                d = _chunk_delta(tq) + o * (tq // CHUNK)
                t = _toeplitz(x_ref[n_off * m + o], tq) * LOG2E
                tab[n_off * m + o] = jnp.where((d >= 0) & (d <= LEFT_CHUNKS), t, NEG_INF)

    q_maps = [_split_maps(qt_ref[:, c * tq:(c + 1) * tq], B_HEAD_DIM) for c in range(n_blk)]

    def run(first_step):
        pairs = [(c, o) for c in range(n_blk) for o in range(n_off - 1, -1, -1)
                 if not first_step or c - o >= 0]
        k_tiles, vt_tiles = {}, {}
        for c, o in pairs:
            if c - o not in k_tiles:
                ks = pl.multiple_of((n_blk * step + c - o) * tq, tq)
                k_tiles[c - o] = k_ref[pl.ds(ks, tq), :]
                vt = vt_ref[:, pl.ds(ks, tq)]
                vt_tiles[c - o] = jnp.concatenate([vt, jnp.ones((ONES_ROWS, tq), vt.dtype)], axis=0)
        def tile_scores(c):
            for o in [o for cc, o in pairs if cc == c]:
                for m in range(2):
                    sbuf[n_off * c + o, m] = jnp.dot(k_tiles[c - o], q_maps[c][m], preferred_element_type=F32)

        for c in range(min(BAND_AHEAD, n_blk)):
            tile_scores(c)
        for c in range(n_blk):
            if c + BAND_AHEAD < n_blk:
                tile_scores(c + BAND_AHEAD)
            outs = []
            for m in range(2):
                offs = [o for cc, o in pairs if cc == c]
                ss = [sbuf[n_off * c + o, m] + tab[n_off * m + o] for o in offs]
                mx = functools.reduce(jnp.maximum, [jnp.max(s, axis=0, keepdims=True) for s in ss])
                acc = sum(jnp.dot(vt_tiles[c - o], jnp.exp2(s - mx).astype(BF16), preferred_element_type=F32)
                          for o, s in zip(offs, ss))
                outs.append(acc[:V7X_LANES] / acc[V7X_LANES:V7X_LANES + 1])
            row = lax.broadcasted_iota(jnp.int32, outs[0].shape, 0)
            o_ref[c * tq:(c + 1) * tq, :] = jnp.where(row < B_HEAD_DIM, outs[0], outs[1]).T.astype(BF16)

    @pl.when(step == 0)
    def _():
        run(True)

    @pl.when(step >= 1)
    def _():
        run(False)


def _attn_specs(batch, seq, nq, tq):
    return dict(
        q=pl.BlockSpec((V7X_LANES, tq), lambda b, h, i: (h, b * nq + i)),
        k=pl.BlockSpec((seq, V7X_LANES), lambda b, h, i: (b, h)),
        v=pl.BlockSpec((V7X_LANES, seq), lambda b, h, i: (h, b)),
        o=pl.BlockSpec((tq, V7X_LANES), lambda b, h, i: (b * nq + i, h)),
    )


def _attention_a(qt, k, vt, xvec, lam, sub_g, batch, seq, lambda_init):
    d, t = qt.shape
    tq = ATTN_TILE
    n_off = xvec.shape[1]
    assert n_off % 2 == 1 and seq % (2 * tq) == 0
    nq = seq // (2 * tq)
    sp = _attn_specs(batch, seq, nq, 2 * tq)
    rows = V7X_LANES + ONES_ROWS
    return pl.pallas_call(
        functools.partial(_attn_a_kernel, n_off=n_off, lambda_init=lambda_init),
        out_shape=jax.ShapeDtypeStruct((t, d), BF16),
        grid=(batch, d // V7X_LANES, nq),
        in_specs=[sp["q"],
                  pl.BlockSpec((V7X_LANES, 2 * tq), lambda b, h, i: (h, b * nq + jnp.minimum(i + 1, nq - 1))),
                  sp["k"], sp["v"],
                  pl.BlockSpec((None,) + xvec.shape[1:], lambda b, h, i: (h, 0, 0, 0)),
                  pl.BlockSpec(lam.shape, lambda b, h, i: (0, 0)),
                  pl.BlockSpec(sub_g.shape, lambda b, h, i: (0, 0))],
        out_specs=sp["o"],
        scratch_shapes=[pltpu.VMEM((n_off, tq, tq), F32), pltpu.VMEM((2, 2, 2, tq, tq), F32),
                        pltpu.VMEM((n_off, 2, 2, tq, tq), F32),
                        pltpu.VMEM((2, 2, 1, tq), F32), pltpu.VMEM((2, 2, rows, tq), F32)],
        compiler_params=_cparams(("parallel", "parallel", "arbitrary")),
        name="diff_attention",
    )(qt, qt, k, vt, xvec, lam, sub_g)


def _attention_b(qt, k, vt, xvec, batch, seq):
    d, t = qt.shape
    tq = BAND_TILE
    g = BAND_QBLOCKS
    n_off = xvec.shape[1] // 2
    assert g >= n_off - 1 and seq % (g * tq) == 0
    nq = seq // (g * tq)
    sp = _attn_specs(batch, seq, nq, g * tq)
    return pl.pallas_call(
        _attn_b_kernel,
        out_shape=jax.ShapeDtypeStruct((t, d), BF16),
        grid=(batch, d // V7X_LANES, nq),
        in_specs=[sp["q"], sp["k"], sp["v"],
                  pl.BlockSpec((None,) + xvec.shape[1:], lambda b, h, i: (h, 0, 0, 0))],
        out_specs=sp["o"],
        scratch_shapes=[pltpu.VMEM((2 * n_off, tq, tq), F32), pltpu.VMEM((n_off * g, 2, tq, tq), F32)],
        compiler_params=_cparams(("parallel", "parallel", "arbitrary")),
        name="chunk_attention",
    )(qt, k, vt, xvec)


def _t5_bucket(rel):
    nb = T5_BUCKETS // 2
    ret = jnp.where(rel > 0, nb, 0)
    n = jnp.abs(rel)
    max_exact = nb // 2
    nf = jnp.maximum(n, 1).astype(F32)
    large = max_exact + (jnp.log(nf / max_exact) / math.log(T5_MAX_DIST / max_exact)
                         * (nb - max_exact)).astype(jnp.int32)
    large = jnp.minimum(large, nb - 1)
    return ret + jnp.where(n < max_exact, n, large)


def _t5_const_distance():
    nb = T5_BUCKETS // 2
    max_exact = nb // 2
    n = np.arange(max_exact, 4 * T5_MAX_DIST, dtype=np.float64)
    large = max_exact + np.floor(np.log(n / max_exact) / math.log(T5_MAX_DIST / max_exact) * (nb - max_exact))
    below = np.nonzero(large < nb - 1)[0]
    return int(n[below[-1]]) + 2


def _tile_rel(tile, n_off):
    i = jnp.arange(2 * tile, dtype=jnp.int32)
    rel = jnp.where(i < tile, -i, 2 * tile - i)
    return rel[None, :] - tile * jnp.arange(n_off, dtype=jnp.int32)[:, None]


def _t5_vectors(t5_bias, tile):
    n_off = 1
    while (n_off - 1) * tile + 1 < _t5_const_distance():
        n_off += 1
    tb = t5_bias.astype(F32)
    vals = tb[_t5_bucket(_tile_rel(tile, n_off))] - tb[T5_BUCKETS // 2 - 1]
    return vals.transpose(2, 0, 1)[:, :, None, :]


def _band_vectors(rel_bias, tile):
    n_off = LEFT_CHUNKS * CHUNK // tile + 1
    idx = jnp.clip(_tile_rel(tile, n_off), -MAX_REL, MAX_REL) + MAX_REL
    vals = rel_bias.astype(F32)[:, idx]
    return vals.reshape(rel_bias.shape[0] // 2, 2 * n_off, 1, 2 * tile)


def _top2_sum4(r0, r1, r2, r3):
    a, b = jnp.maximum(r0, r1), jnp.minimum(r0, r1)
    c, d = jnp.maximum(r2, r3), jnp.minimum(r2, r3)
    return jnp.maximum(a, c) + jnp.maximum(jnp.minimum(a, c), jnp.maximum(b, d))


def _route(logits, rbias):
    n_e, n = logits.shape
    scores = 1.0 / (1.0 + jnp.exp(-logits))
    sel = scores + rbias
    row = lax.broadcasted_iota(jnp.int32, sel.shape, 0)
    best = None
    for g in range(N_GROUPS):
        rows = [sel[g * E_PER_GROUP + i: g * E_PER_GROUP + i + 1, :] for i in range(E_PER_GROUP)]
        gs = _top2_sum4(*rows)
        if best is None:
            best, gidx = gs, jnp.zeros(gs.shape, jnp.int32)
        else:
            gidx = jnp.where(gs > best, g, gidx)
            best = jnp.maximum(best, gs)
    masked = jnp.where(row // E_PER_GROUP == gidx, sel, NEG_INF)
    m1 = jnp.max(masked, axis=0, keepdims=True)
    i1 = jnp.min(jnp.where(masked == m1, row, n_e), axis=0, keepdims=True)
    masked2 = jnp.where(row == i1, -3.0e38, masked)
    m2 = jnp.max(masked2, axis=0, keepdims=True)
    i2 = jnp.min(jnp.where(masked2 == m2, row, n_e), axis=0, keepdims=True)
    w1 = jnp.sum(jnp.where(row == i1, scores, 0.0), axis=0, keepdims=True)
    w2 = jnp.sum(jnp.where(row == i2, scores, 0.0), axis=0, keepdims=True)
    den = w1 + w2
    member = jnp.where((row == i1) | (row == i2), 1.0, 0.0).astype(BF16)
    t_from = lax.broadcasted_iota(jnp.int32, (n, n), 0)
    t_to = lax.broadcasted_iota(jnp.int32, (n, n), 1)
    before = jnp.where((t_from < t_to) & (t_from // MOE_TILE == t_to // MOE_TILE), 1.0, 0.0).astype(BF16)
    pos = jnp.dot(member, before, preferred_element_type=F32)
    p1 = jnp.sum(jnp.where(row == i1, pos, 0.0), axis=0, keepdims=True)
    p2 = jnp.sum(jnp.where(row == i2, pos, 0.0), axis=0, keepdims=True)
    out_row = lax.broadcasted_iota(jnp.int32, (8, n), 0)
    out = jnp.zeros((8, n), F32)
    for r, val in enumerate((i1.astype(F32), i2.astype(F32), p1, p2, w1 / den, w2 / den)):
        out = jnp.where(out_row == r, val, out)
    return out


def _oproj_kernel(o_ref, wo_ref, x_ref, g1_ref, gn_ref, sc_ref, sh_ref, rwt_ref, rb_ref,
                  xo_ref, h_ref, route_ref):
    n_e = rb_ref.shape[0]
    rw = rwt_ref[...]
    rows = o_ref.shape[0] // OPROJ_SPLIT
    groups = [slice(r * rows, (r + 1) * rows) for r in range(OPROJ_SPLIT)]
    ys = [jnp.dot(o_ref[g, :], wo_ref[...], preferred_element_type=F32) for g in groups]
    logits = []
    for g, y in zip(groups, ys):
        xn = x_ref[g, :] + g1_ref[...] * y
        xo_ref[g, :] = xn
        h = _norm_mod(xn, gn_ref[...], sc_ref[...], sh_ref[...])
        h_hi = h.astype(BF16)
        h_ref[g, :] = h_hi
        h_lo = (h - h_hi.astype(F32)).astype(BF16)
        main = lax.dot_general(rw, h_hi, _NT, preferred_element_type=F32)
        corr = lax.dot_general(rw[:n_e], h_lo, _NT, preferred_element_type=F32)
        logits.append(main[:n_e] + main[n_e:] + corr)
    route_ref[...] = _route(jnp.concatenate(logits, axis=1), rb_ref[...])


def _out_proj(o, wo_bf16, x2, g1, gn, sc, sh, rwt, rb, seq):
    t, d = x2.shape
    tm = OPROJ_ROW_TILE
    per_b = seq // tm
    n_e = rb.shape[0]
    vec = lambda i: (i // per_b, 0, 0)
    return pl.pallas_call(
        _oproj_kernel,
        out_shape=(jax.ShapeDtypeStruct((t, d), F32),
                   jax.ShapeDtypeStruct((t, d), BF16),
                   jax.ShapeDtypeStruct((8, t), F32)),
        grid=(t // tm,),
        in_specs=[
            pl.BlockSpec((tm, d), lambda i: (i, 0)),
            pl.BlockSpec((d, d), lambda i: (0, 0)),
            pl.BlockSpec((tm, d), lambda i: (i, 0)),
            pl.BlockSpec((None, 1, d), vec),
            pl.BlockSpec((1, d), lambda i: (0, 0)),
            pl.BlockSpec((None, 1, d), vec),
            pl.BlockSpec((None, 1, d), vec),
            pl.BlockSpec((2 * n_e, d), lambda i: (0, 0)),
            pl.BlockSpec((n_e, 1), lambda i: (0, 0)),
        ],
        out_specs=(pl.BlockSpec((tm, d), lambda i: (i, 0)),
                   pl.BlockSpec((tm, d), lambda i: (i, 0)),
                   pl.BlockSpec((8, tm), lambda i: (0, i))),
        compiler_params=_cparams(("parallel",)),
        name="out_proj_route",
    )(o, wo_bf16, x2, g1, gn, sc, sh, rwt, rb)


def _moe_plan(route, t, tm, n_e):
    n_tiles = t // tm
    ch, ft = MOE_CHUNK, MOE_FFN_TILE
    loc_rows, main_rows, n_sorted = _moe_rows(t, tm, n_e)
    ids = jnp.arange(n_e, dtype=jnp.int32)
    e = route[0:2].astype(jnp.int32)
    pos = route[2:4].astype(jnp.int32)
    oh = (e[:, :, None] == ids).astype(jnp.int32)
    cnt = oh.sum(0).reshape(n_tiles, tm, n_e).sum(1)
    seg = (cnt + ch - 1) // ch * ch
    loc = jnp.cumsum(seg, axis=1) - seg
    tot = seg.sum(0)
    totp = (tot + ft - 1) // ft * ft
    base = jnp.cumsum(totp) - totp
    gdest = base[None, :] + jnp.cumsum(seg, axis=0) - seg
    col = (oh * jnp.repeat(loc, tm, axis=0)[None]).sum(-1) + pos

    row0 = jnp.arange(loc_rows // ch, dtype=jnp.int32) * ch
    ej = (row0[None, :, None] >= (loc + seg)[:, None, :]).sum(-1)
    ohj = (jnp.minimum(ej, n_e - 1)[..., None] == ids).astype(jnp.int32)
    dst = (ohj * (gdest - loc)[:, None, :]).sum(-1) + row0[None, :]
    dump = main_rows + (jnp.arange(n_tiles, dtype=jnp.int32) % 2)[:, None] * loc_rows + row0[None, :]
    dst = jnp.where(ej < n_e, dst, dump) // ch

    r0 = jnp.arange(n_sorted // ft, dtype=jnp.int32) * ft
    ends = base + totp
    te = jnp.minimum((r0[:, None] >= ends[None, :]).sum(-1), n_e - 1)
    tvalid = (r0 < ends[-1]).astype(jnp.int32)
    tfirst = tvalid * (r0 == (((te[:, None] == ids) * base[None, :]).sum(-1))).astype(jnp.int32)
    return col, dst.astype(jnp.int32), te.astype(jnp.int32), tvalid, tfirst


def _moe_rows(t, tm, n_e):
    ch, ft = MOE_CHUNK, MOE_FFN_TILE
    loc_rows = 2 * tm + n_e * ch
    main = 2 * t + (t // tm) * n_e * (ch - 1) + n_e * (ft - 1)
    main = (main + ft - 1) // ft * ft
    return loc_rows, main, (main + 2 * loc_rows + ft - 1) // ft * ft


def _chunk_copies(dst_ref, tile, local_ref, sorted_ref, sem, to_sorted):
    ch = MOE_CHUNK
    copies = []
    for j in range(local_ref.shape[0] // ch):
        far = sorted_ref.at[pl.ds(pl.multiple_of(dst_ref[tile, j] * ch, ch), ch)]
        near = local_ref.at[pl.ds(j * ch, ch)]
        copies.append(pltpu.make_async_copy(near, far, sem) if to_sorted
                      else pltpu.make_async_copy(far, near, sem))
    return copies


def _dispatch_kernel(dst_ref, col_ref, h_ref, init_ref, xs_ref, comp, sem):
    del init_ref
    tile = pl.program_id(0)
    last = pl.num_programs(0) - 1
    slot = tile % 2

    def wait_slot(s):
        pltpu.make_async_copy(comp.at[s], xs_ref.at[pl.ds(0, comp.shape[1])], sem.at[s]).wait()

    @pl.when(tile >= 2)
    def _():
        wait_slot(slot)

    col = col_ref[...]
    srow = lax.broadcasted_iota(jnp.int32, (comp.shape[1], col.shape[1]), 0)
    sel = jnp.where((srow == col[0:1]) | (srow == col[1:2]), 1.0, 0.0).astype(BF16)
    comp[slot] = jnp.dot(sel, h_ref[...], preferred_element_type=F32).astype(BF16)
    for cp in _chunk_copies(dst_ref, tile, comp.at[slot], xs_ref, sem.at[slot], True):
        cp.start()

    @pl.when(tile == last)
    def _():
        wait_slot(slot)

    @pl.when((tile == last) & (tile >= 1))
    def _():
        wait_slot(1 - slot)


def _dispatch(dst, col, h, n_e, init):
    t, d = h.shape
    tm = MOE_TILE
    loc_rows, _, n_sorted = _moe_rows(t, tm, n_e)
    if init is None:
        init = jnp.zeros((n_sorted, d), BF16)
    return pl.pallas_call(
        _dispatch_kernel,
        out_shape=jax.ShapeDtypeStruct((n_sorted, d), BF16),
        grid_spec=pltpu.PrefetchScalarGridSpec(
            num_scalar_prefetch=1,
            grid=(t // tm,),
            in_specs=[pl.BlockSpec((2, tm), lambda i, dst: (0, i)),
                      pl.BlockSpec((tm, d), lambda i, dst: (i, 0)),
                      pl.BlockSpec(memory_space=pl.ANY)],
            out_specs=pl.BlockSpec(memory_space=pl.ANY),
            scratch_shapes=[pltpu.VMEM((2, loc_rows, d), BF16), pltpu.SemaphoreType.DMA((2,))],
        ),
        input_output_aliases={3: 0},
        compiler_params=_cparams(("arbitrary",)),
        name="moe_dispatch",
    )(dst, col, h, init)


def _expert_kernel(te_ref, tv_ref, tf_ref, xs_ref, wg_ref, wu_ref, wd_ref, ys_ref, wg_b, wu_b, wd_b):
    del te_ref
    s = pl.program_id(0)

    @pl.when(tf_ref[s] == 1)
    def _():
        wg_b[...] = wg_ref[...].astype(BF16)
        wu_b[...] = wu_ref[...].astype(BF16)
        wd_b[...] = wd_ref[...].astype(BF16)

    @pl.when(tv_ref[s] == 1)
    def _():
        rows = ys_ref.shape[0] // MOE_FFN_SPLIT
        hs = []
        for r in range(MOE_FFN_SPLIT):
            xs = xs_ref[r * rows:(r + 1) * rows, :]
            hg = jnp.dot(xs, wg_b[...], preferred_element_type=F32)
            hu = jnp.dot(xs, wu_b[...], preferred_element_type=F32)
            hs.append((hg, hu))
        for r, (hg, hu) in enumerate(hs):
            he = hg * (1.0 / (1.0 + jnp.exp(-hg))) * hu
            ys_ref[r * rows:(r + 1) * rows, :] = jnp.dot(
                he.astype(BF16), wd_b[...], preferred_element_type=F32).astype(BF16)

    @pl.when(tv_ref[s] == 0)
    def _():
        ys_ref[...] = jnp.zeros(ys_ref.shape, ys_ref.dtype)


def _experts(te, tvalid, tfirst, xs, wg, wu, wd, layer):
    n_sorted, d = xs.shape
    f = wg.shape[3]
    ft = MOE_FFN_TILE
    wspec = lambda r, c: pl.BlockSpec((None, None, r, c), lambda s, te, tv, tf: (layer, te[s], 0, 0))
    return pl.pallas_call(
        _expert_kernel,
        out_shape=jax.ShapeDtypeStruct((n_sorted, d), BF16),
        grid_spec=pltpu.PrefetchScalarGridSpec(
            num_scalar_prefetch=3,
            grid=(n_sorted // ft,),
            in_specs=[pl.BlockSpec((ft, d), lambda s, te, tv, tf: (s, 0)),
                      wspec(d, f), wspec(d, f), wspec(f, d)],
            out_specs=pl.BlockSpec((ft, d), lambda s, te, tv, tf: (s, 0)),
            scratch_shapes=[pltpu.VMEM((d, f), BF16), pltpu.VMEM((d, f), BF16), pltpu.VMEM((f, d), BF16)],
        ),
        compiler_params=_cparams(("arbitrary",)),
        name="moe_experts",
    )(te, tvalid, tfirst, xs, wg, wu, wd)


def _combine_kernel(dst_ref, col_ref, w_ref, ys_ref, x_ref, g2_ref, o_ref, comp, sem):
    tile = pl.program_id(0)
    slot = tile % 2

    def fetch(t, s):
        for cp in _chunk_copies(dst_ref, t, comp.at[s], ys_ref, sem.at[s], False):
            cp.start()

    def wait_slot(s):
        pltpu.make_async_copy(ys_ref.at[pl.ds(0, comp.shape[1])], comp.at[s], sem.at[s]).wait()

    last = pl.num_programs(0) - 1

    @pl.when(tile == 0)
    def _():
        fetch(tile, slot)

    fetch(jnp.minimum(tile + 1, last), 1 - slot)

    col = col_ref[...]
    w = w_ref[...]
    scol = lax.broadcasted_iota(jnp.int32, (col.shape[0], comp.shape[1]), 1)
    selw = (jnp.where(scol == col[:, 0:1], w[:, 4:5], 0.0)
            + jnp.where(scol == col[:, 1:2], w[:, 5:6], 0.0)).astype(BF16)
    wait_slot(slot)
    acc = jnp.dot(selw, comp[slot], preferred_element_type=F32)
    o_ref[...] = x_ref[...] + g2_ref[...] * acc

    @pl.when(tile == last)
    def _():
        wait_slot(1 - slot)


def _combine(dst, col_t, route_t, ys, x2, g2, seq, n_e):
    t, d = x2.shape
    tm = MOE_TILE
    per_b = seq // tm
    loc_rows = _moe_rows(t, tm, n_e)[0]
    return pl.pallas_call(
        _combine_kernel,
        out_shape=jax.ShapeDtypeStruct((t, d), F32),
        grid_spec=pltpu.PrefetchScalarGridSpec(
            num_scalar_prefetch=1,
            grid=(t // tm,),
            in_specs=[pl.BlockSpec((tm, 2), lambda i, dst: (i, 0)),
                      pl.BlockSpec((tm, 8), lambda i, dst: (i, 0)),
                      pl.BlockSpec(memory_space=pl.ANY),
                      pl.BlockSpec((tm, d), lambda i, dst: (i, 0)),
                      pl.BlockSpec((None, 1, d), lambda i, dst: (i // per_b, 0, 0))],
            out_specs=pl.BlockSpec((tm, d), lambda i, dst: (i, 0)),
            scratch_shapes=[pltpu.VMEM((2, loc_rows, d), BF16), pltpu.SemaphoreType.DMA((2,))],
        ),
        compiler_params=_cparams(("arbitrary",)),
        name="moe_combine",
    )(dst, col_t, route_t, ys, x2, g2)


def _moe(h, route, wg, wu, wd, layer, x2, g2, seq, sorted_init):
    t = x2.shape[0]
    n_e = wg.shape[1]
    col, dst, te, tvalid, tfirst = _moe_plan(route, t, MOE_TILE, n_e)
    xs = _dispatch(dst, col, h, n_e, sorted_init)
    ys = _experts(te, tvalid, tfirst, xs, wg, wu, wd, layer)
    return _combine(dst, col.T, route.T, ys, x2, g2, seq, n_e), ys


def kernel(x, c, ada_w, ada_b, norm_mix_g, norm_ffn_g, t5_bias, a_w_qkv, a_q_gain, a_k_gain, a_lambda, a_subln_g, a_w_o, b_w_qkv, b_q_gain, b_k_gain, b_rel_bias, b_w_o, router_w, router_bias, moe_w_gate, moe_w_up, moe_w_down):
    batch, seq, d = x.shape
    depth = ada_w.shape[0]
    assert seq % ATTN_TILE == 0 and seq % ROW_TILE == 0 and d == A_HEADS * 2 * A_HEAD_DIM
    assert d == B_HEADS * B_HEAD_DIM and A_HEAD_DIM == B_HEAD_DIM

    c_pad = jnp.zeros((8, d), F32).at[:batch].set(c.astype(F32))
    mod = _modulation(c_pad, ada_w.astype(F32), ada_b.astype(F32))[:, :batch]
    mod = mod.reshape(depth, batch, 6, 1, d)

    rwt = router_w.astype(F32).T
    rwt_hi = lax.bitcast_convert_type(
        lax.bitcast_convert_type(rwt, jnp.uint32) & jnp.uint32(0xFFFF0000), F32)
    rwt = jnp.concatenate([rwt_hi, rwt - rwt_hi], axis=0).astype(BF16)
    rb = router_bias.astype(F32).reshape(-1, 1)

    x2 = x.astype(F32).reshape(batch * seq, d)
    sorted_buf = None
    for i in range(depth):
        sh1, sc1, g1, sh2, sc2, g2 = [mod[i, :, k] for k in range(6)]
        j = i // 2
        if i % 2 == 0:
            w_qkv, qg, kg, w_o = a_w_qkv[j], a_q_gain[j], a_k_gain[j], a_w_o[j]
        else:
            w_qkv, qg, kg, w_o = b_w_qkv[j], b_q_gain[j], b_k_gain[j], b_w_o[j]
        n_rep = d // qg.shape[0]
        gq = (jnp.tile(qg.astype(F32), n_rep) * (A_HEAD_DIM ** -0.5 * LOG2E)).reshape(d, 1)
        gk = jnp.tile(kg.astype(F32), n_rep).reshape(1, d)
        qt, k, vt = _qkv_proj(x2, norm_mix_g[i].reshape(1, d), sc1, sh1, w_qkv.astype(BF16),
                              gq, gk, seq, A_HEAD_DIM)
        if i % 2 == 0:
            lambda_init = 0.8 - 0.6 * math.exp(-0.3 * i)
            o = _attention_a(qt, k, vt, _t5_vectors(t5_bias, ATTN_TILE), a_lambda[j].astype(F32),
                             a_subln_g[j].reshape(1, -1), batch, seq, lambda_init)
        else:
            o = _attention_b(qt, k, vt, _band_vectors(b_rel_bias[j], BAND_TILE), batch, seq)
        x2, h, route = _out_proj(o, w_o.astype(BF16), x2, g1, norm_ffn_g[i].reshape(1, d),
                                 sc2, sh2, rwt, rb, seq)
        x2, sorted_buf = _moe(h, route, moe_w_gate, moe_w_up, moe_w_down, i, x2, g2, seq, sorted_buf)
    return x2.reshape(batch, seq, d)
```

```python
import functools
import math

import numpy as np
import jax
import jax.numpy as jnp
from jax import lax
from jax.experimental import pallas as pl
from jax.experimental.pallas import tpu as pltpu

F32 = jnp.float32
BF16 = jnp.bfloat16

CHUNK = 64
A_HEADS = 8
A_HEAD_DIM = 64
T5_BUCKETS = 32
T5_MAX_DIST = 1024
B_HEADS = 16
B_HEAD_DIM = 64
LEFT_CHUNKS = 8
MAX_REL = 256
N_EXPERTS = 16
N_GROUPS = 4
E_PER_GROUP = N_EXPERTS // N_GROUPS
NORM_EPS = 1e-6
NEG_INF = -1e30
LOG2E = math.log2(math.e)

V7X_LANES = 128
V7X_MXU_DIM = 256

ATTN_TILE = 512
MOD_COL_TILE = 1536
FAR_UNROLL = 4
BAND_TILE = 256
BAND_QBLOCKS = 16
BAND_AHEAD = 2
ROW_TILE = 512
QKV_SPLIT = 2
MOE_TILE = 512
OPROJ_ROW_TILE = 1024
OPROJ_SPLIT = 8
MOE_CHUNK = 16
COMBINE_SPLIT = 4
MOE_FFN_TILE = 512
MOE_FFN_SPLIT = 2
VMEM_LIMIT = 56 * 1024 * 1024

_NT = (((1,), (1,)), ((), ()))


def _cparams(sem):
    return pltpu.CompilerParams(dimension_semantics=sem, vmem_limit_bytes=VMEM_LIMIT)


def _mod_kernel(c_ref, w_ref, b_ref, o_ref):
    c = c_ref[...]
    s = c * (1.0 / (1.0 + jnp.exp(-c)))
    o_ref[...] = jnp.dot(s, w_ref[...], preferred_element_type=F32,
                         precision=lax.Precision.HIGHEST) + b_ref[...]


def _modulation(c_pad, ada_w, ada_b):
    depth, d, n = ada_w.shape
    rows = c_pad.shape[0]
    tn = MOD_COL_TILE
    return pl.pallas_call(
        _mod_kernel,
        out_shape=jax.ShapeDtypeStruct((depth, rows, n), F32),
        grid=(depth, n // tn),
        in_specs=[
            pl.BlockSpec((rows, d), lambda i, j: (0, 0)),
            pl.BlockSpec((None, d, tn), lambda i, j: (i, 0, j)),
            pl.BlockSpec((None, 1, tn), lambda i, j: (i, 0, j)),
        ],
        out_specs=pl.BlockSpec((None, rows, tn), lambda i, j: (i, 0, j)),
        compiler_params=_cparams(("parallel", "parallel")),
        name="adaln_mod",
    )(c_pad, ada_w, ada_b.reshape(depth, 1, n))


def _norm_mod(x, g, sc, sh):
    ms = jnp.mean(x * x, axis=-1, keepdims=True)
    return x * lax.rsqrt(ms + NORM_EPS) * g * (1.0 + sc) + sh


def _qkv_kernel(x_ref, g_ref, sc_ref, sh_ref, w_ref, gq_ref, gk_ref,
                qt_ref, k_ref, vt_ref, wqt_ref, wvt_ref, *, head_dim):
    d, tm = qt_ref.shape

    @pl.when(pl.program_id(0) == 0)
    def _():
        wqt_ref[...] = w_ref[:, :d].T
        wvt_ref[...] = w_ref[:, 2 * d:].T

    cw = V7X_MXU_DIM
    r = lax.broadcasted_iota(jnp.int32, (cw, cw), 0) // head_dim
    c = lax.broadcasted_iota(jnp.int32, (cw, cw), 1) // head_dim
    gmat = jnp.where(r == c, 1.0 / head_dim, 0.0).astype(BF16)
    tg = tm // QKV_SPLIT
    for gi in range(QKV_SPLIT):
        rows = slice(gi * tg, (gi + 1) * tg)
        h = _norm_mod(x_ref[rows, :], g_ref[...], sc_ref[...], sh_ref[...]).astype(BF16)
        y = lax.dot_general(wqt_ref[...], h, _NT, preferred_element_type=F32)
        y3 = y.reshape(d // head_dim, head_dim, tg)
        ss = jnp.mean(y3 * y3, axis=1, keepdims=True)
        qt_ref[:, rows] = ((y3 * lax.rsqrt(ss + NORM_EPS)).reshape(d, tg) * gq_ref[...]).astype(BF16)
        y = jnp.dot(h, w_ref[:, d:2 * d], preferred_element_type=F32)
        ysq = (y * y).astype(BF16)
        for ci in range(d // cw):
            cols = slice(ci * cw, (ci + 1) * cw)
            ss = jnp.dot(ysq[:, cols], gmat, preferred_element_type=F32)
            k_ref[rows, cols] = (y[:, cols] * lax.rsqrt(ss + NORM_EPS) * gk_ref[:, cols]).astype(BF16)
        vt_ref[:, rows] = lax.dot_general(wvt_ref[...], h, _NT, preferred_element_type=F32).astype(BF16)


def _qkv_proj(x2, g, sc, sh, w, gq, gk, seq, head_dim):
    t, d = x2.shape
    tm = ROW_TILE * QKV_SPLIT
    per_b = seq // tm
    vec = lambda i: (i // per_b, 0, 0)
    full = lambda i: (0, 0)
    return pl.pallas_call(
        functools.partial(_qkv_kernel, head_dim=head_dim),
        out_shape=(jax.ShapeDtypeStruct((d, t), BF16),
                   jax.ShapeDtypeStruct((t, d), BF16),
                   jax.ShapeDtypeStruct((d, t), BF16)),
        grid=(t // tm,),
        in_specs=[
            pl.BlockSpec((tm, d), lambda i: (i, 0)),
            pl.BlockSpec((1, d), full),
            pl.BlockSpec((None, 1, d), vec),
            pl.BlockSpec((None, 1, d), vec),
            pl.BlockSpec((d, 3 * d), full),
            pl.BlockSpec((d, 1), full),
            pl.BlockSpec((1, d), full),
        ],
        out_specs=(pl.BlockSpec((d, tm), lambda i: (0, i)),
                   pl.BlockSpec((tm, d), lambda i: (i, 0)),
                   pl.BlockSpec((d, tm), lambda i: (0, i))),
        scratch_shapes=[pltpu.VMEM((d, d), BF16), pltpu.VMEM((d, d), BF16)],
        compiler_params=_cparams(("arbitrary",)),
        name="qkv_proj",
    )(x2, g, sc, sh, w, gq, gk)


def _softmax_pv(s, vt1, m_ref, acc_ref):
    m_prev = m_ref[...]
    m_new = jnp.maximum(m_prev, jnp.max(s, axis=0, keepdims=True))
    alpha = jnp.exp2(m_prev - m_new)
    p = jnp.exp2(s - m_new).astype(BF16)
    acc_ref[...] = alpha * acc_ref[...] + jnp.dot(vt1, p, preferred_element_type=F32)
    m_ref[...] = m_new


def _split_maps(qt, head_dim):
    row = lax.broadcasted_iota(jnp.int32, qt.shape, 0)
    zero = jnp.zeros_like(qt)
    return jnp.where(row < head_dim, qt, zero), jnp.where(row >= head_dim, qt, zero)


ONES_ROWS = 16


def _init_state(refs):
    for m_ref, acc_ref in refs:
        m_ref[...] = jnp.full(m_ref.shape, NEG_INF, F32)
        acc_ref[...] = jnp.zeros(acc_ref.shape, F32)


def _normalized(acc_ref):
    acc = acc_ref[...]
    return acc[:V7X_LANES] / acc[V7X_LANES:V7X_LANES + 1]


def _toeplitz(x_row, n):
    x = jnp.broadcast_to(x_row, (n, x_row.shape[1]))
    return pltpu.roll(x, 0, 1, stride=1, stride_axis=0)[:, :n]


def _chunk_delta(n):
    kc = lax.broadcasted_iota(jnp.int32, (n, n), 0) // CHUNK
    qc = lax.broadcasted_iota(jnp.int32, (n, n), 1) // CHUNK
    return qc - kc


def _attn_a_kernel(qt_ref, qn_ref, k_ref, vt_ref, x_ref, lam_ref, sub_ref, o_ref,
                   tab, fbuf, nbuf, m_st, acc_st, *, n_off, lambda_init):
    tq = ATTN_TILE
    step = pl.program_id(2)

    @pl.when(step == 0)
    def _():
        for o in range(n_off):
            t = _toeplitz(x_ref[o], tq) * LOG2E
            if o == 0:
                t = jnp.where(_chunk_delta(tq) >= 0, t, NEG_INF)
            tab[o] = t

    for c in range(2):
        _init_state(((m_st.at[c, 0], acc_st.at[c, 0]), (m_st.at[c, 1], acc_st.at[c, 1])))
    q_maps = [_split_maps(qt_ref[:, c * tq:(c + 1) * tq], A_HEAD_DIM) for c in range(2)]

    def k_tile(j):
        return k_ref[pl.ds(pl.multiple_of(j * tq, tq), tq), :]

    def vt1_tile(j):
        vt = vt_ref[:, pl.ds(pl.multiple_of(j * tq, tq), tq)]
        return jnp.concatenate([vt, jnp.ones((ONES_ROWS, tq), vt.dtype)], axis=0)

    def scores(buf, k, c):
        for m in range(2):
            buf[c, m] = jnp.dot(k, q_maps[c][m], preferred_element_type=F32)

    def consume(buf, vt1, c, o):
        for m in range(2):
            s = buf[c, m]
            _softmax_pv(s if o >= n_off else s + tab[o], vt1, m_st.at[c, m], acc_st.at[c, m])

    def near_block(first_tile, n_tiles, first_scored):
        units = [[c for c in range(2) if n_tiles - 2 + c - s >= 0] for s in range(n_tiles)]
        buf_of = lambda s: fbuf.at[0] if s == 0 else nbuf.at[s - 1]

        for s in range(1 if first_scored else 0, n_tiles):
            k = k_tile(first_tile + s)
            for c in units[s]:
                scores(buf_of(s), k, c)
        for s in range(n_tiles):
            vt1 = vt1_tile(first_tile + s)
            for c in units[s]:
                consume(buf_of(s), vt1, c, n_tiles - 2 + c - s)

    first_general = (n_off - 1) // 2
    for i0 in range(first_general):
        @pl.when(step == i0)
        def _():
            near_block(0, 2 * i0 + 2, False)

    def far_step(j, parity):
        k = k_tile(j + 1)
        for c in range(2):
            scores(fbuf.at[1 - parity], k, c)
        vt1 = vt1_tile(j)
        for c in range(2):
            consume(fbuf.at[parity], vt1, c, n_off)

    n_far = jnp.maximum(2 * step - (n_off - 1), 0)
    rem = n_far % FAR_UNROLL

    @pl.when(rem >= 2)
    def _():
        far_step(0, 0)
        far_step(1, 1)

    def far_trip(t, carry):
        for u in range(FAR_UNROLL):
            far_step(FAR_UNROLL * t + rem + u, u % 2)
        return carry
    lax.fori_loop(0, n_far // FAR_UNROLL, far_trip, 0)

    @pl.when(step >= first_general)
    def _():
        near_block(n_far, n_off + 1, True)

    def epilogue():
        lam = lam_ref[...]
        lam_full = (jnp.exp(jnp.sum(lam[0:1] * lam[1:2], axis=-1, keepdims=True))
                    - jnp.exp(jnp.sum(lam[2:3] * lam[3:4], axis=-1, keepdims=True)) + lambda_init)
        for c in range(2):
            a = _normalized(acc_st.at[c, 0]) - lam_full * _normalized(acc_st.at[c, 1])
            ms = jnp.mean(a * a, axis=0, keepdims=True)
            an = a * lax.rsqrt(ms + NORM_EPS)
            o_ref[c * tq:(c + 1) * tq, :] = (an.T * (sub_ref[...] * (1.0 - lambda_init))).astype(BF16)

    last = pl.num_programs(2) - 1

    @pl.when(step < last)
    def _():
        k = k_tile(0)
        for c in range(2):
            qa, qb = _split_maps(qn_ref[:, c * tq:(c + 1) * tq], A_HEAD_DIM)
            fbuf[0, c, 0] = jnp.dot(k, qa, preferred_element_type=F32)
            fbuf[0, c, 1] = jnp.dot(k, qb, preferred_element_type=F32)
        epilogue()

    @pl.when(step == last)
    def _():
        epilogue()


def _attn_b_kernel(qt_ref, k_ref, vt_ref, x_ref, o_ref, tab, sbuf):
    tq = BAND_TILE
    n_blk = qt_ref.shape[1] // tq
    n_off = tab.shape[0] // 2
    step = pl.program_id(2)

    @pl.when(step == 0)
    def _():
        for m in range(2):
            for o in range(n_off):
                d = _chunk_delta(tq) + o * (tq // CHUNK)
                t = _toeplitz(x_ref[n_off * m + o], tq) * LOG2E
                tab[n_off * m + o] = jnp.where((d >= 0) & (d <= LEFT_CHUNKS), t, NEG_INF)

    q_maps = [_split_maps(qt_ref[:, c * tq:(c + 1) * tq], B_HEAD_DIM) for c in range(n_blk)]

    def run(first_step):
        pairs = [(c, o) for c in range(n_blk) for o in range(n_off - 1, -1, -1)
                 if not first_step or c - o >= 0]
        k_tiles, vt_tiles = {}, {}
        for c, o in pairs:
            if c - o not in k_tiles:
                ks = pl.multiple_of((n_blk * step + c - o) * tq, tq)
                k_tiles[c - o] = k_ref[pl.ds(ks, tq), :]
                vt = vt_ref[:, pl.ds(ks, tq)]
                vt_tiles[c - o] = jnp.concatenate([vt, jnp.ones((ONES_ROWS, tq), vt.dtype)], axis=0)
        def tile_scores(c):
            for o in [o for cc, o in pairs if cc == c]:
                for m in range(2):
                    sbuf[n_off * c + o, m] = jnp.dot(k_tiles[c - o], q_maps[c][m], preferred_element_type=F32)

        for c in range(min(BAND_AHEAD, n_blk)):
            tile_scores(c)
        for c in range(n_blk):
            if c + BAND_AHEAD < n_blk:
                tile_scores(c + BAND_AHEAD)
            outs = []
            for m in range(2):
                offs = [o for cc, o in pairs if cc == c]
                ss = [sbuf[n_off * c + o, m] + tab[n_off * m + o] for o in offs]
                mx = functools.reduce(jnp.maximum, [jnp.max(s, axis=0, keepdims=True) for s in ss])
                acc = sum(jnp.dot(vt_tiles[c - o], jnp.exp2(s - mx).astype(BF16), preferred_element_type=F32)
                          for o, s in zip(offs, ss))
                outs.append(acc[:V7X_LANES] / acc[V7X_LANES:V7X_LANES + 1])
            row = lax.broadcasted_iota(jnp.int32, outs[0].shape, 0)
            o_ref[c * tq:(c + 1) * tq, :] = jnp.where(row < B_HEAD_DIM, outs[0], outs[1]).T.astype(BF16)

    @pl.when(step == 0)
    def _():
        run(True)

    @pl.when(step >= 1)
    def _():
        run(False)


def _attn_specs(batch, seq, nq, tq):
    return dict(
        q=pl.BlockSpec((V7X_LANES, tq), lambda b, h, i: (h, b * nq + i)),
        k=pl.BlockSpec((seq, V7X_LANES), lambda b, h, i: (b, h)),
        v=pl.BlockSpec((V7X_LANES, seq), lambda b, h, i: (h, b)),
        o=pl.BlockSpec((tq, V7X_LANES), lambda b, h, i: (b * nq + i, h)),
    )


def _attention_a(qt, k, vt, xvec, lam, sub_g, batch, seq, lambda_init):
    d, t = qt.shape
    tq = ATTN_TILE
    n_off = xvec.shape[1]
    assert n_off % 2 == 1 and seq % (2 * tq) == 0
    nq = seq // (2 * tq)
    sp = _attn_specs(batch, seq, nq, 2 * tq)
    rows = V7X_LANES + ONES_ROWS
    return pl.pallas_call(
        functools.partial(_attn_a_kernel, n_off=n_off, lambda_init=lambda_init),
        out_shape=jax.ShapeDtypeStruct((t, d), BF16),
        grid=(batch, d // V7X_LANES, nq),
        in_specs=[sp["q"],
                  pl.BlockSpec((V7X_LANES, 2 * tq), lambda b, h, i: (h, b * nq + jnp.minimum(i + 1, nq - 1))),
                  sp["k"], sp["v"],
                  pl.BlockSpec((None,) + xvec.shape[1:], lambda b, h, i: (h, 0, 0, 0)),
                  pl.BlockSpec(lam.shape, lambda b, h, i: (0, 0)),
                  pl.BlockSpec(sub_g.shape, lambda b, h, i: (0, 0))],
        out_specs=sp["o"],
        scratch_shapes=[pltpu.VMEM((n_off, tq, tq), F32), pltpu.VMEM((2, 2, 2, tq, tq), F32),
                        pltpu.VMEM((n_off, 2, 2, tq, tq), F32),
                        pltpu.VMEM((2, 2, 1, tq), F32), pltpu.VMEM((2, 2, rows, tq), F32)],
        compiler_params=_cparams(("parallel", "parallel", "arbitrary")),
        name="diff_attention",
    )(qt, qt, k, vt, xvec, lam, sub_g)


def _attention_b(qt, k, vt, xvec, batch, seq):
    d, t = qt.shape
    tq = BAND_TILE
    g = BAND_QBLOCKS
    n_off = xvec.shape[1] // 2
    assert g >= n_off - 1 and seq % (g * tq) == 0
    nq = seq // (g * tq)
    sp = _attn_specs(batch, seq, nq, g * tq)
    return pl.pallas_call(
        _attn_b_kernel,
        out_shape=jax.ShapeDtypeStruct((t, d), BF16),
        grid=(batch, d // V7X_LANES, nq),
        in_specs=[sp["q"], sp["k"], sp["v"],
                  pl.BlockSpec((None,) + xvec.shape[1:], lambda b, h, i: (h, 0, 0, 0))],
        out_specs=sp["o"],
        scratch_shapes=[pltpu.VMEM((2 * n_off, tq, tq), F32), pltpu.VMEM((n_off * g, 2, tq, tq), F32)],
        compiler_params=_cparams(("parallel", "parallel", "arbitrary")),
        name="chunk_attention",
    )(qt, k, vt, xvec)


def _t5_bucket(rel):
    nb = T5_BUCKETS // 2
    ret = jnp.where(rel > 0, nb, 0)
    n = jnp.abs(rel)
    max_exact = nb // 2
    nf = jnp.maximum(n, 1).astype(F32)
    large = max_exact + (jnp.log(nf / max_exact) / math.log(T5_MAX_DIST / max_exact)
                         * (nb - max_exact)).astype(jnp.int32)
    large = jnp.minimum(large, nb - 1)
    return ret + jnp.where(n < max_exact, n, large)


def _t5_const_distance():
    nb = T5_BUCKETS // 2
    max_exact = nb // 2
    n = np.arange(max_exact, 4 * T5_MAX_DIST, dtype=np.float64)
    large = max_exact + np.floor(np.log(n / max_exact) / math.log(T5_MAX_DIST / max_exact) * (nb - max_exact))
    below = np.nonzero(large < nb - 1)[0]
    return int(n[below[-1]]) + 2


def _tile_rel(tile, n_off):
    i = jnp.arange(2 * tile, dtype=jnp.int32)
    rel = jnp.where(i < tile, -i, 2 * tile - i)
    return rel[None, :] - tile * jnp.arange(n_off, dtype=jnp.int32)[:, None]


def _t5_vectors(t5_bias, tile):
    n_off = 1
    while (n_off - 1) * tile + 1 < _t5_const_distance():
        n_off += 1
    tb = t5_bias.astype(F32)
    vals = tb[_t5_bucket(_tile_rel(tile, n_off))] - tb[T5_BUCKETS // 2 - 1]
    return vals.transpose(2, 0, 1)[:, :, None, :]


def _band_vectors(rel_bias, tile):
    n_off = LEFT_CHUNKS * CHUNK // tile + 1
    idx = jnp.clip(_tile_rel(tile, n_off), -MAX_REL, MAX_REL) + MAX_REL
    vals = rel_bias.astype(F32)[:, idx]
    return vals.reshape(rel_bias.shape[0] // 2, 2 * n_off, 1, 2 * tile)


def _top2_sum4(r0, r1, r2, r3):
    a, b = jnp.maximum(r0, r1), jnp.minimum(r0, r1)
    c, d = jnp.maximum(r2, r3), jnp.minimum(r2, r3)
    return jnp.maximum(a, c) + jnp.maximum(jnp.minimum(a, c), jnp.maximum(b, d))


def _route(logits, rbias):
    n_e, n = logits.shape
    scores = 1.0 / (1.0 + jnp.exp(-logits))
    sel = scores + rbias
    row = lax.broadcasted_iota(jnp.int32, sel.shape, 0)
    best = None
    for g in range(N_GROUPS):
        rows = [sel[g * E_PER_GROUP + i: g * E_PER_GROUP + i + 1, :] for i in range(E_PER_GROUP)]
        gs = _top2_sum4(*rows)
        if best is None:
            best, gidx = gs, jnp.zeros(gs.shape, jnp.int32)
        else:
            gidx = jnp.where(gs > best, g, gidx)
            best = jnp.maximum(best, gs)
    masked = jnp.where(row // E_PER_GROUP == gidx, sel, NEG_INF)
    m1 = jnp.max(masked, axis=0, keepdims=True)
    i1 = jnp.min(jnp.where(masked == m1, row, n_e), axis=0, keepdims=True)
    masked2 = jnp.where(row == i1, -3.0e38, masked)
    m2 = jnp.max(masked2, axis=0, keepdims=True)
    i2 = jnp.min(jnp.where(masked2 == m2, row, n_e), axis=0, keepdims=True)
    w1 = jnp.sum(jnp.where(row == i1, scores, 0.0), axis=0, keepdims=True)
    w2 = jnp.sum(jnp.where(row == i2, scores, 0.0), axis=0, keepdims=True)
    den = w1 + w2
    member = jnp.where((row == i1) | (row == i2), 1.0, 0.0).astype(BF16)
    t_from = lax.broadcasted_iota(jnp.int32, (n, n), 0)
    t_to = lax.broadcasted_iota(jnp.int32, (n, n), 1)
    before = jnp.where((t_from < t_to) & (t_from // MOE_TILE == t_to // MOE_TILE), 1.0, 0.0).astype(BF16)
    pos = jnp.dot(member, before, preferred_element_type=F32)
    p1 = jnp.sum(jnp.where(row == i1, pos, 0.0), axis=0, keepdims=True)
    p2 = jnp.sum(jnp.where(row == i2, pos, 0.0), axis=0, keepdims=True)
    out_row = lax.broadcasted_iota(jnp.int32, (8, n), 0)
    out = jnp.zeros((8, n), F32)
    for r, val in enumerate((i1.astype(F32), i2.astype(F32), p1, p2, w1 / den, w2 / den)):
        out = jnp.where(out_row == r, val, out)
    return out


def _oproj_kernel(o_ref, wo_ref, x_ref, g1_ref, gn_ref, sc_ref, sh_ref, rwt_ref, rb_ref,
                  xo_ref, h_ref, route_ref):
    n_e = rb_ref.shape[0]
    rw = rwt_ref[...]
    rows = o_ref.shape[0] // OPROJ_SPLIT
    groups = [slice(r * rows, (r + 1) * rows) for r in range(OPROJ_SPLIT)]
    ys = [jnp.dot(o_ref[g, :], wo_ref[...], preferred_element_type=F32) for g in groups]
    logits = []
    for g, y in zip(groups, ys):
        xn = x_ref[g, :] + g1_ref[...] * y
        xo_ref[g, :] = xn
        h = _norm_mod(xn, gn_ref[...], sc_ref[...], sh_ref[...])
        h_hi = h.astype(BF16)
        h_ref[g, :] = h_hi
        h_lo = (h - h_hi.astype(F32)).astype(BF16)
        main = lax.dot_general(rw, h_hi, _NT, preferred_element_type=F32)
        corr = lax.dot_general(rw[:n_e], h_lo, _NT, preferred_element_type=F32)
        logits.append(main[:n_e] + main[n_e:] + corr)
    route_ref[...] = _route(jnp.concatenate(logits, axis=1), rb_ref[...])


def _out_proj(o, wo_bf16, x2, g1, gn, sc, sh, rwt, rb, seq):
    t, d = x2.shape
    tm = OPROJ_ROW_TILE
    per_b = seq // tm
    n_e = rb.shape[0]
    vec = lambda i: (i // per_b, 0, 0)
    return pl.pallas_call(
        _oproj_kernel,
        out_shape=(jax.ShapeDtypeStruct((t, d), F32),
                   jax.ShapeDtypeStruct((t, d), BF16),
                   jax.ShapeDtypeStruct((8, t), F32)),
        grid=(t // tm,),
        in_specs=[
            pl.BlockSpec((tm, d), lambda i: (i, 0)),
            pl.BlockSpec((d, d), lambda i: (0, 0)),
            pl.BlockSpec((tm, d), lambda i: (i, 0)),
            pl.BlockSpec((None, 1, d), vec),
            pl.BlockSpec((1, d), lambda i: (0, 0)),
            pl.BlockSpec((None, 1, d), vec),
            pl.BlockSpec((None, 1, d), vec),
            pl.BlockSpec((2 * n_e, d), lambda i: (0, 0)),
            pl.BlockSpec((n_e, 1), lambda i: (0, 0)),
        ],
        out_specs=(pl.BlockSpec((tm, d), lambda i: (i, 0)),
                   pl.BlockSpec((tm, d), lambda i: (i, 0)),
                   pl.BlockSpec((8, tm), lambda i: (0, i))),
        compiler_params=_cparams(("parallel",)),
        name="out_proj_route",
    )(o, wo_bf16, x2, g1, gn, sc, sh, rwt, rb)


def _moe_plan(route, t, tm, n_e):
    n_tiles = t // tm
    ch, ft = MOE_CHUNK, MOE_FFN_TILE
    loc_rows, main_rows, n_sorted = _moe_rows(t, tm, n_e)
    ids = jnp.arange(n_e, dtype=jnp.int32)
    e = route[0:2].astype(jnp.int32)
    pos = route[2:4].astype(jnp.int32)
    oh = (e[:, :, None] == ids).astype(jnp.int32)
    cnt = oh.sum(0).reshape(n_tiles, tm, n_e).sum(1)
    seg = (cnt + ch - 1) // ch * ch
    loc = jnp.cumsum(seg, axis=1) - seg
    tot = seg.sum(0)
    totp = (tot + ft - 1) // ft * ft
    base = jnp.cumsum(totp) - totp
    gdest = base[None, :] + jnp.cumsum(seg, axis=0) - seg
    col = (oh * jnp.repeat(loc, tm, axis=0)[None]).sum(-1) + pos

    row0 = jnp.arange(loc_rows // ch, dtype=jnp.int32) * ch
    ej = (row0[None, :, None] >= (loc + seg)[:, None, :]).sum(-1)
    ohj = (jnp.minimum(ej, n_e - 1)[..., None] == ids).astype(jnp.int32)
    dst = (ohj * (gdest - loc)[:, None, :]).sum(-1) + row0[None, :]
    dump = main_rows + (jnp.arange(n_tiles, dtype=jnp.int32) % 2)[:, None] * loc_rows + row0[None, :]
    dst = jnp.where(ej < n_e, dst, dump) // ch

    r0 = jnp.arange(n_sorted // ft, dtype=jnp.int32) * ft
    ends = base + totp
    te = jnp.minimum((r0[:, None] >= ends[None, :]).sum(-1), n_e - 1)
    tvalid = (r0 < ends[-1]).astype(jnp.int32)
    tfirst = tvalid * (r0 == (((te[:, None] == ids) * base[None, :]).sum(-1))).astype(jnp.int32)
    return col, dst.astype(jnp.int32), te.astype(jnp.int32), tvalid, tfirst


def _moe_rows(t, tm, n_e):
    ch, ft = MOE_CHUNK, MOE_FFN_TILE
    loc_rows = 2 * tm + n_e * ch
    main = 2 * t + (t // tm) * n_e * (ch - 1) + n_e * (ft - 1)
    main = (main + ft - 1) // ft * ft
    return loc_rows, main, (main + 2 * loc_rows + ft - 1) // ft * ft


def _chunk_copies(dst_ref, tile, local_ref, sorted_ref, sem, to_sorted):
    ch = MOE_CHUNK
    copies = []
    for j in range(local_ref.shape[0] // ch):
        far = sorted_ref.at[pl.ds(pl.multiple_of(dst_ref[tile, j] * ch, ch), ch)]
        near = local_ref.at[pl.ds(j * ch, ch)]
        copies.append(pltpu.make_async_copy(near, far, sem) if to_sorted
                      else pltpu.make_async_copy(far, near, sem))
    return copies


def _dispatch_kernel(dst_ref, col_ref, h_ref, init_ref, xs_ref, comp, sem):
    del init_ref
    tile = pl.program_id(0)
    last = pl.num_programs(0) - 1
    slot = tile % 2

    def wait_slot(s):
        pltpu.make_async_copy(comp.at[s], xs_ref.at[pl.ds(0, comp.shape[1])], sem.at[s]).wait()

    @pl.when(tile >= 2)
    def _():
        wait_slot(slot)

    col = col_ref[...]
    srow = lax.broadcasted_iota(jnp.int32, (comp.shape[1], col.shape[1]), 0)
    sel = jnp.where((srow == col[0:1]) | (srow == col[1:2]), 1.0, 0.0).astype(BF16)
    comp[slot] = jnp.dot(sel, h_ref[...], preferred_element_type=F32).astype(BF16)
    for cp in _chunk_copies(dst_ref, tile, comp.at[slot], xs_ref, sem.at[slot], True):
        cp.start()

    @pl.when(tile == last)
    def _():
        wait_slot(slot)

    @pl.when((tile == last) & (tile >= 1))
    def _():
        wait_slot(1 - slot)


def _dispatch(dst, col, h, n_e, init):
    t, d = h.shape
    tm = MOE_TILE
    loc_rows, _, n_sorted = _moe_rows(t, tm, n_e)
    if init is None:
        init = jnp.zeros((n_sorted, d), BF16)
    return pl.pallas_call(
        _dispatch_kernel,
        out_shape=jax.ShapeDtypeStruct((n_sorted, d), BF16),
        grid_spec=pltpu.PrefetchScalarGridSpec(
            num_scalar_prefetch=1,
            grid=(t // tm,),
            in_specs=[pl.BlockSpec((2, tm), lambda i, dst: (0, i)),
                      pl.BlockSpec((tm, d), lambda i, dst: (i, 0)),
                      pl.BlockSpec(memory_space=pl.ANY)],
            out_specs=pl.BlockSpec(memory_space=pl.ANY),
            scratch_shapes=[pltpu.VMEM((2, loc_rows, d), BF16), pltpu.SemaphoreType.DMA((2,))],
        ),
        input_output_aliases={3: 0},
        compiler_params=_cparams(("arbitrary",)),
        name="moe_dispatch",
    )(dst, col, h, init)


def _expert_kernel(te_ref, tv_ref, tf_ref, xs_ref, wg_ref, wu_ref, wd_ref, ys_ref, wg_b, wu_b, wd_b):
    del te_ref
    s = pl.program_id(0)

    @pl.when(tf_ref[s] == 1)
    def _():
        wg_b[...] = wg_ref[...].astype(BF16)
        wu_b[...] = wu_ref[...].astype(BF16)
        wd_b[...] = wd_ref[...].astype(BF16)

    @pl.when(tv_ref[s] == 1)
    def _():
        rows = ys_ref.shape[0] // MOE_FFN_SPLIT
        hs = []
        for r in range(MOE_FFN_SPLIT):
            xs = xs_ref[r * rows:(r + 1) * rows, :]
            hg = jnp.dot(xs, wg_b[...], preferred_element_type=F32)
            hu = jnp.dot(xs, wu_b[...], preferred_element_type=F32)
            hs.append((hg, hu))
        for r, (hg, hu) in enumerate(hs):
            he = hg * (1.0 / (1.0 + jnp.exp(-hg))) * hu
            ys_ref[r * rows:(r + 1) * rows, :] = jnp.dot(
                he.astype(BF16), wd_b[...], preferred_element_type=F32).astype(BF16)

    @pl.when(tv_ref[s] == 0)
    def _():
        ys_ref[...] = jnp.zeros(ys_ref.shape, ys_ref.dtype)


def _experts(te, tvalid, tfirst, xs, wg, wu, wd, layer):
    n_sorted, d = xs.shape
    f = wg.shape[3]
    ft = MOE_FFN_TILE
    wspec = lambda r, c: pl.BlockSpec((None, None, r, c), lambda s, te, tv, tf: (layer, te[s], 0, 0))
    return pl.pallas_call(
        _expert_kernel,
        out_shape=jax.ShapeDtypeStruct((n_sorted, d), BF16),
        grid_spec=pltpu.PrefetchScalarGridSpec(
            num_scalar_prefetch=3,
            grid=(n_sorted // ft,),
            in_specs=[pl.BlockSpec((ft, d), lambda s, te, tv, tf: (s, 0)),
                      wspec(d, f), wspec(d, f), wspec(f, d)],
            out_specs=pl.BlockSpec((ft, d), lambda s, te, tv, tf: (s, 0)),
            scratch_shapes=[pltpu.VMEM((d, f), BF16), pltpu.VMEM((d, f), BF16), pltpu.VMEM((f, d), BF16)],
        ),
        compiler_params=_cparams(("arbitrary",)),
        name="moe_experts",
    )(te, tvalid, tfirst, xs, wg, wu, wd)


def _combine_kernel(dst_ref, col_ref, w_ref, ys_ref, x_ref, g2_ref, o_ref, comp, sem):
    tile = pl.program_id(0)
    slot = tile % 2

    def fetch(t, s):
        for cp in _chunk_copies(dst_ref, t, comp.at[s], ys_ref, sem.at[s], False):
            cp.start()

    def wait_slot(s):
        pltpu.make_async_copy(ys_ref.at[pl.ds(0, comp.shape[1])], comp.at[s], sem.at[s]).wait()

    last = pl.num_programs(0) - 1

    @pl.when(tile == 0)
    def _():
        fetch(tile, slot)

    fetch(jnp.minimum(tile + 1, last), 1 - slot)

    wait_slot(slot)
    rows = x_ref.shape[0] // COMBINE_SPLIT
    for r in range(COMBINE_SPLIT):
        grp = slice(r * rows, (r + 1) * rows)
        col = col_ref[grp, :]
        w = w_ref[grp, :]
        scol = lax.broadcasted_iota(jnp.int32, (rows, comp.shape[1]), 1)
        selw = (jnp.where(scol == col[:, 0:1], w[:, 4:5], 0.0)
                + jnp.where(scol == col[:, 1:2], w[:, 5:6], 0.0)).astype(BF16)
        acc = jnp.dot(selw, comp[slot], preferred_element_type=F32)
        o_ref[grp, :] = x_ref[grp, :] + g2_ref[...] * acc

    @pl.when(tile == last)
    def _():
        wait_slot(1 - slot)


def _combine(dst, col_t, route_t, ys, x2, g2, seq, n_e):
    t, d = x2.shape
    tm = MOE_TILE
    per_b = seq // tm
    loc_rows = _moe_rows(t, tm, n_e)[0]
    return pl.pallas_call(
        _combine_kernel,
        out_shape=jax.ShapeDtypeStruct((t, d), F32),
        grid_spec=pltpu.PrefetchScalarGridSpec(
            num_scalar_prefetch=1,
            grid=(t // tm,),
            in_specs=[pl.BlockSpec((tm, 2), lambda i, dst: (i, 0)),
                      pl.BlockSpec((tm, 8), lambda i, dst: (i, 0)),
                      pl.BlockSpec(memory_space=pl.ANY),
                      pl.BlockSpec((tm, d), lambda i, dst: (i, 0)),
                      pl.BlockSpec((None, 1, d), lambda i, dst: (i // per_b, 0, 0))],
            out_specs=pl.BlockSpec((tm, d), lambda i, dst: (i, 0)),
            scratch_shapes=[pltpu.VMEM((2, loc_rows, d), BF16), pltpu.SemaphoreType.DMA((2,))],
        ),
        compiler_params=_cparams(("arbitrary",)),
        name="moe_combine",
    )(dst, col_t, route_t, ys, x2, g2)


def _moe(h, route, wg, wu, wd, layer, x2, g2, seq, sorted_init):
    t = x2.shape[0]
    n_e = wg.shape[1]
    col, dst, te, tvalid, tfirst = _moe_plan(route, t, MOE_TILE, n_e)
    xs = _dispatch(dst, col, h, n_e, sorted_init)
    ys = _experts(te, tvalid, tfirst, xs, wg, wu, wd, layer)
    return _combine(dst, col.T, route.T, ys, x2, g2, seq, n_e), ys


def kernel(x, c, ada_w, ada_b, norm_mix_g, norm_ffn_g, t5_bias, a_w_qkv, a_q_gain, a_k_gain, a_lambda, a_subln_g, a_w_o, b_w_qkv, b_q_gain, b_k_gain, b_rel_bias, b_w_o, router_w, router_bias, moe_w_gate, moe_w_up, moe_w_down):
    batch, seq, d = x.shape
    depth = ada_w.shape[0]
    assert seq % ATTN_TILE == 0 and seq % ROW_TILE == 0 and d == A_HEADS * 2 * A_HEAD_DIM
    assert d == B_HEADS * B_HEAD_DIM and A_HEAD_DIM == B_HEAD_DIM

    c_pad = jnp.zeros((8, d), F32).at[:batch].set(c.astype(F32))
    mod = _modulation(c_pad, ada_w.astype(F32), ada_b.astype(F32))[:, :batch]
    mod = mod.reshape(depth, batch, 6, 1, d)

    rwt = router_w.astype(F32).T
    rwt_hi = lax.bitcast_convert_type(
        lax.bitcast_convert_type(rwt, jnp.uint32) & jnp.uint32(0xFFFF0000), F32)
    rwt = jnp.concatenate([rwt_hi, rwt - rwt_hi], axis=0).astype(BF16)
    rb = router_bias.astype(F32).reshape(-1, 1)

    x2 = x.astype(F32).reshape(batch * seq, d)
    sorted_buf = None
    for i in range(depth):
        sh1, sc1, g1, sh2, sc2, g2 = [mod[i, :, k] for k in range(6)]
        j = i // 2
        if i % 2 == 0:
            w_qkv, qg, kg, w_o = a_w_qkv[j], a_q_gain[j], a_k_gain[j], a_w_o[j]
        else:
            w_qkv, qg, kg, w_o = b_w_qkv[j], b_q_gain[j], b_k_gain[j], b_w_o[j]
        n_rep = d // qg.shape[0]
        gq = (jnp.tile(qg.astype(F32), n_rep) * (A_HEAD_DIM ** -0.5 * LOG2E)).reshape(d, 1)
        gk = jnp.tile(kg.astype(F32), n_rep).reshape(1, d)
        qt, k, vt = _qkv_proj(x2, norm_mix_g[i].reshape(1, d), sc1, sh1, w_qkv.astype(BF16),
                              gq, gk, seq, A_HEAD_DIM)
        if i % 2 == 0:
            lambda_init = 0.8 - 0.6 * math.exp(-0.3 * i)
            o = _attention_a(qt, k, vt, _t5_vectors(t5_bias, ATTN_TILE), a_lambda[j].astype(F32),
                             a_subln_g[j].reshape(1, -1), batch, seq, lambda_init)
        else:
            o = _attention_b(qt, k, vt, _band_vectors(b_rel_bias[j], BAND_TILE), batch, seq)
        x2, h, route = _out_proj(o, w_o.astype(BF16), x2, g1, norm_ffn_g[i].reshape(1, d),
                                 sc2, sh2, rwt, rb, seq)
        x2, sorted_buf = _moe(h, route, moe_w_gate, moe_w_up, moe_w_down, i, x2, g2, seq, sorted_buf)
    return x2.reshape(batch, seq, d)
```

```python
import functools
import math

import numpy as np
import jax
import jax.numpy as jnp
from jax import lax
from jax.experimental import pallas as pl
from jax.experimental.pallas import tpu as pltpu

F32 = jnp.float32
BF16 = jnp.bfloat16

CHUNK = 64
A_HEADS = 8
A_HEAD_DIM = 64
T5_BUCKETS = 32
T5_MAX_DIST = 1024
B_HEADS = 16
B_HEAD_DIM = 64
LEFT_CHUNKS = 8
MAX_REL = 256
N_EXPERTS = 16
N_GROUPS = 4
E_PER_GROUP = N_EXPERTS // N_GROUPS
NORM_EPS = 1e-6
NEG_INF = -1e30
LOG2E = math.log2(math.e)

V7X_LANES = 128
V7X_MXU_DIM = 256

ATTN_TILE = 512
MOD_COL_TILE = 1536
FAR_UNROLL = 4
BAND_TILE = 256
BAND_QBLOCKS = 16
BAND_AHEAD = 2
ROW_TILE = 512
QKV_SPLIT = 2
MOE_TILE = 512
OPROJ_ROW_TILE = 1024
OPROJ_SPLIT = 4
MOE_CHUNK = 16
MOE_FFN_TILE = 512
MOE_FFN_SPLIT = 2
VMEM_LIMIT = 56 * 1024 * 1024

_NT = (((1,), (1,)), ((), ()))


def _cparams(sem):
    return pltpu.CompilerParams(dimension_semantics=sem, vmem_limit_bytes=VMEM_LIMIT)


def _mod_kernel(c_ref, w_ref, b_ref, o_ref):
    c = c_ref[...]
    s = c * (1.0 / (1.0 + jnp.exp(-c)))
    o_ref[...] = jnp.dot(s, w_ref[...], preferred_element_type=F32,
                         precision=lax.Precision.HIGHEST) + b_ref[...]


def _modulation(c_pad, ada_w, ada_b):
    depth, d, n = ada_w.shape
    rows = c_pad.shape[0]
    tn = MOD_COL_TILE
    return pl.pallas_call(
        _mod_kernel,
        out_shape=jax.ShapeDtypeStruct((depth, rows, n), F32),
        grid=(depth, n // tn),
        in_specs=[
            pl.BlockSpec((rows, d), lambda i, j: (0, 0)),
            pl.BlockSpec((None, d, tn), lambda i, j: (i, 0, j)),
            pl.BlockSpec((None, 1, tn), lambda i, j: (i, 0, j)),
        ],
        out_specs=pl.BlockSpec((None, rows, tn), lambda i, j: (i, 0, j)),
        compiler_params=_cparams(("parallel", "parallel")),
        name="adaln_mod",
    )(c_pad, ada_w, ada_b.reshape(depth, 1, n))


def _norm_mod(x, g, sc, sh):
    ms = jnp.mean(x * x, axis=-1, keepdims=True)
    return x * lax.rsqrt(ms + NORM_EPS) * g * (1.0 + sc) + sh


def _qkv_kernel(x_ref, g_ref, sc_ref, sh_ref, w_ref, gq_ref, gk_ref,
                qt_ref, k_ref, vt_ref, wqt_ref, wvt_ref, *, head_dim):
    d, tm = qt_ref.shape

    @pl.when(pl.program_id(0) == 0)
    def _():
        wqt_ref[...] = w_ref[:, :d].T
        wvt_ref[...] = w_ref[:, 2 * d:].T

    cw = V7X_MXU_DIM
    r = lax.broadcasted_iota(jnp.int32, (cw, cw), 0) // head_dim
    c = lax.broadcasted_iota(jnp.int32, (cw, cw), 1) // head_dim
    gmat = jnp.where(r == c, 1.0 / head_dim, 0.0).astype(BF16)
    tg = tm // QKV_SPLIT
    for gi in range(QKV_SPLIT):
        rows = slice(gi * tg, (gi + 1) * tg)
        h = _norm_mod(x_ref[rows, :], g_ref[...], sc_ref[...], sh_ref[...]).astype(BF16)
        y = lax.dot_general(wqt_ref[...], h, _NT, preferred_element_type=F32)
        y3 = y.reshape(d // head_dim, head_dim, tg)
        ss = jnp.mean(y3 * y3, axis=1, keepdims=True)
        qt_ref[:, rows] = ((y3 * lax.rsqrt(ss + NORM_EPS)).reshape(d, tg) * gq_ref[...]).astype(BF16)
        y = jnp.dot(h, w_ref[:, d:2 * d], preferred_element_type=F32)
        ysq = (y * y).astype(BF16)
        for ci in range(d // cw):
            cols = slice(ci * cw, (ci + 1) * cw)
            ss = jnp.dot(ysq[:, cols], gmat, preferred_element_type=F32)
            k_ref[rows, cols] = (y[:, cols] * lax.rsqrt(ss + NORM_EPS) * gk_ref[:, cols]).astype(BF16)
        vt_ref[:, rows] = lax.dot_general(wvt_ref[...], h, _NT, preferred_element_type=F32).astype(BF16)


def _qkv_proj(x2, g, sc, sh, w, gq, gk, seq, head_dim):
    t, d = x2.shape
    tm = ROW_TILE * QKV_SPLIT
    per_b = seq // tm
    vec = lambda i: (i // per_b, 0, 0)
    full = lambda i: (0, 0)
    return pl.pallas_call(
        functools.partial(_qkv_kernel, head_dim=head_dim),
        out_shape=(jax.ShapeDtypeStruct((d, t), BF16),
                   jax.ShapeDtypeStruct((t, d), BF16),
                   jax.ShapeDtypeStruct((d, t), BF16)),
        grid=(t // tm,),
        in_specs=[
            pl.BlockSpec((tm, d), lambda i: (i, 0)),
            pl.BlockSpec((1, d), full),
            pl.BlockSpec((None, 1, d), vec),
            pl.BlockSpec((None, 1, d), vec),
            pl.BlockSpec((d, 3 * d), full),
            pl.BlockSpec((d, 1), full),
            pl.BlockSpec((1, d), full),
        ],
        out_specs=(pl.BlockSpec((d, tm), lambda i: (0, i)),
                   pl.BlockSpec((tm, d), lambda i: (i, 0)),
                   pl.BlockSpec((d, tm), lambda i: (0, i))),
        scratch_shapes=[pltpu.VMEM((d, d), BF16), pltpu.VMEM((d, d), BF16)],
        compiler_params=_cparams(("arbitrary",)),
        name="qkv_proj",
    )(x2, g, sc, sh, w, gq, gk)


def _softmax_pv(s, vt1, m_ref, acc_ref):
    m_prev = m_ref[...]
    m_new = jnp.maximum(m_prev, jnp.max(s, axis=0, keepdims=True))
    alpha = jnp.exp2(m_prev - m_new)
    p = jnp.exp2(s - m_new).astype(BF16)
    acc_ref[...] = alpha * acc_ref[...] + jnp.dot(vt1, p, preferred_element_type=F32)
    m_ref[...] = m_new


def _split_maps(qt, head_dim):
    row = lax.broadcasted_iota(jnp.int32, qt.shape, 0)
    zero = jnp.zeros_like(qt)
    return jnp.where(row < head_dim, qt, zero), jnp.where(row >= head_dim, qt, zero)


ONES_ROWS = 16


def _init_state(refs):
    for m_ref, acc_ref in refs:
        m_ref[...] = jnp.full(m_ref.shape, NEG_INF, F32)
        acc_ref[...] = jnp.zeros(acc_ref.shape, F32)


def _normalized(acc_ref):
    acc = acc_ref[...]
    return acc[:V7X_LANES] / acc[V7X_LANES:V7X_LANES + 1]


def _toeplitz(x_row, n):
    x = jnp.broadcast_to(x_row, (n, x_row.shape[1]))
    return pltpu.roll(x, 0, 1, stride=1, stride_axis=0)[:, :n]


def _chunk_delta(n):
    kc = lax.broadcasted_iota(jnp.int32, (n, n), 0) // CHUNK
    qc = lax.broadcasted_iota(jnp.int32, (n, n), 1) // CHUNK
    return qc - kc


def _attn_a_kernel(qt_ref, qn_ref, k_ref, vt_ref, x_ref, lam_ref, sub_ref, o_ref,
                   tab, fbuf, nbuf, m_st, acc_st, *, n_off, lambda_init):
    tq = ATTN_TILE
    step = pl.program_id(2)

    @pl.when(step == 0)
    def _():
        for o in range(n_off):
            t = _toeplitz(x_ref[o], tq) * LOG2E
            if o == 0:
                t = jnp.where(_chunk_delta(tq) >= 0, t, NEG_INF)
            tab[o] = t

    for c in range(2):
        _init_state(((m_st.at[c, 0], acc_st.at[c, 0]), (m_st.at[c, 1], acc_st.at[c, 1])))
    q_maps = [_split_maps(qt_ref[:, c * tq:(c + 1) * tq], A_HEAD_DIM) for c in range(2)]

    def k_tile(j):
        return k_ref[pl.ds(pl.multiple_of(j * tq, tq), tq), :]

    def vt1_tile(j):
        vt = vt_ref[:, pl.ds(pl.multiple_of(j * tq, tq), tq)]
        return jnp.concatenate([vt, jnp.ones((ONES_ROWS, tq), vt.dtype)], axis=0)

    def scores(buf, k, c):
        for m in range(2):
            buf[c, m] = jnp.dot(k, q_maps[c][m], preferred_element_type=F32)

    def consume(buf, vt1, c, o):
        for m in range(2):
            s = buf[c, m]
            _softmax_pv(s if o >= n_off else s + tab[o], vt1, m_st.at[c, m], acc_st.at[c, m])

    def near_block(first_tile, n_tiles, first_scored):
        units = [[c for c in range(2) if n_tiles - 2 + c - s >= 0] for s in range(n_tiles)]
        buf_of = lambda s: fbuf.at[0] if s == 0 else nbuf.at[s - 1]

        for s in range(1 if first_scored else 0, n_tiles):
            k = k_tile(first_tile + s)
            for c in units[s]:
                scores(buf_of(s), k, c)
        for s in range(n_tiles):
            vt1 = vt1_tile(first_tile + s)
            for c in units[s]:
                consume(buf_of(s), vt1, c, n_tiles - 2 + c - s)

    first_general = (n_off - 1) // 2
    for i0 in range(first_general):
        @pl.when(step == i0)
        def _():
            near_block(0, 2 * i0 + 2, False)

    def far_step(j, parity):
        k = k_tile(j + 1)
        for c in range(2):
            scores(fbuf.at[1 - parity], k, c)
        vt1 = vt1_tile(j)
        for c in range(2):
            consume(fbuf.at[parity], vt1, c, n_off)

    n_far = jnp.maximum(2 * step - (n_off - 1), 0)
    rem = n_far % FAR_UNROLL

    @pl.when(rem >= 2)
    def _():
        far_step(0, 0)
        far_step(1, 1)

    def far_trip(t, carry):
        for u in range(FAR_UNROLL):
            far_step(FAR_UNROLL * t + rem + u, u % 2)
        return carry
    lax.fori_loop(0, n_far // FAR_UNROLL, far_trip, 0)

    @pl.when(step >= first_general)
    def _():
        near_block(n_far, n_off + 1, True)

    def epilogue():
        lam = lam_ref[...]
        lam_full = (jnp.exp(jnp.sum(lam[0:1] * lam[1:2], axis=-1, keepdims=True))
                    - jnp.exp(jnp.sum(lam[2:3] * lam[3:4], axis=-1, keepdims=True)) + lambda_init)
        for c in range(2):
            a = _normalized(acc_st.at[c, 0]) - lam_full * _normalized(acc_st.at[c, 1])
            ms = jnp.mean(a * a, axis=0, keepdims=True)
            an = a * lax.rsqrt(ms + NORM_EPS)
            o_ref[c * tq:(c + 1) * tq, :] = (an.T * (sub_ref[...] * (1.0 - lambda_init))).astype(BF16)

    last = pl.num_programs(2) - 1

    @pl.when(step < last)
    def _():
        k = k_tile(0)
        for c in range(2):
            qa, qb = _split_maps(qn_ref[:, c * tq:(c + 1) * tq], A_HEAD_DIM)
            fbuf[0, c, 0] = jnp.dot(k, qa, preferred_element_type=F32)
            fbuf[0, c, 1] = jnp.dot(k, qb, preferred_element_type=F32)
        epilogue()

    @pl.when(step == last)
    def _():
        epilogue()


def _attn_b_kernel(qt_ref, k_ref, vt_ref, x_ref, o_ref, tab, sbuf):
    tq = BAND_TILE
    n_blk = qt_ref.shape[1] // tq
    n_off = tab.shape[0] // 2
    step = pl.program_id(2)

    @pl.when(step == 0)
    def _():
        for m in range(2):
            for o in range(n_off):
                d = _chunk_delta(tq) + o * (tq // CHUNK)
                t = _toeplitz(x_ref[n_off * m + o], tq) * LOG2E
                tab[n_off * m + o] = jnp.where((d >= 0) & (d <= LEFT_CHUNKS), t, NEG_INF)

    q_maps = [_split_maps(qt_ref[:, c * tq:(c + 1) * tq], B_HEAD_DIM) for c in range(n_blk)]

    def run(first_step):
        pairs = [(c, o) for c in range(n_blk) for o in range(n_off - 1, -1, -1)
                 if not first_step or c - o >= 0]
        k_tiles, vt_tiles = {}, {}
        for c, o in pairs:
            if c - o not in k_tiles:
                ks = pl.multiple_of((n_blk * step + c - o) * tq, tq)
                k_tiles[c - o] = k_ref[pl.ds(ks, tq), :]
                vt = vt_ref[:, pl.ds(ks, tq)]
                vt_tiles[c - o] = jnp.concatenate([vt, jnp.ones((ONES_ROWS, tq), vt.dtype)], axis=0)
        def tile_scores(c):
            for o in [o for cc, o in pairs if cc == c]:
                for m in range(2):
                    sbuf[n_off * c + o, m] = jnp.dot(k_tiles[c - o], q_maps[c][m], preferred_element_type=F32)

        for c in range(min(BAND_AHEAD, n_blk)):
            tile_scores(c)
        for c in range(n_blk):
            if c + BAND_AHEAD < n_blk:
                tile_scores(c + BAND_AHEAD)
            outs = []
            for m in range(2):
                offs = [o for cc, o in pairs if cc == c]
                ss = [sbuf[n_off * c + o, m] + tab[n_off * m + o] for o in offs]
                mx = functools.reduce(jnp.maximum, [jnp.max(s, axis=0, keepdims=True) for s in ss])
                acc = sum(jnp.dot(vt_tiles[c - o], jnp.exp2(s - mx).astype(BF16), preferred_element_type=F32)
                          for o, s in zip(offs, ss))
                outs.append(acc[:V7X_LANES] / acc[V7X_LANES:V7X_LANES + 1])
            row = lax.broadcasted_iota(jnp.int32, outs[0].shape, 0)
            o_ref[c * tq:(c + 1) * tq, :] = jnp.where(row < B_HEAD_DIM, outs[0], outs[1]).T.astype(BF16)

    @pl.when(step == 0)
    def _():
        run(True)

    @pl.when(step >= 1)
    def _():
        run(False)


def _attn_specs(batch, seq, nq, tq):
    return dict(
        q=pl.BlockSpec((V7X_LANES, tq), lambda b, h, i: (h, b * nq + i)),
        k=pl.BlockSpec((seq, V7X_LANES), lambda b, h, i: (b, h)),
        v=pl.BlockSpec((V7X_LANES, seq), lambda b, h, i: (h, b)),
        o=pl.BlockSpec((tq, V7X_LANES), lambda b, h, i: (b * nq + i, h)),
    )


def _attention_a(qt, k, vt, xvec, lam, sub_g, batch, seq, lambda_init):
    d, t = qt.shape
    tq = ATTN_TILE
    n_off = xvec.shape[1]
    assert n_off % 2 == 1 and seq % (2 * tq) == 0
    nq = seq // (2 * tq)
    sp = _attn_specs(batch, seq, nq, 2 * tq)
    rows = V7X_LANES + ONES_ROWS
    return pl.pallas_call(
        functools.partial(_attn_a_kernel, n_off=n_off, lambda_init=lambda_init),
        out_shape=jax.ShapeDtypeStruct((t, d), BF16),
        grid=(batch, d // V7X_LANES, nq),
        in_specs=[sp["q"],
                  pl.BlockSpec((V7X_LANES, 2 * tq), lambda b, h, i: (h, b * nq + jnp.minimum(i + 1, nq - 1))),
                  sp["k"], sp["v"],
                  pl.BlockSpec((None,) + xvec.shape[1:], lambda b, h, i: (h, 0, 0, 0)),
                  pl.BlockSpec(lam.shape, lambda b, h, i: (0, 0)),
                  pl.BlockSpec(sub_g.shape, lambda b, h, i: (0, 0))],
        out_specs=sp["o"],
        scratch_shapes=[pltpu.VMEM((n_off, tq, tq), F32), pltpu.VMEM((2, 2, 2, tq, tq), F32),
                        pltpu.VMEM((n_off, 2, 2, tq, tq), F32),
                        pltpu.VMEM((2, 2, 1, tq), F32), pltpu.VMEM((2, 2, rows, tq), F32)],
        compiler_params=_cparams(("parallel", "parallel", "arbitrary")),
        name="diff_attention",
    )(qt, qt, k, vt, xvec, lam, sub_g)


def _attention_b(qt, k, vt, xvec, batch, seq):
    d, t = qt.shape
    tq = BAND_TILE
    g = BAND_QBLOCKS
    n_off = xvec.shape[1] // 2
    assert g >= n_off - 1 and seq % (g * tq) == 0
    nq = seq // (g * tq)
    sp = _attn_specs(batch, seq, nq, g * tq)
    return pl.pallas_call(
        _attn_b_kernel,
        out_shape=jax.ShapeDtypeStruct((t, d), BF16),
        grid=(batch, d // V7X_LANES, nq),
        in_specs=[sp["q"], sp["k"], sp["v"],
                  pl.BlockSpec((None,) + xvec.shape[1:], lambda b, h, i: (h, 0, 0, 0))],
        out_specs=sp["o"],
        scratch_shapes=[pltpu.VMEM((2 * n_off, tq, tq), F32), pltpu.VMEM((n_off * g, 2, tq, tq), F32)],
        compiler_params=_cparams(("parallel", "parallel", "arbitrary")),
        name="chunk_attention",
    )(qt, k, vt, xvec)


def _t5_bucket(rel):
    nb = T5_BUCKETS // 2
    ret = jnp.where(rel > 0, nb, 0)
    n = jnp.abs(rel)
    max_exact = nb // 2
    nf = jnp.maximum(n, 1).astype(F32)
    large = max_exact + (jnp.log(nf / max_exact) / math.log(T5_MAX_DIST / max_exact)
                         * (nb - max_exact)).astype(jnp.int32)
    large = jnp.minimum(large, nb - 1)
    return ret + jnp.where(n < max_exact, n, large)


def _t5_const_distance():
    nb = T5_BUCKETS // 2
    max_exact = nb // 2
    n = np.arange(max_exact, 4 * T5_MAX_DIST, dtype=np.float64)
    large = max_exact + np.floor(np.log(n / max_exact) / math.log(T5_MAX_DIST / max_exact) * (nb - max_exact))
    below = np.nonzero(large < nb - 1)[0]
    return int(n[below[-1]]) + 2


def _tile_rel(tile, n_off):
    i = jnp.arange(2 * tile, dtype=jnp.int32)
    rel = jnp.where(i < tile, -i, 2 * tile - i)
    return rel[None, :] - tile * jnp.arange(n_off, dtype=jnp.int32)[:, None]


def _t5_vectors(t5_bias, tile):
    n_off = 1
    while (n_off - 1) * tile + 1 < _t5_const_distance():
        n_off += 1
    tb = t5_bias.astype(F32)
    vals = tb[_t5_bucket(_tile_rel(tile, n_off))] - tb[T5_BUCKETS // 2 - 1]
    return vals.transpose(2, 0, 1)[:, :, None, :]


def _band_vectors(rel_bias, tile):
    n_off = LEFT_CHUNKS * CHUNK // tile + 1
    idx = jnp.clip(_tile_rel(tile, n_off), -MAX_REL, MAX_REL) + MAX_REL
    vals = rel_bias.astype(F32)[:, idx]
    return vals.reshape(rel_bias.shape[0] // 2, 2 * n_off, 1, 2 * tile)


def _top2_sum4(r0, r1, r2, r3):
    a, b = jnp.maximum(r0, r1), jnp.minimum(r0, r1)
    c, d = jnp.maximum(r2, r3), jnp.minimum(r2, r3)
    return jnp.maximum(a, c) + jnp.maximum(jnp.minimum(a, c), jnp.maximum(b, d))


def _route(logits, rbias):
    n_e, n = logits.shape
    scores = 1.0 / (1.0 + jnp.exp(-logits))
    sel = scores + rbias
    row = lax.broadcasted_iota(jnp.int32, sel.shape, 0)
    best = None
    for g in range(N_GROUPS):
        rows = [sel[g * E_PER_GROUP + i: g * E_PER_GROUP + i + 1, :] for i in range(E_PER_GROUP)]
        gs = _top2_sum4(*rows)
        if best is None:
            best, gidx = gs, jnp.zeros(gs.shape, jnp.int32)
        else:
            gidx = jnp.where(gs > best, g, gidx)
            best = jnp.maximum(best, gs)
    masked = jnp.where(row // E_PER_GROUP == gidx, sel, NEG_INF)
    m1 = jnp.max(masked, axis=0, keepdims=True)
    i1 = jnp.min(jnp.where(masked == m1, row, n_e), axis=0, keepdims=True)
    masked2 = jnp.where(row == i1, -3.0e38, masked)
    m2 = jnp.max(masked2, axis=0, keepdims=True)
    i2 = jnp.min(jnp.where(masked2 == m2, row, n_e), axis=0, keepdims=True)
    w1 = jnp.sum(jnp.where(row == i1, scores, 0.0), axis=0, keepdims=True)
    w2 = jnp.sum(jnp.where(row == i2, scores, 0.0), axis=0, keepdims=True)
    den = w1 + w2
    member = jnp.where((row == i1) | (row == i2), 1.0, 0.0).astype(BF16)
    t_from = lax.broadcasted_iota(jnp.int32, (n, n), 0)
    t_to = lax.broadcasted_iota(jnp.int32, (n, n), 1)
    before = jnp.where((t_from < t_to) & (t_from // MOE_TILE == t_to // MOE_TILE), 1.0, 0.0).astype(BF16)
    pos = jnp.dot(member, before, preferred_element_type=F32)
    p1 = jnp.sum(jnp.where(row == i1, pos, 0.0), axis=0, keepdims=True)
    p2 = jnp.sum(jnp.where(row == i2, pos, 0.0), axis=0, keepdims=True)
    out_row = lax.broadcasted_iota(jnp.int32, (8, n), 0)
    out = jnp.zeros((8, n), F32)
    for r, val in enumerate((i1.astype(F32), i2.astype(F32), p1, p2, w1 / den, w2 / den)):
        out = jnp.where(out_row == r, val, out)
    return out


def _oproj_kernel(o_ref, wo_ref, x_ref, g1_ref, gn_ref, sc_ref, sh_ref, rwt_ref, rb_ref,
                  xo_ref, h_ref, route_ref):
    n_e = rb_ref.shape[0]
    rw = rwt_ref[...]
    rows = o_ref.shape[0] // OPROJ_SPLIT
    groups = [slice(r * rows, (r + 1) * rows) for r in range(OPROJ_SPLIT)]
    ys = [jnp.dot(o_ref[g, :], wo_ref[...], preferred_element_type=F32) for g in groups]
    logits = []
    for g, y in zip(groups, ys):
        xn = x_ref[g, :] + g1_ref[...] * y
        xo_ref[g, :] = xn
        h = _norm_mod(xn, gn_ref[...], sc_ref[...], sh_ref[...])
        h_hi = h.astype(BF16)
        h_ref[g, :] = h_hi
        h_lo = (h - h_hi.astype(F32)).astype(BF16)
        main = lax.dot_general(rw, h_hi, _NT, preferred_element_type=F32)
        corr = lax.dot_general(rw[:n_e], h_lo, _NT, preferred_element_type=F32)
        logits.append(main[:n_e] + main[n_e:] + corr)
    route_ref[...] = _route(jnp.concatenate(logits, axis=1), rb_ref[...])


def _out_proj(o, wo_bf16, x2, g1, gn, sc, sh, rwt, rb, seq):
    t, d = x2.shape
    tm = OPROJ_ROW_TILE
    per_b = seq // tm
    n_e = rb.shape[0]
    vec = lambda i: (i // per_b, 0, 0)
    return pl.pallas_call(
        _oproj_kernel,
        out_shape=(jax.ShapeDtypeStruct((t, d), F32),
                   jax.ShapeDtypeStruct((t, d), BF16),
                   jax.ShapeDtypeStruct((8, t), F32)),
        grid=(t // tm,),
        in_specs=[
            pl.BlockSpec((tm, d), lambda i: (i, 0)),
            pl.BlockSpec((d, d), lambda i: (0, 0)),
            pl.BlockSpec((tm, d), lambda i: (i, 0)),
            pl.BlockSpec((None, 1, d), vec),
            pl.BlockSpec((1, d), lambda i: (0, 0)),
            pl.BlockSpec((None, 1, d), vec),
            pl.BlockSpec((None, 1, d), vec),
            pl.BlockSpec((2 * n_e, d), lambda i: (0, 0)),
            pl.BlockSpec((n_e, 1), lambda i: (0, 0)),
        ],
        out_specs=(pl.BlockSpec((tm, d), lambda i: (i, 0)),
                   pl.BlockSpec((tm, d), lambda i: (i, 0)),
                   pl.BlockSpec((8, tm), lambda i: (0, i))),
        compiler_params=_cparams(("parallel",)),
        name="out_proj_route",
    )(o, wo_bf16, x2, g1, gn, sc, sh, rwt, rb)


def _moe_plan(route, t, tm, n_e):
    n_tiles = t // tm
    ch, ft = MOE_CHUNK, MOE_FFN_TILE
    loc_rows, main_rows, n_sorted = _moe_rows(t, tm, n_e)
    ids = jnp.arange(n_e, dtype=jnp.int32)
    e = route[0:2].astype(jnp.int32)
    pos = route[2:4].astype(jnp.int32)
    oh = (e[:, :, None] == ids).astype(jnp.int32)
    cnt = oh.sum(0).reshape(n_tiles, tm, n_e).sum(1)
    seg = (cnt + ch - 1) // ch * ch
    loc = jnp.cumsum(seg, axis=1) - seg
    tot = seg.sum(0)
    totp = (tot + ft - 1) // ft * ft
    base = jnp.cumsum(totp) - totp
    gdest = base[None, :] + jnp.cumsum(seg, axis=0) - seg
    col = (oh * jnp.repeat(loc, tm, axis=0)[None]).sum(-1) + pos

    row0 = jnp.arange(loc_rows // ch, dtype=jnp.int32) * ch
    ej = (row0[None, :, None] >= (loc + seg)[:, None, :]).sum(-1)
    ohj = (jnp.minimum(ej, n_e - 1)[..., None] == ids).astype(jnp.int32)
    dst = (ohj * (gdest - loc)[:, None, :]).sum(-1) + row0[None, :]
    dump = main_rows + (jnp.arange(n_tiles, dtype=jnp.int32) % 2)[:, None] * loc_rows + row0[None, :]
    dst = jnp.where(ej < n_e, dst, dump) // ch

    r0 = jnp.arange(n_sorted // ft, dtype=jnp.int32) * ft
    ends = base + totp
    te = jnp.minimum((r0[:, None] >= ends[None, :]).sum(-1), n_e - 1)
    tvalid = (r0 < ends[-1]).astype(jnp.int32)
    tfirst = tvalid * (r0 == (((te[:, None] == ids) * base[None, :]).sum(-1))).astype(jnp.int32)
    return col, dst.astype(jnp.int32), te.astype(jnp.int32), tvalid, tfirst


def _moe_rows(t, tm, n_e):
    ch, ft = MOE_CHUNK, MOE_FFN_TILE
    loc_rows = 2 * tm + n_e * ch
    main = 2 * t + (t // tm) * n_e * (ch - 1) + n_e * (ft - 1)
    main = (main + ft - 1) // ft * ft
    return loc_rows, main, (main + 2 * loc_rows + ft - 1) // ft * ft


def _chunk_copies(dst_ref, tile, local_ref, sorted_ref, sem, to_sorted):
    ch = MOE_CHUNK
    copies = []
    for j in range(local_ref.shape[0] // ch):
        far = sorted_ref.at[pl.ds(pl.multiple_of(dst_ref[tile, j] * ch, ch), ch)]
        near = local_ref.at[pl.ds(j * ch, ch)]
        copies.append(pltpu.make_async_copy(near, far, sem) if to_sorted
                      else pltpu.make_async_copy(far, near, sem))
    return copies


def _dispatch_kernel(dst_ref, col_ref, h_ref, init_ref, xs_ref, comp, sem):
    del init_ref
    tile = pl.program_id(0)
    last = pl.num_programs(0) - 1
    slot = tile % 2

    def wait_slot(s):
        pltpu.make_async_copy(comp.at[s], xs_ref.at[pl.ds(0, comp.shape[1])], sem.at[s]).wait()

    @pl.when(tile >= 2)
    def _():
        wait_slot(slot)

    col = col_ref[...]
    srow = lax.broadcasted_iota(jnp.int32, (comp.shape[1], col.shape[1]), 0)
    sel = jnp.where((srow == col[0:1]) | (srow == col[1:2]), 1.0, 0.0).astype(BF16)
    comp[slot] = jnp.dot(sel, h_ref[...], preferred_element_type=F32).astype(BF16)
    for cp in _chunk_copies(dst_ref, tile, comp.at[slot], xs_ref, sem.at[slot], True):
        cp.start()

    @pl.when(tile == last)
    def _():
        wait_slot(slot)

    @pl.when((tile == last) & (tile >= 1))
    def _():
        wait_slot(1 - slot)


def _dispatch(dst, col, h, n_e, init):
    t, d = h.shape
    tm = MOE_TILE
    loc_rows, _, n_sorted = _moe_rows(t, tm, n_e)
    if init is None:
        init = jnp.zeros((n_sorted, d), BF16)
    return pl.pallas_call(
        _dispatch_kernel,
        out_shape=jax.ShapeDtypeStruct((n_sorted, d), BF16),
        grid_spec=pltpu.PrefetchScalarGridSpec(
            num_scalar_prefetch=1,
            grid=(t // tm,),
            in_specs=[pl.BlockSpec((2, tm), lambda i, dst: (0, i)),
                      pl.BlockSpec((tm, d), lambda i, dst: (i, 0)),
                      pl.BlockSpec(memory_space=pl.ANY)],
            out_specs=pl.BlockSpec(memory_space=pl.ANY),
            scratch_shapes=[pltpu.VMEM((2, loc_rows, d), BF16), pltpu.SemaphoreType.DMA((2,))],
        ),
        input_output_aliases={3: 0},
        compiler_params=_cparams(("arbitrary",)),
        name="moe_dispatch",
    )(dst, col, h, init)


def _expert_kernel(te_ref, tv_ref, tf_ref, xs_ref, wg_ref, wu_ref, wd_ref, ys_ref, wg_b, wu_b, wd_b):
    del te_ref
    s = pl.program_id(0)

    @pl.when(tf_ref[s] == 1)
    def _():
        wg_b[...] = wg_ref[...].astype(BF16)
        wu_b[...] = wu_ref[...].astype(BF16)
        wd_b[...] = wd_ref[...].astype(BF16)

    @pl.when(tv_ref[s] == 1)
    def _():
        rows = ys_ref.shape[0] // MOE_FFN_SPLIT
        hs = []
        for r in range(MOE_FFN_SPLIT):
            xs = xs_ref[r * rows:(r + 1) * rows, :]
            hg = jnp.dot(xs, wg_b[...], preferred_element_type=F32)
            hu = jnp.dot(xs, wu_b[...], preferred_element_type=F32)
            hs.append((hg, hu))
        for r, (hg, hu) in enumerate(hs):
            he = hg * (1.0 / (1.0 + jnp.exp(-hg))) * hu
            ys_ref[r * rows:(r + 1) * rows, :] = jnp.dot(
                he.astype(BF16), wd_b[...], preferred_element_type=F32).astype(BF16)

    @pl.when(tv_ref[s] == 0)
    def _():
        ys_ref[...] = jnp.zeros(ys_ref.shape, ys_ref.dtype)


def _experts(te, tvalid, tfirst, xs, wg, wu, wd, layer):
    n_sorted, d = xs.shape
    f = wg.shape[3]
    ft = MOE_FFN_TILE
    wspec = lambda r, c: pl.BlockSpec((None, None, r, c), lambda s, te, tv, tf: (layer, te[s], 0, 0))
    return pl.pallas_call(
        _expert_kernel,
        out_shape=jax.ShapeDtypeStruct((n_sorted, d), BF16),
        grid_spec=pltpu.PrefetchScalarGridSpec(
            num_scalar_prefetch=3,
            grid=(n_sorted // ft,),
            in_specs=[pl.BlockSpec((ft, d), lambda s, te, tv, tf: (s, 0)),
                      wspec(d, f), wspec(d, f), wspec(f, d)],
            out_specs=pl.BlockSpec((ft, d), lambda s, te, tv, tf: (s, 0)),
            scratch_shapes=[pltpu.VMEM((d, f), BF16), pltpu.VMEM((d, f), BF16), pltpu.VMEM((f, d), BF16)],
        ),
        compiler_params=_cparams(("arbitrary",)),
        name="moe_experts",
    )(te, tvalid, tfirst, xs, wg, wu, wd)


def _combine_kernel(dst_ref, col_ref, w_ref, ys_ref, x_ref, g2_ref, o_ref, comp, sem):
    tile = pl.program_id(0)
    slot = tile % 2

    def fetch(t, s):
        for cp in _chunk_copies(dst_ref, t, comp.at[s], ys_ref, sem.at[s], False):
            cp.start()

    def wait_slot(s):
        pltpu.make_async_copy(ys_ref.at[pl.ds(0, comp.shape[1])], comp.at[s], sem.at[s]).wait()

    last = pl.num_programs(0) - 1

    @pl.when(tile == 0)
    def _():
        fetch(tile, slot)

    fetch(jnp.minimum(tile + 1, last), 1 - slot)

    col = col_ref[...]
    w = w_ref[...]
    scol = lax.broadcasted_iota(jnp.int32, (col.shape[0], comp.shape[1]), 1)
    selw = (jnp.where(scol == col[:, 0:1], w[:, 4:5], 0.0)
            + jnp.where(scol == col[:, 1:2], w[:, 5:6], 0.0)).astype(BF16)
    wait_slot(slot)
    acc = jnp.dot(selw, comp[slot], preferred_element_type=F32)
    o_ref[...] = x_ref[...] + g2_ref[...] * acc

    @pl.when(tile == last)
    def _():
        wait_slot(1 - slot)


def _combine(dst, col_t, route_t, ys, x2, g2, seq, n_e):
    t, d = x2.shape
    tm = MOE_TILE
    per_b = seq // tm
    loc_rows = _moe_rows(t, tm, n_e)[0]
    return pl.pallas_call(
        _combine_kernel,
        out_shape=jax.ShapeDtypeStruct((t, d), F32),
        grid_spec=pltpu.PrefetchScalarGridSpec(
            num_scalar_prefetch=1,
            grid=(t // tm,),
            in_specs=[pl.BlockSpec((tm, 2), lambda i, dst: (i, 0)),
                      pl.BlockSpec((tm, 8), lambda i, dst: (i, 0)),
                      pl.BlockSpec(memory_space=pl.ANY),
                      pl.BlockSpec((tm, d), lambda i, dst: (i, 0)),
                      pl.BlockSpec((None, 1, d), lambda i, dst: (i // per_b, 0, 0))],
            out_specs=pl.BlockSpec((tm, d), lambda i, dst: (i, 0)),
            scratch_shapes=[pltpu.VMEM((2, loc_rows, d), BF16), pltpu.SemaphoreType.DMA((2,))],
        ),
        compiler_params=_cparams(("arbitrary",)),
        name="moe_combine",
    )(dst, col_t, route_t, ys, x2, g2)


def _moe(h, route, wg, wu, wd, layer, x2, g2, seq, sorted_init):
    t = x2.shape[0]
    n_e = wg.shape[1]
    col, dst, te, tvalid, tfirst = _moe_plan(route, t, MOE_TILE, n_e)
    xs = _dispatch(dst, col, h, n_e, sorted_init)
    ys = _experts(te, tvalid, tfirst, xs, wg, wu, wd, layer)
    return _combine(dst, col.T, route.T, ys, x2, g2, seq, n_e), ys


def kernel(x, c, ada_w, ada_b, norm_mix_g, norm_ffn_g, t5_bias, a_w_qkv, a_q_gain, a_k_gain, a_lambda, a_subln_g, a_w_o, b_w_qkv, b_q_gain, b_k_gain, b_rel_bias, b_w_o, router_w, router_bias, moe_w_gate, moe_w_up, moe_w_down):
    batch, seq, d = x.shape
    depth = ada_w.shape[0]
    assert seq % ATTN_TILE == 0 and seq % ROW_TILE == 0 and d == A_HEADS * 2 * A_HEAD_DIM
    assert d == B_HEADS * B_HEAD_DIM and A_HEAD_DIM == B_HEAD_DIM

    c_pad = jnp.zeros((8, d), F32).at[:batch].set(c.astype(F32))
    mod = _modulation(c_pad, ada_w.astype(F32), ada_b.astype(F32))[:, :batch]
    mod = mod.reshape(depth, batch, 6, 1, d)

    rwt = router_w.astype(F32).T
    rwt_hi = lax.bitcast_convert_type(
        lax.bitcast_convert_type(rwt, jnp.uint32) & jnp.uint32(0xFFFF0000), F32)
    rwt = jnp.concatenate([rwt_hi, rwt - rwt_hi], axis=0).astype(BF16)
    rb = router_bias.astype(F32).reshape(-1, 1)

    x2 = x.astype(F32).reshape(batch * seq, d)
    sorted_buf = None
    for i in range(depth):
        sh1, sc1, g1, sh2, sc2, g2 = [mod[i, :, k] for k in range(6)]
        j = i // 2
        if i % 2 == 0:
            w_qkv, qg, kg, w_o = a_w_qkv[j], a_q_gain[j], a_k_gain[j], a_w_o[j]
        else:
            w_qkv, qg, kg, w_o = b_w_qkv[j], b_q_gain[j], b_k_gain[j], b_w_o[j]
        n_rep = d // qg.shape[0]
        gq = (jnp.tile(qg.astype(F32), n_rep) * (A_HEAD_DIM ** -0.5 * LOG2E)).reshape(d, 1)
        gk = jnp.tile(kg.astype(F32), n_rep).reshape(1, d)
        qt, k, vt = _qkv_proj(x2, norm_mix_g[i].reshape(1, d), sc1, sh1, w_qkv.astype(BF16),
                              gq, gk, seq, A_HEAD_DIM)
        if i % 2 == 0:
            lambda_init = 0.8 - 0.6 * math.exp(-0.3 * i)
            o = _attention_a(qt, k, vt, _t5_vectors(t5_bias, ATTN_TILE), a_lambda[j].astype(F32),
                             a_subln_g[j].reshape(1, -1), batch, seq, lambda_init)
        else:
            o = _attention_b(qt, k, vt, _band_vectors(b_rel_bias[j], BAND_TILE), batch, seq)
        x2, h, route = _out_proj(o, w_o.astype(BF16), x2, g1, norm_ffn_g[i].reshape(1, d),
                                 sc2, sh2, rwt, rb, seq)
        x2, sorted_buf = _moe(h, route, moe_w_gate, moe_w_up, moe_w_down, i, x2, g2, seq, sorted_buf)
    return x2.reshape(batch, seq, d)
```

```python
import functools
import math

import numpy as np
import jax
import jax.numpy as jnp
from jax import lax
from jax.experimental import pallas as pl
from jax.experimental.pallas import tpu as pltpu

F32 = jnp.float32
BF16 = jnp.bfloat16

CHUNK = 64
A_HEADS = 8
A_HEAD_DIM = 64
T5_BUCKETS = 32
T5_MAX_DIST = 1024
B_HEADS = 16
B_HEAD_DIM = 64
LEFT_CHUNKS = 8
MAX_REL = 256
N_EXPERTS = 16
N_GROUPS = 4
E_PER_GROUP = N_EXPERTS // N_GROUPS
NORM_EPS = 1e-6
NEG_INF = -1e30
LOG2E = math.log2(math.e)

V7X_LANES = 128
V7X_MXU_DIM = 256

ATTN_TILE = 512
MOD_COL_TILE = 1536
FAR_UNROLL = 4
BAND_TILE = 256
BAND_QBLOCKS = 16
BAND_AHEAD = 2
ROW_TILE = 512
QKV_SPLIT = 2
MOE_TILE = 512
OPROJ_ROW_TILE = 1024
OPROJ_SPLIT = 4
MOE_CHUNK = 16
MOE_FFN_TILE = 512
MOE_FFN_SPLIT = 2
VMEM_LIMIT = 56 * 1024 * 1024

_NT = (((1,), (1,)), ((), ()))


def _cparams(sem):
    return pltpu.CompilerParams(dimension_semantics=sem, vmem_limit_bytes=VMEM_LIMIT)


def _mod_kernel(c_ref, w_ref, b_ref, o_ref):
    c = c_ref[...]
    s = c * (1.0 / (1.0 + jnp.exp(-c)))
    w = w_ref[...]
    s_hi, w_hi = s.astype(BF16), w.astype(BF16)
    s_lo = (s - s_hi.astype(F32)).astype(BF16)
    w_lo = (w - w_hi.astype(F32)).astype(BF16)
    dot = functools.partial(jnp.dot, preferred_element_type=F32)
    o_ref[...] = dot(s_hi, w_hi) + dot(s_hi, w_lo) + dot(s_lo, w_hi) + b_ref[...]


def _modulation(c_pad, ada_w, ada_b):
    depth, d, n = ada_w.shape
    rows = c_pad.shape[0]
    tn = MOD_COL_TILE
    return pl.pallas_call(
        _mod_kernel,
        out_shape=jax.ShapeDtypeStruct((depth, rows, n), F32),
        grid=(depth, n // tn),
        in_specs=[
            pl.BlockSpec((rows, d), lambda i, j: (0, 0)),
            pl.BlockSpec((None, d, tn), lambda i, j: (i, 0, j)),
            pl.BlockSpec((None, 1, tn), lambda i, j: (i, 0, j)),
        ],
        out_specs=pl.BlockSpec((None, rows, tn), lambda i, j: (i, 0, j)),
        compiler_params=_cparams(("parallel", "parallel")),
        name="adaln_mod",
    )(c_pad, ada_w, ada_b.reshape(depth, 1, n))


def _norm_mod(x, g, sc, sh):
    ms = jnp.mean(x * x, axis=-1, keepdims=True)
    return x * lax.rsqrt(ms + NORM_EPS) * g * (1.0 + sc) + sh


def _qkv_kernel(x_ref, g_ref, sc_ref, sh_ref, w_ref, gq_ref, gk_ref,
                qt_ref, k_ref, vt_ref, wqt_ref, wvt_ref, *, head_dim):
    d, tm = qt_ref.shape

    @pl.when(pl.program_id(0) == 0)
    def _():
        wqt_ref[...] = w_ref[:, :d].T
        wvt_ref[...] = w_ref[:, 2 * d:].T

    cw = V7X_MXU_DIM
    r = lax.broadcasted_iota(jnp.int32, (cw, cw), 0) // head_dim
    c = lax.broadcasted_iota(jnp.int32, (cw, cw), 1) // head_dim
    gmat = jnp.where(r == c, 1.0 / head_dim, 0.0).astype(BF16)
    tg = tm // QKV_SPLIT
    for gi in range(QKV_SPLIT):
        rows = slice(gi * tg, (gi + 1) * tg)
        h = _norm_mod(x_ref[rows, :], g_ref[...], sc_ref[...], sh_ref[...]).astype(BF16)
        y = lax.dot_general(wqt_ref[...], h, _NT, preferred_element_type=F32)
        y3 = y.reshape(d // head_dim, head_dim, tg)
        ss = jnp.mean(y3 * y3, axis=1, keepdims=True)
        qt_ref[:, rows] = ((y3 * lax.rsqrt(ss + NORM_EPS)).reshape(d, tg) * gq_ref[...]).astype(BF16)
        y = jnp.dot(h, w_ref[:, d:2 * d], preferred_element_type=F32)
        ysq = (y * y).astype(BF16)
        for ci in range(d // cw):
            cols = slice(ci * cw, (ci + 1) * cw)
            ss = jnp.dot(ysq[:, cols], gmat, preferred_element_type=F32)
            k_ref[rows, cols] = (y[:, cols] * lax.rsqrt(ss + NORM_EPS) * gk_ref[:, cols]).astype(BF16)
        vt_ref[:, rows] = lax.dot_general(wvt_ref[...], h, _NT, preferred_element_type=F32).astype(BF16)


def _qkv_proj(x2, g, sc, sh, w, gq, gk, seq, head_dim):
    t, d = x2.shape
    tm = ROW_TILE * QKV_SPLIT
    per_b = seq // tm
    vec = lambda i: (i // per_b, 0, 0)
    full = lambda i: (0, 0)
    return pl.pallas_call(
        functools.partial(_qkv_kernel, head_dim=head_dim),
        out_shape=(jax.ShapeDtypeStruct((d, t), BF16),
                   jax.ShapeDtypeStruct((t, d), BF16),
                   jax.ShapeDtypeStruct((d, t), BF16)),
        grid=(t // tm,),
        in_specs=[
            pl.BlockSpec((tm, d), lambda i: (i, 0)),
            pl.BlockSpec((1, d), full),
            pl.BlockSpec((None, 1, d), vec),
            pl.BlockSpec((None, 1, d), vec),
            pl.BlockSpec((d, 3 * d), full),
            pl.BlockSpec((d, 1), full),
            pl.BlockSpec((1, d), full),
        ],
        out_specs=(pl.BlockSpec((d, tm), lambda i: (0, i)),
                   pl.BlockSpec((tm, d), lambda i: (i, 0)),
                   pl.BlockSpec((d, tm), lambda i: (0, i))),
        scratch_shapes=[pltpu.VMEM((d, d), BF16), pltpu.VMEM((d, d), BF16)],
        compiler_params=_cparams(("arbitrary",)),
        name="qkv_proj",
    )(x2, g, sc, sh, w, gq, gk)


def _softmax_pv(s, vt1, m_ref, acc_ref):
    m_prev = m_ref[...]
    m_new = jnp.maximum(m_prev, jnp.max(s, axis=0, keepdims=True))
    alpha = jnp.exp2(m_prev - m_new)
    p = jnp.exp2(s - m_new).astype(BF16)
    acc_ref[...] = alpha * acc_ref[...] + jnp.dot(vt1, p, preferred_element_type=F32)
    m_ref[...] = m_new


def _split_maps(qt, head_dim):
    row = lax.broadcasted_iota(jnp.int32, qt.shape, 0)
    zero = jnp.zeros_like(qt)
    return jnp.where(row < head_dim, qt, zero), jnp.where(row >= head_dim, qt, zero)


ONES_ROWS = 16


def _init_state(refs):
    for m_ref, acc_ref in refs:
        m_ref[...] = jnp.full(m_ref.shape, NEG_INF, F32)
        acc_ref[...] = jnp.zeros(acc_ref.shape, F32)


def _normalized(acc_ref):
    acc = acc_ref[...]
    return acc[:V7X_LANES] / acc[V7X_LANES:V7X_LANES + 1]


def _toeplitz(x_row, n):
    x = jnp.broadcast_to(x_row, (n, x_row.shape[1]))
    return pltpu.roll(x, 0, 1, stride=1, stride_axis=0)[:, :n]


def _chunk_delta(n):
    kc = lax.broadcasted_iota(jnp.int32, (n, n), 0) // CHUNK
    qc = lax.broadcasted_iota(jnp.int32, (n, n), 1) // CHUNK
    return qc - kc


def _attn_a_kernel(qt_ref, qn_ref, k_ref, vt_ref, x_ref, lam_ref, sub_ref, o_ref,
                   tab, fbuf, nbuf, m_st, acc_st, *, n_off, lambda_init):
    tq = ATTN_TILE
    step = pl.program_id(2)

    @pl.when(step == 0)
    def _():
        for o in range(n_off):
            t = _toeplitz(x_ref[o], tq) * LOG2E
            if o == 0:
                t = jnp.where(_chunk_delta(tq) >= 0, t, NEG_INF)
            tab[o] = t

    for c in range(2):
        _init_state(((m_st.at[c, 0], acc_st.at[c, 0]), (m_st.at[c, 1], acc_st.at[c, 1])))
    q_maps = [_split_maps(qt_ref[:, c * tq:(c + 1) * tq], A_HEAD_DIM) for c in range(2)]

    def k_tile(j):
        return k_ref[pl.ds(pl.multiple_of(j * tq, tq), tq), :]

    def vt1_tile(j):
        vt = vt_ref[:, pl.ds(pl.multiple_of(j * tq, tq), tq)]
        return jnp.concatenate([vt, jnp.ones((ONES_ROWS, tq), vt.dtype)], axis=0)

    def scores(buf, k, c):
        for m in range(2):
            buf[c, m] = jnp.dot(k, q_maps[c][m], preferred_element_type=F32)

    def consume(buf, vt1, c, o):
        for m in range(2):
            s = buf[c, m]
            _softmax_pv(s if o >= n_off else s + tab[o], vt1, m_st.at[c, m], acc_st.at[c, m])

    def near_block(first_tile, n_tiles, first_scored):
        units = [[c for c in range(2) if n_tiles - 2 + c - s >= 0] for s in range(n_tiles)]
        buf_of = lambda s: fbuf.at[0] if s == 0 else nbuf.at[s - 1]

        for s in range(1 if first_scored else 0, n_tiles):
            k = k_tile(first_tile + s)
            for c in units[s]:
                scores(buf_of(s), k, c)
        for s in range(n_tiles):
            vt1 = vt1_tile(first_tile + s)
            for c in units[s]:
                consume(buf_of(s), vt1, c, n_tiles - 2 + c - s)

    first_general = (n_off - 1) // 2
    for i0 in range(first_general):
        @pl.when(step == i0)
        def _():
            near_block(0, 2 * i0 + 2, False)

    def far_step(j, parity):
        k = k_tile(j + 1)
        for c in range(2):
            scores(fbuf.at[1 - parity], k, c)
        vt1 = vt1_tile(j)
        for c in range(2):
            consume(fbuf.at[parity], vt1, c, n_off)

    n_far = jnp.maximum(2 * step - (n_off - 1), 0)
    rem = n_far % FAR_UNROLL

    @pl.when(rem >= 2)
    def _():
        far_step(0, 0)
        far_step(1, 1)

    def far_trip(t, carry):
        for u in range(FAR_UNROLL):
            far_step(FAR_UNROLL * t + rem + u, u % 2)
        return carry
    lax.fori_loop(0, n_far // FAR_UNROLL, far_trip, 0)

    @pl.when(step >= first_general)
    def _():
        near_block(n_far, n_off + 1, True)

    def epilogue():
        lam = lam_ref[...]
        lam_full = (jnp.exp(jnp.sum(lam[0:1] * lam[1:2], axis=-1, keepdims=True))
                    - jnp.exp(jnp.sum(lam[2:3] * lam[3:4], axis=-1, keepdims=True)) + lambda_init)
        for c in range(2):
            a = _normalized(acc_st.at[c, 0]) - lam_full * _normalized(acc_st.at[c, 1])
            ms = jnp.mean(a * a, axis=0, keepdims=True)
            an = a * lax.rsqrt(ms + NORM_EPS)
            o_ref[c * tq:(c + 1) * tq, :] = (an.T * (sub_ref[...] * (1.0 - lambda_init))).astype(BF16)

    last = pl.num_programs(2) - 1

    @pl.when(step < last)
    def _():
        k = k_tile(0)
        for c in range(2):
            qa, qb = _split_maps(qn_ref[:, c * tq:(c + 1) * tq], A_HEAD_DIM)
            fbuf[0, c, 0] = jnp.dot(k, qa, preferred_element_type=F32)
            fbuf[0, c, 1] = jnp.dot(k, qb, preferred_element_type=F32)
        epilogue()

    @pl.when(step == last)
    def _():
        epilogue()


def _attn_b_kernel(qt_ref, k_ref, vt_ref, x_ref, o_ref, tab, sbuf):
    tq = BAND_TILE
    n_blk = qt_ref.shape[1] // tq
    n_off = tab.shape[0] // 2
    step = pl.program_id(2)

    @pl.when(step == 0)
    def _():
        for m in range(2):
            for o in range(n_off):
                d = _chunk_delta(tq) + o * (tq // CHUNK)
                t = _toeplitz(x_ref[n_off * m + o], tq) * LOG2E
                tab[n_off * m + o] = jnp.where((d >= 0) & (d <= LEFT_CHUNKS), t, NEG_INF)

    q_maps = [_split_maps(qt_ref[:, c * tq:(c + 1) * tq], B_HEAD_DIM) for c in range(n_blk)]

    def run(first_step):
        pairs = [(c, o) for c in range(n_blk) for o in range(n_off - 1, -1, -1)
                 if not first_step or c - o >= 0]
        k_tiles, vt_tiles = {}, {}
        for c, o in pairs:
            if c - o not in k_tiles:
                ks = pl.multiple_of((n_blk * step + c - o) * tq, tq)
                k_tiles[c - o] = k_ref[pl.ds(ks, tq), :]
                vt = vt_ref[:, pl.ds(ks, tq)]
                vt_tiles[c - o] = jnp.concatenate([vt, jnp.ones((ONES_ROWS, tq), vt.dtype)], axis=0)
        def tile_scores(c):
            for o in [o for cc, o in pairs if cc == c]:
                for m in range(2):
                    sbuf[n_off * c + o, m] = jnp.dot(k_tiles[c - o], q_maps[c][m], preferred_element_type=F32)

        for c in range(min(BAND_AHEAD, n_blk)):
            tile_scores(c)
        for c in range(n_blk):
            if c + BAND_AHEAD < n_blk:
                tile_scores(c + BAND_AHEAD)
            outs = []
            for m in range(2):
                offs = [o for cc, o in pairs if cc == c]
                ss = [sbuf[n_off * c + o, m] + tab[n_off * m + o] for o in offs]
                mx = functools.reduce(jnp.maximum, [jnp.max(s, axis=0, keepdims=True) for s in ss])
                acc = sum(jnp.dot(vt_tiles[c - o], jnp.exp2(s - mx).astype(BF16), preferred_element_type=F32)
                          for o, s in zip(offs, ss))
                outs.append(acc[:V7X_LANES] / acc[V7X_LANES:V7X_LANES + 1])
            row = lax.broadcasted_iota(jnp.int32, outs[0].shape, 0)
            o_ref[c * tq:(c + 1) * tq, :] = jnp.where(row < B_HEAD_DIM, outs[0], outs[1]).T.astype(BF16)

    @pl.when(step == 0)
    def _():
        run(True)

    @pl.when(step >= 1)
    def _():
        run(False)


def _attn_specs(batch, seq, nq, tq):
    return dict(
        q=pl.BlockSpec((V7X_LANES, tq), lambda b, h, i: (h, b * nq + i)),
        k=pl.BlockSpec((seq, V7X_LANES), lambda b, h, i: (b, h)),
        v=pl.BlockSpec((V7X_LANES, seq), lambda b, h, i: (h, b)),
        o=pl.BlockSpec((tq, V7X_LANES), lambda b, h, i: (b * nq + i, h)),
    )


def _attention_a(qt, k, vt, xvec, lam, sub_g, batch, seq, lambda_init):
    d, t = qt.shape
    tq = ATTN_TILE
    n_off = xvec.shape[1]
    assert n_off % 2 == 1 and seq % (2 * tq) == 0
    nq = seq // (2 * tq)
    sp = _attn_specs(batch, seq, nq, 2 * tq)
    rows = V7X_LANES + ONES_ROWS
    return pl.pallas_call(
        functools.partial(_attn_a_kernel, n_off=n_off, lambda_init=lambda_init),
        out_shape=jax.ShapeDtypeStruct((t, d), BF16),
        grid=(batch, d // V7X_LANES, nq),
        in_specs=[sp["q"],
                  pl.BlockSpec((V7X_LANES, 2 * tq), lambda b, h, i: (h, b * nq + jnp.minimum(i + 1, nq - 1))),
                  sp["k"], sp["v"],
                  pl.BlockSpec((None,) + xvec.shape[1:], lambda b, h, i: (h, 0, 0, 0)),
                  pl.BlockSpec(lam.shape, lambda b, h, i: (0, 0)),
                  pl.BlockSpec(sub_g.shape, lambda b, h, i: (0, 0))],
        out_specs=sp["o"],
        scratch_shapes=[pltpu.VMEM((n_off, tq, tq), F32), pltpu.VMEM((2, 2, 2, tq, tq), F32),
                        pltpu.VMEM((n_off, 2, 2, tq, tq), F32),
                        pltpu.VMEM((2, 2, 1, tq), F32), pltpu.VMEM((2, 2, rows, tq), F32)],
        compiler_params=_cparams(("parallel", "parallel", "arbitrary")),
        name="diff_attention",
    )(qt, qt, k, vt, xvec, lam, sub_g)


def _attention_b(qt, k, vt, xvec, batch, seq):
    d, t = qt.shape
    tq = BAND_TILE
    g = BAND_QBLOCKS
    n_off = xvec.shape[1] // 2
    assert g >= n_off - 1 and seq % (g * tq) == 0
    nq = seq // (g * tq)
    sp = _attn_specs(batch, seq, nq, g * tq)
    return pl.pallas_call(
        _attn_b_kernel,
        out_shape=jax.ShapeDtypeStruct((t, d), BF16),
        grid=(batch, d // V7X_LANES, nq),
        in_specs=[sp["q"], sp["k"], sp["v"],
                  pl.BlockSpec((None,) + xvec.shape[1:], lambda b, h, i: (h, 0, 0, 0))],
        out_specs=sp["o"],
        scratch_shapes=[pltpu.VMEM((2 * n_off, tq, tq), F32), pltpu.VMEM((n_off * g, 2, tq, tq), F32)],
        compiler_params=_cparams(("parallel", "parallel", "arbitrary")),
        name="chunk_attention",
    )(qt, k, vt, xvec)


def _t5_bucket(rel):
    nb = T5_BUCKETS // 2
    ret = jnp.where(rel > 0, nb, 0)
    n = jnp.abs(rel)
    max_exact = nb // 2
    nf = jnp.maximum(n, 1).astype(F32)
    large = max_exact + (jnp.log(nf / max_exact) / math.log(T5_MAX_DIST / max_exact)
                         * (nb - max_exact)).astype(jnp.int32)
    large = jnp.minimum(large, nb - 1)
    return ret + jnp.where(n < max_exact, n, large)


def _t5_const_distance():
    nb = T5_BUCKETS // 2
    max_exact = nb // 2
    n = np.arange(max_exact, 4 * T5_MAX_DIST, dtype=np.float64)
    large = max_exact + np.floor(np.log(n / max_exact) / math.log(T5_MAX_DIST / max_exact) * (nb - max_exact))
    below = np.nonzero(large < nb - 1)[0]
    return int(n[below[-1]]) + 2


def _tile_rel(tile, n_off):
    i = jnp.arange(2 * tile, dtype=jnp.int32)
    rel = jnp.where(i < tile, -i, 2 * tile - i)
    return rel[None, :] - tile * jnp.arange(n_off, dtype=jnp.int32)[:, None]


def _t5_vectors(t5_bias, tile):
    n_off = 1
    while (n_off - 1) * tile + 1 < _t5_const_distance():
        n_off += 1
    tb = t5_bias.astype(F32)
    vals = tb[_t5_bucket(_tile_rel(tile, n_off))] - tb[T5_BUCKETS // 2 - 1]
    return vals.transpose(2, 0, 1)[:, :, None, :]


def _band_vectors(rel_bias, tile):
    n_off = LEFT_CHUNKS * CHUNK // tile + 1
    idx = jnp.clip(_tile_rel(tile, n_off), -MAX_REL, MAX_REL) + MAX_REL
    vals = rel_bias.astype(F32)[:, idx]
    return vals.reshape(rel_bias.shape[0] // 2, 2 * n_off, 1, 2 * tile)


def _top2_sum4(r0, r1, r2, r3):
    a, b = jnp.maximum(r0, r1), jnp.minimum(r0, r1)
    c, d = jnp.maximum(r2, r3), jnp.minimum(r2, r3)
    return jnp.maximum(a, c) + jnp.maximum(jnp.minimum(a, c), jnp.maximum(b, d))


def _route(logits, rbias):
    n_e, n = logits.shape
    scores = 1.0 / (1.0 + jnp.exp(-logits))
    sel = scores + rbias
    row = lax.broadcasted_iota(jnp.int32, sel.shape, 0)
    best = None
    for g in range(N_GROUPS):
        rows = [sel[g * E_PER_GROUP + i: g * E_PER_GROUP + i + 1, :] for i in range(E_PER_GROUP)]
        gs = _top2_sum4(*rows)
        if best is None:
            best, gidx = gs, jnp.zeros(gs.shape, jnp.int32)
        else:
            gidx = jnp.where(gs > best, g, gidx)
            best = jnp.maximum(best, gs)
    masked = jnp.where(row // E_PER_GROUP == gidx, sel, NEG_INF)
    m1 = jnp.max(masked, axis=0, keepdims=True)
    i1 = jnp.min(jnp.where(masked == m1, row, n_e), axis=0, keepdims=True)
    masked2 = jnp.where(row == i1, -3.0e38, masked)
    m2 = jnp.max(masked2, axis=0, keepdims=True)
    i2 = jnp.min(jnp.where(masked2 == m2, row, n_e), axis=0, keepdims=True)
    w1 = jnp.sum(jnp.where(row == i1, scores, 0.0), axis=0, keepdims=True)
    w2 = jnp.sum(jnp.where(row == i2, scores, 0.0), axis=0, keepdims=True)
    den = w1 + w2
    member = jnp.where((row == i1) | (row == i2), 1.0, 0.0).astype(BF16)
    t_from = lax.broadcasted_iota(jnp.int32, (n, n), 0)
    t_to = lax.broadcasted_iota(jnp.int32, (n, n), 1)
    before = jnp.where((t_from < t_to) & (t_from // MOE_TILE == t_to // MOE_TILE), 1.0, 0.0).astype(BF16)
    pos = jnp.dot(member, before, preferred_element_type=F32)
    p1 = jnp.sum(jnp.where(row == i1, pos, 0.0), axis=0, keepdims=True)
    p2 = jnp.sum(jnp.where(row == i2, pos, 0.0), axis=0, keepdims=True)
    out_row = lax.broadcasted_iota(jnp.int32, (8, n), 0)
    out = jnp.zeros((8, n), F32)
    for r, val in enumerate((i1.astype(F32), i2.astype(F32), p1, p2, w1 / den, w2 / den)):
        out = jnp.where(out_row == r, val, out)
    return out


def _oproj_kernel(o_ref, wo_ref, x_ref, g1_ref, gn_ref, sc_ref, sh_ref, rwt_ref, rb_ref,
                  xo_ref, h_ref, route_ref):
    n_e = rb_ref.shape[0]
    rw = rwt_ref[...]
    rows = o_ref.shape[0] // OPROJ_SPLIT
    groups = [slice(r * rows, (r + 1) * rows) for r in range(OPROJ_SPLIT)]
    ys = [jnp.dot(o_ref[g, :], wo_ref[...], preferred_element_type=F32) for g in groups]
    logits = []
    for g, y in zip(groups, ys):
        xn = x_ref[g, :] + g1_ref[...] * y
        xo_ref[g, :] = xn
        h = _norm_mod(xn, gn_ref[...], sc_ref[...], sh_ref[...])
        h_hi = h.astype(BF16)
        h_ref[g, :] = h_hi
        h_lo = (h - h_hi.astype(F32)).astype(BF16)
        main = lax.dot_general(rw, h_hi, _NT, preferred_element_type=F32)
        corr = lax.dot_general(rw[:n_e], h_lo, _NT, preferred_element_type=F32)
        logits.append(main[:n_e] + main[n_e:] + corr)
    route_ref[...] = _route(jnp.concatenate(logits, axis=1), rb_ref[...])


def _out_proj(o, wo_bf16, x2, g1, gn, sc, sh, rwt, rb, seq):
    t, d = x2.shape
    tm = OPROJ_ROW_TILE
    per_b = seq // tm
    n_e = rb.shape[0]
    vec = lambda i: (i // per_b, 0, 0)
    return pl.pallas_call(
        _oproj_kernel,
        out_shape=(jax.ShapeDtypeStruct((t, d), F32),
                   jax.ShapeDtypeStruct((t, d), BF16),
                   jax.ShapeDtypeStruct((8, t), F32)),
        grid=(t // tm,),
        in_specs=[
            pl.BlockSpec((tm, d), lambda i: (i, 0)),
            pl.BlockSpec((d, d), lambda i: (0, 0)),
            pl.BlockSpec((tm, d), lambda i: (i, 0)),
            pl.BlockSpec((None, 1, d), vec),
            pl.BlockSpec((1, d), lambda i: (0, 0)),
            pl.BlockSpec((None, 1, d), vec),
            pl.BlockSpec((None, 1, d), vec),
            pl.BlockSpec((2 * n_e, d), lambda i: (0, 0)),
            pl.BlockSpec((n_e, 1), lambda i: (0, 0)),
        ],
        out_specs=(pl.BlockSpec((tm, d), lambda i: (i, 0)),
                   pl.BlockSpec((tm, d), lambda i: (i, 0)),
                   pl.BlockSpec((8, tm), lambda i: (0, i))),
        compiler_params=_cparams(("parallel",)),
        name="out_proj_route",
    )(o, wo_bf16, x2, g1, gn, sc, sh, rwt, rb)


def _moe_plan(route, t, tm, n_e):
    n_tiles = t // tm
    ch, ft = MOE_CHUNK, MOE_FFN_TILE
    loc_rows, main_rows, n_sorted = _moe_rows(t, tm, n_e)
    ids = jnp.arange(n_e, dtype=jnp.int32)
    e = route[0:2].astype(jnp.int32)
    pos = route[2:4].astype(jnp.int32)
    oh = (e[:, :, None] == ids).astype(jnp.int32)
    cnt = oh.sum(0).reshape(n_tiles, tm, n_e).sum(1)
    seg = (cnt + ch - 1) // ch * ch
    loc = jnp.cumsum(seg, axis=1) - seg
    tot = seg.sum(0)
    totp = (tot + ft - 1) // ft * ft
    base = jnp.cumsum(totp) - totp
    gdest = base[None, :] + jnp.cumsum(seg, axis=0) - seg
    col = (oh * jnp.repeat(loc, tm, axis=0)[None]).sum(-1) + pos

    row0 = jnp.arange(loc_rows // ch, dtype=jnp.int32) * ch
    ej = (row0[None, :, None] >= (loc + seg)[:, None, :]).sum(-1)
    ohj = (jnp.minimum(ej, n_e - 1)[..., None] == ids).astype(jnp.int32)
    dst = (ohj * (gdest - loc)[:, None, :]).sum(-1) + row0[None, :]
    dump = main_rows + (jnp.arange(n_tiles, dtype=jnp.int32) % 2)[:, None] * loc_rows + row0[None, :]
    dst = jnp.where(ej < n_e, dst, dump) // ch

    r0 = jnp.arange(n_sorted // ft, dtype=jnp.int32) * ft
    ends = base + totp
    te = jnp.minimum((r0[:, None] >= ends[None, :]).sum(-1), n_e - 1)
    tvalid = (r0 < ends[-1]).astype(jnp.int32)
    tfirst = tvalid * (r0 == (((te[:, None] == ids) * base[None, :]).sum(-1))).astype(jnp.int32)
    return col, dst.astype(jnp.int32), te.astype(jnp.int32), tvalid, tfirst


def _moe_rows(t, tm, n_e):
    ch, ft = MOE_CHUNK, MOE_FFN_TILE
    loc_rows = 2 * tm + n_e * ch
    main = 2 * t + (t // tm) * n_e * (ch - 1) + n_e * (ft - 1)
    main = (main + ft - 1) // ft * ft
    return loc_rows, main, (main + 2 * loc_rows + ft - 1) // ft * ft


def _chunk_copies(dst_ref, tile, local_ref, sorted_ref, sem, to_sorted):
    ch = MOE_CHUNK
    copies = []
    for j in range(local_ref.shape[0] // ch):
        far = sorted_ref.at[pl.ds(pl.multiple_of(dst_ref[tile, j] * ch, ch), ch)]
        near = local_ref.at[pl.ds(j * ch, ch)]
        copies.append(pltpu.make_async_copy(near, far, sem) if to_sorted
                      else pltpu.make_async_copy(far, near, sem))
    return copies


def _dispatch_kernel(dst_ref, col_ref, h_ref, init_ref, xs_ref, comp, sem):
    del init_ref
    tile = pl.program_id(0)
    last = pl.num_programs(0) - 1
    slot = tile % 2

    def wait_slot(s):
        pltpu.make_async_copy(comp.at[s], xs_ref.at[pl.ds(0, comp.shape[1])], sem.at[s]).wait()

    @pl.when(tile >= 2)
    def _():
        wait_slot(slot)

    col = col_ref[...]
    srow = lax.broadcasted_iota(jnp.int32, (comp.shape[1], col.shape[1]), 0)
    sel = jnp.where((srow == col[0:1]) | (srow == col[1:2]), 1.0, 0.0).astype(BF16)
    comp[slot] = jnp.dot(sel, h_ref[...], preferred_element_type=F32).astype(BF16)
    for cp in _chunk_copies(dst_ref, tile, comp.at[slot], xs_ref, sem.at[slot], True):
        cp.start()

    @pl.when(tile == last)
    def _():
        wait_slot(slot)

    @pl.when((tile == last) & (tile >= 1))
    def _():
        wait_slot(1 - slot)


def _dispatch(dst, col, h, n_e, init):
    t, d = h.shape
    tm = MOE_TILE
    loc_rows, _, n_sorted = _moe_rows(t, tm, n_e)
    if init is None:
        init = jnp.zeros((n_sorted, d), BF16)
    return pl.pallas_call(
        _dispatch_kernel,
        out_shape=jax.ShapeDtypeStruct((n_sorted, d), BF16),
        grid_spec=pltpu.PrefetchScalarGridSpec(
            num_scalar_prefetch=1,
            grid=(t // tm,),
            in_specs=[pl.BlockSpec((2, tm), lambda i, dst: (0, i)),
                      pl.BlockSpec((tm, d), lambda i, dst: (i, 0)),
                      pl.BlockSpec(memory_space=pl.ANY)],
            out_specs=pl.BlockSpec(memory_space=pl.ANY),
            scratch_shapes=[pltpu.VMEM((2, loc_rows, d), BF16), pltpu.SemaphoreType.DMA((2,))],
        ),
        input_output_aliases={3: 0},
        compiler_params=_cparams(("arbitrary",)),
        name="moe_dispatch",
    )(dst, col, h, init)


def _expert_kernel(te_ref, tv_ref, tf_ref, xs_ref, wg_ref, wu_ref, wd_ref, ys_ref, wg_b, wu_b, wd_b):
    del te_ref
    s = pl.program_id(0)

    @pl.when(tf_ref[s] == 1)
    def _():
        wg_b[...] = wg_ref[...].astype(BF16)
        wu_b[...] = wu_ref[...].astype(BF16)
        wd_b[...] = wd_ref[...].astype(BF16)

    @pl.when(tv_ref[s] == 1)
    def _():
        rows = ys_ref.shape[0] // MOE_FFN_SPLIT
        hs = []
        for r in range(MOE_FFN_SPLIT):
            xs = xs_ref[r * rows:(r + 1) * rows, :]
            hg = jnp.dot(xs, wg_b[...], preferred_element_type=F32)
            hu = jnp.dot(xs, wu_b[...], preferred_element_type=F32)
            hs.append((hg, hu))
        for r, (hg, hu) in enumerate(hs):
            he = hg * (1.0 / (1.0 + jnp.exp(-hg))) * hu
            ys_ref[r * rows:(r + 1) * rows, :] = jnp.dot(
                he.astype(BF16), wd_b[...], preferred_element_type=F32).astype(BF16)

    @pl.when(tv_ref[s] == 0)
    def _():
        ys_ref[...] = jnp.zeros(ys_ref.shape, ys_ref.dtype)


def _experts(te, tvalid, tfirst, xs, wg, wu, wd, layer):
    n_sorted, d = xs.shape
    f = wg.shape[3]
    ft = MOE_FFN_TILE
    wspec = lambda r, c: pl.BlockSpec((None, None, r, c), lambda s, te, tv, tf: (layer, te[s], 0, 0))
    return pl.pallas_call(
        _expert_kernel,
        out_shape=jax.ShapeDtypeStruct((n_sorted, d), BF16),
        grid_spec=pltpu.PrefetchScalarGridSpec(
            num_scalar_prefetch=3,
            grid=(n_sorted // ft,),
            in_specs=[pl.BlockSpec((ft, d), lambda s, te, tv, tf: (s, 0)),
                      wspec(d, f), wspec(d, f), wspec(f, d)],
            out_specs=pl.BlockSpec((ft, d), lambda s, te, tv, tf: (s, 0)),
            scratch_shapes=[pltpu.VMEM((d, f), BF16), pltpu.VMEM((d, f), BF16), pltpu.VMEM((f, d), BF16)],
        ),
        compiler_params=_cparams(("arbitrary",)),
        name="moe_experts",
    )(te, tvalid, tfirst, xs, wg, wu, wd)


def _combine_kernel(dst_ref, col_ref, w_ref, ys_ref, x_ref, g2_ref, o_ref, comp, sem):
    tile = pl.program_id(0)
    slot = tile % 2

    def fetch(t, s):
        for cp in _chunk_copies(dst_ref, t, comp.at[s], ys_ref, sem.at[s], False):
            cp.start()

    def wait_slot(s):
        pltpu.make_async_copy(ys_ref.at[pl.ds(0, comp.shape[1])], comp.at[s], sem.at[s]).wait()

    last = pl.num_programs(0) - 1

    @pl.when(tile == 0)
    def _():
        fetch(tile, slot)

    fetch(jnp.minimum(tile + 1, last), 1 - slot)

    col = col_ref[...]
    w = w_ref[...]
    scol = lax.broadcasted_iota(jnp.int32, (col.shape[0], comp.shape[1]), 1)
    selw = (jnp.where(scol == col[:, 0:1], w[:, 4:5], 0.0)
            + jnp.where(scol == col[:, 1:2], w[:, 5:6], 0.0)).astype(BF16)
    wait_slot(slot)
    acc = jnp.dot(selw, comp[slot], preferred_element_type=F32)
    o_ref[...] = x_ref[...] + g2_ref[...] * acc

    @pl.when(tile == last)
    def _():
        wait_slot(1 - slot)


def _combine(dst, col_t, route_t, ys, x2, g2, seq, n_e):
    t, d = x2.shape
    tm = MOE_TILE
    per_b = seq // tm
    loc_rows = _moe_rows(t, tm, n_e)[0]
    return pl.pallas_call(
        _combine_kernel,
        out_shape=jax.ShapeDtypeStruct((t, d), F32),
        grid_spec=pltpu.PrefetchScalarGridSpec(
            num_scalar_prefetch=1,
            grid=(t // tm,),
            in_specs=[pl.BlockSpec((tm, 2), lambda i, dst: (i, 0)),
                      pl.BlockSpec((tm, 8), lambda i, dst: (i, 0)),
                      pl.BlockSpec(memory_space=pl.ANY),
                      pl.BlockSpec((tm, d), lambda i, dst: (i, 0)),
                      pl.BlockSpec((None, 1, d), lambda i, dst: (i // per_b, 0, 0))],
            out_specs=pl.BlockSpec((tm, d), lambda i, dst: (i, 0)),
            scratch_shapes=[pltpu.VMEM((2, loc_rows, d), BF16), pltpu.SemaphoreType.DMA((2,))],
        ),
        compiler_params=_cparams(("arbitrary",)),
        name="moe_combine",
    )(dst, col_t, route_t, ys, x2, g2)


def _moe(h, route, wg, wu, wd, layer, x2, g2, seq, sorted_init):
    t = x2.shape[0]
    n_e = wg.shape[1]
    col, dst, te, tvalid, tfirst = _moe_plan(route, t, MOE_TILE, n_e)
    xs = _dispatch(dst, col, h, n_e, sorted_init)
    ys = _experts(te, tvalid, tfirst, xs, wg, wu, wd, layer)
    return _combine(dst, col.T, route.T, ys, x2, g2, seq, n_e), ys


def kernel(x, c, ada_w, ada_b, norm_mix_g, norm_ffn_g, t5_bias, a_w_qkv, a_q_gain, a_k_gain, a_lambda, a_subln_g, a_w_o, b_w_qkv, b_q_gain, b_k_gain, b_rel_bias, b_w_o, router_w, router_bias, moe_w_gate, moe_w_up, moe_w_down):
    batch, seq, d = x.shape
    depth = ada_w.shape[0]
    assert seq % ATTN_TILE == 0 and seq % ROW_TILE == 0 and d == A_HEADS * 2 * A_HEAD_DIM
    assert d == B_HEADS * B_HEAD_DIM and A_HEAD_DIM == B_HEAD_DIM

    c_pad = jnp.zeros((8, d), F32).at[:batch].set(c.astype(F32))
    mod = _modulation(c_pad, ada_w.astype(F32), ada_b.astype(F32))[:, :batch]
    mod = mod.reshape(depth, batch, 6, 1, d)

    rwt = router_w.astype(F32).T
    rwt_hi = lax.bitcast_convert_type(
        lax.bitcast_convert_type(rwt, jnp.uint32) & jnp.uint32(0xFFFF0000), F32)
    rwt = jnp.concatenate([rwt_hi, rwt - rwt_hi], axis=0).astype(BF16)
    rb = router_bias.astype(F32).reshape(-1, 1)

    x2 = x.astype(F32).reshape(batch * seq, d)
    sorted_buf = None
    for i in range(depth):
        sh1, sc1, g1, sh2, sc2, g2 = [mod[i, :, k] for k in range(6)]
        j = i // 2
        if i % 2 == 0:
            w_qkv, qg, kg, w_o = a_w_qkv[j], a_q_gain[j], a_k_gain[j], a_w_o[j]
        else:
            w_qkv, qg, kg, w_o = b_w_qkv[j], b_q_gain[j], b_k_gain[j], b_w_o[j]
        n_rep = d // qg.shape[0]
        gq = (jnp.tile(qg.astype(F32), n_rep) * (A_HEAD_DIM ** -0.5 * LOG2E)).reshape(d, 1)
        gk = jnp.tile(kg.astype(F32), n_rep).reshape(1, d)
        qt, k, vt = _qkv_proj(x2, norm_mix_g[i].reshape(1, d), sc1, sh1, w_qkv.astype(BF16),
                              gq, gk, seq, A_HEAD_DIM)
        if i % 2 == 0:
            lambda_init = 0.8 - 0.6 * math.exp(-0.3 * i)
            o = _attention_a(qt, k, vt, _t5_vectors(t5_bias, ATTN_TILE), a_lambda[j].astype(F32),
                             a_subln_g[j].reshape(1, -1), batch, seq, lambda_init)
        else:
            o = _attention_b(qt, k, vt, _band_vectors(b_rel_bias[j], BAND_TILE), batch, seq)
        x2, h, route = _out_proj(o, w_o.astype(BF16), x2, g1, norm_ffn_g[i].reshape(1, d),
                                 sc2, sh2, rwt, rb, seq)
        x2, sorted_buf = _moe(h, route, moe_w_gate, moe_w_up, moe_w_down, i, x2, g2, seq, sorted_buf)
    return x2.reshape(batch, seq, d)
```

```python
import functools
import math

import numpy as np
import jax
import jax.numpy as jnp
from jax import lax
from jax.experimental import pallas as pl
from jax.experimental.pallas import tpu as pltpu

F32 = jnp.float32
BF16 = jnp.bfloat16

CHUNK = 64
A_HEADS = 8
A_HEAD_DIM = 64
T5_BUCKETS = 32
T5_MAX_DIST = 1024
B_HEADS = 16
B_HEAD_DIM = 64
LEFT_CHUNKS = 8
MAX_REL = 256
N_EXPERTS = 16
N_GROUPS = 4
E_PER_GROUP = N_EXPERTS // N_GROUPS
NORM_EPS = 1e-6
NEG_INF = -1e30
LOG2E = math.log2(math.e)

V7X_LANES = 128
V7X_MXU_DIM = 256

ATTN_TILE = 512
MOD_COL_TILE = 1536
FAR_UNROLL = 4
BAND_TILE = 256
BAND_QBLOCKS = 16
BAND_AHEAD = 2
ROW_TILE = 512
QKV_SPLIT = 2
MOE_TILE = 512
OPROJ_ROW_TILE = 1024
OPROJ_SPLIT = 4
MOE_CHUNK = 16
MOE_FFN_TILE = 512
MOE_FFN_SPLIT = 2
VMEM_LIMIT = 56 * 1024 * 1024

_NT = (((1,), (1,)), ((), ()))


def _cparams(sem):
    return pltpu.CompilerParams(dimension_semantics=sem, vmem_limit_bytes=VMEM_LIMIT)


def _mod_kernel(c_ref, w_ref, b_ref, o_ref):
    c = c_ref[...]
    s = c * (1.0 / (1.0 + jnp.exp(-c)))
    w = w_ref[...]
    s_hi, w_hi = s.astype(BF16), w.astype(BF16)
    s_lo = (s - s_hi.astype(F32)).astype(BF16)
    w_lo = (w - w_hi.astype(F32)).astype(BF16)
    dot = functools.partial(jnp.dot, preferred_element_type=F32)
    o_ref[...] = dot(s_hi, w_hi) + dot(s_hi, w_lo) + dot(s_lo, w_hi) + b_ref[...]


def _modulation(c_pad, ada_w, ada_b):
    depth, d, n = ada_w.shape
    rows = c_pad.shape[0]
    tn = MOD_COL_TILE
    return pl.pallas_call(
        _mod_kernel,
        out_shape=jax.ShapeDtypeStruct((depth, rows, n), F32),
        grid=(depth, n // tn),
        in_specs=[
            pl.BlockSpec((rows, d), lambda i, j: (0, 0)),
            pl.BlockSpec((None, d, tn), lambda i, j: (i, 0, j)),
            pl.BlockSpec((None, 1, tn), lambda i, j: (i, 0, j)),
        ],
        out_specs=pl.BlockSpec((None, rows, tn), lambda i, j: (i, 0, j)),
        compiler_params=_cparams(("parallel", "parallel")),
        name="adaln_mod",
    )(c_pad, ada_w, ada_b.reshape(depth, 1, n))


def _norm_mod(x, g, sc, sh):
    ms = jnp.mean(x * x, axis=-1, keepdims=True)
    return x * lax.rsqrt(ms + NORM_EPS) * g * (1.0 + sc) + sh


def _qkv_kernel(x_ref, g_ref, sc_ref, sh_ref, w_ref, gq_ref, gk_ref,
                qt_ref, k_ref, vt_ref, wqt_ref, wvt_ref, *, head_dim):
    d, tm = qt_ref.shape

    @pl.when(pl.program_id(0) == 0)
    def _():
        wqt_ref[...] = w_ref[:, :d].T
        wvt_ref[...] = w_ref[:, 2 * d:].T

    cw = V7X_MXU_DIM
    r = lax.broadcasted_iota(jnp.int32, (cw, cw), 0) // head_dim
    c = lax.broadcasted_iota(jnp.int32, (cw, cw), 1) // head_dim
    gmat = jnp.where(r == c, 1.0 / head_dim, 0.0).astype(BF16)
    tg = tm // QKV_SPLIT
    for gi in range(QKV_SPLIT):
        rows = slice(gi * tg, (gi + 1) * tg)
        h = _norm_mod(x_ref[rows, :], g_ref[...], sc_ref[...], sh_ref[...]).astype(BF16)
        y = lax.dot_general(wqt_ref[...], h, _NT, preferred_element_type=F32)
        y3 = y.reshape(d // head_dim, head_dim, tg)
        ss = jnp.mean(y3 * y3, axis=1, keepdims=True)
        qt_ref[:, rows] = ((y3 * lax.rsqrt(ss + NORM_EPS)).reshape(d, tg) * gq_ref[...]).astype(BF16)
        y = jnp.dot(h, w_ref[:, d:2 * d], preferred_element_type=F32)
        ysq = (y * y).astype(BF16)
        for ci in range(d // cw):
            cols = slice(ci * cw, (ci + 1) * cw)
            ss = jnp.dot(ysq[:, cols], gmat, preferred_element_type=F32)
            k_ref[rows, cols] = (y[:, cols] * lax.rsqrt(ss + NORM_EPS) * gk_ref[:, cols]).astype(BF16)
        vt_ref[:, rows] = lax.dot_general(wvt_ref[...], h, _NT, preferred_element_type=F32).astype(BF16)


def _qkv_proj(x2, g, sc, sh, w, gq, gk, seq, head_dim):
    t, d = x2.shape
    tm = ROW_TILE * QKV_SPLIT
    per_b = seq // tm
    vec = lambda i: (i // per_b, 0, 0)
    full = lambda i: (0, 0)
    return pl.pallas_call(
        functools.partial(_qkv_kernel, head_dim=head_dim),
        out_shape=(jax.ShapeDtypeStruct((d, t), BF16),
                   jax.ShapeDtypeStruct((t, d), BF16),
                   jax.ShapeDtypeStruct((d, t), BF16)),
        grid=(t // tm,),
        in_specs=[
            pl.BlockSpec((tm, d), lambda i: (i, 0)),
            pl.BlockSpec((1, d), full),
            pl.BlockSpec((None, 1, d), vec),
            pl.BlockSpec((None, 1, d), vec),
            pl.BlockSpec((d, 3 * d), full),
            pl.BlockSpec((d, 1), full),
            pl.BlockSpec((1, d), full),
        ],
        out_specs=(pl.BlockSpec((d, tm), lambda i: (0, i)),
                   pl.BlockSpec((tm, d), lambda i: (i, 0)),
                   pl.BlockSpec((d, tm), lambda i: (0, i))),
        scratch_shapes=[pltpu.VMEM((d, d), BF16), pltpu.VMEM((d, d), BF16)],
        compiler_params=_cparams(("arbitrary",)),
        name="qkv_proj",
    )(x2, g, sc, sh, w, gq, gk)


def _softmax_pv(s, vt1, m_ref, acc_ref):
    m_prev = m_ref[...]
    m_new = jnp.maximum(m_prev, jnp.max(s, axis=0, keepdims=True))
    alpha = jnp.exp2(m_prev - m_new)
    p = jnp.exp2(s - m_new).astype(BF16)
    acc_ref[...] = alpha * acc_ref[...] + jnp.dot(vt1, p, preferred_element_type=F32)
    m_ref[...] = m_new


def _split_maps(qt, head_dim):
    row = lax.broadcasted_iota(jnp.int32, qt.shape, 0)
    zero = jnp.zeros_like(qt)
    return jnp.where(row < head_dim, qt, zero), jnp.where(row >= head_dim, qt, zero)


ONES_ROWS = 16


def _init_state(refs):
    for m_ref, acc_ref in refs:
        m_ref[...] = jnp.full(m_ref.shape, NEG_INF, F32)
        acc_ref[...] = jnp.zeros(acc_ref.shape, F32)


def _normalized(acc_ref):
    acc = acc_ref[...]
    return acc[:V7X_LANES] / acc[V7X_LANES:V7X_LANES + 1]


def _toeplitz(x_row, n):
    x = jnp.broadcast_to(x_row, (n, x_row.shape[1]))
    return pltpu.roll(x, 0, 1, stride=1, stride_axis=0)[:, :n]


def _chunk_delta(n):
    kc = lax.broadcasted_iota(jnp.int32, (n, n), 0) // CHUNK
    qc = lax.broadcasted_iota(jnp.int32, (n, n), 1) // CHUNK
    return qc - kc


def _attn_a_kernel(qt_ref, qn_ref, k_ref, vt_ref, x_ref, lam_ref, sub_ref, o_ref,
                   tab, fbuf, nbuf, m_st, acc_st, *, n_off, lambda_init):
    tq = ATTN_TILE
    step = pl.program_id(2)

    @pl.when(step == 0)
    def _():
        for o in range(n_off):
            t = _toeplitz(x_ref[o], tq) * LOG2E
            if o == 0:
                t = jnp.where(_chunk_delta(tq) >= 0, t, NEG_INF)
            tab[o] = t

    for c in range(2):
        _init_state(((m_st.at[c, 0], acc_st.at[c, 0]), (m_st.at[c, 1], acc_st.at[c, 1])))
    q_maps = [_split_maps(qt_ref[:, c * tq:(c + 1) * tq], A_HEAD_DIM) for c in range(2)]

    def k_tile(j):
        return k_ref[pl.ds(pl.multiple_of(j * tq, tq), tq), :]

    def vt1_tile(j):
        vt = vt_ref[:, pl.ds(pl.multiple_of(j * tq, tq), tq)]
        return jnp.concatenate([vt, jnp.ones((ONES_ROWS, tq), vt.dtype)], axis=0)

    def scores(buf, k, c):
        for m in range(2):
            buf[c, m] = jnp.dot(k, q_maps[c][m], preferred_element_type=F32)

    def consume(buf, vt1, c, o):
        for m in range(2):
            s = buf[c, m]
            _softmax_pv(s if o >= n_off else s + tab[o], vt1, m_st.at[c, m], acc_st.at[c, m])

    def near_block(first_tile, n_tiles, first_scored):
        units = [[c for c in range(2) if n_tiles - 2 + c - s >= 0] for s in range(n_tiles)]
        buf_of = lambda s: fbuf.at[0] if s == 0 else nbuf.at[s - 1]

        for s in range(1 if first_scored else 0, n_tiles):
            k = k_tile(first_tile + s)
            for c in units[s]:
                scores(buf_of(s), k, c)
        for s in range(n_tiles):
            vt1 = vt1_tile(first_tile + s)
            for c in units[s]:
                consume(buf_of(s), vt1, c, n_tiles - 2 + c - s)

    first_general = (n_off - 1) // 2
    for i0 in range(first_general):
        @pl.when(step == i0)
        def _():
            near_block(0, 2 * i0 + 2, False)

    def far_step(j, parity):
        k = k_tile(j + 1)
        for c in range(2):
            scores(fbuf.at[1 - parity], k, c)
        vt1 = vt1_tile(j)
        for c in range(2):
            consume(fbuf.at[parity], vt1, c, n_off)

    n_far = jnp.maximum(2 * step - (n_off - 1), 0)
    rem = n_far % FAR_UNROLL

    @pl.when(rem >= 2)
    def _():
        far_step(0, 0)
        far_step(1, 1)

    def far_trip(t, carry):
        for u in range(FAR_UNROLL):
            far_step(FAR_UNROLL * t + rem + u, u % 2)
        return carry
    lax.fori_loop(0, n_far // FAR_UNROLL, far_trip, 0)

    @pl.when(step >= first_general)
    def _():
        near_block(n_far, n_off + 1, True)

    def epilogue():
        lam = lam_ref[...]
        lam_full = (jnp.exp(jnp.sum(lam[0:1] * lam[1:2], axis=-1, keepdims=True))
                    - jnp.exp(jnp.sum(lam[2:3] * lam[3:4], axis=-1, keepdims=True)) + lambda_init)
        for c in range(2):
            a = _normalized(acc_st.at[c, 0]) - lam_full * _normalized(acc_st.at[c, 1])
            ms = jnp.mean(a * a, axis=0, keepdims=True)
            an = a * lax.rsqrt(ms + NORM_EPS)
            o_ref[c * tq:(c + 1) * tq, :] = (an.T * (sub_ref[...] * (1.0 - lambda_init))).astype(BF16)

    last = pl.num_programs(2) - 1

    @pl.when(step < last)
    def _():
        k = k_tile(0)
        for c in range(2):
            qa, qb = _split_maps(qn_ref[:, c * tq:(c + 1) * tq], A_HEAD_DIM)
            fbuf[0, c, 0] = jnp.dot(k, qa, preferred_element_type=F32)
            fbuf[0, c, 1] = jnp.dot(k, qb, preferred_element_type=F32)
        epilogue()

    @pl.when(step == last)
    def _():
        epilogue()


def _attn_b_kernel(qt_ref, k_ref, vt_ref, x_ref, o_ref, tab, sbuf):
    tq = BAND_TILE
    n_blk = qt_ref.shape[1] // tq
    n_off = tab.shape[0] // 2
    step = pl.program_id(2)

    @pl.when(step == 0)
    def _():
        for m in range(2):
            for o in range(n_off):
                d = _chunk_delta(tq) + o * (tq // CHUNK)
                t = _toeplitz(x_ref[n_off * m + o], tq) * LOG2E
                tab[n_off * m + o] = jnp.where((d >= 0) & (d <= LEFT_CHUNKS), t, NEG_INF)

    q_maps = [_split_maps(qt_ref[:, c * tq:(c + 1) * tq], B_HEAD_DIM) for c in range(n_blk)]

    def run(first_step):
        pairs = [(c, o) for c in range(n_blk) for o in range(n_off - 1, -1, -1)
                 if not first_step or c - o >= 0]
        k_tiles, vt_tiles = {}, {}
        for c, o in pairs:
            if c - o not in k_tiles:
                ks = pl.multiple_of((n_blk * step + c - o) * tq, tq)
                k_tiles[c - o] = k_ref[pl.ds(ks, tq), :]
                vt = vt_ref[:, pl.ds(ks, tq)]
                vt_tiles[c - o] = jnp.concatenate([vt, jnp.ones((ONES_ROWS, tq), vt.dtype)], axis=0)
        def tile_scores(c):
            for o in [o for cc, o in pairs if cc == c]:
                for m in range(2):
                    sbuf[n_off * c + o, m] = jnp.dot(k_tiles[c - o], q_maps[c][m], preferred_element_type=F32)

        for c in range(min(BAND_AHEAD, n_blk)):
            tile_scores(c)
        for c in range(n_blk):
            if c + BAND_AHEAD < n_blk:
                tile_scores(c + BAND_AHEAD)
            outs = []
            for m in range(2):
                offs = [o for cc, o in pairs if cc == c]
                ss = [sbuf[n_off * c + o, m] + tab[n_off * m + o] for o in offs]
                mx = functools.reduce(jnp.maximum, [jnp.max(s, axis=0, keepdims=True) for s in ss])
                acc = sum(jnp.dot(vt_tiles[c - o], jnp.exp2(s - mx).astype(BF16), preferred_element_type=F32)
                          for o, s in zip(offs, ss))
                outs.append(acc[:V7X_LANES] / acc[V7X_LANES:V7X_LANES + 1])
            row = lax.broadcasted_iota(jnp.int32, outs[0].shape, 0)
            o_ref[c * tq:(c + 1) * tq, :] = jnp.where(row < B_HEAD_DIM, outs[0], outs[1]).T.astype(BF16)

    @pl.when(step == 0)
    def _():
        run(True)

    @pl.when(step >= 1)
    def _():
        run(False)


def _attn_specs(batch, seq, nq, tq):
    return dict(
        q=pl.BlockSpec((V7X_LANES, tq), lambda b, h, i: (h, b * nq + i)),
        k=pl.BlockSpec((seq, V7X_LANES), lambda b, h, i: (b, h)),
        v=pl.BlockSpec((V7X_LANES, seq), lambda b, h, i: (h, b)),
        o=pl.BlockSpec((tq, V7X_LANES), lambda b, h, i: (b * nq + i, h)),
    )


def _attention_a(qt, k, vt, xvec, lam, sub_g, batch, seq, lambda_init):
    d, t = qt.shape
    tq = ATTN_TILE
    n_off = xvec.shape[1]
    assert n_off % 2 == 1 and seq % (2 * tq) == 0
    nq = seq // (2 * tq)
    sp = _attn_specs(batch, seq, nq, 2 * tq)
    rows = V7X_LANES + ONES_ROWS
    return pl.pallas_call(
        functools.partial(_attn_a_kernel, n_off=n_off, lambda_init=lambda_init),
        out_shape=jax.ShapeDtypeStruct((t, d), BF16),
        grid=(batch, d // V7X_LANES, nq),
        in_specs=[sp["q"],
                  pl.BlockSpec((V7X_LANES, 2 * tq), lambda b, h, i: (h, b * nq + jnp.minimum(i + 1, nq - 1))),
                  sp["k"], sp["v"],
                  pl.BlockSpec((None,) + xvec.shape[1:], lambda b, h, i: (h, 0, 0, 0)),
                  pl.BlockSpec(lam.shape, lambda b, h, i: (0, 0)),
                  pl.BlockSpec(sub_g.shape, lambda b, h, i: (0, 0))],
        out_specs=sp["o"],
        scratch_shapes=[pltpu.VMEM((n_off, tq, tq), F32), pltpu.VMEM((2, 2, 2, tq, tq), F32),
                        pltpu.VMEM((n_off, 2, 2, tq, tq), F32),
                        pltpu.VMEM((2, 2, 1, tq), F32), pltpu.VMEM((2, 2, rows, tq), F32)],
        compiler_params=_cparams(("parallel", "parallel", "arbitrary")),
        name="diff_attention",
    )(qt, qt, k, vt, xvec, lam, sub_g)


def _attention_b(qt, k, vt, xvec, batch, seq):
    d, t = qt.shape
    tq = BAND_TILE
    g = BAND_QBLOCKS
    n_off = xvec.shape[1] // 2
    assert g >= n_off - 1 and seq % (g * tq) == 0
    nq = seq // (g * tq)
    sp = _attn_specs(batch, seq, nq, g * tq)
    return pl.pallas_call(
        _attn_b_kernel,
        out_shape=jax.ShapeDtypeStruct((t, d), BF16),
        grid=(batch, d // V7X_LANES, nq),
        in_specs=[sp["q"], sp["k"], sp["v"],
                  pl.BlockSpec((None,) + xvec.shape[1:], lambda b, h, i: (h, 0, 0, 0))],
        out_specs=sp["o"],
        scratch_shapes=[pltpu.VMEM((2 * n_off, tq, tq), F32), pltpu.VMEM((n_off * g, 2, tq, tq), F32)],
        compiler_params=_cparams(("parallel", "parallel", "arbitrary")),
        name="chunk_attention",
    )(qt, k, vt, xvec)


def _t5_bucket(rel):
    nb = T5_BUCKETS // 2
    ret = jnp.where(rel > 0, nb, 0)
    n = jnp.abs(rel)
    max_exact = nb // 2
    nf = jnp.maximum(n, 1).astype(F32)
    large = max_exact + (jnp.log(nf / max_exact) / math.log(T5_MAX_DIST / max_exact)
                         * (nb - max_exact)).astype(jnp.int32)
    large = jnp.minimum(large, nb - 1)
    return ret + jnp.where(n < max_exact, n, large)


def _t5_const_distance():
    nb = T5_BUCKETS // 2
    max_exact = nb // 2
    n = np.arange(max_exact, 4 * T5_MAX_DIST, dtype=np.float64)
    large = max_exact + np.floor(np.log(n / max_exact) / math.log(T5_MAX_DIST / max_exact) * (nb - max_exact))
    below = np.nonzero(large < nb - 1)[0]
    return int(n[below[-1]]) + 2


def _tile_rel(tile, n_off):
    i = jnp.arange(2 * tile, dtype=jnp.int32)
    rel = jnp.where(i < tile, -i, 2 * tile - i)
    return rel[None, :] - tile * jnp.arange(n_off, dtype=jnp.int32)[:, None]


def _t5_vectors(t5_bias, tile):
    n_off = 1
    while (n_off - 1) * tile + 1 < _t5_const_distance():
        n_off += 1
    tb = t5_bias.astype(F32)
    vals = tb[_t5_bucket(_tile_rel(tile, n_off))] - tb[T5_BUCKETS // 2 - 1]
    return vals.transpose(2, 0, 1)[:, :, None, :]


def _band_vectors(rel_bias, tile):
    n_off = LEFT_CHUNKS * CHUNK // tile + 1
    idx = jnp.clip(_tile_rel(tile, n_off), -MAX_REL, MAX_REL) + MAX_REL
    vals = rel_bias.astype(F32)[:, idx]
    return vals.reshape(rel_bias.shape[0] // 2, 2 * n_off, 1, 2 * tile)


def _top2_sum4(r0, r1, r2, r3):
    a, b = jnp.maximum(r0, r1), jnp.minimum(r0, r1)
    c, d = jnp.maximum(r2, r3), jnp.minimum(r2, r3)
    return jnp.maximum(a, c) + jnp.maximum(jnp.minimum(a, c), jnp.maximum(b, d))


def _route(logits, rbias, before_ref):
    n_e, n = logits.shape
    scores = 1.0 / (1.0 + jnp.exp(-logits))
    sel = scores + rbias
    row = lax.broadcasted_iota(jnp.int32, sel.shape, 0)
    best = None
    for g in range(N_GROUPS):
        rows = [sel[g * E_PER_GROUP + i: g * E_PER_GROUP + i + 1, :] for i in range(E_PER_GROUP)]
        gs = _top2_sum4(*rows)
        if best is None:
            best, gidx = gs, jnp.zeros(gs.shape, jnp.int32)
        else:
            gidx = jnp.where(gs > best, g, gidx)
            best = jnp.maximum(best, gs)
    masked = jnp.where(row // E_PER_GROUP == gidx, sel, NEG_INF)
    m1 = jnp.max(masked, axis=0, keepdims=True)
    i1 = jnp.min(jnp.where(masked == m1, row, n_e), axis=0, keepdims=True)
    masked2 = jnp.where(row == i1, -3.0e38, masked)
    m2 = jnp.max(masked2, axis=0, keepdims=True)
    i2 = jnp.min(jnp.where(masked2 == m2, row, n_e), axis=0, keepdims=True)
    w1 = jnp.sum(jnp.where(row == i1, scores, 0.0), axis=0, keepdims=True)
    w2 = jnp.sum(jnp.where(row == i2, scores, 0.0), axis=0, keepdims=True)
    den = w1 + w2
    member = jnp.where((row == i1) | (row == i2), 1.0, 0.0).astype(BF16)
    before = before_ref[...]
    pos = jnp.concatenate(
        [jnp.dot(member[:, i:i + MOE_TILE], before, preferred_element_type=F32)
         for i in range(0, n, MOE_TILE)], axis=1)
    p1 = jnp.sum(jnp.where(row == i1, pos, 0.0), axis=0, keepdims=True)
    p2 = jnp.sum(jnp.where(row == i2, pos, 0.0), axis=0, keepdims=True)
    out_row = lax.broadcasted_iota(jnp.int32, (8, n), 0)
    out = jnp.zeros((8, n), F32)
    for r, val in enumerate((i1.astype(F32), i2.astype(F32), p1, p2, w1 / den, w2 / den)):
        out = jnp.where(out_row == r, val, out)
    return out


def _oproj_kernel(o_ref, wo_ref, x_ref, g1_ref, gn_ref, sc_ref, sh_ref, rwt_ref, rb_ref,
                  xo_ref, h_ref, route_ref, before_ref):
    @pl.when(pl.program_id(0) == 0)
    def _():
        t_from = lax.broadcasted_iota(jnp.int32, before_ref.shape, 0)
        t_to = lax.broadcasted_iota(jnp.int32, before_ref.shape, 1)
        before_ref[...] = jnp.where(t_from < t_to, 1.0, 0.0).astype(BF16)

    n_e = rb_ref.shape[0]
    rw = rwt_ref[...]
    rows = o_ref.shape[0] // OPROJ_SPLIT
    groups = [slice(r * rows, (r + 1) * rows) for r in range(OPROJ_SPLIT)]
    ys = [jnp.dot(o_ref[g, :], wo_ref[...], preferred_element_type=F32) for g in groups]
    logits = []
    for g, y in zip(groups, ys):
        xn = x_ref[g, :] + g1_ref[...] * y
        xo_ref[g, :] = xn
        h = _norm_mod(xn, gn_ref[...], sc_ref[...], sh_ref[...])
        h_hi = h.astype(BF16)
        h_ref[g, :] = h_hi
        h_lo = (h - h_hi.astype(F32)).astype(BF16)
        main = lax.dot_general(rw, h_hi, _NT, preferred_element_type=F32)
        corr = lax.dot_general(rw[:n_e], h_lo, _NT, preferred_element_type=F32)
        logits.append(main[:n_e] + main[n_e:] + corr)
    route_ref[...] = _route(jnp.concatenate(logits, axis=1), rb_ref[...], before_ref)


def _out_proj(o, wo_bf16, x2, g1, gn, sc, sh, rwt, rb, seq):
    t, d = x2.shape
    tm = OPROJ_ROW_TILE
    per_b = seq // tm
    n_e = rb.shape[0]
    vec = lambda i: (i // per_b, 0, 0)
    return pl.pallas_call(
        _oproj_kernel,
        out_shape=(jax.ShapeDtypeStruct((t, d), F32),
                   jax.ShapeDtypeStruct((t, d), BF16),
                   jax.ShapeDtypeStruct((8, t), F32)),
        grid=(t // tm,),
        in_specs=[
            pl.BlockSpec((tm, d), lambda i: (i, 0)),
            pl.BlockSpec((d, d), lambda i: (0, 0)),
            pl.BlockSpec((tm, d), lambda i: (i, 0)),
            pl.BlockSpec((None, 1, d), vec),
            pl.BlockSpec((1, d), lambda i: (0, 0)),
            pl.BlockSpec((None, 1, d), vec),
            pl.BlockSpec((None, 1, d), vec),
            pl.BlockSpec((2 * n_e, d), lambda i: (0, 0)),
            pl.BlockSpec((n_e, 1), lambda i: (0, 0)),
        ],
        out_specs=(pl.BlockSpec((tm, d), lambda i: (i, 0)),
                   pl.BlockSpec((tm, d), lambda i: (i, 0)),
                   pl.BlockSpec((8, tm), lambda i: (0, i))),
        scratch_shapes=[pltpu.VMEM((MOE_TILE, MOE_TILE), BF16)],
        compiler_params=_cparams(("arbitrary",)),
        name="out_proj_route",
    )(o, wo_bf16, x2, g1, gn, sc, sh, rwt, rb)


def _moe_plan(route, t, tm, n_e):
    n_tiles = t // tm
    ch, ft = MOE_CHUNK, MOE_FFN_TILE
    loc_rows, main_rows, n_sorted = _moe_rows(t, tm, n_e)
    ids = jnp.arange(n_e, dtype=jnp.int32)
    e = route[0:2].astype(jnp.int32)
    pos = route[2:4].astype(jnp.int32)
    oh = (e[:, :, None] == ids).astype(jnp.int32)
    cnt = oh.sum(0).reshape(n_tiles, tm, n_e).sum(1)
    seg = (cnt + ch - 1) // ch * ch
    loc = jnp.cumsum(seg, axis=1) - seg
    tot = seg.sum(0)
    totp = (tot + ft - 1) // ft * ft
    base = jnp.cumsum(totp) - totp
    gdest = base[None, :] + jnp.cumsum(seg, axis=0) - seg
    col = (oh * jnp.repeat(loc, tm, axis=0)[None]).sum(-1) + pos

    row0 = jnp.arange(loc_rows // ch, dtype=jnp.int32) * ch
    ej = (row0[None, :, None] >= (loc + seg)[:, None, :]).sum(-1)
    ohj = (jnp.minimum(ej, n_e - 1)[..., None] == ids).astype(jnp.int32)
    dst = (ohj * (gdest - loc)[:, None, :]).sum(-1) + row0[None, :]
    dump = main_rows + (jnp.arange(n_tiles, dtype=jnp.int32) % 2)[:, None] * loc_rows + row0[None, :]
    dst = jnp.where(ej < n_e, dst, dump) // ch

    r0 = jnp.arange(n_sorted // ft, dtype=jnp.int32) * ft
    ends = base + totp
    te = jnp.minimum((r0[:, None] >= ends[None, :]).sum(-1), n_e - 1)
    tvalid = (r0 < ends[-1]).astype(jnp.int32)
    tfirst = tvalid * (r0 == (((te[:, None] == ids) * base[None, :]).sum(-1))).astype(jnp.int32)
    return col, dst.astype(jnp.int32), te.astype(jnp.int32), tvalid, tfirst


def _moe_rows(t, tm, n_e):
    ch, ft = MOE_CHUNK, MOE_FFN_TILE
    loc_rows = 2 * tm + n_e * ch
    main = 2 * t + (t // tm) * n_e * (ch - 1) + n_e * (ft - 1)
    main = (main + ft - 1) // ft * ft
    return loc_rows, main, (main + 2 * loc_rows + ft - 1) // ft * ft


def _chunk_copies(dst_ref, tile, local_ref, sorted_ref, sem, to_sorted):
    ch = MOE_CHUNK
    copies = []
    for j in range(local_ref.shape[0] // ch):
        far = sorted_ref.at[pl.ds(pl.multiple_of(dst_ref[tile, j] * ch, ch), ch)]
        near = local_ref.at[pl.ds(j * ch, ch)]
        copies.append(pltpu.make_async_copy(near, far, sem) if to_sorted
                      else pltpu.make_async_copy(far, near, sem))
    return copies


def _dispatch_kernel(dst_ref, col_ref, h_ref, init_ref, xs_ref, comp, sem):
    del init_ref
    tile = pl.program_id(0)
    last = pl.num_programs(0) - 1
    slot = tile % 2

    def wait_slot(s):
        pltpu.make_async_copy(comp.at[s], xs_ref.at[pl.ds(0, comp.shape[1])], sem.at[s]).wait()

    @pl.when(tile >= 2)
    def _():
        wait_slot(slot)

    col = col_ref[...]
    srow = lax.broadcasted_iota(jnp.int32, (comp.shape[1], col.shape[1]), 0)
    sel = jnp.where((srow == col[0:1]) | (srow == col[1:2]), 1.0, 0.0).astype(BF16)
    comp[slot] = jnp.dot(sel, h_ref[...], preferred_element_type=F32).astype(BF16)
    for cp in _chunk_copies(dst_ref, tile, comp.at[slot], xs_ref, sem.at[slot], True):
        cp.start()

    @pl.when(tile == last)
    def _():
        wait_slot(slot)

    @pl.when((tile == last) & (tile >= 1))
    def _():
        wait_slot(1 - slot)


def _dispatch(dst, col, h, n_e, init):
    t, d = h.shape
    tm = MOE_TILE
    loc_rows, _, n_sorted = _moe_rows(t, tm, n_e)
    if init is None:
        init = jnp.zeros((n_sorted, d), BF16)
    return pl.pallas_call(
        _dispatch_kernel,
        out_shape=jax.ShapeDtypeStruct((n_sorted, d), BF16),
        grid_spec=pltpu.PrefetchScalarGridSpec(
            num_scalar_prefetch=1,
            grid=(t // tm,),
            in_specs=[pl.BlockSpec((2, tm), lambda i, dst: (0, i)),
                      pl.BlockSpec((tm, d), lambda i, dst: (i, 0)),
                      pl.BlockSpec(memory_space=pl.ANY)],
            out_specs=pl.BlockSpec(memory_space=pl.ANY),
            scratch_shapes=[pltpu.VMEM((2, loc_rows, d), BF16), pltpu.SemaphoreType.DMA((2,))],
        ),
        input_output_aliases={3: 0},
        compiler_params=_cparams(("arbitrary",)),
        name="moe_dispatch",
    )(dst, col, h, init)


def _expert_kernel(te_ref, tv_ref, tf_ref, xs_ref, wg_ref, wu_ref, wd_ref, ys_ref, wg_b, wu_b, wd_b):
    del te_ref
    s = pl.program_id(0)

    @pl.when(tf_ref[s] == 1)
    def _():
        wg_b[...] = wg_ref[...].astype(BF16)
        wu_b[...] = wu_ref[...].astype(BF16)
        wd_b[...] = wd_ref[...].astype(BF16)

    @pl.when(tv_ref[s] == 1)
    def _():
        rows = ys_ref.shape[0] // MOE_FFN_SPLIT
        hs = []
        for r in range(MOE_FFN_SPLIT):
            xs = xs_ref[r * rows:(r + 1) * rows, :]
            hg = jnp.dot(xs, wg_b[...], preferred_element_type=F32)
            hu = jnp.dot(xs, wu_b[...], preferred_element_type=F32)
            hs.append((hg, hu))
        for r, (hg, hu) in enumerate(hs):
            he = hg * (1.0 / (1.0 + jnp.exp(-hg))) * hu
            ys_ref[r * rows:(r + 1) * rows, :] = jnp.dot(
                he.astype(BF16), wd_b[...], preferred_element_type=F32).astype(BF16)

    @pl.when(tv_ref[s] == 0)
    def _():
        ys_ref[...] = jnp.zeros(ys_ref.shape, ys_ref.dtype)


def _experts(te, tvalid, tfirst, xs, wg, wu, wd, layer):
    n_sorted, d = xs.shape
    f = wg.shape[3]
    ft = MOE_FFN_TILE
    wspec = lambda r, c: pl.BlockSpec((None, None, r, c), lambda s, te, tv, tf: (layer, te[s], 0, 0))
    return pl.pallas_call(
        _expert_kernel,
        out_shape=jax.ShapeDtypeStruct((n_sorted, d), BF16),
        grid_spec=pltpu.PrefetchScalarGridSpec(
            num_scalar_prefetch=3,
            grid=(n_sorted // ft,),
            in_specs=[pl.BlockSpec((ft, d), lambda s, te, tv, tf: (s, 0)),
                      wspec(d, f), wspec(d, f), wspec(f, d)],
            out_specs=pl.BlockSpec((ft, d), lambda s, te, tv, tf: (s, 0)),
            scratch_shapes=[pltpu.VMEM((d, f), BF16), pltpu.VMEM((d, f), BF16), pltpu.VMEM((f, d), BF16)],
        ),
        compiler_params=_cparams(("arbitrary",)),
        name="moe_experts",
    )(te, tvalid, tfirst, xs, wg, wu, wd)


def _combine_kernel(dst_ref, col_ref, w_ref, ys_ref, x_ref, g2_ref, o_ref, comp, sem):
    tile = pl.program_id(0)
    slot = tile % 2

    def fetch(t, s):
        for cp in _chunk_copies(dst_ref, t, comp.at[s], ys_ref, sem.at[s], False):
            cp.start()

    def wait_slot(s):
        pltpu.make_async_copy(ys_ref.at[pl.ds(0, comp.shape[1])], comp.at[s], sem.at[s]).wait()

    last = pl.num_programs(0) - 1

    @pl.when(tile == 0)
    def _():
        fetch(tile, slot)

    fetch(jnp.minimum(tile + 1, last), 1 - slot)

    col = col_ref[...]
    w = w_ref[...]
    scol = lax.broadcasted_iota(jnp.int32, (col.shape[0], comp.shape[1]), 1)
    selw = (jnp.where(scol == col[:, 0:1], w[:, 4:5], 0.0)
            + jnp.where(scol == col[:, 1:2], w[:, 5:6], 0.0)).astype(BF16)
    wait_slot(slot)
    acc = jnp.dot(selw, comp[slot], preferred_element_type=F32)
    o_ref[...] = x_ref[...] + g2_ref[...] * acc

    @pl.when(tile == last)
    def _():
        wait_slot(1 - slot)


def _combine(dst, col_t, route_t, ys, x2, g2, seq, n_e):
    t, d = x2.shape
    tm = MOE_TILE
    per_b = seq // tm
    loc_rows = _moe_rows(t, tm, n_e)[0]
    return pl.pallas_call(
        _combine_kernel,
        out_shape=jax.ShapeDtypeStruct((t, d), F32),
        grid_spec=pltpu.PrefetchScalarGridSpec(
            num_scalar_prefetch=1,
            grid=(t // tm,),
            in_specs=[pl.BlockSpec((tm, 2), lambda i, dst: (i, 0)),
                      pl.BlockSpec((tm, 8), lambda i, dst: (i, 0)),
                      pl.BlockSpec(memory_space=pl.ANY),
                      pl.BlockSpec((tm, d), lambda i, dst: (i, 0)),
                      pl.BlockSpec((None, 1, d), lambda i, dst: (i // per_b, 0, 0))],
            out_specs=pl.BlockSpec((tm, d), lambda i, dst: (i, 0)),
            scratch_shapes=[pltpu.VMEM((2, loc_rows, d), BF16), pltpu.SemaphoreType.DMA((2,))],
        ),
        compiler_params=_cparams(("arbitrary",)),
        name="moe_combine",
    )(dst, col_t, route_t, ys, x2, g2)


def _moe(h, route, wg, wu, wd, layer, x2, g2, seq, sorted_init):
    t = x2.shape[0]
    n_e = wg.shape[1]
    col, dst, te, tvalid, tfirst = _moe_plan(route, t, MOE_TILE, n_e)
    xs = _dispatch(dst, col, h, n_e, sorted_init)
    ys = _experts(te, tvalid, tfirst, xs, wg, wu, wd, layer)
    return _combine(dst, col.T, route.T, ys, x2, g2, seq, n_e), ys


def kernel(x, c, ada_w, ada_b, norm_mix_g, norm_ffn_g, t5_bias, a_w_qkv, a_q_gain, a_k_gain, a_lambda, a_subln_g, a_w_o, b_w_qkv, b_q_gain, b_k_gain, b_rel_bias, b_w_o, router_w, router_bias, moe_w_gate, moe_w_up, moe_w_down):
    batch, seq, d = x.shape
    depth = ada_w.shape[0]
    assert seq % ATTN_TILE == 0 and seq % ROW_TILE == 0 and d == A_HEADS * 2 * A_HEAD_DIM
    assert d == B_HEADS * B_HEAD_DIM and A_HEAD_DIM == B_HEAD_DIM

    c_pad = jnp.zeros((8, d), F32).at[:batch].set(c.astype(F32))
    mod = _modulation(c_pad, ada_w.astype(F32), ada_b.astype(F32))[:, :batch]
    mod = mod.reshape(depth, batch, 6, 1, d)

    rwt = router_w.astype(F32).T
    rwt_hi = lax.bitcast_convert_type(
        lax.bitcast_convert_type(rwt, jnp.uint32) & jnp.uint32(0xFFFF0000), F32)
    rwt = jnp.concatenate([rwt_hi, rwt - rwt_hi], axis=0).astype(BF16)
    rb = router_bias.astype(F32).reshape(-1, 1)

    x2 = x.astype(F32).reshape(batch * seq, d)
    sorted_buf = None
    for i in range(depth):
        sh1, sc1, g1, sh2, sc2, g2 = [mod[i, :, k] for k in range(6)]
        j = i // 2
        if i % 2 == 0:
            w_qkv, qg, kg, w_o = a_w_qkv[j], a_q_gain[j], a_k_gain[j], a_w_o[j]
        else:
            w_qkv, qg, kg, w_o = b_w_qkv[j], b_q_gain[j], b_k_gain[j], b_w_o[j]
        n_rep = d // qg.shape[0]
        gq = (jnp.tile(qg.astype(F32), n_rep) * (A_HEAD_DIM ** -0.5 * LOG2E)).reshape(d, 1)
        gk = jnp.tile(kg.astype(F32), n_rep).reshape(1, d)
        qt, k, vt = _qkv_proj(x2, norm_mix_g[i].reshape(1, d), sc1, sh1, w_qkv.astype(BF16),
                              gq, gk, seq, A_HEAD_DIM)
        if i % 2 == 0:
            lambda_init = 0.8 - 0.6 * math.exp(-0.3 * i)
            o = _attention_a(qt, k, vt, _t5_vectors(t5_bias, ATTN_TILE), a_lambda[j].astype(F32),
                             a_subln_g[j].reshape(1, -1), batch, seq, lambda_init)
        else:
            o = _attention_b(qt, k, vt, _band_vectors(b_rel_bias[j], BAND_TILE), batch, seq)
        x2, h, route = _out_proj(o, w_o.astype(BF16), x2, g1, norm_ffn_g[i].reshape(1, d),
                                 sc2, sh2, rwt, rb, seq)
        x2, sorted_buf = _moe(h, route, moe_w_gate, moe_w_up, moe_w_down, i, x2, g2, seq, sorted_buf)
    return x2.reshape(batch, seq, d)
```

```python
import functools
import math

import numpy as np
import jax
import jax.numpy as jnp
from jax import lax
from jax.experimental import pallas as pl
from jax.experimental.pallas import tpu as pltpu

F32 = jnp.float32
BF16 = jnp.bfloat16

CHUNK = 64
A_HEADS = 8
A_HEAD_DIM = 64
T5_BUCKETS = 32
T5_MAX_DIST = 1024
B_HEADS = 16
B_HEAD_DIM = 64
LEFT_CHUNKS = 8
MAX_REL = 256
N_EXPERTS = 16
N_GROUPS = 4
E_PER_GROUP = N_EXPERTS // N_GROUPS
NORM_EPS = 1e-6
NEG_INF = -1e30
LOG2E = math.log2(math.e)

V7X_LANES = 128
V7X_MXU_DIM = 256

ATTN_TILE = 512
MOD_COL_TILE = 1536
FAR_UNROLL = 4
BAND_TILE = 256
BAND_QBLOCKS = 16
BAND_AHEAD = 2
ROW_TILE = 512
QKV_SPLIT = 2
MOE_TILE = 512
OPROJ_ROW_TILE = 1024
OPROJ_SPLIT = 4
MOE_CHUNK = 16
MOE_FFN_TILE = 512
MOE_FFN_SPLIT = 2
VMEM_LIMIT = 56 * 1024 * 1024

_NT = (((1,), (1,)), ((), ()))


def _cparams(sem):
    return pltpu.CompilerParams(dimension_semantics=sem, vmem_limit_bytes=VMEM_LIMIT)


def _mod_kernel(c_ref, w_ref, b_ref, o_ref):
    c = c_ref[...]
    s = c * (1.0 / (1.0 + jnp.exp(-c)))
    w = w_ref[...]
    s_hi, w_hi = s.astype(BF16), w.astype(BF16)
    s_lo = (s - s_hi.astype(F32)).astype(BF16)
    w_lo = (w - w_hi.astype(F32)).astype(BF16)
    dot = functools.partial(jnp.dot, preferred_element_type=F32)
    o_ref[...] = dot(s_hi, w_hi) + dot(s_hi, w_lo) + dot(s_lo, w_hi) + b_ref[...]


def _modulation(c_pad, ada_w, ada_b):
    depth, d, n = ada_w.shape
    rows = c_pad.shape[0]
    tn = MOD_COL_TILE
    return pl.pallas_call(
        _mod_kernel,
        out_shape=jax.ShapeDtypeStruct((depth, rows, n), F32),
        grid=(depth, n // tn),
        in_specs=[
            pl.BlockSpec((rows, d), lambda i, j: (0, 0)),
            pl.BlockSpec((None, d, tn), lambda i, j: (i, 0, j)),
            pl.BlockSpec((None, 1, tn), lambda i, j: (i, 0, j)),
        ],
        out_specs=pl.BlockSpec((None, rows, tn), lambda i, j: (i, 0, j)),
        compiler_params=_cparams(("parallel", "parallel")),
        name="adaln_mod",
    )(c_pad, ada_w, ada_b.reshape(depth, 1, n))


def _norm_mod(x, g, sc, sh):
    ms = jnp.mean(x * x, axis=-1, keepdims=True)
    return x * lax.rsqrt(ms + NORM_EPS) * g * (1.0 + sc) + sh


def _qkv_kernel(x_ref, g_ref, sc_ref, sh_ref, w_ref, gq_ref, gk_ref,
                qt_ref, k_ref, vt_ref, wqt_ref, wvt_ref, *, head_dim):
    d, tm = qt_ref.shape

    @pl.when(pl.program_id(0) == 0)
    def _():
        wqt_ref[...] = w_ref[:, :d].T
        wvt_ref[...] = w_ref[:, 2 * d:].T

    cw = V7X_MXU_DIM
    r = lax.broadcasted_iota(jnp.int32, (cw, cw), 0) // head_dim
    c = lax.broadcasted_iota(jnp.int32, (cw, cw), 1) // head_dim
    gmat = jnp.where(r == c, 1.0 / head_dim, 0.0).astype(BF16)
    tg = tm // QKV_SPLIT
    for gi in range(QKV_SPLIT):
        rows = slice(gi * tg, (gi + 1) * tg)
        h = _norm_mod(x_ref[rows, :], g_ref[...], sc_ref[...], sh_ref[...]).astype(BF16)
        y = lax.dot_general(wqt_ref[...], h, _NT, preferred_element_type=F32)
        y3 = y.reshape(d // head_dim, head_dim, tg)
        ss = jnp.mean(y3 * y3, axis=1, keepdims=True)
        qt_ref[:, rows] = ((y3 * lax.rsqrt(ss + NORM_EPS)).reshape(d, tg) * gq_ref[...]).astype(BF16)
        y = jnp.dot(h, w_ref[:, d:2 * d], preferred_element_type=F32)
        ysq = (y * y).astype(BF16)
        for ci in range(d // cw):
            cols = slice(ci * cw, (ci + 1) * cw)
            ss = jnp.dot(ysq[:, cols], gmat, preferred_element_type=F32)
            k_ref[rows, cols] = (y[:, cols] * lax.rsqrt(ss + NORM_EPS) * gk_ref[:, cols]).astype(BF16)
        vt_ref[:, rows] = lax.dot_general(wvt_ref[...], h, _NT, preferred_element_type=F32).astype(BF16)


def _qkv_proj(x2, g, sc, sh, w, gq, gk, seq, head_dim):
    t, d = x2.shape
    tm = ROW_TILE * QKV_SPLIT
    per_b = seq // tm
    vec = lambda i: (i // per_b, 0, 0)
    full = lambda i: (0, 0)
    return pl.pallas_call(
        functools.partial(_qkv_kernel, head_dim=head_dim),
        out_shape=(jax.ShapeDtypeStruct((d, t), BF16),
                   jax.ShapeDtypeStruct((t, d), BF16),
                   jax.ShapeDtypeStruct((d, t), BF16)),
        grid=(t // tm,),
        in_specs=[
            pl.BlockSpec((tm, d), lambda i: (i, 0)),
            pl.BlockSpec((1, d), full),
            pl.BlockSpec((None, 1, d), vec),
            pl.BlockSpec((None, 1, d), vec),
            pl.BlockSpec((d, 3 * d), full),
            pl.BlockSpec((d, 1), full),
            pl.BlockSpec((1, d), full),
        ],
        out_specs=(pl.BlockSpec((d, tm), lambda i: (0, i)),
                   pl.BlockSpec((tm, d), lambda i: (i, 0)),
                   pl.BlockSpec((d, tm), lambda i: (0, i))),
        scratch_shapes=[pltpu.VMEM((d, d), BF16), pltpu.VMEM((d, d), BF16)],
        compiler_params=_cparams(("arbitrary",)),
        name="qkv_proj",
    )(x2, g, sc, sh, w, gq, gk)


def _softmax_pv(s, vt1, m_ref, acc_ref):
    m_prev = m_ref[...]
    m_new = jnp.maximum(m_prev, jnp.max(s, axis=0, keepdims=True))
    alpha = jnp.exp2(m_prev - m_new)
    p = jnp.exp2(s - m_new).astype(BF16)
    acc_ref[...] = alpha * acc_ref[...] + jnp.dot(vt1, p, preferred_element_type=F32)
    m_ref[...] = m_new


def _split_maps(qt, head_dim):
    row = lax.broadcasted_iota(jnp.int32, qt.shape, 0)
    zero = jnp.zeros_like(qt)
    return jnp.where(row < head_dim, qt, zero), jnp.where(row >= head_dim, qt, zero)


ONES_ROWS = 16


def _init_state(refs):
    for m_ref, acc_ref in refs:
        m_ref[...] = jnp.full(m_ref.shape, NEG_INF, F32)
        acc_ref[...] = jnp.zeros(acc_ref.shape, F32)


def _normalized(acc_ref):
    acc = acc_ref[...]
    return acc[:V7X_LANES] / acc[V7X_LANES:V7X_LANES + 1]


def _toeplitz(x_row, n):
    x = jnp.broadcast_to(x_row, (n, x_row.shape[1]))
    return pltpu.roll(x, 0, 1, stride=1, stride_axis=0)[:, :n]


def _chunk_delta(n):
    kc = lax.broadcasted_iota(jnp.int32, (n, n), 0) // CHUNK
    qc = lax.broadcasted_iota(jnp.int32, (n, n), 1) // CHUNK
    return qc - kc


def _attn_a_kernel(qt_ref, qn_ref, k_ref, vt_ref, x_ref, lam_ref, sub_ref, o_ref,
                   tab, fbuf, nbuf, m_st, acc_st, *, n_off, lambda_init):
    tq = ATTN_TILE
    step = pl.program_id(2)

    @pl.when(step == 0)
    def _():
        for o in range(n_off):
            t = _toeplitz(x_ref[o], tq) * LOG2E
            if o == 0:
                t = jnp.where(_chunk_delta(tq) >= 0, t, NEG_INF)
            tab[o] = t

    for c in range(2):
        _init_state(((m_st.at[c, 0], acc_st.at[c, 0]), (m_st.at[c, 1], acc_st.at[c, 1])))
    q_maps = [_split_maps(qt_ref[:, c * tq:(c + 1) * tq], A_HEAD_DIM) for c in range(2)]

    def k_tile(j):
        return k_ref[pl.ds(pl.multiple_of(j * tq, tq), tq), :]

    def vt1_tile(j):
        vt = vt_ref[:, pl.ds(pl.multiple_of(j * tq, tq), tq)]
        return jnp.concatenate([vt, jnp.ones((ONES_ROWS, tq), vt.dtype)], axis=0)

    def scores(buf, k, c):
        for m in range(2):
            buf[c, m] = jnp.dot(k, q_maps[c][m], preferred_element_type=F32)

    def consume(buf, vt1, c, o):
        for m in range(2):
            s = buf[c, m]
            _softmax_pv(s if o >= n_off else s + tab[o], vt1, m_st.at[c, m], acc_st.at[c, m])

    def near_block(first_tile, n_tiles, first_scored):
        units = [[c for c in range(2) if n_tiles - 2 + c - s >= 0] for s in range(n_tiles)]
        buf_of = lambda s: fbuf.at[0] if s == 0 else nbuf.at[s - 1]

        for s in range(1 if first_scored else 0, n_tiles):
            k = k_tile(first_tile + s)
            for c in units[s]:
                scores(buf_of(s), k, c)
        for s in range(n_tiles):
            vt1 = vt1_tile(first_tile + s)
            for c in units[s]:
                consume(buf_of(s), vt1, c, n_tiles - 2 + c - s)

    first_general = (n_off - 1) // 2
    for i0 in range(first_general):
        @pl.when(step == i0)
        def _():
            near_block(0, 2 * i0 + 2, False)

    def far_step(j, parity):
        k = k_tile(j + 1)
        for c in range(2):
            scores(fbuf.at[1 - parity], k, c)
        vt1 = vt1_tile(j)
        for c in range(2):
            consume(fbuf.at[parity], vt1, c, n_off)

    n_far = jnp.maximum(2 * step - (n_off - 1), 0)
    rem = n_far % FAR_UNROLL

    @pl.when(rem >= 2)
    def _():
        far_step(0, 0)
        far_step(1, 1)

    def far_trip(t, carry):
        for u in range(FAR_UNROLL):
            far_step(FAR_UNROLL * t + rem + u, u % 2)
        return carry
    lax.fori_loop(0, n_far // FAR_UNROLL, far_trip, 0)

    @pl.when(step >= first_general)
    def _():
        near_block(n_far, n_off + 1, True)

    def epilogue():
        lam = lam_ref[...]
        lam_full = (jnp.exp(jnp.sum(lam[0:1] * lam[1:2], axis=-1, keepdims=True))
                    - jnp.exp(jnp.sum(lam[2:3] * lam[3:4], axis=-1, keepdims=True)) + lambda_init)
        for c in range(2):
            a = _normalized(acc_st.at[c, 0]) - lam_full * _normalized(acc_st.at[c, 1])
            ms = jnp.mean(a * a, axis=0, keepdims=True)
            an = a * lax.rsqrt(ms + NORM_EPS)
            o_ref[c * tq:(c + 1) * tq, :] = (an.T * (sub_ref[...] * (1.0 - lambda_init))).astype(BF16)

    last = pl.num_programs(2) - 1

    @pl.when(step < last)
    def _():
        k = k_tile(0)
        for c in range(2):
            qa, qb = _split_maps(qn_ref[:, c * tq:(c + 1) * tq], A_HEAD_DIM)
            fbuf[0, c, 0] = jnp.dot(k, qa, preferred_element_type=F32)
            fbuf[0, c, 1] = jnp.dot(k, qb, preferred_element_type=F32)
        epilogue()

    @pl.when(step == last)
    def _():
        epilogue()


def _attn_b_kernel(qt_ref, k_ref, vt_ref, x_ref, o_ref, tab, sbuf):
    tq = BAND_TILE
    n_blk = qt_ref.shape[1] // tq
    n_off = tab.shape[0] // 2
    step = pl.program_id(2)

    @pl.when(step == 0)
    def _():
        for m in range(2):
            for o in range(n_off):
                d = _chunk_delta(tq) + o * (tq // CHUNK)
                t = _toeplitz(x_ref[n_off * m + o], tq) * LOG2E
                tab[n_off * m + o] = jnp.where((d >= 0) & (d <= LEFT_CHUNKS), t, NEG_INF)

    q_maps = [_split_maps(qt_ref[:, c * tq:(c + 1) * tq], B_HEAD_DIM) for c in range(n_blk)]

    def run(first_step):
        pairs = [(c, o) for c in range(n_blk) for o in range(n_off - 1, -1, -1)
                 if not first_step or c - o >= 0]
        k_tiles, vt_tiles = {}, {}
        for c, o in pairs:
            if c - o not in k_tiles:
                ks = pl.multiple_of((n_blk * step + c - o) * tq, tq)
                k_tiles[c - o] = k_ref[pl.ds(ks, tq), :]
                vt = vt_ref[:, pl.ds(ks, tq)]
                vt_tiles[c - o] = jnp.concatenate([vt, jnp.ones((ONES_ROWS, tq), vt.dtype)], axis=0)
        def tile_scores(c):
            for o in [o for cc, o in pairs if cc == c]:
                for m in range(2):
                    sbuf[n_off * c + o, m] = jnp.dot(k_tiles[c - o], q_maps[c][m], preferred_element_type=F32)

        for c in range(min(BAND_AHEAD, n_blk)):
            tile_scores(c)
        for c in range(n_blk):
            if c + BAND_AHEAD < n_blk:
                tile_scores(c + BAND_AHEAD)
            outs = []
            for m in range(2):
                offs = [o for cc, o in pairs if cc == c]
                ss = [sbuf[n_off * c + o, m] + tab[n_off * m + o] for o in offs]
                mx = functools.reduce(jnp.maximum, [jnp.max(s, axis=0, keepdims=True) for s in ss])
                acc = sum(jnp.dot(vt_tiles[c - o], jnp.exp2(s - mx).astype(BF16), preferred_element_type=F32)
                          for o, s in zip(offs, ss))
                outs.append(acc[:V7X_LANES] / acc[V7X_LANES:V7X_LANES + 1])
            row = lax.broadcasted_iota(jnp.int32, outs[0].shape, 0)
            o_ref[c * tq:(c + 1) * tq, :] = jnp.where(row < B_HEAD_DIM, outs[0], outs[1]).T.astype(BF16)

    @pl.when(step == 0)
    def _():
        run(True)

    @pl.when(step >= 1)
    def _():
        run(False)


def _attn_specs(batch, seq, nq, tq):
    return dict(
        q=pl.BlockSpec((V7X_LANES, tq), lambda b, h, i: (h, b * nq + i)),
        k=pl.BlockSpec((seq, V7X_LANES), lambda b, h, i: (b, h)),
        v=pl.BlockSpec((V7X_LANES, seq), lambda b, h, i: (h, b)),
        o=pl.BlockSpec((tq, V7X_LANES), lambda b, h, i: (b * nq + i, h)),
    )


def _attention_a(qt, k, vt, xvec, lam, sub_g, batch, seq, lambda_init):
    d, t = qt.shape
    tq = ATTN_TILE
    n_off = xvec.shape[1]
    assert n_off % 2 == 1 and seq % (2 * tq) == 0
    nq = seq // (2 * tq)
    sp = _attn_specs(batch, seq, nq, 2 * tq)
    rows = V7X_LANES + ONES_ROWS
    return pl.pallas_call(
        functools.partial(_attn_a_kernel, n_off=n_off, lambda_init=lambda_init),
        out_shape=jax.ShapeDtypeStruct((t, d), BF16),
        grid=(batch, d // V7X_LANES, nq),
        in_specs=[sp["q"],
                  pl.BlockSpec((V7X_LANES, 2 * tq), lambda b, h, i: (h, b * nq + jnp.minimum(i + 1, nq - 1))),
                  sp["k"], sp["v"],
                  pl.BlockSpec((None,) + xvec.shape[1:], lambda b, h, i: (h, 0, 0, 0)),
                  pl.BlockSpec(lam.shape, lambda b, h, i: (0, 0)),
                  pl.BlockSpec(sub_g.shape, lambda b, h, i: (0, 0))],
        out_specs=sp["o"],
        scratch_shapes=[pltpu.VMEM((n_off, tq, tq), F32), pltpu.VMEM((2, 2, 2, tq, tq), F32),
                        pltpu.VMEM((n_off, 2, 2, tq, tq), F32),
                        pltpu.VMEM((2, 2, 1, tq), F32), pltpu.VMEM((2, 2, rows, tq), F32)],
        compiler_params=_cparams(("parallel", "parallel", "arbitrary")),
        name="diff_attention",
    )(qt, qt, k, vt, xvec, lam, sub_g)


def _attention_b(qt, k, vt, xvec, batch, seq):
    d, t = qt.shape
    tq = BAND_TILE
    g = BAND_QBLOCKS
    n_off = xvec.shape[1] // 2
    assert g >= n_off - 1 and seq % (g * tq) == 0
    nq = seq // (g * tq)
    sp = _attn_specs(batch, seq, nq, g * tq)
    return pl.pallas_call(
        _attn_b_kernel,
        out_shape=jax.ShapeDtypeStruct((t, d), BF16),
        grid=(batch, d // V7X_LANES, nq),
        in_specs=[sp["q"], sp["k"], sp["v"],
                  pl.BlockSpec((None,) + xvec.shape[1:], lambda b, h, i: (h, 0, 0, 0))],
        out_specs=sp["o"],
        scratch_shapes=[pltpu.VMEM((2 * n_off, tq, tq), F32), pltpu.VMEM((n_off * g, 2, tq, tq), F32)],
        compiler_params=_cparams(("parallel", "parallel", "arbitrary")),
        name="chunk_attention",
    )(qt, k, vt, xvec)


def _t5_bucket(rel):
    nb = T5_BUCKETS // 2
    ret = jnp.where(rel > 0, nb, 0)
    n = jnp.abs(rel)
    max_exact = nb // 2
    nf = jnp.maximum(n, 1).astype(F32)
    large = max_exact + (jnp.log(nf / max_exact) / math.log(T5_MAX_DIST / max_exact)
                         * (nb - max_exact)).astype(jnp.int32)
    large = jnp.minimum(large, nb - 1)
    return ret + jnp.where(n < max_exact, n, large)


def _t5_const_distance():
    nb = T5_BUCKETS // 2
    max_exact = nb // 2
    n = np.arange(max_exact, 4 * T5_MAX_DIST, dtype=np.float64)
    large = max_exact + np.floor(np.log(n / max_exact) / math.log(T5_MAX_DIST / max_exact) * (nb - max_exact))
    below = np.nonzero(large < nb - 1)[0]
    return int(n[below[-1]]) + 2


def _tile_rel(tile, n_off):
    i = jnp.arange(2 * tile, dtype=jnp.int32)
    rel = jnp.where(i < tile, -i, 2 * tile - i)
    return rel[None, :] - tile * jnp.arange(n_off, dtype=jnp.int32)[:, None]


def _t5_vectors(t5_bias, tile):
    n_off = 1
    while (n_off - 1) * tile + 1 < _t5_const_distance():
        n_off += 1
    tb = t5_bias.astype(F32)
    vals = tb[_t5_bucket(_tile_rel(tile, n_off))] - tb[T5_BUCKETS // 2 - 1]
    return vals.transpose(2, 0, 1)[:, :, None, :]


def _band_vectors(rel_bias, tile):
    n_off = LEFT_CHUNKS * CHUNK // tile + 1
    idx = jnp.clip(_tile_rel(tile, n_off), -MAX_REL, MAX_REL) + MAX_REL
    vals = rel_bias.astype(F32)[:, idx]
    return vals.reshape(rel_bias.shape[0] // 2, 2 * n_off, 1, 2 * tile)


def _top2_sum4(r0, r1, r2, r3):
    a, b = jnp.maximum(r0, r1), jnp.minimum(r0, r1)
    c, d = jnp.maximum(r2, r3), jnp.minimum(r2, r3)
    return jnp.maximum(a, c) + jnp.maximum(jnp.minimum(a, c), jnp.maximum(b, d))


def _route(logits, rbias, before_ref):
    n_e, n = logits.shape
    scores = 1.0 / (1.0 + jnp.exp(-logits))
    sel = scores + rbias
    row = lax.broadcasted_iota(jnp.int32, sel.shape, 0)
    best = None
    for g in range(N_GROUPS):
        rows = [sel[g * E_PER_GROUP + i: g * E_PER_GROUP + i + 1, :] for i in range(E_PER_GROUP)]
        gs = _top2_sum4(*rows)
        if best is None:
            best, gidx = gs, jnp.zeros(gs.shape, jnp.int32)
        else:
            gidx = jnp.where(gs > best, g, gidx)
            best = jnp.maximum(best, gs)
    masked = jnp.where(row // E_PER_GROUP == gidx, sel, NEG_INF)
    m1 = jnp.max(masked, axis=0, keepdims=True)
    i1 = jnp.min(jnp.where(masked == m1, row, n_e), axis=0, keepdims=True)
    masked2 = jnp.where(row == i1, -3.0e38, masked)
    m2 = jnp.max(masked2, axis=0, keepdims=True)
    i2 = jnp.min(jnp.where(masked2 == m2, row, n_e), axis=0, keepdims=True)
    w1 = jnp.sum(jnp.where(row == i1, scores, 0.0), axis=0, keepdims=True)
    w2 = jnp.sum(jnp.where(row == i2, scores, 0.0), axis=0, keepdims=True)
    den = w1 + w2
    member = jnp.where((row == i1) | (row == i2), 1.0, 0.0).astype(BF16)
    before = before_ref[...]
    pos = jnp.concatenate(
        [jnp.dot(member[:, i:i + MOE_TILE], before, preferred_element_type=F32)
         for i in range(0, n, MOE_TILE)], axis=1)
    p1 = jnp.sum(jnp.where(row == i1, pos, 0.0), axis=0, keepdims=True)
    p2 = jnp.sum(jnp.where(row == i2, pos, 0.0), axis=0, keepdims=True)
    out_row = lax.broadcasted_iota(jnp.int32, (8, n), 0)
    out = jnp.zeros((8, n), F32)
    for r, val in enumerate((i1.astype(F32), i2.astype(F32), p1, p2, w1 / den, w2 / den)):
        out = jnp.where(out_row == r, val, out)
    return out


def _oproj_kernel(o_ref, wo_ref, x_ref, g1_ref, gn_ref, sc_ref, sh_ref, rwt_ref, rb_ref,
                  xo_ref, h_ref, route_ref, before_ref):
    @pl.when(pl.program_id(0) == 0)
    def _():
        t_from = lax.broadcasted_iota(jnp.int32, before_ref.shape, 0)
        t_to = lax.broadcasted_iota(jnp.int32, before_ref.shape, 1)
        before_ref[...] = jnp.where(t_from < t_to, 1.0, 0.0).astype(BF16)

    n_e = rb_ref.shape[0]
    rw = rwt_ref[...]
    rows = o_ref.shape[0] // OPROJ_SPLIT
    groups = [slice(r * rows, (r + 1) * rows) for r in range(OPROJ_SPLIT)]
    ys = [jnp.dot(o_ref[g, :], wo_ref[...], preferred_element_type=F32) for g in groups]
    logits = []
    for g, y in zip(groups, ys):
        xn = x_ref[g, :] + g1_ref[...] * y
        xo_ref[g, :] = xn
        h = _norm_mod(xn, gn_ref[...], sc_ref[...], sh_ref[...])
        h_hi = h.astype(BF16)
        h_ref[g, :] = h_hi
        h_lo = (h - h_hi.astype(F32)).astype(BF16)
        main = lax.dot_general(rw, h_hi, _NT, preferred_element_type=F32)
        corr = lax.dot_general(rw[:n_e], h_lo, _NT, preferred_element_type=F32)
        logits.append(main[:n_e] + main[n_e:] + corr)
    route_ref[...] = _route(jnp.concatenate(logits, axis=1), rb_ref[...], before_ref)


def _out_proj(o, wo_bf16, x2, g1, gn, sc, sh, rwt, rb, seq):
    t, d = x2.shape
    tm = OPROJ_ROW_TILE
    per_b = seq // tm
    n_e = rb.shape[0]
    vec = lambda i: (i // per_b, 0, 0)
    return pl.pallas_call(
        _oproj_kernel,
        out_shape=(jax.ShapeDtypeStruct((t, d), F32),
                   jax.ShapeDtypeStruct((t, d), BF16),
                   jax.ShapeDtypeStruct((8, t), F32)),
        grid=(t // tm,),
        in_specs=[
            pl.BlockSpec((tm, d), lambda i: (i, 0)),
            pl.BlockSpec((d, d), lambda i: (0, 0)),
            pl.BlockSpec((tm, d), lambda i: (i, 0)),
            pl.BlockSpec((None, 1, d), vec),
            pl.BlockSpec((1, d), lambda i: (0, 0)),
            pl.BlockSpec((None, 1, d), vec),
            pl.BlockSpec((None, 1, d), vec),
            pl.BlockSpec((2 * n_e, d), lambda i: (0, 0)),
            pl.BlockSpec((n_e, 1), lambda i: (0, 0)),
        ],
        out_specs=(pl.BlockSpec((tm, d), lambda i: (i, 0)),
                   pl.BlockSpec((tm, d), lambda i: (i, 0)),
                   pl.BlockSpec((8, tm), lambda i: (0, i))),
        scratch_shapes=[pltpu.VMEM((MOE_TILE, MOE_TILE), BF16)],
        compiler_params=_cparams(("arbitrary",)),
        name="out_proj_route",
    )(o, wo_bf16, x2, g1, gn, sc, sh, rwt, rb)


def _moe_plan(route, t, tm, n_e):
    n_tiles = t // tm
    ch, ft = MOE_CHUNK, MOE_FFN_TILE
    loc_rows, main_rows, n_sorted = _moe_rows(t, tm, n_e)
    ids = jnp.arange(n_e, dtype=jnp.int32)
    e = route[0:2].astype(jnp.int32)
    pos = route[2:4].astype(jnp.int32)
    oh = (e[:, :, None] == ids).astype(jnp.int32)
    cnt = oh.sum(0).reshape(n_tiles, tm, n_e).sum(1)
    seg = (cnt + ch - 1) // ch * ch
    loc = jnp.cumsum(seg, axis=1) - seg
    tot = seg.sum(0)
    totp = (tot + ft - 1) // ft * ft
    base = jnp.cumsum(totp) - totp
    gdest = base[None, :] + jnp.cumsum(seg, axis=0) - seg
    col = (oh * jnp.repeat(loc, tm, axis=0)[None]).sum(-1) + pos

    row0 = jnp.arange(loc_rows // ch, dtype=jnp.int32) * ch
    ej = (row0[None, :, None] >= (loc + seg)[:, None, :]).sum(-1)
    ohj = (jnp.minimum(ej, n_e - 1)[..., None] == ids).astype(jnp.int32)
    dst = (ohj * (gdest - loc)[:, None, :]).sum(-1) + row0[None, :]
    dump = main_rows + (jnp.arange(n_tiles, dtype=jnp.int32) % 2)[:, None] * loc_rows + row0[None, :]
    dst = jnp.where(ej < n_e, dst, dump) // ch

    r0 = jnp.arange(n_sorted // ft, dtype=jnp.int32) * ft
    ends = base + totp
    te = jnp.minimum((r0[:, None] >= ends[None, :]).sum(-1), n_e - 1)
    tvalid = (r0 < ends[-1]).astype(jnp.int32)
    tfirst = tvalid * (r0 == (((te[:, None] == ids) * base[None, :]).sum(-1))).astype(jnp.int32)
    return col, dst.astype(jnp.int32), te.astype(jnp.int32), tvalid, tfirst


def _moe_rows(t, tm, n_e):
    ch, ft = MOE_CHUNK, MOE_FFN_TILE
    loc_rows = 2 * tm + n_e * ch
    main = 2 * t + (t // tm) * n_e * (ch - 1) + n_e * (ft - 1)
    main = (main + ft - 1) // ft * ft
    return loc_rows, main, (main + 2 * loc_rows + ft - 1) // ft * ft


def _chunk_copies(dst_ref, tile, local_ref, sorted_ref, sem, to_sorted):
    ch = MOE_CHUNK
    copies = []
    for j in range(local_ref.shape[0] // ch):
        far = sorted_ref.at[pl.ds(pl.multiple_of(dst_ref[tile, j] * ch, ch), ch)]
        near = local_ref.at[pl.ds(j * ch, ch)]
        copies.append(pltpu.make_async_copy(near, far, sem) if to_sorted
                      else pltpu.make_async_copy(far, near, sem))
    return copies


def _dispatch_kernel(dst_ref, col_ref, h_ref, init_ref, xs_ref, comp, sem):
    del init_ref
    tile = pl.program_id(0)
    last = pl.num_programs(0) - 1
    slot = tile % 2

    def wait_slot(s):
        pltpu.make_async_copy(comp.at[s], xs_ref.at[pl.ds(0, comp.shape[1])], sem.at[s]).wait()

    @pl.when(tile >= 2)
    def _():
        wait_slot(slot)

    col = col_ref[...]
    srow = lax.broadcasted_iota(jnp.int32, (comp.shape[1], col.shape[1]), 0)
    sel = jnp.where((srow == col[0:1]) | (srow == col[1:2]), 1.0, 0.0).astype(BF16)
    comp[slot] = jnp.dot(sel, h_ref[...], preferred_element_type=F32).astype(BF16)
    for cp in _chunk_copies(dst_ref, tile, comp.at[slot], xs_ref, sem.at[slot], True):
        cp.start()

    @pl.when(tile == last)
    def _():
        wait_slot(slot)

    @pl.when((tile == last) & (tile >= 1))
    def _():
        wait_slot(1 - slot)


def _dispatch(dst, col, h, n_e, init):
    t, d = h.shape
    tm = MOE_TILE
    loc_rows, _, n_sorted = _moe_rows(t, tm, n_e)
    if init is None:
        init = jnp.zeros((n_sorted, d), BF16)
    return pl.pallas_call(
        _dispatch_kernel,
        out_shape=jax.ShapeDtypeStruct((n_sorted, d), BF16),
        grid_spec=pltpu.PrefetchScalarGridSpec(
            num_scalar_prefetch=1,
            grid=(t // tm,),
            in_specs=[pl.BlockSpec((2, tm), lambda i, dst: (0, i)),
                      pl.BlockSpec((tm, d), lambda i, dst: (i, 0)),
                      pl.BlockSpec(memory_space=pl.ANY)],
            out_specs=pl.BlockSpec(memory_space=pl.ANY),
            scratch_shapes=[pltpu.VMEM((2, loc_rows, d), BF16), pltpu.SemaphoreType.DMA((2,))],
        ),
        input_output_aliases={3: 0},
        compiler_params=_cparams(("arbitrary",)),
        name="moe_dispatch",
    )(dst, col, h, init)


def _expert_kernel(te_ref, tv_ref, tf_ref, xs_ref, wg_ref, wu_ref, wd_ref, ys_ref, wg_b, wu_b, wd_b):
    del te_ref
    s = pl.program_id(0)

    @pl.when(tf_ref[s] == 1)
    def _():
        wg_b[...] = wg_ref[...].astype(BF16)
        wu_b[...] = wu_ref[...].astype(BF16)
        wd_b[...] = wd_ref[...].astype(BF16)

    @pl.when(tv_ref[s] == 1)
    def _():
        rows = ys_ref.shape[0] // MOE_FFN_SPLIT
        hs = []
        for r in range(MOE_FFN_SPLIT):
            xs = xs_ref[r * rows:(r + 1) * rows, :]
            hg = jnp.dot(xs, wg_b[...], preferred_element_type=F32)
            hu = jnp.dot(xs, wu_b[...], preferred_element_type=F32)
            hs.append((hg, hu))
        for r, (hg, hu) in enumerate(hs):
            he = hg * (1.0 / (1.0 + jnp.exp(-hg))) * hu
            ys_ref[r * rows:(r + 1) * rows, :] = jnp.dot(
                he.astype(BF16), wd_b[...], preferred_element_type=F32).astype(BF16)

    @pl.when(tv_ref[s] == 0)
    def _():
        ys_ref[...] = jnp.zeros(ys_ref.shape, ys_ref.dtype)


def _experts(te, tvalid, tfirst, xs, wg, wu, wd, layer):
    n_sorted, d = xs.shape
    f = wg.shape[3]
    ft = MOE_FFN_TILE
    wspec = lambda r, c: pl.BlockSpec((None, None, r, c), lambda s, te, tv, tf: (layer, te[s], 0, 0))
    return pl.pallas_call(
        _expert_kernel,
        out_shape=jax.ShapeDtypeStruct((n_sorted, d), BF16),
        grid_spec=pltpu.PrefetchScalarGridSpec(
            num_scalar_prefetch=3,
            grid=(n_sorted // ft,),
            in_specs=[pl.BlockSpec((ft, d), lambda s, te, tv, tf: (s, 0)),
                      wspec(d, f), wspec(d, f), wspec(f, d)],
            out_specs=pl.BlockSpec((ft, d), lambda s, te, tv, tf: (s, 0)),
            scratch_shapes=[pltpu.VMEM((d, f), BF16), pltpu.VMEM((d, f), BF16), pltpu.VMEM((f, d), BF16)],
        ),
        compiler_params=_cparams(("arbitrary",)),
        name="moe_experts",
    )(te, tvalid, tfirst, xs, wg, wu, wd)


def _combine_kernel(dst_ref, w_ref, ys_ref, x_ref, g2_ref, o_ref, comp, sem):
    tile = pl.program_id(0)
    slot = tile % 2

    def fetch(t, s):
        for cp in _chunk_copies(dst_ref, t, comp.at[s], ys_ref, sem.at[s], False):
            cp.start()

    def wait_slot(s):
        pltpu.make_async_copy(ys_ref.at[pl.ds(0, comp.shape[1])], comp.at[s], sem.at[s]).wait()

    last = pl.num_programs(0) - 1

    @pl.when(tile == 0)
    def _():
        fetch(tile, slot)

    fetch(jnp.minimum(tile + 1, last), 1 - slot)

    w = w_ref[...].T
    scol = lax.broadcasted_iota(jnp.int32, (w.shape[0], comp.shape[1]), 1).astype(F32)
    selw = (jnp.where(scol == w[:, 6:7], w[:, 4:5], 0.0)
            + jnp.where(scol == w[:, 7:8], w[:, 5:6], 0.0)).astype(BF16)
    wait_slot(slot)
    acc = jnp.dot(selw, comp[slot], preferred_element_type=F32)
    o_ref[...] = x_ref[...] + g2_ref[...] * acc

    @pl.when(tile == last)
    def _():
        wait_slot(1 - slot)


def _combine(dst, route_col, ys, x2, g2, seq, n_e):
    t, d = x2.shape
    tm = MOE_TILE
    per_b = seq // tm
    loc_rows = _moe_rows(t, tm, n_e)[0]
    return pl.pallas_call(
        _combine_kernel,
        out_shape=jax.ShapeDtypeStruct((t, d), F32),
        grid_spec=pltpu.PrefetchScalarGridSpec(
            num_scalar_prefetch=1,
            grid=(t // tm,),
            in_specs=[pl.BlockSpec((8, tm), lambda i, dst: (0, i)),
                      pl.BlockSpec(memory_space=pl.ANY),
                      pl.BlockSpec((tm, d), lambda i, dst: (i, 0)),
                      pl.BlockSpec((None, 1, d), lambda i, dst: (i // per_b, 0, 0))],
            out_specs=pl.BlockSpec((tm, d), lambda i, dst: (i, 0)),
            scratch_shapes=[pltpu.VMEM((2, loc_rows, d), BF16), pltpu.SemaphoreType.DMA((2,))],
        ),
        compiler_params=_cparams(("arbitrary",)),
        name="moe_combine",
    )(dst, route_col, ys, x2, g2)


def _moe(h, route, wg, wu, wd, layer, x2, g2, seq, sorted_init):
    t = x2.shape[0]
    n_e = wg.shape[1]
    col, dst, te, tvalid, tfirst = _moe_plan(route, t, MOE_TILE, n_e)
    xs = _dispatch(dst, col, h, n_e, sorted_init)
    ys = _experts(te, tvalid, tfirst, xs, wg, wu, wd, layer)
    route_col = jnp.concatenate([route[:6], col.astype(F32)], axis=0)
    return _combine(dst, route_col, ys, x2, g2, seq, n_e), ys


def kernel(x, c, ada_w, ada_b, norm_mix_g, norm_ffn_g, t5_bias, a_w_qkv, a_q_gain, a_k_gain, a_lambda, a_subln_g, a_w_o, b_w_qkv, b_q_gain, b_k_gain, b_rel_bias, b_w_o, router_w, router_bias, moe_w_gate, moe_w_up, moe_w_down):
    batch, seq, d = x.shape
    depth = ada_w.shape[0]
    assert seq % ATTN_TILE == 0 and seq % ROW_TILE == 0 and d == A_HEADS * 2 * A_HEAD_DIM
    assert d == B_HEADS * B_HEAD_DIM and A_HEAD_DIM == B_HEAD_DIM

    c_pad = jnp.zeros((8, d), F32).at[:batch].set(c.astype(F32))
    mod = _modulation(c_pad, ada_w.astype(F32), ada_b.astype(F32))[:, :batch]
    mod = mod.reshape(depth, batch, 6, 1, d)

    rwt = router_w.astype(F32).T
    rwt_hi = lax.bitcast_convert_type(
        lax.bitcast_convert_type(rwt, jnp.uint32) & jnp.uint32(0xFFFF0000), F32)
    rwt = jnp.concatenate([rwt_hi, rwt - rwt_hi], axis=0).astype(BF16)
    rb = router_bias.astype(F32).reshape(-1, 1)

    x2 = x.astype(F32).reshape(batch * seq, d)
    sorted_buf = None
    for i in range(depth):
        sh1, sc1, g1, sh2, sc2, g2 = [mod[i, :, k] for k in range(6)]
        j = i // 2
        if i % 2 == 0:
            w_qkv, qg, kg, w_o = a_w_qkv[j], a_q_gain[j], a_k_gain[j], a_w_o[j]
        else:
            w_qkv, qg, kg, w_o = b_w_qkv[j], b_q_gain[j], b_k_gain[j], b_w_o[j]
        n_rep = d // qg.shape[0]
        gq = (jnp.tile(qg.astype(F32), n_rep) * (A_HEAD_DIM ** -0.5 * LOG2E)).reshape(d, 1)
        gk = jnp.tile(kg.astype(F32), n_rep).reshape(1, d)
        qt, k, vt = _qkv_proj(x2, norm_mix_g[i].reshape(1, d), sc1, sh1, w_qkv.astype(BF16),
                              gq, gk, seq, A_HEAD_DIM)
        if i % 2 == 0:
            lambda_init = 0.8 - 0.6 * math.exp(-0.3 * i)
            o = _attention_a(qt, k, vt, _t5_vectors(t5_bias, ATTN_TILE), a_lambda[j].astype(F32),
                             a_subln_g[j].reshape(1, -1), batch, seq, lambda_init)
        else:
            o = _attention_b(qt, k, vt, _band_vectors(b_rel_bias[j], BAND_TILE), batch, seq)
        x2, h, route = _out_proj(o, w_o.astype(BF16), x2, g1, norm_ffn_g[i].reshape(1, d),
                                 sc2, sh2, rwt, rb, seq)
        x2, sorted_buf = _moe(h, route, moe_w_gate, moe_w_up, moe_w_down, i, x2, g2, seq, sorted_buf)
    return x2.reshape(batch, seq, d)
```

```python
import functools
import math

import numpy as np
import jax
import jax.numpy as jnp
from jax import lax
from jax.experimental import pallas as pl
from jax.experimental.pallas import tpu as pltpu

F32 = jnp.float32
BF16 = jnp.bfloat16

CHUNK = 64
A_HEADS = 8
A_HEAD_DIM = 64
T5_BUCKETS = 32
T5_MAX_DIST = 1024
B_HEADS = 16
B_HEAD_DIM = 64
LEFT_CHUNKS = 8
MAX_REL = 256
N_EXPERTS = 16
N_GROUPS = 4
E_PER_GROUP = N_EXPERTS // N_GROUPS
NORM_EPS = 1e-6
NEG_INF = -1e30
LOG2E = math.log2(math.e)

V7X_LANES = 128
V7X_MXU_DIM = 256

ATTN_TILE = 512
MOD_COL_TILE = 1536
FAR_UNROLL = 4
BAND_TILE = 256
BAND_QBLOCKS = 16
BAND_AHEAD = 2
ROW_TILE = 512
QKV_SPLIT = 2
MOE_TILE = 512
OPROJ_ROW_TILE = 1024
OPROJ_SPLIT = 4
MOE_CHUNK = 16
MOE_FFN_TILE = 512
MOE_FFN_SPLIT = 2
VMEM_LIMIT = 56 * 1024 * 1024

_NT = (((1,), (1,)), ((), ()))


def _cparams(sem):
    return pltpu.CompilerParams(dimension_semantics=sem, vmem_limit_bytes=VMEM_LIMIT)


def _mod_kernel(c_ref, w_ref, b_ref, o_ref):
    c = c_ref[...]
    s = c * (1.0 / (1.0 + jnp.exp(-c)))
    w = w_ref[...]
    s_hi, w_hi = s.astype(BF16), w.astype(BF16)
    s_lo = (s - s_hi.astype(F32)).astype(BF16)
    w_lo = (w - w_hi.astype(F32)).astype(BF16)
    dot = functools.partial(jnp.dot, preferred_element_type=F32)
    o_ref[...] = dot(s_hi, w_hi) + dot(s_hi, w_lo) + dot(s_lo, w_hi) + b_ref[...]


def _modulation(c_pad, ada_w, ada_b):
    depth, d, n = ada_w.shape
    rows = c_pad.shape[0]
    tn = MOD_COL_TILE
    return pl.pallas_call(
        _mod_kernel,
        out_shape=jax.ShapeDtypeStruct((depth, rows, n), F32),
        grid=(depth, n // tn),
        in_specs=[
            pl.BlockSpec((rows, d), lambda i, j: (0, 0)),
            pl.BlockSpec((None, d, tn), lambda i, j: (i, 0, j)),
            pl.BlockSpec((None, 1, tn), lambda i, j: (i, 0, j)),
        ],
        out_specs=pl.BlockSpec((None, rows, tn), lambda i, j: (i, 0, j)),
        compiler_params=_cparams(("parallel", "parallel")),
        name="adaln_mod",
    )(c_pad, ada_w, ada_b.reshape(depth, 1, n))


def _norm_mod(x, g, sc, sh):
    ms = jnp.mean(x * x, axis=-1, keepdims=True)
    return x * lax.rsqrt(ms + NORM_EPS) * g * (1.0 + sc) + sh


def _qkv_kernel(x_ref, g_ref, sc_ref, sh_ref, w_ref, gq_ref, gk_ref,
                qt_ref, k_ref, vt_ref, wqt_ref, wk_ref, wvt_ref, *, head_dim):
    d, tm = qt_ref.shape

    @pl.when(pl.program_id(0) == 0)
    def _():
        wqt_ref[...] = w_ref[:, :d].T.astype(BF16)
        wk_ref[...] = w_ref[:, d:2 * d].astype(BF16)
        wvt_ref[...] = w_ref[:, 2 * d:].T.astype(BF16)

    cw = V7X_MXU_DIM
    r = lax.broadcasted_iota(jnp.int32, (cw, cw), 0) // head_dim
    c = lax.broadcasted_iota(jnp.int32, (cw, cw), 1) // head_dim
    gmat = jnp.where(r == c, 1.0 / head_dim, 0.0).astype(BF16)
    tg = tm // QKV_SPLIT
    for gi in range(QKV_SPLIT):
        rows = slice(gi * tg, (gi + 1) * tg)
        h = _norm_mod(x_ref[rows, :], g_ref[...], sc_ref[...], sh_ref[...]).astype(BF16)
        y = lax.dot_general(wqt_ref[...], h, _NT, preferred_element_type=F32)
        y3 = y.reshape(d // head_dim, head_dim, tg)
        ss = jnp.mean(y3 * y3, axis=1, keepdims=True)
        qt_ref[:, rows] = ((y3 * lax.rsqrt(ss + NORM_EPS)).reshape(d, tg) * gq_ref[...]).astype(BF16)
        y = jnp.dot(h, wk_ref[...], preferred_element_type=F32)
        ysq = (y * y).astype(BF16)
        for ci in range(d // cw):
            cols = slice(ci * cw, (ci + 1) * cw)
            ss = jnp.dot(ysq[:, cols], gmat, preferred_element_type=F32)
            k_ref[rows, cols] = (y[:, cols] * lax.rsqrt(ss + NORM_EPS) * gk_ref[:, cols]).astype(BF16)
        vt_ref[:, rows] = lax.dot_general(wvt_ref[...], h, _NT, preferred_element_type=F32).astype(BF16)


def _qkv_proj(x2, g, sc, sh, w, gq, gk, seq, head_dim):
    t, d = x2.shape
    tm = ROW_TILE * QKV_SPLIT
    per_b = seq // tm
    vec = lambda i: (i // per_b, 0, 0)
    full = lambda i: (0, 0)
    return pl.pallas_call(
        functools.partial(_qkv_kernel, head_dim=head_dim),
        out_shape=(jax.ShapeDtypeStruct((d, t), BF16),
                   jax.ShapeDtypeStruct((t, d), BF16),
                   jax.ShapeDtypeStruct((d, t), BF16)),
        grid=(t // tm,),
        in_specs=[
            pl.BlockSpec((tm, d), lambda i: (i, 0)),
            pl.BlockSpec((1, d), full),
            pl.BlockSpec((None, 1, d), vec),
            pl.BlockSpec((None, 1, d), vec),
            pl.BlockSpec((d, 3 * d), full, pipeline_mode=pl.Buffered(1)),
            pl.BlockSpec((d, 1), full),
            pl.BlockSpec((1, d), full),
        ],
        out_specs=(pl.BlockSpec((d, tm), lambda i: (0, i)),
                   pl.BlockSpec((tm, d), lambda i: (i, 0)),
                   pl.BlockSpec((d, tm), lambda i: (0, i))),
        scratch_shapes=[pltpu.VMEM((d, d), BF16), pltpu.VMEM((d, d), BF16), pltpu.VMEM((d, d), BF16)],
        compiler_params=_cparams(("arbitrary",)),
        name="qkv_proj",
    )(x2, g, sc, sh, w, gq, gk)


def _softmax_pv(s, vt1, m_ref, acc_ref):
    m_prev = m_ref[...]
    m_new = jnp.maximum(m_prev, jnp.max(s, axis=0, keepdims=True))
    alpha = jnp.exp2(m_prev - m_new)
    p = jnp.exp2(s - m_new).astype(BF16)
    acc_ref[...] = alpha * acc_ref[...] + jnp.dot(vt1, p, preferred_element_type=F32)
    m_ref[...] = m_new


def _split_maps(qt, head_dim):
    row = lax.broadcasted_iota(jnp.int32, qt.shape, 0)
    zero = jnp.zeros_like(qt)
    return jnp.where(row < head_dim, qt, zero), jnp.where(row >= head_dim, qt, zero)


ONES_ROWS = 16


def _init_state(refs):
    for m_ref, acc_ref in refs:
        m_ref[...] = jnp.full(m_ref.shape, NEG_INF, F32)
        acc_ref[...] = jnp.zeros(acc_ref.shape, F32)


def _normalized(acc_ref):
    acc = acc_ref[...]
    return acc[:V7X_LANES] / acc[V7X_LANES:V7X_LANES + 1]


def _toeplitz(x_row, n):
    x = jnp.broadcast_to(x_row, (n, x_row.shape[1]))
    return pltpu.roll(x, 0, 1, stride=1, stride_axis=0)[:, :n]


def _chunk_delta(n):
    kc = lax.broadcasted_iota(jnp.int32, (n, n), 0) // CHUNK
    qc = lax.broadcasted_iota(jnp.int32, (n, n), 1) // CHUNK
    return qc - kc


def _attn_a_kernel(qt_ref, qn_ref, k_ref, vt_ref, x_ref, lam_ref, sub_ref, o_ref,
                   tab, fbuf, nbuf, m_st, acc_st, *, n_off, lambda_init):
    tq = ATTN_TILE
    step = pl.program_id(2)

    @pl.when(step == 0)
    def _():
        for o in range(n_off):
            t = _toeplitz(x_ref[o], tq) * LOG2E
            if o == 0:
                t = jnp.where(_chunk_delta(tq) >= 0, t, NEG_INF)
            tab[o] = t

    for c in range(2):
        _init_state(((m_st.at[c, 0], acc_st.at[c, 0]), (m_st.at[c, 1], acc_st.at[c, 1])))
    q_maps = [_split_maps(qt_ref[:, c * tq:(c + 1) * tq], A_HEAD_DIM) for c in range(2)]

    def k_tile(j):
        return k_ref[pl.ds(pl.multiple_of(j * tq, tq), tq), :]

    def vt1_tile(j):
        vt = vt_ref[:, pl.ds(pl.multiple_of(j * tq, tq), tq)]
        return jnp.concatenate([vt, jnp.ones((ONES_ROWS, tq), vt.dtype)], axis=0)

    def scores(buf, k, c):
        for m in range(2):
            buf[c, m] = jnp.dot(k, q_maps[c][m], preferred_element_type=F32)

    def consume(buf, vt1, c, o):
        for m in range(2):
            s = buf[c, m]
            _softmax_pv(s if o >= n_off else s + tab[o], vt1, m_st.at[c, m], acc_st.at[c, m])

    def near_block(first_tile, n_tiles, first_scored):
        units = [[c for c in range(2) if n_tiles - 2 + c - s >= 0] for s in range(n_tiles)]
        buf_of = lambda s: fbuf.at[0] if s == 0 else nbuf.at[s - 1]

        for s in range(1 if first_scored else 0, n_tiles):
            k = k_tile(first_tile + s)
            for c in units[s]:
                scores(buf_of(s), k, c)
        for s in range(n_tiles):
            vt1 = vt1_tile(first_tile + s)
            for c in units[s]:
                consume(buf_of(s), vt1, c, n_tiles - 2 + c - s)

    first_general = (n_off - 1) // 2
    for i0 in range(first_general):
        @pl.when(step == i0)
        def _():
            near_block(0, 2 * i0 + 2, False)

    def far_step(j, parity):
        k = k_tile(j + 1)
        for c in range(2):
            scores(fbuf.at[1 - parity], k, c)
        vt1 = vt1_tile(j)
        for c in range(2):
            consume(fbuf.at[parity], vt1, c, n_off)

    n_far = jnp.maximum(2 * step - (n_off - 1), 0)
    rem = n_far % FAR_UNROLL

    @pl.when(rem >= 2)
    def _():
        far_step(0, 0)
        far_step(1, 1)

    def far_trip(t, carry):
        for u in range(FAR_UNROLL):
            far_step(FAR_UNROLL * t + rem + u, u % 2)
        return carry
    lax.fori_loop(0, n_far // FAR_UNROLL, far_trip, 0)

    @pl.when(step >= first_general)
    def _():
        near_block(n_far, n_off + 1, True)

    def epilogue():
        lam = lam_ref[...]
        lam_full = (jnp.exp(jnp.sum(lam[0:1] * lam[1:2], axis=-1, keepdims=True))
                    - jnp.exp(jnp.sum(lam[2:3] * lam[3:4], axis=-1, keepdims=True)) + lambda_init)
        for c in range(2):
            a = _normalized(acc_st.at[c, 0]) - lam_full * _normalized(acc_st.at[c, 1])
            ms = jnp.mean(a * a, axis=0, keepdims=True)
            an = a * lax.rsqrt(ms + NORM_EPS)
            o_ref[c * tq:(c + 1) * tq, :] = (an.T * (sub_ref[...] * (1.0 - lambda_init))).astype(BF16)

    last = pl.num_programs(2) - 1

    @pl.when(step < last)
    def _():
        k = k_tile(0)
        for c in range(2):
            qa, qb = _split_maps(qn_ref[:, c * tq:(c + 1) * tq], A_HEAD_DIM)
            fbuf[0, c, 0] = jnp.dot(k, qa, preferred_element_type=F32)
            fbuf[0, c, 1] = jnp.dot(k, qb, preferred_element_type=F32)
        epilogue()

    @pl.when(step == last)
    def _():
        epilogue()


def _attn_b_kernel(qt_ref, k_ref, vt_ref, x_ref, o_ref, tab, sbuf):
    tq = BAND_TILE
    n_blk = qt_ref.shape[1] // tq
    n_off = tab.shape[0] // 2
    step = pl.program_id(2)

    @pl.when(step == 0)
    def _():
        for m in range(2):
            for o in range(n_off):
                d = _chunk_delta(tq) + o * (tq // CHUNK)
                t = _toeplitz(x_ref[n_off * m + o], tq) * LOG2E
                tab[n_off * m + o] = jnp.where((d >= 0) & (d <= LEFT_CHUNKS), t, NEG_INF)

    q_maps = [_split_maps(qt_ref[:, c * tq:(c + 1) * tq], B_HEAD_DIM) for c in range(n_blk)]

    def run(first_step):
        pairs = [(c, o) for c in range(n_blk) for o in range(n_off - 1, -1, -1)
                 if not first_step or c - o >= 0]
        k_tiles, vt_tiles = {}, {}
        for c, o in pairs:
            if c - o not in k_tiles:
                ks = pl.multiple_of((n_blk * step + c - o) * tq, tq)
                k_tiles[c - o] = k_ref[pl.ds(ks, tq), :]
                vt = vt_ref[:, pl.ds(ks, tq)]
                vt_tiles[c - o] = jnp.concatenate([vt, jnp.ones((ONES_ROWS, tq), vt.dtype)], axis=0)
        def tile_scores(c):
            for o in [o for cc, o in pairs if cc == c]:
                for m in range(2):
                    sbuf[n_off * c + o, m] = jnp.dot(k_tiles[c - o], q_maps[c][m], preferred_element_type=F32)

        for c in range(min(BAND_AHEAD, n_blk)):
            tile_scores(c)
        for c in range(n_blk):
            if c + BAND_AHEAD < n_blk:
                tile_scores(c + BAND_AHEAD)
            outs = []
            for m in range(2):
                offs = [o for cc, o in pairs if cc == c]
                ss = [sbuf[n_off * c + o, m] + tab[n_off * m + o] for o in offs]
                mx = functools.reduce(jnp.maximum, [jnp.max(s, axis=0, keepdims=True) for s in ss])
                acc = sum(jnp.dot(vt_tiles[c - o], jnp.exp2(s - mx).astype(BF16), preferred_element_type=F32)
                          for o, s in zip(offs, ss))
                outs.append(acc[:V7X_LANES] / acc[V7X_LANES:V7X_LANES + 1])
            row = lax.broadcasted_iota(jnp.int32, outs[0].shape, 0)
            o_ref[c * tq:(c + 1) * tq, :] = jnp.where(row < B_HEAD_DIM, outs[0], outs[1]).T.astype(BF16)

    @pl.when(step == 0)
    def _():
        run(True)

    @pl.when(step >= 1)
    def _():
        run(False)


def _attn_specs(batch, seq, nq, tq):
    return dict(
        q=pl.BlockSpec((V7X_LANES, tq), lambda b, h, i: (h, b * nq + i)),
        k=pl.BlockSpec((seq, V7X_LANES), lambda b, h, i: (b, h)),
        v=pl.BlockSpec((V7X_LANES, seq), lambda b, h, i: (h, b)),
        o=pl.BlockSpec((tq, V7X_LANES), lambda b, h, i: (b * nq + i, h)),
    )


def _attention_a(qt, k, vt, xvec, lam, sub_g, batch, seq, lambda_init):
    d, t = qt.shape
    tq = ATTN_TILE
    n_off = xvec.shape[1]
    assert n_off % 2 == 1 and seq % (2 * tq) == 0
    nq = seq // (2 * tq)
    sp = _attn_specs(batch, seq, nq, 2 * tq)
    rows = V7X_LANES + ONES_ROWS
    return pl.pallas_call(
        functools.partial(_attn_a_kernel, n_off=n_off, lambda_init=lambda_init),
        out_shape=jax.ShapeDtypeStruct((t, d), BF16),
        grid=(batch, d // V7X_LANES, nq),
        in_specs=[sp["q"],
                  pl.BlockSpec((V7X_LANES, 2 * tq), lambda b, h, i: (h, b * nq + jnp.minimum(i + 1, nq - 1))),
                  sp["k"], sp["v"],
                  pl.BlockSpec((None,) + xvec.shape[1:], lambda b, h, i: (h, 0, 0, 0)),
                  pl.BlockSpec(lam.shape, lambda b, h, i: (0, 0)),
                  pl.BlockSpec(sub_g.shape, lambda b, h, i: (0, 0))],
        out_specs=sp["o"],
        scratch_shapes=[pltpu.VMEM((n_off, tq, tq), F32), pltpu.VMEM((2, 2, 2, tq, tq), F32),
                        pltpu.VMEM((n_off, 2, 2, tq, tq), F32),
                        pltpu.VMEM((2, 2, 1, tq), F32), pltpu.VMEM((2, 2, rows, tq), F32)],
        compiler_params=_cparams(("parallel", "parallel", "arbitrary")),
        name="diff_attention",
    )(qt, qt, k, vt, xvec, lam, sub_g)


def _attention_b(qt, k, vt, xvec, batch, seq):
    d, t = qt.shape
    tq = BAND_TILE
    g = BAND_QBLOCKS
    n_off = xvec.shape[1] // 2
    assert g >= n_off - 1 and seq % (g * tq) == 0
    nq = seq // (g * tq)
    sp = _attn_specs(batch, seq, nq, g * tq)
    return pl.pallas_call(
        _attn_b_kernel,
        out_shape=jax.ShapeDtypeStruct((t, d), BF16),
        grid=(batch, d // V7X_LANES, nq),
        in_specs=[sp["q"], sp["k"], sp["v"],
                  pl.BlockSpec((None,) + xvec.shape[1:], lambda b, h, i: (h, 0, 0, 0))],
        out_specs=sp["o"],
        scratch_shapes=[pltpu.VMEM((2 * n_off, tq, tq), F32), pltpu.VMEM((n_off * g, 2, tq, tq), F32)],
        compiler_params=_cparams(("parallel", "parallel", "arbitrary")),
        name="chunk_attention",
    )(qt, k, vt, xvec)


def _t5_bucket(rel):
    nb = T5_BUCKETS // 2
    ret = jnp.where(rel > 0, nb, 0)
    n = jnp.abs(rel)
    max_exact = nb // 2
    nf = jnp.maximum(n, 1).astype(F32)
    large = max_exact + (jnp.log(nf / max_exact) / math.log(T5_MAX_DIST / max_exact)
                         * (nb - max_exact)).astype(jnp.int32)
    large = jnp.minimum(large, nb - 1)
    return ret + jnp.where(n < max_exact, n, large)


def _t5_const_distance():
    nb = T5_BUCKETS // 2
    max_exact = nb // 2
    n = np.arange(max_exact, 4 * T5_MAX_DIST, dtype=np.float64)
    large = max_exact + np.floor(np.log(n / max_exact) / math.log(T5_MAX_DIST / max_exact) * (nb - max_exact))
    below = np.nonzero(large < nb - 1)[0]
    return int(n[below[-1]]) + 2


def _tile_rel(tile, n_off):
    i = jnp.arange(2 * tile, dtype=jnp.int32)
    rel = jnp.where(i < tile, -i, 2 * tile - i)
    return rel[None, :] - tile * jnp.arange(n_off, dtype=jnp.int32)[:, None]


def _t5_vectors(t5_bias, tile):
    n_off = 1
    while (n_off - 1) * tile + 1 < _t5_const_distance():
        n_off += 1
    tb = t5_bias.astype(F32)
    vals = tb[_t5_bucket(_tile_rel(tile, n_off))] - tb[T5_BUCKETS // 2 - 1]
    return vals.transpose(2, 0, 1)[:, :, None, :]


def _band_vectors(rel_bias, tile):
    n_off = LEFT_CHUNKS * CHUNK // tile + 1
    idx = jnp.clip(_tile_rel(tile, n_off), -MAX_REL, MAX_REL) + MAX_REL
    vals = rel_bias.astype(F32)[:, idx]
    return vals.reshape(rel_bias.shape[0] // 2, 2 * n_off, 1, 2 * tile)


def _top2_sum4(r0, r1, r2, r3):
    a, b = jnp.maximum(r0, r1), jnp.minimum(r0, r1)
    c, d = jnp.maximum(r2, r3), jnp.minimum(r2, r3)
    return jnp.maximum(a, c) + jnp.maximum(jnp.minimum(a, c), jnp.maximum(b, d))


def _route(logits, rbias, before_ref):
    n_e, n = logits.shape
    scores = 1.0 / (1.0 + jnp.exp(-logits))
    sel = scores + rbias
    row = lax.broadcasted_iota(jnp.int32, sel.shape, 0)
    best = None
    for g in range(N_GROUPS):
        rows = [sel[g * E_PER_GROUP + i: g * E_PER_GROUP + i + 1, :] for i in range(E_PER_GROUP)]
        gs = _top2_sum4(*rows)
        if best is None:
            best, gidx = gs, jnp.zeros(gs.shape, jnp.int32)
        else:
            gidx = jnp.where(gs > best, g, gidx)
            best = jnp.maximum(best, gs)
    masked = jnp.where(row // E_PER_GROUP == gidx, sel, NEG_INF)
    m1 = jnp.max(masked, axis=0, keepdims=True)
    i1 = jnp.min(jnp.where(masked == m1, row, n_e), axis=0, keepdims=True)
    masked2 = jnp.where(row == i1, -3.0e38, masked)
    m2 = jnp.max(masked2, axis=0, keepdims=True)
    i2 = jnp.min(jnp.where(masked2 == m2, row, n_e), axis=0, keepdims=True)
    w1 = jnp.sum(jnp.where(row == i1, scores, 0.0), axis=0, keepdims=True)
    w2 = jnp.sum(jnp.where(row == i2, scores, 0.0), axis=0, keepdims=True)
    den = w1 + w2
    member = jnp.where((row == i1) | (row == i2), 1.0, 0.0).astype(BF16)
    before = before_ref[...]
    pos = jnp.concatenate(
        [jnp.dot(member[:, i:i + MOE_TILE], before, preferred_element_type=F32)
         for i in range(0, n, MOE_TILE)], axis=1)
    p1 = jnp.sum(jnp.where(row == i1, pos, 0.0), axis=0, keepdims=True)
    p2 = jnp.sum(jnp.where(row == i2, pos, 0.0), axis=0, keepdims=True)
    out_row = lax.broadcasted_iota(jnp.int32, (8, n), 0)
    out = jnp.zeros((8, n), F32)
    for r, val in enumerate((i1.astype(F32), i2.astype(F32), p1, p2, w1 / den, w2 / den)):
        out = jnp.where(out_row == r, val, out)
    return out


def _oproj_kernel(o_ref, wo_ref, x_ref, g1_ref, gn_ref, sc_ref, sh_ref, rwt_ref, rb_ref,
                  xo_ref, h_ref, route_ref, before_ref):
    @pl.when(pl.program_id(0) == 0)
    def _():
        t_from = lax.broadcasted_iota(jnp.int32, before_ref.shape, 0)
        t_to = lax.broadcasted_iota(jnp.int32, before_ref.shape, 1)
        before_ref[...] = jnp.where(t_from < t_to, 1.0, 0.0).astype(BF16)

    n_e = rb_ref.shape[0]
    rw = rwt_ref[...]
    rows = o_ref.shape[0] // OPROJ_SPLIT
    groups = [slice(r * rows, (r + 1) * rows) for r in range(OPROJ_SPLIT)]
    ys = [jnp.dot(o_ref[g, :], wo_ref[...], preferred_element_type=F32) for g in groups]
    logits = []
    for g, y in zip(groups, ys):
        xn = x_ref[g, :] + g1_ref[...] * y
        xo_ref[g, :] = xn
        h = _norm_mod(xn, gn_ref[...], sc_ref[...], sh_ref[...])
        h_hi = h.astype(BF16)
        h_ref[g, :] = h_hi
        h_lo = (h - h_hi.astype(F32)).astype(BF16)
        main = lax.dot_general(rw, h_hi, _NT, preferred_element_type=F32)
        corr = lax.dot_general(rw[:n_e], h_lo, _NT, preferred_element_type=F32)
        logits.append(main[:n_e] + main[n_e:] + corr)
    route_ref[...] = _route(jnp.concatenate(logits, axis=1), rb_ref[...], before_ref)


def _out_proj(o, wo_bf16, x2, g1, gn, sc, sh, rwt, rb, seq):
    t, d = x2.shape
    tm = OPROJ_ROW_TILE
    per_b = seq // tm
    n_e = rb.shape[0]
    vec = lambda i: (i // per_b, 0, 0)
    return pl.pallas_call(
        _oproj_kernel,
        out_shape=(jax.ShapeDtypeStruct((t, d), F32),
                   jax.ShapeDtypeStruct((t, d), BF16),
                   jax.ShapeDtypeStruct((8, t), F32)),
        grid=(t // tm,),
        in_specs=[
            pl.BlockSpec((tm, d), lambda i: (i, 0)),
            pl.BlockSpec((d, d), lambda i: (0, 0)),
            pl.BlockSpec((tm, d), lambda i: (i, 0)),
            pl.BlockSpec((None, 1, d), vec),
            pl.BlockSpec((1, d), lambda i: (0, 0)),
            pl.BlockSpec((None, 1, d), vec),
            pl.BlockSpec((None, 1, d), vec),
            pl.BlockSpec((2 * n_e, d), lambda i: (0, 0)),
            pl.BlockSpec((n_e, 1), lambda i: (0, 0)),
        ],
        out_specs=(pl.BlockSpec((tm, d), lambda i: (i, 0)),
                   pl.BlockSpec((tm, d), lambda i: (i, 0)),
                   pl.BlockSpec((8, tm), lambda i: (0, i))),
        scratch_shapes=[pltpu.VMEM((MOE_TILE, MOE_TILE), BF16)],
        compiler_params=_cparams(("arbitrary",)),
        name="out_proj_route",
    )(o, wo_bf16, x2, g1, gn, sc, sh, rwt, rb)


def _moe_plan(route, t, tm, n_e):
    n_tiles = t // tm
    ch, ft = MOE_CHUNK, MOE_FFN_TILE
    loc_rows, main_rows, n_sorted = _moe_rows(t, tm, n_e)
    ids = jnp.arange(n_e, dtype=jnp.int32)
    e = route[0:2].astype(jnp.int32)
    pos = route[2:4].astype(jnp.int32)
    oh = (e[:, :, None] == ids).astype(jnp.int32)
    cnt = oh.sum(0).reshape(n_tiles, tm, n_e).sum(1)
    seg = (cnt + ch - 1) // ch * ch
    loc = jnp.cumsum(seg, axis=1) - seg
    tot = seg.sum(0)
    totp = (tot + ft - 1) // ft * ft
    base = jnp.cumsum(totp) - totp
    gdest = base[None, :] + jnp.cumsum(seg, axis=0) - seg
    col = (oh * jnp.repeat(loc, tm, axis=0)[None]).sum(-1) + pos

    row0 = jnp.arange(loc_rows // ch, dtype=jnp.int32) * ch
    ej = (row0[None, :, None] >= (loc + seg)[:, None, :]).sum(-1)
    ohj = (jnp.minimum(ej, n_e - 1)[..., None] == ids).astype(jnp.int32)
    dst = (ohj * (gdest - loc)[:, None, :]).sum(-1) + row0[None, :]
    dump = main_rows + (jnp.arange(n_tiles, dtype=jnp.int32) % 2)[:, None] * loc_rows + row0[None, :]
    dst = jnp.where(ej < n_e, dst, dump) // ch

    r0 = jnp.arange(n_sorted // ft, dtype=jnp.int32) * ft
    ends = base + totp
    te = jnp.minimum((r0[:, None] >= ends[None, :]).sum(-1), n_e - 1)
    tvalid = (r0 < ends[-1]).astype(jnp.int32)
    tfirst = tvalid * (r0 == (((te[:, None] == ids) * base[None, :]).sum(-1))).astype(jnp.int32)
    return col, dst.astype(jnp.int32), te.astype(jnp.int32), tvalid, tfirst


def _moe_rows(t, tm, n_e):
    ch, ft = MOE_CHUNK, MOE_FFN_TILE
    loc_rows = 2 * tm + n_e * ch
    main = 2 * t + (t // tm) * n_e * (ch - 1) + n_e * (ft - 1)
    main = (main + ft - 1) // ft * ft
    return loc_rows, main, (main + 2 * loc_rows + ft - 1) // ft * ft


def _chunk_copies(dst_ref, tile, local_ref, sorted_ref, sem, to_sorted):
    ch = MOE_CHUNK
    copies = []
    for j in range(local_ref.shape[0] // ch):
        far = sorted_ref.at[pl.ds(pl.multiple_of(dst_ref[tile, j] * ch, ch), ch)]
        near = local_ref.at[pl.ds(j * ch, ch)]
        copies.append(pltpu.make_async_copy(near, far, sem) if to_sorted
                      else pltpu.make_async_copy(far, near, sem))
    return copies


def _dispatch_kernel(dst_ref, col_ref, h_ref, init_ref, xs_ref, comp, sem):
    del init_ref
    tile = pl.program_id(0)
    last = pl.num_programs(0) - 1
    slot = tile % 2

    def wait_slot(s):
        pltpu.make_async_copy(comp.at[s], xs_ref.at[pl.ds(0, comp.shape[1])], sem.at[s]).wait()

    @pl.when(tile >= 2)
    def _():
        wait_slot(slot)

    col = col_ref[...]
    srow = lax.broadcasted_iota(jnp.int32, (comp.shape[1], col.shape[1]), 0)
    sel = jnp.where((srow == col[0:1]) | (srow == col[1:2]), 1.0, 0.0).astype(BF16)
    comp[slot] = jnp.dot(sel, h_ref[...], preferred_element_type=F32).astype(BF16)
    for cp in _chunk_copies(dst_ref, tile, comp.at[slot], xs_ref, sem.at[slot], True):
        cp.start()

    @pl.when(tile == last)
    def _():
        wait_slot(slot)

    @pl.when((tile == last) & (tile >= 1))
    def _():
        wait_slot(1 - slot)


def _dispatch(dst, col, h, n_e, init):
    t, d = h.shape
    tm = MOE_TILE
    loc_rows, _, n_sorted = _moe_rows(t, tm, n_e)
    if init is None:
        init = jnp.zeros((n_sorted, d), BF16)
    return pl.pallas_call(
        _dispatch_kernel,
        out_shape=jax.ShapeDtypeStruct((n_sorted, d), BF16),
        grid_spec=pltpu.PrefetchScalarGridSpec(
            num_scalar_prefetch=1,
            grid=(t // tm,),
            in_specs=[pl.BlockSpec((2, tm), lambda i, dst: (0, i)),
                      pl.BlockSpec((tm, d), lambda i, dst: (i, 0)),
                      pl.BlockSpec(memory_space=pl.ANY)],
            out_specs=pl.BlockSpec(memory_space=pl.ANY),
            scratch_shapes=[pltpu.VMEM((2, loc_rows, d), BF16), pltpu.SemaphoreType.DMA((2,))],
        ),
        input_output_aliases={3: 0},
        compiler_params=_cparams(("arbitrary",)),
        name="moe_dispatch",
    )(dst, col, h, init)


def _expert_kernel(te_ref, tv_ref, tf_ref, xs_ref, wg_ref, wu_ref, wd_ref, ys_ref, wg_b, wu_b, wd_b):
    del te_ref
    s = pl.program_id(0)

    @pl.when(tf_ref[s] == 1)
    def _():
        wg_b[...] = wg_ref[...].astype(BF16)
        wu_b[...] = wu_ref[...].astype(BF16)
        wd_b[...] = wd_ref[...].astype(BF16)

    @pl.when(tv_ref[s] == 1)
    def _():
        rows = ys_ref.shape[0] // MOE_FFN_SPLIT
        hs = []
        for r in range(MOE_FFN_SPLIT):
            xs = xs_ref[r * rows:(r + 1) * rows, :]
            hg = jnp.dot(xs, wg_b[...], preferred_element_type=F32)
            hu = jnp.dot(xs, wu_b[...], preferred_element_type=F32)
            hs.append((hg, hu))
        for r, (hg, hu) in enumerate(hs):
            he = hg * (1.0 / (1.0 + jnp.exp(-hg))) * hu
            ys_ref[r * rows:(r + 1) * rows, :] = jnp.dot(
                he.astype(BF16), wd_b[...], preferred_element_type=F32).astype(BF16)

    @pl.when(tv_ref[s] == 0)
    def _():
        ys_ref[...] = jnp.zeros(ys_ref.shape, ys_ref.dtype)


def _experts(te, tvalid, tfirst, xs, wg, wu, wd, layer):
    n_sorted, d = xs.shape
    f = wg.shape[3]
    ft = MOE_FFN_TILE
    wspec = lambda r, c: pl.BlockSpec((None, None, r, c), lambda s, te, tv, tf: (layer, te[s], 0, 0))
    return pl.pallas_call(
        _expert_kernel,
        out_shape=jax.ShapeDtypeStruct((n_sorted, d), BF16),
        grid_spec=pltpu.PrefetchScalarGridSpec(
            num_scalar_prefetch=3,
            grid=(n_sorted // ft,),
            in_specs=[pl.BlockSpec((ft, d), lambda s, te, tv, tf: (s, 0)),
                      wspec(d, f), wspec(d, f), wspec(f, d)],
            out_specs=pl.BlockSpec((ft, d), lambda s, te, tv, tf: (s, 0)),
            scratch_shapes=[pltpu.VMEM((d, f), BF16), pltpu.VMEM((d, f), BF16), pltpu.VMEM((f, d), BF16)],
        ),
        compiler_params=_cparams(("arbitrary",)),
        name="moe_experts",
    )(te, tvalid, tfirst, xs, wg, wu, wd)


def _combine_kernel(dst_ref, w_ref, ys_ref, x_ref, g2_ref, o_ref, comp, sem):
    tile = pl.program_id(0)
    slot = tile % 2

    def fetch(t, s):
        for cp in _chunk_copies(dst_ref, t, comp.at[s], ys_ref, sem.at[s], False):
            cp.start()

    def wait_slot(s):
        pltpu.make_async_copy(ys_ref.at[pl.ds(0, comp.shape[1])], comp.at[s], sem.at[s]).wait()

    last = pl.num_programs(0) - 1

    @pl.when(tile == 0)
    def _():
        fetch(tile, slot)

    fetch(jnp.minimum(tile + 1, last), 1 - slot)

    w = w_ref[...].T
    scol = lax.broadcasted_iota(jnp.int32, (w.shape[0], comp.shape[1]), 1).astype(F32)
    selw = (jnp.where(scol == w[:, 6:7], w[:, 4:5], 0.0)
            + jnp.where(scol == w[:, 7:8], w[:, 5:6], 0.0)).astype(BF16)
    wait_slot(slot)
    acc = jnp.dot(selw, comp[slot], preferred_element_type=F32)
    o_ref[...] = x_ref[...] + g2_ref[...] * acc

    @pl.when(tile == last)
    def _():
        wait_slot(1 - slot)


def _combine(dst, route_col, ys, x2, g2, seq, n_e):
    t, d = x2.shape
    tm = MOE_TILE
    per_b = seq // tm
    loc_rows = _moe_rows(t, tm, n_e)[0]
    return pl.pallas_call(
        _combine_kernel,
        out_shape=jax.ShapeDtypeStruct((t, d), F32),
        grid_spec=pltpu.PrefetchScalarGridSpec(
            num_scalar_prefetch=1,
            grid=(t // tm,),
            in_specs=[pl.BlockSpec((8, tm), lambda i, dst: (0, i)),
                      pl.BlockSpec(memory_space=pl.ANY),
                      pl.BlockSpec((tm, d), lambda i, dst: (i, 0)),
                      pl.BlockSpec((None, 1, d), lambda i, dst: (i // per_b, 0, 0))],
            out_specs=pl.BlockSpec((tm, d), lambda i, dst: (i, 0)),
            scratch_shapes=[pltpu.VMEM((2, loc_rows, d), BF16), pltpu.SemaphoreType.DMA((2,))],
        ),
        compiler_params=_cparams(("arbitrary",)),
        name="moe_combine",
    )(dst, route_col, ys, x2, g2)


def _moe(h, route, wg, wu, wd, layer, x2, g2, seq, sorted_init):
    t = x2.shape[0]
    n_e = wg.shape[1]
    col, dst, te, tvalid, tfirst = _moe_plan(route, t, MOE_TILE, n_e)
    xs = _dispatch(dst, col, h, n_e, sorted_init)
    ys = _experts(te, tvalid, tfirst, xs, wg, wu, wd, layer)
    route_col = jnp.concatenate([route[:6], col.astype(F32)], axis=0)
    return _combine(dst, route_col, ys, x2, g2, seq, n_e), ys


def kernel(x, c, ada_w, ada_b, norm_mix_g, norm_ffn_g, t5_bias, a_w_qkv, a_q_gain, a_k_gain, a_lambda, a_subln_g, a_w_o, b_w_qkv, b_q_gain, b_k_gain, b_rel_bias, b_w_o, router_w, router_bias, moe_w_gate, moe_w_up, moe_w_down):
    batch, seq, d = x.shape
    depth = ada_w.shape[0]
    assert seq % ATTN_TILE == 0 and seq % ROW_TILE == 0 and d == A_HEADS * 2 * A_HEAD_DIM
    assert d == B_HEADS * B_HEAD_DIM and A_HEAD_DIM == B_HEAD_DIM

    c_pad = jnp.zeros((8, d), F32).at[:batch].set(c.astype(F32))
    mod = _modulation(c_pad, ada_w.astype(F32), ada_b.astype(F32))[:, :batch]
    mod = mod.reshape(depth, batch, 6, 1, d)

    rwt = router_w.astype(F32).T
    rwt_hi = lax.bitcast_convert_type(
        lax.bitcast_convert_type(rwt, jnp.uint32) & jnp.uint32(0xFFFF0000), F32)
    rwt = jnp.concatenate([rwt_hi, rwt - rwt_hi], axis=0).astype(BF16)
    rb = router_bias.astype(F32).reshape(-1, 1)

    x2 = x.astype(F32).reshape(batch * seq, d)
    sorted_buf = None
    for i in range(depth):
        sh1, sc1, g1, sh2, sc2, g2 = [mod[i, :, k] for k in range(6)]
        j = i // 2
        if i % 2 == 0:
            w_qkv, qg, kg, w_o = a_w_qkv[j], a_q_gain[j], a_k_gain[j], a_w_o[j]
        else:
            w_qkv, qg, kg, w_o = b_w_qkv[j], b_q_gain[j], b_k_gain[j], b_w_o[j]
        n_rep = d // qg.shape[0]
        gq = (jnp.tile(qg.astype(F32), n_rep) * (A_HEAD_DIM ** -0.5 * LOG2E)).reshape(d, 1)
        gk = jnp.tile(kg.astype(F32), n_rep).reshape(1, d)
        qt, k, vt = _qkv_proj(x2, norm_mix_g[i].reshape(1, d), sc1, sh1, w_qkv.astype(F32),
                              gq, gk, seq, A_HEAD_DIM)
        if i % 2 == 0:
            lambda_init = 0.8 - 0.6 * math.exp(-0.3 * i)
            o = _attention_a(qt, k, vt, _t5_vectors(t5_bias, ATTN_TILE), a_lambda[j].astype(F32),
                             a_subln_g[j].reshape(1, -1), batch, seq, lambda_init)
        else:
            o = _attention_b(qt, k, vt, _band_vectors(b_rel_bias[j], BAND_TILE), batch, seq)
        x2, h, route = _out_proj(o, w_o.astype(BF16), x2, g1, norm_ffn_g[i].reshape(1, d),
                                 sc2, sh2, rwt, rb, seq)
        x2, sorted_buf = _moe(h, route, moe_w_gate, moe_w_up, moe_w_down, i, x2, g2, seq, sorted_buf)
    return x2.reshape(batch, seq, d)
```

```python
import functools
import math

import numpy as np
import jax
import jax.numpy as jnp
from jax import lax
from jax.experimental import pallas as pl
from jax.experimental.pallas import tpu as pltpu

F32 = jnp.float32
BF16 = jnp.bfloat16

CHUNK = 64
A_HEADS = 8
A_HEAD_DIM = 64
T5_BUCKETS = 32
T5_MAX_DIST = 1024
B_HEADS = 16
B_HEAD_DIM = 64
LEFT_CHUNKS = 8
MAX_REL = 256
N_EXPERTS = 16
N_GROUPS = 4
E_PER_GROUP = N_EXPERTS // N_GROUPS
NORM_EPS = 1e-6
NEG_INF = -1e30
LOG2E = math.log2(math.e)

V7X_LANES = 128
V7X_MXU_DIM = 256

ATTN_TILE = 512
MOD_COL_TILE = 1536
FAR_UNROLL = 4
BAND_TILE = 256
BAND_QBLOCKS = 16
BAND_AHEAD = 2
ROW_TILE = 512
QKV_SPLIT = 2
MOE_TILE = 512
OPROJ_ROW_TILE = 1024
OPROJ_SPLIT = 4
MOE_CHUNK = 16
MOE_FFN_TILE = 512
MOE_FFN_SPLIT = 2
VMEM_LIMIT = 56 * 1024 * 1024

_NT = (((1,), (1,)), ((), ()))


def _cparams(sem):
    return pltpu.CompilerParams(dimension_semantics=sem, vmem_limit_bytes=VMEM_LIMIT)


def _mod_kernel(c_ref, w_ref, b_ref, o_ref):
    c = c_ref[...]
    s = c * (1.0 / (1.0 + jnp.exp(-c)))
    w = w_ref[...]
    s_hi, w_hi = s.astype(BF16), w.astype(BF16)
    s_lo = (s - s_hi.astype(F32)).astype(BF16)
    w_lo = (w - w_hi.astype(F32)).astype(BF16)
    dot = functools.partial(jnp.dot, preferred_element_type=F32)
    o_ref[...] = dot(s_hi, w_hi) + dot(s_hi, w_lo) + dot(s_lo, w_hi) + b_ref[...]


def _modulation(c_pad, ada_w, ada_b):
    depth, d, n = ada_w.shape
    rows = c_pad.shape[0]
    tn = MOD_COL_TILE
    return pl.pallas_call(
        _mod_kernel,
        out_shape=jax.ShapeDtypeStruct((depth, rows, n), F32),
        grid=(depth, n // tn),
        in_specs=[
            pl.BlockSpec((rows, d), lambda i, j: (0, 0)),
            pl.BlockSpec((None, d, tn), lambda i, j: (i, 0, j)),
            pl.BlockSpec((None, 1, tn), lambda i, j: (i, 0, j)),
        ],
        out_specs=pl.BlockSpec((None, rows, tn), lambda i, j: (i, 0, j)),
        compiler_params=_cparams(("parallel", "parallel")),
        name="adaln_mod",
    )(c_pad, ada_w, ada_b.reshape(depth, 1, n))


def _norm_mod(x, g, sc, sh):
    ms = jnp.mean(x * x, axis=-1, keepdims=True)
    return x * lax.rsqrt(ms + NORM_EPS) * g * (1.0 + sc) + sh


def _qkv_kernel(x_ref, g_ref, sc_ref, sh_ref, w_ref, gq_ref, gk_ref,
                qt_ref, k_ref, vt_ref, wqt_ref, wk_ref, wvt_ref, *, head_dim):
    d, tm = qt_ref.shape

    @pl.when(pl.program_id(0) == 0)
    def _():
        wqt_ref[...] = w_ref[:, :d].T.astype(BF16)
        wk_ref[...] = w_ref[:, d:2 * d].astype(BF16)
        wvt_ref[...] = w_ref[:, 2 * d:].T.astype(BF16)

    cw = V7X_MXU_DIM
    r = lax.broadcasted_iota(jnp.int32, (cw, cw), 0) // head_dim
    c = lax.broadcasted_iota(jnp.int32, (cw, cw), 1) // head_dim
    gmat = jnp.where(r == c, 1.0 / head_dim, 0.0).astype(BF16)
    tg = tm // QKV_SPLIT
    for gi in range(QKV_SPLIT):
        rows = slice(gi * tg, (gi + 1) * tg)
        h = _norm_mod(x_ref[rows, :], g_ref[...], sc_ref[...], sh_ref[...]).astype(BF16)
        y = lax.dot_general(wqt_ref[...], h, _NT, preferred_element_type=F32)
        y3 = y.reshape(d // head_dim, head_dim, tg)
        ss = jnp.mean(y3 * y3, axis=1, keepdims=True)
        qt_ref[:, rows] = ((y3 * lax.rsqrt(ss + NORM_EPS)).reshape(d, tg) * gq_ref[...]).astype(BF16)
        y = jnp.dot(h, wk_ref[...], preferred_element_type=F32)
        ysq = (y * y).astype(BF16)
        for ci in range(d // cw):
            cols = slice(ci * cw, (ci + 1) * cw)
            ss = jnp.dot(ysq[:, cols], gmat, preferred_element_type=F32)
            k_ref[rows, cols] = (y[:, cols] * lax.rsqrt(ss + NORM_EPS) * gk_ref[:, cols]).astype(BF16)
        vt_ref[:, rows] = lax.dot_general(wvt_ref[...], h, _NT, preferred_element_type=F32).astype(BF16)


def _qkv_proj(x2, g, sc, sh, w, gq, gk, seq, head_dim):
    t, d = x2.shape
    tm = ROW_TILE * QKV_SPLIT
    per_b = seq // tm
    vec = lambda i: (i // per_b, 0, 0)
    full = lambda i: (0, 0)
    return pl.pallas_call(
        functools.partial(_qkv_kernel, head_dim=head_dim),
        out_shape=(jax.ShapeDtypeStruct((d, t), BF16),
                   jax.ShapeDtypeStruct((t, d), BF16),
                   jax.ShapeDtypeStruct((d, t), BF16)),
        grid=(t // tm,),
        in_specs=[
            pl.BlockSpec((tm, d), lambda i: (i, 0)),
            pl.BlockSpec((1, d), full),
            pl.BlockSpec((None, 1, d), vec),
            pl.BlockSpec((None, 1, d), vec),
            pl.BlockSpec((d, 3 * d), full, pipeline_mode=pl.Buffered(1)),
            pl.BlockSpec((d, 1), full),
            pl.BlockSpec((1, d), full),
        ],
        out_specs=(pl.BlockSpec((d, tm), lambda i: (0, i)),
                   pl.BlockSpec((tm, d), lambda i: (i, 0)),
                   pl.BlockSpec((d, tm), lambda i: (0, i))),
        scratch_shapes=[pltpu.VMEM((d, d), BF16), pltpu.VMEM((d, d), BF16), pltpu.VMEM((d, d), BF16)],
        compiler_params=_cparams(("arbitrary",)),
        name="qkv_proj",
    )(x2, g, sc, sh, w, gq, gk)


def _softmax_pv(s, vt1, m_ref, acc_ref):
    m_prev = m_ref[...]
    m_new = jnp.maximum(m_prev, jnp.max(s, axis=0, keepdims=True))
    alpha = jnp.exp2(m_prev - m_new)
    p = jnp.exp2(s - m_new).astype(BF16)
    acc_ref[...] = alpha * acc_ref[...] + jnp.dot(vt1, p, preferred_element_type=F32)
    m_ref[...] = m_new


def _split_maps(qt, head_dim):
    row = lax.broadcasted_iota(jnp.int32, qt.shape, 0)
    zero = jnp.zeros_like(qt)
    return jnp.where(row < head_dim, qt, zero), jnp.where(row >= head_dim, qt, zero)


ONES_ROWS = 16


def _init_state(refs):
    for m_ref, acc_ref in refs:
        m_ref[...] = jnp.full(m_ref.shape, NEG_INF, F32)
        acc_ref[...] = jnp.zeros(acc_ref.shape, F32)


def _normalized(acc_ref):
    acc = acc_ref[...]
    return acc[:V7X_LANES] / acc[V7X_LANES:V7X_LANES + 1]


def _toeplitz(x_row, n):
    x = jnp.broadcast_to(x_row, (n, x_row.shape[1]))
    return pltpu.roll(x, 0, 1, stride=1, stride_axis=0)[:, :n]


def _chunk_delta(n):
    kc = lax.broadcasted_iota(jnp.int32, (n, n), 0) // CHUNK
    qc = lax.broadcasted_iota(jnp.int32, (n, n), 1) // CHUNK
    return qc - kc


def _attn_a_kernel(qt_ref, qn_ref, k_ref, vt_ref, x_ref, lam_ref, sub_ref, o_ref, z_ref,
                   tab, fbuf, nbuf, m_st, acc_st, *, n_off, lambda_init):
    tq = ATTN_TILE
    step = pl.program_id(2)
    z_ref[...] = jnp.zeros(z_ref.shape, z_ref.dtype)

    @pl.when(step == 0)
    def _():
        for o in range(n_off):
            t = _toeplitz(x_ref[o], tq) * LOG2E
            if o == 0:
                t = jnp.where(_chunk_delta(tq) >= 0, t, NEG_INF)
            tab[o] = t

    for c in range(2):
        _init_state(((m_st.at[c, 0], acc_st.at[c, 0]), (m_st.at[c, 1], acc_st.at[c, 1])))
    q_maps = [_split_maps(qt_ref[:, c * tq:(c + 1) * tq], A_HEAD_DIM) for c in range(2)]

    def k_tile(j):
        return k_ref[pl.ds(pl.multiple_of(j * tq, tq), tq), :]

    def vt1_tile(j):
        vt = vt_ref[:, pl.ds(pl.multiple_of(j * tq, tq), tq)]
        return jnp.concatenate([vt, jnp.ones((ONES_ROWS, tq), vt.dtype)], axis=0)

    def scores(buf, k, c):
        for m in range(2):
            buf[c, m] = jnp.dot(k, q_maps[c][m], preferred_element_type=F32)

    def consume(buf, vt1, c, o):
        for m in range(2):
            s = buf[c, m]
            _softmax_pv(s if o >= n_off else s + tab[o], vt1, m_st.at[c, m], acc_st.at[c, m])

    def near_block(first_tile, n_tiles, first_scored):
        units = [[c for c in range(2) if n_tiles - 2 + c - s >= 0] for s in range(n_tiles)]
        buf_of = lambda s: fbuf.at[0] if s == 0 else nbuf.at[s - 1]

        for s in range(1 if first_scored else 0, n_tiles):
            k = k_tile(first_tile + s)
            for c in units[s]:
                scores(buf_of(s), k, c)
        for s in range(n_tiles):
            vt1 = vt1_tile(first_tile + s)
            for c in units[s]:
                consume(buf_of(s), vt1, c, n_tiles - 2 + c - s)

    first_general = (n_off - 1) // 2
    for i0 in range(first_general):
        @pl.when(step == i0)
        def _():
            near_block(0, 2 * i0 + 2, False)

    def far_step(j, parity):
        k = k_tile(j + 1)
        for c in range(2):
            scores(fbuf.at[1 - parity], k, c)
        vt1 = vt1_tile(j)
        for c in range(2):
            consume(fbuf.at[parity], vt1, c, n_off)

    n_far = jnp.maximum(2 * step - (n_off - 1), 0)
    rem = n_far % FAR_UNROLL

    @pl.when(rem >= 2)
    def _():
        far_step(0, 0)
        far_step(1, 1)

    def far_trip(t, carry):
        for u in range(FAR_UNROLL):
            far_step(FAR_UNROLL * t + rem + u, u % 2)
        return carry
    lax.fori_loop(0, n_far // FAR_UNROLL, far_trip, 0)

    @pl.when(step >= first_general)
    def _():
        near_block(n_far, n_off + 1, True)

    def epilogue():
        lam = lam_ref[...]
        lam_full = (jnp.exp(jnp.sum(lam[0:1] * lam[1:2], axis=-1, keepdims=True))
                    - jnp.exp(jnp.sum(lam[2:3] * lam[3:4], axis=-1, keepdims=True)) + lambda_init)
        for c in range(2):
            a = _normalized(acc_st.at[c, 0]) - lam_full * _normalized(acc_st.at[c, 1])
            ms = jnp.mean(a * a, axis=0, keepdims=True)
            an = a * lax.rsqrt(ms + NORM_EPS)
            o_ref[c * tq:(c + 1) * tq, :] = (an.T * (sub_ref[...] * (1.0 - lambda_init))).astype(BF16)

    last = pl.num_programs(2) - 1

    @pl.when(step < last)
    def _():
        k = k_tile(0)
        for c in range(2):
            qa, qb = _split_maps(qn_ref[:, c * tq:(c + 1) * tq], A_HEAD_DIM)
            fbuf[0, c, 0] = jnp.dot(k, qa, preferred_element_type=F32)
            fbuf[0, c, 1] = jnp.dot(k, qb, preferred_element_type=F32)
        epilogue()

    @pl.when(step == last)
    def _():
        epilogue()


def _attn_b_kernel(qt_ref, k_ref, vt_ref, x_ref, o_ref, tab, sbuf):
    tq = BAND_TILE
    n_blk = qt_ref.shape[1] // tq
    n_off = tab.shape[0] // 2
    step = pl.program_id(2)

    @pl.when(step == 0)
    def _():
        for m in range(2):
            for o in range(n_off):
                d = _chunk_delta(tq) + o * (tq // CHUNK)
                t = _toeplitz(x_ref[n_off * m + o], tq) * LOG2E
                tab[n_off * m + o] = jnp.where((d >= 0) & (d <= LEFT_CHUNKS), t, NEG_INF)

    q_maps = [_split_maps(qt_ref[:, c * tq:(c + 1) * tq], B_HEAD_DIM) for c in range(n_blk)]

    def run(first_step):
        pairs = [(c, o) for c in range(n_blk) for o in range(n_off - 1, -1, -1)
                 if not first_step or c - o >= 0]
        k_tiles, vt_tiles = {}, {}
        for c, o in pairs:
            if c - o not in k_tiles:
                ks = pl.multiple_of((n_blk * step + c - o) * tq, tq)
                k_tiles[c - o] = k_ref[pl.ds(ks, tq), :]
                vt = vt_ref[:, pl.ds(ks, tq)]
                vt_tiles[c - o] = jnp.concatenate([vt, jnp.ones((ONES_ROWS, tq), vt.dtype)], axis=0)
        def tile_scores(c):
            for o in [o for cc, o in pairs if cc == c]:
                for m in range(2):
                    sbuf[n_off * c + o, m] = jnp.dot(k_tiles[c - o], q_maps[c][m], preferred_element_type=F32)

        for c in range(min(BAND_AHEAD, n_blk)):
            tile_scores(c)
        for c in range(n_blk):
            if c + BAND_AHEAD < n_blk:
                tile_scores(c + BAND_AHEAD)
            outs = []
            for m in range(2):
                offs = [o for cc, o in pairs if cc == c]
                ss = [sbuf[n_off * c + o, m] + tab[n_off * m + o] for o in offs]
                mx = functools.reduce(jnp.maximum, [jnp.max(s, axis=0, keepdims=True) for s in ss])
                acc = sum(jnp.dot(vt_tiles[c - o], jnp.exp2(s - mx).astype(BF16), preferred_element_type=F32)
                          for o, s in zip(offs, ss))
                outs.append(acc[:V7X_LANES] / acc[V7X_LANES:V7X_LANES + 1])
            row = lax.broadcasted_iota(jnp.int32, outs[0].shape, 0)
            o_ref[c * tq:(c + 1) * tq, :] = jnp.where(row < B_HEAD_DIM, outs[0], outs[1]).T.astype(BF16)

    @pl.when(step == 0)
    def _():
        run(True)

    @pl.when(step >= 1)
    def _():
        run(False)


def _attn_specs(batch, seq, nq, tq):
    return dict(
        q=pl.BlockSpec((V7X_LANES, tq), lambda b, h, i: (h, b * nq + i)),
        k=pl.BlockSpec((seq, V7X_LANES), lambda b, h, i: (b, h)),
        v=pl.BlockSpec((V7X_LANES, seq), lambda b, h, i: (h, b)),
        o=pl.BlockSpec((tq, V7X_LANES), lambda b, h, i: (b * nq + i, h)),
    )


def _attention_a(qt, k, vt, xvec, lam, sub_g, batch, seq, lambda_init, zero_rows):
    d, t = qt.shape
    tq = ATTN_TILE
    n_off = xvec.shape[1]
    assert n_off % 2 == 1 and seq % (2 * tq) == 0
    nq = seq // (2 * tq)
    n_h = d // V7X_LANES
    zr = zero_rows // (batch * n_h * nq)
    assert zr * batch * n_h * nq == zero_rows and zr % MOE_CHUNK == 0
    sp = _attn_specs(batch, seq, nq, 2 * tq)
    rows = V7X_LANES + ONES_ROWS
    return pl.pallas_call(
        functools.partial(_attn_a_kernel, n_off=n_off, lambda_init=lambda_init),
        out_shape=(jax.ShapeDtypeStruct((t, d), BF16), jax.ShapeDtypeStruct((zero_rows, d), BF16)),
        grid=(batch, n_h, nq),
        in_specs=[sp["q"],
                  pl.BlockSpec((V7X_LANES, 2 * tq), lambda b, h, i: (h, b * nq + jnp.minimum(i + 1, nq - 1))),
                  sp["k"], sp["v"],
                  pl.BlockSpec((None,) + xvec.shape[1:], lambda b, h, i: (h, 0, 0, 0)),
                  pl.BlockSpec(lam.shape, lambda b, h, i: (0, 0)),
                  pl.BlockSpec(sub_g.shape, lambda b, h, i: (0, 0))],
        out_specs=(sp["o"], pl.BlockSpec((zr, d), lambda b, h, i: ((b * n_h + h) * nq + i, 0))),
        scratch_shapes=[pltpu.VMEM((n_off, tq, tq), F32), pltpu.VMEM((2, 2, 2, tq, tq), F32),
                        pltpu.VMEM((n_off, 2, 2, tq, tq), F32),
                        pltpu.VMEM((2, 2, 1, tq), F32), pltpu.VMEM((2, 2, rows, tq), F32)],
        compiler_params=_cparams(("parallel", "parallel", "arbitrary")),
        name="diff_attention",
    )(qt, qt, k, vt, xvec, lam, sub_g)


def _attention_b(qt, k, vt, xvec, batch, seq):
    d, t = qt.shape
    tq = BAND_TILE
    g = BAND_QBLOCKS
    n_off = xvec.shape[1] // 2
    assert g >= n_off - 1 and seq % (g * tq) == 0
    nq = seq // (g * tq)
    sp = _attn_specs(batch, seq, nq, g * tq)
    return pl.pallas_call(
        _attn_b_kernel,
        out_shape=jax.ShapeDtypeStruct((t, d), BF16),
        grid=(batch, d // V7X_LANES, nq),
        in_specs=[sp["q"], sp["k"], sp["v"],
                  pl.BlockSpec((None,) + xvec.shape[1:], lambda b, h, i: (h, 0, 0, 0))],
        out_specs=sp["o"],
        scratch_shapes=[pltpu.VMEM((2 * n_off, tq, tq), F32), pltpu.VMEM((n_off * g, 2, tq, tq), F32)],
        compiler_params=_cparams(("parallel", "parallel", "arbitrary")),
        name="chunk_attention",
    )(qt, k, vt, xvec)


def _t5_bucket(rel):
    nb = T5_BUCKETS // 2
    ret = jnp.where(rel > 0, nb, 0)
    n = jnp.abs(rel)
    max_exact = nb // 2
    nf = jnp.maximum(n, 1).astype(F32)
    large = max_exact + (jnp.log(nf / max_exact) / math.log(T5_MAX_DIST / max_exact)
                         * (nb - max_exact)).astype(jnp.int32)
    large = jnp.minimum(large, nb - 1)
    return ret + jnp.where(n < max_exact, n, large)


def _t5_const_distance():
    nb = T5_BUCKETS // 2
    max_exact = nb // 2
    n = np.arange(max_exact, 4 * T5_MAX_DIST, dtype=np.float64)
    large = max_exact + np.floor(np.log(n / max_exact) / math.log(T5_MAX_DIST / max_exact) * (nb - max_exact))
    below = np.nonzero(large < nb - 1)[0]
    return int(n[below[-1]]) + 2


def _tile_rel(tile, n_off):
    i = jnp.arange(2 * tile, dtype=jnp.int32)
    rel = jnp.where(i < tile, -i, 2 * tile - i)
    return rel[None, :] - tile * jnp.arange(n_off, dtype=jnp.int32)[:, None]


def _t5_vectors(t5_bias, tile):
    n_off = 1
    while (n_off - 1) * tile + 1 < _t5_const_distance():
        n_off += 1
    tb = t5_bias.astype(F32)
    vals = tb[_t5_bucket(_tile_rel(tile, n_off))] - tb[T5_BUCKETS // 2 - 1]
    return vals.transpose(2, 0, 1)[:, :, None, :]


def _band_vectors(rel_bias, tile):
    n_off = LEFT_CHUNKS * CHUNK // tile + 1
    idx = jnp.clip(_tile_rel(tile, n_off), -MAX_REL, MAX_REL) + MAX_REL
    vals = rel_bias.astype(F32)[:, idx]
    return vals.reshape(rel_bias.shape[0] // 2, 2 * n_off, 1, 2 * tile)


def _top2_sum4(r0, r1, r2, r3):
    a, b = jnp.maximum(r0, r1), jnp.minimum(r0, r1)
    c, d = jnp.maximum(r2, r3), jnp.minimum(r2, r3)
    return jnp.maximum(a, c) + jnp.maximum(jnp.minimum(a, c), jnp.maximum(b, d))


def _route(logits, rbias, before_ref):
    n_e, n = logits.shape
    scores = 1.0 / (1.0 + jnp.exp(-logits))
    sel = scores + rbias
    row = lax.broadcasted_iota(jnp.int32, sel.shape, 0)
    best = None
    for g in range(N_GROUPS):
        rows = [sel[g * E_PER_GROUP + i: g * E_PER_GROUP + i + 1, :] for i in range(E_PER_GROUP)]
        gs = _top2_sum4(*rows)
        if best is None:
            best, gidx = gs, jnp.zeros(gs.shape, jnp.int32)
        else:
            gidx = jnp.where(gs > best, g, gidx)
            best = jnp.maximum(best, gs)
    masked = jnp.where(row // E_PER_GROUP == gidx, sel, NEG_INF)
    m1 = jnp.max(masked, axis=0, keepdims=True)
    i1 = jnp.min(jnp.where(masked == m1, row, n_e), axis=0, keepdims=True)
    masked2 = jnp.where(row == i1, -3.0e38, masked)
    m2 = jnp.max(masked2, axis=0, keepdims=True)
    i2 = jnp.min(jnp.where(masked2 == m2, row, n_e), axis=0, keepdims=True)
    w1 = jnp.sum(jnp.where(row == i1, scores, 0.0), axis=0, keepdims=True)
    w2 = jnp.sum(jnp.where(row == i2, scores, 0.0), axis=0, keepdims=True)
    den = w1 + w2
    member = jnp.where((row == i1) | (row == i2), 1.0, 0.0).astype(BF16)
    before = before_ref[...]
    pos = jnp.concatenate(
        [jnp.dot(member[:, i:i + MOE_TILE], before, preferred_element_type=F32)
         for i in range(0, n, MOE_TILE)], axis=1)
    p1 = jnp.sum(jnp.where(row == i1, pos, 0.0), axis=0, keepdims=True)
    p2 = jnp.sum(jnp.where(row == i2, pos, 0.0), axis=0, keepdims=True)
    out_row = lax.broadcasted_iota(jnp.int32, (8, n), 0)
    out = jnp.zeros((8, n), F32)
    for r, val in enumerate((i1.astype(F32), i2.astype(F32), p1, p2, w1 / den, w2 / den)):
        out = jnp.where(out_row == r, val, out)
    return out


def _oproj_kernel(o_ref, wo_ref, x_ref, g1_ref, gn_ref, sc_ref, sh_ref, rwt_ref, rb_ref,
                  xo_ref, h_ref, route_ref, before_ref):
    @pl.when(pl.program_id(0) == 0)
    def _():
        t_from = lax.broadcasted_iota(jnp.int32, before_ref.shape, 0)
        t_to = lax.broadcasted_iota(jnp.int32, before_ref.shape, 1)
        before_ref[...] = jnp.where(t_from < t_to, 1.0, 0.0).astype(BF16)

    n_e = rb_ref.shape[0]
    rw = rwt_ref[...]
    rows = o_ref.shape[0] // OPROJ_SPLIT
    groups = [slice(r * rows, (r + 1) * rows) for r in range(OPROJ_SPLIT)]
    ys = [jnp.dot(o_ref[g, :], wo_ref[...], preferred_element_type=F32) for g in groups]
    logits = []
    for g, y in zip(groups, ys):
        xn = x_ref[g, :] + g1_ref[...] * y
        xo_ref[g, :] = xn
        h = _norm_mod(xn, gn_ref[...], sc_ref[...], sh_ref[...])
        h_hi = h.astype(BF16)
        h_ref[g, :] = h_hi
        h_lo = (h - h_hi.astype(F32)).astype(BF16)
        main = lax.dot_general(rw, h_hi, _NT, preferred_element_type=F32)
        corr = lax.dot_general(rw[:n_e], h_lo, _NT, preferred_element_type=F32)
        logits.append(main[:n_e] + main[n_e:] + corr)
    route_ref[...] = _route(jnp.concatenate(logits, axis=1), rb_ref[...], before_ref)


def _out_proj(o, wo_bf16, x2, g1, gn, sc, sh, rwt, rb, seq):
    t, d = x2.shape
    tm = OPROJ_ROW_TILE
    per_b = seq // tm
    n_e = rb.shape[0]
    vec = lambda i: (i // per_b, 0, 0)
    return pl.pallas_call(
        _oproj_kernel,
        out_shape=(jax.ShapeDtypeStruct((t, d), F32),
                   jax.ShapeDtypeStruct((t, d), BF16),
                   jax.ShapeDtypeStruct((8, t), F32)),
        grid=(t // tm,),
        in_specs=[
            pl.BlockSpec((tm, d), lambda i: (i, 0)),
            pl.BlockSpec((d, d), lambda i: (0, 0)),
            pl.BlockSpec((tm, d), lambda i: (i, 0)),
            pl.BlockSpec((None, 1, d), vec),
            pl.BlockSpec((1, d), lambda i: (0, 0)),
            pl.BlockSpec((None, 1, d), vec),
            pl.BlockSpec((None, 1, d), vec),
            pl.BlockSpec((2 * n_e, d), lambda i: (0, 0)),
            pl.BlockSpec((n_e, 1), lambda i: (0, 0)),
        ],
        out_specs=(pl.BlockSpec((tm, d), lambda i: (i, 0)),
                   pl.BlockSpec((tm, d), lambda i: (i, 0)),
                   pl.BlockSpec((8, tm), lambda i: (0, i))),
        scratch_shapes=[pltpu.VMEM((MOE_TILE, MOE_TILE), BF16)],
        compiler_params=_cparams(("arbitrary",)),
        name="out_proj_route",
    )(o, wo_bf16, x2, g1, gn, sc, sh, rwt, rb)


def _moe_plan(route, t, tm, n_e):
    n_tiles = t // tm
    ch, ft = MOE_CHUNK, MOE_FFN_TILE
    loc_rows, main_rows, n_sorted = _moe_rows(t, tm, n_e)
    ids = jnp.arange(n_e, dtype=jnp.int32)
    e = route[0:2].astype(jnp.int32)
    pos = route[2:4].astype(jnp.int32)
    oh = (e[:, :, None] == ids).astype(jnp.int32)
    cnt = oh.sum(0).reshape(n_tiles, tm, n_e).sum(1)
    seg = (cnt + ch - 1) // ch * ch
    loc = jnp.cumsum(seg, axis=1) - seg
    tot = seg.sum(0)
    totp = (tot + ft - 1) // ft * ft
    base = jnp.cumsum(totp) - totp
    gdest = base[None, :] + jnp.cumsum(seg, axis=0) - seg
    col = (oh * jnp.repeat(loc, tm, axis=0)[None]).sum(-1) + pos

    row0 = jnp.arange(loc_rows // ch, dtype=jnp.int32) * ch
    ej = (row0[None, :, None] >= (loc + seg)[:, None, :]).sum(-1)
    ohj = (jnp.minimum(ej, n_e - 1)[..., None] == ids).astype(jnp.int32)
    dst = (ohj * (gdest - loc)[:, None, :]).sum(-1) + row0[None, :]
    dump = main_rows + (jnp.arange(n_tiles, dtype=jnp.int32) % 2)[:, None] * loc_rows + row0[None, :]
    dst = jnp.where(ej < n_e, dst, dump) // ch

    r0 = jnp.arange(n_sorted // ft, dtype=jnp.int32) * ft
    ends = base + totp
    te = jnp.minimum((r0[:, None] >= ends[None, :]).sum(-1), n_e - 1)
    tvalid = (r0 < ends[-1]).astype(jnp.int32)
    tfirst = tvalid * (r0 == (((te[:, None] == ids) * base[None, :]).sum(-1))).astype(jnp.int32)
    return col, dst.astype(jnp.int32), te.astype(jnp.int32), tvalid, tfirst


def _moe_rows(t, tm, n_e):
    ch, ft = MOE_CHUNK, MOE_FFN_TILE
    loc_rows = 2 * tm + n_e * ch
    main = 2 * t + (t // tm) * n_e * (ch - 1) + n_e * (ft - 1)
    main = (main + ft - 1) // ft * ft
    return loc_rows, main, (main + 2 * loc_rows + ft - 1) // ft * ft


def _chunk_copies(dst_ref, tile, local_ref, sorted_ref, sem, to_sorted):
    ch = MOE_CHUNK
    copies = []
    for j in range(local_ref.shape[0] // ch):
        far = sorted_ref.at[pl.ds(pl.multiple_of(dst_ref[tile, j] * ch, ch), ch)]
        near = local_ref.at[pl.ds(j * ch, ch)]
        copies.append(pltpu.make_async_copy(near, far, sem) if to_sorted
                      else pltpu.make_async_copy(far, near, sem))
    return copies


def _dispatch_kernel(dst_ref, col_ref, h_ref, init_ref, xs_ref, comp, sem):
    del init_ref
    tile = pl.program_id(0)
    last = pl.num_programs(0) - 1
    slot = tile % 2

    def wait_slot(s):
        pltpu.make_async_copy(comp.at[s], xs_ref.at[pl.ds(0, comp.shape[1])], sem.at[s]).wait()

    @pl.when(tile >= 2)
    def _():
        wait_slot(slot)

    col = col_ref[...]
    srow = lax.broadcasted_iota(jnp.int32, (comp.shape[1], col.shape[1]), 0)
    sel = jnp.where((srow == col[0:1]) | (srow == col[1:2]), 1.0, 0.0).astype(BF16)
    comp[slot] = jnp.dot(sel, h_ref[...], preferred_element_type=F32).astype(BF16)
    for cp in _chunk_copies(dst_ref, tile, comp.at[slot], xs_ref, sem.at[slot], True):
        cp.start()

    @pl.when(tile == last)
    def _():
        wait_slot(slot)

    @pl.when((tile == last) & (tile >= 1))
    def _():
        wait_slot(1 - slot)


def _dispatch(dst, col, h, n_e, init):
    t, d = h.shape
    tm = MOE_TILE
    loc_rows, _, n_sorted = _moe_rows(t, tm, n_e)
    assert init.shape == (n_sorted, d) and init.dtype == BF16
    return pl.pallas_call(
        _dispatch_kernel,
        out_shape=jax.ShapeDtypeStruct((n_sorted, d), BF16),
        grid_spec=pltpu.PrefetchScalarGridSpec(
            num_scalar_prefetch=1,
            grid=(t // tm,),
            in_specs=[pl.BlockSpec((2, tm), lambda i, dst: (0, i)),
                      pl.BlockSpec((tm, d), lambda i, dst: (i, 0)),
                      pl.BlockSpec(memory_space=pl.ANY)],
            out_specs=pl.BlockSpec(memory_space=pl.ANY),
            scratch_shapes=[pltpu.VMEM((2, loc_rows, d), BF16), pltpu.SemaphoreType.DMA((2,))],
        ),
        input_output_aliases={3: 0},
        compiler_params=_cparams(("arbitrary",)),
        name="moe_dispatch",
    )(dst, col, h, init)


def _expert_kernel(te_ref, tv_ref, tf_ref, xs_ref, wg_ref, wu_ref, wd_ref, ys_ref, wg_b, wu_b, wd_b):
    del te_ref
    s = pl.program_id(0)

    @pl.when(tf_ref[s] == 1)
    def _():
        wg_b[...] = wg_ref[...].astype(BF16)
        wu_b[...] = wu_ref[...].astype(BF16)
        wd_b[...] = wd_ref[...].astype(BF16)

    @pl.when(tv_ref[s] == 1)
    def _():
        rows = ys_ref.shape[0] // MOE_FFN_SPLIT
        hs = []
        for r in range(MOE_FFN_SPLIT):
            xs = xs_ref[r * rows:(r + 1) * rows, :]
            hg = jnp.dot(xs, wg_b[...], preferred_element_type=F32)
            hu = jnp.dot(xs, wu_b[...], preferred_element_type=F32)
            hs.append((hg, hu))
        for r, (hg, hu) in enumerate(hs):
            he = hg * (1.0 / (1.0 + jnp.exp(-hg))) * hu
            ys_ref[r * rows:(r + 1) * rows, :] = jnp.dot(
                he.astype(BF16), wd_b[...], preferred_element_type=F32).astype(BF16)

    @pl.when(tv_ref[s] == 0)
    def _():
        ys_ref[...] = jnp.zeros(ys_ref.shape, ys_ref.dtype)


def _experts(te, tvalid, tfirst, xs, wg, wu, wd, layer):
    n_sorted, d = xs.shape
    f = wg.shape[3]
    ft = MOE_FFN_TILE
    wspec = lambda r, c: pl.BlockSpec((None, None, r, c), lambda s, te, tv, tf: (layer, te[s], 0, 0))
    return pl.pallas_call(
        _expert_kernel,
        out_shape=jax.ShapeDtypeStruct((n_sorted, d), BF16),
        grid_spec=pltpu.PrefetchScalarGridSpec(
            num_scalar_prefetch=3,
            grid=(n_sorted // ft,),
            in_specs=[pl.BlockSpec((ft, d), lambda s, te, tv, tf: (s, 0)),
                      wspec(d, f), wspec(d, f), wspec(f, d)],
            out_specs=pl.BlockSpec((ft, d), lambda s, te, tv, tf: (s, 0)),
            scratch_shapes=[pltpu.VMEM((d, f), BF16), pltpu.VMEM((d, f), BF16), pltpu.VMEM((f, d), BF16)],
        ),
        compiler_params=_cparams(("arbitrary",)),
        name="moe_experts",
    )(te, tvalid, tfirst, xs, wg, wu, wd)


def _combine_kernel(dst_ref, w_ref, ys_ref, x_ref, g2_ref, o_ref, comp, sem):
    tile = pl.program_id(0)
    slot = tile % 2

    def fetch(t, s):
        for cp in _chunk_copies(dst_ref, t, comp.at[s], ys_ref, sem.at[s], False):
            cp.start()

    def wait_slot(s):
        pltpu.make_async_copy(ys_ref.at[pl.ds(0, comp.shape[1])], comp.at[s], sem.at[s]).wait()

    last = pl.num_programs(0) - 1

    @pl.when(tile == 0)
    def _():
        fetch(tile, slot)

    fetch(jnp.minimum(tile + 1, last), 1 - slot)

    w = w_ref[...].T
    scol = lax.broadcasted_iota(jnp.int32, (w.shape[0], comp.shape[1]), 1).astype(F32)
    selw = (jnp.where(scol == w[:, 6:7], w[:, 4:5], 0.0)
            + jnp.where(scol == w[:, 7:8], w[:, 5:6], 0.0)).astype(BF16)
    wait_slot(slot)
    acc = jnp.dot(selw, comp[slot], preferred_element_type=F32)
    o_ref[...] = x_ref[...] + g2_ref[...] * acc

    @pl.when(tile == last)
    def _():
        wait_slot(1 - slot)


def _combine(dst, route_col, ys, x2, g2, seq, n_e):
    t, d = x2.shape
    tm = MOE_TILE
    per_b = seq // tm
    loc_rows = _moe_rows(t, tm, n_e)[0]
    return pl.pallas_call(
        _combine_kernel,
        out_shape=jax.ShapeDtypeStruct((t, d), F32),
        grid_spec=pltpu.PrefetchScalarGridSpec(
            num_scalar_prefetch=1,
            grid=(t // tm,),
            in_specs=[pl.BlockSpec((8, tm), lambda i, dst: (0, i)),
                      pl.BlockSpec(memory_space=pl.ANY),
                      pl.BlockSpec((tm, d), lambda i, dst: (i, 0)),
                      pl.BlockSpec((None, 1, d), lambda i, dst: (i // per_b, 0, 0))],
            out_specs=pl.BlockSpec((tm, d), lambda i, dst: (i, 0)),
            scratch_shapes=[pltpu.VMEM((2, loc_rows, d), BF16), pltpu.SemaphoreType.DMA((2,))],
        ),
        compiler_params=_cparams(("arbitrary",)),
        name="moe_combine",
    )(dst, route_col, ys, x2, g2)


def _moe(h, route, wg, wu, wd, layer, x2, g2, seq, sorted_init):
    t = x2.shape[0]
    n_e = wg.shape[1]
    col, dst, te, tvalid, tfirst = _moe_plan(route, t, MOE_TILE, n_e)
    xs = _dispatch(dst, col, h, n_e, sorted_init)
    ys = _experts(te, tvalid, tfirst, xs, wg, wu, wd, layer)
    route_col = jnp.concatenate([route[:6], col.astype(F32)], axis=0)
    return _combine(dst, route_col, ys, x2, g2, seq, n_e), ys


def kernel(x, c, ada_w, ada_b, norm_mix_g, norm_ffn_g, t5_bias, a_w_qkv, a_q_gain, a_k_gain, a_lambda, a_subln_g, a_w_o, b_w_qkv, b_q_gain, b_k_gain, b_rel_bias, b_w_o, router_w, router_bias, moe_w_gate, moe_w_up, moe_w_down):
    batch, seq, d = x.shape
    depth = ada_w.shape[0]
    assert seq % ATTN_TILE == 0 and seq % ROW_TILE == 0 and d == A_HEADS * 2 * A_HEAD_DIM
    assert d == B_HEADS * B_HEAD_DIM and A_HEAD_DIM == B_HEAD_DIM

    c_pad = jnp.zeros((8, d), F32).at[:batch].set(c.astype(F32))
    mod = _modulation(c_pad, ada_w.astype(F32), ada_b.astype(F32))[:, :batch]
    mod = mod.reshape(depth, batch, 6, 1, d)

    rwt = router_w.astype(F32).T
    rwt_hi = lax.bitcast_convert_type(
        lax.bitcast_convert_type(rwt, jnp.uint32) & jnp.uint32(0xFFFF0000), F32)
    rwt = jnp.concatenate([rwt_hi, rwt - rwt_hi], axis=0).astype(BF16)
    rb = router_bias.astype(F32).reshape(-1, 1)

    x2 = x.astype(F32).reshape(batch * seq, d)
    sorted_buf = None
    for i in range(depth):
        sh1, sc1, g1, sh2, sc2, g2 = [mod[i, :, k] for k in range(6)]
        j = i // 2
        if i % 2 == 0:
            w_qkv, qg, kg, w_o = a_w_qkv[j], a_q_gain[j], a_k_gain[j], a_w_o[j]
        else:
            w_qkv, qg, kg, w_o = b_w_qkv[j], b_q_gain[j], b_k_gain[j], b_w_o[j]
        n_rep = d // qg.shape[0]
        gq = (jnp.tile(qg.astype(F32), n_rep) * (A_HEAD_DIM ** -0.5 * LOG2E)).reshape(d, 1)
        gk = jnp.tile(kg.astype(F32), n_rep).reshape(1, d)
        qt, k, vt = _qkv_proj(x2, norm_mix_g[i].reshape(1, d), sc1, sh1, w_qkv.astype(F32),
                              gq, gk, seq, A_HEAD_DIM)
        if i % 2 == 0:
            lambda_init = 0.8 - 0.6 * math.exp(-0.3 * i)
            n_sorted = _moe_rows(batch * seq, MOE_TILE, moe_w_gate.shape[1])[2]
            o, zeros_buf = _attention_a(qt, k, vt, _t5_vectors(t5_bias, ATTN_TILE), a_lambda[j].astype(F32),
                                        a_subln_g[j].reshape(1, -1), batch, seq, lambda_init, n_sorted)
            if sorted_buf is None:
                sorted_buf = zeros_buf
        else:
            o = _attention_b(qt, k, vt, _band_vectors(b_rel_bias[j], BAND_TILE), batch, seq)
        x2, h, route = _out_proj(o, w_o.astype(BF16), x2, g1, norm_ffn_g[i].reshape(1, d),
                                 sc2, sh2, rwt, rb, seq)
        x2, sorted_buf = _moe(h, route, moe_w_gate, moe_w_up, moe_w_down, i, x2, g2, seq, sorted_buf)
    return x2.reshape(batch, seq, d)
```

```python
import functools
import math

import numpy as np
import jax
import jax.numpy as jnp
from jax import lax
from jax.experimental import pallas as pl
from jax.experimental.pallas import tpu as pltpu

F32 = jnp.float32
BF16 = jnp.bfloat16

CHUNK = 64
A_HEADS = 8
A_HEAD_DIM = 64
T5_BUCKETS = 32
T5_MAX_DIST = 1024
B_HEADS = 16
B_HEAD_DIM = 64
LEFT_CHUNKS = 8
MAX_REL = 256
N_EXPERTS = 16
N_GROUPS = 4
E_PER_GROUP = N_EXPERTS // N_GROUPS
NORM_EPS = 1e-6
NEG_INF = -1e30
LOG2E = math.log2(math.e)

V7X_LANES = 128
V7X_MXU_DIM = 256
DMA_PRIORITIES = 2

ATTN_TILE = 512
MOD_COL_TILE = 1536
FAR_UNROLL = 4
BAND_TILE = 256
BAND_QBLOCKS = 16
BAND_AHEAD = 2
ROW_TILE = 512
QKV_SPLIT = 2
MOE_TILE = 512
OPROJ_ROW_TILE = 1024
OPROJ_SPLIT = 4
MOE_CHUNK = 16
MOE_FFN_TILE = 512
MOE_FFN_SPLIT = 2
VMEM_LIMIT = 56 * 1024 * 1024

_NT = (((1,), (1,)), ((), ()))


def _cparams(sem):
    return pltpu.CompilerParams(dimension_semantics=sem, vmem_limit_bytes=VMEM_LIMIT)


def _mod_kernel(c_ref, w_ref, b_ref, o_ref):
    c = c_ref[...]
    s = c * (1.0 / (1.0 + jnp.exp(-c)))
    w = w_ref[...]
    s_hi, w_hi = s.astype(BF16), w.astype(BF16)
    s_lo = (s - s_hi.astype(F32)).astype(BF16)
    w_lo = (w - w_hi.astype(F32)).astype(BF16)
    dot = functools.partial(jnp.dot, preferred_element_type=F32)
    o_ref[...] = dot(s_hi, w_hi) + dot(s_hi, w_lo) + dot(s_lo, w_hi) + b_ref[...]


def _modulation(c_pad, ada_w, ada_b):
    depth, d, n = ada_w.shape
    rows = c_pad.shape[0]
    tn = MOD_COL_TILE
    return pl.pallas_call(
        _mod_kernel,
        out_shape=jax.ShapeDtypeStruct((depth, rows, n), F32),
        grid=(depth, n // tn),
        in_specs=[
            pl.BlockSpec((rows, d), lambda i, j: (0, 0)),
            pl.BlockSpec((None, d, tn), lambda i, j: (i, 0, j)),
            pl.BlockSpec((None, 1, tn), lambda i, j: (i, 0, j)),
        ],
        out_specs=pl.BlockSpec((None, rows, tn), lambda i, j: (i, 0, j)),
        compiler_params=_cparams(("parallel", "parallel")),
        name="adaln_mod",
    )(c_pad, ada_w, ada_b.reshape(depth, 1, n))


def _norm_mod(x, g, sc, sh):
    ms = jnp.mean(x * x, axis=-1, keepdims=True)
    return x * lax.rsqrt(ms + NORM_EPS) * g * (1.0 + sc) + sh


def _qkv_kernel(x_ref, g_ref, sc_ref, sh_ref, w_ref, gq_ref, gk_ref,
                qt_ref, k_ref, vt_ref, wqt_ref, wk_ref, wvt_ref, *, head_dim):
    d, tm = qt_ref.shape

    @pl.when(pl.program_id(0) == 0)
    def _():
        wqt_ref[...] = w_ref[:, :d].T.astype(BF16)
        wk_ref[...] = w_ref[:, d:2 * d].astype(BF16)
        wvt_ref[...] = w_ref[:, 2 * d:].T.astype(BF16)

    cw = V7X_MXU_DIM
    r = lax.broadcasted_iota(jnp.int32, (cw, cw), 0) // head_dim
    c = lax.broadcasted_iota(jnp.int32, (cw, cw), 1) // head_dim
    gmat = jnp.where(r == c, 1.0 / head_dim, 0.0).astype(BF16)
    tg = tm // QKV_SPLIT
    for gi in range(QKV_SPLIT):
        rows = slice(gi * tg, (gi + 1) * tg)
        h = _norm_mod(x_ref[rows, :], g_ref[...], sc_ref[...], sh_ref[...]).astype(BF16)
        y = lax.dot_general(wqt_ref[...], h, _NT, preferred_element_type=F32)
        y3 = y.reshape(d // head_dim, head_dim, tg)
        ss = jnp.mean(y3 * y3, axis=1, keepdims=True)
        qt_ref[:, rows] = ((y3 * lax.rsqrt(ss + NORM_EPS)).reshape(d, tg) * gq_ref[...]).astype(BF16)
        y = jnp.dot(h, wk_ref[...], preferred_element_type=F32)
        ysq = (y * y).astype(BF16)
        for ci in range(d // cw):
            cols = slice(ci * cw, (ci + 1) * cw)
            ss = jnp.dot(ysq[:, cols], gmat, preferred_element_type=F32)
            k_ref[rows, cols] = (y[:, cols] * lax.rsqrt(ss + NORM_EPS) * gk_ref[:, cols]).astype(BF16)
        vt_ref[:, rows] = lax.dot_general(wvt_ref[...], h, _NT, preferred_element_type=F32).astype(BF16)


def _qkv_proj(x2, g, sc, sh, w, gq, gk, seq, head_dim):
    t, d = x2.shape
    tm = ROW_TILE * QKV_SPLIT
    per_b = seq // tm
    vec = lambda i: (i // per_b, 0, 0)
    full = lambda i: (0, 0)
    return pl.pallas_call(
        functools.partial(_qkv_kernel, head_dim=head_dim),
        out_shape=(jax.ShapeDtypeStruct((d, t), BF16),
                   jax.ShapeDtypeStruct((t, d), BF16),
                   jax.ShapeDtypeStruct((d, t), BF16)),
        grid=(t // tm,),
        in_specs=[
            pl.BlockSpec((tm, d), lambda i: (i, 0)),
            pl.BlockSpec((1, d), full),
            pl.BlockSpec((None, 1, d), vec),
            pl.BlockSpec((None, 1, d), vec),
            pl.BlockSpec((d, 3 * d), full, pipeline_mode=pl.Buffered(1)),
            pl.BlockSpec((d, 1), full),
            pl.BlockSpec((1, d), full),
        ],
        out_specs=(pl.BlockSpec((d, tm), lambda i: (0, i)),
                   pl.BlockSpec((tm, d), lambda i: (i, 0)),
                   pl.BlockSpec((d, tm), lambda i: (0, i))),
        scratch_shapes=[pltpu.VMEM((d, d), BF16), pltpu.VMEM((d, d), BF16), pltpu.VMEM((d, d), BF16)],
        compiler_params=_cparams(("arbitrary",)),
        name="qkv_proj",
    )(x2, g, sc, sh, w, gq, gk)


def _softmax_pv(s, vt1, m_ref, acc_ref):
    m_prev = m_ref[...]
    m_new = jnp.maximum(m_prev, jnp.max(s, axis=0, keepdims=True))
    alpha = jnp.exp2(m_prev - m_new)
    p = jnp.exp2(s - m_new).astype(BF16)
    acc_ref[...] = alpha * acc_ref[...] + jnp.dot(vt1, p, preferred_element_type=F32)
    m_ref[...] = m_new


def _split_maps(qt, head_dim):
    row = lax.broadcasted_iota(jnp.int32, qt.shape, 0)
    zero = jnp.zeros_like(qt)
    return jnp.where(row < head_dim, qt, zero), jnp.where(row >= head_dim, qt, zero)


ONES_ROWS = 16


def _init_state(refs):
    for m_ref, acc_ref in refs:
        m_ref[...] = jnp.full(m_ref.shape, NEG_INF, F32)
        acc_ref[...] = jnp.zeros(acc_ref.shape, F32)


def _normalized(acc_ref):
    acc = acc_ref[...]
    return acc[:V7X_LANES] / acc[V7X_LANES:V7X_LANES + 1]


def _toeplitz(x_row, n):
    x = jnp.broadcast_to(x_row, (n, x_row.shape[1]))
    return pltpu.roll(x, 0, 1, stride=1, stride_axis=0)[:, :n]


def _chunk_delta(n):
    kc = lax.broadcasted_iota(jnp.int32, (n, n), 0) // CHUNK
    qc = lax.broadcasted_iota(jnp.int32, (n, n), 1) // CHUNK
    return qc - kc


def _attn_a_kernel(qt_ref, qn_ref, k_ref, vt_ref, x_ref, lam_ref, sub_ref, o_ref, z_ref,
                   tab, fbuf, nbuf, m_st, acc_st, *, n_off, lambda_init):
    tq = ATTN_TILE
    step = pl.program_id(2)
    z_ref[...] = jnp.zeros(z_ref.shape, z_ref.dtype)

    @pl.when(step == 0)
    def _():
        for o in range(n_off):
            t = _toeplitz(x_ref[o], tq) * LOG2E
            if o == 0:
                t = jnp.where(_chunk_delta(tq) >= 0, t, NEG_INF)
            tab[o] = t

    for c in range(2):
        _init_state(((m_st.at[c, 0], acc_st.at[c, 0]), (m_st.at[c, 1], acc_st.at[c, 1])))
    q_maps = [_split_maps(qt_ref[:, c * tq:(c + 1) * tq], A_HEAD_DIM) for c in range(2)]

    def k_tile(j):
        return k_ref[pl.ds(pl.multiple_of(j * tq, tq), tq), :]

    def vt1_tile(j):
        vt = vt_ref[:, pl.ds(pl.multiple_of(j * tq, tq), tq)]
        return jnp.concatenate([vt, jnp.ones((ONES_ROWS, tq), vt.dtype)], axis=0)

    def scores(buf, k, c):
        for m in range(2):
            buf[c, m] = jnp.dot(k, q_maps[c][m], preferred_element_type=F32)

    def consume(buf, vt1, c, o):
        for m in range(2):
            s = buf[c, m]
            _softmax_pv(s if o >= n_off else s + tab[o], vt1, m_st.at[c, m], acc_st.at[c, m])

    def near_block(first_tile, n_tiles, first_scored):
        units = [[c for c in range(2) if n_tiles - 2 + c - s >= 0] for s in range(n_tiles)]
        buf_of = lambda s: fbuf.at[0] if s == 0 else nbuf.at[s - 1]

        for s in range(1 if first_scored else 0, n_tiles):
            k = k_tile(first_tile + s)
            for c in units[s]:
                scores(buf_of(s), k, c)
        for s in range(n_tiles):
            vt1 = vt1_tile(first_tile + s)
            for c in units[s]:
                consume(buf_of(s), vt1, c, n_tiles - 2 + c - s)

    first_general = (n_off - 1) // 2
    for i0 in range(first_general):
        @pl.when(step == i0)
        def _():
            near_block(0, 2 * i0 + 2, False)

    def far_step(j, parity):
        k = k_tile(j + 1)
        for c in range(2):
            scores(fbuf.at[1 - parity], k, c)
        vt1 = vt1_tile(j)
        for c in range(2):
            consume(fbuf.at[parity], vt1, c, n_off)

    n_far = jnp.maximum(2 * step - (n_off - 1), 0)
    rem = n_far % FAR_UNROLL

    @pl.when(rem >= 2)
    def _():
        far_step(0, 0)
        far_step(1, 1)

    def far_trip(t, carry):
        for u in range(FAR_UNROLL):
            far_step(FAR_UNROLL * t + rem + u, u % 2)
        return carry
    lax.fori_loop(0, n_far // FAR_UNROLL, far_trip, 0)

    @pl.when(step >= first_general)
    def _():
        near_block(n_far, n_off + 1, True)

    def epilogue():
        lam = lam_ref[...]
        lam_full = (jnp.exp(jnp.sum(lam[0:1] * lam[1:2], axis=-1, keepdims=True))
                    - jnp.exp(jnp.sum(lam[2:3] * lam[3:4], axis=-1, keepdims=True)) + lambda_init)
        for c in range(2):
            a = _normalized(acc_st.at[c, 0]) - lam_full * _normalized(acc_st.at[c, 1])
            ms = jnp.mean(a * a, axis=0, keepdims=True)
            an = a * lax.rsqrt(ms + NORM_EPS)
            o_ref[c * tq:(c + 1) * tq, :] = (an.T * (sub_ref[...] * (1.0 - lambda_init))).astype(BF16)

    last = pl.num_programs(2) - 1

    @pl.when(step < last)
    def _():
        k = k_tile(0)
        for c in range(2):
            qa, qb = _split_maps(qn_ref[:, c * tq:(c + 1) * tq], A_HEAD_DIM)
            fbuf[0, c, 0] = jnp.dot(k, qa, preferred_element_type=F32)
            fbuf[0, c, 1] = jnp.dot(k, qb, preferred_element_type=F32)
        epilogue()

    @pl.when(step == last)
    def _():
        epilogue()


def _attn_b_kernel(qt_ref, k_ref, vt_ref, x_ref, o_ref, tab, sbuf):
    tq = BAND_TILE
    n_blk = qt_ref.shape[1] // tq
    n_off = tab.shape[0] // 2
    step = pl.program_id(2)

    @pl.when(step == 0)
    def _():
        for m in range(2):
            for o in range(n_off):
                d = _chunk_delta(tq) + o * (tq // CHUNK)
                t = _toeplitz(x_ref[n_off * m + o], tq) * LOG2E
                tab[n_off * m + o] = jnp.where((d >= 0) & (d <= LEFT_CHUNKS), t, NEG_INF)

    q_maps = [_split_maps(qt_ref[:, c * tq:(c + 1) * tq], B_HEAD_DIM) for c in range(n_blk)]

    def run(first_step):
        pairs = [(c, o) for c in range(n_blk) for o in range(n_off - 1, -1, -1)
                 if not first_step or c - o >= 0]
        k_tiles, vt_tiles = {}, {}
        for c, o in pairs:
            if c - o not in k_tiles:
                ks = pl.multiple_of((n_blk * step + c - o) * tq, tq)
                k_tiles[c - o] = k_ref[pl.ds(ks, tq), :]
                vt = vt_ref[:, pl.ds(ks, tq)]
                vt_tiles[c - o] = jnp.concatenate([vt, jnp.ones((ONES_ROWS, tq), vt.dtype)], axis=0)
        def tile_scores(c):
            for o in [o for cc, o in pairs if cc == c]:
                for m in range(2):
                    sbuf[n_off * c + o, m] = jnp.dot(k_tiles[c - o], q_maps[c][m], preferred_element_type=F32)

        for c in range(min(BAND_AHEAD, n_blk)):
            tile_scores(c)
        for c in range(n_blk):
            if c + BAND_AHEAD < n_blk:
                tile_scores(c + BAND_AHEAD)
            outs = []
            for m in range(2):
                offs = [o for cc, o in pairs if cc == c]
                ss = [sbuf[n_off * c + o, m] + tab[n_off * m + o] for o in offs]
                mx = functools.reduce(jnp.maximum, [jnp.max(s, axis=0, keepdims=True) for s in ss])
                acc = sum(jnp.dot(vt_tiles[c - o], jnp.exp2(s - mx).astype(BF16), preferred_element_type=F32)
                          for o, s in zip(offs, ss))
                outs.append(acc[:V7X_LANES] / acc[V7X_LANES:V7X_LANES + 1])
            row = lax.broadcasted_iota(jnp.int32, outs[0].shape, 0)
            o_ref[c * tq:(c + 1) * tq, :] = jnp.where(row < B_HEAD_DIM, outs[0], outs[1]).T.astype(BF16)

    @pl.when(step == 0)
    def _():
        run(True)

    @pl.when(step >= 1)
    def _():
        run(False)


def _attn_specs(batch, seq, nq, tq):
    return dict(
        q=pl.BlockSpec((V7X_LANES, tq), lambda b, h, i: (h, b * nq + i)),
        k=pl.BlockSpec((seq, V7X_LANES), lambda b, h, i: (b, h)),
        v=pl.BlockSpec((V7X_LANES, seq), lambda b, h, i: (h, b)),
        o=pl.BlockSpec((tq, V7X_LANES), lambda b, h, i: (b * nq + i, h)),
    )


def _attention_a(qt, k, vt, xvec, lam, sub_g, batch, seq, lambda_init, zero_rows):
    d, t = qt.shape
    tq = ATTN_TILE
    n_off = xvec.shape[1]
    assert n_off % 2 == 1 and seq % (2 * tq) == 0
    nq = seq // (2 * tq)
    n_h = d // V7X_LANES
    zr = zero_rows // (batch * n_h * nq)
    assert zr * batch * n_h * nq == zero_rows and zr % MOE_CHUNK == 0
    sp = _attn_specs(batch, seq, nq, 2 * tq)
    rows = V7X_LANES + ONES_ROWS
    return pl.pallas_call(
        functools.partial(_attn_a_kernel, n_off=n_off, lambda_init=lambda_init),
        out_shape=(jax.ShapeDtypeStruct((t, d), BF16), jax.ShapeDtypeStruct((zero_rows, d), BF16)),
        grid=(batch, n_h, nq),
        in_specs=[sp["q"],
                  pl.BlockSpec((V7X_LANES, 2 * tq), lambda b, h, i: (h, b * nq + jnp.minimum(i + 1, nq - 1))),
                  sp["k"], sp["v"],
                  pl.BlockSpec((None,) + xvec.shape[1:], lambda b, h, i: (h, 0, 0, 0)),
                  pl.BlockSpec(lam.shape, lambda b, h, i: (0, 0)),
                  pl.BlockSpec(sub_g.shape, lambda b, h, i: (0, 0))],
        out_specs=(sp["o"], pl.BlockSpec((zr, d), lambda b, h, i: ((b * n_h + h) * nq + i, 0))),
        scratch_shapes=[pltpu.VMEM((n_off, tq, tq), F32), pltpu.VMEM((2, 2, 2, tq, tq), F32),
                        pltpu.VMEM((n_off, 2, 2, tq, tq), F32),
                        pltpu.VMEM((2, 2, 1, tq), F32), pltpu.VMEM((2, 2, rows, tq), F32)],
        compiler_params=_cparams(("parallel", "parallel", "arbitrary")),
        name="diff_attention",
    )(qt, qt, k, vt, xvec, lam, sub_g)


def _attention_b(qt, k, vt, xvec, batch, seq):
    d, t = qt.shape
    tq = BAND_TILE
    g = BAND_QBLOCKS
    n_off = xvec.shape[1] // 2
    assert g >= n_off - 1 and seq % (g * tq) == 0
    nq = seq // (g * tq)
    sp = _attn_specs(batch, seq, nq, g * tq)
    return pl.pallas_call(
        _attn_b_kernel,
        out_shape=jax.ShapeDtypeStruct((t, d), BF16),
        grid=(batch, d // V7X_LANES, nq),
        in_specs=[sp["q"], sp["k"], sp["v"],
                  pl.BlockSpec((None,) + xvec.shape[1:], lambda b, h, i: (h, 0, 0, 0))],
        out_specs=sp["o"],
        scratch_shapes=[pltpu.VMEM((2 * n_off, tq, tq), F32), pltpu.VMEM((n_off * g, 2, tq, tq), F32)],
        compiler_params=_cparams(("parallel", "parallel", "arbitrary")),
        name="chunk_attention",
    )(qt, k, vt, xvec)


def _t5_bucket(rel):
    nb = T5_BUCKETS // 2
    ret = jnp.where(rel > 0, nb, 0)
    n = jnp.abs(rel)
    max_exact = nb // 2
    nf = jnp.maximum(n, 1).astype(F32)
    large = max_exact + (jnp.log(nf / max_exact) / math.log(T5_MAX_DIST / max_exact)
                         * (nb - max_exact)).astype(jnp.int32)
    large = jnp.minimum(large, nb - 1)
    return ret + jnp.where(n < max_exact, n, large)


def _t5_const_distance():
    nb = T5_BUCKETS // 2
    max_exact = nb // 2
    n = np.arange(max_exact, 4 * T5_MAX_DIST, dtype=np.float64)
    large = max_exact + np.floor(np.log(n / max_exact) / math.log(T5_MAX_DIST / max_exact) * (nb - max_exact))
    below = np.nonzero(large < nb - 1)[0]
    return int(n[below[-1]]) + 2


def _tile_rel(tile, n_off):
    i = jnp.arange(2 * tile, dtype=jnp.int32)
    rel = jnp.where(i < tile, -i, 2 * tile - i)
    return rel[None, :] - tile * jnp.arange(n_off, dtype=jnp.int32)[:, None]


def _t5_vectors(t5_bias, tile):
    n_off = 1
    while (n_off - 1) * tile + 1 < _t5_const_distance():
        n_off += 1
    tb = t5_bias.astype(F32)
    vals = tb[_t5_bucket(_tile_rel(tile, n_off))] - tb[T5_BUCKETS // 2 - 1]
    return vals.transpose(2, 0, 1)[:, :, None, :]


def _band_vectors(rel_bias, tile):
    n_off = LEFT_CHUNKS * CHUNK // tile + 1
    idx = jnp.clip(_tile_rel(tile, n_off), -MAX_REL, MAX_REL) + MAX_REL
    vals = rel_bias.astype(F32)[:, idx]
    return vals.reshape(rel_bias.shape[0] // 2, 2 * n_off, 1, 2 * tile)


def _top2_sum4(r0, r1, r2, r3):
    a, b = jnp.maximum(r0, r1), jnp.minimum(r0, r1)
    c, d = jnp.maximum(r2, r3), jnp.minimum(r2, r3)
    return jnp.maximum(a, c) + jnp.maximum(jnp.minimum(a, c), jnp.maximum(b, d))


def _route(logits, rbias, before_ref):
    n_e, n = logits.shape
    scores = 1.0 / (1.0 + jnp.exp(-logits))
    sel = scores + rbias
    row = lax.broadcasted_iota(jnp.int32, sel.shape, 0)
    best = None
    for g in range(N_GROUPS):
        rows = [sel[g * E_PER_GROUP + i: g * E_PER_GROUP + i + 1, :] for i in range(E_PER_GROUP)]
        gs = _top2_sum4(*rows)
        if best is None:
            best, gidx = gs, jnp.zeros(gs.shape, jnp.int32)
        else:
            gidx = jnp.where(gs > best, g, gidx)
            best = jnp.maximum(best, gs)
    masked = jnp.where(row // E_PER_GROUP == gidx, sel, NEG_INF)
    m1 = jnp.max(masked, axis=0, keepdims=True)
    i1 = jnp.min(jnp.where(masked == m1, row, n_e), axis=0, keepdims=True)
    masked2 = jnp.where(row == i1, -3.0e38, masked)
    m2 = jnp.max(masked2, axis=0, keepdims=True)
    i2 = jnp.min(jnp.where(masked2 == m2, row, n_e), axis=0, keepdims=True)
    w1 = jnp.sum(jnp.where(row == i1, scores, 0.0), axis=0, keepdims=True)
    w2 = jnp.sum(jnp.where(row == i2, scores, 0.0), axis=0, keepdims=True)
    den = w1 + w2
    member = jnp.where((row == i1) | (row == i2), 1.0, 0.0).astype(BF16)
    before = before_ref[...]
    pos = jnp.concatenate(
        [jnp.dot(member[:, i:i + MOE_TILE], before, preferred_element_type=F32)
         for i in range(0, n, MOE_TILE)], axis=1)
    p1 = jnp.sum(jnp.where(row == i1, pos, 0.0), axis=0, keepdims=True)
    p2 = jnp.sum(jnp.where(row == i2, pos, 0.0), axis=0, keepdims=True)
    out_row = lax.broadcasted_iota(jnp.int32, (8, n), 0)
    out = jnp.zeros((8, n), F32)
    for r, val in enumerate((i1.astype(F32), i2.astype(F32), p1, p2, w1 / den, w2 / den)):
        out = jnp.where(out_row == r, val, out)
    return out


def _oproj_kernel(o_ref, wo_ref, x_ref, g1_ref, gn_ref, sc_ref, sh_ref, rwt_ref, rb_ref,
                  xo_ref, h_ref, route_ref, before_ref):
    @pl.when(pl.program_id(0) == 0)
    def _():
        t_from = lax.broadcasted_iota(jnp.int32, before_ref.shape, 0)
        t_to = lax.broadcasted_iota(jnp.int32, before_ref.shape, 1)
        before_ref[...] = jnp.where(t_from < t_to, 1.0, 0.0).astype(BF16)

    n_e = rb_ref.shape[0]
    rw = rwt_ref[...]
    rows = o_ref.shape[0] // OPROJ_SPLIT
    groups = [slice(r * rows, (r + 1) * rows) for r in range(OPROJ_SPLIT)]
    ys = [jnp.dot(o_ref[g, :], wo_ref[...], preferred_element_type=F32) for g in groups]
    logits = []
    for g, y in zip(groups, ys):
        xn = x_ref[g, :] + g1_ref[...] * y
        xo_ref[g, :] = xn
        h = _norm_mod(xn, gn_ref[...], sc_ref[...], sh_ref[...])
        h_hi = h.astype(BF16)
        h_ref[g, :] = h_hi
        h_lo = (h - h_hi.astype(F32)).astype(BF16)
        main = lax.dot_general(rw, h_hi, _NT, preferred_element_type=F32)
        corr = lax.dot_general(rw[:n_e], h_lo, _NT, preferred_element_type=F32)
        logits.append(main[:n_e] + main[n_e:] + corr)
    route_ref[...] = _route(jnp.concatenate(logits, axis=1), rb_ref[...], before_ref)


def _out_proj(o, wo_bf16, x2, g1, gn, sc, sh, rwt, rb, seq):
    t, d = x2.shape
    tm = OPROJ_ROW_TILE
    per_b = seq // tm
    n_e = rb.shape[0]
    vec = lambda i: (i // per_b, 0, 0)
    return pl.pallas_call(
        _oproj_kernel,
        out_shape=(jax.ShapeDtypeStruct((t, d), F32),
                   jax.ShapeDtypeStruct((t, d), BF16),
                   jax.ShapeDtypeStruct((8, t), F32)),
        grid=(t // tm,),
        in_specs=[
            pl.BlockSpec((tm, d), lambda i: (i, 0)),
            pl.BlockSpec((d, d), lambda i: (0, 0)),
            pl.BlockSpec((tm, d), lambda i: (i, 0)),
            pl.BlockSpec((None, 1, d), vec),
            pl.BlockSpec((1, d), lambda i: (0, 0)),
            pl.BlockSpec((None, 1, d), vec),
            pl.BlockSpec((None, 1, d), vec),
            pl.BlockSpec((2 * n_e, d), lambda i: (0, 0)),
            pl.BlockSpec((n_e, 1), lambda i: (0, 0)),
        ],
        out_specs=(pl.BlockSpec((tm, d), lambda i: (i, 0)),
                   pl.BlockSpec((tm, d), lambda i: (i, 0)),
                   pl.BlockSpec((8, tm), lambda i: (0, i))),
        scratch_shapes=[pltpu.VMEM((MOE_TILE, MOE_TILE), BF16)],
        compiler_params=_cparams(("arbitrary",)),
        name="out_proj_route",
    )(o, wo_bf16, x2, g1, gn, sc, sh, rwt, rb)


def _moe_plan(route, t, tm, n_e):
    n_tiles = t // tm
    ch, ft = MOE_CHUNK, MOE_FFN_TILE
    loc_rows, main_rows, n_sorted = _moe_rows(t, tm, n_e)
    ids = jnp.arange(n_e, dtype=jnp.int32)
    e = route[0:2].astype(jnp.int32)
    pos = route[2:4].astype(jnp.int32)
    oh = (e[:, :, None] == ids).astype(jnp.int32)
    cnt = oh.sum(0).reshape(n_tiles, tm, n_e).sum(1)
    seg = (cnt + ch - 1) // ch * ch
    loc = jnp.cumsum(seg, axis=1) - seg
    tot = seg.sum(0)
    totp = (tot + ft - 1) // ft * ft
    base = jnp.cumsum(totp) - totp
    gdest = base[None, :] + jnp.cumsum(seg, axis=0) - seg
    col = (oh * jnp.repeat(loc, tm, axis=0)[None]).sum(-1) + pos

    row0 = jnp.arange(loc_rows // ch, dtype=jnp.int32) * ch
    ej = (row0[None, :, None] >= (loc + seg)[:, None, :]).sum(-1)
    ohj = (jnp.minimum(ej, n_e - 1)[..., None] == ids).astype(jnp.int32)
    dst = (ohj * (gdest - loc)[:, None, :]).sum(-1) + row0[None, :]
    dump = main_rows + (jnp.arange(n_tiles, dtype=jnp.int32) % 2)[:, None] * loc_rows + row0[None, :]
    dst = jnp.where(ej < n_e, dst, dump) // ch

    r0 = jnp.arange(n_sorted // ft, dtype=jnp.int32) * ft
    ends = base + totp
    te = jnp.minimum((r0[:, None] >= ends[None, :]).sum(-1), n_e - 1)
    tvalid = (r0 < ends[-1]).astype(jnp.int32)
    tfirst = tvalid * (r0 == (((te[:, None] == ids) * base[None, :]).sum(-1))).astype(jnp.int32)
    return col, dst.astype(jnp.int32), te.astype(jnp.int32), tvalid, tfirst


def _moe_rows(t, tm, n_e):
    ch, ft = MOE_CHUNK, MOE_FFN_TILE
    loc_rows = 2 * tm + n_e * ch
    main = 2 * t + (t // tm) * n_e * (ch - 1) + n_e * (ft - 1)
    main = (main + ft - 1) // ft * ft
    return loc_rows, main, (main + 2 * loc_rows + ft - 1) // ft * ft


def _chunk_copies(dst_ref, tile, local_ref, sorted_ref, sem, to_sorted):
    ch = MOE_CHUNK
    copies = []
    for j in range(local_ref.shape[0] // ch):
        far = sorted_ref.at[pl.ds(pl.multiple_of(dst_ref[tile, j] * ch, ch), ch)]
        near = local_ref.at[pl.ds(j * ch, ch)]
        copies.append(pltpu.make_async_copy(near, far, sem) if to_sorted
                      else pltpu.make_async_copy(far, near, sem))
    return copies


def _start_alternating(copies):
    for j, cp in enumerate(copies):
        cp.start(priority=j % DMA_PRIORITIES)


def _dispatch_kernel(dst_ref, col_ref, h_ref, init_ref, xs_ref, comp, sem):
    del init_ref
    tile = pl.program_id(0)
    last = pl.num_programs(0) - 1
    slot = tile % 2

    def wait_slot(s):
        pltpu.make_async_copy(comp.at[s], xs_ref.at[pl.ds(0, comp.shape[1])], sem.at[s]).wait()

    @pl.when(tile >= 2)
    def _():
        wait_slot(slot)

    col = col_ref[...]
    srow = lax.broadcasted_iota(jnp.int32, (comp.shape[1], col.shape[1]), 0)
    sel = jnp.where((srow == col[0:1]) | (srow == col[1:2]), 1.0, 0.0).astype(BF16)
    comp[slot] = jnp.dot(sel, h_ref[...], preferred_element_type=F32).astype(BF16)
    _start_alternating(_chunk_copies(dst_ref, tile, comp.at[slot], xs_ref, sem.at[slot], True))

    @pl.when(tile == last)
    def _():
        wait_slot(slot)

    @pl.when((tile == last) & (tile >= 1))
    def _():
        wait_slot(1 - slot)


def _dispatch(dst, col, h, n_e, init):
    t, d = h.shape
    tm = MOE_TILE
    loc_rows, _, n_sorted = _moe_rows(t, tm, n_e)
    assert init.shape == (n_sorted, d) and init.dtype == BF16
    return pl.pallas_call(
        _dispatch_kernel,
        out_shape=jax.ShapeDtypeStruct((n_sorted, d), BF16),
        grid_spec=pltpu.PrefetchScalarGridSpec(
            num_scalar_prefetch=1,
            grid=(t // tm,),
            in_specs=[pl.BlockSpec((2, tm), lambda i, dst: (0, i)),
                      pl.BlockSpec((tm, d), lambda i, dst: (i, 0)),
                      pl.BlockSpec(memory_space=pl.ANY)],
            out_specs=pl.BlockSpec(memory_space=pl.ANY),
            scratch_shapes=[pltpu.VMEM((2, loc_rows, d), BF16), pltpu.SemaphoreType.DMA((2,))],
        ),
        input_output_aliases={3: 0},
        compiler_params=_cparams(("arbitrary",)),
        name="moe_dispatch",
    )(dst, col, h, init)


def _expert_kernel(te_ref, tv_ref, tf_ref, xs_ref, wg_ref, wu_ref, wd_ref, ys_ref, wg_b, wu_b, wd_b):
    del te_ref
    s = pl.program_id(0)

    @pl.when(tf_ref[s] == 1)
    def _():
        wg_b[...] = wg_ref[...].astype(BF16)
        wu_b[...] = wu_ref[...].astype(BF16)
        wd_b[...] = wd_ref[...].astype(BF16)

    @pl.when(tv_ref[s] == 1)
    def _():
        rows = ys_ref.shape[0] // MOE_FFN_SPLIT
        hs = []
        for r in range(MOE_FFN_SPLIT):
            xs = xs_ref[r * rows:(r + 1) * rows, :]
            hg = jnp.dot(xs, wg_b[...], preferred_element_type=F32)
            hu = jnp.dot(xs, wu_b[...], preferred_element_type=F32)
            hs.append((hg, hu))
        for r, (hg, hu) in enumerate(hs):
            he = hg * (1.0 / (1.0 + jnp.exp(-hg))) * hu
            ys_ref[r * rows:(r + 1) * rows, :] = jnp.dot(
                he.astype(BF16), wd_b[...], preferred_element_type=F32).astype(BF16)

    @pl.when(tv_ref[s] == 0)
    def _():
        ys_ref[...] = jnp.zeros(ys_ref.shape, ys_ref.dtype)


def _experts(te, tvalid, tfirst, xs, wg, wu, wd, layer):
    n_sorted, d = xs.shape
    f = wg.shape[3]
    ft = MOE_FFN_TILE
    wspec = lambda r, c: pl.BlockSpec((None, None, r, c), lambda s, te, tv, tf: (layer, te[s], 0, 0))
    return pl.pallas_call(
        _expert_kernel,
        out_shape=jax.ShapeDtypeStruct((n_sorted, d), BF16),
        grid_spec=pltpu.PrefetchScalarGridSpec(
            num_scalar_prefetch=3,
            grid=(n_sorted // ft,),
            in_specs=[pl.BlockSpec((ft, d), lambda s, te, tv, tf: (s, 0)),
                      wspec(d, f), wspec(d, f), wspec(f, d)],
            out_specs=pl.BlockSpec((ft, d), lambda s, te, tv, tf: (s, 0)),
            scratch_shapes=[pltpu.VMEM((d, f), BF16), pltpu.VMEM((d, f), BF16), pltpu.VMEM((f, d), BF16)],
        ),
        compiler_params=_cparams(("arbitrary",)),
        name="moe_experts",
    )(te, tvalid, tfirst, xs, wg, wu, wd)


def _combine_kernel(dst_ref, w_ref, ys_ref, x_ref, g2_ref, o_ref, comp, sem):
    tile = pl.program_id(0)
    slot = tile % 2

    def fetch(t, s):
        _start_alternating(_chunk_copies(dst_ref, t, comp.at[s], ys_ref, sem.at[s], False))

    def wait_slot(s):
        pltpu.make_async_copy(ys_ref.at[pl.ds(0, comp.shape[1])], comp.at[s], sem.at[s]).wait()

    last = pl.num_programs(0) - 1

    @pl.when(tile == 0)
    def _():
        fetch(tile, slot)

    fetch(jnp.minimum(tile + 1, last), 1 - slot)

    w = w_ref[...].T
    scol = lax.broadcasted_iota(jnp.int32, (w.shape[0], comp.shape[1]), 1).astype(F32)
    selw = (jnp.where(scol == w[:, 6:7], w[:, 4:5], 0.0)
            + jnp.where(scol == w[:, 7:8], w[:, 5:6], 0.0)).astype(BF16)
    wait_slot(slot)
    acc = jnp.dot(selw, comp[slot], preferred_element_type=F32)
    o_ref[...] = x_ref[...] + g2_ref[...] * acc

    @pl.when(tile == last)
    def _():
        wait_slot(1 - slot)


def _combine(dst, route_col, ys, x2, g2, seq, n_e):
    t, d = x2.shape
    tm = MOE_TILE
    per_b = seq // tm
    loc_rows = _moe_rows(t, tm, n_e)[0]
    return pl.pallas_call(
        _combine_kernel,
        out_shape=jax.ShapeDtypeStruct((t, d), F32),
        grid_spec=pltpu.PrefetchScalarGridSpec(
            num_scalar_prefetch=1,
            grid=(t // tm,),
            in_specs=[pl.BlockSpec((8, tm), lambda i, dst: (0, i)),
                      pl.BlockSpec(memory_space=pl.ANY),
                      pl.BlockSpec((tm, d), lambda i, dst: (i, 0)),
                      pl.BlockSpec((None, 1, d), lambda i, dst: (i // per_b, 0, 0))],
            out_specs=pl.BlockSpec((tm, d), lambda i, dst: (i, 0)),
            scratch_shapes=[pltpu.VMEM((2, loc_rows, d), BF16), pltpu.SemaphoreType.DMA((2,))],
        ),
        compiler_params=_cparams(("arbitrary",)),
        name="moe_combine",
    )(dst, route_col, ys, x2, g2)


def _moe(h, route, wg, wu, wd, layer, x2, g2, seq, sorted_init):
    t = x2.shape[0]
    n_e = wg.shape[1]
    col, dst, te, tvalid, tfirst = _moe_plan(route, t, MOE_TILE, n_e)
    xs = _dispatch(dst, col, h, n_e, sorted_init)
    ys = _experts(te, tvalid, tfirst, xs, wg, wu, wd, layer)
    route_col = jnp.concatenate([route[:6], col.astype(F32)], axis=0)
    return _combine(dst, route_col, ys, x2, g2, seq, n_e), ys


def kernel(x, c, ada_w, ada_b, norm_mix_g, norm_ffn_g, t5_bias, a_w_qkv, a_q_gain, a_k_gain, a_lambda, a_subln_g, a_w_o, b_w_qkv, b_q_gain, b_k_gain, b_rel_bias, b_w_o, router_w, router_bias, moe_w_gate, moe_w_up, moe_w_down):
    batch, seq, d = x.shape
    depth = ada_w.shape[0]
    assert seq % ATTN_TILE == 0 and seq % ROW_TILE == 0 and d == A_HEADS * 2 * A_HEAD_DIM
    assert d == B_HEADS * B_HEAD_DIM and A_HEAD_DIM == B_HEAD_DIM

    c_pad = jnp.zeros((8, d), F32).at[:batch].set(c.astype(F32))
    mod = _modulation(c_pad, ada_w.astype(F32), ada_b.astype(F32))[:, :batch]
    mod = mod.reshape(depth, batch, 6, 1, d)

    rwt = router_w.astype(F32).T
    rwt_hi = lax.bitcast_convert_type(
        lax.bitcast_convert_type(rwt, jnp.uint32) & jnp.uint32(0xFFFF0000), F32)
    rwt = jnp.concatenate([rwt_hi, rwt - rwt_hi], axis=0).astype(BF16)
    rb = router_bias.astype(F32).reshape(-1, 1)

    x2 = x.astype(F32).reshape(batch * seq, d)
    sorted_buf = None
    for i in range(depth):
        sh1, sc1, g1, sh2, sc2, g2 = [mod[i, :, k] for k in range(6)]
        j = i // 2
        if i % 2 == 0:
            w_qkv, qg, kg, w_o = a_w_qkv[j], a_q_gain[j], a_k_gain[j], a_w_o[j]
        else:
            w_qkv, qg, kg, w_o = b_w_qkv[j], b_q_gain[j], b_k_gain[j], b_w_o[j]
        n_rep = d // qg.shape[0]
        gq = (jnp.tile(qg.astype(F32), n_rep) * (A_HEAD_DIM ** -0.5 * LOG2E)).reshape(d, 1)
        gk = jnp.tile(kg.astype(F32), n_rep).reshape(1, d)
        qt, k, vt = _qkv_proj(x2, norm_mix_g[i].reshape(1, d), sc1, sh1, w_qkv.astype(F32),
                              gq, gk, seq, A_HEAD_DIM)
        if i % 2 == 0:
            lambda_init = 0.8 - 0.6 * math.exp(-0.3 * i)
            n_sorted = _moe_rows(batch * seq, MOE_TILE, moe_w_gate.shape[1])[2]
            o, zeros_buf = _attention_a(qt, k, vt, _t5_vectors(t5_bias, ATTN_TILE), a_lambda[j].astype(F32),
                                        a_subln_g[j].reshape(1, -1), batch, seq, lambda_init, n_sorted)
            if sorted_buf is None:
                sorted_buf = zeros_buf
        else:
            o = _attention_b(qt, k, vt, _band_vectors(b_rel_bias[j], BAND_TILE), batch, seq)
        x2, h, route = _out_proj(o, w_o.astype(BF16), x2, g1, norm_ffn_g[i].reshape(1, d),
                                 sc2, sh2, rwt, rb, seq)
        x2, sorted_buf = _moe(h, route, moe_w_gate, moe_w_up, moe_w_down, i, x2, g2, seq, sorted_buf)
    return x2.reshape(batch, seq, d)
```

```python
import functools
import math

import numpy as np
import jax
import jax.numpy as jnp
from jax import lax
from jax.experimental import pallas as pl
from jax.experimental.pallas import tpu as pltpu

F32 = jnp.float32
BF16 = jnp.bfloat16

CHUNK = 64
A_HEADS = 8
A_HEAD_DIM = 64
T5_BUCKETS = 32
T5_MAX_DIST = 1024
B_HEADS = 16
B_HEAD_DIM = 64
LEFT_CHUNKS = 8
MAX_REL = 256
N_EXPERTS = 16
N_GROUPS = 4
E_PER_GROUP = N_EXPERTS // N_GROUPS
NORM_EPS = 1e-6
NEG_INF = -1e30
LOG2E = math.log2(math.e)

V7X_LANES = 128
V7X_MXU_DIM = 256

ATTN_TILE = 512
MOD_COL_TILE = 1536
FAR_UNROLL = 4
BAND_TILE = 256
BAND_QBLOCKS = 16
BAND_AHEAD = 2
ROW_TILE = 512
QKV_SPLIT = 2
MOE_TILE = 512
OPROJ_ROW_TILE = 1024
OPROJ_SPLIT = 4
MOE_CHUNK = 16
MOE_FFN_TILE = 512
MOE_FFN_SPLIT = 2
VMEM_LIMIT = 56 * 1024 * 1024

_NT = (((1,), (1,)), ((), ()))


def _cparams(sem):
    return pltpu.CompilerParams(dimension_semantics=sem, vmem_limit_bytes=VMEM_LIMIT)


def _mod_kernel(c_ref, w_ref, b_ref, o_ref):
    c = c_ref[...]
    s = c * (1.0 / (1.0 + jnp.exp(-c)))
    w = w_ref[...]
    s_hi, w_hi = s.astype(BF16), w.astype(BF16)
    s_lo = (s - s_hi.astype(F32)).astype(BF16)
    w_lo = (w - w_hi.astype(F32)).astype(BF16)
    dot = functools.partial(jnp.dot, preferred_element_type=F32)
    o_ref[...] = dot(s_hi, w_hi) + dot(s_hi, w_lo) + dot(s_lo, w_hi) + b_ref[...]


def _modulation(c_pad, ada_w, ada_b):
    depth, d, n = ada_w.shape
    rows = c_pad.shape[0]
    tn = MOD_COL_TILE
    return pl.pallas_call(
        _mod_kernel,
        out_shape=jax.ShapeDtypeStruct((depth, rows, n), F32),
        grid=(depth, n // tn),
        in_specs=[
            pl.BlockSpec((rows, d), lambda i, j: (0, 0)),
            pl.BlockSpec((None, d, tn), lambda i, j: (i, 0, j)),
            pl.BlockSpec((None, 1, tn), lambda i, j: (i, 0, j)),
        ],
        out_specs=pl.BlockSpec((None, rows, tn), lambda i, j: (i, 0, j)),
        compiler_params=_cparams(("parallel", "parallel")),
        name="adaln_mod",
    )(c_pad, ada_w, ada_b.reshape(depth, 1, n))


def _norm_mod(x, g, sc, sh):
    ms = jnp.mean(x * x, axis=-1, keepdims=True)
    return x * lax.rsqrt(ms + NORM_EPS) * g * (1.0 + sc) + sh


def _qkv_kernel(x_ref, g_ref, sc_ref, sh_ref, w_ref, gq_ref, gk_ref,
                qt_ref, k_ref, vt_ref, wqt_ref, wk_ref, wvt_ref, *, head_dim):
    d, tm = qt_ref.shape

    @pl.when(pl.program_id(0) == 0)
    def _():
        wqt_ref[...] = w_ref[:, :d].T.astype(BF16)
        wk_ref[...] = w_ref[:, d:2 * d].astype(BF16)
        wvt_ref[...] = w_ref[:, 2 * d:].T.astype(BF16)

    cw = V7X_MXU_DIM
    r = lax.broadcasted_iota(jnp.int32, (cw, cw), 0) // head_dim
    c = lax.broadcasted_iota(jnp.int32, (cw, cw), 1) // head_dim
    gmat = jnp.where(r == c, 1.0 / head_dim, 0.0).astype(BF16)
    tg = tm // QKV_SPLIT
    for gi in range(QKV_SPLIT):
        rows = slice(gi * tg, (gi + 1) * tg)
        h = _norm_mod(x_ref[rows, :], g_ref[...], sc_ref[...], sh_ref[...]).astype(BF16)
        y = lax.dot_general(wqt_ref[...], h, _NT, preferred_element_type=F32)
        y3 = y.reshape(d // head_dim, head_dim, tg)
        ss = jnp.mean(y3 * y3, axis=1, keepdims=True)
        qt_ref[:, rows] = ((y3 * lax.rsqrt(ss + NORM_EPS)).reshape(d, tg) * gq_ref[...]).astype(BF16)
        y = jnp.dot(h, wk_ref[...], preferred_element_type=F32)
        ysq = (y * y).astype(BF16)
        for ci in range(d // cw):
            cols = slice(ci * cw, (ci + 1) * cw)
            ss = jnp.dot(ysq[:, cols], gmat, preferred_element_type=F32)
            k_ref[rows, cols] = (y[:, cols] * lax.rsqrt(ss + NORM_EPS) * gk_ref[:, cols]).astype(BF16)
        vt_ref[:, rows] = lax.dot_general(wvt_ref[...], h, _NT, preferred_element_type=F32).astype(BF16)


def _qkv_proj(x2, g, sc, sh, w, gq, gk, seq, head_dim):
    t, d = x2.shape
    tm = ROW_TILE * QKV_SPLIT
    per_b = seq // tm
    vec = lambda i: (i // per_b, 0, 0)
    full = lambda i: (0, 0)
    return pl.pallas_call(
        functools.partial(_qkv_kernel, head_dim=head_dim),
        out_shape=(jax.ShapeDtypeStruct((d, t), BF16),
                   jax.ShapeDtypeStruct((t, d), BF16),
                   jax.ShapeDtypeStruct((d, t), BF16)),
        grid=(t // tm,),
        in_specs=[
            pl.BlockSpec((tm, d), lambda i: (i, 0)),
            pl.BlockSpec((1, d), full),
            pl.BlockSpec((None, 1, d), vec),
            pl.BlockSpec((None, 1, d), vec),
            pl.BlockSpec((d, 3 * d), full, pipeline_mode=pl.Buffered(1)),
            pl.BlockSpec((d, 1), full),
            pl.BlockSpec((1, d), full),
        ],
        out_specs=(pl.BlockSpec((d, tm), lambda i: (0, i)),
                   pl.BlockSpec((tm, d), lambda i: (i, 0)),
                   pl.BlockSpec((d, tm), lambda i: (0, i))),
        scratch_shapes=[pltpu.VMEM((d, d), BF16), pltpu.VMEM((d, d), BF16), pltpu.VMEM((d, d), BF16)],
        compiler_params=_cparams(("arbitrary",)),
        name="qkv_proj",
    )(x2, g, sc, sh, w, gq, gk)


def _softmax_pv(s, vt1, m_ref, acc_ref):
    m_prev = m_ref[...]
    m_new = jnp.maximum(m_prev, jnp.max(s, axis=0, keepdims=True))
    alpha = jnp.exp2(m_prev - m_new)
    p = jnp.exp2(s - m_new).astype(BF16)
    acc_ref[...] = alpha * acc_ref[...] + jnp.dot(vt1, p, preferred_element_type=F32)
    m_ref[...] = m_new


def _split_maps(qt, head_dim):
    row = lax.broadcasted_iota(jnp.int32, qt.shape, 0)
    zero = jnp.zeros_like(qt)
    return jnp.where(row < head_dim, qt, zero), jnp.where(row >= head_dim, qt, zero)


ONES_ROWS = 16


def _init_state(refs):
    for m_ref, acc_ref in refs:
        m_ref[...] = jnp.full(m_ref.shape, NEG_INF, F32)
        acc_ref[...] = jnp.zeros(acc_ref.shape, F32)


def _normalized(acc_ref):
    acc = acc_ref[...]
    return acc[:V7X_LANES] / acc[V7X_LANES:V7X_LANES + 1]


def _toeplitz(x_row, n):
    x = jnp.broadcast_to(x_row, (n, x_row.shape[1]))
    return pltpu.roll(x, 0, 1, stride=1, stride_axis=0)[:, :n]


def _chunk_delta(n):
    kc = lax.broadcasted_iota(jnp.int32, (n, n), 0) // CHUNK
    qc = lax.broadcasted_iota(jnp.int32, (n, n), 1) // CHUNK
    return qc - kc


def _attn_a_kernel(qt_ref, qn_ref, k_ref, vt_ref, x_ref, lam_ref, sub_ref, o_ref, z_ref,
                   tab, fbuf, nbuf, m_st, acc_st, *, n_off, lambda_init):
    tq = ATTN_TILE
    step = pl.program_id(2)
    z_ref[...] = jnp.zeros(z_ref.shape, z_ref.dtype)

    @pl.when(step == 0)
    def _():
        for o in range(n_off):
            t = _toeplitz(x_ref[o], tq) * LOG2E
            if o == 0:
                t = jnp.where(_chunk_delta(tq) >= 0, t, NEG_INF)
            tab[o] = t

    for c in range(2):
        _init_state(((m_st.at[c, 0], acc_st.at[c, 0]), (m_st.at[c, 1], acc_st.at[c, 1])))
    q_maps = [_split_maps(qt_ref[:, c * tq:(c + 1) * tq], A_HEAD_DIM) for c in range(2)]

    def k_tile(j):
        return k_ref[pl.ds(pl.multiple_of(j * tq, tq), tq), :]

    def vt1_tile(j):
        vt = vt_ref[:, pl.ds(pl.multiple_of(j * tq, tq), tq)]
        return jnp.concatenate([vt, jnp.ones((ONES_ROWS, tq), vt.dtype)], axis=0)

    def scores(buf, k, c):
        for m in range(2):
            buf[c, m] = jnp.dot(k, q_maps[c][m], preferred_element_type=F32)

    def consume(buf, vt1, c, o):
        for m in range(2):
            s = buf[c, m]
            _softmax_pv(s if o >= n_off else s + tab[o], vt1, m_st.at[c, m], acc_st.at[c, m])

    def near_block(first_tile, n_tiles, first_scored):
        units = [[c for c in range(2) if n_tiles - 2 + c - s >= 0] for s in range(n_tiles)]
        buf_of = lambda s: fbuf.at[0] if s == 0 else nbuf.at[s - 1]

        for s in range(1 if first_scored else 0, n_tiles):
            k = k_tile(first_tile + s)
            for c in units[s]:
                scores(buf_of(s), k, c)
        for s in range(n_tiles):
            vt1 = vt1_tile(first_tile + s)
            for c in units[s]:
                consume(buf_of(s), vt1, c, n_tiles - 2 + c - s)

    first_general = (n_off - 1) // 2
    for i0 in range(first_general):
        @pl.when(step == i0)
        def _():
            near_block(0, 2 * i0 + 2, False)

    def far_step(j, parity):
        k = k_tile(j + 1)
        for c in range(2):
            scores(fbuf.at[1 - parity], k, c)
        vt1 = vt1_tile(j)
        for c in range(2):
            consume(fbuf.at[parity], vt1, c, n_off)

    n_far = jnp.maximum(2 * step - (n_off - 1), 0)
    rem = n_far % FAR_UNROLL

    @pl.when(rem >= 2)
    def _():
        far_step(0, 0)
        far_step(1, 1)

    def far_trip(t, carry):
        for u in range(FAR_UNROLL):
            far_step(FAR_UNROLL * t + rem + u, u % 2)
        return carry
    lax.fori_loop(0, n_far // FAR_UNROLL, far_trip, 0)

    @pl.when(step >= first_general)
    def _():
        near_block(n_far, n_off + 1, True)

    def epilogue():
        lam = lam_ref[...]
        lam_full = (jnp.exp(jnp.sum(lam[0:1] * lam[1:2], axis=-1, keepdims=True))
                    - jnp.exp(jnp.sum(lam[2:3] * lam[3:4], axis=-1, keepdims=True)) + lambda_init)
        for c in range(2):
            a = _normalized(acc_st.at[c, 0]) - lam_full * _normalized(acc_st.at[c, 1])
            ms = jnp.mean(a * a, axis=0, keepdims=True)
            an = a * lax.rsqrt(ms + NORM_EPS)
            o_ref[c * tq:(c + 1) * tq, :] = (an.T * (sub_ref[...] * (1.0 - lambda_init))).astype(BF16)

    last = pl.num_programs(2) - 1

    @pl.when(step < last)
    def _():
        k = k_tile(0)
        for c in range(2):
            qa, qb = _split_maps(qn_ref[:, c * tq:(c + 1) * tq], A_HEAD_DIM)
            fbuf[0, c, 0] = jnp.dot(k, qa, preferred_element_type=F32)
            fbuf[0, c, 1] = jnp.dot(k, qb, preferred_element_type=F32)
        epilogue()

    @pl.when(step == last)
    def _():
        epilogue()


def _attn_b_kernel(qt_ref, k_ref, vt_ref, x_ref, o_ref, tab, sbuf):
    tq = BAND_TILE
    n_blk = qt_ref.shape[1] // tq
    n_off = tab.shape[0] // 2
    step = pl.program_id(2)

    @pl.when(step == 0)
    def _():
        for m in range(2):
            for o in range(n_off):
                d = _chunk_delta(tq) + o * (tq // CHUNK)
                t = _toeplitz(x_ref[n_off * m + o], tq) * LOG2E
                tab[n_off * m + o] = jnp.where((d >= 0) & (d <= LEFT_CHUNKS), t, NEG_INF)

    q_maps = [_split_maps(qt_ref[:, c * tq:(c + 1) * tq], B_HEAD_DIM) for c in range(n_blk)]

    def run(first_step):
        pairs = [(c, o) for c in range(n_blk) for o in range(n_off - 1, -1, -1)
                 if not first_step or c - o >= 0]
        k_tiles, vt_tiles = {}, {}
        for c, o in pairs:
            if c - o not in k_tiles:
                ks = pl.multiple_of((n_blk * step + c - o) * tq, tq)
                k_tiles[c - o] = k_ref[pl.ds(ks, tq), :]
                vt = vt_ref[:, pl.ds(ks, tq)]
                vt_tiles[c - o] = jnp.concatenate([vt, jnp.ones((ONES_ROWS, tq), vt.dtype)], axis=0)
        def tile_scores(c):
            for o in [o for cc, o in pairs if cc == c]:
                for m in range(2):
                    sbuf[n_off * c + o, m] = jnp.dot(k_tiles[c - o], q_maps[c][m], preferred_element_type=F32)

        for c in range(min(BAND_AHEAD, n_blk)):
            tile_scores(c)
        for c in range(n_blk):
            if c + BAND_AHEAD < n_blk:
                tile_scores(c + BAND_AHEAD)
            outs = []
            for m in range(2):
                offs = [o for cc, o in pairs if cc == c]
                ss = [sbuf[n_off * c + o, m] + tab[n_off * m + o] for o in offs]
                mx = functools.reduce(jnp.maximum, [jnp.max(s, axis=0, keepdims=True) for s in ss])
                acc = sum(jnp.dot(vt_tiles[c - o], jnp.exp2(s - mx).astype(BF16), preferred_element_type=F32)
                          for o, s in zip(offs, ss))
                outs.append(acc[:V7X_LANES] / acc[V7X_LANES:V7X_LANES + 1])
            row = lax.broadcasted_iota(jnp.int32, outs[0].shape, 0)
            o_ref[c * tq:(c + 1) * tq, :] = jnp.where(row < B_HEAD_DIM, outs[0], outs[1]).T.astype(BF16)

    @pl.when(step == 0)
    def _():
        run(True)

    @pl.when(step >= 1)
    def _():
        run(False)


def _attn_specs(batch, seq, nq, tq):
    return dict(
        q=pl.BlockSpec((V7X_LANES, tq), lambda b, h, i: (h, b * nq + i)),
        k=pl.BlockSpec((seq, V7X_LANES), lambda b, h, i: (b, h)),
        v=pl.BlockSpec((V7X_LANES, seq), lambda b, h, i: (h, b)),
        o=pl.BlockSpec((tq, V7X_LANES), lambda b, h, i: (b * nq + i, h)),
    )


def _attention_a(qt, k, vt, xvec, lam, sub_g, batch, seq, lambda_init, zero_rows):
    d, t = qt.shape
    tq = ATTN_TILE
    n_off = xvec.shape[1]
    assert n_off % 2 == 1 and seq % (2 * tq) == 0
    nq = seq // (2 * tq)
    n_h = d // V7X_LANES
    zr = zero_rows // (batch * n_h * nq)
    assert zr * batch * n_h * nq == zero_rows and zr % MOE_CHUNK == 0
    sp = _attn_specs(batch, seq, nq, 2 * tq)
    rows = V7X_LANES + ONES_ROWS
    return pl.pallas_call(
        functools.partial(_attn_a_kernel, n_off=n_off, lambda_init=lambda_init),
        out_shape=(jax.ShapeDtypeStruct((t, d), BF16), jax.ShapeDtypeStruct((zero_rows, d), BF16)),
        grid=(batch, n_h, nq),
        in_specs=[sp["q"],
                  pl.BlockSpec((V7X_LANES, 2 * tq), lambda b, h, i: (h, b * nq + jnp.minimum(i + 1, nq - 1))),
                  sp["k"], sp["v"],
                  pl.BlockSpec((None,) + xvec.shape[1:], lambda b, h, i: (h, 0, 0, 0)),
                  pl.BlockSpec(lam.shape, lambda b, h, i: (0, 0)),
                  pl.BlockSpec(sub_g.shape, lambda b, h, i: (0, 0))],
        out_specs=(sp["o"], pl.BlockSpec((zr, d), lambda b, h, i: ((b * n_h + h) * nq + i, 0))),
        scratch_shapes=[pltpu.VMEM((n_off, tq, tq), F32), pltpu.VMEM((2, 2, 2, tq, tq), F32),
                        pltpu.VMEM((n_off, 2, 2, tq, tq), F32),
                        pltpu.VMEM((2, 2, 1, tq), F32), pltpu.VMEM((2, 2, rows, tq), F32)],
        compiler_params=_cparams(("parallel", "parallel", "arbitrary")),
        name="diff_attention",
    )(qt, qt, k, vt, xvec, lam, sub_g)


def _attention_b(qt, k, vt, xvec, batch, seq):
    d, t = qt.shape
    tq = BAND_TILE
    g = BAND_QBLOCKS
    n_off = xvec.shape[1] // 2
    assert g >= n_off - 1 and seq % (g * tq) == 0
    nq = seq // (g * tq)
    sp = _attn_specs(batch, seq, nq, g * tq)
    return pl.pallas_call(
        _attn_b_kernel,
        out_shape=jax.ShapeDtypeStruct((t, d), BF16),
        grid=(batch, d // V7X_LANES, nq),
        in_specs=[sp["q"], sp["k"], sp["v"],
                  pl.BlockSpec((None,) + xvec.shape[1:], lambda b, h, i: (h, 0, 0, 0))],
        out_specs=sp["o"],
        scratch_shapes=[pltpu.VMEM((2 * n_off, tq, tq), F32), pltpu.VMEM((n_off * g, 2, tq, tq), F32)],
        compiler_params=_cparams(("parallel", "parallel", "arbitrary")),
        name="chunk_attention",
    )(qt, k, vt, xvec)


def _t5_bucket(rel):
    nb = T5_BUCKETS // 2
    ret = jnp.where(rel > 0, nb, 0)
    n = jnp.abs(rel)
    max_exact = nb // 2
    nf = jnp.maximum(n, 1).astype(F32)
    large = max_exact + (jnp.log(nf / max_exact) / math.log(T5_MAX_DIST / max_exact)
                         * (nb - max_exact)).astype(jnp.int32)
    large = jnp.minimum(large, nb - 1)
    return ret + jnp.where(n < max_exact, n, large)


def _t5_const_distance():
    nb = T5_BUCKETS // 2
    max_exact = nb // 2
    n = np.arange(max_exact, 4 * T5_MAX_DIST, dtype=np.float64)
    large = max_exact + np.floor(np.log(n / max_exact) / math.log(T5_MAX_DIST / max_exact) * (nb - max_exact))
    below = np.nonzero(large < nb - 1)[0]
    return int(n[below[-1]]) + 2


def _tile_rel(tile, n_off):
    i = jnp.arange(2 * tile, dtype=jnp.int32)
    rel = jnp.where(i < tile, -i, 2 * tile - i)
    return rel[None, :] - tile * jnp.arange(n_off, dtype=jnp.int32)[:, None]


def _t5_vectors(t5_bias, tile):
    n_off = 1
    while (n_off - 1) * tile + 1 < _t5_const_distance():
        n_off += 1
    tb = t5_bias.astype(F32)
    vals = tb[_t5_bucket(_tile_rel(tile, n_off))] - tb[T5_BUCKETS // 2 - 1]
    return vals.transpose(2, 0, 1)[:, :, None, :]


def _band_vectors(rel_bias, tile):
    n_off = LEFT_CHUNKS * CHUNK // tile + 1
    idx = jnp.clip(_tile_rel(tile, n_off), -MAX_REL, MAX_REL) + MAX_REL
    vals = rel_bias.astype(F32)[:, idx]
    return vals.reshape(rel_bias.shape[0] // 2, 2 * n_off, 1, 2 * tile)


def _top2_sum4(r0, r1, r2, r3):
    a, b = jnp.maximum(r0, r1), jnp.minimum(r0, r1)
    c, d = jnp.maximum(r2, r3), jnp.minimum(r2, r3)
    return jnp.maximum(a, c) + jnp.maximum(jnp.minimum(a, c), jnp.maximum(b, d))


def _route(logits, rbias, before_ref):
    n_e, n = logits.shape
    scores = 1.0 / (1.0 + jnp.exp(-logits))
    sel = scores + rbias
    row = lax.broadcasted_iota(jnp.int32, sel.shape, 0)
    best = None
    for g in range(N_GROUPS):
        rows = [sel[g * E_PER_GROUP + i: g * E_PER_GROUP + i + 1, :] for i in range(E_PER_GROUP)]
        gs = _top2_sum4(*rows)
        if best is None:
            best, gidx = gs, jnp.zeros(gs.shape, jnp.int32)
        else:
            gidx = jnp.where(gs > best, g, gidx)
            best = jnp.maximum(best, gs)
    masked = jnp.where(row // E_PER_GROUP == gidx, sel, NEG_INF)
    m1 = jnp.max(masked, axis=0, keepdims=True)
    i1 = jnp.min(jnp.where(masked == m1, row, n_e), axis=0, keepdims=True)
    masked2 = jnp.where(row == i1, -3.0e38, masked)
    m2 = jnp.max(masked2, axis=0, keepdims=True)
    i2 = jnp.min(jnp.where(masked2 == m2, row, n_e), axis=0, keepdims=True)
    w1 = jnp.sum(jnp.where(row == i1, scores, 0.0), axis=0, keepdims=True)
    w2 = jnp.sum(jnp.where(row == i2, scores, 0.0), axis=0, keepdims=True)
    den = w1 + w2
    member = jnp.where((row == i1) | (row == i2), 1.0, 0.0).astype(BF16)
    before = before_ref[...]
    pos = jnp.concatenate(
        [jnp.dot(member[:, i:i + MOE_TILE], before, preferred_element_type=F32)
         for i in range(0, n, MOE_TILE)], axis=1)
    p1 = jnp.sum(jnp.where(row == i1, pos, 0.0), axis=0, keepdims=True)
    p2 = jnp.sum(jnp.where(row == i2, pos, 0.0), axis=0, keepdims=True)
    out_row = lax.broadcasted_iota(jnp.int32, (8, n), 0)
    out = jnp.zeros((8, n), F32)
    for r, val in enumerate((i1.astype(F32), i2.astype(F32), p1, p2, w1 / den, w2 / den)):
        out = jnp.where(out_row == r, val, out)
    return out


def _oproj_kernel(o_ref, wo_ref, x_ref, g1_ref, gn_ref, sc_ref, sh_ref, rwt_ref, rb_ref,
                  xo_ref, h_ref, route_ref, before_ref):
    @pl.when(pl.program_id(0) == 0)
    def _():
        t_from = lax.broadcasted_iota(jnp.int32, before_ref.shape, 0)
        t_to = lax.broadcasted_iota(jnp.int32, before_ref.shape, 1)
        before_ref[...] = jnp.where(t_from < t_to, 1.0, 0.0).astype(BF16)

    n_e = rb_ref.shape[0]
    rw = rwt_ref[...]
    rows = o_ref.shape[0] // OPROJ_SPLIT
    groups = [slice(r * rows, (r + 1) * rows) for r in range(OPROJ_SPLIT)]
    ys = [jnp.dot(o_ref[g, :], wo_ref[...], preferred_element_type=F32) for g in groups]
    logits = []
    for g, y in zip(groups, ys):
        xn = x_ref[g, :] + g1_ref[...] * y
        xo_ref[g, :] = xn
        h = _norm_mod(xn, gn_ref[...], sc_ref[...], sh_ref[...])
        h_hi = h.astype(BF16)
        h_ref[g, :] = h_hi
        h_lo = (h - h_hi.astype(F32)).astype(BF16)
        main = lax.dot_general(rw, h_hi, _NT, preferred_element_type=F32)
        corr = lax.dot_general(rw[:n_e], h_lo, _NT, preferred_element_type=F32)
        logits.append(main[:n_e] + main[n_e:] + corr)
    route_ref[...] = _route(jnp.concatenate(logits, axis=1), rb_ref[...], before_ref)


def _out_proj(o, wo_bf16, x2, g1, gn, sc, sh, rwt, rb, seq):
    t, d = x2.shape
    tm = OPROJ_ROW_TILE
    per_b = seq // tm
    n_e = rb.shape[0]
    vec = lambda i: (i // per_b, 0, 0)
    return pl.pallas_call(
        _oproj_kernel,
        out_shape=(jax.ShapeDtypeStruct((t, d), F32),
                   jax.ShapeDtypeStruct((t, d), BF16),
                   jax.ShapeDtypeStruct((8, t), F32)),
        grid=(t // tm,),
        in_specs=[
            pl.BlockSpec((tm, d), lambda i: (i, 0)),
            pl.BlockSpec((d, d), lambda i: (0, 0)),
            pl.BlockSpec((tm, d), lambda i: (i, 0)),
            pl.BlockSpec((None, 1, d), vec),
            pl.BlockSpec((1, d), lambda i: (0, 0)),
            pl.BlockSpec((None, 1, d), vec),
            pl.BlockSpec((None, 1, d), vec),
            pl.BlockSpec((2 * n_e, d), lambda i: (0, 0)),
            pl.BlockSpec((n_e, 1), lambda i: (0, 0)),
        ],
        out_specs=(pl.BlockSpec((tm, d), lambda i: (i, 0)),
                   pl.BlockSpec((tm, d), lambda i: (i, 0)),
                   pl.BlockSpec((8, tm), lambda i: (0, i))),
        scratch_shapes=[pltpu.VMEM((MOE_TILE, MOE_TILE), BF16)],
        compiler_params=_cparams(("arbitrary",)),
        name="out_proj_route",
    )(o, wo_bf16, x2, g1, gn, sc, sh, rwt, rb)


def _moe_plan(route, t, tm, n_e):
    n_tiles = t // tm
    ch, ft = MOE_CHUNK, MOE_FFN_TILE
    loc_rows, main_rows, n_sorted = _moe_rows(t, tm, n_e)
    ids = jnp.arange(n_e, dtype=jnp.int32)
    e = route[0:2].astype(jnp.int32)
    pos = route[2:4].astype(jnp.int32)
    oh = (e[:, :, None] == ids).astype(jnp.int32)
    cnt = oh.sum(0).reshape(n_tiles, tm, n_e).sum(1)
    seg = (cnt + ch - 1) // ch * ch
    loc = jnp.cumsum(seg, axis=1) - seg
    tot = seg.sum(0)
    totp = (tot + ft - 1) // ft * ft
    base = jnp.cumsum(totp) - totp
    gdest = base[None, :] + jnp.cumsum(seg, axis=0) - seg
    col = (oh * jnp.repeat(loc, tm, axis=0)[None]).sum(-1) + pos

    row0 = jnp.arange(loc_rows // ch, dtype=jnp.int32) * ch
    ej = (row0[None, :, None] >= (loc + seg)[:, None, :]).sum(-1)
    ohj = (jnp.minimum(ej, n_e - 1)[..., None] == ids).astype(jnp.int32)
    dst = (ohj * (gdest - loc)[:, None, :]).sum(-1) + row0[None, :]
    dump = main_rows + (jnp.arange(n_tiles, dtype=jnp.int32) % 2)[:, None] * loc_rows + row0[None, :]
    dst = jnp.where(ej < n_e, dst, dump) // ch

    r0 = jnp.arange(n_sorted // ft, dtype=jnp.int32) * ft
    ends = base + totp
    te = jnp.minimum((r0[:, None] >= ends[None, :]).sum(-1), n_e - 1)
    tvalid = (r0 < ends[-1]).astype(jnp.int32)
    tfirst = tvalid * (r0 == (((te[:, None] == ids) * base[None, :]).sum(-1))).astype(jnp.int32)
    tblock = jnp.minimum(r0 // ft, ends[-1] // ft - 1).astype(jnp.int32)
    return col, dst.astype(jnp.int32), te.astype(jnp.int32), tblock, tfirst


def _moe_rows(t, tm, n_e):
    ch, ft = MOE_CHUNK, MOE_FFN_TILE
    loc_rows = 2 * tm + n_e * ch
    main = 2 * t + (t // tm) * n_e * (ch - 1) + n_e * (ft - 1)
    main = (main + ft - 1) // ft * ft
    return loc_rows, main, (main + 2 * loc_rows + ft - 1) // ft * ft


def _chunk_copies(dst_ref, tile, local_ref, sorted_ref, sem, to_sorted):
    ch = MOE_CHUNK
    copies = []
    for j in range(local_ref.shape[0] // ch):
        far = sorted_ref.at[pl.ds(pl.multiple_of(dst_ref[tile, j] * ch, ch), ch)]
        near = local_ref.at[pl.ds(j * ch, ch)]
        copies.append(pltpu.make_async_copy(near, far, sem) if to_sorted
                      else pltpu.make_async_copy(far, near, sem))
    return copies


def _dispatch_kernel(dst_ref, col_ref, h_ref, init_ref, xs_ref, comp, sem):
    del init_ref
    tile = pl.program_id(0)
    last = pl.num_programs(0) - 1
    slot = tile % 2

    def wait_slot(s):
        pltpu.make_async_copy(comp.at[s], xs_ref.at[pl.ds(0, comp.shape[1])], sem.at[s]).wait()

    @pl.when(tile >= 2)
    def _():
        wait_slot(slot)

    col = col_ref[...]
    srow = lax.broadcasted_iota(jnp.int32, (comp.shape[1], col.shape[1]), 0)
    sel = jnp.where((srow == col[0:1]) | (srow == col[1:2]), 1.0, 0.0).astype(BF16)
    comp[slot] = jnp.dot(sel, h_ref[...], preferred_element_type=F32).astype(BF16)
    for cp in _chunk_copies(dst_ref, tile, comp.at[slot], xs_ref, sem.at[slot], True):
        cp.start()

    @pl.when(tile == last)
    def _():
        wait_slot(slot)

    @pl.when((tile == last) & (tile >= 1))
    def _():
        wait_slot(1 - slot)


def _dispatch(dst, col, h, n_e, init):
    t, d = h.shape
    tm = MOE_TILE
    loc_rows, _, n_sorted = _moe_rows(t, tm, n_e)
    assert init.shape == (n_sorted, d) and init.dtype == BF16
    return pl.pallas_call(
        _dispatch_kernel,
        out_shape=jax.ShapeDtypeStruct((n_sorted, d), BF16),
        grid_spec=pltpu.PrefetchScalarGridSpec(
            num_scalar_prefetch=1,
            grid=(t // tm,),
            in_specs=[pl.BlockSpec((2, tm), lambda i, dst: (0, i)),
                      pl.BlockSpec((tm, d), lambda i, dst: (i, 0)),
                      pl.BlockSpec(memory_space=pl.ANY)],
            out_specs=pl.BlockSpec(memory_space=pl.ANY),
            scratch_shapes=[pltpu.VMEM((2, loc_rows, d), BF16), pltpu.SemaphoreType.DMA((2,))],
        ),
        input_output_aliases={3: 0},
        compiler_params=_cparams(("arbitrary",)),
        name="moe_dispatch",
    )(dst, col, h, init)


def _expert_kernel(te_ref, tb_ref, tf_ref, xs_ref, wg_ref, wu_ref, wd_ref, ys_ref, wg_b, wu_b, wd_b):
    del te_ref
    s = pl.program_id(0)

    @pl.when(tf_ref[s] == 1)
    def _():
        wg_b[...] = wg_ref[...].astype(BF16)
        wu_b[...] = wu_ref[...].astype(BF16)
        wd_b[...] = wd_ref[...].astype(BF16)

    @pl.when(tb_ref[s] == s)
    def _():
        rows = ys_ref.shape[0] // MOE_FFN_SPLIT
        hs = []
        for r in range(MOE_FFN_SPLIT):
            xs = xs_ref[r * rows:(r + 1) * rows, :]
            hg = jnp.dot(xs, wg_b[...], preferred_element_type=F32)
            hu = jnp.dot(xs, wu_b[...], preferred_element_type=F32)
            hs.append((hg, hu))
        for r, (hg, hu) in enumerate(hs):
            he = hg * (1.0 / (1.0 + jnp.exp(-hg))) * hu
            ys_ref[r * rows:(r + 1) * rows, :] = jnp.dot(
                he.astype(BF16), wd_b[...], preferred_element_type=F32).astype(BF16)


def _experts(te, tblock, tfirst, xs, wg, wu, wd, layer):
    n_sorted, d = xs.shape
    f = wg.shape[3]
    ft = MOE_FFN_TILE
    wspec = lambda r, c: pl.BlockSpec((None, None, r, c), lambda s, te, tb, tf: (layer, te[s], 0, 0))
    return pl.pallas_call(
        _expert_kernel,
        out_shape=jax.ShapeDtypeStruct((n_sorted, d), BF16),
        grid_spec=pltpu.PrefetchScalarGridSpec(
            num_scalar_prefetch=3,
            grid=(n_sorted // ft,),
            in_specs=[pl.BlockSpec((ft, d), lambda s, te, tb, tf: (tb[s], 0)),
                      wspec(d, f), wspec(d, f), wspec(f, d)],
            out_specs=pl.BlockSpec((ft, d), lambda s, te, tb, tf: (tb[s], 0)),
            scratch_shapes=[pltpu.VMEM((d, f), BF16), pltpu.VMEM((d, f), BF16), pltpu.VMEM((f, d), BF16)],
        ),
        input_output_aliases={3: 0},
        compiler_params=_cparams(("arbitrary",)),
        name="moe_experts",
    )(te, tblock, tfirst, xs, wg, wu, wd)


def _combine_kernel(dst_ref, w_ref, ys_ref, x_ref, g2_ref, o_ref, comp, sem):
    tile = pl.program_id(0)
    slot = tile % 2

    def fetch(t, s):
        for cp in _chunk_copies(dst_ref, t, comp.at[s], ys_ref, sem.at[s], False):
            cp.start()

    def wait_slot(s):
        pltpu.make_async_copy(ys_ref.at[pl.ds(0, comp.shape[1])], comp.at[s], sem.at[s]).wait()

    last = pl.num_programs(0) - 1

    @pl.when(tile == 0)
    def _():
        fetch(tile, slot)

    fetch(jnp.minimum(tile + 1, last), 1 - slot)

    w = w_ref[...].T
    scol = lax.broadcasted_iota(jnp.int32, (w.shape[0], comp.shape[1]), 1).astype(F32)
    selw = (jnp.where(scol == w[:, 6:7], w[:, 4:5], 0.0)
            + jnp.where(scol == w[:, 7:8], w[:, 5:6], 0.0)).astype(BF16)
    wait_slot(slot)
    acc = jnp.dot(selw, comp[slot], preferred_element_type=F32)
    o_ref[...] = x_ref[...] + g2_ref[...] * acc

    @pl.when(tile == last)
    def _():
        wait_slot(1 - slot)


def _combine(dst, route_col, ys, x2, g2, seq, n_e):
    t, d = x2.shape
    tm = MOE_TILE
    per_b = seq // tm
    loc_rows = _moe_rows(t, tm, n_e)[0]
    return pl.pallas_call(
        _combine_kernel,
        out_shape=jax.ShapeDtypeStruct((t, d), F32),
        grid_spec=pltpu.PrefetchScalarGridSpec(
            num_scalar_prefetch=1,
            grid=(t // tm,),
            in_specs=[pl.BlockSpec((8, tm), lambda i, dst: (0, i)),
                      pl.BlockSpec(memory_space=pl.ANY),
                      pl.BlockSpec((tm, d), lambda i, dst: (i, 0)),
                      pl.BlockSpec((None, 1, d), lambda i, dst: (i // per_b, 0, 0))],
            out_specs=pl.BlockSpec((tm, d), lambda i, dst: (i, 0)),
            scratch_shapes=[pltpu.VMEM((2, loc_rows, d), BF16), pltpu.SemaphoreType.DMA((2,))],
        ),
        compiler_params=_cparams(("arbitrary",)),
        name="moe_combine",
    )(dst, route_col, ys, x2, g2)


def _moe(h, route, wg, wu, wd, layer, x2, g2, seq, sorted_init):
    t = x2.shape[0]
    n_e = wg.shape[1]
    col, dst, te, tblock, tfirst = _moe_plan(route, t, MOE_TILE, n_e)
    xs = _dispatch(dst, col, h, n_e, sorted_init)
    ys = _experts(te, tblock, tfirst, xs, wg, wu, wd, layer)
    route_col = jnp.concatenate([route[:6], col.astype(F32)], axis=0)
    return _combine(dst, route_col, ys, x2, g2, seq, n_e), ys


def kernel(x, c, ada_w, ada_b, norm_mix_g, norm_ffn_g, t5_bias, a_w_qkv, a_q_gain, a_k_gain, a_lambda, a_subln_g, a_w_o, b_w_qkv, b_q_gain, b_k_gain, b_rel_bias, b_w_o, router_w, router_bias, moe_w_gate, moe_w_up, moe_w_down):
    batch, seq, d = x.shape
    depth = ada_w.shape[0]
    assert seq % ATTN_TILE == 0 and seq % ROW_TILE == 0 and d == A_HEADS * 2 * A_HEAD_DIM
    assert d == B_HEADS * B_HEAD_DIM and A_HEAD_DIM == B_HEAD_DIM

    c_pad = jnp.zeros((8, d), F32).at[:batch].set(c.astype(F32))
    mod = _modulation(c_pad, ada_w.astype(F32), ada_b.astype(F32))[:, :batch]
    mod = mod.reshape(depth, batch, 6, 1, d)

    rwt = router_w.astype(F32).T
    rwt_hi = lax.bitcast_convert_type(
        lax.bitcast_convert_type(rwt, jnp.uint32) & jnp.uint32(0xFFFF0000), F32)
    rwt = jnp.concatenate([rwt_hi, rwt - rwt_hi], axis=0).astype(BF16)
    rb = router_bias.astype(F32).reshape(-1, 1)

    x2 = x.astype(F32).reshape(batch * seq, d)
    sorted_buf = None
    for i in range(depth):
        sh1, sc1, g1, sh2, sc2, g2 = [mod[i, :, k] for k in range(6)]
        j = i // 2
        if i % 2 == 0:
            w_qkv, qg, kg, w_o = a_w_qkv[j], a_q_gain[j], a_k_gain[j], a_w_o[j]
        else:
            w_qkv, qg, kg, w_o = b_w_qkv[j], b_q_gain[j], b_k_gain[j], b_w_o[j]
        n_rep = d // qg.shape[0]
        gq = (jnp.tile(qg.astype(F32), n_rep) * (A_HEAD_DIM ** -0.5 * LOG2E)).reshape(d, 1)
        gk = jnp.tile(kg.astype(F32), n_rep).reshape(1, d)
        qt, k, vt = _qkv_proj(x2, norm_mix_g[i].reshape(1, d), sc1, sh1, w_qkv.astype(F32),
                              gq, gk, seq, A_HEAD_DIM)
        if i % 2 == 0:
            lambda_init = 0.8 - 0.6 * math.exp(-0.3 * i)
            n_sorted = _moe_rows(batch * seq, MOE_TILE, moe_w_gate.shape[1])[2]
            o, zeros_buf = _attention_a(qt, k, vt, _t5_vectors(t5_bias, ATTN_TILE), a_lambda[j].astype(F32),
                                        a_subln_g[j].reshape(1, -1), batch, seq, lambda_init, n_sorted)
            if sorted_buf is None:
                sorted_buf = zeros_buf
        else:
            o = _attention_b(qt, k, vt, _band_vectors(b_rel_bias[j], BAND_TILE), batch, seq)
        x2, h, route = _out_proj(o, w_o.astype(BF16), x2, g1, norm_ffn_g[i].reshape(1, d),
                                 sc2, sh2, rwt, rb, seq)
        x2, sorted_buf = _moe(h, route, moe_w_gate, moe_w_up, moe_w_down, i, x2, g2, seq, sorted_buf)
    return x2.reshape(batch, seq, d)
```

```python
import functools
import math

import numpy as np
import jax
import jax.numpy as jnp
from jax import lax
from jax.experimental import pallas as pl
from jax.experimental.pallas import tpu as pltpu

F32 = jnp.float32
BF16 = jnp.bfloat16

CHUNK = 64
A_HEADS = 8
A_HEAD_DIM = 64
T5_BUCKETS = 32
T5_MAX_DIST = 1024
B_HEADS = 16
B_HEAD_DIM = 64
LEFT_CHUNKS = 8
MAX_REL = 256
N_EXPERTS = 16
N_GROUPS = 4
E_PER_GROUP = N_EXPERTS // N_GROUPS
NORM_EPS = 1e-6
NEG_INF = -1e30
LOG2E = math.log2(math.e)

V7X_LANES = 128
V7X_MXU_DIM = 256

ATTN_TILE = 512
MOD_COL_TILE = 1536
FAR_UNROLL = 4
BAND_TILE = 256
BAND_QBLOCKS = 16
BAND_AHEAD = 2
ROW_TILE = 512
QKV_SPLIT = 2
MOE_TILE = 512
OPROJ_ROW_TILE = 1024
OPROJ_SPLIT = 4
MOE_CHUNK = 16
MOE_FFN_TILE = 512
MOE_FFN_SPLIT = 2
VMEM_LIMIT = 56 * 1024 * 1024

_NT = (((1,), (1,)), ((), ()))


def _cparams(sem):
    return pltpu.CompilerParams(dimension_semantics=sem, vmem_limit_bytes=VMEM_LIMIT)


def _mod_kernel(c_ref, w_ref, b_ref, o_ref):
    c = c_ref[...]
    s = c * (1.0 / (1.0 + jnp.exp(-c)))
    w = w_ref[...]
    s_hi, w_hi = s.astype(BF16), w.astype(BF16)
    s_lo = (s - s_hi.astype(F32)).astype(BF16)
    w_lo = (w - w_hi.astype(F32)).astype(BF16)
    dot = functools.partial(jnp.dot, preferred_element_type=F32)
    o_ref[...] = dot(s_hi, w_hi) + dot(s_hi, w_lo) + dot(s_lo, w_hi) + b_ref[...]


def _modulation(c_pad, ada_w, ada_b):
    depth, d, n = ada_w.shape
    rows = c_pad.shape[0]
    tn = MOD_COL_TILE
    return pl.pallas_call(
        _mod_kernel,
        out_shape=jax.ShapeDtypeStruct((depth, rows, n), F32),
        grid=(depth, n // tn),
        in_specs=[
            pl.BlockSpec((rows, d), lambda i, j: (0, 0)),
            pl.BlockSpec((None, d, tn), lambda i, j: (i, 0, j)),
            pl.BlockSpec((None, 1, tn), lambda i, j: (i, 0, j)),
        ],
        out_specs=pl.BlockSpec((None, rows, tn), lambda i, j: (i, 0, j)),
        compiler_params=_cparams(("parallel", "parallel")),
        name="adaln_mod",
    )(c_pad, ada_w, ada_b.reshape(depth, 1, n))


def _norm_mod(x, g, sc, sh):
    ms = jnp.mean(x * x, axis=-1, keepdims=True)
    return x * lax.rsqrt(ms + NORM_EPS) * g * (1.0 + sc) + sh


def _qkv_kernel(x_ref, g_ref, sc_ref, sh_ref, w_ref, gq_ref, gk_ref,
                qt_ref, k_ref, vt_ref, wqt_ref, wk_ref, wvt_ref, *, head_dim):
    d, tm = qt_ref.shape

    @pl.when(pl.program_id(0) == 0)
    def _():
        wqt_ref[...] = w_ref[:, :d].T.astype(BF16)
        wk_ref[...] = w_ref[:, d:2 * d].astype(BF16)
        wvt_ref[...] = w_ref[:, 2 * d:].T.astype(BF16)

    cw = V7X_MXU_DIM
    r = lax.broadcasted_iota(jnp.int32, (cw, cw), 0) // head_dim
    c = lax.broadcasted_iota(jnp.int32, (cw, cw), 1) // head_dim
    gmat = jnp.where(r == c, 1.0 / head_dim, 0.0).astype(BF16)
    tg = tm // QKV_SPLIT
    for gi in range(QKV_SPLIT):
        rows = slice(gi * tg, (gi + 1) * tg)
        h = _norm_mod(x_ref[rows, :], g_ref[...], sc_ref[...], sh_ref[...]).astype(BF16)
        y = lax.dot_general(wqt_ref[...], h, _NT, preferred_element_type=F32)
        y3 = y.reshape(d // head_dim, head_dim, tg)
        ss = jnp.mean(y3 * y3, axis=1, keepdims=True)
        qt_ref[:, rows] = ((y3 * lax.rsqrt(ss + NORM_EPS)).reshape(d, tg) * gq_ref[...]).astype(BF16)
        y = jnp.dot(h, wk_ref[...], preferred_element_type=F32)
        ysq = (y * y).astype(BF16)
        for ci in range(d // cw):
            cols = slice(ci * cw, (ci + 1) * cw)
            ss = jnp.dot(ysq[:, cols], gmat, preferred_element_type=F32)
            k_ref[rows, cols] = (y[:, cols] * lax.rsqrt(ss + NORM_EPS) * gk_ref[:, cols]).astype(BF16)
        vt_ref[:, rows] = lax.dot_general(wvt_ref[...], h, _NT, preferred_element_type=F32).astype(BF16)


def _qkv_proj(x2, g, sc, sh, w, gq, gk, seq, head_dim):
    t, d = x2.shape
    tm = ROW_TILE * QKV_SPLIT
    per_b = seq // tm
    vec = lambda i: (i // per_b, 0, 0)
    full = lambda i: (0, 0)
    return pl.pallas_call(
        functools.partial(_qkv_kernel, head_dim=head_dim),
        out_shape=(jax.ShapeDtypeStruct((d, t), BF16),
                   jax.ShapeDtypeStruct((t, d), BF16),
                   jax.ShapeDtypeStruct((d, t), BF16)),
        grid=(t // tm,),
        in_specs=[
            pl.BlockSpec((tm, d), lambda i: (i, 0)),
            pl.BlockSpec((1, d), full),
            pl.BlockSpec((None, 1, d), vec),
            pl.BlockSpec((None, 1, d), vec),
            pl.BlockSpec((d, 3 * d), full, pipeline_mode=pl.Buffered(1)),
            pl.BlockSpec((d, 1), full),
            pl.BlockSpec((1, d), full),
        ],
        out_specs=(pl.BlockSpec((d, tm), lambda i: (0, i)),
                   pl.BlockSpec((tm, d), lambda i: (i, 0)),
                   pl.BlockSpec((d, tm), lambda i: (0, i))),
        scratch_shapes=[pltpu.VMEM((d, d), BF16), pltpu.VMEM((d, d), BF16), pltpu.VMEM((d, d), BF16)],
        compiler_params=_cparams(("arbitrary",)),
        name="qkv_proj",
    )(x2, g, sc, sh, w, gq, gk)


def _softmax_pv(s, vt1, m_ref, acc_ref):
    m_prev = m_ref[...]
    m_new = jnp.maximum(m_prev, jnp.max(s, axis=0, keepdims=True))
    alpha = jnp.exp2(m_prev - m_new)
    p = jnp.exp2(s - m_new).astype(BF16)
    acc_ref[...] = alpha * acc_ref[...] + jnp.dot(vt1, p, preferred_element_type=F32)
    m_ref[...] = m_new


def _split_maps(qt, head_dim):
    row = lax.broadcasted_iota(jnp.int32, qt.shape, 0)
    zero = jnp.zeros_like(qt)
    return jnp.where(row < head_dim, qt, zero), jnp.where(row >= head_dim, qt, zero)


ONES_ROWS = 16


def _init_state(refs):
    for m_ref, acc_ref in refs:
        m_ref[...] = jnp.full(m_ref.shape, NEG_INF, F32)
        acc_ref[...] = jnp.zeros(acc_ref.shape, F32)


def _normalized(acc_ref):
    acc = acc_ref[...]
    return acc[:V7X_LANES] / acc[V7X_LANES:V7X_LANES + 1]


def _toeplitz(x_row, n):
    x = jnp.broadcast_to(x_row, (n, x_row.shape[1]))
    return pltpu.roll(x, 0, 1, stride=1, stride_axis=0)[:, :n]


def _chunk_delta(n):
    kc = lax.broadcasted_iota(jnp.int32, (n, n), 0) // CHUNK
    qc = lax.broadcasted_iota(jnp.int32, (n, n), 1) // CHUNK
    return qc - kc


def _attn_a_kernel(qt_ref, qn_ref, k_ref, vt_ref, x_ref, lam_ref, sub_ref, o_ref, z_ref,
                   tab, fbuf, nbuf, m_st, acc_st, *, n_off, lambda_init):
    tq = ATTN_TILE
    step = pl.program_id(2)
    z_ref[...] = jnp.zeros(z_ref.shape, z_ref.dtype)

    @pl.when(step == 0)
    def _():
        for o in range(n_off):
            t = _toeplitz(x_ref[o], tq) * LOG2E
            if o == 0:
                t = jnp.where(_chunk_delta(tq) >= 0, t, NEG_INF)
            tab[o] = t

    for c in range(2):
        _init_state(((m_st.at[c, 0], acc_st.at[c, 0]), (m_st.at[c, 1], acc_st.at[c, 1])))
    q_maps = [_split_maps(qt_ref[:, c * tq:(c + 1) * tq], A_HEAD_DIM) for c in range(2)]

    def k_tile(j):
        return k_ref[pl.ds(pl.multiple_of(j * tq, tq), tq), :]

    def vt1_tile(j):
        vt = vt_ref[:, pl.ds(pl.multiple_of(j * tq, tq), tq)]
        return jnp.concatenate([vt, jnp.ones((ONES_ROWS, tq), vt.dtype)], axis=0)

    def scores(buf, k, c):
        for m in range(2):
            buf[c, m] = jnp.dot(k, q_maps[c][m], preferred_element_type=F32)

    def consume(buf, vt1, c, o):
        for m in range(2):
            s = buf[c, m]
            _softmax_pv(s if o >= n_off else s + tab[o], vt1, m_st.at[c, m], acc_st.at[c, m])

    def near_block(first_tile, n_tiles, first_scored):
        units = [[c for c in range(2) if n_tiles - 2 + c - s >= 0] for s in range(n_tiles)]
        buf_of = lambda s: fbuf.at[0] if s == 0 else nbuf.at[s - 1]

        for s in range(1 if first_scored else 0, n_tiles):
            k = k_tile(first_tile + s)
            for c in units[s]:
                scores(buf_of(s), k, c)
        for s in range(n_tiles):
            vt1 = vt1_tile(first_tile + s)
            for c in units[s]:
                consume(buf_of(s), vt1, c, n_tiles - 2 + c - s)

    first_general = (n_off - 1) // 2
    for i0 in range(first_general):
        @pl.when(step == i0)
        def _():
            near_block(0, 2 * i0 + 2, False)

    def far_step(j, parity):
        k = k_tile(j + 1)
        for c in range(2):
            scores(fbuf.at[1 - parity], k, c)
        vt1 = vt1_tile(j)
        for c in range(2):
            consume(fbuf.at[parity], vt1, c, n_off)

    n_far = jnp.maximum(2 * step - (n_off - 1), 0)
    rem = n_far % FAR_UNROLL

    @pl.when(rem >= 2)
    def _():
        far_step(0, 0)
        far_step(1, 1)

    def far_trip(t, carry):
        for u in range(FAR_UNROLL):
            far_step(FAR_UNROLL * t + rem + u, u % 2)
        return carry
    lax.fori_loop(0, n_far // FAR_UNROLL, far_trip, 0)

    @pl.when(step >= first_general)
    def _():
        near_block(n_far, n_off + 1, True)

    def epilogue():
        lam = lam_ref[...]
        lam_full = (jnp.exp(jnp.sum(lam[0:1] * lam[1:2], axis=-1, keepdims=True))
                    - jnp.exp(jnp.sum(lam[2:3] * lam[3:4], axis=-1, keepdims=True)) + lambda_init)
        for c in range(2):
            a = _normalized(acc_st.at[c, 0]) - lam_full * _normalized(acc_st.at[c, 1])
            ms = jnp.mean(a * a, axis=0, keepdims=True)
            an = a * lax.rsqrt(ms + NORM_EPS)
            o_ref[c * tq:(c + 1) * tq, :] = (an.T * (sub_ref[...] * (1.0 - lambda_init))).astype(BF16)

    last = pl.num_programs(2) - 1

    @pl.when(step < last)
    def _():
        k = k_tile(0)
        for c in range(2):
            qa, qb = _split_maps(qn_ref[:, c * tq:(c + 1) * tq], A_HEAD_DIM)
            fbuf[0, c, 0] = jnp.dot(k, qa, preferred_element_type=F32)
            fbuf[0, c, 1] = jnp.dot(k, qb, preferred_element_type=F32)
        epilogue()

    @pl.when(step == last)
    def _():
        epilogue()


def _attn_b_kernel(qt_ref, k_ref, vt_ref, x_ref, o_ref, tab, sbuf):
    tq = BAND_TILE
    n_blk = qt_ref.shape[1] // tq
    n_off = tab.shape[0] // 2
    step = pl.program_id(2)

    @pl.when(step == 0)
    def _():
        for m in range(2):
            for o in range(n_off):
                d = _chunk_delta(tq) + o * (tq // CHUNK)
                t = _toeplitz(x_ref[n_off * m + o], tq) * LOG2E
                tab[n_off * m + o] = jnp.where((d >= 0) & (d <= LEFT_CHUNKS), t, NEG_INF)

    q_maps = [_split_maps(qt_ref[:, c * tq:(c + 1) * tq], B_HEAD_DIM) for c in range(n_blk)]

    def run(first_step):
        pairs = [(c, o) for c in range(n_blk) for o in range(n_off - 1, -1, -1)
                 if not first_step or c - o >= 0]
        k_tiles, vt_tiles = {}, {}
        for c, o in pairs:
            if c - o not in k_tiles:
                ks = pl.multiple_of((n_blk * step + c - o) * tq, tq)
                k_tiles[c - o] = k_ref[pl.ds(ks, tq), :]
                vt = vt_ref[:, pl.ds(ks, tq)]
                vt_tiles[c - o] = jnp.concatenate([vt, jnp.ones((ONES_ROWS, tq), vt.dtype)], axis=0)
        def tile_scores(c):
            for o in [o for cc, o in pairs if cc == c]:
                for m in range(2):
                    sbuf[n_off * c + o, m] = jnp.dot(k_tiles[c - o], q_maps[c][m], preferred_element_type=F32)

        for c in range(min(BAND_AHEAD, n_blk)):
            tile_scores(c)
        for c in range(n_blk):
            if c + BAND_AHEAD < n_blk:
                tile_scores(c + BAND_AHEAD)
            outs = []
            for m in range(2):
                offs = [o for cc, o in pairs if cc == c]
                ss = [sbuf[n_off * c + o, m] + tab[n_off * m + o] for o in offs]
                mx = functools.reduce(jnp.maximum, [jnp.max(s, axis=0, keepdims=True) for s in ss])
                acc = sum(jnp.dot(vt_tiles[c - o], jnp.exp2(s - mx).astype(BF16), preferred_element_type=F32)
                          for o, s in zip(offs, ss))
                outs.append(acc[:V7X_LANES] / acc[V7X_LANES:V7X_LANES + 1])
            row = lax.broadcasted_iota(jnp.int32, outs[0].shape, 0)
            o_ref[c * tq:(c + 1) * tq, :] = jnp.where(row < B_HEAD_DIM, outs[0], outs[1]).T.astype(BF16)

    @pl.when(step == 0)
    def _():
        run(True)

    @pl.when(step >= 1)
    def _():
        run(False)


def _attn_specs(batch, seq, nq, tq):
    return dict(
        q=pl.BlockSpec((V7X_LANES, tq), lambda b, h, i: (h, b * nq + i)),
        k=pl.BlockSpec((seq, V7X_LANES), lambda b, h, i: (b, h)),
        v=pl.BlockSpec((V7X_LANES, seq), lambda b, h, i: (h, b)),
        o=pl.BlockSpec((tq, V7X_LANES), lambda b, h, i: (b * nq + i, h)),
    )


def _attention_a(qt, k, vt, xvec, lam, sub_g, batch, seq, lambda_init, zero_rows):
    d, t = qt.shape
    tq = ATTN_TILE
    n_off = xvec.shape[1]
    assert n_off % 2 == 1 and seq % (2 * tq) == 0
    nq = seq // (2 * tq)
    n_h = d // V7X_LANES
    zr = zero_rows // (batch * n_h * nq)
    assert zr * batch * n_h * nq == zero_rows and zr % MOE_CHUNK == 0
    sp = _attn_specs(batch, seq, nq, 2 * tq)
    rows = V7X_LANES + ONES_ROWS
    return pl.pallas_call(
        functools.partial(_attn_a_kernel, n_off=n_off, lambda_init=lambda_init),
        out_shape=(jax.ShapeDtypeStruct((t, d), BF16), jax.ShapeDtypeStruct((zero_rows, d), BF16)),
        grid=(batch, n_h, nq),
        in_specs=[sp["q"],
                  pl.BlockSpec((V7X_LANES, 2 * tq), lambda b, h, i: (h, b * nq + jnp.minimum(i + 1, nq - 1))),
                  sp["k"], sp["v"],
                  pl.BlockSpec((None,) + xvec.shape[1:], lambda b, h, i: (h, 0, 0, 0)),
                  pl.BlockSpec(lam.shape, lambda b, h, i: (0, 0)),
                  pl.BlockSpec(sub_g.shape, lambda b, h, i: (0, 0))],
        out_specs=(sp["o"], pl.BlockSpec((zr, d), lambda b, h, i: ((b * n_h + h) * nq + i, 0))),
        scratch_shapes=[pltpu.VMEM((n_off, tq, tq), F32), pltpu.VMEM((2, 2, 2, tq, tq), F32),
                        pltpu.VMEM((n_off, 2, 2, tq, tq), F32),
                        pltpu.VMEM((2, 2, 1, tq), F32), pltpu.VMEM((2, 2, rows, tq), F32)],
        compiler_params=_cparams(("parallel", "parallel", "arbitrary")),
        name="diff_attention",
    )(qt, qt, k, vt, xvec, lam, sub_g)


def _attention_b(qt, k, vt, xvec, batch, seq):
    d, t = qt.shape
    tq = BAND_TILE
    g = BAND_QBLOCKS
    n_off = xvec.shape[1] // 2
    assert g >= n_off - 1 and seq % (g * tq) == 0
    nq = seq // (g * tq)
    sp = _attn_specs(batch, seq, nq, g * tq)
    return pl.pallas_call(
        _attn_b_kernel,
        out_shape=jax.ShapeDtypeStruct((t, d), BF16),
        grid=(batch, d // V7X_LANES, nq),
        in_specs=[sp["q"], sp["k"], sp["v"],
                  pl.BlockSpec((None,) + xvec.shape[1:], lambda b, h, i: (h, 0, 0, 0))],
        out_specs=sp["o"],
        scratch_shapes=[pltpu.VMEM((2 * n_off, tq, tq), F32), pltpu.VMEM((n_off * g, 2, tq, tq), F32)],
        compiler_params=_cparams(("parallel", "parallel", "arbitrary")),
        name="chunk_attention",
    )(qt, k, vt, xvec)


def _t5_bucket(rel):
    nb = T5_BUCKETS // 2
    ret = jnp.where(rel > 0, nb, 0)
    n = jnp.abs(rel)
    max_exact = nb // 2
    nf = jnp.maximum(n, 1).astype(F32)
    large = max_exact + (jnp.log(nf / max_exact) / math.log(T5_MAX_DIST / max_exact)
                         * (nb - max_exact)).astype(jnp.int32)
    large = jnp.minimum(large, nb - 1)
    return ret + jnp.where(n < max_exact, n, large)


def _t5_const_distance():
    nb = T5_BUCKETS // 2
    max_exact = nb // 2
    n = np.arange(max_exact, 4 * T5_MAX_DIST, dtype=np.float64)
    large = max_exact + np.floor(np.log(n / max_exact) / math.log(T5_MAX_DIST / max_exact) * (nb - max_exact))
    below = np.nonzero(large < nb - 1)[0]
    return int(n[below[-1]]) + 2


def _tile_rel(tile, n_off):
    i = jnp.arange(2 * tile, dtype=jnp.int32)
    rel = jnp.where(i < tile, -i, 2 * tile - i)
    return rel[None, :] - tile * jnp.arange(n_off, dtype=jnp.int32)[:, None]


def _t5_vectors(t5_bias, tile):
    n_off = 1
    while (n_off - 1) * tile + 1 < _t5_const_distance():
        n_off += 1
    tb = t5_bias.astype(F32)
    vals = tb[_t5_bucket(_tile_rel(tile, n_off))] - tb[T5_BUCKETS // 2 - 1]
    return vals.transpose(2, 0, 1)[:, :, None, :]


def _band_vectors(rel_bias, tile):
    n_off = LEFT_CHUNKS * CHUNK // tile + 1
    idx = jnp.clip(_tile_rel(tile, n_off), -MAX_REL, MAX_REL) + MAX_REL
    vals = rel_bias.astype(F32)[:, idx]
    return vals.reshape(rel_bias.shape[0] // 2, 2 * n_off, 1, 2 * tile)


def _top2_sum4(r0, r1, r2, r3):
    a, b = jnp.maximum(r0, r1), jnp.minimum(r0, r1)
    c, d = jnp.maximum(r2, r3), jnp.minimum(r2, r3)
    return jnp.maximum(a, c) + jnp.maximum(jnp.minimum(a, c), jnp.maximum(b, d))


def _route(logits, rbias, before_ref):
    n_e, n = logits.shape
    scores = 1.0 / (1.0 + jnp.exp(-logits))
    sel = scores + rbias
    row = lax.broadcasted_iota(jnp.int32, sel.shape, 0)
    best = None
    for g in range(N_GROUPS):
        rows = [sel[g * E_PER_GROUP + i: g * E_PER_GROUP + i + 1, :] for i in range(E_PER_GROUP)]
        gs = _top2_sum4(*rows)
        if best is None:
            best, gidx = gs, jnp.zeros(gs.shape, jnp.int32)
        else:
            gidx = jnp.where(gs > best, g, gidx)
            best = jnp.maximum(best, gs)
    masked = jnp.where(row // E_PER_GROUP == gidx, sel, NEG_INF)
    m1 = jnp.max(masked, axis=0, keepdims=True)
    i1 = jnp.min(jnp.where(masked == m1, row, n_e), axis=0, keepdims=True)
    masked2 = jnp.where(row == i1, -3.0e38, masked)
    m2 = jnp.max(masked2, axis=0, keepdims=True)
    i2 = jnp.min(jnp.where(masked2 == m2, row, n_e), axis=0, keepdims=True)
    w1 = jnp.sum(jnp.where(row == i1, scores, 0.0), axis=0, keepdims=True)
    w2 = jnp.sum(jnp.where(row == i2, scores, 0.0), axis=0, keepdims=True)
    den = w1 + w2
    member = jnp.where((row == i1) | (row == i2), 1.0, 0.0).astype(BF16)
    before = before_ref[...]
    pos = jnp.concatenate(
        [jnp.dot(member[:, i:i + MOE_TILE], before, preferred_element_type=F32)
         for i in range(0, n, MOE_TILE)], axis=1)
    p1 = jnp.sum(jnp.where(row == i1, pos, 0.0), axis=0, keepdims=True)
    p2 = jnp.sum(jnp.where(row == i2, pos, 0.0), axis=0, keepdims=True)
    out_row = lax.broadcasted_iota(jnp.int32, (8, n), 0)
    out = jnp.zeros((8, n), F32)
    for r, val in enumerate((i1.astype(F32), i2.astype(F32), p1, p2, w1 / den, w2 / den)):
        out = jnp.where(out_row == r, val, out)
    return out


def _oproj_kernel(o_ref, wo_ref, x_ref, g1_ref, gn_ref, sc_ref, sh_ref, rwt_ref, rb_ref,
                  xo_ref, h_ref, route_ref, before_ref):
    @pl.when(pl.program_id(0) == 0)
    def _():
        t_from = lax.broadcasted_iota(jnp.int32, before_ref.shape, 0)
        t_to = lax.broadcasted_iota(jnp.int32, before_ref.shape, 1)
        before_ref[...] = jnp.where(t_from < t_to, 1.0, 0.0).astype(BF16)

    n_e = rb_ref.shape[0]
    rw = rwt_ref[...]
    rows = o_ref.shape[0] // OPROJ_SPLIT
    groups = [slice(r * rows, (r + 1) * rows) for r in range(OPROJ_SPLIT)]
    ys = [jnp.dot(o_ref[g, :], wo_ref[...], preferred_element_type=F32) for g in groups]
    logits = []
    for g, y in zip(groups, ys):
        xn = x_ref[g, :] + g1_ref[...] * y
        xo_ref[g, :] = xn
        h = _norm_mod(xn, gn_ref[...], sc_ref[...], sh_ref[...])
        h_hi = h.astype(BF16)
        h_ref[g, :] = h_hi
        h_lo = (h - h_hi.astype(F32)).astype(BF16)
        main = lax.dot_general(rw, h_hi, _NT, preferred_element_type=F32)
        corr = lax.dot_general(rw[:n_e], h_lo, _NT, preferred_element_type=F32)
        logits.append(main[:n_e] + main[n_e:] + corr)
    route_ref[...] = _route(jnp.concatenate(logits, axis=1), rb_ref[...], before_ref)


def _out_proj(o, wo_bf16, x2, g1, gn, sc, sh, rwt, rb, seq):
    t, d = x2.shape
    tm = OPROJ_ROW_TILE
    per_b = seq // tm
    n_e = rb.shape[0]
    vec = lambda i: (i // per_b, 0, 0)
    return pl.pallas_call(
        _oproj_kernel,
        out_shape=(jax.ShapeDtypeStruct((t, d), F32),
                   jax.ShapeDtypeStruct((t, d), BF16),
                   jax.ShapeDtypeStruct((8, t), F32)),
        grid=(t // tm,),
        in_specs=[
            pl.BlockSpec((tm, d), lambda i: (i, 0)),
            pl.BlockSpec((d, d), lambda i: (0, 0)),
            pl.BlockSpec((tm, d), lambda i: (i, 0)),
            pl.BlockSpec((None, 1, d), vec),
            pl.BlockSpec((1, d), lambda i: (0, 0)),
            pl.BlockSpec((None, 1, d), vec),
            pl.BlockSpec((None, 1, d), vec),
            pl.BlockSpec((2 * n_e, d), lambda i: (0, 0)),
            pl.BlockSpec((n_e, 1), lambda i: (0, 0)),
        ],
        out_specs=(pl.BlockSpec((tm, d), lambda i: (i, 0)),
                   pl.BlockSpec((tm, d), lambda i: (i, 0)),
                   pl.BlockSpec((8, tm), lambda i: (0, i))),
        scratch_shapes=[pltpu.VMEM((MOE_TILE, MOE_TILE), BF16)],
        compiler_params=_cparams(("arbitrary",)),
        name="out_proj_route",
    )(o, wo_bf16, x2, g1, gn, sc, sh, rwt, rb)


def _moe_plan(route, t, tm, n_e):
    n_tiles = t // tm
    ch, ft = MOE_CHUNK, MOE_FFN_TILE
    loc_rows, main_rows, n_sorted = _moe_rows(t, tm, n_e)
    ids = jnp.arange(n_e, dtype=jnp.int32)
    e = route[0:2].astype(jnp.int32)
    pos = route[2:4].astype(jnp.int32)
    oh = (e[:, :, None] == ids).astype(jnp.int32)
    cnt = oh.sum(0).reshape(n_tiles, tm, n_e).sum(1)
    seg = (cnt + ch - 1) // ch * ch
    loc = jnp.cumsum(seg, axis=1) - seg
    tot = seg.sum(0)
    totp = (tot + ft - 1) // ft * ft
    base = jnp.cumsum(totp) - totp
    gdest = base[None, :] + jnp.cumsum(seg, axis=0) - seg
    col = (oh * jnp.repeat(loc, tm, axis=0)[None]).sum(-1) + pos

    row0 = jnp.arange(loc_rows // ch, dtype=jnp.int32) * ch
    ej = (row0[None, :, None] >= (loc + seg)[:, None, :]).sum(-1)
    ohj = (jnp.minimum(ej, n_e - 1)[..., None] == ids).astype(jnp.int32)
    dst = (ohj * (gdest - loc)[:, None, :]).sum(-1) + row0[None, :]
    dump = main_rows + (jnp.arange(n_tiles, dtype=jnp.int32) % 2)[:, None] * loc_rows + row0[None, :]
    dst = jnp.where(ej < n_e, dst, dump) // ch

    r0 = jnp.arange(n_sorted // ft, dtype=jnp.int32) * ft
    ends = base + totp
    te = jnp.minimum((r0[:, None] >= ends[None, :]).sum(-1), n_e - 1)
    tvalid = (r0 < ends[-1]).astype(jnp.int32)
    tfirst = tvalid * (r0 == (((te[:, None] == ids) * base[None, :]).sum(-1))).astype(jnp.int32)
    later = (ids[None, :] > ids[:, None]) & (tot > 0)[None, :]
    nxt = jnp.where(later, ids[None, :], n_e).min(-1)
    nxt = jnp.where(nxt < n_e, nxt + 2, 1)
    tfirst = tfirst * ((te[:, None] == ids) * nxt[None, :]).sum(-1).astype(jnp.int32)
    tblock = jnp.minimum(r0 // ft, ends[-1] // ft - 1).astype(jnp.int32)
    return col, dst.astype(jnp.int32), te.astype(jnp.int32), tblock, tfirst


def _moe_rows(t, tm, n_e):
    ch, ft = MOE_CHUNK, MOE_FFN_TILE
    loc_rows = 2 * tm + n_e * ch
    main = 2 * t + (t // tm) * n_e * (ch - 1) + n_e * (ft - 1)
    main = (main + ft - 1) // ft * ft
    return loc_rows, main, (main + 2 * loc_rows + ft - 1) // ft * ft


def _chunk_copies(dst_ref, tile, local_ref, sorted_ref, sem, to_sorted):
    ch = MOE_CHUNK
    copies = []
    for j in range(local_ref.shape[0] // ch):
        far = sorted_ref.at[pl.ds(pl.multiple_of(dst_ref[tile, j] * ch, ch), ch)]
        near = local_ref.at[pl.ds(j * ch, ch)]
        copies.append(pltpu.make_async_copy(near, far, sem) if to_sorted
                      else pltpu.make_async_copy(far, near, sem))
    return copies


def _dispatch_kernel(dst_ref, col_ref, h_ref, init_ref, xs_ref, comp, sem):
    del init_ref
    tile = pl.program_id(0)
    last = pl.num_programs(0) - 1
    slot = tile % 2

    def wait_slot(s):
        pltpu.make_async_copy(comp.at[s], xs_ref.at[pl.ds(0, comp.shape[1])], sem.at[s]).wait()

    @pl.when(tile >= 2)
    def _():
        wait_slot(slot)

    col = col_ref[...]
    srow = lax.broadcasted_iota(jnp.int32, (comp.shape[1], col.shape[1]), 0)
    sel = jnp.where((srow == col[0:1]) | (srow == col[1:2]), 1.0, 0.0).astype(BF16)
    comp[slot] = jnp.dot(sel, h_ref[...], preferred_element_type=F32).astype(BF16)
    for cp in _chunk_copies(dst_ref, tile, comp.at[slot], xs_ref, sem.at[slot], True):
        cp.start()

    @pl.when(tile == last)
    def _():
        wait_slot(slot)

    @pl.when((tile == last) & (tile >= 1))
    def _():
        wait_slot(1 - slot)


def _dispatch(dst, col, h, n_e, init):
    t, d = h.shape
    tm = MOE_TILE
    loc_rows, _, n_sorted = _moe_rows(t, tm, n_e)
    assert init.shape == (n_sorted, d) and init.dtype == BF16
    return pl.pallas_call(
        _dispatch_kernel,
        out_shape=jax.ShapeDtypeStruct((n_sorted, d), BF16),
        grid_spec=pltpu.PrefetchScalarGridSpec(
            num_scalar_prefetch=1,
            grid=(t // tm,),
            in_specs=[pl.BlockSpec((2, tm), lambda i, dst: (0, i)),
                      pl.BlockSpec((tm, d), lambda i, dst: (i, 0)),
                      pl.BlockSpec(memory_space=pl.ANY)],
            out_specs=pl.BlockSpec(memory_space=pl.ANY),
            scratch_shapes=[pltpu.VMEM((2, loc_rows, d), BF16), pltpu.SemaphoreType.DMA((2,))],
        ),
        input_output_aliases={3: 0},
        compiler_params=_cparams(("arbitrary",)),
        name="moe_dispatch",
    )(dst, col, h, init)


def _expert_kernel(te_ref, tb_ref, tf_ref, xs_ref, wg_hbm, wu_hbm, wd_hbm, ys_ref,
                   wg_f, wu_f, wd_f, wg_b, wu_b, wd_b, sem, *, layer):
    s = pl.program_id(0)
    tf = tf_ref[s]

    def weight_copies(e):
        return [pltpu.make_async_copy(w.at[layer, e], stage, sem.at[i])
                for i, (w, stage) in enumerate(((wg_hbm, wg_f), (wu_hbm, wu_f), (wd_hbm, wd_f)))]

    @pl.when(s == 0)
    def _():
        for cp in weight_copies(te_ref[0]):
            cp.start()

    @pl.when(tf >= 1)
    def _():
        for cp in weight_copies(te_ref[s]):
            cp.wait()
        wg_b[...] = wg_f[...].astype(BF16)
        wu_b[...] = wu_f[...].astype(BF16)
        wd_b[...] = wd_f[...].astype(BF16)

    @pl.when(tf >= 2)
    def _():
        for cp in weight_copies(tf - 2):
            cp.start()

    @pl.when(tb_ref[s] == s)
    def _():
        rows = ys_ref.shape[0] // MOE_FFN_SPLIT
        hs = []
        for r in range(MOE_FFN_SPLIT):
            xs = xs_ref[r * rows:(r + 1) * rows, :]
            hg = jnp.dot(xs, wg_b[...], preferred_element_type=F32)
            hu = jnp.dot(xs, wu_b[...], preferred_element_type=F32)
            hs.append((hg, hu))
        for r, (hg, hu) in enumerate(hs):
            he = hg * (1.0 / (1.0 + jnp.exp(-hg))) * hu
            ys_ref[r * rows:(r + 1) * rows, :] = jnp.dot(
                he.astype(BF16), wd_b[...], preferred_element_type=F32).astype(BF16)


def _experts(te, tblock, tfirst, xs, wg, wu, wd, layer):
    n_sorted, d = xs.shape
    f = wg.shape[3]
    ft = MOE_FFN_TILE
    wspec = pl.BlockSpec(memory_space=pl.ANY)
    return pl.pallas_call(
        functools.partial(_expert_kernel, layer=layer),
        out_shape=jax.ShapeDtypeStruct((n_sorted, d), BF16),
        grid_spec=pltpu.PrefetchScalarGridSpec(
            num_scalar_prefetch=3,
            grid=(n_sorted // ft,),
            in_specs=[pl.BlockSpec((ft, d), lambda s, te, tb, tf: (tb[s], 0)), wspec, wspec, wspec],
            out_specs=pl.BlockSpec((ft, d), lambda s, te, tb, tf: (tb[s], 0)),
            scratch_shapes=[pltpu.VMEM((d, f), F32), pltpu.VMEM((d, f), F32), pltpu.VMEM((f, d), F32),
                            pltpu.VMEM((d, f), BF16), pltpu.VMEM((d, f), BF16), pltpu.VMEM((f, d), BF16),
                            pltpu.SemaphoreType.DMA((3,))],
        ),
        input_output_aliases={3: 0},
        compiler_params=_cparams(("arbitrary",)),
        name="moe_experts",
    )(te, tblock, tfirst, xs, wg, wu, wd)


def _combine_kernel(dst_ref, w_ref, ys_ref, x_ref, g2_ref, o_ref, comp, sem):
    tile = pl.program_id(0)
    slot = tile % 2

    def fetch(t, s):
        for cp in _chunk_copies(dst_ref, t, comp.at[s], ys_ref, sem.at[s], False):
            cp.start()

    def wait_slot(s):
        pltpu.make_async_copy(ys_ref.at[pl.ds(0, comp.shape[1])], comp.at[s], sem.at[s]).wait()

    last = pl.num_programs(0) - 1

    @pl.when(tile == 0)
    def _():
        fetch(tile, slot)

    fetch(jnp.minimum(tile + 1, last), 1 - slot)

    w = w_ref[...].T
    scol = lax.broadcasted_iota(jnp.int32, (w.shape[0], comp.shape[1]), 1).astype(F32)
    selw = (jnp.where(scol == w[:, 6:7], w[:, 4:5], 0.0)
            + jnp.where(scol == w[:, 7:8], w[:, 5:6], 0.0)).astype(BF16)
    wait_slot(slot)
    acc = jnp.dot(selw, comp[slot], preferred_element_type=F32)
    o_ref[...] = x_ref[...] + g2_ref[...] * acc

    @pl.when(tile == last)
    def _():
        wait_slot(1 - slot)


def _combine(dst, route_col, ys, x2, g2, seq, n_e):
    t, d = x2.shape
    tm = MOE_TILE
    per_b = seq // tm
    loc_rows = _moe_rows(t, tm, n_e)[0]
    return pl.pallas_call(
        _combine_kernel,
        out_shape=jax.ShapeDtypeStruct((t, d), F32),
        grid_spec=pltpu.PrefetchScalarGridSpec(
            num_scalar_prefetch=1,
            grid=(t // tm,),
            in_specs=[pl.BlockSpec((8, tm), lambda i, dst: (0, i)),
                      pl.BlockSpec(memory_space=pl.ANY),
                      pl.BlockSpec((tm, d), lambda i, dst: (i, 0)),
                      pl.BlockSpec((None, 1, d), lambda i, dst: (i // per_b, 0, 0))],
            out_specs=pl.BlockSpec((tm, d), lambda i, dst: (i, 0)),
            scratch_shapes=[pltpu.VMEM((2, loc_rows, d), BF16), pltpu.SemaphoreType.DMA((2,))],
        ),
        compiler_params=_cparams(("arbitrary",)),
        name="moe_combine",
    )(dst, route_col, ys, x2, g2)


def _moe(h, route, wg, wu, wd, layer, x2, g2, seq, sorted_init):
    t = x2.shape[0]
    n_e = wg.shape[1]
    col, dst, te, tblock, tfirst = _moe_plan(route, t, MOE_TILE, n_e)
    xs = _dispatch(dst, col, h, n_e, sorted_init)
    ys = _experts(te, tblock, tfirst, xs, wg, wu, wd, layer)
    route_col = jnp.concatenate([route[:6], col.astype(F32)], axis=0)
    return _combine(dst, route_col, ys, x2, g2, seq, n_e), ys


def kernel(x, c, ada_w, ada_b, norm_mix_g, norm_ffn_g, t5_bias, a_w_qkv, a_q_gain, a_k_gain, a_lambda, a_subln_g, a_w_o, b_w_qkv, b_q_gain, b_k_gain, b_rel_bias, b_w_o, router_w, router_bias, moe_w_gate, moe_w_up, moe_w_down):
    batch, seq, d = x.shape
    depth = ada_w.shape[0]
    assert seq % ATTN_TILE == 0 and seq % ROW_TILE == 0 and d == A_HEADS * 2 * A_HEAD_DIM
    assert d == B_HEADS * B_HEAD_DIM and A_HEAD_DIM == B_HEAD_DIM

    c_pad = jnp.zeros((8, d), F32).at[:batch].set(c.astype(F32))
    mod = _modulation(c_pad, ada_w.astype(F32), ada_b.astype(F32))[:, :batch]
    mod = mod.reshape(depth, batch, 6, 1, d)

    rwt = router_w.astype(F32).T
    rwt_hi = lax.bitcast_convert_type(
        lax.bitcast_convert_type(rwt, jnp.uint32) & jnp.uint32(0xFFFF0000), F32)
    rwt = jnp.concatenate([rwt_hi, rwt - rwt_hi], axis=0).astype(BF16)
    rb = router_bias.astype(F32).reshape(-1, 1)

    x2 = x.astype(F32).reshape(batch * seq, d)
    sorted_buf = None
    for i in range(depth):
        sh1, sc1, g1, sh2, sc2, g2 = [mod[i, :, k] for k in range(6)]
        j = i // 2
        if i % 2 == 0:
            w_qkv, qg, kg, w_o = a_w_qkv[j], a_q_gain[j], a_k_gain[j], a_w_o[j]
        else:
            w_qkv, qg, kg, w_o = b_w_qkv[j], b_q_gain[j], b_k_gain[j], b_w_o[j]
        n_rep = d // qg.shape[0]
        gq = (jnp.tile(qg.astype(F32), n_rep) * (A_HEAD_DIM ** -0.5 * LOG2E)).reshape(d, 1)
        gk = jnp.tile(kg.astype(F32), n_rep).reshape(1, d)
        qt, k, vt = _qkv_proj(x2, norm_mix_g[i].reshape(1, d), sc1, sh1, w_qkv.astype(F32),
                              gq, gk, seq, A_HEAD_DIM)
        if i % 2 == 0:
            lambda_init = 0.8 - 0.6 * math.exp(-0.3 * i)
            n_sorted = _moe_rows(batch * seq, MOE_TILE, moe_w_gate.shape[1])[2]
            o, zeros_buf = _attention_a(qt, k, vt, _t5_vectors(t5_bias, ATTN_TILE), a_lambda[j].astype(F32),
                                        a_subln_g[j].reshape(1, -1), batch, seq, lambda_init, n_sorted)
            if sorted_buf is None:
                sorted_buf = zeros_buf
        else:
            o = _attention_b(qt, k, vt, _band_vectors(b_rel_bias[j], BAND_TILE), batch, seq)
        x2, h, route = _out_proj(o, w_o.astype(BF16), x2, g1, norm_ffn_g[i].reshape(1, d),
                                 sc2, sh2, rwt, rb, seq)
        x2, sorted_buf = _moe(h, route, moe_w_gate, moe_w_up, moe_w_down, i, x2, g2, seq, sorted_buf)
    return x2.reshape(batch, seq, d)
```
